```python
import jax, jax.numpy as jnp
from jax import lax
import numpy as np

D_MODEL = 1024
BATCH = 32
SEQ = 256
DEPTH = 1
DEC_BATCH = 4
DEC_SEQ = 2048
PAST_LEN = 512

GRID_W = 64
HGRN_HEADS = 4
HEAD_K = 128
HEAD_V = 128
HGRN_WIDTH = HGRN_HEADS * HEAD_V
POOL_WINDOWS = (2, 4, 8, 16)
POOL_GROUP = 128
POOL_WIDTH = len(POOL_WINDOWS) * POOL_GROUP
CHUNK = 32
N_EXPERTS = 32
TOP_K = 4
D_FF = D_MODEL
SWIGLU_LIMIT = 7.0
SWIGLU_ALPHA = 1.702
EPS = 1e-6
IN_WIDTH = 5 * HGRN_WIDTH + POOL_WIDTH + 2 * D_MODEL
IN_SPLITS = (HGRN_WIDTH, 2 * HGRN_WIDTH, 3 * HGRN_WIDTH, 4 * HGRN_WIDTH, 5 * HGRN_WIDTH,
             5 * HGRN_WIDTH + POOL_WIDTH, 5 * HGRN_WIDTH + POOL_WIDTH + D_MODEL)

kernel_name = 'hgrn2_pool_moe_flow_step'


def rms_norm(x, g):
    xf = x.astype(jnp.float32)
    y = xf * lax.rsqrt(jnp.mean(xf * xf, axis=-1, keepdims=True) + EPS)
    return (y * g.astype(jnp.float32)).astype(x.dtype)


def gla_chunk_scan(q, k, v, g, s0):
    b_, h_, L, _ = q.shape
    n = L // CHUNK

    def chunks(a):
        return a.reshape(b_, h_, n, CHUNK, a.shape[-1]).transpose(2, 0, 1, 3, 4)

    lower = jnp.tril(jnp.ones((CHUNK, CHUNK), dtype=bool))[:, :, None]

    def step(S, inp):
        qc, kc, vc, gc = inp
        cum = jnp.cumsum(gc, axis=2)
        diff = cum[:, :, :, None, :] - cum[:, :, None, :, :]
        decay = jnp.exp(jnp.where(lower, diff, -jnp.inf))
        scores = jnp.einsum('bhtk,bhtsk,bhsk->bhts', qc, decay, kc)
        o = (jnp.einsum('bhts,bhsv->bhtv', scores, vc)
             + jnp.einsum('bhtk,bhkv->bhtv', qc * jnp.exp(cum), S))
        last = cum[:, :, -1:, :]
        S = (jnp.exp(last[:, :, 0, :])[..., None] * S
             + jnp.einsum('bhsk,bhsv->bhkv', kc * jnp.exp(last - cum), vc))
        return S, o

    s_fin, o = lax.scan(step, s0, (chunks(q), chunks(k), chunks(v), chunks(g)))
    return o.transpose(1, 2, 0, 3, 4).reshape(b_, h_, L, -1), s_fin


def hgrn2_mixer(q, f_fwd, f_bwd, iv, og, lb, norm_g, s0):
    B, L, _ = q.shape

    def heads(a):
        return a.reshape(B, L, HGRN_HEADS, -1).transpose(0, 2, 1, 3).astype(jnp.float32)

    qh = heads(q) * (HEAD_K ** -0.5)
    vh = heads(iv)
    s0f = s0.astype(jnp.float32)

    def direction(fz, lbd, s_init, rev):
        f = lbd + (1.0 - lbd) * jax.nn.sigmoid(fz.astype(jnp.float32))
        fh = heads(f)
        qd, kd, vd, gd = qh, 1.0 - fh, vh, jnp.log(fh)
        if rev:
            qd, kd, vd, gd = (jnp.flip(a, axis=2) for a in (qd, kd, vd, gd))
        o, s_fin = gla_chunk_scan(qd, kd, vd, gd, s_init)
        if rev:
            o = jnp.flip(o, axis=2)
        return o, s_fin

    o_f, s_f = direction(f_fwd, lb[0], s0f[:, 0], False)
    o_b, s_b = direction(f_bwd, lb[1], s0f[:, 1], True)
    o = (o_f + o_b).transpose(0, 2, 1, 3)
    o = rms_norm(o, norm_g).reshape(B, L, HGRN_WIDTH) * jax.nn.silu(og.astype(jnp.float32))
    return o.astype(q.dtype), jnp.stack([s_f, s_b], axis=1).astype(q.dtype)


def window_mean(u, w, axis):
    L = u.shape[axis]
    pos = jnp.arange(L)
    lo = jnp.clip(pos - w // 2, 0, L - 1)
    hi = jnp.clip(pos - w // 2 + w - 1, 0, L - 1)
    cs = jnp.cumsum(u.astype(jnp.float32), axis=axis)
    cs = jnp.concatenate([jnp.zeros_like(lax.slice_in_dim(cs, 0, 1, axis=axis)), cs], axis=axis)
    s = jnp.take(cs, hi + 1, axis=axis) - jnp.take(cs, lo, axis=axis)
    shape = [1] * u.ndim
    shape[axis] = L
    cnt = (hi - lo + 1).astype(jnp.float32).reshape(shape)
    return (s / cnt).astype(u.dtype)


def pool_mixer(u, w_pool, pool_scale, grid):
    B, L, _ = u.shape
    outs = []
    for g, w in enumerate(POOL_WINDOWS):
        ug = u[..., g * POOL_GROUP:(g + 1) * POOL_GROUP]
        if grid:
            rows = L // GRID_W
            ug2 = ug.reshape(B, rows, GRID_W, POOL_GROUP)
            pm = window_mean(window_mean(ug2, w, 1), w, 2).reshape(B, L, POOL_GROUP)
        else:
            pm = window_mean(ug, w, 1)
        outs.append(jnp.einsum('blc,cd->bld', pm - ug, w_pool[g]))
    return jnp.concatenate(outs, axis=-1) * pool_scale


def moe_ffn(h, w_router, b_router, w_gate_up, b_gate_up, w_down, b_down):
    B, L, D = h.shape
    T = B * L
    hf = h.reshape(T, D)
    logits = (hf @ w_router + b_router).astype(jnp.float32)
    top_v, top_i = lax.top_k(logits, TOP_K)
    wts = jax.nn.softmax(top_v, axis=-1)
    flat_e = top_i.reshape(-1)
    order = jnp.argsort(flat_e)
    tok = order // TOP_K
    e_sorted = flat_e[order]
    sizes = jnp.bincount(flat_e, length=N_EXPERTS).astype(jnp.int32)
    xs = hf[tok]
    gu = lax.ragged_dot(xs, w_gate_up, sizes) + b_gate_up[e_sorted]
    gate = jnp.minimum(gu[:, :D_FF].astype(jnp.float32), SWIGLU_LIMIT)
    up = jnp.clip(gu[:, D_FF:].astype(jnp.float32), -SWIGLU_LIMIT, SWIGLU_LIMIT)
    act = ((up + 1.0) * gate * jax.nn.sigmoid(SWIGLU_ALPHA * gate)).astype(h.dtype)
    out = lax.ragged_dot(act, w_down, sizes) + b_down[e_sorted]
    out = out.astype(jnp.float32) * wts.reshape(-1)[order][:, None]
    y = jax.ops.segment_sum(out, tok, num_segments=T)
    return y.reshape(B, L, D).astype(h.dtype)


def trunk_layer(x, mod, s0, grid, w_in, lb, hgrn_norm, w_pool, pool_scale, w_branch_a, w_branch_b,
                w_out, norm1, norm2, w_router, b_router, w_gate_up, b_gate_up, w_down, b_down):
    shift1, scale1, gate1, shift2, scale2, gate2 = jnp.split(mod, 6, axis=-1)
    h = rms_norm(x, norm1) * (1 + scale1) + shift1
    q, f_fwd, f_bwd, iv, og, pu, ga, gb = jnp.split(h @ w_in, IN_SPLITS, axis=-1)
    y_a, s_fin = hgrn2_mixer(q, f_fwd, f_bwd, iv, og, lb, hgrn_norm, s0)
    y_b = pool_mixer(pu, w_pool, pool_scale, grid)
    merged = jax.nn.sigmoid(ga) * (y_a @ w_branch_a) + jax.nn.sigmoid(gb) * (y_b @ w_branch_b)
    x = x + gate1 * (merged @ w_out)
    h2 = rms_norm(x, norm2) * (1 + scale2) + shift2
    x = x + gate2 * moe_ffn(h2, w_router, b_router, w_gate_up, b_gate_up, w_down, b_down)
    return x, s_fin


def setup_inputs(seed: int = 0) -> dict:
    key = jax.random.key(seed)
    ks = jax.random.split(key, 24)
    D = D_MODEL

    def nrm(k, shape, s):
        return jax.random.normal(k, shape, jnp.float32) * s

    return {
        'x_prompt': nrm(ks[0], (BATCH, SEQ, D), 1.0),
        'x_sample': nrm(ks[1], (DEC_BATCH, DEC_SEQ, D), 1.0),
        'state_hgrn': nrm(ks[2], (DEC_BATCH, DEPTH, 2, HGRN_HEADS, HEAD_K, HEAD_V), 0.5),
        'c': nrm(ks[3], (DEC_BATCH, D), 1.0),
        'c_ctx': nrm(ks[4], (D,), 1.0),
        'w_ada': nrm(ks[5], (DEPTH, D, 6 * D), 0.5 * D ** -0.5),
        'b_ada': nrm(ks[6], (DEPTH, 6 * D), 0.02),
        'norm1': 1.0 + nrm(ks[7], (DEPTH, D), 0.05),
        'w_in': nrm(ks[8], (DEPTH, D, IN_WIDTH), D ** -0.5),
        'hgrn_lb': nrm(ks[9], (DEPTH + 1, 2, HGRN_WIDTH), 0.5),
        'hgrn_norm': 1.0 + nrm(ks[10], (DEPTH, HEAD_V), 0.05),
        'w_pool': nrm(ks[11], (DEPTH, len(POOL_WINDOWS), POOL_GROUP, POOL_GROUP), POOL_GROUP ** -0.5),
        'pool_scale': 1.0 + nrm(ks[12], (DEPTH, POOL_WIDTH), 0.1),
        'w_branch_a': nrm(ks[13], (DEPTH, HGRN_WIDTH, D), HGRN_WIDTH ** -0.5),
        'w_branch_b': nrm(ks[14], (DEPTH, POOL_WIDTH, D), POOL_WIDTH ** -0.5),
        'w_out': nrm(ks[15], (DEPTH, D, D), D ** -0.5),
        'norm2': 1.0 + nrm(ks[16], (DEPTH, D), 0.05),
        'w_router': nrm(ks[17], (DEPTH, D, N_EXPERTS), D ** -0.5),
        'b_router': nrm(ks[18], (DEPTH, N_EXPERTS), 0.01),
        'w_gate_up': nrm(ks[19], (DEPTH, N_EXPERTS, D, 2 * D_FF), D ** -0.5),
        'b_gate_up': nrm(ks[20], (DEPTH, N_EXPERTS, 2 * D_FF), 0.02),
        'w_down': nrm(ks[21], (DEPTH, N_EXPERTS, D_FF, D), D_FF ** -0.5),
        'b_down': nrm(ks[22], (DEPTH, N_EXPERTS, D), 0.02),
        'final_norm': 1.0 + nrm(ks[23], (D,), 0.05),
    }


def reference(x_prompt, x_sample, state_hgrn, c, c_ctx, w_ada, b_ada, norm1, w_in, hgrn_lb, hgrn_norm,
              w_pool, pool_scale, w_branch_a, w_branch_b, w_out, norm2, w_router, b_router,
              w_gate_up, b_gate_up, w_down, b_down, final_norm):
    lb_all = jnp.cumsum(jax.nn.softmax(hgrn_lb.astype(jnp.float32), axis=0), axis=0)
    xp, xs = x_prompt, x_sample
    zero_state = jnp.zeros((x_prompt.shape[0], 2, HGRN_HEADS, HEAD_K, HEAD_V), jnp.float32)
    ctx_states = []
    for l in range(DEPTH):
        weights = (w_in[l], lb_all[l], hgrn_norm[l], w_pool[l], pool_scale[l], w_branch_a[l],
                   w_branch_b[l], w_out[l], norm1[l], norm2[l], w_router[l], b_router[l],
                   w_gate_up[l], b_gate_up[l], w_down[l], b_down[l])
        mod_ctx = (jax.nn.silu(c_ctx) @ w_ada[l] + b_ada[l])[None, None, :]
        mod_lat = (jax.nn.silu(c) @ w_ada[l] + b_ada[l])[:, None, :]
        xp, s_ctx = trunk_layer(xp, mod_ctx, zero_state, False, *weights)
        xs, _ = trunk_layer(xs, mod_lat, state_hgrn[:, l], True, *weights)
        ctx_states.append(s_ctx)
    y_prompt = rms_norm(xp, final_norm)
    y_sample = rms_norm(xs, final_norm)
    new_state_hgrn = jnp.stack(ctx_states, axis=1)
    return (y_prompt, y_sample, new_state_hgrn)
```

```python
import functools

import numpy as np
import jax
import jax.numpy as jnp
from jax import lax
from jax.experimental import pallas as pl
from jax.experimental.pallas import tpu as pltpu

F32 = jnp.float32
BF16 = jnp.bfloat16
I32 = jnp.int32

D_MODEL = 1024
N_CTX_SEQ, CTX_LEN = 32, 256
N_LAT_SEQ, LAT_LEN = 4, 2048
T_CTX = N_CTX_SEQ * CTX_LEN
T_LAT = N_LAT_SEQ * LAT_LEN
T_ALL = T_CTX + T_LAT
TB = 256
NB = T_ALL // TB
NB_CTX = T_CTX // TB
LAT_BLOCKS = LAT_LEN // TB
HEADS, HEAD_K, HEAD_V = 4, 128, 128
HGRN_W = HEADS * HEAD_V
POOL_WINDOWS = (2, 4, 8, 16)
POOL_G = 128
POOL_W = len(POOL_WINDOWS) * POOL_G
GRID_W = 64
GRID_H = LAT_LEN // GRID_W
IN_W = 5 * HGRN_W + POOL_W + 2 * D_MODEL
N_EXP, TOP_K, D_FF = 32, 4, 1024
SWIGLU_LIMIT = 7.0
SWIGLU_ALPHA = 1.702
EPS = 1e-6
CHUNK = 64
N_LEVELS = 6
EXP_ROWS = (N_LEVELS + 2) * CHUNK
TM = 256
N_TILES = T_ALL * TOP_K // TM + N_EXP
MOD_ROWS = 8
VMEM_LIMIT = 56 * 1024 * 1024


def _params(sem=("arbitrary",)):
    return pltpu.CompilerParams(dimension_semantics=sem, vmem_limit_bytes=VMEM_LIMIT)


def _dot(a, b):
    return jnp.dot(a, b, preferred_element_type=F32)


def _dot_nt(a, b):
    return lax.dot_general(a, b, (((1,), (1,)), ((), ())), preferred_element_type=F32)


def _dot_tn(a, b):
    return lax.dot_general(a, b, (((0,), (0,)), ((), ())), preferred_element_type=F32)


def _split2(x):
    hi = x.astype(BF16)
    lo = (x - hi.astype(F32)).astype(BF16)
    return hi, lo


def _mod_row(i):
    return jnp.where(i < NB_CTX, N_LAT_SEQ, (i - NB_CTX) // LAT_BLOCKS)


def _ada_kernel(c_ref, w_ref, b_ref, o_ref):
    c = c_ref[...]
    s = c * jax.nn.sigmoid(c)
    o_ref[...] = jnp.dot(s, w_ref[...], preferred_element_type=F32,
                         precision=lax.Precision.HIGHEST) + b_ref[...]


def _ada(cc, w_ada, b_ada):
    nblk = 1536
    return pl.pallas_call(
        _ada_kernel,
        out_shape=jax.ShapeDtypeStruct((MOD_ROWS, 6 * D_MODEL), F32),
        grid=(6 * D_MODEL // nblk,),
        in_specs=[pl.BlockSpec((MOD_ROWS, D_MODEL), lambda j: (0, 0)),
                  pl.BlockSpec((D_MODEL, nblk), lambda j: (0, j)),
                  pl.BlockSpec((1, nblk), lambda j: (0, j))],
        out_specs=pl.BlockSpec((MOD_ROWS, nblk), lambda j: (0, j)),
        compiler_params=_params(),
        name="ada",
    )(cc, w_ada, b_ada)


def _rms(x, g):
    ms = jnp.mean(x * x, axis=-1, keepdims=True)
    return x * lax.rsqrt(ms + EPS) * g


def _inproj_kernel(x_ref, mod_ref, n1_ref, w_ref, o_ref):
    row = _mod_row(pl.program_id(0))
    shift = mod_ref[pl.ds(row, 1), pl.ds(0, D_MODEL)]
    scale = mod_ref[pl.ds(row, 1), pl.ds(D_MODEL, D_MODEL)]
    h = _rms(x_ref[...], n1_ref[...]) * (1.0 + scale) + shift
    o_ref[...] = _dot(h.astype(BF16), w_ref[...])


def _inproj(x_all, mod, norm1, w_in_bf):
    return pl.pallas_call(
        _inproj_kernel,
        out_shape=jax.ShapeDtypeStruct((T_ALL, IN_W), F32),
        grid=(NB,),
        in_specs=[pl.BlockSpec((TB, D_MODEL), lambda i: (i, 0)),
                  pl.BlockSpec((MOD_ROWS, 6 * D_MODEL), lambda i: (0, 0)),
                  pl.BlockSpec((1, D_MODEL), lambda i: (0, 0)),
                  pl.BlockSpec((D_MODEL, IN_W), lambda i: (0, 0))],
        out_specs=pl.BlockSpec((TB, IN_W), lambda i: (i, 0)),
        compiler_params=_params(),
        name="inproj",
    )(x_all, mod, norm1, w_in_bf)


def _hgrn_consts():
    c = CHUNK
    t = np.arange(c)[:, None]
    u = np.arange(c)[None, :]
    blocks = [u <= t]
    masks = [np.eye(c, dtype=bool)]
    h = c // 2
    while h >= 1:
        bi = t // h
        upper = (bi % 2) == 1
        e_up = (u >= bi * h) & (u <= t)
        e_lo = (u > t) & (u <= bi * h + h - 1)
        blocks.append(np.where(upper, e_up, e_lo))
        masks.append(((t // (2 * h)) == (u // (2 * h))) & (((t // h) % 2) == 1) & (((u // h) % 2) == 0))
        h //= 2
    blocks.append(u > t)
    m_f = np.stack(blocks).astype(np.float32)
    k_f = np.stack(masks).astype(np.float32)
    m_b = m_f[:, ::-1, ::-1]
    k_b = k_f[:, ::-1, ::-1]
    m = np.stack([m_f.reshape(EXP_ROWS, c), m_b.reshape(EXP_ROWS, c)])
    m3 = np.concatenate([m, m, m], axis=2)
    return jnp.asarray(m3, BF16), jnp.asarray(np.stack([k_f, k_b]), F32)


def _hgrn_dir(d, q_ref, f_ref, v_ref, o_ref, r0, lb, mall_ref, mask_ref, st_ref):
    c = CHUNK
    rows = pl.ds(r0, c)
    q = q_ref[rows, :] * (HEAD_K ** -0.5)
    v = v_ref[rows, :].astype(BF16)
    f = lb + (1.0 - lb) * jax.nn.sigmoid(f_ref[rows, :])
    g = jnp.log(f)
    k = 1.0 - f
    g1 = g.astype(BF16)
    r1 = g - g1.astype(F32)
    g2 = r1.astype(BF16)
    g3 = (r1 - g2.astype(F32)).astype(BF16)
    gsplit = jnp.concatenate([g1, g2, g3], axis=0)
    z = jnp.exp(_dot(mall_ref[d], gsplit))
    tot_row = c - 1 if d == 0 else 0
    for h in range(HEADS):
        sl = slice(h * HEAD_K, (h + 1) * HEAD_K)
        qh, kh, vh, zh = q[:, sl], k[:, sl], v[:, sl], z[:, sl]
        st = st_ref[d, h]
        o = _dot_nt((qh * zh[0:c]).astype(BF16), st.astype(BF16))
        scores = mask_ref[d, 0] * _dot_nt(qh.astype(BF16), kh.astype(BF16))
        for lev in range(N_LEVELS):
            zl = zh[(lev + 1) * c:(lev + 2) * c]
            scores = scores + mask_ref[d, lev + 1] * _dot_nt((qh * zl).astype(BF16), (kh * zl).astype(BF16))
        o = o + _dot(scores.astype(BF16), vh)
        kd = (kh * zh[(N_LEVELS + 1) * c:]).astype(BF16)
        st_ref[d, h] = st * zh[tot_row:tot_row + 1] + _dot_tn(vh, kd)
        o_ref[rows, sl] = o


def _hgrn_kernel(qf_ref, ff_ref, vf_ref, qb_ref, fb_ref, vb_ref, lbraw_ref, s0_ref, mall_ref, mask_ref,
                 of_ref, ob_ref, sout_hbm, st_ref, stage_ref, sem):
    i = pl.program_id(0)
    j = (i - NB_CTX) % LAT_BLOCKS
    is_ctx = i < NB_CTX

    @pl.when(is_ctx)
    def _():
        st_ref[...] = jnp.zeros_like(st_ref)

    @pl.when(jnp.logical_and(jnp.logical_not(is_ctx), j == 0))
    def _():
        for d in range(2):
            for h in range(HEADS):
                st_ref[d, h] = s0_ref[0, d, h].T

    a0 = lbraw_ref[0]
    a1 = lbraw_ref[1]
    mx = jnp.maximum(a0, a1)
    e0 = jnp.exp(a0 - mx)
    e1 = jnp.exp(a1 - mx)
    lb = e0 / (e0 + e1)

    nchunk = TB // CHUNK

    def body(ci, carry):
        rf = pl.multiple_of(ci * CHUNK, CHUNK)
        rb = pl.multiple_of((nchunk - 1 - ci) * CHUNK, CHUNK)
        _hgrn_dir(0, qf_ref, ff_ref, vf_ref, of_ref, rf, lb[0:1], mall_ref, mask_ref, st_ref)
        _hgrn_dir(1, qb_ref, fb_ref, vb_ref, ob_ref, rb, lb[1:2], mall_ref, mask_ref, st_ref)
        return carry

    lax.fori_loop(0, nchunk, body, 0)

    @pl.when(is_ctx)
    def _():
        for d in range(2):
            for h in range(HEADS):
                stage_ref[d, h] = st_ref[d, h].T
        cp = pltpu.make_async_copy(stage_ref, sout_hbm.at[i], sem)
        cp.start()
        cp.wait()


def _bwd_block(i):
    j = (i - NB_CTX) % LAT_BLOCKS
    return jnp.where(i < NB_CTX, i, i - j + (LAT_BLOCKS - 1 - j))


def _hgrn(proj, hgrn_lb, s0, mall, masks):
    nh = HGRN_W
    fwd = lambda col: pl.BlockSpec((TB, nh), lambda i: (i, col))
    bwd = lambda col: pl.BlockSpec((TB, nh), lambda i: (_bwd_block(i), col))
    lat_seq = lambda i: jnp.clip((i - NB_CTX) // LAT_BLOCKS, 0, N_LAT_SEQ - 1)
    return pl.pallas_call(
        _hgrn_kernel,
        out_shape=(jax.ShapeDtypeStruct((T_ALL, nh), F32),
                   jax.ShapeDtypeStruct((T_ALL, nh), F32),
                   jax.ShapeDtypeStruct((N_CTX_SEQ, 2, HEADS, HEAD_K, HEAD_V), F32)),
        grid=(NB,),
        in_specs=[fwd(0), fwd(1), fwd(3), bwd(0), bwd(2), bwd(3),
                  pl.BlockSpec((2, 2, nh), lambda i: (0, 0, 0)),
                  pl.BlockSpec((1, 2, HEADS, HEAD_K, HEAD_V), lambda i: (lat_seq(i), 0, 0, 0, 0)),
                  pl.BlockSpec((2, EXP_ROWS, 3 * CHUNK), lambda i: (0, 0, 0)),
                  pl.BlockSpec((2, N_LEVELS + 1, CHUNK, CHUNK), lambda i: (0, 0, 0, 0))],
        out_specs=(pl.BlockSpec((TB, nh), lambda i: (i, 0)),
                   pl.BlockSpec((TB, nh), lambda i: (_bwd_block(i), 0)),
                   pl.BlockSpec(memory_space=pl.ANY)),
        scratch_shapes=[pltpu.VMEM((2, HEADS, HEAD_V, HEAD_K), F32),
                        pltpu.VMEM((2, HEADS, HEAD_K, HEAD_V), F32),
                        pltpu.SemaphoreType.DMA],
        compiler_params=_params(),
        name="hgrn",
    )(proj, proj, proj, proj, proj, proj, hgrn_lb, s0, mall, masks)


def _window_bounds(n, w):
    pos = np.arange(n)
    lo = np.clip(pos - w // 2, 0, n - 1)
    hi = np.clip(pos - w // 2 + w - 1, 0, n - 1)
    return lo, hi


def _pool_consts():
    seq, img, cnt_seq, cnt_col = [], [], [], []
    for w in POOL_WINDOWS:
        lo, hi = _window_bounds(CTX_LEN, w)
        u = np.arange(CTX_LEN)[None, :]
        seq.append((u >= lo[:, None]) & (u <= hi[:, None]))
        cnt_seq.append(hi - lo + 1)
        lo, hi = _window_bounds(GRID_W, w)
        u = np.arange(GRID_W)[None, :]
        band = (u >= lo[:, None]) & (u <= hi[:, None])
        img.append(np.kron(np.eye(TB // GRID_W, dtype=bool), band))
        cnt_col.append(np.tile(hi - lo + 1, TB // GRID_W))
    a = np.stack([np.stack(seq), np.stack(img)]).astype(np.float32)
    cnt = np.stack([np.stack(cnt_seq), np.stack(cnt_col)]).astype(np.float32)
    cnt = np.broadcast_to(cnt[..., None], cnt.shape + (POOL_G,))
    return jnp.asarray(a, BF16), jnp.asarray(cnt, F32)


POOL_ROWS = LAT_LEN


def _pool_kernel(u_ref, a_ref, cnt_ref, wp_ref, ps_ref, o_ref, cp_ref):
    i = pl.program_id(0)
    nblk = POOL_ROWS // TB

    def finish(g, r0, nrows, pm):
        sl = slice(g * POOL_G, (g + 1) * POOL_G)
        d = pm - u_ref[pl.ds(r0, nrows), sl]
        y = _dot(d.astype(BF16), wp_ref[g]) * ps_ref[:, sl]
        o_ref[pl.ds(r0, nrows), sl] = y.astype(o_ref.dtype)

    def window_sum(kind, g, b):
        sl = slice(g * POOL_G, (g + 1) * POOL_G)
        hi, lo = _split2(u_ref[pl.ds(b * TB, TB), sl])
        a = a_ref[kind, g]
        return (_dot(a, hi) + _dot(a, lo)) / cnt_ref[kind, g]

    @pl.when(i < T_CTX // POOL_ROWS)
    def _():
        for g in range(len(POOL_WINDOWS)):
            for b in range(nblk):
                finish(g, b * TB, TB, window_sum(0, g, b))

    @pl.when(i >= T_CTX // POOL_ROWS)
    def _():
        for g, w in enumerate(POOL_WINDOWS):
            for b in range(nblk):
                cp_ref[pl.ds(b * TB, TB), :] = window_sum(1, g, b)
            lo, hi = _window_bounds(GRID_H, w)
            for r in range(GRID_H):
                acc = cp_ref[pl.ds(int(lo[r]) * GRID_W, GRID_W), :]
                for rr in range(int(lo[r]) + 1, int(hi[r]) + 1):
                    acc = acc + cp_ref[pl.ds(rr * GRID_W, GRID_W), :]
                finish(g, r * GRID_W, GRID_W, acc / float(hi[r] - lo[r] + 1))


def _pool(proj, a_pool, cnt_pool, w_pool_bf, pool_scale):
    col = (5 * HGRN_W) // POOL_W
    return pl.pallas_call(
        _pool_kernel,
        out_shape=jax.ShapeDtypeStruct((T_ALL, POOL_W), BF16),
        grid=(T_ALL // POOL_ROWS,),
        in_specs=[pl.BlockSpec((POOL_ROWS, POOL_W), lambda i: (i, col)),
                  pl.BlockSpec((2, 4, TB, TB), lambda i: (0, 0, 0, 0)),
                  pl.BlockSpec((2, 4, TB, POOL_G), lambda i: (0, 0, 0, 0)),
                  pl.BlockSpec((4, POOL_G, POOL_G), lambda i: (0, 0, 0)),
                  pl.BlockSpec((1, POOL_W), lambda i: (0, 0))],
        out_specs=pl.BlockSpec((POOL_ROWS, POOL_W), lambda i: (i, 0)),
        scratch_shapes=[pltpu.VMEM((POOL_ROWS, POOL_G), F32)],
        compiler_params=_params(),
        name="pool",
    )(proj, a_pool, cnt_pool, w_pool_bf, pool_scale)


def _merge_kernel(x_ref, of_ref, ob_ref, og_ref, yb_ref, ga_ref, gb_ref, mod_ref, hn_ref, wa_ref, wb_ref,
                  wo_ref, n2_ref, wrh_ref, wrl_ref, br_ref, tri_ref,
                  x1_ref, h2_ref, te_ref, tw_ref, rk_ref, hist_ref):
    row = _mod_row(pl.program_id(0))
    gate1 = mod_ref[pl.ds(row, 1), pl.ds(2 * D_MODEL, D_MODEL)]
    shift2 = mod_ref[pl.ds(row, 1), pl.ds(3 * D_MODEL, D_MODEL)]
    scale2 = mod_ref[pl.ds(row, 1), pl.ds(4 * D_MODEL, D_MODEL)]

    o = of_ref[...] + ob_ref[...]
    og = og_ref[...]
    ya = jnp.concatenate(
        [_rms(o[:, h * HEAD_V:(h + 1) * HEAD_V], hn_ref[...]) for h in range(HEADS)], axis=1)
    ya = ya * (og * jax.nn.sigmoid(og))
    merged = (jax.nn.sigmoid(ga_ref[...]) * _dot(ya.astype(BF16), wa_ref[...])
              + jax.nn.sigmoid(gb_ref[...]) * _dot(yb_ref[...], wb_ref[...]))
    x1 = x_ref[...] + gate1 * _dot(merged.astype(BF16), wo_ref[...])
    x1_ref[...] = x1
    h2 = _rms(x1, n2_ref[...]) * (1.0 + scale2) + shift2
    for jj in range(D_MODEL // 128):
        h2_ref[:, jj, :] = h2[:, jj * 128:(jj + 1) * 128]

    hh, hl = _split2(h2)
    lt = _dot_nt(wrh_ref[...], hh) + _dot_nt(wrl_ref[...], hh) + _dot_nt(wrh_ref[...], hl) + br_ref[...]
    eidx = lax.broadcasted_iota(I32, (N_EXP, TB), 0)
    vals, idxs, cnt = [], [], jnp.zeros((N_EXP, TB), F32)
    sels = []
    for _ in range(TOP_K):
        m = jnp.max(lt, axis=0, keepdims=True)
        idx = jnp.min(jnp.where(lt == m, eidx, N_EXP), axis=0, keepdims=True)
        sel = eidx == idx
        vals.append(m)
        idxs.append(idx)
        sels.append(sel)
        cnt = cnt + sel.astype(F32)
        lt = jnp.where(sel, -jnp.inf, lt)
    ex = [jnp.exp(v - vals[0]) for v in vals]
    den = ex[0] + ex[1] + ex[2] + ex[3]
    tw_ref[0] = jnp.concatenate([e / den for e in ex], axis=0)
    te_ref[0] = jnp.concatenate(idxs, axis=0)
    before = _dot(cnt.astype(BF16), tri_ref[...])
    rk_ref[0] = jnp.concatenate(
        [jnp.sum(jnp.where(s, before, 0.0), axis=0, keepdims=True) for s in sels], axis=0).astype(I32)
    hist_ref[0] = jnp.sum(cnt, axis=1, keepdims=True).astype(I32)


def _merge(x_all, o_f, o_b, proj, yb, mod, hgrn_norm, wa_bf, wb_bf, wo_bf, norm2, wr_hi, wr_lo, b_router, tri):
    full = lambda shape: pl.BlockSpec(shape, lambda i: (0,) * len(shape))
    return pl.pallas_call(
        _merge_kernel,
        out_shape=(jax.ShapeDtypeStruct((T_ALL, D_MODEL), F32),
                   jax.ShapeDtypeStruct((T_ALL, D_MODEL // 128, 128), F32),
                   jax.ShapeDtypeStruct((NB, TOP_K, TB), I32),
                   jax.ShapeDtypeStruct((NB, TOP_K, TB), F32),
                   jax.ShapeDtypeStruct((NB, TOP_K, TB), I32),
                   jax.ShapeDtypeStruct((NB, N_EXP, 1), I32)),
        grid=(NB,),
        in_specs=[pl.BlockSpec((TB, D_MODEL), lambda i: (i, 0)),
                  pl.BlockSpec((TB, HGRN_W), lambda i: (i, 0)),
                  pl.BlockSpec((TB, HGRN_W), lambda i: (i, 0)),
                  pl.BlockSpec((TB, HGRN_W), lambda i: (i, 4)),
                  pl.BlockSpec((TB, POOL_W), lambda i: (i, 0)),
                  pl.BlockSpec((TB, D_MODEL), lambda i: (i, 3)),
                  pl.BlockSpec((TB, D_MODEL), lambda i: (i, 4)),
                  full((MOD_ROWS, 6 * D_MODEL)),
                  full((1, HEAD_V)),
                  full((HGRN_W, D_MODEL)),
                  full((POOL_W, D_MODEL)),
                  full((D_MODEL, D_MODEL)),
                  full((1, D_MODEL)),
                  full((N_EXP, D_MODEL)),
                  full((N_EXP, D_MODEL)),
                  full((N_EXP, 1)),
                  full((TB, TB))],
        out_specs=(pl.BlockSpec((TB, D_MODEL), lambda i: (i, 0)),
                   pl.BlockSpec((TB, D_MODEL // 128, 128), lambda i: (i, 0, 0)),
                   pl.BlockSpec((1, TOP_K, TB), lambda i: (i, 0, 0)),
                   pl.BlockSpec((1, TOP_K, TB), lambda i: (i, 0, 0)),
                   pl.BlockSpec((1, TOP_K, TB), lambda i: (i, 0, 0)),
                   pl.BlockSpec((1, N_EXP, 1), lambda i: (i, 0, 0))),
        compiler_params=_params(),
        name="merge",
    )(x_all, o_f, o_b, proj, yb, proj, proj, mod, hgrn_norm, wa_bf, wb_bf, wo_bf, norm2,
      wr_hi, wr_lo, b_router, tri)


def _moe_kernel(te_ref, nv_ref, nu_ref, idx_hbm, h2_hbm, wgu_ref, bgu_ref, wd_ref, bd_ref, slots_hbm,
                idx_smem, xbuf, xmat, obuf, wgu_bf, wd_bf, sem_i, sem_g, sem_s):
    i = pl.program_id(0)
    n_used = nu_ref[0]
    slot = i % 2

    def gather_copy(src, s, r):
        return pltpu.make_async_copy(h2_hbm.at[src], xbuf.at[s, r], sem_g.at[s])

    def scatter_copy(s, r, dst):
        return pltpu.make_async_copy(obuf.at[s, r], slots_hbm.at[dst], sem_s.at[s])

    def start_gather(t, s):
        cp = pltpu.make_async_copy(idx_hbm.at[t], idx_smem.at[s], sem_i)
        cp.start()
        cp.wait()

        def body(r, carry):
            gather_copy(idx_smem[s, r] & (T_ALL - 1), s, r).start()
            return carry

        lax.fori_loop(0, TM, body, 0)

    def wait_scatter(t, s):
        n = nv_ref[t]
        pltpu.make_async_copy(obuf.at[s, pl.ds(0, n)], slots_hbm.at[pl.ds(0, n)], sem_s.at[s]).wait()

    @pl.when(i == 0)
    def _():
        start_gather(0, 0)

    @pl.when(i + 1 < n_used)
    def _():
        start_gather(i + 1, 1 - slot)

    @pl.when(jnp.logical_and(i >= 2, i - 2 < n_used))
    def _():
        wait_scatter(i - 2, slot)

    @pl.when(i < n_used)
    def _():
        prev = te_ref[jnp.maximum(i - 1, 0)]

        @pl.when(jnp.logical_or(i == 0, te_ref[i] != prev))
        def _():
            wgu_bf[...] = wgu_ref[0].astype(BF16)
            wd_bf[...] = wd_ref[0].astype(BF16)

        pltpu.make_async_copy(h2_hbm.at[pl.ds(0, TM)], xbuf.at[slot], sem_g.at[slot]).wait()
        for jj in range(D_MODEL // 128):
            xmat[:, jj * 128:(jj + 1) * 128] = xbuf[slot, :, jj, :].astype(BF16)
        gu = _dot(xmat[...], wgu_bf[...]) + bgu_ref[0]
        gate = jnp.minimum(gu[:, :D_FF], SWIGLU_LIMIT)
        up = jnp.clip(gu[:, D_FF:], -SWIGLU_LIMIT, SWIGLU_LIMIT)
        act = (up + 1.0) * gate * jax.nn.sigmoid(SWIGLU_ALPHA * gate)
        out = _dot(act.astype(BF16), wd_bf[...]) + bd_ref[0]
        for jj in range(D_MODEL // 128):
            obuf[slot, :, jj, :] = out[:, jj * 128:(jj + 1) * 128]

        def body(r, carry):
            scatter_copy(slot, r, idx_smem[slot, r]).start()
            return carry

        lax.fori_loop(0, nv_ref[i], body, 0)

    @pl.when(i == N_TILES - 1)
    def _():
        @pl.when(i - 1 < n_used)
        def _():
            wait_scatter(i - 1, 1 - slot)

        @pl.when(i < n_used)
        def _():
            wait_scatter(i, slot)


def _moe(tile_e, tile_nv, n_used, rows, h2, w_gate_up, b_gate_up, w_down, b_down):
    grid_spec = pltpu.PrefetchScalarGridSpec(
        num_scalar_prefetch=3,
        grid=(N_TILES,),
        in_specs=[pl.BlockSpec(memory_space=pl.ANY),
                  pl.BlockSpec(memory_space=pl.ANY),
                  pl.BlockSpec((1, D_MODEL, 2 * D_FF), lambda i, te, nv, nu: (te[i], 0, 0)),
                  pl.BlockSpec((1, 1, 2 * D_FF), lambda i, te, nv, nu: (te[i], 0, 0)),
                  pl.BlockSpec((1, D_FF, D_MODEL), lambda i, te, nv, nu: (te[i], 0, 0)),
                  pl.BlockSpec((1, 1, D_MODEL), lambda i, te, nv, nu: (te[i], 0, 0))],
        out_specs=pl.BlockSpec(memory_space=pl.ANY),
        scratch_shapes=[pltpu.SMEM((2, TM), I32),
                        pltpu.VMEM((2, TM, D_MODEL // 128, 128), F32),
                        pltpu.VMEM((TM, D_MODEL), BF16),
                        pltpu.VMEM((2, TM, D_MODEL // 128, 128), F32),
                        pltpu.VMEM((D_MODEL, 2 * D_FF), BF16),
                        pltpu.VMEM((D_FF, D_MODEL), BF16),
                        pltpu.SemaphoreType.DMA,
                        pltpu.SemaphoreType.DMA((2,)),
                        pltpu.SemaphoreType.DMA((2,))])
    return pl.pallas_call(
        _moe_kernel,
        out_shape=jax.ShapeDtypeStruct((TOP_K * T_ALL, D_MODEL // 128, 128), F32),
        grid_spec=grid_spec,
        compiler_params=_params(),
        name="moe",
    )(tile_e, tile_nv, n_used, rows, h2, w_gate_up, b_gate_up.reshape(N_EXP, 1, 2 * D_FF),
      w_down, b_down.reshape(N_EXP, 1, D_MODEL))


def _final_kernel(x1_ref, slots_ref, tw_ref, mod_ref, fn_ref, o_ref):
    row = _mod_row(pl.program_id(0))
    gate2 = mod_ref[pl.ds(row, 1), pl.ds(5 * D_MODEL, D_MODEL)]
    tw = tw_ref[...]
    cols = []
    for jj in range(D_MODEL // 128):
        acc = tw[:, 0:1] * slots_ref[0, :, jj, :]
        for kk in range(1, TOP_K):
            acc = acc + tw[:, kk:kk + 1] * slots_ref[kk, :, jj, :]
        cols.append(acc)
    y = jnp.concatenate(cols, axis=1)
    o_ref[...] = _rms(x1_ref[...] + gate2 * y, fn_ref[...])


def _final(x1, slots, tw, mod, final_norm):
    return pl.pallas_call(
        _final_kernel,
        out_shape=jax.ShapeDtypeStruct((T_ALL, D_MODEL), F32),
        grid=(NB,),
        in_specs=[pl.BlockSpec((TB, D_MODEL), lambda i: (i, 0)),
                  pl.BlockSpec((TOP_K, TB, D_MODEL // 128, 128), lambda i: (0, i, 0, 0)),
                  pl.BlockSpec((TB, TOP_K), lambda i: (i, 0)),
                  pl.BlockSpec((MOD_ROWS, 6 * D_MODEL), lambda i: (0, 0)),
                  pl.BlockSpec((1, D_MODEL), lambda i: (0, 0))],
        out_specs=pl.BlockSpec((TB, D_MODEL), lambda i: (i, 0)),
        compiler_params=_params(),
        name="final",
    )(x1, slots, tw, mod, final_norm)


def _dispatch_tables(top_e, rank, hist):
    hist = hist.reshape(NB, N_EXP)
    counts = jnp.sum(hist, axis=0)
    padded = ((counts + TM - 1) // TM) * TM
    pad_end = jnp.cumsum(padded)
    pad_off = pad_end - padded
    blk_base = pad_off[None, :] + jnp.cumsum(hist, axis=0) - hist
    base = jnp.take_along_axis(blk_base, top_e.reshape(NB, TOP_K * TB), axis=1).reshape(NB, TOP_K, TB)
    pos = base + rank
    tok = (jnp.arange(NB, dtype=I32) * TB)[:, None, None] + jnp.arange(TB, dtype=I32)[None, None, :]
    dst = jnp.arange(TOP_K, dtype=I32)[None, :, None] * T_ALL + tok
    rows = jnp.zeros((N_TILES * TM,), I32).at[pos.reshape(-1)].set(
        dst.reshape(-1), unique_indices=True, mode="drop").reshape(N_TILES, TM)
    n_used = (pad_end[-1] // TM).astype(I32)
    start = jnp.arange(N_TILES, dtype=I32) * TM
    tile_e = jnp.minimum(jnp.searchsorted(pad_end, start, side="right"), N_EXP - 1).astype(I32)
    last_e = tile_e[jnp.maximum(n_used - 1, 0)]
    used = jnp.arange(N_TILES) < n_used
    tile_e = jnp.where(used, tile_e, last_e)
    tile_nv = jnp.where(used, jnp.clip(counts[tile_e] - (start - pad_off[tile_e]), 0, TM), 0).astype(I32)
    return tile_e, tile_nv, n_used.reshape(1), rows


def kernel(x_prompt, x_sample, state_hgrn, c, c_ctx, w_ada, b_ada, norm1, w_in, hgrn_lb, hgrn_norm, w_pool,
           pool_scale, w_branch_a, w_branch_b, w_out, norm2, w_router, b_router, w_gate_up, b_gate_up,
           w_down, b_down, final_norm):
    x_all = jnp.concatenate([x_prompt.reshape(T_CTX, D_MODEL), x_sample.reshape(T_LAT, D_MODEL)], axis=0)
    cc = jnp.zeros((MOD_ROWS, D_MODEL), F32).at[:N_LAT_SEQ].set(c).at[N_LAT_SEQ].set(c_ctx)
    mod = _ada(cc, w_ada[0], b_ada)

    proj = _inproj(x_all, mod, norm1, w_in[0].astype(BF16))

    mall, masks = _hgrn_consts()
    o_f, o_b, new_state = _hgrn(proj, hgrn_lb, state_hgrn[:, 0], mall, masks)

    a_pool, cnt_pool = _pool_consts()
    yb = _pool(proj, a_pool, cnt_pool, w_pool[0].astype(BF16), pool_scale)

    wr_t = w_router[0].T
    wr_hi = wr_t.astype(BF16)
    wr_lo = (wr_t - wr_hi.astype(F32)).astype(BF16)
    tri = jnp.asarray(np.triu(np.ones((TB, TB), np.float32), 1), BF16)
    x1, h2, top_e, top_w, rank, hist = _merge(
        x_all, o_f, o_b, proj, yb, mod, hgrn_norm, w_branch_a[0].astype(BF16), w_branch_b[0].astype(BF16),
        w_out[0].astype(BF16), norm2, wr_hi, wr_lo, b_router.reshape(N_EXP, 1), tri)

    tile_e, tile_nv, n_used, rows = _dispatch_tables(top_e, rank, hist)
    slots = _moe(tile_e, tile_nv, n_used, rows, h2, w_gate_up[0], b_gate_up[0], w_down[0], b_down[0])

    tw = top_w.transpose(0, 2, 1).reshape(T_ALL, TOP_K)
    y = _final(x1, slots.reshape(TOP_K, T_ALL, D_MODEL // 128, 128), tw, mod, final_norm.reshape(1, D_MODEL))
    y_prompt = y[:T_CTX].reshape(N_CTX_SEQ, CTX_LEN, D_MODEL)
    y_sample = y[T_CTX:].reshape(N_LAT_SEQ, LAT_LEN, D_MODEL)
    return y_prompt, y_sample, new_state[:, None]
```

```python
import functools

import numpy as np
import jax
import jax.numpy as jnp
from jax import lax
from jax.experimental import pallas as pl
from jax.experimental.pallas import tpu as pltpu

F32 = jnp.float32
BF16 = jnp.bfloat16
I32 = jnp.int32

D_MODEL = 1024
N_CTX_SEQ, CTX_LEN = 32, 256
N_LAT_SEQ, LAT_LEN = 4, 2048
T_CTX = N_CTX_SEQ * CTX_LEN
T_LAT = N_LAT_SEQ * LAT_LEN
T_ALL = T_CTX + T_LAT
TB = 256
NB = T_ALL // TB
NB_CTX = T_CTX // TB
LAT_BLOCKS = LAT_LEN // TB
HEADS, HEAD_K, HEAD_V = 4, 128, 128
HGRN_W = HEADS * HEAD_V
POOL_WINDOWS = (2, 4, 8, 16)
POOL_G = 128
POOL_W = len(POOL_WINDOWS) * POOL_G
GRID_W = 64
GRID_H = LAT_LEN // GRID_W
IN_W = 5 * HGRN_W + POOL_W + 2 * D_MODEL
N_EXP, TOP_K, D_FF = 32, 4, 1024
SWIGLU_LIMIT = 7.0
SWIGLU_ALPHA = 1.702
EPS = 1e-6
CHUNK = 64
N_LEVELS = 6
EXP_ROWS = (N_LEVELS + 2) * CHUNK
TM = 256
LROWS = TB * TOP_K + N_EXP * 8
SEG_BITS = 6
N_TILES = -(-(T_ALL * TOP_K + NB * N_EXP * 7 + N_EXP * (TM - 1)) // TM)
MOD_ROWS = 8
VMEM_LIMIT = 56 * 1024 * 1024


def _params(sem=("arbitrary",)):
    return pltpu.CompilerParams(dimension_semantics=sem, vmem_limit_bytes=VMEM_LIMIT)


def _dot(a, b):
    return jnp.dot(a, b, preferred_element_type=F32)


def _dot_nt(a, b):
    return lax.dot_general(a, b, (((1,), (1,)), ((), ())), preferred_element_type=F32)


def _dot_tn(a, b):
    return lax.dot_general(a, b, (((0,), (0,)), ((), ())), preferred_element_type=F32)


def _split2(x):
    hi = x.astype(BF16)
    lo = (x - hi.astype(F32)).astype(BF16)
    return hi, lo


def _mod_row(i):
    return jnp.where(i < NB_CTX, N_LAT_SEQ, (i - NB_CTX) // LAT_BLOCKS)


def _ada_kernel(c_ref, w_ref, b_ref, o_ref):
    c = c_ref[...]
    s = c * jax.nn.sigmoid(c)
    o_ref[...] = jnp.dot(s, w_ref[...], preferred_element_type=F32,
                         precision=lax.Precision.HIGHEST) + b_ref[...]


def _ada(cc, w_ada, b_ada):
    nblk = 1536
    return pl.pallas_call(
        _ada_kernel,
        out_shape=jax.ShapeDtypeStruct((MOD_ROWS, 6 * D_MODEL), F32),
        grid=(6 * D_MODEL // nblk,),
        in_specs=[pl.BlockSpec((MOD_ROWS, D_MODEL), lambda j: (0, 0)),
                  pl.BlockSpec((D_MODEL, nblk), lambda j: (0, j)),
                  pl.BlockSpec((1, nblk), lambda j: (0, j))],
        out_specs=pl.BlockSpec((MOD_ROWS, nblk), lambda j: (0, j)),
        compiler_params=_params(),
        name="ada",
    )(cc, w_ada, b_ada)


def _rms(x, g):
    ms = jnp.mean(x * x, axis=-1, keepdims=True)
    return x * lax.rsqrt(ms + EPS) * g


def _inproj_kernel(x_ref, mod_ref, n1_ref, w_ref, o_ref):
    row = _mod_row(pl.program_id(0))
    shift = mod_ref[pl.ds(row, 1), pl.ds(0, D_MODEL)]
    scale = mod_ref[pl.ds(row, 1), pl.ds(D_MODEL, D_MODEL)]
    h = _rms(x_ref[...], n1_ref[...]) * (1.0 + scale) + shift
    o_ref[...] = _dot(h.astype(BF16), w_ref[...])


def _inproj(x_all, mod, norm1, w_in_bf):
    return pl.pallas_call(
        _inproj_kernel,
        out_shape=jax.ShapeDtypeStruct((T_ALL, IN_W), F32),
        grid=(NB,),
        in_specs=[pl.BlockSpec((TB, D_MODEL), lambda i: (i, 0)),
                  pl.BlockSpec((MOD_ROWS, 6 * D_MODEL), lambda i: (0, 0)),
                  pl.BlockSpec((1, D_MODEL), lambda i: (0, 0)),
                  pl.BlockSpec((D_MODEL, IN_W), lambda i: (0, 0))],
        out_specs=pl.BlockSpec((TB, IN_W), lambda i: (i, 0)),
        compiler_params=_params(),
        name="inproj",
    )(x_all, mod, norm1, w_in_bf)


def _hgrn_consts():
    c = CHUNK
    t = np.arange(c)[:, None]
    u = np.arange(c)[None, :]
    blocks = [u <= t]
    masks = [np.eye(c, dtype=bool)]
    h = c // 2
    while h >= 1:
        bi = t // h
        upper = (bi % 2) == 1
        e_up = (u >= bi * h) & (u <= t)
        e_lo = (u > t) & (u <= bi * h + h - 1)
        blocks.append(np.where(upper, e_up, e_lo))
        masks.append(((t // (2 * h)) == (u // (2 * h))) & (((t // h) % 2) == 1) & (((u // h) % 2) == 0))
        h //= 2
    blocks.append(u > t)
    m_f = np.stack(blocks).astype(np.float32)
    k_f = np.stack(masks).astype(np.float32)
    m_b = m_f[:, ::-1, ::-1]
    k_b = k_f[:, ::-1, ::-1]
    m = np.stack([m_f.reshape(EXP_ROWS, c), m_b.reshape(EXP_ROWS, c)])
    m3 = np.concatenate([m, m, m], axis=2)
    return jnp.asarray(m3, BF16), jnp.asarray(np.stack([k_f, k_b]), F32)


def _hgrn_dir(d, q_ref, f_ref, v_ref, o_ref, r0, lb, mall_ref, mask_ref, st_ref):
    c = CHUNK
    rows = pl.ds(r0, c)
    q = q_ref[rows, :] * (HEAD_K ** -0.5)
    v = v_ref[rows, :].astype(BF16)
    f = lb + (1.0 - lb) * jax.nn.sigmoid(f_ref[rows, :])
    g = jnp.log(f)
    k = 1.0 - f
    g1 = g.astype(BF16)
    r1 = g - g1.astype(F32)
    g2 = r1.astype(BF16)
    g3 = (r1 - g2.astype(F32)).astype(BF16)
    gsplit = jnp.concatenate([g1, g2, g3], axis=0)
    z = jnp.exp(_dot(mall_ref[d], gsplit))
    tot_row = c - 1 if d == 0 else 0
    for h in range(HEADS):
        sl = slice(h * HEAD_K, (h + 1) * HEAD_K)
        qh, kh, vh, zh = q[:, sl], k[:, sl], v[:, sl], z[:, sl]
        st = st_ref[d, h]
        o = _dot_nt((qh * zh[0:c]).astype(BF16), st.astype(BF16))
        scores = mask_ref[d, 0] * _dot_nt(qh.astype(BF16), kh.astype(BF16))
        for lev in range(N_LEVELS):
            zl = zh[(lev + 1) * c:(lev + 2) * c]
            scores = scores + mask_ref[d, lev + 1] * _dot_nt((qh * zl).astype(BF16), (kh * zl).astype(BF16))
        o = o + _dot(scores.astype(BF16), vh)
        kd = (kh * zh[(N_LEVELS + 1) * c:]).astype(BF16)
        st_ref[d, h] = st * zh[tot_row:tot_row + 1] + _dot_tn(vh, kd)
        o_ref[rows, sl] = o


def _hgrn_kernel(qf_ref, ff_ref, vf_ref, qb_ref, fb_ref, vb_ref, lbraw_ref, s0_ref, mall_ref, mask_ref,
                 of_ref, ob_ref, sout_hbm, st_ref, stage_ref, sem):
    i = pl.program_id(0)
    j = (i - NB_CTX) % LAT_BLOCKS
    is_ctx = i < NB_CTX

    @pl.when(is_ctx)
    def _():
        st_ref[...] = jnp.zeros_like(st_ref)

    @pl.when(jnp.logical_and(jnp.logical_not(is_ctx), j == 0))
    def _():
        for d in range(2):
            for h in range(HEADS):
                st_ref[d, h] = s0_ref[0, d, h].T

    a0 = lbraw_ref[0]
    a1 = lbraw_ref[1]
    mx = jnp.maximum(a0, a1)
    e0 = jnp.exp(a0 - mx)
    e1 = jnp.exp(a1 - mx)
    lb = e0 / (e0 + e1)

    nchunk = TB // CHUNK

    def body(ci, carry):
        rf = pl.multiple_of(ci * CHUNK, CHUNK)
        rb = pl.multiple_of((nchunk - 1 - ci) * CHUNK, CHUNK)
        _hgrn_dir(0, qf_ref, ff_ref, vf_ref, of_ref, rf, lb[0:1], mall_ref, mask_ref, st_ref)
        _hgrn_dir(1, qb_ref, fb_ref, vb_ref, ob_ref, rb, lb[1:2], mall_ref, mask_ref, st_ref)
        return carry

    lax.fori_loop(0, nchunk, body, 0)

    @pl.when(is_ctx)
    def _():
        for d in range(2):
            for h in range(HEADS):
                stage_ref[d, h] = st_ref[d, h].T
        cp = pltpu.make_async_copy(stage_ref, sout_hbm.at[i], sem)
        cp.start()
        cp.wait()


def _bwd_block(i):
    j = (i - NB_CTX) % LAT_BLOCKS
    return jnp.where(i < NB_CTX, i, i - j + (LAT_BLOCKS - 1 - j))


def _hgrn(proj, hgrn_lb, s0, mall, masks):
    nh = HGRN_W
    fwd = lambda col: pl.BlockSpec((TB, nh), lambda i: (i, col))
    bwd = lambda col: pl.BlockSpec((TB, nh), lambda i: (_bwd_block(i), col))
    lat_seq = lambda i: jnp.clip((i - NB_CTX) // LAT_BLOCKS, 0, N_LAT_SEQ - 1)
    return pl.pallas_call(
        _hgrn_kernel,
        out_shape=(jax.ShapeDtypeStruct((T_ALL, nh), F32),
                   jax.ShapeDtypeStruct((T_ALL, nh), F32),
                   jax.ShapeDtypeStruct((N_CTX_SEQ, 2, HEADS, HEAD_K, HEAD_V), F32)),
        grid=(NB,),
        in_specs=[fwd(0), fwd(1), fwd(3), bwd(0), bwd(2), bwd(3),
                  pl.BlockSpec((2, 2, nh), lambda i: (0, 0, 0)),
                  pl.BlockSpec((1, 2, HEADS, HEAD_K, HEAD_V), lambda i: (lat_seq(i), 0, 0, 0, 0)),
                  pl.BlockSpec((2, EXP_ROWS, 3 * CHUNK), lambda i: (0, 0, 0)),
                  pl.BlockSpec((2, N_LEVELS + 1, CHUNK, CHUNK), lambda i: (0, 0, 0, 0))],
        out_specs=(pl.BlockSpec((TB, nh), lambda i: (i, 0)),
                   pl.BlockSpec((TB, nh), lambda i: (_bwd_block(i), 0)),
                   pl.BlockSpec(memory_space=pl.ANY)),
        scratch_shapes=[pltpu.VMEM((2, HEADS, HEAD_V, HEAD_K), F32),
                        pltpu.VMEM((2, HEADS, HEAD_K, HEAD_V), F32),
                        pltpu.SemaphoreType.DMA],
        compiler_params=_params(),
        name="hgrn",
    )(proj, proj, proj, proj, proj, proj, hgrn_lb, s0, mall, masks)


def _window_bounds(n, w):
    pos = np.arange(n)
    lo = np.clip(pos - w // 2, 0, n - 1)
    hi = np.clip(pos - w // 2 + w - 1, 0, n - 1)
    return lo, hi


def _pool_consts():
    seq, img, cnt_seq, cnt_col = [], [], [], []
    for w in POOL_WINDOWS:
        lo, hi = _window_bounds(CTX_LEN, w)
        u = np.arange(CTX_LEN)[None, :]
        seq.append((u >= lo[:, None]) & (u <= hi[:, None]))
        cnt_seq.append(hi - lo + 1)
        lo, hi = _window_bounds(GRID_W, w)
        u = np.arange(GRID_W)[None, :]
        band = (u >= lo[:, None]) & (u <= hi[:, None])
        img.append(np.kron(np.eye(TB // GRID_W, dtype=bool), band))
        cnt_col.append(np.tile(hi - lo + 1, TB // GRID_W))
    a = np.stack([np.stack(seq), np.stack(img)]).astype(np.float32)
    cnt = np.stack([np.stack(cnt_seq), np.stack(cnt_col)]).astype(np.float32)
    cnt = np.broadcast_to(cnt[..., None], cnt.shape + (POOL_G,))
    return jnp.asarray(a, BF16), jnp.asarray(cnt, F32)


POOL_ROWS = LAT_LEN


def _pool_kernel(u_ref, a_ref, cnt_ref, wp_ref, ps_ref, o_ref, cp_ref):
    i = pl.program_id(0)
    nblk = POOL_ROWS // TB

    def finish(g, r0, nrows, pm):
        sl = slice(g * POOL_G, (g + 1) * POOL_G)
        d = pm - u_ref[pl.ds(r0, nrows), sl]
        y = _dot(d.astype(BF16), wp_ref[g]) * ps_ref[:, sl]
        o_ref[pl.ds(r0, nrows), sl] = y.astype(o_ref.dtype)

    def window_sum(kind, g, b):
        sl = slice(g * POOL_G, (g + 1) * POOL_G)
        hi, lo = _split2(u_ref[pl.ds(b * TB, TB), sl])
        a = a_ref[kind, g]
        return (_dot(a, hi) + _dot(a, lo)) / cnt_ref[kind, g]

    @pl.when(i < T_CTX // POOL_ROWS)
    def _():
        for g in range(len(POOL_WINDOWS)):
            for b in range(nblk):
                finish(g, b * TB, TB, window_sum(0, g, b))

    @pl.when(i >= T_CTX // POOL_ROWS)
    def _():
        for g, w in enumerate(POOL_WINDOWS):
            for b in range(nblk):
                cp_ref[pl.ds(b * TB, TB), :] = window_sum(1, g, b)
            lo, hi = _window_bounds(GRID_H, w)
            for r in range(GRID_H):
                acc = cp_ref[pl.ds(int(lo[r]) * GRID_W, GRID_W), :]
                for rr in range(int(lo[r]) + 1, int(hi[r]) + 1):
                    acc = acc + cp_ref[pl.ds(rr * GRID_W, GRID_W), :]
                finish(g, r * GRID_W, GRID_W, acc / float(hi[r] - lo[r] + 1))


def _pool(proj, a_pool, cnt_pool, w_pool_bf, pool_scale):
    col = (5 * HGRN_W) // POOL_W
    return pl.pallas_call(
        _pool_kernel,
        out_shape=jax.ShapeDtypeStruct((T_ALL, POOL_W), BF16),
        grid=(T_ALL // POOL_ROWS,),
        in_specs=[pl.BlockSpec((POOL_ROWS, POOL_W), lambda i: (i, col)),
                  pl.BlockSpec((2, 4, TB, TB), lambda i: (0, 0, 0, 0)),
                  pl.BlockSpec((2, 4, TB, POOL_G), lambda i: (0, 0, 0, 0)),
                  pl.BlockSpec((4, POOL_G, POOL_G), lambda i: (0, 0, 0)),
                  pl.BlockSpec((1, POOL_W), lambda i: (0, 0))],
        out_specs=pl.BlockSpec((POOL_ROWS, POOL_W), lambda i: (i, 0)),
        scratch_shapes=[pltpu.VMEM((POOL_ROWS, POOL_G), F32)],
        compiler_params=_params(),
        name="pool",
    )(proj, a_pool, cnt_pool, w_pool_bf, pool_scale)


def _merge_kernel(x_ref, of_ref, ob_ref, og_ref, yb_ref, ga_ref, gb_ref, mod_ref, hn_ref, wa_ref, wb_ref,
                  wo_ref, n2_ref, wrh_ref, wrl_ref, br_ref,
                  x1_ref, h2_ref, te_ref, tw_ref, hist_ref):
    row = _mod_row(pl.program_id(0))
    gate1 = mod_ref[pl.ds(row, 1), pl.ds(2 * D_MODEL, D_MODEL)]
    shift2 = mod_ref[pl.ds(row, 1), pl.ds(3 * D_MODEL, D_MODEL)]
    scale2 = mod_ref[pl.ds(row, 1), pl.ds(4 * D_MODEL, D_MODEL)]

    o = of_ref[...] + ob_ref[...]
    og = og_ref[...]
    ya = jnp.concatenate(
        [_rms(o[:, h * HEAD_V:(h + 1) * HEAD_V], hn_ref[...]) for h in range(HEADS)], axis=1)
    ya = ya * (og * jax.nn.sigmoid(og))
    merged = (jax.nn.sigmoid(ga_ref[...]) * _dot(ya.astype(BF16), wa_ref[...])
              + jax.nn.sigmoid(gb_ref[...]) * _dot(yb_ref[...], wb_ref[...]))
    x1 = x_ref[...] + gate1 * _dot(merged.astype(BF16), wo_ref[...])
    x1_ref[...] = x1
    h2 = _rms(x1, n2_ref[...]) * (1.0 + scale2) + shift2
    hh, hl = _split2(h2)
    h2_ref[...] = hh

    lt = _dot_nt(wrh_ref[...], hh) + _dot_nt(wrl_ref[...], hh) + _dot_nt(wrh_ref[...], hl) + br_ref[...]
    eidx = lax.broadcasted_iota(I32, (N_EXP, TB), 0)
    vals, idxs, cnt = [], [], jnp.zeros((N_EXP, TB), F32)
    for _ in range(TOP_K):
        m = jnp.max(lt, axis=0, keepdims=True)
        idx = jnp.min(jnp.where(lt == m, eidx, N_EXP), axis=0, keepdims=True)
        sel = eidx == idx
        vals.append(m)
        idxs.append(idx)
        cnt = cnt + sel.astype(F32)
        lt = jnp.where(sel, -jnp.inf, lt)
    ex = [jnp.exp(v - vals[0]) for v in vals]
    den = ex[0] + ex[1] + ex[2] + ex[3]
    tw_ref[0] = jnp.concatenate([e / den for e in ex], axis=0)
    te_ref[0] = jnp.concatenate(idxs, axis=0)
    hist_ref[0] = jnp.sum(cnt, axis=1, keepdims=True).astype(I32)


def _merge(x_all, o_f, o_b, proj, yb, mod, hgrn_norm, wa_bf, wb_bf, wo_bf, norm2, wr_hi, wr_lo, b_router):
    full = lambda shape: pl.BlockSpec(shape, lambda i: (0,) * len(shape))
    return pl.pallas_call(
        _merge_kernel,
        out_shape=(jax.ShapeDtypeStruct((T_ALL, D_MODEL), F32),
                   jax.ShapeDtypeStruct((T_ALL, D_MODEL), BF16),
                   jax.ShapeDtypeStruct((NB, TOP_K, TB), I32),
                   jax.ShapeDtypeStruct((NB, TOP_K, TB), F32),
                   jax.ShapeDtypeStruct((NB, N_EXP, 1), I32)),
        grid=(NB,),
        in_specs=[pl.BlockSpec((TB, D_MODEL), lambda i: (i, 0)),
                  pl.BlockSpec((TB, HGRN_W), lambda i: (i, 0)),
                  pl.BlockSpec((TB, HGRN_W), lambda i: (i, 0)),
                  pl.BlockSpec((TB, HGRN_W), lambda i: (i, 4)),
                  pl.BlockSpec((TB, POOL_W), lambda i: (i, 0)),
                  pl.BlockSpec((TB, D_MODEL), lambda i: (i, 3)),
                  pl.BlockSpec((TB, D_MODEL), lambda i: (i, 4)),
                  full((MOD_ROWS, 6 * D_MODEL)),
                  full((1, HEAD_V)),
                  full((HGRN_W, D_MODEL)),
                  full((POOL_W, D_MODEL)),
                  full((D_MODEL, D_MODEL)),
                  full((1, D_MODEL)),
                  full((N_EXP, D_MODEL)),
                  full((N_EXP, D_MODEL)),
                  full((N_EXP, 1))],
        out_specs=(pl.BlockSpec((TB, D_MODEL), lambda i: (i, 0)),
                   pl.BlockSpec((TB, D_MODEL), lambda i: (i, 0)),
                   pl.BlockSpec((1, TOP_K, TB), lambda i: (i, 0, 0)),
                   pl.BlockSpec((1, TOP_K, TB), lambda i: (i, 0, 0)),
                   pl.BlockSpec((1, N_EXP, 1), lambda i: (i, 0, 0))),
        compiler_params=_params(),
        name="merge",
    )(x_all, o_f, o_b, proj, yb, proj, proj, mod, hgrn_norm, wa_bf, wb_bf, wo_bf, norm2,
      wr_hi, wr_lo, b_router)


def _local_rows(te_ref, loff_ref, tri_ref):
    te = te_ref[0]
    eidx = lax.broadcasted_iota(I32, (N_EXP, TB), 0)
    sels = [eidx == te[k:k + 1] for k in range(TOP_K)]
    cnt = sels[0].astype(F32)
    for s in sels[1:]:
        cnt = cnt + s.astype(F32)
    base = _dot(cnt.astype(BF16), tri_ref[...]) + loff_ref[0]
    return [jnp.sum(jnp.where(s, base, 0.0), axis=0, keepdims=True) for s in sels]


def _segment_copies(make_copy, local_off, global_off, units):
    for j in range(SEG_BITS):
        low = (units & ((1 << j) - 1)) * 8

        @pl.when(((units >> j) & 1) == 1)
        def _():
            make_copy(pl.multiple_of(local_off + low, 8), pl.multiple_of(global_off + low, 8), 8 << j).start()


def _dispatch_kernel(loff_s, seg_s, gbase_s, tail_s, h2_ref, te_ref, loffv_ref, tri_ref, xs_hbm,
                     loc, zeros, sem, sem_z):
    b = pl.program_id(0)
    lrow = _local_rows(te_ref, loffv_ref, tri_ref)
    piota = lax.broadcasted_iota(I32, (LROWS, TB), 0).astype(F32)
    hit = piota == lrow[0]
    for k in range(1, TOP_K):
        hit = jnp.logical_or(hit, piota == lrow[k])
    loc[...] = _dot(jnp.where(hit, 1.0, 0.0).astype(BF16), h2_ref[...])

    def out_copy(a, g, size):
        return pltpu.make_async_copy(loc.at[pl.ds(a, size)], xs_hbm.at[pl.ds(g, size)], sem)

    def body(e, tot):
        idx = b * N_EXP + e
        _segment_copies(out_copy, loff_s[idx], gbase_s[idx], seg_s[idx] // 8)
        return tot + seg_s[idx]

    tot = lax.fori_loop(0, N_EXP, body, 0)

    @pl.when(b == NB - 1)
    def _():
        zeros[...] = jnp.zeros_like(zeros)

        def zero_copy(a, g, size):
            return pltpu.make_async_copy(zeros.at[pl.ds(a, size)], xs_hbm.at[pl.ds(g, size)], sem_z)

        def zbody(e, ztot):
            _segment_copies(zero_copy, 0, tail_s[e], tail_s[N_EXP + e] // 8)
            return ztot + tail_s[N_EXP + e]

        def tbody(t, carry):
            pltpu.make_async_copy(zeros, xs_hbm.at[pl.ds(pl.multiple_of(t * TM, TM), TM)], sem_z).start()
            return carry

        n_used = tail_s[2 * N_EXP]
        lax.fori_loop(n_used, N_TILES, tbody, 0)
        ztot = lax.fori_loop(0, N_EXP, zbody, 0) + (N_TILES - n_used) * TM
        ztot = pl.multiple_of(ztot, 8)

        @pl.when(ztot > 0)
        def _():
            pltpu.make_async_copy(xs_hbm.at[pl.ds(0, ztot)], xs_hbm.at[pl.ds(0, ztot)], sem_z).wait()

    tot = pl.multiple_of(tot, 8)
    pltpu.make_async_copy(loc.at[pl.ds(0, tot)], xs_hbm.at[pl.ds(0, tot)], sem).wait()


def _dispatch(tables, h2, top_e, tri):
    loff_s, seg_s, gbase_s, tail_s, loff_v = tables
    grid_spec = pltpu.PrefetchScalarGridSpec(
        num_scalar_prefetch=4,
        grid=(NB,),
        in_specs=[pl.BlockSpec((TB, D_MODEL), lambda i, *_: (i, 0)),
                  pl.BlockSpec((1, TOP_K, TB), lambda i, *_: (i, 0, 0)),
                  pl.BlockSpec((1, N_EXP, 1), lambda i, *_: (i, 0, 0)),
                  pl.BlockSpec((TB, TB), lambda i, *_: (0, 0))],
        out_specs=pl.BlockSpec(memory_space=pl.ANY),
        scratch_shapes=[pltpu.VMEM((LROWS, D_MODEL), F32),
                        pltpu.VMEM((TM, D_MODEL), F32),
                        pltpu.SemaphoreType.DMA,
                        pltpu.SemaphoreType.DMA])
    return pl.pallas_call(
        _dispatch_kernel,
        out_shape=jax.ShapeDtypeStruct((N_TILES * TM, D_MODEL), F32),
        grid_spec=grid_spec,
        compiler_params=_params(),
        name="dispatch",
    )(loff_s, seg_s, gbase_s, tail_s, h2, top_e, loff_v, tri)


def _moe_kernel(te_ref, nu_ref, xs_ref, wgu_ref, bgu_ref, wd_ref, bd_ref, o_ref, wgu_bf, wd_bf):
    i = pl.program_id(0)

    @pl.when(i < nu_ref[0])
    def _():
        prev = te_ref[jnp.maximum(i - 1, 0)]

        @pl.when(jnp.logical_or(i == 0, te_ref[i] != prev))
        def _():
            wgu_bf[...] = wgu_ref[0].astype(BF16)
            wd_bf[...] = wd_ref[0].astype(BF16)

        gu = _dot(xs_ref[...].astype(BF16), wgu_bf[...]) + bgu_ref[0]
        gate = jnp.minimum(gu[:, :D_FF], SWIGLU_LIMIT)
        up = jnp.clip(gu[:, D_FF:], -SWIGLU_LIMIT, SWIGLU_LIMIT)
        act = (up + 1.0) * gate * jax.nn.sigmoid(SWIGLU_ALPHA * gate)
        o_ref[...] = _dot(act.astype(BF16), wd_bf[...]) + bd_ref[0]

    @pl.when(i >= nu_ref[0])
    def _():
        o_ref[...] = jnp.zeros_like(o_ref)


def _moe(tile_e, n_used, xs, w_gate_up, b_gate_up, w_down, b_down):
    row_tile = lambda i, te, nu: (jnp.minimum(i, nu[0] - 1), 0)
    out_tile = lambda i, te, nu: (i, 0)
    grid_spec = pltpu.PrefetchScalarGridSpec(
        num_scalar_prefetch=2,
        grid=(N_TILES,),
        in_specs=[pl.BlockSpec((TM, D_MODEL), row_tile),
                  pl.BlockSpec((1, D_MODEL, 2 * D_FF), lambda i, te, nu: (te[i], 0, 0)),
                  pl.BlockSpec((1, 1, 2 * D_FF), lambda i, te, nu: (te[i], 0, 0)),
                  pl.BlockSpec((1, D_FF, D_MODEL), lambda i, te, nu: (te[i], 0, 0)),
                  pl.BlockSpec((1, 1, D_MODEL), lambda i, te, nu: (te[i], 0, 0))],
        out_specs=pl.BlockSpec((TM, D_MODEL), out_tile),
        scratch_shapes=[pltpu.VMEM((D_MODEL, 2 * D_FF), BF16),
                        pltpu.VMEM((D_FF, D_MODEL), BF16)])
    return pl.pallas_call(
        _moe_kernel,
        out_shape=jax.ShapeDtypeStruct((N_TILES * TM, D_MODEL), F32),
        grid_spec=grid_spec,
        compiler_params=_params(),
        name="moe",
    )(tile_e, n_used, xs, w_gate_up, b_gate_up.reshape(N_EXP, 1, 2 * D_FF),
      w_down, b_down.reshape(N_EXP, 1, D_MODEL))


def _final_kernel(loff_s, seg_s, gbase_s, ys_hbm, x1_ref, te_ref, tw_ref, loffv_ref, tri_ref, mod_ref, fn_ref,
                  o_ref, loc, sem):
    b = pl.program_id(0)

    @pl.when(b == 0)
    def _():
        loc[...] = jnp.zeros_like(loc)

    def in_copy(a, g, size):
        return pltpu.make_async_copy(ys_hbm.at[pl.ds(g, size)], loc.at[pl.ds(a, size)], sem)

    def body(e, tot):
        idx = b * N_EXP + e
        _segment_copies(in_copy, loff_s[idx], gbase_s[idx], seg_s[idx] // 8)
        return tot + seg_s[idx]

    tot = pl.multiple_of(lax.fori_loop(0, N_EXP, body, 0), 8)

    lrow = _local_rows(te_ref, loffv_ref, tri_ref)
    tw = tw_ref[0]
    piota = lax.broadcasted_iota(I32, (LROWS, TB), 0).astype(F32)
    pw = jnp.where(piota == lrow[0], tw[0:1], 0.0)
    for k in range(1, TOP_K):
        pw = pw + jnp.where(piota == lrow[k], tw[k:k + 1], 0.0)

    pltpu.make_async_copy(ys_hbm.at[pl.ds(0, tot)], loc.at[pl.ds(0, tot)], sem).wait()
    y = _dot_tn(pw.astype(BF16), loc[...].astype(BF16))

    row = _mod_row(b)
    gate2 = mod_ref[pl.ds(row, 1), pl.ds(5 * D_MODEL, D_MODEL)]
    o_ref[...] = _rms(x1_ref[...] + gate2 * y, fn_ref[...])


def _final(tables, ys, x1, top_e, top_w, tri, mod, final_norm):
    loff_s, seg_s, gbase_s, _, loff_v = tables
    grid_spec = pltpu.PrefetchScalarGridSpec(
        num_scalar_prefetch=3,
        grid=(NB,),
        in_specs=[pl.BlockSpec(memory_space=pl.ANY),
                  pl.BlockSpec((TB, D_MODEL), lambda i, *_: (i, 0)),
                  pl.BlockSpec((1, TOP_K, TB), lambda i, *_: (i, 0, 0)),
                  pl.BlockSpec((1, TOP_K, TB), lambda i, *_: (i, 0, 0)),
                  pl.BlockSpec((1, N_EXP, 1), lambda i, *_: (i, 0, 0)),
                  pl.BlockSpec((TB, TB), lambda i, *_: (0, 0)),
                  pl.BlockSpec((MOD_ROWS, 6 * D_MODEL), lambda i, *_: (0, 0)),
                  pl.BlockSpec((1, D_MODEL), lambda i, *_: (0, 0))],
        out_specs=pl.BlockSpec((TB, D_MODEL), lambda i, *_: (i, 0)),
        scratch_shapes=[pltpu.VMEM((LROWS, D_MODEL), F32),
                        pltpu.SemaphoreType.DMA])
    return pl.pallas_call(
        _final_kernel,
        out_shape=jax.ShapeDtypeStruct((T_ALL, D_MODEL), F32),
        grid_spec=grid_spec,
        compiler_params=_params(),
        name="final",
    )(loff_s, seg_s, gbase_s, ys, x1, top_e, top_w, loff_v, tri, mod, final_norm)


def _dispatch_tables(hist):
    hist = hist.reshape(NB, N_EXP)
    seg = ((hist + 7) // 8) * 8
    loff = jnp.cumsum(seg, axis=1) - seg
    rows_e = jnp.sum(seg, axis=0)
    region = ((rows_e + TM - 1) // TM) * TM
    region_end = jnp.cumsum(region)
    region_start = region_end - region
    gbase = region_start[None, :] + jnp.cumsum(seg, axis=0) - seg
    n_used = (region_end[-1] // TM).astype(I32)
    tail = jnp.concatenate([region_start + rows_e, region - rows_e, n_used.reshape(1)])
    start = jnp.arange(N_TILES, dtype=I32) * TM
    tile_e = jnp.sum((start[:, None] >= region_end[None, :]).astype(I32), axis=1)
    tile_e = jnp.minimum(tile_e, tile_e[jnp.maximum(n_used - 1, 0)])
    flat = lambda a: a.reshape(-1).astype(I32)
    tables = (flat(loff), flat(seg), flat(gbase), flat(tail), loff.astype(F32).reshape(NB, N_EXP, 1))
    return tables, tile_e.astype(I32), n_used.reshape(1)


def kernel(x_prompt, x_sample, state_hgrn, c, c_ctx, w_ada, b_ada, norm1, w_in, hgrn_lb, hgrn_norm, w_pool,
           pool_scale, w_branch_a, w_branch_b, w_out, norm2, w_router, b_router, w_gate_up, b_gate_up,
           w_down, b_down, final_norm):
    x_all = jnp.concatenate([x_prompt.reshape(T_CTX, D_MODEL), x_sample.reshape(T_LAT, D_MODEL)], axis=0)
    cc = jnp.zeros((MOD_ROWS, D_MODEL), F32).at[:N_LAT_SEQ].set(c).at[N_LAT_SEQ].set(c_ctx)
    mod = _ada(cc, w_ada[0], b_ada)

    proj = _inproj(x_all, mod, norm1, w_in[0].astype(BF16))

    mall, masks = _hgrn_consts()
    o_f, o_b, new_state = _hgrn(proj, hgrn_lb, state_hgrn[:, 0], mall, masks)

    a_pool, cnt_pool = _pool_consts()
    yb = _pool(proj, a_pool, cnt_pool, w_pool[0].astype(BF16), pool_scale)

    wr_t = w_router[0].T
    wr_hi = wr_t.astype(BF16)
    wr_lo = (wr_t - wr_hi.astype(F32)).astype(BF16)
    tri = jnp.asarray(np.triu(np.ones((TB, TB), np.float32), 1), BF16)
    x1, h2, top_e, top_w, hist = _merge(
        x_all, o_f, o_b, proj, yb, mod, hgrn_norm, w_branch_a[0].astype(BF16), w_branch_b[0].astype(BF16),
        w_out[0].astype(BF16), norm2, wr_hi, wr_lo, b_router.reshape(N_EXP, 1))

    tables, tile_e, n_used = _dispatch_tables(hist)
    xs = _dispatch(tables, h2, top_e, tri)
    ys = _moe(tile_e, n_used, xs, w_gate_up[0], b_gate_up[0], w_down[0], b_down[0])
    y = _final(tables, ys, x1, top_e, top_w, tri, mod, final_norm.reshape(1, D_MODEL))
    y_prompt = y[:T_CTX].reshape(N_CTX_SEQ, CTX_LEN, D_MODEL)
    y_sample = y[T_CTX:].reshape(N_LAT_SEQ, LAT_LEN, D_MODEL)
    return y_prompt, y_sample, new_state[:, None]
```

```python
import functools

import numpy as np
import jax
import jax.numpy as jnp
from jax import lax
from jax.experimental import pallas as pl
from jax.experimental.pallas import tpu as pltpu

F32 = jnp.float32
BF16 = jnp.bfloat16
I32 = jnp.int32
U32 = jnp.uint32

D_MODEL = 1024
N_CTX_SEQ, CTX_LEN = 32, 256
N_LAT_SEQ, LAT_LEN = 4, 2048
T_CTX = N_CTX_SEQ * CTX_LEN
T_LAT = N_LAT_SEQ * LAT_LEN
T_ALL = T_CTX + T_LAT
TB = 256
NB = T_ALL // TB
NB_CTX = T_CTX // TB
LAT_BLOCKS = LAT_LEN // TB
HEADS, HEAD_K, HEAD_V = 4, 128, 128
HGRN_W = HEADS * HEAD_V
POOL_WINDOWS = (2, 4, 8, 16)
POOL_G = 128
POOL_W = len(POOL_WINDOWS) * POOL_G
GRID_W = 64
GRID_H = LAT_LEN // GRID_W
IN_W = 5 * HGRN_W + POOL_W + 2 * D_MODEL
N_EXP, TOP_K, D_FF = 32, 4, 1024
SWIGLU_LIMIT = 7.0
SWIGLU_ALPHA = 1.702
EPS = 1e-6
CHUNK = 64
N_LEVELS = 6
EXP_ROWS = (N_LEVELS + 2) * CHUNK
TM = 256
LROWS = TB * TOP_K + N_EXP * 8
SEG_BITS = 6
N_TILES = -(-(T_ALL * TOP_K + NB * N_EXP * 7 + N_EXP * (TM - 1)) // TM)
MOD_ROWS = 8
VMEM_LIMIT = 56 * 1024 * 1024


def _params(sem=("arbitrary",)):
    return pltpu.CompilerParams(dimension_semantics=sem, vmem_limit_bytes=VMEM_LIMIT)


def _dot(a, b):
    return jnp.dot(a, b, preferred_element_type=F32)


def _dot_nt(a, b):
    return lax.dot_general(a, b, (((1,), (1,)), ((), ())), preferred_element_type=F32)


def _dot_tn(a, b):
    return lax.dot_general(a, b, (((0,), (0,)), ((), ())), preferred_element_type=F32)


def _split2(x):
    hi = x.astype(BF16)
    lo = (x - hi.astype(F32)).astype(BF16)
    return hi, lo


def _mod_row(i):
    return jnp.where(i < NB_CTX, N_LAT_SEQ, (i - NB_CTX) // LAT_BLOCKS)


def _ada_kernel(c_ref, w_ref, b_ref, o_ref):
    c = c_ref[...]
    s = c * jax.nn.sigmoid(c)
    o_ref[...] = jnp.dot(s, w_ref[...], preferred_element_type=F32,
                         precision=lax.Precision.HIGHEST) + b_ref[...]


def _ada(cc, w_ada, b_ada):
    nblk = 1536
    return pl.pallas_call(
        _ada_kernel,
        out_shape=jax.ShapeDtypeStruct((MOD_ROWS, 6 * D_MODEL), F32),
        grid=(6 * D_MODEL // nblk,),
        in_specs=[pl.BlockSpec((MOD_ROWS, D_MODEL), lambda j: (0, 0)),
                  pl.BlockSpec((D_MODEL, nblk), lambda j: (0, j)),
                  pl.BlockSpec((1, nblk), lambda j: (0, j))],
        out_specs=pl.BlockSpec((MOD_ROWS, nblk), lambda j: (0, j)),
        compiler_params=_params(),
        name="ada",
    )(cc, w_ada, b_ada)


def _rms(x, g):
    ms = jnp.mean(x * x, axis=-1, keepdims=True)
    return x * lax.rsqrt(ms + EPS) * g


def _x_specs():
    return [pl.BlockSpec((TB, D_MODEL), lambda i, *_: (jnp.minimum(i, NB_CTX - 1), 0)),
            pl.BlockSpec((TB, D_MODEL), lambda i, *_: (jnp.maximum(i - NB_CTX, 0), 0))]


def _x_block(xc_ref, xl_ref):
    return jnp.where(pl.program_id(0) < NB_CTX, xc_ref[...], xl_ref[...])


def _inproj_kernel(xc_ref, xl_ref, mod_ref, n1_ref, w_ref, o_ref):
    row = _mod_row(pl.program_id(0))
    shift = mod_ref[pl.ds(row, 1), pl.ds(0, D_MODEL)]
    scale = mod_ref[pl.ds(row, 1), pl.ds(D_MODEL, D_MODEL)]
    h = _rms(_x_block(xc_ref, xl_ref), n1_ref[...]) * (1.0 + scale) + shift
    o_ref[...] = _dot(h.astype(BF16), w_ref[...])


def _inproj(x_ctx, x_lat, mod, norm1, w_in_bf):
    return pl.pallas_call(
        _inproj_kernel,
        out_shape=jax.ShapeDtypeStruct((T_ALL, IN_W), F32),
        grid=(NB,),
        in_specs=_x_specs() + [
                  pl.BlockSpec((MOD_ROWS, 6 * D_MODEL), lambda i: (0, 0)),
                  pl.BlockSpec((1, D_MODEL), lambda i: (0, 0)),
                  pl.BlockSpec((D_MODEL, IN_W), lambda i: (0, 0))],
        out_specs=pl.BlockSpec((TB, IN_W), lambda i: (i, 0)),
        compiler_params=_params(),
        name="inproj",
    )(x_ctx, x_lat, mod, norm1, w_in_bf)


def _hgrn_consts():
    c = CHUNK
    t = np.arange(c)[:, None]
    u = np.arange(c)[None, :]
    blocks = [u <= t]
    masks = [np.eye(c, dtype=bool)]
    h = c // 2
    while h >= 1:
        bi = t // h
        upper = (bi % 2) == 1
        e_up = (u >= bi * h) & (u <= t)
        e_lo = (u > t) & (u <= bi * h + h - 1)
        blocks.append(np.where(upper, e_up, e_lo))
        masks.append(((t // (2 * h)) == (u // (2 * h))) & (((t // h) % 2) == 1) & (((u // h) % 2) == 0))
        h //= 2
    blocks.append(u > t)
    m_f = np.stack(blocks).astype(np.float32)
    k_f = np.stack(masks).astype(np.float32)
    m_b = m_f[:, ::-1, ::-1]
    k_b = k_f[:, ::-1, ::-1]
    m = np.stack([m_f.reshape(EXP_ROWS, c), m_b.reshape(EXP_ROWS, c)])
    m3 = np.concatenate([m, m, m], axis=2)
    return jnp.asarray(m3, BF16), jnp.asarray(np.stack([k_f, k_b]), F32)


def _hgrn_dir(d, q_ref, f_ref, v_ref, o_ref, r0, lb, mall_ref, mask_ref, st_ref):
    c = CHUNK
    rows = pl.ds(r0, c)
    q = q_ref[rows, :] * (HEAD_K ** -0.5)
    v = v_ref[rows, :].astype(BF16)
    f = lb + (1.0 - lb) * jax.nn.sigmoid(f_ref[rows, :])
    g = jnp.log(f)
    k = 1.0 - f
    g1 = g.astype(BF16)
    r1 = g - g1.astype(F32)
    g2 = r1.astype(BF16)
    g3 = (r1 - g2.astype(F32)).astype(BF16)
    gsplit = jnp.concatenate([g1, g2, g3], axis=0)
    z = jnp.exp(_dot(mall_ref[d], gsplit))
    tot_row = c - 1 if d == 0 else 0
    for h in range(HEADS):
        sl = slice(h * HEAD_K, (h + 1) * HEAD_K)
        qh, kh, vh, zh = q[:, sl], k[:, sl], v[:, sl], z[:, sl]
        st = st_ref[d, h]
        o = _dot_nt((qh * zh[0:c]).astype(BF16), st.astype(BF16))
        scores = mask_ref[d, 0] * _dot_nt(qh.astype(BF16), kh.astype(BF16))
        for lev in range(N_LEVELS):
            zl = zh[(lev + 1) * c:(lev + 2) * c]
            scores = scores + mask_ref[d, lev + 1] * _dot_nt((qh * zl).astype(BF16), (kh * zl).astype(BF16))
        o = o + _dot(scores.astype(BF16), vh)
        kd = (kh * zh[(N_LEVELS + 1) * c:]).astype(BF16)
        st_ref[d, h] = st * zh[tot_row:tot_row + 1] + _dot_tn(vh, kd)
        o_ref[rows, sl] = o


def _hgrn_kernel(qf_ref, ff_ref, vf_ref, qb_ref, fb_ref, vb_ref, lbraw_ref, s0_ref, mall_ref, mask_ref,
                 of_ref, ob_ref, sout_hbm, st_ref, stage_ref, sem):
    i = pl.program_id(0)
    j = (i - NB_CTX) % LAT_BLOCKS
    is_ctx = i < NB_CTX

    @pl.when(is_ctx)
    def _():
        st_ref[...] = jnp.zeros_like(st_ref)

    @pl.when(jnp.logical_and(jnp.logical_not(is_ctx), j == 0))
    def _():
        for d in range(2):
            for h in range(HEADS):
                st_ref[d, h] = s0_ref[0, d, h].T

    a0 = lbraw_ref[0]
    a1 = lbraw_ref[1]
    mx = jnp.maximum(a0, a1)
    e0 = jnp.exp(a0 - mx)
    e1 = jnp.exp(a1 - mx)
    lb = e0 / (e0 + e1)

    nchunk = TB // CHUNK

    def body(ci, carry):
        rf = pl.multiple_of(ci * CHUNK, CHUNK)
        rb = pl.multiple_of((nchunk - 1 - ci) * CHUNK, CHUNK)
        _hgrn_dir(0, qf_ref, ff_ref, vf_ref, of_ref, rf, lb[0:1], mall_ref, mask_ref, st_ref)
        _hgrn_dir(1, qb_ref, fb_ref, vb_ref, ob_ref, rb, lb[1:2], mall_ref, mask_ref, st_ref)
        return carry

    lax.fori_loop(0, nchunk, body, 0)

    @pl.when(is_ctx)
    def _():
        for d in range(2):
            for h in range(HEADS):
                stage_ref[d, h] = st_ref[d, h].T
        cp = pltpu.make_async_copy(stage_ref, sout_hbm.at[i], sem)
        cp.start()
        cp.wait()


def _bwd_block(i):
    j = (i - NB_CTX) % LAT_BLOCKS
    return jnp.where(i < NB_CTX, i, i - j + (LAT_BLOCKS - 1 - j))


def _hgrn(proj, hgrn_lb, s0, mall, masks):
    nh = HGRN_W
    fwd = lambda col: pl.BlockSpec((TB, nh), lambda i: (i, col))
    bwd = lambda col: pl.BlockSpec((TB, nh), lambda i: (_bwd_block(i), col))
    lat_seq = lambda i: jnp.clip((i - NB_CTX) // LAT_BLOCKS, 0, N_LAT_SEQ - 1)
    return pl.pallas_call(
        _hgrn_kernel,
        out_shape=(jax.ShapeDtypeStruct((T_ALL, nh), F32),
                   jax.ShapeDtypeStruct((T_ALL, nh), F32),
                   jax.ShapeDtypeStruct((N_CTX_SEQ, 2, HEADS, HEAD_K, HEAD_V), F32)),
        grid=(NB,),
        in_specs=[fwd(0), fwd(1), fwd(3), bwd(0), bwd(2), bwd(3),
                  pl.BlockSpec((2, 2, nh), lambda i: (0, 0, 0)),
                  pl.BlockSpec((1, 2, HEADS, HEAD_K, HEAD_V), lambda i: (lat_seq(i), 0, 0, 0, 0)),
                  pl.BlockSpec((2, EXP_ROWS, 3 * CHUNK), lambda i: (0, 0, 0)),
                  pl.BlockSpec((2, N_LEVELS + 1, CHUNK, CHUNK), lambda i: (0, 0, 0, 0))],
        out_specs=(pl.BlockSpec((TB, nh), lambda i: (i, 0)),
                   pl.BlockSpec((TB, nh), lambda i: (_bwd_block(i), 0)),
                   pl.BlockSpec(memory_space=pl.ANY)),
        scratch_shapes=[pltpu.VMEM((2, HEADS, HEAD_V, HEAD_K), F32),
                        pltpu.VMEM((2, HEADS, HEAD_K, HEAD_V), F32),
                        pltpu.SemaphoreType.DMA],
        compiler_params=_params(),
        name="hgrn",
    )(proj, proj, proj, proj, proj, proj, hgrn_lb, s0, mall, masks)


def _window_bounds(n, w):
    pos = np.arange(n)
    lo = np.clip(pos - w // 2, 0, n - 1)
    hi = np.clip(pos - w // 2 + w - 1, 0, n - 1)
    return lo, hi


def _pool_consts():
    seq, img, cnt_seq, cnt_col = [], [], [], []
    for w in POOL_WINDOWS:
        lo, hi = _window_bounds(CTX_LEN, w)
        u = np.arange(CTX_LEN)[None, :]
        seq.append((u >= lo[:, None]) & (u <= hi[:, None]))
        cnt_seq.append(hi - lo + 1)
        lo, hi = _window_bounds(GRID_W, w)
        u = np.arange(GRID_W)[None, :]
        band = (u >= lo[:, None]) & (u <= hi[:, None])
        img.append(np.kron(np.eye(TB // GRID_W, dtype=bool), band))
        cnt_col.append(np.tile(hi - lo + 1, TB // GRID_W))
    a = np.stack([np.stack(seq), np.stack(img)]).astype(np.float32)
    cnt = np.stack([np.stack(cnt_seq), np.stack(cnt_col)]).astype(np.float32)
    cnt = np.broadcast_to(cnt[..., None], cnt.shape + (POOL_G,))
    return jnp.asarray(a, BF16), jnp.asarray(cnt, F32)


POOL_ROWS = LAT_LEN


def _pool_kernel(u_ref, a_ref, cnt_ref, wp_ref, ps_ref, o_ref, cp_ref):
    i = pl.program_id(0)
    nblk = POOL_ROWS // TB

    def finish(g, r0, nrows, pm):
        sl = slice(g * POOL_G, (g + 1) * POOL_G)
        d = pm - u_ref[pl.ds(r0, nrows), sl]
        y = _dot(d.astype(BF16), wp_ref[g]) * ps_ref[:, sl]
        o_ref[pl.ds(r0, nrows), sl] = y.astype(o_ref.dtype)

    def window_sum(kind, g, b):
        sl = slice(g * POOL_G, (g + 1) * POOL_G)
        hi, lo = _split2(u_ref[pl.ds(b * TB, TB), sl])
        a = a_ref[kind, g]
        return (_dot(a, hi) + _dot(a, lo)) / cnt_ref[kind, g]

    @pl.when(i < T_CTX // POOL_ROWS)
    def _():
        for g in range(len(POOL_WINDOWS)):
            for b in range(nblk):
                finish(g, b * TB, TB, window_sum(0, g, b))

    @pl.when(i >= T_CTX // POOL_ROWS)
    def _():
        for g, w in enumerate(POOL_WINDOWS):
            for b in range(nblk):
                cp_ref[pl.ds(b * TB, TB), :] = window_sum(1, g, b)
            lo, hi = _window_bounds(GRID_H, w)
            for r in range(GRID_H):
                acc = cp_ref[pl.ds(int(lo[r]) * GRID_W, GRID_W), :]
                for rr in range(int(lo[r]) + 1, int(hi[r]) + 1):
                    acc = acc + cp_ref[pl.ds(rr * GRID_W, GRID_W), :]
                finish(g, r * GRID_W, GRID_W, acc / float(hi[r] - lo[r] + 1))


def _pool(proj, a_pool, cnt_pool, w_pool_bf, pool_scale):
    col = (5 * HGRN_W) // POOL_W
    return pl.pallas_call(
        _pool_kernel,
        out_shape=jax.ShapeDtypeStruct((T_ALL, POOL_W), BF16),
        grid=(T_ALL // POOL_ROWS,),
        in_specs=[pl.BlockSpec((POOL_ROWS, POOL_W), lambda i: (i, col)),
                  pl.BlockSpec((2, 4, TB, TB), lambda i: (0, 0, 0, 0)),
                  pl.BlockSpec((2, 4, TB, POOL_G), lambda i: (0, 0, 0, 0)),
                  pl.BlockSpec((4, POOL_G, POOL_G), lambda i: (0, 0, 0)),
                  pl.BlockSpec((1, POOL_W), lambda i: (0, 0))],
        out_specs=pl.BlockSpec((POOL_ROWS, POOL_W), lambda i: (i, 0)),
        scratch_shapes=[pltpu.VMEM((POOL_ROWS, POOL_G), F32)],
        compiler_params=_params(),
        name="pool",
    )(proj, a_pool, cnt_pool, w_pool_bf, pool_scale)


def _merge_kernel(xc_ref, xl_ref, of_ref, ob_ref, og_ref, yb_ref, ga_ref, gb_ref, mod_ref, hn_ref, wa_ref, wb_ref,
                  wo_ref, n2_ref, wrh_ref, wrl_ref, br_ref,
                  x1_ref, h2_ref, te_ref, tw_ref, hist_ref):
    row = _mod_row(pl.program_id(0))
    gate1 = mod_ref[pl.ds(row, 1), pl.ds(2 * D_MODEL, D_MODEL)]
    shift2 = mod_ref[pl.ds(row, 1), pl.ds(3 * D_MODEL, D_MODEL)]
    scale2 = mod_ref[pl.ds(row, 1), pl.ds(4 * D_MODEL, D_MODEL)]

    o = of_ref[...] + ob_ref[...]
    og = og_ref[...]
    ya = jnp.concatenate(
        [_rms(o[:, h * HEAD_V:(h + 1) * HEAD_V], hn_ref[...]) for h in range(HEADS)], axis=1)
    ya = ya * (og * jax.nn.sigmoid(og))
    merged = (jax.nn.sigmoid(ga_ref[...]) * _dot(ya.astype(BF16), wa_ref[...])
              + jax.nn.sigmoid(gb_ref[...]) * _dot(yb_ref[...], wb_ref[...]))
    x1 = _x_block(xc_ref, xl_ref) + gate1 * _dot(merged.astype(BF16), wo_ref[...])
    x1_ref[...] = x1
    h2 = _rms(x1, n2_ref[...]) * (1.0 + scale2) + shift2
    hh, hl = _split2(h2)
    h2_ref[...] = hh

    lt = _dot_nt(wrh_ref[...], hh) + _dot_nt(wrl_ref[...], hh) + _dot_nt(wrh_ref[...], hl) + br_ref[...]
    eidx = lax.broadcasted_iota(I32, (N_EXP, TB), 0)
    vals, idxs, cnt = [], [], jnp.zeros((N_EXP, TB), F32)
    for _ in range(TOP_K):
        m = jnp.max(lt, axis=0, keepdims=True)
        idx = jnp.min(jnp.where(lt == m, eidx, N_EXP), axis=0, keepdims=True)
        sel = eidx == idx
        vals.append(m)
        idxs.append(idx)
        cnt = cnt + sel.astype(F32)
        lt = jnp.where(sel, -jnp.inf, lt)
    ex = [jnp.exp(v - vals[0]) for v in vals]
    den = ex[0] + ex[1] + ex[2] + ex[3]
    tw_ref[0] = jnp.concatenate([e / den for e in ex], axis=0)
    te_ref[0] = jnp.concatenate(idxs, axis=0)
    hist_ref[0] = jnp.sum(cnt, axis=1, keepdims=True).astype(I32)


def _merge(x_ctx, x_lat, o_f, o_b, proj, yb, mod, hgrn_norm, wa_bf, wb_bf, wo_bf, norm2, wr_hi, wr_lo, b_router):
    full = lambda shape: pl.BlockSpec(shape, lambda i: (0,) * len(shape))
    return pl.pallas_call(
        _merge_kernel,
        out_shape=(jax.ShapeDtypeStruct((T_ALL, D_MODEL), F32),
                   jax.ShapeDtypeStruct((T_ALL, D_MODEL), BF16),
                   jax.ShapeDtypeStruct((NB, TOP_K, TB), I32),
                   jax.ShapeDtypeStruct((NB, TOP_K, TB), F32),
                   jax.ShapeDtypeStruct((NB, N_EXP, 1), I32)),
        grid=(NB,),
        in_specs=_x_specs() + [
                  pl.BlockSpec((TB, HGRN_W), lambda i: (i, 0)),
                  pl.BlockSpec((TB, HGRN_W), lambda i: (i, 0)),
                  pl.BlockSpec((TB, HGRN_W), lambda i: (i, 4)),
                  pl.BlockSpec((TB, POOL_W), lambda i: (i, 0)),
                  pl.BlockSpec((TB, D_MODEL), lambda i: (i, 3)),
                  pl.BlockSpec((TB, D_MODEL), lambda i: (i, 4)),
                  full((MOD_ROWS, 6 * D_MODEL)),
                  full((1, HEAD_V)),
                  full((HGRN_W, D_MODEL)),
                  full((POOL_W, D_MODEL)),
                  full((D_MODEL, D_MODEL)),
                  full((1, D_MODEL)),
                  full((N_EXP, D_MODEL)),
                  full((N_EXP, D_MODEL)),
                  full((N_EXP, 1))],
        out_specs=(pl.BlockSpec((TB, D_MODEL), lambda i: (i, 0)),
                   pl.BlockSpec((TB, D_MODEL), lambda i: (i, 0)),
                   pl.BlockSpec((1, TOP_K, TB), lambda i: (i, 0, 0)),
                   pl.BlockSpec((1, TOP_K, TB), lambda i: (i, 0, 0)),
                   pl.BlockSpec((1, N_EXP, 1), lambda i: (i, 0, 0))),
        compiler_params=_params(),
        name="merge",
    )(x_ctx, x_lat, o_f, o_b, proj, yb, proj, proj, mod, hgrn_norm, wa_bf, wb_bf, wo_bf, norm2,
      wr_hi, wr_lo, b_router)


def _local_rows(te_ref, loff_ref, tri_ref):
    te = te_ref[0]
    eidx = lax.broadcasted_iota(I32, (N_EXP, TB), 0)
    sels = [eidx == te[k:k + 1] for k in range(TOP_K)]
    cnt = sels[0].astype(F32)
    for s in sels[1:]:
        cnt = cnt + s.astype(F32)
    base = _dot(cnt.astype(BF16), tri_ref[...]) + loff_ref[0]
    return [jnp.sum(jnp.where(s, base, 0.0), axis=0, keepdims=True) for s in sels]


def _segment_copies(make_copy, local_off, global_off, units):
    for j in range(SEG_BITS):
        low = (units & ((1 << j) - 1)) * 8

        @pl.when(((units >> j) & 1) == 1)
        def _():
            make_copy(pl.multiple_of(local_off + low, 8), pl.multiple_of(global_off + low, 8), 8 << j).start()


def _pack_pairs(x):
    half = D_MODEL // 2
    lo = lax.bitcast_convert_type(x[:, :half], U32) >> 16
    hi = lax.bitcast_convert_type(x[:, half:], U32) & jnp.uint32(0xFFFF0000)
    return hi | lo


def _unpack_pairs(p):
    lo = lax.bitcast_convert_type(p << 16, F32).astype(BF16)
    hi = lax.bitcast_convert_type(p & jnp.uint32(0xFFFF0000), F32).astype(BF16)
    return jnp.concatenate([lo, hi], axis=1)


def _block_rows(loff_s, seg_s, b):
    last = b * N_EXP + N_EXP - 1
    return pl.multiple_of(loff_s[last] + seg_s[last], 8)


def _dispatch_kernel(loff_s, seg_s, gbase_s, tail_s, h2_ref, te_ref, loffv_ref, tri_ref, xs_hbm,
                     loc, zeros, sem, sem_z):
    b = pl.program_id(0)
    slot = b % 2

    def wait_block(blk, s):
        n = _block_rows(loff_s, seg_s, blk)
        pltpu.make_async_copy(loc.at[s, pl.ds(0, n)], xs_hbm.at[pl.ds(0, n)], sem.at[s]).wait()

    lrow = _local_rows(te_ref, loffv_ref, tri_ref)
    piota = lax.broadcasted_iota(I32, (LROWS, TB), 0).astype(F32)
    hit = piota == lrow[0]
    for k in range(1, TOP_K):
        hit = jnp.logical_or(hit, piota == lrow[k])
    packed = _pack_pairs(_dot(jnp.where(hit, 1.0, 0.0).astype(BF16), h2_ref[...]))

    @pl.when(b >= 2)
    def _():
        wait_block(b - 2, slot)

    loc[slot] = packed

    def out_copy(a, g, size):
        return pltpu.make_async_copy(loc.at[slot, pl.ds(a, size)], xs_hbm.at[pl.ds(g, size)], sem.at[slot])

    def body(e, carry):
        idx = b * N_EXP + e
        _segment_copies(out_copy, loff_s[idx], gbase_s[idx], seg_s[idx] // 8)
        return carry

    lax.fori_loop(0, N_EXP, body, 0)

    @pl.when(b == NB - 1)
    def _():
        zeros[...] = jnp.zeros_like(zeros)

        def zero_copy(a, g, size):
            return pltpu.make_async_copy(zeros.at[pl.ds(a, size)], xs_hbm.at[pl.ds(g, size)], sem_z)

        def zbody(e, ztot):
            _segment_copies(zero_copy, 0, tail_s[e], tail_s[N_EXP + e] // 8)
            return ztot + tail_s[N_EXP + e]

        def tbody(t, carry):
            pltpu.make_async_copy(zeros, xs_hbm.at[pl.ds(pl.multiple_of(t * TM, TM), TM)], sem_z).start()
            return carry

        n_used = tail_s[2 * N_EXP]
        lax.fori_loop(n_used, N_TILES, tbody, 0)
        ztot = lax.fori_loop(0, N_EXP, zbody, 0) + (N_TILES - n_used) * TM
        ztot = pl.multiple_of(ztot, 8)

        @pl.when(ztot > 0)
        def _():
            pltpu.make_async_copy(xs_hbm.at[pl.ds(0, ztot)], xs_hbm.at[pl.ds(0, ztot)], sem_z).wait()

        wait_block(b - 1, 1 - slot)
        wait_block(b, slot)


def _dispatch(tables, h2, top_e, tri):
    loff_s, seg_s, gbase_s, tail_s, loff_v = tables
    grid_spec = pltpu.PrefetchScalarGridSpec(
        num_scalar_prefetch=4,
        grid=(NB,),
        in_specs=[pl.BlockSpec((TB, D_MODEL), lambda i, *_: (i, 0)),
                  pl.BlockSpec((1, TOP_K, TB), lambda i, *_: (i, 0, 0)),
                  pl.BlockSpec((1, N_EXP, 1), lambda i, *_: (i, 0, 0)),
                  pl.BlockSpec((TB, TB), lambda i, *_: (0, 0))],
        out_specs=pl.BlockSpec(memory_space=pl.ANY),
        scratch_shapes=[pltpu.VMEM((2, LROWS, D_MODEL // 2), U32),
                        pltpu.VMEM((TM, D_MODEL // 2), U32),
                        pltpu.SemaphoreType.DMA((2,)),
                        pltpu.SemaphoreType.DMA])
    return pl.pallas_call(
        _dispatch_kernel,
        out_shape=jax.ShapeDtypeStruct((N_TILES * TM, D_MODEL // 2), U32),
        grid_spec=grid_spec,
        compiler_params=_params(),
        name="dispatch",
    )(loff_s, seg_s, gbase_s, tail_s, h2, top_e, loff_v, tri)


def _moe_kernel(te_ref, first_ref, par_ref, next_ref, nu_ref, xs_ref, bgu_ref, bd_ref, wgu_hbm, wd_hbm, o_ref,
                wgu_st, wd_st, wgu_bf, wd_bf, sem):
    i = pl.program_id(0)

    def fetch(e, s):
        return (pltpu.make_async_copy(wgu_hbm.at[e], wgu_st.at[s], sem.at[0, s]),
                pltpu.make_async_copy(wd_hbm.at[e], wd_st.at[s], sem.at[1, s]))

    @pl.when(i < nu_ref[0])
    def _():
        @pl.when(first_ref[i] == 1)
        def _():
            s = par_ref[i]

            @pl.when(i == 0)
            def _():
                for cp in fetch(te_ref[0], 0):
                    cp.start()

            for cp in fetch(te_ref[i], s):
                cp.wait()

            @pl.when(next_ref[i] >= 0)
            def _():
                for cp in fetch(next_ref[i], 1 - s):
                    cp.start()

            wgu_bf[...] = wgu_st[s].astype(BF16)
            wd_bf[...] = wd_st[s].astype(BF16)

        gu = _dot(_unpack_pairs(xs_ref[...]), wgu_bf[...]) + bgu_ref[0]
        gate = jnp.minimum(gu[:, :D_FF], SWIGLU_LIMIT)
        up = jnp.clip(gu[:, D_FF:], -SWIGLU_LIMIT, SWIGLU_LIMIT)
        act = (up + 1.0) * gate * jax.nn.sigmoid(SWIGLU_ALPHA * gate)
        out = _dot(act.astype(BF16), wd_bf[...]) + bd_ref[0]
        o_ref[...] = _pack_pairs(out.astype(BF16).astype(F32))

    @pl.when(i >= nu_ref[0])
    def _():
        o_ref[...] = jnp.zeros_like(o_ref)


def _moe(tile_tables, xs, w_gate_up, b_gate_up, w_down, b_down):
    nsp = len(tile_tables)
    row_tile = lambda i, *s: (jnp.minimum(i, s[nsp - 1][0] - 1), 0)
    grid_spec = pltpu.PrefetchScalarGridSpec(
        num_scalar_prefetch=nsp,
        grid=(N_TILES,),
        in_specs=[pl.BlockSpec((TM, D_MODEL // 2), row_tile),
                  pl.BlockSpec((1, 1, 2 * D_FF), lambda i, te, *_: (te[i], 0, 0)),
                  pl.BlockSpec((1, 1, D_MODEL), lambda i, te, *_: (te[i], 0, 0)),
                  pl.BlockSpec(memory_space=pl.ANY),
                  pl.BlockSpec(memory_space=pl.ANY)],
        out_specs=pl.BlockSpec((TM, D_MODEL // 2), lambda i, *_: (i, 0)),
        scratch_shapes=[pltpu.VMEM((2, D_MODEL, 2 * D_FF), F32),
                        pltpu.VMEM((2, D_FF, D_MODEL), F32),
                        pltpu.VMEM((D_MODEL, 2 * D_FF), BF16),
                        pltpu.VMEM((D_FF, D_MODEL), BF16),
                        pltpu.SemaphoreType.DMA((2, 2))])
    return pl.pallas_call(
        _moe_kernel,
        out_shape=jax.ShapeDtypeStruct((N_TILES * TM, D_MODEL // 2), U32),
        grid_spec=grid_spec,
        compiler_params=_params(),
        name="moe",
    )(*tile_tables, xs, b_gate_up.reshape(N_EXP, 1, 2 * D_FF), b_down.reshape(N_EXP, 1, D_MODEL),
      w_gate_up, w_down)


def _final_kernel(loff_s, seg_s, gbase_s, ys_hbm, x1_ref, te_ref, tw_ref, loffv_ref, tri_ref, mod_ref, fn_ref,
                  oc_ref, ol_ref, loc, sem):
    b = pl.program_id(0)
    slot = b % 2

    def start_block(blk, s):
        def in_copy(a, g, size):
            return pltpu.make_async_copy(ys_hbm.at[pl.ds(g, size)], loc.at[s, pl.ds(a, size)], sem.at[s])

        def body(e, carry):
            idx = blk * N_EXP + e
            _segment_copies(in_copy, loff_s[idx], gbase_s[idx], seg_s[idx] // 8)
            return carry

        lax.fori_loop(0, N_EXP, body, 0)

    @pl.when(b == 0)
    def _():
        loc[...] = jnp.zeros_like(loc)
        start_block(0, 0)

    @pl.when(b + 1 < NB)
    def _():
        start_block(b + 1, 1 - slot)

    lrow = _local_rows(te_ref, loffv_ref, tri_ref)
    tw = tw_ref[0]
    piota = lax.broadcasted_iota(I32, (LROWS, TB), 0).astype(F32)
    pw = jnp.where(piota == lrow[0], tw[0:1], 0.0)
    for k in range(1, TOP_K):
        pw = pw + jnp.where(piota == lrow[k], tw[k:k + 1], 0.0)

    n = _block_rows(loff_s, seg_s, b)
    pltpu.make_async_copy(ys_hbm.at[pl.ds(0, n)], loc.at[slot, pl.ds(0, n)], sem.at[slot]).wait()
    y = _dot_tn(pw.astype(BF16), _unpack_pairs(loc[slot]))

    row = _mod_row(b)
    gate2 = mod_ref[pl.ds(row, 1), pl.ds(5 * D_MODEL, D_MODEL)]
    out = _rms(x1_ref[...] + gate2 * y, fn_ref[...])

    @pl.when(b < NB_CTX)
    def _():
        oc_ref[...] = out

    @pl.when(b >= NB_CTX)
    def _():
        ol_ref[...] = out


def _final(tables, ys, x1, top_e, top_w, tri, mod, final_norm):
    loff_s, seg_s, gbase_s, _, loff_v = tables
    grid_spec = pltpu.PrefetchScalarGridSpec(
        num_scalar_prefetch=3,
        grid=(NB,),
        in_specs=[pl.BlockSpec(memory_space=pl.ANY),
                  pl.BlockSpec((TB, D_MODEL), lambda i, *_: (i, 0)),
                  pl.BlockSpec((1, TOP_K, TB), lambda i, *_: (i, 0, 0)),
                  pl.BlockSpec((1, TOP_K, TB), lambda i, *_: (i, 0, 0)),
                  pl.BlockSpec((1, N_EXP, 1), lambda i, *_: (i, 0, 0)),
                  pl.BlockSpec((TB, TB), lambda i, *_: (0, 0)),
                  pl.BlockSpec((MOD_ROWS, 6 * D_MODEL), lambda i, *_: (0, 0)),
                  pl.BlockSpec((1, D_MODEL), lambda i, *_: (0, 0))],
        out_specs=tuple(_x_specs()),
        scratch_shapes=[pltpu.VMEM((2, LROWS, D_MODEL // 2), U32),
                        pltpu.SemaphoreType.DMA((2,))])
    return pl.pallas_call(
        _final_kernel,
        out_shape=(jax.ShapeDtypeStruct((T_CTX, D_MODEL), F32),
                   jax.ShapeDtypeStruct((T_LAT, D_MODEL), F32)),
        grid_spec=grid_spec,
        compiler_params=_params(),
        name="final",
    )(loff_s, seg_s, gbase_s, ys, x1, top_e, top_w, loff_v, tri, mod, final_norm)


def _dispatch_tables(hist):
    hist = hist.reshape(NB, N_EXP)
    seg = ((hist + 7) // 8) * 8
    loff = jnp.cumsum(seg, axis=1) - seg
    rows_e = jnp.sum(seg, axis=0)
    region = ((rows_e + TM - 1) // TM) * TM
    region_end = jnp.cumsum(region)
    region_start = region_end - region
    gbase = region_start[None, :] + jnp.cumsum(seg, axis=0) - seg
    n_used = (region_end[-1] // TM).astype(I32)
    tail = jnp.concatenate([region_start + rows_e, region - rows_e, n_used.reshape(1)])
    start = jnp.arange(N_TILES, dtype=I32) * TM
    tile_e = jnp.sum((start[:, None] >= region_end[None, :]).astype(I32), axis=1)
    tile_e = jnp.minimum(tile_e, tile_e[jnp.maximum(n_used - 1, 0)])
    first = jnp.concatenate([jnp.ones((1,), I32), (tile_e[1:] != tile_e[:-1]).astype(I32)])
    parity = (jnp.cumsum(first) - 1) % 2
    later = jnp.where(tile_e[None, :] > tile_e[:, None], tile_e[None, :], N_EXP)
    nxt = jnp.min(later, axis=1)
    nxt = jnp.where(nxt == N_EXP, -1, nxt)
    flat = lambda a: a.reshape(-1).astype(I32)
    tables = (flat(loff), flat(seg), flat(gbase), flat(tail), loff.astype(F32).reshape(NB, N_EXP, 1))
    tile_tables = (flat(tile_e), flat(first), flat(parity), flat(nxt), n_used.reshape(1))
    return tables, tile_tables


def kernel(x_prompt, x_sample, state_hgrn, c, c_ctx, w_ada, b_ada, norm1, w_in, hgrn_lb, hgrn_norm, w_pool,
           pool_scale, w_branch_a, w_branch_b, w_out, norm2, w_router, b_router, w_gate_up, b_gate_up,
           w_down, b_down, final_norm):
    x_ctx = x_prompt.reshape(T_CTX, D_MODEL)
    x_lat = x_sample.reshape(T_LAT, D_MODEL)
    cc = jnp.zeros((MOD_ROWS, D_MODEL), F32).at[:N_LAT_SEQ].set(c).at[N_LAT_SEQ].set(c_ctx)
    mod = _ada(cc, w_ada[0], b_ada)

    proj = _inproj(x_ctx, x_lat, mod, norm1, w_in[0].astype(BF16))

    mall, masks = _hgrn_consts()
    o_f, o_b, new_state = _hgrn(proj, hgrn_lb, state_hgrn[:, 0], mall, masks)

    a_pool, cnt_pool = _pool_consts()
    yb = _pool(proj, a_pool, cnt_pool, w_pool[0].astype(BF16), pool_scale)

    wr_t = w_router[0].T
    wr_hi = wr_t.astype(BF16)
    wr_lo = (wr_t - wr_hi.astype(F32)).astype(BF16)
    tri = jnp.asarray(np.triu(np.ones((TB, TB), np.float32), 1), BF16)
    x1, h2, top_e, top_w, hist = _merge(
        x_ctx, x_lat, o_f, o_b, proj, yb, mod, hgrn_norm, w_branch_a[0].astype(BF16), w_branch_b[0].astype(BF16),
        w_out[0].astype(BF16), norm2, wr_hi, wr_lo, b_router.reshape(N_EXP, 1))

    tables, tile_tables = _dispatch_tables(hist)
    xs = _dispatch(tables, h2, top_e, tri)
    ys = _moe(tile_tables, xs, w_gate_up[0], b_gate_up[0], w_down[0], b_down[0])
    y_ctx, y_lat = _final(tables, ys, x1, top_e, top_w, tri, mod, final_norm.reshape(1, D_MODEL))
    y_prompt = y_ctx.reshape(N_CTX_SEQ, CTX_LEN, D_MODEL)
    y_sample = y_lat.reshape(N_LAT_SEQ, LAT_LEN, D_MODEL)
    return y_prompt, y_sample, new_state[:, None]
```

```python
import functools

import numpy as np
import jax
import jax.numpy as jnp
from jax import lax
from jax.experimental import pallas as pl
from jax.experimental.pallas import tpu as pltpu

F32 = jnp.float32
BF16 = jnp.bfloat16
I32 = jnp.int32
U32 = jnp.uint32

D_MODEL = 1024
N_CTX_SEQ, CTX_LEN = 32, 256
N_LAT_SEQ, LAT_LEN = 4, 2048
T_CTX = N_CTX_SEQ * CTX_LEN
T_LAT = N_LAT_SEQ * LAT_LEN
T_ALL = T_CTX + T_LAT
TB = 256
NB = T_ALL // TB
NB_CTX = T_CTX // TB
LAT_BLOCKS = LAT_LEN // TB
HEADS, HEAD_K, HEAD_V = 4, 128, 128
HGRN_W = HEADS * HEAD_V
POOL_WINDOWS = (2, 4, 8, 16)
POOL_G = 128
POOL_W = len(POOL_WINDOWS) * POOL_G
GRID_W = 64
GRID_H = LAT_LEN // GRID_W
IN_W = 5 * HGRN_W + POOL_W + 2 * D_MODEL
N_EXP, TOP_K, D_FF = 32, 4, 1024
SWIGLU_LIMIT = 7.0
SWIGLU_ALPHA = 1.702
EPS = 1e-6
CHUNK = 64
N_LEVELS = 6
EXP_ROWS = (N_LEVELS + 2) * CHUNK
TM = 512
TBD = 512
NBD = T_ALL // TBD
NBD_CTX = T_CTX // TBD
SORT_CHUNK = 256
LROWS = TBD * TOP_K + N_EXP * 8
SEG_BITS = 7
N_TILES = -(-(T_ALL * TOP_K + NBD * N_EXP * 7 + N_EXP * (TM - 1)) // TM)
MOD_ROWS = 8
VMEM_LIMIT = 56 * 1024 * 1024


def _params(sem=("arbitrary",)):
    return pltpu.CompilerParams(dimension_semantics=sem, vmem_limit_bytes=VMEM_LIMIT)


def _dot(a, b):
    return jnp.dot(a, b, preferred_element_type=F32)


def _dot_nt(a, b):
    return lax.dot_general(a, b, (((1,), (1,)), ((), ())), preferred_element_type=F32)


def _dot_tn(a, b):
    return lax.dot_general(a, b, (((0,), (0,)), ((), ())), preferred_element_type=F32)


def _split2(x):
    hi = x.astype(BF16)
    lo = (x - hi.astype(F32)).astype(BF16)
    return hi, lo


def _mod_row(i):
    return jnp.where(i < NB_CTX, N_LAT_SEQ, (i - NB_CTX) // LAT_BLOCKS)


def _ada_kernel(c_ref, w_ref, b_ref, o_ref):
    c = c_ref[...]
    s = c * jax.nn.sigmoid(c)
    o_ref[...] = jnp.dot(s, w_ref[...], preferred_element_type=F32,
                         precision=lax.Precision.HIGHEST) + b_ref[...]


def _ada(cc, w_ada, b_ada):
    nblk = 1536
    return pl.pallas_call(
        _ada_kernel,
        out_shape=jax.ShapeDtypeStruct((MOD_ROWS, 6 * D_MODEL), F32),
        grid=(6 * D_MODEL // nblk,),
        in_specs=[pl.BlockSpec((MOD_ROWS, D_MODEL), lambda j: (0, 0)),
                  pl.BlockSpec((D_MODEL, nblk), lambda j: (0, j)),
                  pl.BlockSpec((1, nblk), lambda j: (0, j))],
        out_specs=pl.BlockSpec((MOD_ROWS, nblk), lambda j: (0, j)),
        compiler_params=_params(),
        name="ada",
    )(cc, w_ada, b_ada)


def _rms(x, g):
    ms = jnp.mean(x * x, axis=-1, keepdims=True)
    return x * lax.rsqrt(ms + EPS) * g


def _x_specs():
    return [pl.BlockSpec((TB, D_MODEL), lambda i, *_: (jnp.minimum(i, NB_CTX - 1), 0)),
            pl.BlockSpec((TB, D_MODEL), lambda i, *_: (jnp.maximum(i - NB_CTX, 0), 0))]


def _x_block(xc_ref, xl_ref):
    return jnp.where(pl.program_id(0) < NB_CTX, xc_ref[...], xl_ref[...])


def _inproj_kernel(xc_ref, xl_ref, mod_ref, n1_ref, w_ref, o_ref):
    row = _mod_row(pl.program_id(0))
    shift = mod_ref[pl.ds(row, 1), pl.ds(0, D_MODEL)]
    scale = mod_ref[pl.ds(row, 1), pl.ds(D_MODEL, D_MODEL)]
    h = _rms(_x_block(xc_ref, xl_ref), n1_ref[...]) * (1.0 + scale) + shift
    o_ref[...] = _dot(h.astype(BF16), w_ref[...])


def _inproj(x_ctx, x_lat, mod, norm1, w_in_bf):
    return pl.pallas_call(
        _inproj_kernel,
        out_shape=jax.ShapeDtypeStruct((T_ALL, IN_W), F32),
        grid=(NB,),
        in_specs=_x_specs() + [
                  pl.BlockSpec((MOD_ROWS, 6 * D_MODEL), lambda i: (0, 0)),
                  pl.BlockSpec((1, D_MODEL), lambda i: (0, 0)),
                  pl.BlockSpec((D_MODEL, IN_W), lambda i: (0, 0))],
        out_specs=pl.BlockSpec((TB, IN_W), lambda i: (i, 0)),
        compiler_params=_params(),
        name="inproj",
    )(x_ctx, x_lat, mod, norm1, w_in_bf)


def _hgrn_consts():
    c = CHUNK
    t = np.arange(c)[:, None]
    u = np.arange(c)[None, :]
    blocks = [u <= t]
    masks = [np.eye(c, dtype=bool)]
    h = c // 2
    while h >= 1:
        bi = t // h
        upper = (bi % 2) == 1
        e_up = (u >= bi * h) & (u <= t)
        e_lo = (u > t) & (u <= bi * h + h - 1)
        blocks.append(np.where(upper, e_up, e_lo))
        masks.append(((t // (2 * h)) == (u // (2 * h))) & (((t // h) % 2) == 1) & (((u // h) % 2) == 0))
        h //= 2
    blocks.append(u > t)
    m_f = np.stack(blocks).astype(np.float32)
    k_f = np.stack(masks).astype(np.float32)
    m_b = m_f[:, ::-1, ::-1]
    k_b = k_f[:, ::-1, ::-1]
    m = np.stack([m_f.reshape(EXP_ROWS, c), m_b.reshape(EXP_ROWS, c)])
    m3 = np.concatenate([m, m, m], axis=2)
    return jnp.asarray(m3, BF16), jnp.asarray(np.stack([k_f, k_b]), F32)


def _hgrn_dir(d, q_ref, f_ref, v_ref, o_ref, r0, lb, mall_ref, mask_ref, st_ref):
    c = CHUNK
    rows = pl.ds(r0, c)
    q = q_ref[rows, :] * (HEAD_K ** -0.5)
    v = v_ref[rows, :].astype(BF16)
    f = lb + (1.0 - lb) * jax.nn.sigmoid(f_ref[rows, :])
    g = jnp.log(f)
    k = 1.0 - f
    g1 = g.astype(BF16)
    r1 = g - g1.astype(F32)
    g2 = r1.astype(BF16)
    g3 = (r1 - g2.astype(F32)).astype(BF16)
    gsplit = jnp.concatenate([g1, g2, g3], axis=0)
    z = jnp.exp(_dot(mall_ref[d], gsplit))
    tot_row = c - 1 if d == 0 else 0
    for h in range(HEADS):
        sl = slice(h * HEAD_K, (h + 1) * HEAD_K)
        qh, kh, vh, zh = q[:, sl], k[:, sl], v[:, sl], z[:, sl]
        st = st_ref[d, h]
        o = _dot_nt((qh * zh[0:c]).astype(BF16), st.astype(BF16))
        scores = mask_ref[d, 0] * _dot_nt(qh.astype(BF16), kh.astype(BF16))
        for lev in range(N_LEVELS):
            zl = zh[(lev + 1) * c:(lev + 2) * c]
            scores = scores + mask_ref[d, lev + 1] * _dot_nt((qh * zl).astype(BF16), (kh * zl).astype(BF16))
        o = o + _dot(scores.astype(BF16), vh)
        kd = (kh * zh[(N_LEVELS + 1) * c:]).astype(BF16)
        st_ref[d, h] = st * zh[tot_row:tot_row + 1] + _dot_tn(vh, kd)
        o_ref[rows, sl] = o


def _hgrn_kernel(qf_ref, ff_ref, vf_ref, qb_ref, fb_ref, vb_ref, lbraw_ref, s0_ref, mall_ref, mask_ref,
                 of_ref, ob_ref, sout_hbm, st_ref, stage_ref, sem):
    i = pl.program_id(0)
    j = (i - NB_CTX) % LAT_BLOCKS
    is_ctx = i < NB_CTX

    @pl.when(is_ctx)
    def _():
        st_ref[...] = jnp.zeros_like(st_ref)

    @pl.when(jnp.logical_and(jnp.logical_not(is_ctx), j == 0))
    def _():
        for d in range(2):
            for h in range(HEADS):
                st_ref[d, h] = s0_ref[0, d, h].T

    a0 = lbraw_ref[0]
    a1 = lbraw_ref[1]
    mx = jnp.maximum(a0, a1)
    e0 = jnp.exp(a0 - mx)
    e1 = jnp.exp(a1 - mx)
    lb = e0 / (e0 + e1)

    nchunk = TB // CHUNK

    def body(ci, carry):
        rf = pl.multiple_of(ci * CHUNK, CHUNK)
        rb = pl.multiple_of((nchunk - 1 - ci) * CHUNK, CHUNK)
        _hgrn_dir(0, qf_ref, ff_ref, vf_ref, of_ref, rf, lb[0:1], mall_ref, mask_ref, st_ref)
        _hgrn_dir(1, qb_ref, fb_ref, vb_ref, ob_ref, rb, lb[1:2], mall_ref, mask_ref, st_ref)
        return carry

    lax.fori_loop(0, nchunk, body, 0)

    @pl.when(is_ctx)
    def _():
        for d in range(2):
            for h in range(HEADS):
                stage_ref[d, h] = st_ref[d, h].T
        cp = pltpu.make_async_copy(stage_ref, sout_hbm.at[i], sem)
        cp.start()
        cp.wait()


def _bwd_block(i):
    j = (i - NB_CTX) % LAT_BLOCKS
    return jnp.where(i < NB_CTX, i, i - j + (LAT_BLOCKS - 1 - j))


def _hgrn(proj, hgrn_lb, s0, mall, masks):
    nh = HGRN_W
    fwd = lambda col: pl.BlockSpec((TB, nh), lambda i: (i, col))
    bwd = lambda col: pl.BlockSpec((TB, nh), lambda i: (_bwd_block(i), col))
    lat_seq = lambda i: jnp.clip((i - NB_CTX) // LAT_BLOCKS, 0, N_LAT_SEQ - 1)
    return pl.pallas_call(
        _hgrn_kernel,
        out_shape=(jax.ShapeDtypeStruct((T_ALL, nh), F32),
                   jax.ShapeDtypeStruct((T_ALL, nh), F32),
                   jax.ShapeDtypeStruct((N_CTX_SEQ, 2, HEADS, HEAD_K, HEAD_V), F32)),
        grid=(NB,),
        in_specs=[fwd(0), fwd(1), fwd(3), bwd(0), bwd(2), bwd(3),
                  pl.BlockSpec((2, 2, nh), lambda i: (0, 0, 0)),
                  pl.BlockSpec((1, 2, HEADS, HEAD_K, HEAD_V), lambda i: (lat_seq(i), 0, 0, 0, 0)),
                  pl.BlockSpec((2, EXP_ROWS, 3 * CHUNK), lambda i: (0, 0, 0)),
                  pl.BlockSpec((2, N_LEVELS + 1, CHUNK, CHUNK), lambda i: (0, 0, 0, 0))],
        out_specs=(pl.BlockSpec((TB, nh), lambda i: (i, 0)),
                   pl.BlockSpec((TB, nh), lambda i: (_bwd_block(i), 0)),
                   pl.BlockSpec(memory_space=pl.ANY)),
        scratch_shapes=[pltpu.VMEM((2, HEADS, HEAD_V, HEAD_K), F32),
                        pltpu.VMEM((2, HEADS, HEAD_K, HEAD_V), F32),
                        pltpu.SemaphoreType.DMA],
        compiler_params=_params(),
        name="hgrn",
    )(proj, proj, proj, proj, proj, proj, hgrn_lb, s0, mall, masks)


def _window_bounds(n, w):
    pos = np.arange(n)
    lo = np.clip(pos - w // 2, 0, n - 1)
    hi = np.clip(pos - w // 2 + w - 1, 0, n - 1)
    return lo, hi


def _pool_consts():
    seq, img, cnt_seq, cnt_col = [], [], [], []
    for w in POOL_WINDOWS:
        lo, hi = _window_bounds(CTX_LEN, w)
        u = np.arange(CTX_LEN)[None, :]
        seq.append((u >= lo[:, None]) & (u <= hi[:, None]))
        cnt_seq.append(hi - lo + 1)
        lo, hi = _window_bounds(GRID_W, w)
        u = np.arange(GRID_W)[None, :]
        band = (u >= lo[:, None]) & (u <= hi[:, None])
        img.append(np.kron(np.eye(TB // GRID_W, dtype=bool), band))
        cnt_col.append(np.tile(hi - lo + 1, TB // GRID_W))
    a = np.stack([np.stack(seq), np.stack(img)]).astype(np.float32)
    cnt = np.stack([np.stack(cnt_seq), np.stack(cnt_col)]).astype(np.float32)
    cnt = np.broadcast_to(cnt[..., None], cnt.shape + (POOL_G,))
    return jnp.asarray(a, BF16), jnp.asarray(cnt, F32)


POOL_ROWS = LAT_LEN


def _pool_kernel(u_ref, a_ref, cnt_ref, wp_ref, ps_ref, o_ref, cp_ref):
    i = pl.program_id(0)
    nblk = POOL_ROWS // TB

    def finish(g, r0, nrows, pm):
        sl = slice(g * POOL_G, (g + 1) * POOL_G)
        d = pm - u_ref[pl.ds(r0, nrows), sl]
        y = _dot(d.astype(BF16), wp_ref[g]) * ps_ref[:, sl]
        o_ref[pl.ds(r0, nrows), sl] = y.astype(o_ref.dtype)

    def window_sum(kind, g, b):
        sl = slice(g * POOL_G, (g + 1) * POOL_G)
        hi, lo = _split2(u_ref[pl.ds(b * TB, TB), sl])
        a = a_ref[kind, g]
        return (_dot(a, hi) + _dot(a, lo)) / cnt_ref[kind, g]

    @pl.when(i < T_CTX // POOL_ROWS)
    def _():
        for g in range(len(POOL_WINDOWS)):
            for b in range(nblk):
                finish(g, b * TB, TB, window_sum(0, g, b))

    @pl.when(i >= T_CTX // POOL_ROWS)
    def _():
        for g, w in enumerate(POOL_WINDOWS):
            for b in range(nblk):
                cp_ref[pl.ds(b * TB, TB), :] = window_sum(1, g, b)
            lo, hi = _window_bounds(GRID_H, w)
            for r in range(GRID_H):
                acc = cp_ref[pl.ds(int(lo[r]) * GRID_W, GRID_W), :]
                for rr in range(int(lo[r]) + 1, int(hi[r]) + 1):
                    acc = acc + cp_ref[pl.ds(rr * GRID_W, GRID_W), :]
                finish(g, r * GRID_W, GRID_W, acc / float(hi[r] - lo[r] + 1))


def _pool(proj, a_pool, cnt_pool, w_pool_bf, pool_scale):
    col = (5 * HGRN_W) // POOL_W
    return pl.pallas_call(
        _pool_kernel,
        out_shape=jax.ShapeDtypeStruct((T_ALL, POOL_W), BF16),
        grid=(T_ALL // POOL_ROWS,),
        in_specs=[pl.BlockSpec((POOL_ROWS, POOL_W), lambda i: (i, col)),
                  pl.BlockSpec((2, 4, TB, TB), lambda i: (0, 0, 0, 0)),
                  pl.BlockSpec((2, 4, TB, POOL_G), lambda i: (0, 0, 0, 0)),
                  pl.BlockSpec((4, POOL_G, POOL_G), lambda i: (0, 0, 0)),
                  pl.BlockSpec((1, POOL_W), lambda i: (0, 0))],
        out_specs=pl.BlockSpec((POOL_ROWS, POOL_W), lambda i: (i, 0)),
        scratch_shapes=[pltpu.VMEM((POOL_ROWS, POOL_G), F32)],
        compiler_params=_params(),
        name="pool",
    )(proj, a_pool, cnt_pool, w_pool_bf, pool_scale)


def _merge_kernel(xc_ref, xl_ref, of_ref, ob_ref, og_ref, yb_ref, ga_ref, gb_ref, mod_ref, hn_ref, wa_ref, wb_ref,
                  wo_ref, n2_ref, wrh_ref, wrl_ref, br_ref,
                  x1_ref, h2_ref, te_ref, tw_ref, hist_ref):
    row = _mod_row(pl.program_id(0))
    gate1 = mod_ref[pl.ds(row, 1), pl.ds(2 * D_MODEL, D_MODEL)]
    shift2 = mod_ref[pl.ds(row, 1), pl.ds(3 * D_MODEL, D_MODEL)]
    scale2 = mod_ref[pl.ds(row, 1), pl.ds(4 * D_MODEL, D_MODEL)]

    o = of_ref[...] + ob_ref[...]
    og = og_ref[...]
    ya = jnp.concatenate(
        [_rms(o[:, h * HEAD_V:(h + 1) * HEAD_V], hn_ref[...]) for h in range(HEADS)], axis=1)
    ya = ya * (og * jax.nn.sigmoid(og))
    merged = (jax.nn.sigmoid(ga_ref[...]) * _dot(ya.astype(BF16), wa_ref[...])
              + jax.nn.sigmoid(gb_ref[...]) * _dot(yb_ref[...], wb_ref[...]))
    x1 = _x_block(xc_ref, xl_ref) + gate1 * _dot(merged.astype(BF16), wo_ref[...])
    x1_ref[...] = x1
    h2 = _rms(x1, n2_ref[...]) * (1.0 + scale2) + shift2
    hh, hl = _split2(h2)
    h2_ref[...] = hh

    lt = _dot_nt(wrh_ref[...], hh) + _dot_nt(wrl_ref[...], hh) + _dot_nt(wrh_ref[...], hl) + br_ref[...]
    eidx = lax.broadcasted_iota(I32, (N_EXP, TB), 0)
    vals, idxs, cnt = [], [], jnp.zeros((N_EXP, TB), F32)
    for _ in range(TOP_K):
        m = jnp.max(lt, axis=0, keepdims=True)
        idx = jnp.min(jnp.where(lt == m, eidx, N_EXP), axis=0, keepdims=True)
        sel = eidx == idx
        vals.append(m)
        idxs.append(idx)
        cnt = cnt + sel.astype(F32)
        lt = jnp.where(sel, -jnp.inf, lt)
    ex = [jnp.exp(v - vals[0]) for v in vals]
    den = ex[0] + ex[1] + ex[2] + ex[3]
    tw_ref[0] = jnp.concatenate([e / den for e in ex], axis=0)
    te_ref[0] = jnp.concatenate(idxs, axis=0)
    hist_ref[0] = jnp.sum(cnt, axis=1, keepdims=True).astype(I32)


def _merge(x_ctx, x_lat, o_f, o_b, proj, yb, mod, hgrn_norm, wa_bf, wb_bf, wo_bf, norm2, wr_hi, wr_lo, b_router):
    full = lambda shape: pl.BlockSpec(shape, lambda i: (0,) * len(shape))
    return pl.pallas_call(
        _merge_kernel,
        out_shape=(jax.ShapeDtypeStruct((T_ALL, D_MODEL), F32),
                   jax.ShapeDtypeStruct((T_ALL, D_MODEL), BF16),
                   jax.ShapeDtypeStruct((NB, TOP_K, TB), I32),
                   jax.ShapeDtypeStruct((NB, TOP_K, TB), F32),
                   jax.ShapeDtypeStruct((NB, N_EXP, 1), I32)),
        grid=(NB,),
        in_specs=_x_specs() + [
                  pl.BlockSpec((TB, HGRN_W), lambda i: (i, 0)),
                  pl.BlockSpec((TB, HGRN_W), lambda i: (i, 0)),
                  pl.BlockSpec((TB, HGRN_W), lambda i: (i, 4)),
                  pl.BlockSpec((TB, POOL_W), lambda i: (i, 0)),
                  pl.BlockSpec((TB, D_MODEL), lambda i: (i, 3)),
                  pl.BlockSpec((TB, D_MODEL), lambda i: (i, 4)),
                  full((MOD_ROWS, 6 * D_MODEL)),
                  full((1, HEAD_V)),
                  full((HGRN_W, D_MODEL)),
                  full((POOL_W, D_MODEL)),
                  full((D_MODEL, D_MODEL)),
                  full((1, D_MODEL)),
                  full((N_EXP, D_MODEL)),
                  full((N_EXP, D_MODEL)),
                  full((N_EXP, 1))],
        out_specs=(pl.BlockSpec((TB, D_MODEL), lambda i: (i, 0)),
                   pl.BlockSpec((TB, D_MODEL), lambda i: (i, 0)),
                   pl.BlockSpec((1, TOP_K, TB), lambda i: (i, 0, 0)),
                   pl.BlockSpec((1, TOP_K, TB), lambda i: (i, 0, 0)),
                   pl.BlockSpec((1, N_EXP, 1), lambda i: (i, 0, 0))),
        compiler_params=_params(),
        name="merge",
    )(x_ctx, x_lat, o_f, o_b, proj, yb, proj, proj, mod, hgrn_norm, wa_bf, wb_bf, wo_bf, norm2,
      wr_hi, wr_lo, b_router)


def _local_rows(te_ref, loff_ref, tri_ref):
    te = jnp.concatenate([te_ref[j] for j in range(TBD // TB)], axis=1)
    eidx = lax.broadcasted_iota(I32, (N_EXP, TBD), 0)
    sels = [eidx == te[k:k + 1] for k in range(TOP_K)]
    cnt = sels[0].astype(F32)
    for s in sels[1:]:
        cnt = cnt + s.astype(F32)
    base = _dot(cnt.astype(BF16), tri_ref[...]) + loff_ref[0]
    return [jnp.sum(jnp.where(s, base, 0.0), axis=0, keepdims=True) for s in sels]


def _segment_copies(make_copy, local_off, global_off, units):
    for j in range(SEG_BITS):
        low = (units & ((1 << j) - 1)) * 8

        @pl.when(((units >> j) & 1) == 1)
        def _():
            make_copy(pl.multiple_of(local_off + low, 8), pl.multiple_of(global_off + low, 8), 8 << j).start()


def _pack_pairs(x):
    half = D_MODEL // 2
    lo = lax.bitcast_convert_type(x[:, :half], U32) >> 16
    hi = lax.bitcast_convert_type(x[:, half:], U32) & jnp.uint32(0xFFFF0000)
    return hi | lo


def _unpack_pairs(p):
    lo = lax.bitcast_convert_type(p << 16, F32).astype(BF16)
    hi = lax.bitcast_convert_type(p & jnp.uint32(0xFFFF0000), F32).astype(BF16)
    return jnp.concatenate([lo, hi], axis=1)


def _block_rows(loff_s, seg_s, b):
    last = b * N_EXP + N_EXP - 1
    return pl.multiple_of(loff_s[last] + seg_s[last], 8)


def _dispatch_kernel(loff_s, seg_s, gbase_s, tail_s, h2_ref, te_ref, loffv_ref, tri_ref, xs_hbm,
                     loc, zeros, sem, sem_z):
    b = pl.program_id(0)
    slot = b % 2

    def wait_block(blk, s):
        n = _block_rows(loff_s, seg_s, blk)
        pltpu.make_async_copy(loc.at[s, pl.ds(0, n)], xs_hbm.at[pl.ds(0, n)], sem.at[s]).wait()

    lrow = _local_rows(te_ref, loffv_ref, tri_ref)

    @pl.when(b >= 2)
    def _():
        wait_block(b - 2, slot)

    for r0 in range(0, LROWS, SORT_CHUNK):
        piota = (lax.broadcasted_iota(I32, (SORT_CHUNK, TBD), 0) + r0).astype(F32)
        hit = piota == lrow[0]
        for k in range(1, TOP_K):
            hit = jnp.logical_or(hit, piota == lrow[k])
        sorted_rows = _dot(jnp.where(hit, 1.0, 0.0).astype(BF16), h2_ref[...])
        loc[slot, r0:r0 + SORT_CHUNK, :] = _pack_pairs(sorted_rows)

    def out_copy(a, g, size):
        return pltpu.make_async_copy(loc.at[slot, pl.ds(a, size)], xs_hbm.at[pl.ds(g, size)], sem.at[slot])

    def body(e, carry):
        idx = b * N_EXP + e
        _segment_copies(out_copy, loff_s[idx], gbase_s[idx], seg_s[idx] // 8)
        return carry

    lax.fori_loop(0, N_EXP, body, 0)

    @pl.when(b == NBD - 1)
    def _():
        zeros[...] = jnp.zeros_like(zeros)

        def zero_copy(a, g, size):
            return pltpu.make_async_copy(zeros.at[pl.ds(a, size)], xs_hbm.at[pl.ds(g, size)], sem_z)

        def zbody(e, ztot):
            _segment_copies(zero_copy, 0, tail_s[e], tail_s[N_EXP + e] // 8)
            return ztot + tail_s[N_EXP + e]

        def tbody(t, carry):
            pltpu.make_async_copy(zeros, xs_hbm.at[pl.ds(pl.multiple_of(t * TM, TM), TM)], sem_z).start()
            return carry

        n_used = tail_s[2 * N_EXP]
        lax.fori_loop(n_used, N_TILES, tbody, 0)
        ztot = lax.fori_loop(0, N_EXP, zbody, 0) + (N_TILES - n_used) * TM
        ztot = pl.multiple_of(ztot, 8)

        @pl.when(ztot > 0)
        def _():
            pltpu.make_async_copy(xs_hbm.at[pl.ds(0, ztot)], xs_hbm.at[pl.ds(0, ztot)], sem_z).wait()

        wait_block(b - 1, 1 - slot)
        wait_block(b, slot)


def _dispatch(tables, h2, top_e, tri):
    loff_s, seg_s, gbase_s, tail_s, loff_v = tables
    grid_spec = pltpu.PrefetchScalarGridSpec(
        num_scalar_prefetch=4,
        grid=(NBD,),
        in_specs=[pl.BlockSpec((TBD, D_MODEL), lambda i, *_: (i, 0)),
                  pl.BlockSpec((TBD // TB, TOP_K, TB), lambda i, *_: (i, 0, 0)),
                  pl.BlockSpec((1, N_EXP, 1), lambda i, *_: (i, 0, 0)),
                  pl.BlockSpec((TBD, TBD), lambda i, *_: (0, 0))],
        out_specs=pl.BlockSpec(memory_space=pl.ANY),
        scratch_shapes=[pltpu.VMEM((2, LROWS, D_MODEL // 2), U32),
                        pltpu.VMEM((TM, D_MODEL // 2), U32),
                        pltpu.SemaphoreType.DMA((2,)),
                        pltpu.SemaphoreType.DMA])
    return pl.pallas_call(
        _dispatch_kernel,
        out_shape=jax.ShapeDtypeStruct((N_TILES * TM, D_MODEL // 2), U32),
        grid_spec=grid_spec,
        compiler_params=_params(),
        name="dispatch",
    )(loff_s, seg_s, gbase_s, tail_s, h2, top_e, loff_v, tri)


def _moe_kernel(te_ref, first_ref, par_ref, next_ref, nu_ref, xs_ref, bgu_ref, bd_ref, wgu_hbm, wd_hbm, o_ref,
                wgu_st, wd_st, wgu_bf, wd_bf, sem):
    i = pl.program_id(0)

    def fetch(e, s):
        return (pltpu.make_async_copy(wgu_hbm.at[e], wgu_st.at[s], sem.at[0, s]),
                pltpu.make_async_copy(wd_hbm.at[e], wd_st.at[s], sem.at[1, s]))

    @pl.when(i < nu_ref[0])
    def _():
        @pl.when(first_ref[i] == 1)
        def _():
            s = par_ref[i]

            @pl.when(i == 0)
            def _():
                for cp in fetch(te_ref[0], 0):
                    cp.start()

            for cp in fetch(te_ref[i], s):
                cp.wait()

            @pl.when(next_ref[i] >= 0)
            def _():
                for cp in fetch(next_ref[i], 1 - s):
                    cp.start()

            wgu_bf[...] = wgu_st[s].astype(BF16)
            wd_bf[...] = wd_st[s].astype(BF16)

        gu = _dot(_unpack_pairs(xs_ref[...]), wgu_bf[...]) + bgu_ref[0]
        gate = jnp.minimum(gu[:, :D_FF], SWIGLU_LIMIT)
        up = jnp.clip(gu[:, D_FF:], -SWIGLU_LIMIT, SWIGLU_LIMIT)
        act = (up + 1.0) * gate * jax.nn.sigmoid(SWIGLU_ALPHA * gate)
        out = _dot(act.astype(BF16), wd_bf[...]) + bd_ref[0]
        o_ref[...] = _pack_pairs(out.astype(BF16).astype(F32))

    @pl.when(i >= nu_ref[0])
    def _():
        o_ref[...] = jnp.zeros_like(o_ref)


def _moe(tile_tables, xs, w_gate_up, b_gate_up, w_down, b_down):
    nsp = len(tile_tables)
    row_tile = lambda i, *s: (jnp.minimum(i, s[nsp - 1][0] - 1), 0)
    grid_spec = pltpu.PrefetchScalarGridSpec(
        num_scalar_prefetch=nsp,
        grid=(N_TILES,),
        in_specs=[pl.BlockSpec((TM, D_MODEL // 2), row_tile),
                  pl.BlockSpec((1, 1, 2 * D_FF), lambda i, te, *_: (te[i], 0, 0)),
                  pl.BlockSpec((1, 1, D_MODEL), lambda i, te, *_: (te[i], 0, 0)),
                  pl.BlockSpec(memory_space=pl.ANY),
                  pl.BlockSpec(memory_space=pl.ANY)],
        out_specs=pl.BlockSpec((TM, D_MODEL // 2), lambda i, *_: (i, 0)),
        scratch_shapes=[pltpu.VMEM((2, D_MODEL, 2 * D_FF), F32),
                        pltpu.VMEM((2, D_FF, D_MODEL), F32),
                        pltpu.VMEM((D_MODEL, 2 * D_FF), BF16),
                        pltpu.VMEM((D_FF, D_MODEL), BF16),
                        pltpu.SemaphoreType.DMA((2, 2))])
    return pl.pallas_call(
        _moe_kernel,
        out_shape=jax.ShapeDtypeStruct((N_TILES * TM, D_MODEL // 2), U32),
        grid_spec=grid_spec,
        compiler_params=_params(),
        name="moe",
    )(*tile_tables, xs, b_gate_up.reshape(N_EXP, 1, 2 * D_FF), b_down.reshape(N_EXP, 1, D_MODEL),
      w_gate_up, w_down)


def _final_kernel(loff_s, seg_s, gbase_s, ys_hbm, x1_ref, te_ref, tw_ref, loffv_ref, tri_ref, mod_ref, fn_ref,
                  oc_ref, ol_ref, loc, sem):
    b = pl.program_id(0)
    slot = b % 2

    def start_block(blk, s):
        def in_copy(a, g, size):
            return pltpu.make_async_copy(ys_hbm.at[pl.ds(g, size)], loc.at[s, pl.ds(a, size)], sem.at[s])

        def body(e, carry):
            idx = blk * N_EXP + e
            _segment_copies(in_copy, loff_s[idx], gbase_s[idx], seg_s[idx] // 8)
            return carry

        lax.fori_loop(0, N_EXP, body, 0)

    @pl.when(b == 0)
    def _():
        loc[...] = jnp.zeros_like(loc)
        start_block(0, 0)

    @pl.when(b + 1 < NBD)
    def _():
        start_block(b + 1, 1 - slot)

    lrow = _local_rows(te_ref, loffv_ref, tri_ref)
    tw = jnp.concatenate([tw_ref[j] for j in range(TBD // TB)], axis=1)
    rows8 = jnp.concatenate(lrow + [tw], axis=0)
    cols = jnp.concatenate([rows8, jnp.zeros((128 - 2 * TOP_K, TBD), F32)], axis=0).T

    n = _block_rows(loff_s, seg_s, b)
    pltpu.make_async_copy(ys_hbm.at[pl.ds(0, n)], loc.at[slot, pl.ds(0, n)], sem.at[slot]).wait()

    y = None
    for r0 in range(0, LROWS, SORT_CHUNK):
        piota = (lax.broadcasted_iota(I32, (TBD, SORT_CHUNK), 1) + r0).astype(F32)
        pw = jnp.where(piota == cols[:, 0:1], cols[:, TOP_K:TOP_K + 1], 0.0)
        for k in range(1, TOP_K):
            pw = pw + jnp.where(piota == cols[:, k:k + 1], cols[:, TOP_K + k:TOP_K + k + 1], 0.0)
        part = _dot(pw.astype(BF16), _unpack_pairs(loc[slot, r0:r0 + SORT_CHUNK, :]))
        y = part if y is None else y + part

    row = jnp.where(b < NBD_CTX, N_LAT_SEQ, (b - NBD_CTX) // (LAT_LEN // TBD))
    gate2 = mod_ref[pl.ds(row, 1), pl.ds(5 * D_MODEL, D_MODEL)]
    out = _rms(x1_ref[...] + gate2 * y, fn_ref[...])

    @pl.when(b < NBD_CTX)
    def _():
        oc_ref[...] = out

    @pl.when(b >= NBD_CTX)
    def _():
        ol_ref[...] = out


def _final(tables, ys, x1, top_e, top_w, tri, mod, final_norm):
    loff_s, seg_s, gbase_s, _, loff_v = tables
    grid_spec = pltpu.PrefetchScalarGridSpec(
        num_scalar_prefetch=3,
        grid=(NBD,),
        in_specs=[pl.BlockSpec(memory_space=pl.ANY),
                  pl.BlockSpec((TBD, D_MODEL), lambda i, *_: (i, 0)),
                  pl.BlockSpec((TBD // TB, TOP_K, TB), lambda i, *_: (i, 0, 0)),
                  pl.BlockSpec((TBD // TB, TOP_K, TB), lambda i, *_: (i, 0, 0)),
                  pl.BlockSpec((1, N_EXP, 1), lambda i, *_: (i, 0, 0)),
                  pl.BlockSpec((TBD, TBD), lambda i, *_: (0, 0)),
                  pl.BlockSpec((MOD_ROWS, 6 * D_MODEL), lambda i, *_: (0, 0)),
                  pl.BlockSpec((1, D_MODEL), lambda i, *_: (0, 0))],
        out_specs=(pl.BlockSpec((TBD, D_MODEL), lambda i, *_: (jnp.minimum(i, NBD_CTX - 1), 0)),
                   pl.BlockSpec((TBD, D_MODEL), lambda i, *_: (jnp.maximum(i - NBD_CTX, 0), 0))),
        scratch_shapes=[pltpu.VMEM((2, LROWS, D_MODEL // 2), U32),
                        pltpu.SemaphoreType.DMA((2,))])
    return pl.pallas_call(
        _final_kernel,
        out_shape=(jax.ShapeDtypeStruct((T_CTX, D_MODEL), F32),
                   jax.ShapeDtypeStruct((T_LAT, D_MODEL), F32)),
        grid_spec=grid_spec,
        compiler_params=_params(),
        name="final",
    )(loff_s, seg_s, gbase_s, ys, x1, top_e, top_w, loff_v, tri, mod, final_norm)


def _dispatch_tables(hist):
    hist = jnp.sum(hist.reshape(NBD, TBD // TB, N_EXP), axis=1)
    seg = ((hist + 7) // 8) * 8
    loff = jnp.cumsum(seg, axis=1) - seg
    rows_e = jnp.sum(seg, axis=0)
    region = ((rows_e + TM - 1) // TM) * TM
    region_end = jnp.cumsum(region)
    region_start = region_end - region
    gbase = region_start[None, :] + jnp.cumsum(seg, axis=0) - seg
    n_used = (region_end[-1] // TM).astype(I32)
    tail = jnp.concatenate([region_start + rows_e, region - rows_e, n_used.reshape(1)])
    start = jnp.arange(N_TILES, dtype=I32) * TM
    tile_e = jnp.sum((start[:, None] >= region_end[None, :]).astype(I32), axis=1)
    tile_e = jnp.minimum(tile_e, tile_e[jnp.maximum(n_used - 1, 0)])
    first = jnp.concatenate([jnp.ones((1,), I32), (tile_e[1:] != tile_e[:-1]).astype(I32)])
    parity = (jnp.cumsum(first) - 1) % 2
    later = jnp.where(tile_e[None, :] > tile_e[:, None], tile_e[None, :], N_EXP)
    nxt = jnp.min(later, axis=1)
    nxt = jnp.where(nxt == N_EXP, -1, nxt)
    flat = lambda a: a.reshape(-1).astype(I32)
    tables = (flat(loff), flat(seg), flat(gbase), flat(tail), loff.astype(F32).reshape(NBD, N_EXP, 1))
    tile_tables = (flat(tile_e), flat(first), flat(parity), flat(nxt), n_used.reshape(1))
    return tables, tile_tables


def kernel(x_prompt, x_sample, state_hgrn, c, c_ctx, w_ada, b_ada, norm1, w_in, hgrn_lb, hgrn_norm, w_pool,
           pool_scale, w_branch_a, w_branch_b, w_out, norm2, w_router, b_router, w_gate_up, b_gate_up,
           w_down, b_down, final_norm):
    x_ctx = x_prompt.reshape(T_CTX, D_MODEL)
    x_lat = x_sample.reshape(T_LAT, D_MODEL)
    cc = jnp.zeros((MOD_ROWS, D_MODEL), F32).at[:N_LAT_SEQ].set(c).at[N_LAT_SEQ].set(c_ctx)
    mod = _ada(cc, w_ada[0], b_ada)

    proj = _inproj(x_ctx, x_lat, mod, norm1, w_in[0].astype(BF16))

    mall, masks = _hgrn_consts()
    o_f, o_b, new_state = _hgrn(proj, hgrn_lb, state_hgrn[:, 0], mall, masks)

    a_pool, cnt_pool = _pool_consts()
    yb = _pool(proj, a_pool, cnt_pool, w_pool[0].astype(BF16), pool_scale)

    wr_t = w_router[0].T
    wr_hi = wr_t.astype(BF16)
    wr_lo = (wr_t - wr_hi.astype(F32)).astype(BF16)
    tri = jnp.asarray(np.triu(np.ones((TBD, TBD), np.float32), 1), BF16)
    x1, h2, top_e, top_w, hist = _merge(
        x_ctx, x_lat, o_f, o_b, proj, yb, mod, hgrn_norm, w_branch_a[0].astype(BF16), w_branch_b[0].astype(BF16),
        w_out[0].astype(BF16), norm2, wr_hi, wr_lo, b_router.reshape(N_EXP, 1))

    tables, tile_tables = _dispatch_tables(hist)
    xs = _dispatch(tables, h2, top_e, tri)
    ys = _moe(tile_tables, xs, w_gate_up[0], b_gate_up[0], w_down[0], b_down[0])
    y_ctx, y_lat = _final(tables, ys, x1, top_e, top_w, tri, mod, final_norm.reshape(1, D_MODEL))
    y_prompt = y_ctx.reshape(N_CTX_SEQ, CTX_LEN, D_MODEL)
    y_sample = y_lat.reshape(N_LAT_SEQ, LAT_LEN, D_MODEL)
    return y_prompt, y_sample, new_state[:, None]
```

```python
import functools

import numpy as np
import jax
import jax.numpy as jnp
from jax import lax
from jax.experimental import pallas as pl
from jax.experimental.pallas import tpu as pltpu

F32 = jnp.float32
BF16 = jnp.bfloat16
I32 = jnp.int32
U32 = jnp.uint32

D_MODEL = 1024
N_CTX_SEQ, CTX_LEN = 32, 256
N_LAT_SEQ, LAT_LEN = 4, 2048
T_CTX = N_CTX_SEQ * CTX_LEN
T_LAT = N_LAT_SEQ * LAT_LEN
T_ALL = T_CTX + T_LAT
TB = 256
NB = T_ALL // TB
NB_CTX = T_CTX // TB
LAT_BLOCKS = LAT_LEN // TB
HEADS, HEAD_K, HEAD_V = 4, 128, 128
HGRN_W = HEADS * HEAD_V
POOL_WINDOWS = (2, 4, 8, 16)
POOL_G = 128
POOL_W = len(POOL_WINDOWS) * POOL_G
GRID_W = 64
GRID_H = LAT_LEN // GRID_W
IN_W = 5 * HGRN_W + POOL_W + 2 * D_MODEL
N_EXP, TOP_K, D_FF = 32, 4, 1024
SWIGLU_LIMIT = 7.0
SWIGLU_ALPHA = 1.702
EPS = 1e-6
LOG2_E = 1.4426950408889634
CHUNK = 64
N_LEVELS = 6
EXP_ROWS = (N_LEVELS + 2) * CHUNK
TM = 512
TBD = 512
NBD = T_ALL // TBD
NBD_CTX = T_CTX // TBD
SORT_CHUNK = 256
LROWS = TBD * TOP_K + N_EXP * 8
SEG_BITS = 7
N_TILES = -(-(T_ALL * TOP_K + NBD * N_EXP * 7 + N_EXP * (TM - 1)) // TM)
MOD_ROWS = 8
VMEM_LIMIT = 56 * 1024 * 1024


def _params(sem=("arbitrary",)):
    return pltpu.CompilerParams(dimension_semantics=sem, vmem_limit_bytes=VMEM_LIMIT)


def _dot(a, b):
    return jnp.dot(a, b, preferred_element_type=F32)


def _dot_nt(a, b):
    return lax.dot_general(a, b, (((1,), (1,)), ((), ())), preferred_element_type=F32)


def _dot_tn(a, b):
    return lax.dot_general(a, b, (((0,), (0,)), ((), ())), preferred_element_type=F32)


def _split2(x):
    hi = x.astype(BF16)
    lo = (x - hi.astype(F32)).astype(BF16)
    return hi, lo


def _mod_row(i):
    return jnp.where(i < NB_CTX, N_LAT_SEQ, (i - NB_CTX) // LAT_BLOCKS)


def _ada_kernel(c_ref, w_ref, b_ref, o_ref):
    c = c_ref[...]
    s = c * jax.nn.sigmoid(c)
    o_ref[...] = jnp.dot(s, w_ref[...], preferred_element_type=F32,
                         precision=lax.Precision.HIGHEST) + b_ref[...]


def _ada(cc, w_ada, b_ada):
    nblk = 1536
    return pl.pallas_call(
        _ada_kernel,
        out_shape=jax.ShapeDtypeStruct((MOD_ROWS, 6 * D_MODEL), F32),
        grid=(6 * D_MODEL // nblk,),
        in_specs=[pl.BlockSpec((MOD_ROWS, D_MODEL), lambda j: (0, 0)),
                  pl.BlockSpec((D_MODEL, nblk), lambda j: (0, j)),
                  pl.BlockSpec((1, nblk), lambda j: (0, j))],
        out_specs=pl.BlockSpec((MOD_ROWS, nblk), lambda j: (0, j)),
        compiler_params=_params(),
        name="ada",
    )(cc, w_ada, b_ada)


def _rms(x, g):
    ms = jnp.mean(x * x, axis=-1, keepdims=True)
    return x * lax.rsqrt(ms + EPS) * g


def _x_specs():
    return [pl.BlockSpec((TB, D_MODEL), lambda i, *_: (jnp.minimum(i, NB_CTX - 1), 0)),
            pl.BlockSpec((TB, D_MODEL), lambda i, *_: (jnp.maximum(i - NB_CTX, 0), 0))]


def _x_block(xc_ref, xl_ref):
    return jnp.where(pl.program_id(0) < NB_CTX, xc_ref[...], xl_ref[...])


def _inproj_kernel(xc_ref, xl_ref, mod_ref, n1_ref, w_ref, o_ref):
    row = _mod_row(pl.program_id(0))
    shift = mod_ref[pl.ds(row, 1), pl.ds(0, D_MODEL)]
    scale = mod_ref[pl.ds(row, 1), pl.ds(D_MODEL, D_MODEL)]
    h = _rms(_x_block(xc_ref, xl_ref), n1_ref[...]) * (1.0 + scale) + shift
    o_ref[...] = _dot(h.astype(BF16), w_ref[...])


def _inproj(x_ctx, x_lat, mod, norm1, w_in_bf):
    return pl.pallas_call(
        _inproj_kernel,
        out_shape=jax.ShapeDtypeStruct((T_ALL, IN_W), F32),
        grid=(NB,),
        in_specs=_x_specs() + [
                  pl.BlockSpec((MOD_ROWS, 6 * D_MODEL), lambda i: (0, 0)),
                  pl.BlockSpec((1, D_MODEL), lambda i: (0, 0)),
                  pl.BlockSpec((D_MODEL, IN_W), lambda i: (0, 0))],
        out_specs=pl.BlockSpec((TB, IN_W), lambda i: (i, 0)),
        compiler_params=_params(),
        name="inproj",
    )(x_ctx, x_lat, mod, norm1, w_in_bf)


def _hgrn_consts():
    c = CHUNK
    t = np.arange(c)[:, None]
    u = np.arange(c)[None, :]
    blocks = [u <= t]
    masks = [np.eye(c, dtype=bool)]
    h = c // 2
    while h >= 1:
        bi = t // h
        upper = (bi % 2) == 1
        e_up = (u >= bi * h) & (u <= t)
        e_lo = (u > t) & (u <= bi * h + h - 1)
        blocks.append(np.where(upper, e_up, e_lo))
        masks.append(((t // (2 * h)) == (u // (2 * h))) & (((t // h) % 2) == 1) & (((u // h) % 2) == 0))
        h //= 2
    blocks.append(u > t)
    m_f = np.stack(blocks).astype(np.float32)
    k_f = np.stack(masks).astype(np.float32)
    m_b = m_f[:, ::-1, ::-1]
    k_b = k_f[:, ::-1, ::-1]
    m = np.stack([m_f.reshape(EXP_ROWS, c), m_b.reshape(EXP_ROWS, c)])
    m3 = np.concatenate([m, m, m], axis=2)
    return jnp.asarray(m3, BF16), jnp.asarray(np.stack([k_f, k_b]), F32)


def _hgrn_block(dirs, lb, mall_ref, mask_ref, st_ref, z_ref, k_ref, sc_ref):
    c = CHUNK
    nchunk = TB // c
    units = [(d, h) for d in range(2) for h in range(HEADS)]
    sl = [slice(h * HEAD_K, (h + 1) * HEAD_K) for h in range(HEADS)]

    def rows(ci, d):
        r0 = ci * c if d == 0 else (nchunk - 1 - ci) * c
        return slice(r0, r0 + c)

    def exponents(ci):
        s = ci % 2
        for d in range(2):
            f = lb[d:d + 1] + (1.0 - lb[d:d + 1]) * jax.nn.sigmoid(dirs[d][1][rows(ci, d), :])
            k_ref[s, d] = 1.0 - f
            g = jnp.log(f) * LOG2_E
            g1 = g.astype(BF16)
            r1 = g - g1.astype(F32)
            g2 = r1.astype(BF16)
            g3 = (r1 - g2.astype(F32)).astype(BF16)
            gsplit = jnp.concatenate([g1, g2, g3], axis=0)
            z_ref[s, d] = jnp.exp2(_dot(mall_ref[d], gsplit))

    def q_of(ci, d, h):
        return dirs[d][0][rows(ci, d), sl[h]]

    def v_of(ci, d, h):
        return dirs[d][2][rows(ci, d), sl[h]].astype(BF16)

    def qz(ci, d, h, blk):
        return (q_of(ci, d, h) * z_ref[ci % 2, d, blk * c:(blk + 1) * c, sl[h]]).astype(BF16)

    def kz(ci, d, h, blk):
        return (k_ref[ci % 2, d, :, sl[h]] * z_ref[ci % 2, d, blk * c:(blk + 1) * c, sl[h]]).astype(BF16)

    def levels(ci):
        for d, h in units:
            sc_ref[d, h] = mask_ref[d, 0] * _dot_nt(q_of(ci, d, h).astype(BF16),
                                                    k_ref[ci % 2, d, :, sl[h]].astype(BF16))
        for lev in range(N_LEVELS):
            for d, h in units:
                sc_ref[d, h] += mask_ref[d, lev + 1] * _dot_nt(qz(ci, d, h, lev + 1), kz(ci, d, h, lev + 1))

    def tail(ci):
        for d, h in units:
            o = (_dot_nt(qz(ci, d, h, 0), st_ref[d, h].astype(BF16))
                 + _dot(sc_ref[d, h].astype(BF16), v_of(ci, d, h)))
            dirs[d][3][rows(ci, d), sl[h]] = o * (HEAD_K ** -0.5)
        for d, h in units:
            tot_row = c - 1 if d == 0 else 0
            decay = z_ref[ci % 2, d, tot_row:tot_row + 1, sl[h]]
            st_ref[d, h] = st_ref[d, h] * decay + _dot_tn(v_of(ci, d, h), kz(ci, d, h, N_LEVELS + 1))

    exponents(0)
    for ci in range(nchunk):
        levels(ci)
        if ci + 1 < nchunk:
            exponents(ci + 1)
        tail(ci)


def _hgrn_kernel(qf_ref, ff_ref, vf_ref, qb_ref, fb_ref, vb_ref, lbraw_ref, s0_ref, mall_ref, mask_ref,
                 of_ref, ob_ref, sout_hbm, st_ref, stage_ref, z_ref, k_ref, sc_ref, sem):
    i = pl.program_id(0)
    j = (i - NB_CTX) % LAT_BLOCKS
    is_ctx = i < NB_CTX

    @pl.when(is_ctx)
    def _():
        st_ref[...] = jnp.zeros_like(st_ref)

    @pl.when(jnp.logical_and(jnp.logical_not(is_ctx), j == 0))
    def _():
        for d in range(2):
            for h in range(HEADS):
                st_ref[d, h] = s0_ref[0, d, h].T

    a0 = lbraw_ref[0]
    a1 = lbraw_ref[1]
    mx = jnp.maximum(a0, a1)
    e0 = jnp.exp(a0 - mx)
    e1 = jnp.exp(a1 - mx)
    lb = e0 / (e0 + e1)

    dirs = ((qf_ref, ff_ref, vf_ref, of_ref), (qb_ref, fb_ref, vb_ref, ob_ref))
    _hgrn_block(dirs, lb, mall_ref, mask_ref, st_ref, z_ref, k_ref, sc_ref)

    @pl.when(is_ctx)
    def _():
        for d in range(2):
            for h in range(HEADS):
                stage_ref[d, h] = st_ref[d, h].T
        cp = pltpu.make_async_copy(stage_ref, sout_hbm.at[i], sem)
        cp.start()
        cp.wait()


def _bwd_block(i):
    j = (i - NB_CTX) % LAT_BLOCKS
    return jnp.where(i < NB_CTX, i, i - j + (LAT_BLOCKS - 1 - j))


def _hgrn(proj, hgrn_lb, s0, mall, masks):
    nh = HGRN_W
    fwd = lambda col: pl.BlockSpec((TB, nh), lambda i: (i, col))
    bwd = lambda col: pl.BlockSpec((TB, nh), lambda i: (_bwd_block(i), col))
    lat_seq = lambda i: jnp.clip((i - NB_CTX) // LAT_BLOCKS, 0, N_LAT_SEQ - 1)
    return pl.pallas_call(
        _hgrn_kernel,
        out_shape=(jax.ShapeDtypeStruct((T_ALL, nh), F32),
                   jax.ShapeDtypeStruct((T_ALL, nh), F32),
                   jax.ShapeDtypeStruct((N_CTX_SEQ, 2, HEADS, HEAD_K, HEAD_V), F32)),
        grid=(NB,),
        in_specs=[fwd(0), fwd(1), fwd(3), bwd(0), bwd(2), bwd(3),
                  pl.BlockSpec((2, 2, nh), lambda i: (0, 0, 0)),
                  pl.BlockSpec((1, 2, HEADS, HEAD_K, HEAD_V), lambda i: (lat_seq(i), 0, 0, 0, 0)),
                  pl.BlockSpec((2, EXP_ROWS, 3 * CHUNK), lambda i: (0, 0, 0)),
                  pl.BlockSpec((2, N_LEVELS + 1, CHUNK, CHUNK), lambda i: (0, 0, 0, 0))],
        out_specs=(pl.BlockSpec((TB, nh), lambda i: (i, 0)),
                   pl.BlockSpec((TB, nh), lambda i: (_bwd_block(i), 0)),
                   pl.BlockSpec(memory_space=pl.ANY)),
        scratch_shapes=[pltpu.VMEM((2, HEADS, HEAD_V, HEAD_K), F32),
                        pltpu.VMEM((2, HEADS, HEAD_K, HEAD_V), F32),
                        pltpu.VMEM((2, 2, EXP_ROWS, HGRN_W), F32),
                        pltpu.VMEM((2, 2, CHUNK, HGRN_W), F32),
                        pltpu.VMEM((2, HEADS, CHUNK, CHUNK), F32),
                        pltpu.SemaphoreType.DMA],
        compiler_params=_params(),
        name="hgrn",
    )(proj, proj, proj, proj, proj, proj, hgrn_lb, s0, mall, masks)


def _window_bounds(n, w):
    pos = np.arange(n)
    lo = np.clip(pos - w // 2, 0, n - 1)
    hi = np.clip(pos - w // 2 + w - 1, 0, n - 1)
    return lo, hi


def _pool_consts():
    seq, img, cnt_seq, cnt_col = [], [], [], []
    for w in POOL_WINDOWS:
        lo, hi = _window_bounds(CTX_LEN, w)
        u = np.arange(CTX_LEN)[None, :]
        seq.append((u >= lo[:, None]) & (u <= hi[:, None]))
        cnt_seq.append(hi - lo + 1)
        lo, hi = _window_bounds(GRID_W, w)
        u = np.arange(GRID_W)[None, :]
        band = (u >= lo[:, None]) & (u <= hi[:, None])
        img.append(np.kron(np.eye(TB // GRID_W, dtype=bool), band))
        cnt_col.append(np.tile(hi - lo + 1, TB // GRID_W))
    a = np.stack([np.stack(seq), np.stack(img)]).astype(np.float32)
    cnt = np.stack([np.stack(cnt_seq), np.stack(cnt_col)]).astype(np.float32)
    cnt = np.broadcast_to(cnt[..., None], cnt.shape + (POOL_G,))
    return jnp.asarray(a, BF16), jnp.asarray(cnt, F32)


POOL_ROWS = LAT_LEN


def _pool_kernel(u_ref, a_ref, cnt_ref, wp_ref, ps_ref, o_ref, cp_ref):
    i = pl.program_id(0)
    nblk = POOL_ROWS // TB

    def finish(g, r0, nrows, pm):
        sl = slice(g * POOL_G, (g + 1) * POOL_G)
        d = pm - u_ref[pl.ds(r0, nrows), sl]
        y = _dot(d.astype(BF16), wp_ref[g]) * ps_ref[:, sl]
        o_ref[pl.ds(r0, nrows), sl] = y.astype(o_ref.dtype)

    def window_sum(kind, g, b):
        sl = slice(g * POOL_G, (g + 1) * POOL_G)
        hi, lo = _split2(u_ref[pl.ds(b * TB, TB), sl])
        a = a_ref[kind, g]
        return (_dot(a, hi) + _dot(a, lo)) / cnt_ref[kind, g]

    @pl.when(i < T_CTX // POOL_ROWS)
    def _():
        for g in range(len(POOL_WINDOWS)):
            for b in range(nblk):
                finish(g, b * TB, TB, window_sum(0, g, b))

    @pl.when(i >= T_CTX // POOL_ROWS)
    def _():
        for g, w in enumerate(POOL_WINDOWS):
            for b in range(nblk):
                cp_ref[pl.ds(b * TB, TB), :] = window_sum(1, g, b)
            lo, hi = _window_bounds(GRID_H, w)
            for r in range(GRID_H):
                acc = cp_ref[pl.ds(int(lo[r]) * GRID_W, GRID_W), :]
                for rr in range(int(lo[r]) + 1, int(hi[r]) + 1):
                    acc = acc + cp_ref[pl.ds(rr * GRID_W, GRID_W), :]
                finish(g, r * GRID_W, GRID_W, acc / float(hi[r] - lo[r] + 1))


def _pool(proj, a_pool, cnt_pool, w_pool_bf, pool_scale):
    col = (5 * HGRN_W) // POOL_W
    return pl.pallas_call(
        _pool_kernel,
        out_shape=jax.ShapeDtypeStruct((T_ALL, POOL_W), BF16),
        grid=(T_ALL // POOL_ROWS,),
        in_specs=[pl.BlockSpec((POOL_ROWS, POOL_W), lambda i: (i, col)),
                  pl.BlockSpec((2, 4, TB, TB), lambda i: (0, 0, 0, 0)),
                  pl.BlockSpec((2, 4, TB, POOL_G), lambda i: (0, 0, 0, 0)),
                  pl.BlockSpec((4, POOL_G, POOL_G), lambda i: (0, 0, 0)),
                  pl.BlockSpec((1, POOL_W), lambda i: (0, 0))],
        out_specs=pl.BlockSpec((POOL_ROWS, POOL_W), lambda i: (i, 0)),
        scratch_shapes=[pltpu.VMEM((POOL_ROWS, POOL_G), F32)],
        compiler_params=_params(),
        name="pool",
    )(proj, a_pool, cnt_pool, w_pool_bf, pool_scale)


def _merge_kernel(xc_ref, xl_ref, of_ref, ob_ref, og_ref, yb_ref, ga_ref, gb_ref, mod_ref, hn_ref, wa_ref, wb_ref,
                  wo_ref, n2_ref, wrh_ref, wrl_ref, br_ref,
                  x1_ref, h2_ref, te_ref, tw_ref, hist_ref):
    row = _mod_row(pl.program_id(0))
    gate1 = mod_ref[pl.ds(row, 1), pl.ds(2 * D_MODEL, D_MODEL)]
    shift2 = mod_ref[pl.ds(row, 1), pl.ds(3 * D_MODEL, D_MODEL)]
    scale2 = mod_ref[pl.ds(row, 1), pl.ds(4 * D_MODEL, D_MODEL)]

    o = of_ref[...] + ob_ref[...]
    og = og_ref[...]
    ya = jnp.concatenate(
        [_rms(o[:, h * HEAD_V:(h + 1) * HEAD_V], hn_ref[...]) for h in range(HEADS)], axis=1)
    ya = ya * (og * jax.nn.sigmoid(og))
    merged = (jax.nn.sigmoid(ga_ref[...]) * _dot(ya.astype(BF16), wa_ref[...])
              + jax.nn.sigmoid(gb_ref[...]) * _dot(yb_ref[...], wb_ref[...]))
    x1 = _x_block(xc_ref, xl_ref) + gate1 * _dot(merged.astype(BF16), wo_ref[...])
    x1_ref[...] = x1
    h2 = _rms(x1, n2_ref[...]) * (1.0 + scale2) + shift2
    hh, hl = _split2(h2)
    h2_ref[...] = hh

    lt = _dot_nt(wrh_ref[...], hh) + _dot_nt(wrl_ref[...], hh) + _dot_nt(wrh_ref[...], hl) + br_ref[...]
    eidx = lax.broadcasted_iota(I32, (N_EXP, TB), 0)
    vals, idxs, cnt = [], [], jnp.zeros((N_EXP, TB), F32)
    for _ in range(TOP_K):
        m = jnp.max(lt, axis=0, keepdims=True)
        idx = jnp.min(jnp.where(lt == m, eidx, N_EXP), axis=0, keepdims=True)
        sel = eidx == idx
        vals.append(m)
        idxs.append(idx)
        cnt = cnt + sel.astype(F32)
        lt = jnp.where(sel, -jnp.inf, lt)
    ex = [jnp.exp(v - vals[0]) for v in vals]
    den = ex[0] + ex[1] + ex[2] + ex[3]
    tw_ref[0] = jnp.concatenate([e / den for e in ex], axis=0)
    te_ref[0] = jnp.concatenate(idxs, axis=0)
    hist_ref[0] = jnp.sum(cnt, axis=1, keepdims=True).astype(I32)


def _merge(x_ctx, x_lat, o_f, o_b, proj, yb, mod, hgrn_norm, wa_bf, wb_bf, wo_bf, norm2, wr_hi, wr_lo, b_router):
    full = lambda shape: pl.BlockSpec(shape, lambda i: (0,) * len(shape))
    return pl.pallas_call(
        _merge_kernel,
        out_shape=(jax.ShapeDtypeStruct((T_ALL, D_MODEL), F32),
                   jax.ShapeDtypeStruct((T_ALL, D_MODEL), BF16),
                   jax.ShapeDtypeStruct((NB, TOP_K, TB), I32),
                   jax.ShapeDtypeStruct((NB, TOP_K, TB), F32),
                   jax.ShapeDtypeStruct((NB, N_EXP, 1), I32)),
        grid=(NB,),
        in_specs=_x_specs() + [
                  pl.BlockSpec((TB, HGRN_W), lambda i: (i, 0)),
                  pl.BlockSpec((TB, HGRN_W), lambda i: (i, 0)),
                  pl.BlockSpec((TB, HGRN_W), lambda i: (i, 4)),
                  pl.BlockSpec((TB, POOL_W), lambda i: (i, 0)),
                  pl.BlockSpec((TB, D_MODEL), lambda i: (i, 3)),
                  pl.BlockSpec((TB, D_MODEL), lambda i: (i, 4)),
                  full((MOD_ROWS, 6 * D_MODEL)),
                  full((1, HEAD_V)),
                  full((HGRN_W, D_MODEL)),
                  full((POOL_W, D_MODEL)),
                  full((D_MODEL, D_MODEL)),
                  full((1, D_MODEL)),
                  full((N_EXP, D_MODEL)),
                  full((N_EXP, D_MODEL)),
                  full((N_EXP, 1))],
        out_specs=(pl.BlockSpec((TB, D_MODEL), lambda i: (i, 0)),
                   pl.BlockSpec((TB, D_MODEL), lambda i: (i, 0)),
                   pl.BlockSpec((1, TOP_K, TB), lambda i: (i, 0, 0)),
                   pl.BlockSpec((1, TOP_K, TB), lambda i: (i, 0, 0)),
                   pl.BlockSpec((1, N_EXP, 1), lambda i: (i, 0, 0))),
        compiler_params=_params(),
        name="merge",
    )(x_ctx, x_lat, o_f, o_b, proj, yb, proj, proj, mod, hgrn_norm, wa_bf, wb_bf, wo_bf, norm2,
      wr_hi, wr_lo, b_router)


def _local_rows(te_ref, loff_ref, tri_ref):
    te = jnp.concatenate([te_ref[j] for j in range(TBD // TB)], axis=1)
    eidx = lax.broadcasted_iota(I32, (N_EXP, TBD), 0)
    sels = [eidx == te[k:k + 1] for k in range(TOP_K)]
    cnt = sels[0].astype(F32)
    for s in sels[1:]:
        cnt = cnt + s.astype(F32)
    base = _dot(cnt.astype(BF16), tri_ref[...]) + loff_ref[0]
    return [jnp.sum(jnp.where(s, base, 0.0), axis=0, keepdims=True) for s in sels]


def _segment_copies(make_copy, local_off, global_off, units):
    for j in range(SEG_BITS):
        low = (units & ((1 << j) - 1)) * 8

        @pl.when(((units >> j) & 1) == 1)
        def _():
            make_copy(pl.multiple_of(local_off + low, 8), pl.multiple_of(global_off + low, 8), 8 << j).start()


def _pack_pairs(x):
    half = D_MODEL // 2
    lo = lax.bitcast_convert_type(x[:, :half], U32) >> 16
    hi = lax.bitcast_convert_type(x[:, half:], U32) & jnp.uint32(0xFFFF0000)
    return hi | lo


def _unpack_pairs(p):
    lo = lax.bitcast_convert_type(p << 16, F32).astype(BF16)
    hi = lax.bitcast_convert_type(p & jnp.uint32(0xFFFF0000), F32).astype(BF16)
    return jnp.concatenate([lo, hi], axis=1)


def _block_rows(loff_s, seg_s, b):
    last = b * N_EXP + N_EXP - 1
    return pl.multiple_of(loff_s[last] + seg_s[last], 8)


def _dispatch_kernel(loff_s, seg_s, gbase_s, tail_s, h2_ref, te_ref, loffv_ref, tri_ref, xs_hbm,
                     loc, zeros, sem, sem_z):
    b = pl.program_id(0)
    slot = b % 2

    def wait_block(blk, s):
        n = _block_rows(loff_s, seg_s, blk)
        pltpu.make_async_copy(loc.at[s, pl.ds(0, n)], xs_hbm.at[pl.ds(0, n)], sem.at[s]).wait()

    lrow = _local_rows(te_ref, loffv_ref, tri_ref)

    @pl.when(b >= 2)
    def _():
        wait_block(b - 2, slot)

    for r0 in range(0, LROWS, SORT_CHUNK):
        piota = (lax.broadcasted_iota(I32, (SORT_CHUNK, TBD), 0) + r0).astype(F32)
        hit = piota == lrow[0]
        for k in range(1, TOP_K):
            hit = jnp.logical_or(hit, piota == lrow[k])
        sorted_rows = _dot(jnp.where(hit, 1.0, 0.0).astype(BF16), h2_ref[...])
        loc[slot, r0:r0 + SORT_CHUNK, :] = _pack_pairs(sorted_rows)

    def out_copy(a, g, size):
        return pltpu.make_async_copy(loc.at[slot, pl.ds(a, size)], xs_hbm.at[pl.ds(g, size)], sem.at[slot])

    def body(e, carry):
        idx = b * N_EXP + e
        _segment_copies(out_copy, loff_s[idx], gbase_s[idx], seg_s[idx] // 8)
        return carry

    lax.fori_loop(0, N_EXP, body, 0)

    @pl.when(b == NBD - 1)
    def _():
        zeros[...] = jnp.zeros_like(zeros)

        def zero_copy(a, g, size):
            return pltpu.make_async_copy(zeros.at[pl.ds(a, size)], xs_hbm.at[pl.ds(g, size)], sem_z)

        def zbody(e, ztot):
            _segment_copies(zero_copy, 0, tail_s[e], tail_s[N_EXP + e] // 8)
            return ztot + tail_s[N_EXP + e]

        def tbody(t, carry):
            pltpu.make_async_copy(zeros, xs_hbm.at[pl.ds(pl.multiple_of(t * TM, TM), TM)], sem_z).start()
            return carry

        n_used = tail_s[2 * N_EXP]
        lax.fori_loop(n_used, N_TILES, tbody, 0)
        ztot = lax.fori_loop(0, N_EXP, zbody, 0) + (N_TILES - n_used) * TM
        ztot = pl.multiple_of(ztot, 8)

        @pl.when(ztot > 0)
        def _():
            pltpu.make_async_copy(xs_hbm.at[pl.ds(0, ztot)], xs_hbm.at[pl.ds(0, ztot)], sem_z).wait()

        wait_block(b - 1, 1 - slot)
        wait_block(b, slot)


def _dispatch(tables, h2, top_e, tri):
    loff_s, seg_s, gbase_s, tail_s, loff_v = tables
    grid_spec = pltpu.PrefetchScalarGridSpec(
        num_scalar_prefetch=4,
        grid=(NBD,),
        in_specs=[pl.BlockSpec((TBD, D_MODEL), lambda i, *_: (i, 0)),
                  pl.BlockSpec((TBD // TB, TOP_K, TB), lambda i, *_: (i, 0, 0)),
                  pl.BlockSpec((1, N_EXP, 1), lambda i, *_: (i, 0, 0)),
                  pl.BlockSpec((TBD, TBD), lambda i, *_: (0, 0))],
        out_specs=pl.BlockSpec(memory_space=pl.ANY),
        scratch_shapes=[pltpu.VMEM((2, LROWS, D_MODEL // 2), U32),
                        pltpu.VMEM((TM, D_MODEL // 2), U32),
                        pltpu.SemaphoreType.DMA((2,)),
                        pltpu.SemaphoreType.DMA])
    return pl.pallas_call(
        _dispatch_kernel,
        out_shape=jax.ShapeDtypeStruct((N_TILES * TM, D_MODEL // 2), U32),
        grid_spec=grid_spec,
        compiler_params=_params(),
        name="dispatch",
    )(loff_s, seg_s, gbase_s, tail_s, h2, top_e, loff_v, tri)


def _moe_kernel(te_ref, first_ref, par_ref, next_ref, nu_ref, xs_ref, bgu_ref, bd_ref, wgu_hbm, wd_hbm, o_ref,
                wgu_st, wd_st, wgu_bf, wd_bf, sem):
    i = pl.program_id(0)

    def fetch(e, s):
        return (pltpu.make_async_copy(wgu_hbm.at[e], wgu_st.at[s], sem.at[0, s]),
                pltpu.make_async_copy(wd_hbm.at[e], wd_st.at[s], sem.at[1, s]))

    @pl.when(i < nu_ref[0])
    def _():
        @pl.when(first_ref[i] == 1)
        def _():
            s = par_ref[i]

            @pl.when(i == 0)
            def _():
                for cp in fetch(te_ref[0], 0):
                    cp.start()

            for cp in fetch(te_ref[i], s):
                cp.wait()

            @pl.when(next_ref[i] >= 0)
            def _():
                for cp in fetch(next_ref[i], 1 - s):
                    cp.start()

            wgu_bf[...] = wgu_st[s].astype(BF16)
            wd_bf[...] = wd_st[s].astype(BF16)

        gu = _dot(_unpack_pairs(xs_ref[...]), wgu_bf[...]) + bgu_ref[0]
        gate = jnp.minimum(gu[:, :D_FF], SWIGLU_LIMIT)
        up = jnp.clip(gu[:, D_FF:], -SWIGLU_LIMIT, SWIGLU_LIMIT)
        act = (up + 1.0) * gate * jax.nn.sigmoid(SWIGLU_ALPHA * gate)
        out = _dot(act.astype(BF16), wd_bf[...]) + bd_ref[0]
        o_ref[...] = _pack_pairs(out.astype(BF16).astype(F32))

    @pl.when(i >= nu_ref[0])
    def _():
        o_ref[...] = jnp.zeros_like(o_ref)


def _moe(tile_tables, xs, w_gate_up, b_gate_up, w_down, b_down):
    nsp = len(tile_tables)
    row_tile = lambda i, *s: (jnp.minimum(i, s[nsp - 1][0] - 1), 0)
    grid_spec = pltpu.PrefetchScalarGridSpec(
        num_scalar_prefetch=nsp,
        grid=(N_TILES,),
        in_specs=[pl.BlockSpec((TM, D_MODEL // 2), row_tile),
                  pl.BlockSpec((1, 1, 2 * D_FF), lambda i, te, *_: (te[i], 0, 0)),
                  pl.BlockSpec((1, 1, D_MODEL), lambda i, te, *_: (te[i], 0, 0)),
                  pl.BlockSpec(memory_space=pl.ANY),
                  pl.BlockSpec(memory_space=pl.ANY)],
        out_specs=pl.BlockSpec((TM, D_MODEL // 2), lambda i, *_: (i, 0)),
        scratch_shapes=[pltpu.VMEM((2, D_MODEL, 2 * D_FF), F32),
                        pltpu.VMEM((2, D_FF, D_MODEL), F32),
                        pltpu.VMEM((D_MODEL, 2 * D_FF), BF16),
                        pltpu.VMEM((D_FF, D_MODEL), BF16),
                        pltpu.SemaphoreType.DMA((2, 2))])
    return pl.pallas_call(
        _moe_kernel,
        out_shape=jax.ShapeDtypeStruct((N_TILES * TM, D_MODEL // 2), U32),
        grid_spec=grid_spec,
        compiler_params=_params(),
        name="moe",
    )(*tile_tables, xs, b_gate_up.reshape(N_EXP, 1, 2 * D_FF), b_down.reshape(N_EXP, 1, D_MODEL),
      w_gate_up, w_down)


def _final_kernel(loff_s, seg_s, gbase_s, ys_hbm, x1_ref, te_ref, tw_ref, loffv_ref, tri_ref, mod_ref, fn_ref,
                  oc_ref, ol_ref, loc, sem):
    b = pl.program_id(0)
    slot = b % 2

    def start_block(blk, s):
        def in_copy(a, g, size):
            return pltpu.make_async_copy(ys_hbm.at[pl.ds(g, size)], loc.at[s, pl.ds(a, size)], sem.at[s])

        def body(e, carry):
            idx = blk * N_EXP + e
            _segment_copies(in_copy, loff_s[idx], gbase_s[idx], seg_s[idx] // 8)
            return carry

        lax.fori_loop(0, N_EXP, body, 0)

    @pl.when(b == 0)
    def _():
        loc[...] = jnp.zeros_like(loc)
        start_block(0, 0)

    @pl.when(b + 1 < NBD)
    def _():
        start_block(b + 1, 1 - slot)

    lrow = _local_rows(te_ref, loffv_ref, tri_ref)
    tw = jnp.concatenate([tw_ref[j] for j in range(TBD // TB)], axis=1)
    rows8 = jnp.concatenate(lrow + [tw], axis=0)
    cols = jnp.concatenate([rows8, jnp.zeros((128 - 2 * TOP_K, TBD), F32)], axis=0).T

    n = _block_rows(loff_s, seg_s, b)
    pltpu.make_async_copy(ys_hbm.at[pl.ds(0, n)], loc.at[slot, pl.ds(0, n)], sem.at[slot]).wait()

    y = None
    for r0 in range(0, LROWS, SORT_CHUNK):
        piota = (lax.broadcasted_iota(I32, (TBD, SORT_CHUNK), 1) + r0).astype(F32)
        pw = jnp.where(piota == cols[:, 0:1], cols[:, TOP_K:TOP_K + 1], 0.0)
        for k in range(1, TOP_K):
            pw = pw + jnp.where(piota == cols[:, k:k + 1], cols[:, TOP_K + k:TOP_K + k + 1], 0.0)
        part = _dot(pw.astype(BF16), _unpack_pairs(loc[slot, r0:r0 + SORT_CHUNK, :]))
        y = part if y is None else y + part

    row = jnp.where(b < NBD_CTX, N_LAT_SEQ, (b - NBD_CTX) // (LAT_LEN // TBD))
    gate2 = mod_ref[pl.ds(row, 1), pl.ds(5 * D_MODEL, D_MODEL)]
    out = _rms(x1_ref[...] + gate2 * y, fn_ref[...])

    @pl.when(b < NBD_CTX)
    def _():
        oc_ref[...] = out

    @pl.when(b >= NBD_CTX)
    def _():
        ol_ref[...] = out


def _final(tables, ys, x1, top_e, top_w, tri, mod, final_norm):
    loff_s, seg_s, gbase_s, _, loff_v = tables
    grid_spec = pltpu.PrefetchScalarGridSpec(
        num_scalar_prefetch=3,
        grid=(NBD,),
        in_specs=[pl.BlockSpec(memory_space=pl.ANY),
                  pl.BlockSpec((TBD, D_MODEL), lambda i, *_: (i, 0)),
                  pl.BlockSpec((TBD // TB, TOP_K, TB), lambda i, *_: (i, 0, 0)),
                  pl.BlockSpec((TBD // TB, TOP_K, TB), lambda i, *_: (i, 0, 0)),
                  pl.BlockSpec((1, N_EXP, 1), lambda i, *_: (i, 0, 0)),
                  pl.BlockSpec((TBD, TBD), lambda i, *_: (0, 0)),
                  pl.BlockSpec((MOD_ROWS, 6 * D_MODEL), lambda i, *_: (0, 0)),
                  pl.BlockSpec((1, D_MODEL), lambda i, *_: (0, 0))],
        out_specs=(pl.BlockSpec((TBD, D_MODEL), lambda i, *_: (jnp.minimum(i, NBD_CTX - 1), 0)),
                   pl.BlockSpec((TBD, D_MODEL), lambda i, *_: (jnp.maximum(i - NBD_CTX, 0), 0))),
        scratch_shapes=[pltpu.VMEM((2, LROWS, D_MODEL // 2), U32),
                        pltpu.SemaphoreType.DMA((2,))])
    return pl.pallas_call(
        _final_kernel,
        out_shape=(jax.ShapeDtypeStruct((T_CTX, D_MODEL), F32),
                   jax.ShapeDtypeStruct((T_LAT, D_MODEL), F32)),
        grid_spec=grid_spec,
        compiler_params=_params(),
        name="final",
    )(loff_s, seg_s, gbase_s, ys, x1, top_e, top_w, loff_v, tri, mod, final_norm)


def _dispatch_tables(hist):
    hist = jnp.sum(hist.reshape(NBD, TBD // TB, N_EXP), axis=1)
    seg = ((hist + 7) // 8) * 8
    loff = jnp.cumsum(seg, axis=1) - seg
    rows_e = jnp.sum(seg, axis=0)
    region = ((rows_e + TM - 1) // TM) * TM
    region_end = jnp.cumsum(region)
    region_start = region_end - region
    gbase = region_start[None, :] + jnp.cumsum(seg, axis=0) - seg
    n_used = (region_end[-1] // TM).astype(I32)
    tail = jnp.concatenate([region_start + rows_e, region - rows_e, n_used.reshape(1)])
    start = jnp.arange(N_TILES, dtype=I32) * TM
    tile_e = jnp.sum((start[:, None] >= region_end[None, :]).astype(I32), axis=1)
    tile_e = jnp.minimum(tile_e, tile_e[jnp.maximum(n_used - 1, 0)])
    first = jnp.concatenate([jnp.ones((1,), I32), (tile_e[1:] != tile_e[:-1]).astype(I32)])
    parity = (jnp.cumsum(first) - 1) % 2
    later = jnp.where(tile_e[None, :] > tile_e[:, None], tile_e[None, :], N_EXP)
    nxt = jnp.min(later, axis=1)
    nxt = jnp.where(nxt == N_EXP, -1, nxt)
    flat = lambda a: a.reshape(-1).astype(I32)
    tables = (flat(loff), flat(seg), flat(gbase), flat(tail), loff.astype(F32).reshape(NBD, N_EXP, 1))
    tile_tables = (flat(tile_e), flat(first), flat(parity), flat(nxt), n_used.reshape(1))
    return tables, tile_tables


def kernel(x_prompt, x_sample, state_hgrn, c, c_ctx, w_ada, b_ada, norm1, w_in, hgrn_lb, hgrn_norm, w_pool,
           pool_scale, w_branch_a, w_branch_b, w_out, norm2, w_router, b_router, w_gate_up, b_gate_up,
           w_down, b_down, final_norm):
    x_ctx = x_prompt.reshape(T_CTX, D_MODEL)
    x_lat = x_sample.reshape(T_LAT, D_MODEL)
    cc = jnp.zeros((MOD_ROWS, D_MODEL), F32).at[:N_LAT_SEQ].set(c).at[N_LAT_SEQ].set(c_ctx)
    mod = _ada(cc, w_ada[0], b_ada)

    proj = _inproj(x_ctx, x_lat, mod, norm1, w_in[0].astype(BF16))

    mall, masks = _hgrn_consts()
    o_f, o_b, new_state = _hgrn(proj, hgrn_lb, state_hgrn[:, 0], mall, masks)

    a_pool, cnt_pool = _pool_consts()
    yb = _pool(proj, a_pool, cnt_pool, w_pool[0].astype(BF16), pool_scale)

    wr_t = w_router[0].T
    wr_hi = wr_t.astype(BF16)
    wr_lo = (wr_t - wr_hi.astype(F32)).astype(BF16)
    tri = jnp.asarray(np.triu(np.ones((TBD, TBD), np.float32), 1), BF16)
    x1, h2, top_e, top_w, hist = _merge(
        x_ctx, x_lat, o_f, o_b, proj, yb, mod, hgrn_norm, w_branch_a[0].astype(BF16), w_branch_b[0].astype(BF16),
        w_out[0].astype(BF16), norm2, wr_hi, wr_lo, b_router.reshape(N_EXP, 1))

    tables, tile_tables = _dispatch_tables(hist)
    xs = _dispatch(tables, h2, top_e, tri)
    ys = _moe(tile_tables, xs, w_gate_up[0], b_gate_up[0], w_down[0], b_down[0])
    y_ctx, y_lat = _final(tables, ys, x1, top_e, top_w, tri, mod, final_norm.reshape(1, D_MODEL))
    y_prompt = y_ctx.reshape(N_CTX_SEQ, CTX_LEN, D_MODEL)
    y_sample = y_lat.reshape(N_LAT_SEQ, LAT_LEN, D_MODEL)
    return y_prompt, y_sample, new_state[:, None]
```

```python
import functools

import numpy as np
import jax
import jax.numpy as jnp
from jax import lax
from jax.experimental import pallas as pl
from jax.experimental.pallas import tpu as pltpu

F32 = jnp.float32
BF16 = jnp.bfloat16
I32 = jnp.int32
U32 = jnp.uint32

D_MODEL = 1024
N_CTX_SEQ, CTX_LEN = 32, 256
N_LAT_SEQ, LAT_LEN = 4, 2048
T_CTX = N_CTX_SEQ * CTX_LEN
T_LAT = N_LAT_SEQ * LAT_LEN
T_ALL = T_CTX + T_LAT
TB = 256
NB = T_ALL // TB
NB_CTX = T_CTX // TB
LAT_BLOCKS = LAT_LEN // TB
HEADS, HEAD_K, HEAD_V = 4, 128, 128
HGRN_W = HEADS * HEAD_V
POOL_WINDOWS = (2, 4, 8, 16)
POOL_G = 128
POOL_W = len(POOL_WINDOWS) * POOL_G
GRID_W = 64
GRID_H = LAT_LEN // GRID_W
IN_W = 5 * HGRN_W + POOL_W + 2 * D_MODEL
GATE_W = 2 * HGRN_W
REST_W = IN_W - GATE_W
N_EXP, TOP_K, D_FF = 32, 4, 1024
SWIGLU_LIMIT = 7.0
SWIGLU_ALPHA = 1.702
EPS = 1e-6
LOG2_E = 1.4426950408889634
CHUNK = 64
N_LEVELS = 6
EXP_ROWS = (N_LEVELS + 2) * CHUNK
TM = 512
TBD = 512
NBD = T_ALL // TBD
NBD_CTX = T_CTX // TBD
SORT_CHUNK = 256
LROWS = TBD * TOP_K + N_EXP * 8
SEG_BITS = 7
N_TILES = -(-(T_ALL * TOP_K + NBD * N_EXP * 7 + N_EXP * (TM - 1)) // TM)
MOD_ROWS = 8
VMEM_LIMIT = 56 * 1024 * 1024


def _params(sem=("arbitrary",)):
    return pltpu.CompilerParams(dimension_semantics=sem, vmem_limit_bytes=VMEM_LIMIT)


def _dot(a, b):
    return jnp.dot(a, b, preferred_element_type=F32)


def _dot_nt(a, b):
    return lax.dot_general(a, b, (((1,), (1,)), ((), ())), preferred_element_type=F32)


def _dot_tn(a, b):
    return lax.dot_general(a, b, (((0,), (0,)), ((), ())), preferred_element_type=F32)


def _split2(x):
    hi = x.astype(BF16)
    lo = (x - hi.astype(F32)).astype(BF16)
    return hi, lo


def _mod_row(i):
    return jnp.where(i < NB_CTX, N_LAT_SEQ, (i - NB_CTX) // LAT_BLOCKS)


def _ada_kernel(c_ref, w_ref, b_ref, o_ref):
    c = c_ref[...]
    s = c * jax.nn.sigmoid(c)
    o_ref[...] = jnp.dot(s, w_ref[...], preferred_element_type=F32,
                         precision=lax.Precision.HIGHEST) + b_ref[...]


def _ada(cc, w_ada, b_ada):
    nblk = 1536
    return pl.pallas_call(
        _ada_kernel,
        out_shape=jax.ShapeDtypeStruct((MOD_ROWS, 6 * D_MODEL), F32),
        grid=(6 * D_MODEL // nblk,),
        in_specs=[pl.BlockSpec((MOD_ROWS, D_MODEL), lambda j: (0, 0)),
                  pl.BlockSpec((D_MODEL, nblk), lambda j: (0, j)),
                  pl.BlockSpec((1, nblk), lambda j: (0, j))],
        out_specs=pl.BlockSpec((MOD_ROWS, nblk), lambda j: (0, j)),
        compiler_params=_params(),
        name="ada",
    )(cc, w_ada, b_ada)


def _rms(x, g):
    ms = jnp.mean(x * x, axis=-1, keepdims=True)
    return x * lax.rsqrt(ms + EPS) * g


def _x_specs():
    return [pl.BlockSpec((TB, D_MODEL), lambda i, *_: (jnp.minimum(i, NB_CTX - 1), 0)),
            pl.BlockSpec((TB, D_MODEL), lambda i, *_: (jnp.maximum(i - NB_CTX, 0), 0))]


def _x_block(xc_ref, xl_ref):
    return jnp.where(pl.program_id(0) < NB_CTX, xc_ref[...], xl_ref[...])


def _inproj_kernel(xc_ref, xl_ref, mod_ref, n1_ref, wg_ref, wr_ref, og_ref, or_ref):
    row = _mod_row(pl.program_id(0))
    shift = mod_ref[pl.ds(row, 1), pl.ds(0, D_MODEL)]
    scale = mod_ref[pl.ds(row, 1), pl.ds(D_MODEL, D_MODEL)]
    h = (_rms(_x_block(xc_ref, xl_ref), n1_ref[...]) * (1.0 + scale) + shift).astype(BF16)
    og_ref[...] = _dot(h, wg_ref[...])
    or_ref[...] = _dot(h, wr_ref[...]).astype(BF16)


def _inproj(x_ctx, x_lat, mod, norm1, w_gates_bf, w_rest_bf):
    return pl.pallas_call(
        _inproj_kernel,
        out_shape=(jax.ShapeDtypeStruct((T_ALL, GATE_W), F32),
                   jax.ShapeDtypeStruct((T_ALL, REST_W), BF16)),
        grid=(NB,),
        in_specs=_x_specs() + [
                  pl.BlockSpec((MOD_ROWS, 6 * D_MODEL), lambda i: (0, 0)),
                  pl.BlockSpec((1, D_MODEL), lambda i: (0, 0)),
                  pl.BlockSpec((D_MODEL, GATE_W), lambda i: (0, 0)),
                  pl.BlockSpec((D_MODEL, REST_W), lambda i: (0, 0))],
        out_specs=(pl.BlockSpec((TB, GATE_W), lambda i: (i, 0)),
                   pl.BlockSpec((TB, REST_W), lambda i: (i, 0))),
        compiler_params=_params(),
        name="inproj",
    )(x_ctx, x_lat, mod, norm1, w_gates_bf, w_rest_bf)


def _hgrn_consts():
    c = CHUNK
    t = np.arange(c)[:, None]
    u = np.arange(c)[None, :]
    blocks = [u <= t]
    masks = [np.eye(c, dtype=bool)]
    h = c // 2
    while h >= 1:
        bi = t // h
        upper = (bi % 2) == 1
        e_up = (u >= bi * h) & (u <= t)
        e_lo = (u > t) & (u <= bi * h + h - 1)
        blocks.append(np.where(upper, e_up, e_lo))
        masks.append(((t // (2 * h)) == (u // (2 * h))) & (((t // h) % 2) == 1) & (((u // h) % 2) == 0))
        h //= 2
    blocks.append(u > t)
    m_f = np.stack(blocks).astype(np.float32)
    k_f = np.stack(masks).astype(np.float32)
    m_b = m_f[:, ::-1, ::-1]
    k_b = k_f[:, ::-1, ::-1]
    m = np.stack([m_f.reshape(EXP_ROWS, c), m_b.reshape(EXP_ROWS, c)])
    m3 = np.concatenate([m, m, m], axis=2)
    return jnp.asarray(m3, BF16), jnp.asarray(np.stack([k_f, k_b]), F32)


def _hgrn_block(dirs, lb, mall_ref, mask_ref, st_ref, z_ref, k_ref, sc_ref):
    c = CHUNK
    nchunk = TB // c
    units = [(d, h) for d in range(2) for h in range(HEADS)]
    sl = [slice(h * HEAD_K, (h + 1) * HEAD_K) for h in range(HEADS)]

    def rows(ci, d):
        r0 = ci * c if d == 0 else (nchunk - 1 - ci) * c
        return slice(r0, r0 + c)

    def exponents(ci):
        s = ci % 2
        for d in range(2):
            f = lb[d:d + 1] + (1.0 - lb[d:d + 1]) * jax.nn.sigmoid(dirs[d][1][rows(ci, d), :])
            k_ref[s, d] = 1.0 - f
            g = jnp.log(f) * LOG2_E
            g1 = g.astype(BF16)
            r1 = g - g1.astype(F32)
            g2 = r1.astype(BF16)
            g3 = (r1 - g2.astype(F32)).astype(BF16)
            gsplit = jnp.concatenate([g1, g2, g3], axis=0)
            z_ref[s, d] = jnp.exp2(_dot(mall_ref[d], gsplit))

    def q_of(ci, d, h):
        return dirs[d][0][rows(ci, d), sl[h]]

    def v_of(ci, d, h):
        return dirs[d][2][rows(ci, d), sl[h]].astype(BF16)

    def qz(ci, d, h, blk):
        return (q_of(ci, d, h) * z_ref[ci % 2, d, blk * c:(blk + 1) * c, sl[h]]).astype(BF16)

    def kz(ci, d, h, blk):
        return (k_ref[ci % 2, d, :, sl[h]] * z_ref[ci % 2, d, blk * c:(blk + 1) * c, sl[h]]).astype(BF16)

    def levels(ci):
        for d, h in units:
            sc_ref[d, h] = mask_ref[d, 0] * _dot_nt(q_of(ci, d, h).astype(BF16),
                                                    k_ref[ci % 2, d, :, sl[h]].astype(BF16))
        for lev in range(N_LEVELS):
            for d, h in units:
                sc_ref[d, h] += mask_ref[d, lev + 1] * _dot_nt(qz(ci, d, h, lev + 1), kz(ci, d, h, lev + 1))

    def tail(ci):
        for d, h in units:
            o = (_dot_nt(qz(ci, d, h, 0), st_ref[d, h].astype(BF16))
                 + _dot(sc_ref[d, h].astype(BF16), v_of(ci, d, h)))
            dirs[d][3][rows(ci, d), sl[h]] = o * (HEAD_K ** -0.5)
        for d, h in units:
            tot_row = c - 1 if d == 0 else 0
            decay = z_ref[ci % 2, d, tot_row:tot_row + 1, sl[h]]
            st_ref[d, h] = st_ref[d, h] * decay + _dot_tn(v_of(ci, d, h), kz(ci, d, h, N_LEVELS + 1))

    exponents(0)
    for ci in range(nchunk):
        levels(ci)
        if ci + 1 < nchunk:
            exponents(ci + 1)
        tail(ci)


def _hgrn_kernel(qf_ref, ff_ref, vf_ref, qb_ref, fb_ref, vb_ref, lbraw_ref, s0_ref, mall_ref, mask_ref,
                 of_ref, ob_ref, sout_hbm, st_ref, stage_ref, z_ref, k_ref, sc_ref, sem):
    i = pl.program_id(0)
    j = (i - NB_CTX) % LAT_BLOCKS
    is_ctx = i < NB_CTX

    @pl.when(is_ctx)
    def _():
        st_ref[...] = jnp.zeros_like(st_ref)

    @pl.when(jnp.logical_and(jnp.logical_not(is_ctx), j == 0))
    def _():
        for d in range(2):
            for h in range(HEADS):
                st_ref[d, h] = s0_ref[0, d, h].T

    a0 = lbraw_ref[0]
    a1 = lbraw_ref[1]
    mx = jnp.maximum(a0, a1)
    e0 = jnp.exp(a0 - mx)
    e1 = jnp.exp(a1 - mx)
    lb = e0 / (e0 + e1)

    dirs = ((qf_ref, ff_ref, vf_ref, of_ref), (qb_ref, fb_ref, vb_ref, ob_ref))
    _hgrn_block(dirs, lb, mall_ref, mask_ref, st_ref, z_ref, k_ref, sc_ref)

    @pl.when(is_ctx)
    def _():
        for d in range(2):
            for h in range(HEADS):
                stage_ref[d, h] = st_ref[d, h].T
        cp = pltpu.make_async_copy(stage_ref, sout_hbm.at[i], sem)
        cp.start()
        cp.wait()


def _bwd_block(i):
    j = (i - NB_CTX) % LAT_BLOCKS
    return jnp.where(i < NB_CTX, i, i - j + (LAT_BLOCKS - 1 - j))


def _hgrn(gates, rest, hgrn_lb, s0, mall, masks):
    nh = HGRN_W
    fwd = lambda col: pl.BlockSpec((TB, nh), lambda i: (i, col))
    bwd = lambda col: pl.BlockSpec((TB, nh), lambda i: (_bwd_block(i), col))
    lat_seq = lambda i: jnp.clip((i - NB_CTX) // LAT_BLOCKS, 0, N_LAT_SEQ - 1)
    return pl.pallas_call(
        _hgrn_kernel,
        out_shape=(jax.ShapeDtypeStruct((T_ALL, nh), F32),
                   jax.ShapeDtypeStruct((T_ALL, nh), F32),
                   jax.ShapeDtypeStruct((N_CTX_SEQ, 2, HEADS, HEAD_K, HEAD_V), F32)),
        grid=(NB,),
        in_specs=[fwd(0), fwd(0), fwd(1), bwd(0), bwd(1), bwd(1),
                  pl.BlockSpec((2, 2, nh), lambda i: (0, 0, 0)),
                  pl.BlockSpec((1, 2, HEADS, HEAD_K, HEAD_V), lambda i: (lat_seq(i), 0, 0, 0, 0)),
                  pl.BlockSpec((2, EXP_ROWS, 3 * CHUNK), lambda i: (0, 0, 0)),
                  pl.BlockSpec((2, N_LEVELS + 1, CHUNK, CHUNK), lambda i: (0, 0, 0, 0))],
        out_specs=(pl.BlockSpec((TB, nh), lambda i: (i, 0)),
                   pl.BlockSpec((TB, nh), lambda i: (_bwd_block(i), 0)),
                   pl.BlockSpec(memory_space=pl.ANY)),
        scratch_shapes=[pltpu.VMEM((2, HEADS, HEAD_V, HEAD_K), F32),
                        pltpu.VMEM((2, HEADS, HEAD_K, HEAD_V), F32),
                        pltpu.VMEM((2, 2, EXP_ROWS, HGRN_W), F32),
                        pltpu.VMEM((2, 2, CHUNK, HGRN_W), F32),
                        pltpu.VMEM((2, HEADS, CHUNK, CHUNK), F32),
                        pltpu.SemaphoreType.DMA],
        compiler_params=_params(),
        name="hgrn",
    )(rest, gates, rest, rest, gates, rest, hgrn_lb, s0, mall, masks)


def _window_bounds(n, w):
    pos = np.arange(n)
    lo = np.clip(pos - w // 2, 0, n - 1)
    hi = np.clip(pos - w // 2 + w - 1, 0, n - 1)
    return lo, hi


def _pool_consts():
    seq, img, cnt_seq, cnt_col = [], [], [], []
    for w in POOL_WINDOWS:
        lo, hi = _window_bounds(CTX_LEN, w)
        u = np.arange(CTX_LEN)[None, :]
        seq.append((u >= lo[:, None]) & (u <= hi[:, None]))
        cnt_seq.append(hi - lo + 1)
        lo, hi = _window_bounds(GRID_W, w)
        u = np.arange(GRID_W)[None, :]
        band = (u >= lo[:, None]) & (u <= hi[:, None])
        img.append(np.kron(np.eye(TB // GRID_W, dtype=bool), band))
        cnt_col.append(np.tile(hi - lo + 1, TB // GRID_W))
    a = np.stack([np.stack(seq), np.stack(img)]).astype(np.float32)
    cnt = np.stack([np.stack(cnt_seq), np.stack(cnt_col)]).astype(np.float32)
    cnt = np.broadcast_to(cnt[..., None], cnt.shape + (POOL_G,))
    return jnp.asarray(a, BF16), jnp.asarray(cnt, F32)


POOL_ROWS = LAT_LEN


def _pool_kernel(u_ref, a_ref, cnt_ref, wp_ref, ps_ref, o_ref, cp_ref):
    i = pl.program_id(0)
    nblk = POOL_ROWS // TB

    def finish(g, r0, nrows, pm):
        sl = slice(g * POOL_G, (g + 1) * POOL_G)
        d = pm - u_ref[pl.ds(r0, nrows), sl].astype(F32)
        y = _dot(d.astype(BF16), wp_ref[g]) * ps_ref[:, sl]
        o_ref[pl.ds(r0, nrows), sl] = y.astype(o_ref.dtype)

    def window_sum(kind, g, b):
        sl = slice(g * POOL_G, (g + 1) * POOL_G)
        return _dot(a_ref[kind, g], u_ref[pl.ds(b * TB, TB), sl]) / cnt_ref[kind, g]

    @pl.when(i < T_CTX // POOL_ROWS)
    def _():
        for g in range(len(POOL_WINDOWS)):
            for b in range(nblk):
                finish(g, b * TB, TB, window_sum(0, g, b))

    @pl.when(i >= T_CTX // POOL_ROWS)
    def _():
        for g, w in enumerate(POOL_WINDOWS):
            for b in range(nblk):
                cp_ref[pl.ds(b * TB, TB), :] = window_sum(1, g, b)
            lo, hi = _window_bounds(GRID_H, w)
            for r in range(GRID_H):
                acc = cp_ref[pl.ds(int(lo[r]) * GRID_W, GRID_W), :]
                for rr in range(int(lo[r]) + 1, int(hi[r]) + 1):
                    acc = acc + cp_ref[pl.ds(rr * GRID_W, GRID_W), :]
                finish(g, r * GRID_W, GRID_W, acc / float(hi[r] - lo[r] + 1))


def _pool(rest, a_pool, cnt_pool, w_pool_bf, pool_scale):
    col = 3
    return pl.pallas_call(
        _pool_kernel,
        out_shape=jax.ShapeDtypeStruct((T_ALL, POOL_W), BF16),
        grid=(T_ALL // POOL_ROWS,),
        in_specs=[pl.BlockSpec((POOL_ROWS, POOL_W), lambda i: (i, col)),
                  pl.BlockSpec((2, 4, TB, TB), lambda i: (0, 0, 0, 0)),
                  pl.BlockSpec((2, 4, TB, POOL_G), lambda i: (0, 0, 0, 0)),
                  pl.BlockSpec((4, POOL_G, POOL_G), lambda i: (0, 0, 0)),
                  pl.BlockSpec((1, POOL_W), lambda i: (0, 0))],
        out_specs=pl.BlockSpec((POOL_ROWS, POOL_W), lambda i: (i, 0)),
        scratch_shapes=[pltpu.VMEM((POOL_ROWS, POOL_G), F32)],
        compiler_params=_params(),
        name="pool",
    )(rest, a_pool, cnt_pool, w_pool_bf, pool_scale)


def _merge_kernel(xc_ref, xl_ref, of_ref, ob_ref, og_ref, yb_ref, ga_ref, gb_ref, mod_ref, hn_ref, wa_ref, wb_ref,
                  wo_ref, n2_ref, wrh_ref, wrl_ref, br_ref,
                  x1_ref, h2_ref, te_ref, tw_ref, hist_ref):
    row = _mod_row(pl.program_id(0))
    gate1 = mod_ref[pl.ds(row, 1), pl.ds(2 * D_MODEL, D_MODEL)]
    shift2 = mod_ref[pl.ds(row, 1), pl.ds(3 * D_MODEL, D_MODEL)]
    scale2 = mod_ref[pl.ds(row, 1), pl.ds(4 * D_MODEL, D_MODEL)]

    o = of_ref[...] + ob_ref[...]
    og = og_ref[...].astype(F32)
    ya = jnp.concatenate(
        [_rms(o[:, h * HEAD_V:(h + 1) * HEAD_V], hn_ref[...]) for h in range(HEADS)], axis=1)
    ya = ya * (og * jax.nn.sigmoid(og))
    merged = (jax.nn.sigmoid(ga_ref[...].astype(F32)) * _dot(ya.astype(BF16), wa_ref[...])
              + jax.nn.sigmoid(gb_ref[...].astype(F32)) * _dot(yb_ref[...], wb_ref[...]))
    x1 = _x_block(xc_ref, xl_ref) + gate1 * _dot(merged.astype(BF16), wo_ref[...])
    x1_ref[...] = x1
    h2 = _rms(x1, n2_ref[...]) * (1.0 + scale2) + shift2
    hh, hl = _split2(h2)
    h2_ref[...] = hh

    lt = _dot_nt(wrh_ref[...], hh) + _dot_nt(wrl_ref[...], hh) + _dot_nt(wrh_ref[...], hl) + br_ref[...]
    eidx = lax.broadcasted_iota(I32, (N_EXP, TB), 0)
    vals, idxs, cnt = [], [], jnp.zeros((N_EXP, TB), F32)
    for _ in range(TOP_K):
        m = jnp.max(lt, axis=0, keepdims=True)
        idx = jnp.min(jnp.where(lt == m, eidx, N_EXP), axis=0, keepdims=True)
        sel = eidx == idx
        vals.append(m)
        idxs.append(idx)
        cnt = cnt + sel.astype(F32)
        lt = jnp.where(sel, -jnp.inf, lt)
    ex = [jnp.exp(v - vals[0]) for v in vals]
    den = ex[0] + ex[1] + ex[2] + ex[3]
    tw_ref[0] = jnp.concatenate([e / den for e in ex], axis=0)
    te_ref[0] = jnp.concatenate(idxs, axis=0)
    hist_ref[0] = jnp.sum(cnt, axis=1, keepdims=True).astype(I32)


def _merge(x_ctx, x_lat, o_f, o_b, rest, yb, mod, hgrn_norm, wa_bf, wb_bf, wo_bf, norm2, wr_hi, wr_lo, b_router):
    full = lambda shape: pl.BlockSpec(shape, lambda i: (0,) * len(shape))
    return pl.pallas_call(
        _merge_kernel,
        out_shape=(jax.ShapeDtypeStruct((T_ALL, D_MODEL), F32),
                   jax.ShapeDtypeStruct((T_ALL, D_MODEL), BF16),
                   jax.ShapeDtypeStruct((NB, TOP_K, TB), I32),
                   jax.ShapeDtypeStruct((NB, TOP_K, TB), F32),
                   jax.ShapeDtypeStruct((NB, N_EXP, 1), I32)),
        grid=(NB,),
        in_specs=_x_specs() + [
                  pl.BlockSpec((TB, HGRN_W), lambda i: (i, 0)),
                  pl.BlockSpec((TB, HGRN_W), lambda i: (i, 0)),
                  pl.BlockSpec((TB, HGRN_W), lambda i: (i, 2)),
                  pl.BlockSpec((TB, POOL_W), lambda i: (i, 0)),
                  pl.BlockSpec((TB, D_MODEL), lambda i: (i, 2)),
                  pl.BlockSpec((TB, D_MODEL), lambda i: (i, 3)),
                  full((MOD_ROWS, 6 * D_MODEL)),
                  full((1, HEAD_V)),
                  full((HGRN_W, D_MODEL)),
                  full((POOL_W, D_MODEL)),
                  full((D_MODEL, D_MODEL)),
                  full((1, D_MODEL)),
                  full((N_EXP, D_MODEL)),
                  full((N_EXP, D_MODEL)),
                  full((N_EXP, 1))],
        out_specs=(pl.BlockSpec((TB, D_MODEL), lambda i: (i, 0)),
                   pl.BlockSpec((TB, D_MODEL), lambda i: (i, 0)),
                   pl.BlockSpec((1, TOP_K, TB), lambda i: (i, 0, 0)),
                   pl.BlockSpec((1, TOP_K, TB), lambda i: (i, 0, 0)),
                   pl.BlockSpec((1, N_EXP, 1), lambda i: (i, 0, 0))),
        compiler_params=_params(),
        name="merge",
    )(x_ctx, x_lat, o_f, o_b, rest, yb, rest, rest, mod, hgrn_norm, wa_bf, wb_bf, wo_bf, norm2,
      wr_hi, wr_lo, b_router)


def _local_rows(te_ref, loff_ref, tri_ref):
    te = jnp.concatenate([te_ref[j] for j in range(TBD // TB)], axis=1)
    eidx = lax.broadcasted_iota(I32, (N_EXP, TBD), 0)
    sels = [eidx == te[k:k + 1] for k in range(TOP_K)]
    cnt = sels[0].astype(F32)
    for s in sels[1:]:
        cnt = cnt + s.astype(F32)
    base = _dot(cnt.astype(BF16), tri_ref[...]) + loff_ref[0]
    return [jnp.sum(jnp.where(s, base, 0.0), axis=0, keepdims=True) for s in sels]


def _chunk_relative(rows, r0):
    out = []
    for r in rows:
        inside = jnp.logical_and(r >= r0, r < r0 + SORT_CHUNK)
        out.append(jnp.where(inside, r - r0, -1.0).astype(BF16))
    return out


def _segment_copies(make_copy, local_off, global_off, units):
    for j in range(SEG_BITS):
        low = (units & ((1 << j) - 1)) * 8

        @pl.when(((units >> j) & 1) == 1)
        def _():
            make_copy(pl.multiple_of(local_off + low, 8), pl.multiple_of(global_off + low, 8), 8 << j).start()


def _pack_pairs(x):
    half = D_MODEL // 2
    lo = lax.bitcast_convert_type(x[:, :half], U32) >> 16
    hi = lax.bitcast_convert_type(x[:, half:], U32) & jnp.uint32(0xFFFF0000)
    return hi | lo


def _unpack_pairs(p):
    lo = lax.bitcast_convert_type(p << 16, F32).astype(BF16)
    hi = lax.bitcast_convert_type(p & jnp.uint32(0xFFFF0000), F32).astype(BF16)
    return jnp.concatenate([lo, hi], axis=1)


def _block_rows(loff_s, seg_s, b):
    last = b * N_EXP + N_EXP - 1
    return pl.multiple_of(loff_s[last] + seg_s[last], 8)


def _dispatch_kernel(loff_s, seg_s, gbase_s, tail_s, h2_ref, te_ref, loffv_ref, tri_ref, iota_ref, xs_hbm,
                     loc, zeros, sem, sem_z):
    b = pl.program_id(0)
    slot = b % 2

    def wait_block(blk, s):
        n = _block_rows(loff_s, seg_s, blk)
        pltpu.make_async_copy(loc.at[s, pl.ds(0, n)], xs_hbm.at[pl.ds(0, n)], sem.at[s]).wait()

    lrow = _local_rows(te_ref, loffv_ref, tri_ref)

    @pl.when(b >= 2)
    def _():
        wait_block(b - 2, slot)

    for r0 in range(0, LROWS, SORT_CHUNK):
        rel = _chunk_relative(lrow, r0)
        p = jnp.zeros((SORT_CHUNK, TBD), BF16)
        for k in reversed(range(TOP_K)):
            p = jnp.where(iota_ref[...] == rel[k], jnp.ones_like(p), p)
        loc[slot, r0:r0 + SORT_CHUNK, :] = _pack_pairs(_dot(p, h2_ref[...]))

    def out_copy(a, g, size):
        return pltpu.make_async_copy(loc.at[slot, pl.ds(a, size)], xs_hbm.at[pl.ds(g, size)], sem.at[slot])

    def body(e, carry):
        idx = b * N_EXP + e
        _segment_copies(out_copy, loff_s[idx], gbase_s[idx], seg_s[idx] // 8)
        return carry

    lax.fori_loop(0, N_EXP, body, 0)

    @pl.when(b == NBD - 1)
    def _():
        zeros[...] = jnp.zeros_like(zeros)

        def zero_copy(a, g, size):
            return pltpu.make_async_copy(zeros.at[pl.ds(a, size)], xs_hbm.at[pl.ds(g, size)], sem_z)

        def zbody(e, ztot):
            _segment_copies(zero_copy, 0, tail_s[e], tail_s[N_EXP + e] // 8)
            return ztot + tail_s[N_EXP + e]

        def tbody(t, carry):
            pltpu.make_async_copy(zeros, xs_hbm.at[pl.ds(pl.multiple_of(t * TM, TM), TM)], sem_z).start()
            return carry

        n_used = tail_s[2 * N_EXP]
        lax.fori_loop(n_used, N_TILES, tbody, 0)
        ztot = lax.fori_loop(0, N_EXP, zbody, 0) + (N_TILES - n_used) * TM
        ztot = pl.multiple_of(ztot, 8)

        @pl.when(ztot > 0)
        def _():
            pltpu.make_async_copy(xs_hbm.at[pl.ds(0, ztot)], xs_hbm.at[pl.ds(0, ztot)], sem_z).wait()

        wait_block(b - 1, 1 - slot)
        wait_block(b, slot)


def _dispatch(tables, h2, top_e, tri, row_iota):
    loff_s, seg_s, gbase_s, tail_s, loff_v = tables
    grid_spec = pltpu.PrefetchScalarGridSpec(
        num_scalar_prefetch=4,
        grid=(NBD,),
        in_specs=[pl.BlockSpec((TBD, D_MODEL), lambda i, *_: (i, 0)),
                  pl.BlockSpec((TBD // TB, TOP_K, TB), lambda i, *_: (i, 0, 0)),
                  pl.BlockSpec((1, N_EXP, 1), lambda i, *_: (i, 0, 0)),
                  pl.BlockSpec((TBD, TBD), lambda i, *_: (0, 0)),
                  pl.BlockSpec((SORT_CHUNK, TBD), lambda i, *_: (0, 0))],
        out_specs=pl.BlockSpec(memory_space=pl.ANY),
        scratch_shapes=[pltpu.VMEM((2, LROWS, D_MODEL // 2), U32),
                        pltpu.VMEM((TM, D_MODEL // 2), U32),
                        pltpu.SemaphoreType.DMA((2,)),
                        pltpu.SemaphoreType.DMA])
    return pl.pallas_call(
        _dispatch_kernel,
        out_shape=jax.ShapeDtypeStruct((N_TILES * TM, D_MODEL // 2), U32),
        grid_spec=grid_spec,
        compiler_params=_params(),
        name="dispatch",
    )(loff_s, seg_s, gbase_s, tail_s, h2, top_e, loff_v, tri, row_iota)


def _moe_kernel(te_ref, first_ref, par_ref, next_ref, nu_ref, xs_ref, bgu_ref, bd_ref, wgu_hbm, wd_hbm, o_ref,
                wgu_st, wd_st, wgu_bf, wd_bf, sem):
    i = pl.program_id(0)

    def fetch(e, s):
        return (pltpu.make_async_copy(wgu_hbm.at[e], wgu_st.at[s], sem.at[0, s]),
                pltpu.make_async_copy(wd_hbm.at[e], wd_st.at[s], sem.at[1, s]))

    @pl.when(i < nu_ref[0])
    def _():
        @pl.when(first_ref[i] == 1)
        def _():
            s = par_ref[i]

            @pl.when(i == 0)
            def _():
                for cp in fetch(te_ref[0], 0):
                    cp.start()

            for cp in fetch(te_ref[i], s):
                cp.wait()

            @pl.when(next_ref[i] >= 0)
            def _():
                for cp in fetch(next_ref[i], 1 - s):
                    cp.start()

            wgu_bf[...] = wgu_st[s].astype(BF16)
            wd_bf[...] = wd_st[s].astype(BF16)

        gu = _dot(_unpack_pairs(xs_ref[...]), wgu_bf[...]) + bgu_ref[0]
        gate = jnp.minimum(gu[:, :D_FF], SWIGLU_LIMIT)
        up = jnp.clip(gu[:, D_FF:], -SWIGLU_LIMIT, SWIGLU_LIMIT)
        act = (up + 1.0) * gate * jax.nn.sigmoid(SWIGLU_ALPHA * gate)
        out = _dot(act.astype(BF16), wd_bf[...]) + bd_ref[0]
        o_ref[...] = _pack_pairs(out.astype(BF16).astype(F32))

    @pl.when(i >= nu_ref[0])
    def _():
        o_ref[...] = jnp.zeros_like(o_ref)


def _moe(tile_tables, xs, w_gate_up, b_gate_up, w_down, b_down):
    nsp = len(tile_tables)
    row_tile = lambda i, *s: (jnp.minimum(i, s[nsp - 1][0] - 1), 0)
    grid_spec = pltpu.PrefetchScalarGridSpec(
        num_scalar_prefetch=nsp,
        grid=(N_TILES,),
        in_specs=[pl.BlockSpec((TM, D_MODEL // 2), row_tile),
                  pl.BlockSpec((1, 1, 2 * D_FF), lambda i, te, *_: (te[i], 0, 0)),
                  pl.BlockSpec((1, 1, D_MODEL), lambda i, te, *_: (te[i], 0, 0)),
                  pl.BlockSpec(memory_space=pl.ANY),
                  pl.BlockSpec(memory_space=pl.ANY)],
        out_specs=pl.BlockSpec((TM, D_MODEL // 2), lambda i, *_: (i, 0)),
        scratch_shapes=[pltpu.VMEM((2, D_MODEL, 2 * D_FF), F32),
                        pltpu.VMEM((2, D_FF, D_MODEL), F32),
                        pltpu.VMEM((D_MODEL, 2 * D_FF), BF16),
                        pltpu.VMEM((D_FF, D_MODEL), BF16),
                        pltpu.SemaphoreType.DMA((2, 2))])
    return pl.pallas_call(
        _moe_kernel,
        out_shape=jax.ShapeDtypeStruct((N_TILES * TM, D_MODEL // 2), U32),
        grid_spec=grid_spec,
        compiler_params=_params(),
        name="moe",
    )(*tile_tables, xs, b_gate_up.reshape(N_EXP, 1, 2 * D_FF), b_down.reshape(N_EXP, 1, D_MODEL),
      w_gate_up, w_down)


def _final_kernel(loff_s, seg_s, gbase_s, ys_hbm, x1_ref, te_ref, tw_ref, loffv_ref, tri_ref, iota_ref, mod_ref, fn_ref,
                  oc_ref, ol_ref, loc, sem):
    b = pl.program_id(0)
    slot = b % 2

    def start_block(blk, s):
        def in_copy(a, g, size):
            return pltpu.make_async_copy(ys_hbm.at[pl.ds(g, size)], loc.at[s, pl.ds(a, size)], sem.at[s])

        def body(e, carry):
            idx = blk * N_EXP + e
            _segment_copies(in_copy, loff_s[idx], gbase_s[idx], seg_s[idx] // 8)
            return carry

        lax.fori_loop(0, N_EXP, body, 0)

    @pl.when(b == 0)
    def _():
        loc[...] = jnp.zeros_like(loc)
        start_block(0, 0)

    @pl.when(b + 1 < NBD)
    def _():
        start_block(b + 1, 1 - slot)

    lrow = _local_rows(te_ref, loffv_ref, tri_ref)
    tw = jnp.concatenate([tw_ref[j] for j in range(TBD // TB)], axis=1)
    rows8 = jnp.concatenate(lrow + [tw], axis=0)
    cols = jnp.concatenate([rows8, jnp.zeros((128 - 2 * TOP_K, TBD), F32)], axis=0).T

    n = _block_rows(loff_s, seg_s, b)
    pltpu.make_async_copy(ys_hbm.at[pl.ds(0, n)], loc.at[slot, pl.ds(0, n)], sem.at[slot]).wait()

    y = None
    wts = [cols[:, TOP_K + k:TOP_K + k + 1].astype(BF16) for k in range(TOP_K)]
    for r0 in range(0, LROWS, SORT_CHUNK):
        rel = _chunk_relative([cols[:, k:k + 1] for k in range(TOP_K)], r0)
        pw = jnp.zeros((TBD, SORT_CHUNK), BF16)
        for k in reversed(range(TOP_K)):
            pw = jnp.where(iota_ref[...] == rel[k], wts[k], pw)
        part = _dot(pw, _unpack_pairs(loc[slot, r0:r0 + SORT_CHUNK, :]))
        y = part if y is None else y + part

    row = jnp.where(b < NBD_CTX, N_LAT_SEQ, (b - NBD_CTX) // (LAT_LEN // TBD))
    gate2 = mod_ref[pl.ds(row, 1), pl.ds(5 * D_MODEL, D_MODEL)]
    out = _rms(x1_ref[...] + gate2 * y, fn_ref[...])

    @pl.when(b < NBD_CTX)
    def _():
        oc_ref[...] = out

    @pl.when(b >= NBD_CTX)
    def _():
        ol_ref[...] = out


def _final(tables, ys, x1, top_e, top_w, tri, col_iota, mod, final_norm):
    loff_s, seg_s, gbase_s, _, loff_v = tables
    grid_spec = pltpu.PrefetchScalarGridSpec(
        num_scalar_prefetch=3,
        grid=(NBD,),
        in_specs=[pl.BlockSpec(memory_space=pl.ANY),
                  pl.BlockSpec((TBD, D_MODEL), lambda i, *_: (i, 0)),
                  pl.BlockSpec((TBD // TB, TOP_K, TB), lambda i, *_: (i, 0, 0)),
                  pl.BlockSpec((TBD // TB, TOP_K, TB), lambda i, *_: (i, 0, 0)),
                  pl.BlockSpec((1, N_EXP, 1), lambda i, *_: (i, 0, 0)),
                  pl.BlockSpec((TBD, TBD), lambda i, *_: (0, 0)),
                  pl.BlockSpec((TBD, SORT_CHUNK), lambda i, *_: (0, 0)),
                  pl.BlockSpec((MOD_ROWS, 6 * D_MODEL), lambda i, *_: (0, 0)),
                  pl.BlockSpec((1, D_MODEL), lambda i, *_: (0, 0))],
        out_specs=(pl.BlockSpec((TBD, D_MODEL), lambda i, *_: (jnp.minimum(i, NBD_CTX - 1), 0)),
                   pl.BlockSpec((TBD, D_MODEL), lambda i, *_: (jnp.maximum(i - NBD_CTX, 0), 0))),
        scratch_shapes=[pltpu.VMEM((2, LROWS, D_MODEL // 2), U32),
                        pltpu.SemaphoreType.DMA((2,))])
    return pl.pallas_call(
        _final_kernel,
        out_shape=(jax.ShapeDtypeStruct((T_CTX, D_MODEL), F32),
                   jax.ShapeDtypeStruct((T_LAT, D_MODEL), F32)),
        grid_spec=grid_spec,
        compiler_params=_params(),
        name="final",
    )(loff_s, seg_s, gbase_s, ys, x1, top_e, top_w, loff_v, tri, col_iota, mod, final_norm)


def _dispatch_tables(hist):
    hist = jnp.sum(hist.reshape(NBD, TBD // TB, N_EXP), axis=1)
    seg = ((hist + 7) // 8) * 8
    loff = jnp.cumsum(seg, axis=1) - seg
    rows_e = jnp.sum(seg, axis=0)
    region = ((rows_e + TM - 1) // TM) * TM
    region_end = jnp.cumsum(region)
    region_start = region_end - region
    gbase = region_start[None, :] + jnp.cumsum(seg, axis=0) - seg
    n_used = (region_end[-1] // TM).astype(I32)
    tail = jnp.concatenate([region_start + rows_e, region - rows_e, n_used.reshape(1)])
    start = jnp.arange(N_TILES, dtype=I32) * TM
    tile_e = jnp.sum((start[:, None] >= region_end[None, :]).astype(I32), axis=1)
    tile_e = jnp.minimum(tile_e, tile_e[jnp.maximum(n_used - 1, 0)])
    first = jnp.concatenate([jnp.ones((1,), I32), (tile_e[1:] != tile_e[:-1]).astype(I32)])
    parity = (jnp.cumsum(first) - 1) % 2
    later = jnp.where(tile_e[None, :] > tile_e[:, None], tile_e[None, :], N_EXP)
    nxt = jnp.min(later, axis=1)
    nxt = jnp.where(nxt == N_EXP, -1, nxt)
    flat = lambda a: a.reshape(-1).astype(I32)
    tables = (flat(loff), flat(seg), flat(gbase), flat(tail), loff.astype(F32).reshape(NBD, N_EXP, 1))
    tile_tables = (flat(tile_e), flat(first), flat(parity), flat(nxt), n_used.reshape(1))
    return tables, tile_tables


def kernel(x_prompt, x_sample, state_hgrn, c, c_ctx, w_ada, b_ada, norm1, w_in, hgrn_lb, hgrn_norm, w_pool,
           pool_scale, w_branch_a, w_branch_b, w_out, norm2, w_router, b_router, w_gate_up, b_gate_up,
           w_down, b_down, final_norm):
    x_ctx = x_prompt.reshape(T_CTX, D_MODEL)
    x_lat = x_sample.reshape(T_LAT, D_MODEL)
    cc = jnp.zeros((MOD_ROWS, D_MODEL), F32).at[:N_LAT_SEQ].set(c).at[N_LAT_SEQ].set(c_ctx)
    mod = _ada(cc, w_ada[0], b_ada)

    w_in_bf = w_in[0].astype(BF16)
    w_gates = w_in_bf[:, HGRN_W:HGRN_W + GATE_W]
    w_rest = jnp.concatenate([w_in_bf[:, :HGRN_W], w_in_bf[:, HGRN_W + GATE_W:]], axis=1)
    gates, rest = _inproj(x_ctx, x_lat, mod, norm1, w_gates, w_rest)

    mall, masks = _hgrn_consts()
    o_f, o_b, new_state = _hgrn(gates, rest, hgrn_lb, state_hgrn[:, 0], mall, masks)

    a_pool, cnt_pool = _pool_consts()
    yb = _pool(rest, a_pool, cnt_pool, w_pool[0].astype(BF16), pool_scale)

    wr_t = w_router[0].T
    wr_hi = wr_t.astype(BF16)
    wr_lo = (wr_t - wr_hi.astype(F32)).astype(BF16)
    tri = jnp.asarray(np.triu(np.ones((TBD, TBD), np.float32), 1), BF16)
    x1, h2, top_e, top_w, hist = _merge(
        x_ctx, x_lat, o_f, o_b, rest, yb, mod, hgrn_norm, w_branch_a[0].astype(BF16), w_branch_b[0].astype(BF16),
        w_out[0].astype(BF16), norm2, wr_hi, wr_lo, b_router.reshape(N_EXP, 1))

    tables, tile_tables = _dispatch_tables(hist)
    row_iota = jnp.asarray(np.broadcast_to(np.arange(SORT_CHUNK, dtype=np.float32)[:, None], (SORT_CHUNK, TBD)), BF16)
    xs = _dispatch(tables, h2, top_e, tri, row_iota)
    ys = _moe(tile_tables, xs, w_gate_up[0], b_gate_up[0], w_down[0], b_down[0])
    y_ctx, y_lat = _final(tables, ys, x1, top_e, top_w, tri, row_iota.T, mod, final_norm.reshape(1, D_MODEL))
    y_prompt = y_ctx.reshape(N_CTX_SEQ, CTX_LEN, D_MODEL)
    y_sample = y_lat.reshape(N_LAT_SEQ, LAT_LEN, D_MODEL)
    return y_prompt, y_sample, new_state[:, None]
```

```python
import functools

import numpy as np
import jax
import jax.numpy as jnp
from jax import lax
from jax.experimental import pallas as pl
from jax.experimental.pallas import tpu as pltpu

F32 = jnp.float32
BF16 = jnp.bfloat16
I32 = jnp.int32
U32 = jnp.uint32

D_MODEL = 1024
N_CTX_SEQ, CTX_LEN = 32, 256
N_LAT_SEQ, LAT_LEN = 4, 2048
T_CTX = N_CTX_SEQ * CTX_LEN
T_LAT = N_LAT_SEQ * LAT_LEN
T_ALL = T_CTX + T_LAT
TB = 256
NB = T_ALL // TB
NB_CTX = T_CTX // TB
LAT_BLOCKS = LAT_LEN // TB
HEADS, HEAD_K, HEAD_V = 4, 128, 128
HGRN_W = HEADS * HEAD_V
POOL_WINDOWS = (2, 4, 8, 16)
POOL_G = 128
POOL_W = len(POOL_WINDOWS) * POOL_G
GRID_W = 64
GRID_H = LAT_LEN // GRID_W
IN_W = 5 * HGRN_W + POOL_W + 2 * D_MODEL
GATE_W = 2 * HGRN_W
REST_W = IN_W - GATE_W
N_EXP, TOP_K, D_FF = 32, 4, 1024
SWIGLU_LIMIT = 7.0
SWIGLU_ALPHA = 1.702
EPS = 1e-6
LOG2_E = 1.4426950408889634
CHUNK = 64
N_LEVELS = 6
EXP_ROWS = (N_LEVELS + 2) * CHUNK
TM = 512
TBD = 512
NBD = T_ALL // TBD
NBD_CTX = T_CTX // TBD
SORT_CHUNK = 256
LROWS = TBD * TOP_K + N_EXP * 8
SEG_BITS = 7
N_TILES = -(-(T_ALL * TOP_K + NBD * N_EXP * 7 + N_EXP * (TM - 1)) // TM)
MOD_ROWS = 8
VMEM_LIMIT = 56 * 1024 * 1024


def _params(sem=("arbitrary",)):
    return pltpu.CompilerParams(dimension_semantics=sem, vmem_limit_bytes=VMEM_LIMIT)


def _dot(a, b):
    return jnp.dot(a, b, preferred_element_type=F32)


def _dot_nt(a, b):
    return lax.dot_general(a, b, (((1,), (1,)), ((), ())), preferred_element_type=F32)


def _dot_tn(a, b):
    return lax.dot_general(a, b, (((0,), (0,)), ((), ())), preferred_element_type=F32)


def _split2(x):
    hi = x.astype(BF16)
    lo = (x - hi.astype(F32)).astype(BF16)
    return hi, lo


def _mod_row(i):
    return jnp.where(i < NB_CTX, N_LAT_SEQ, (i - NB_CTX) // LAT_BLOCKS)


def _ada_kernel(c_ref, w_ref, b_ref, o_ref):
    c = c_ref[...]
    s = c * jax.nn.sigmoid(c)
    o_ref[...] = jnp.dot(s, w_ref[...], preferred_element_type=F32,
                         precision=lax.Precision.HIGHEST) + b_ref[...]


def _ada(cc, w_ada, b_ada):
    nblk = 1536
    return pl.pallas_call(
        _ada_kernel,
        out_shape=jax.ShapeDtypeStruct((MOD_ROWS, 6 * D_MODEL), F32),
        grid=(6 * D_MODEL // nblk,),
        in_specs=[pl.BlockSpec((MOD_ROWS, D_MODEL), lambda j: (0, 0)),
                  pl.BlockSpec((D_MODEL, nblk), lambda j: (0, j)),
                  pl.BlockSpec((1, nblk), lambda j: (0, j))],
        out_specs=pl.BlockSpec((MOD_ROWS, nblk), lambda j: (0, j)),
        compiler_params=_params(),
        name="ada",
    )(cc, w_ada, b_ada)


def _rms(x, g):
    ms = jnp.mean(x * x, axis=-1, keepdims=True)
    return x * lax.rsqrt(ms + EPS) * g


def _x_specs():
    return [pl.BlockSpec((TB, D_MODEL), lambda i, *_: (jnp.minimum(i, NB_CTX - 1), 0)),
            pl.BlockSpec((TB, D_MODEL), lambda i, *_: (jnp.maximum(i - NB_CTX, 0), 0))]


def _x_block(xc_ref, xl_ref):
    return jnp.where(pl.program_id(0) < NB_CTX, xc_ref[...], xl_ref[...])


def _inproj_kernel(xc_ref, xl_ref, mod_ref, n1_ref, wg_ref, wr_ref, og_ref, or_ref):
    row = _mod_row(pl.program_id(0))
    shift = mod_ref[pl.ds(row, 1), pl.ds(0, D_MODEL)]
    scale = mod_ref[pl.ds(row, 1), pl.ds(D_MODEL, D_MODEL)]
    h = (_rms(_x_block(xc_ref, xl_ref), n1_ref[...]) * (1.0 + scale) + shift).astype(BF16)
    og_ref[...] = _dot(h, wg_ref[...])
    or_ref[...] = _dot(h, wr_ref[...]).astype(BF16)


def _inproj(x_ctx, x_lat, mod, norm1, w_gates_bf, w_rest_bf):
    return pl.pallas_call(
        _inproj_kernel,
        out_shape=(jax.ShapeDtypeStruct((T_ALL, GATE_W), F32),
                   jax.ShapeDtypeStruct((T_ALL, REST_W), BF16)),
        grid=(NB,),
        in_specs=_x_specs() + [
                  pl.BlockSpec((MOD_ROWS, 6 * D_MODEL), lambda i: (0, 0)),
                  pl.BlockSpec((1, D_MODEL), lambda i: (0, 0)),
                  pl.BlockSpec((D_MODEL, GATE_W), lambda i: (0, 0)),
                  pl.BlockSpec((D_MODEL, REST_W), lambda i: (0, 0))],
        out_specs=(pl.BlockSpec((TB, GATE_W), lambda i: (i, 0)),
                   pl.BlockSpec((TB, REST_W), lambda i: (i, 0))),
        compiler_params=_params(),
        name="inproj",
    )(x_ctx, x_lat, mod, norm1, w_gates_bf, w_rest_bf)


def _hgrn_consts():
    c = CHUNK
    t = np.arange(c)[:, None]
    u = np.arange(c)[None, :]
    blocks = [u <= t]
    masks = [np.eye(c, dtype=bool)]
    h = c // 2
    while h >= 1:
        bi = t // h
        upper = (bi % 2) == 1
        e_up = (u >= bi * h) & (u <= t)
        e_lo = (u > t) & (u <= bi * h + h - 1)
        blocks.append(np.where(upper, e_up, e_lo))
        masks.append(((t // (2 * h)) == (u // (2 * h))) & (((t // h) % 2) == 1) & (((u // h) % 2) == 0))
        h //= 2
    blocks.append(u > t)
    m_f = np.stack(blocks).astype(np.float32)
    k_f = np.stack(masks).astype(np.float32)
    m_b = m_f[:, ::-1, ::-1]
    k_b = k_f[:, ::-1, ::-1]
    m = np.stack([m_f.reshape(EXP_ROWS, c), m_b.reshape(EXP_ROWS, c)])
    m3 = np.concatenate([m, m, m], axis=2)
    return jnp.asarray(m3, BF16), jnp.asarray(np.stack([k_f, k_b]), F32)


def _hgrn_block(dirs, lb, mall_ref, mask_ref, st_ref, z_ref, k_ref, sc_ref):
    c = CHUNK
    nchunk = TB // c
    units = [(d, h) for d in range(2) for h in range(HEADS)]
    sl = [slice(h * HEAD_K, (h + 1) * HEAD_K) for h in range(HEADS)]

    def rows(ci, d):
        r0 = ci * c if d == 0 else (nchunk - 1 - ci) * c
        return slice(r0, r0 + c)

    def exponents(ci):
        s = ci % 2
        for d in range(2):
            f = lb[d:d + 1] + (1.0 - lb[d:d + 1]) * jax.nn.sigmoid(dirs[d][1][rows(ci, d), :])
            k_ref[s, d] = 1.0 - f
            g = jnp.log(f) * LOG2_E
            g1 = g.astype(BF16)
            r1 = g - g1.astype(F32)
            g2 = r1.astype(BF16)
            g3 = (r1 - g2.astype(F32)).astype(BF16)
            gsplit = jnp.concatenate([g1, g2, g3], axis=0)
            z_ref[s, d] = jnp.exp2(_dot(mall_ref[d], gsplit))

    def q_of(ci, d, h):
        return dirs[d][0][rows(ci, d), sl[h]]

    def v_of(ci, d, h):
        return dirs[d][2][rows(ci, d), sl[h]].astype(BF16)

    def qz(ci, d, h, blk):
        return (q_of(ci, d, h) * z_ref[ci % 2, d, blk * c:(blk + 1) * c, sl[h]]).astype(BF16)

    def kz(ci, d, h, blk):
        return (k_ref[ci % 2, d, :, sl[h]] * z_ref[ci % 2, d, blk * c:(blk + 1) * c, sl[h]]).astype(BF16)

    def levels(ci):
        for d, h in units:
            sc_ref[d, h] = mask_ref[d, 0] * _dot_nt(q_of(ci, d, h).astype(BF16),
                                                    k_ref[ci % 2, d, :, sl[h]].astype(BF16))
        for lev in range(N_LEVELS):
            for d, h in units:
                sc_ref[d, h] += mask_ref[d, lev + 1] * _dot_nt(qz(ci, d, h, lev + 1), kz(ci, d, h, lev + 1))

    def tail(ci):
        for d, h in units:
            o = (_dot_nt(qz(ci, d, h, 0), st_ref[d, h].astype(BF16))
                 + _dot(sc_ref[d, h].astype(BF16), v_of(ci, d, h)))
            dirs[d][3][rows(ci, d), sl[h]] = o * (HEAD_K ** -0.5)
        for d, h in units:
            tot_row = c - 1 if d == 0 else 0
            decay = z_ref[ci % 2, d, tot_row:tot_row + 1, sl[h]]
            st_ref[d, h] = st_ref[d, h] * decay + _dot_tn(v_of(ci, d, h), kz(ci, d, h, N_LEVELS + 1))

    exponents(0)
    for ci in range(nchunk):
        levels(ci)
        if ci + 1 < nchunk:
            exponents(ci + 1)
        tail(ci)


def _hgrn_kernel(qf_ref, ff_ref, vf_ref, qb_ref, fb_ref, vb_ref, lbraw_ref, s0_ref, mall_ref, mask_ref,
                 of_ref, ob_ref, sout_hbm, st_ref, stage_ref, z_ref, k_ref, sc_ref, sem):
    i = pl.program_id(0)
    j = (i - NB_CTX) % LAT_BLOCKS
    is_ctx = i < NB_CTX

    @pl.when(is_ctx)
    def _():
        st_ref[...] = jnp.zeros_like(st_ref)

    @pl.when(jnp.logical_and(jnp.logical_not(is_ctx), j == 0))
    def _():
        for d in range(2):
            for h in range(HEADS):
                st_ref[d, h] = s0_ref[0, d, h].T

    a0 = lbraw_ref[0]
    a1 = lbraw_ref[1]
    mx = jnp.maximum(a0, a1)
    e0 = jnp.exp(a0 - mx)
    e1 = jnp.exp(a1 - mx)
    lb = e0 / (e0 + e1)

    dirs = ((qf_ref, ff_ref, vf_ref, of_ref), (qb_ref, fb_ref, vb_ref, ob_ref))
    _hgrn_block(dirs, lb, mall_ref, mask_ref, st_ref, z_ref, k_ref, sc_ref)

    @pl.when(is_ctx)
    def _():
        for d in range(2):
            for h in range(HEADS):
                stage_ref[d, h] = st_ref[d, h].T
        cp = pltpu.make_async_copy(stage_ref, sout_hbm.at[i], sem)
        cp.start()
        cp.wait()


def _bwd_block(i):
    j = (i - NB_CTX) % LAT_BLOCKS
    return jnp.where(i < NB_CTX, i, i - j + (LAT_BLOCKS - 1 - j))


def _hgrn(gates, rest, hgrn_lb, s0, mall, masks):
    nh = HGRN_W
    fwd = lambda col: pl.BlockSpec((TB, nh), lambda i: (i, col))
    bwd = lambda col: pl.BlockSpec((TB, nh), lambda i: (_bwd_block(i), col))
    lat_seq = lambda i: jnp.clip((i - NB_CTX) // LAT_BLOCKS, 0, N_LAT_SEQ - 1)
    return pl.pallas_call(
        _hgrn_kernel,
        out_shape=(jax.ShapeDtypeStruct((T_ALL, nh), F32),
                   jax.ShapeDtypeStruct((T_ALL, nh), F32),
                   jax.ShapeDtypeStruct((N_CTX_SEQ, 2, HEADS, HEAD_K, HEAD_V), F32)),
        grid=(NB,),
        in_specs=[fwd(0), fwd(0), fwd(1), bwd(0), bwd(1), bwd(1),
                  pl.BlockSpec((2, 2, nh), lambda i: (0, 0, 0)),
                  pl.BlockSpec((1, 2, HEADS, HEAD_K, HEAD_V), lambda i: (lat_seq(i), 0, 0, 0, 0)),
                  pl.BlockSpec((2, EXP_ROWS, 3 * CHUNK), lambda i: (0, 0, 0)),
                  pl.BlockSpec((2, N_LEVELS + 1, CHUNK, CHUNK), lambda i: (0, 0, 0, 0))],
        out_specs=(pl.BlockSpec((TB, nh), lambda i: (i, 0)),
                   pl.BlockSpec((TB, nh), lambda i: (_bwd_block(i), 0)),
                   pl.BlockSpec(memory_space=pl.ANY)),
        scratch_shapes=[pltpu.VMEM((2, HEADS, HEAD_V, HEAD_K), F32),
                        pltpu.VMEM((2, HEADS, HEAD_K, HEAD_V), F32),
                        pltpu.VMEM((2, 2, EXP_ROWS, HGRN_W), F32),
                        pltpu.VMEM((2, 2, CHUNK, HGRN_W), F32),
                        pltpu.VMEM((2, HEADS, CHUNK, CHUNK), F32),
                        pltpu.SemaphoreType.DMA],
        compiler_params=_params(),
        name="hgrn",
    )(rest, gates, rest, rest, gates, rest, hgrn_lb, s0, mall, masks)


def _window_bounds(n, w):
    pos = np.arange(n)
    lo = np.clip(pos - w // 2, 0, n - 1)
    hi = np.clip(pos - w // 2 + w - 1, 0, n - 1)
    return lo, hi


def _pool_consts():
    seq, img, cnt_seq, cnt_col = [], [], [], []
    for w in POOL_WINDOWS:
        lo, hi = _window_bounds(CTX_LEN, w)
        u = np.arange(CTX_LEN)[None, :]
        seq.append((u >= lo[:, None]) & (u <= hi[:, None]))
        cnt_seq.append(hi - lo + 1)
        lo, hi = _window_bounds(GRID_W, w)
        u = np.arange(GRID_W)[None, :]
        band = (u >= lo[:, None]) & (u <= hi[:, None])
        img.append(np.kron(np.eye(TB // GRID_W, dtype=bool), band))
        cnt_col.append(np.tile(hi - lo + 1, TB // GRID_W))
    a = np.stack([np.stack(seq), np.stack(img)]).astype(np.float32)
    cnt = np.stack([np.stack(cnt_seq), np.stack(cnt_col)]).astype(np.float32)
    cnt = np.broadcast_to(cnt[..., None], cnt.shape + (POOL_G,))
    return jnp.asarray(a, BF16), jnp.asarray(cnt, F32)


POOL_ROWS = LAT_LEN


def _pool_kernel(u_ref, a_ref, cnt_ref, wp_ref, ps_ref, o_ref, cp_ref):
    i = pl.program_id(0)
    nblk = POOL_ROWS // TB

    def finish(g, r0, nrows, pm):
        sl = slice(g * POOL_G, (g + 1) * POOL_G)
        d = pm - u_ref[pl.ds(r0, nrows), sl].astype(F32)
        y = _dot(d.astype(BF16), wp_ref[g]) * ps_ref[:, sl]
        o_ref[pl.ds(r0, nrows), sl] = y.astype(o_ref.dtype)

    def window_sum(kind, g, b):
        sl = slice(g * POOL_G, (g + 1) * POOL_G)
        return _dot(a_ref[kind, g], u_ref[pl.ds(b * TB, TB), sl]) / cnt_ref[kind, g]

    @pl.when(i < T_CTX // POOL_ROWS)
    def _():
        for g in range(len(POOL_WINDOWS)):
            for b in range(nblk):
                finish(g, b * TB, TB, window_sum(0, g, b))

    @pl.when(i >= T_CTX // POOL_ROWS)
    def _():
        for g, w in enumerate(POOL_WINDOWS):
            for b in range(nblk):
                cp_ref[pl.ds(b * TB, TB), :] = window_sum(1, g, b)
            lo, hi = _window_bounds(GRID_H, w)
            for r in range(GRID_H):
                acc = cp_ref[pl.ds(int(lo[r]) * GRID_W, GRID_W), :]
                for rr in range(int(lo[r]) + 1, int(hi[r]) + 1):
                    acc = acc + cp_ref[pl.ds(rr * GRID_W, GRID_W), :]
                finish(g, r * GRID_W, GRID_W, acc / float(hi[r] - lo[r] + 1))


def _pool(rest, a_pool, cnt_pool, w_pool_bf, pool_scale):
    col = 3
    return pl.pallas_call(
        _pool_kernel,
        out_shape=jax.ShapeDtypeStruct((T_ALL, POOL_W), BF16),
        grid=(T_ALL // POOL_ROWS,),
        in_specs=[pl.BlockSpec((POOL_ROWS, POOL_W), lambda i: (i, col)),
                  pl.BlockSpec((2, 4, TB, TB), lambda i: (0, 0, 0, 0)),
                  pl.BlockSpec((2, 4, TB, POOL_G), lambda i: (0, 0, 0, 0)),
                  pl.BlockSpec((4, POOL_G, POOL_G), lambda i: (0, 0, 0)),
                  pl.BlockSpec((1, POOL_W), lambda i: (0, 0))],
        out_specs=pl.BlockSpec((POOL_ROWS, POOL_W), lambda i: (i, 0)),
        scratch_shapes=[pltpu.VMEM((POOL_ROWS, POOL_G), F32)],
        compiler_params=_params(),
        name="pool",
    )(rest, a_pool, cnt_pool, w_pool_bf, pool_scale)


def _merge_kernel(xc_ref, xl_ref, of_ref, ob_ref, og_ref, yb_ref, ga_ref, gb_ref, mod_ref, hn_ref, wa_ref, wb_ref,
                  wo_ref, n2_ref, wrh_ref, wrl_ref, br_ref,
                  x1_ref, h2_ref, te_ref, tw_ref, hist_ref):
    row = _mod_row(pl.program_id(0))
    gate1 = mod_ref[pl.ds(row, 1), pl.ds(2 * D_MODEL, D_MODEL)]
    shift2 = mod_ref[pl.ds(row, 1), pl.ds(3 * D_MODEL, D_MODEL)]
    scale2 = mod_ref[pl.ds(row, 1), pl.ds(4 * D_MODEL, D_MODEL)]

    halves = [slice(j * (TB // 2), (j + 1) * (TB // 2)) for j in range(2)]
    is_ctx = pl.program_id(0) < NB_CTX

    def head_out(r):
        o = of_ref[r, :] + ob_ref[r, :]
        og = og_ref[r, :].astype(F32)
        ya = jnp.concatenate(
            [_rms(o[:, h * HEAD_V:(h + 1) * HEAD_V], hn_ref[...]) for h in range(HEADS)], axis=1)
        return (ya * (og * jax.nn.sigmoid(og))).astype(BF16)

    ya = [head_out(r) for r in halves]
    pa = [_dot(ya[j], wa_ref[...]) for j in range(2)]
    pb = [_dot(yb_ref[r, :], wb_ref[...]) for r in halves]
    merged = [(jax.nn.sigmoid(ga_ref[r, :].astype(F32)) * pa[j]
               + jax.nn.sigmoid(gb_ref[r, :].astype(F32)) * pb[j]).astype(BF16) for j, r in enumerate(halves)]
    po = [_dot(merged[j], wo_ref[...]) for j in range(2)]
    hh, hl = [], []
    for j, r in enumerate(halves):
        x1 = jnp.where(is_ctx, xc_ref[r, :], xl_ref[r, :]) + gate1 * po[j]
        x1_ref[r, :] = x1
        hi, lo = _split2(_rms(x1, n2_ref[...]) * (1.0 + scale2) + shift2)
        h2_ref[r, :] = hi
        hh.append(hi)
        hl.append(lo)
    hh = jnp.concatenate(hh, axis=0)
    hl = jnp.concatenate(hl, axis=0)

    lt = _dot_nt(wrh_ref[...], hh) + _dot_nt(wrl_ref[...], hh) + _dot_nt(wrh_ref[...], hl) + br_ref[...]
    eidx = lax.broadcasted_iota(I32, (N_EXP, TB), 0)
    vals, idxs, cnt = [], [], jnp.zeros((N_EXP, TB), F32)
    for _ in range(TOP_K):
        m = jnp.max(lt, axis=0, keepdims=True)
        idx = jnp.min(jnp.where(lt == m, eidx, N_EXP), axis=0, keepdims=True)
        sel = eidx == idx
        vals.append(m)
        idxs.append(idx)
        cnt = cnt + sel.astype(F32)
        lt = jnp.where(sel, -jnp.inf, lt)
    ex = [jnp.exp(v - vals[0]) for v in vals]
    den = ex[0] + ex[1] + ex[2] + ex[3]
    tw_ref[0] = jnp.concatenate([e / den for e in ex], axis=0)
    te_ref[0] = jnp.concatenate(idxs, axis=0)
    hist_ref[0] = jnp.sum(cnt, axis=1, keepdims=True).astype(I32)


def _merge(x_ctx, x_lat, o_f, o_b, rest, yb, mod, hgrn_norm, wa_bf, wb_bf, wo_bf, norm2, wr_hi, wr_lo, b_router):
    full = lambda shape: pl.BlockSpec(shape, lambda i: (0,) * len(shape))
    return pl.pallas_call(
        _merge_kernel,
        out_shape=(jax.ShapeDtypeStruct((T_ALL, D_MODEL), F32),
                   jax.ShapeDtypeStruct((T_ALL, D_MODEL), BF16),
                   jax.ShapeDtypeStruct((NB, TOP_K, TB), I32),
                   jax.ShapeDtypeStruct((NB, TOP_K, TB), F32),
                   jax.ShapeDtypeStruct((NB, N_EXP, 1), I32)),
        grid=(NB,),
        in_specs=_x_specs() + [
                  pl.BlockSpec((TB, HGRN_W), lambda i: (i, 0)),
                  pl.BlockSpec((TB, HGRN_W), lambda i: (i, 0)),
                  pl.BlockSpec((TB, HGRN_W), lambda i: (i, 2)),
                  pl.BlockSpec((TB, POOL_W), lambda i: (i, 0)),
                  pl.BlockSpec((TB, D_MODEL), lambda i: (i, 2)),
                  pl.BlockSpec((TB, D_MODEL), lambda i: (i, 3)),
                  full((MOD_ROWS, 6 * D_MODEL)),
                  full((1, HEAD_V)),
                  full((HGRN_W, D_MODEL)),
                  full((POOL_W, D_MODEL)),
                  full((D_MODEL, D_MODEL)),
                  full((1, D_MODEL)),
                  full((N_EXP, D_MODEL)),
                  full((N_EXP, D_MODEL)),
                  full((N_EXP, 1))],
        out_specs=(pl.BlockSpec((TB, D_MODEL), lambda i: (i, 0)),
                   pl.BlockSpec((TB, D_MODEL), lambda i: (i, 0)),
                   pl.BlockSpec((1, TOP_K, TB), lambda i: (i, 0, 0)),
                   pl.BlockSpec((1, TOP_K, TB), lambda i: (i, 0, 0)),
                   pl.BlockSpec((1, N_EXP, 1), lambda i: (i, 0, 0))),
        compiler_params=_params(),
        name="merge",
    )(x_ctx, x_lat, o_f, o_b, rest, yb, rest, rest, mod, hgrn_norm, wa_bf, wb_bf, wo_bf, norm2,
      wr_hi, wr_lo, b_router)


def _local_rows(te_ref, loff_ref, tri_ref):
    te = jnp.concatenate([te_ref[j] for j in range(TBD // TB)], axis=1)
    eidx = lax.broadcasted_iota(I32, (N_EXP, TBD), 0)
    sels = [eidx == te[k:k + 1] for k in range(TOP_K)]
    cnt = sels[0].astype(F32)
    for s in sels[1:]:
        cnt = cnt + s.astype(F32)
    base = _dot(cnt.astype(BF16), tri_ref[...]) + loff_ref[0]
    return [jnp.sum(jnp.where(s, base, 0.0), axis=0, keepdims=True) for s in sels]


def _chunk_relative(rows, r0):
    out = []
    for r in rows:
        inside = jnp.logical_and(r >= r0, r < r0 + SORT_CHUNK)
        out.append(jnp.where(inside, r - r0, -1.0).astype(BF16))
    return out


def _segment_copies(make_copy, local_off, global_off, units):
    for j in range(SEG_BITS):
        low = (units & ((1 << j) - 1)) * 8

        @pl.when(((units >> j) & 1) == 1)
        def _():
            make_copy(pl.multiple_of(local_off + low, 8), pl.multiple_of(global_off + low, 8), 8 << j).start()


def _pack_pairs(x):
    half = D_MODEL // 2
    lo = lax.bitcast_convert_type(x[:, :half], U32) >> 16
    hi = lax.bitcast_convert_type(x[:, half:], U32) & jnp.uint32(0xFFFF0000)
    return hi | lo


def _unpack_pairs(p):
    lo = lax.bitcast_convert_type(p << 16, F32).astype(BF16)
    hi = lax.bitcast_convert_type(p & jnp.uint32(0xFFFF0000), F32).astype(BF16)
    return jnp.concatenate([lo, hi], axis=1)


def _block_rows(loff_s, seg_s, b):
    last = b * N_EXP + N_EXP - 1
    return pl.multiple_of(loff_s[last] + seg_s[last], 8)


def _dispatch_kernel(loff_s, seg_s, gbase_s, tail_s, h2_ref, te_ref, loffv_ref, tri_ref, iota_ref, xs_hbm,
                     loc, zeros, sem, sem_z):
    b = pl.program_id(0)
    slot = b % 2

    def wait_block(blk, s):
        n = _block_rows(loff_s, seg_s, blk)
        pltpu.make_async_copy(loc.at[s, pl.ds(0, n)], xs_hbm.at[pl.ds(0, n)], sem.at[s]).wait()

    lrow = _local_rows(te_ref, loffv_ref, tri_ref)

    @pl.when(b >= 2)
    def _():
        wait_block(b - 2, slot)

    for r0 in range(0, LROWS, SORT_CHUNK):
        rel = _chunk_relative(lrow, r0)
        p = jnp.zeros((SORT_CHUNK, TBD), BF16)
        for k in reversed(range(TOP_K)):
            p = jnp.where(iota_ref[...] == rel[k], jnp.ones_like(p), p)
        loc[slot, r0:r0 + SORT_CHUNK, :] = _pack_pairs(_dot(p, h2_ref[...]))

    def out_copy(a, g, size):
        return pltpu.make_async_copy(loc.at[slot, pl.ds(a, size)], xs_hbm.at[pl.ds(g, size)], sem.at[slot])

    def body(e, carry):
        idx = b * N_EXP + e
        _segment_copies(out_copy, loff_s[idx], gbase_s[idx], seg_s[idx] // 8)
        return carry

    lax.fori_loop(0, N_EXP, body, 0)

    @pl.when(b == NBD - 1)
    def _():
        zeros[...] = jnp.zeros_like(zeros)

        def zero_copy(a, g, size):
            return pltpu.make_async_copy(zeros.at[pl.ds(a, size)], xs_hbm.at[pl.ds(g, size)], sem_z)

        def zbody(e, ztot):
            _segment_copies(zero_copy, 0, tail_s[e], tail_s[N_EXP + e] // 8)
            return ztot + tail_s[N_EXP + e]

        def tbody(t, carry):
            pltpu.make_async_copy(zeros, xs_hbm.at[pl.ds(pl.multiple_of(t * TM, TM), TM)], sem_z).start()
            return carry

        n_used = tail_s[2 * N_EXP]
        lax.fori_loop(n_used, N_TILES, tbody, 0)
        ztot = lax.fori_loop(0, N_EXP, zbody, 0) + (N_TILES - n_used) * TM
        ztot = pl.multiple_of(ztot, 8)

        @pl.when(ztot > 0)
        def _():
            pltpu.make_async_copy(xs_hbm.at[pl.ds(0, ztot)], xs_hbm.at[pl.ds(0, ztot)], sem_z).wait()

        wait_block(b - 1, 1 - slot)
        wait_block(b, slot)


def _dispatch(tables, h2, top_e, tri, row_iota):
    loff_s, seg_s, gbase_s, tail_s, loff_v = tables
    grid_spec = pltpu.PrefetchScalarGridSpec(
        num_scalar_prefetch=4,
        grid=(NBD,),
        in_specs=[pl.BlockSpec((TBD, D_MODEL), lambda i, *_: (i, 0)),
                  pl.BlockSpec((TBD // TB, TOP_K, TB), lambda i, *_: (i, 0, 0)),
                  pl.BlockSpec((1, N_EXP, 1), lambda i, *_: (i, 0, 0)),
                  pl.BlockSpec((TBD, TBD), lambda i, *_: (0, 0)),
                  pl.BlockSpec((SORT_CHUNK, TBD), lambda i, *_: (0, 0))],
        out_specs=pl.BlockSpec(memory_space=pl.ANY),
        scratch_shapes=[pltpu.VMEM((2, LROWS, D_MODEL // 2), U32),
                        pltpu.VMEM((TM, D_MODEL // 2), U32),
                        pltpu.SemaphoreType.DMA((2,)),
                        pltpu.SemaphoreType.DMA])
    return pl.pallas_call(
        _dispatch_kernel,
        out_shape=jax.ShapeDtypeStruct((N_TILES * TM, D_MODEL // 2), U32),
        grid_spec=grid_spec,
        compiler_params=_params(),
        name="dispatch",
    )(loff_s, seg_s, gbase_s, tail_s, h2, top_e, loff_v, tri, row_iota)


def _moe_kernel(te_ref, first_ref, par_ref, next_ref, nv_ref, nu_ref, xs_ref, bgu_ref, bd_ref, wgu_hbm, wd_hbm, o_ref,
                wgu_st, wd_st, wgu_bf, wd_bf, sem):
    i = pl.program_id(0)

    def fetch(e, s):
        return (pltpu.make_async_copy(wgu_hbm.at[e], wgu_st.at[s], sem.at[0, s]),
                pltpu.make_async_copy(wd_hbm.at[e], wd_st.at[s], sem.at[1, s]))

    @pl.when(i < nu_ref[0])
    def _():
        @pl.when(first_ref[i] == 1)
        def _():
            s = par_ref[i]

            @pl.when(i == 0)
            def _():
                for cp in fetch(te_ref[0], 0):
                    cp.start()

            for cp in fetch(te_ref[i], s):
                cp.wait()

            @pl.when(next_ref[i] >= 0)
            def _():
                for cp in fetch(next_ref[i], 1 - s):
                    cp.start()

            wgu_bf[...] = wgu_st[s].astype(BF16)
            wd_bf[...] = wd_st[s].astype(BF16)

        half = TM // 2
        halves = [slice(0, half), slice(half, TM)]

        def gate_up(r):
            return _dot(_unpack_pairs(xs_ref[r, :]), wgu_bf[...]) + bgu_ref[0]

        def activation(gu):
            gate = jnp.minimum(gu[:, :D_FF], SWIGLU_LIMIT)
            up = jnp.clip(gu[:, D_FF:], -SWIGLU_LIMIT, SWIGLU_LIMIT)
            return ((up + 1.0) * gate * jax.nn.sigmoid(SWIGLU_ALPHA * gate)).astype(BF16)

        def down(r, act):
            out = _dot(act, wd_bf[...]) + bd_ref[0]
            o_ref[r, :] = _pack_pairs(out.astype(BF16).astype(F32))

        @pl.when(nv_ref[i] > half)
        def _():
            gu = [gate_up(r) for r in halves]
            for r, g in zip(halves, gu):
                down(r, activation(g))

        @pl.when(nv_ref[i] <= half)
        def _():
            down(halves[0], activation(gate_up(halves[0])))
            o_ref[halves[1], :] = jnp.zeros((half, D_MODEL // 2), U32)

    @pl.when(i >= nu_ref[0])
    def _():
        o_ref[...] = jnp.zeros_like(o_ref)


def _moe(tile_tables, xs, w_gate_up, b_gate_up, w_down, b_down):
    nsp = len(tile_tables)
    row_tile = lambda i, *s: (jnp.minimum(i, s[nsp - 1][0] - 1), 0)
    grid_spec = pltpu.PrefetchScalarGridSpec(
        num_scalar_prefetch=nsp,
        grid=(N_TILES,),
        in_specs=[pl.BlockSpec((TM, D_MODEL // 2), row_tile),
                  pl.BlockSpec((1, 1, 2 * D_FF), lambda i, te, *_: (te[i], 0, 0)),
                  pl.BlockSpec((1, 1, D_MODEL), lambda i, te, *_: (te[i], 0, 0)),
                  pl.BlockSpec(memory_space=pl.ANY),
                  pl.BlockSpec(memory_space=pl.ANY)],
        out_specs=pl.BlockSpec((TM, D_MODEL // 2), lambda i, *_: (i, 0)),
        scratch_shapes=[pltpu.VMEM((2, D_MODEL, 2 * D_FF), F32),
                        pltpu.VMEM((2, D_FF, D_MODEL), F32),
                        pltpu.VMEM((D_MODEL, 2 * D_FF), BF16),
                        pltpu.VMEM((D_FF, D_MODEL), BF16),
                        pltpu.SemaphoreType.DMA((2, 2))])
    return pl.pallas_call(
        _moe_kernel,
        out_shape=jax.ShapeDtypeStruct((N_TILES * TM, D_MODEL // 2), U32),
        grid_spec=grid_spec,
        compiler_params=_params(),
        name="moe",
    )(*tile_tables, xs, b_gate_up.reshape(N_EXP, 1, 2 * D_FF), b_down.reshape(N_EXP, 1, D_MODEL),
      w_gate_up, w_down)


def _final_kernel(loff_s, seg_s, gbase_s, ys_hbm, x1_ref, te_ref, tw_ref, loffv_ref, tri_ref, iota_ref, mod_ref, fn_ref,
                  oc_ref, ol_ref, loc, sem):
    b = pl.program_id(0)
    slot = b % 2

    def start_block(blk, s):
        def in_copy(a, g, size):
            return pltpu.make_async_copy(ys_hbm.at[pl.ds(g, size)], loc.at[s, pl.ds(a, size)], sem.at[s])

        def body(e, carry):
            idx = blk * N_EXP + e
            _segment_copies(in_copy, loff_s[idx], gbase_s[idx], seg_s[idx] // 8)
            return carry

        lax.fori_loop(0, N_EXP, body, 0)

    @pl.when(b == 0)
    def _():
        loc[...] = jnp.zeros_like(loc)
        start_block(0, 0)

    @pl.when(b + 1 < NBD)
    def _():
        start_block(b + 1, 1 - slot)

    lrow = _local_rows(te_ref, loffv_ref, tri_ref)
    tw = jnp.concatenate([tw_ref[j] for j in range(TBD // TB)], axis=1)
    rows8 = jnp.concatenate(lrow + [tw], axis=0)
    cols = jnp.concatenate([rows8, jnp.zeros((128 - 2 * TOP_K, TBD), F32)], axis=0).T

    n = _block_rows(loff_s, seg_s, b)
    pltpu.make_async_copy(ys_hbm.at[pl.ds(0, n)], loc.at[slot, pl.ds(0, n)], sem.at[slot]).wait()

    y = None
    wts = [cols[:, TOP_K + k:TOP_K + k + 1].astype(BF16) for k in range(TOP_K)]
    for r0 in range(0, LROWS, SORT_CHUNK):
        rel = _chunk_relative([cols[:, k:k + 1] for k in range(TOP_K)], r0)
        pw = jnp.zeros((TBD, SORT_CHUNK), BF16)
        for k in reversed(range(TOP_K)):
            pw = jnp.where(iota_ref[...] == rel[k], wts[k], pw)
        part = _dot(pw, _unpack_pairs(loc[slot, r0:r0 + SORT_CHUNK, :]))
        y = part if y is None else y + part

    row = jnp.where(b < NBD_CTX, N_LAT_SEQ, (b - NBD_CTX) // (LAT_LEN // TBD))
    gate2 = mod_ref[pl.ds(row, 1), pl.ds(5 * D_MODEL, D_MODEL)]
    out = _rms(x1_ref[...] + gate2 * y, fn_ref[...])

    @pl.when(b < NBD_CTX)
    def _():
        oc_ref[...] = out

    @pl.when(b >= NBD_CTX)
    def _():
        ol_ref[...] = out


def _final(tables, ys, x1, top_e, top_w, tri, col_iota, mod, final_norm):
    loff_s, seg_s, gbase_s, _, loff_v = tables
    grid_spec = pltpu.PrefetchScalarGridSpec(
        num_scalar_prefetch=3,
        grid=(NBD,),
        in_specs=[pl.BlockSpec(memory_space=pl.ANY),
                  pl.BlockSpec((TBD, D_MODEL), lambda i, *_: (i, 0)),
                  pl.BlockSpec((TBD // TB, TOP_K, TB), lambda i, *_: (i, 0, 0)),
                  pl.BlockSpec((TBD // TB, TOP_K, TB), lambda i, *_: (i, 0, 0)),
                  pl.BlockSpec((1, N_EXP, 1), lambda i, *_: (i, 0, 0)),
                  pl.BlockSpec((TBD, TBD), lambda i, *_: (0, 0)),
                  pl.BlockSpec((TBD, SORT_CHUNK), lambda i, *_: (0, 0)),
                  pl.BlockSpec((MOD_ROWS, 6 * D_MODEL), lambda i, *_: (0, 0)),
                  pl.BlockSpec((1, D_MODEL), lambda i, *_: (0, 0))],
        out_specs=(pl.BlockSpec((TBD, D_MODEL), lambda i, *_: (jnp.minimum(i, NBD_CTX - 1), 0)),
                   pl.BlockSpec((TBD, D_MODEL), lambda i, *_: (jnp.maximum(i - NBD_CTX, 0), 0))),
        scratch_shapes=[pltpu.VMEM((2, LROWS, D_MODEL // 2), U32),
                        pltpu.SemaphoreType.DMA((2,))])
    return pl.pallas_call(
        _final_kernel,
        out_shape=(jax.ShapeDtypeStruct((T_CTX, D_MODEL), F32),
                   jax.ShapeDtypeStruct((T_LAT, D_MODEL), F32)),
        grid_spec=grid_spec,
        compiler_params=_params(),
        name="final",
    )(loff_s, seg_s, gbase_s, ys, x1, top_e, top_w, loff_v, tri, col_iota, mod, final_norm)


def _dispatch_tables(hist):
    hist = jnp.sum(hist.reshape(NBD, TBD // TB, N_EXP), axis=1)
    seg = ((hist + 7) // 8) * 8
    loff = jnp.cumsum(seg, axis=1) - seg
    rows_e = jnp.sum(seg, axis=0)
    region = ((rows_e + TM - 1) // TM) * TM
    region_end = jnp.cumsum(region)
    region_start = region_end - region
    gbase = region_start[None, :] + jnp.cumsum(seg, axis=0) - seg
    n_used = (region_end[-1] // TM).astype(I32)
    tail = jnp.concatenate([region_start + rows_e, region - rows_e, n_used.reshape(1)])
    start = jnp.arange(N_TILES, dtype=I32) * TM
    tile_e = jnp.sum((start[:, None] >= region_end[None, :]).astype(I32), axis=1)
    tile_e = jnp.minimum(tile_e, tile_e[jnp.maximum(n_used - 1, 0)])
    first = jnp.concatenate([jnp.ones((1,), I32), (tile_e[1:] != tile_e[:-1]).astype(I32)])
    parity = (jnp.cumsum(first) - 1) % 2
    later = jnp.where(tile_e[None, :] > tile_e[:, None], tile_e[None, :], N_EXP)
    nxt = jnp.min(later, axis=1)
    nxt = jnp.where(nxt == N_EXP, -1, nxt)
    flat = lambda a: a.reshape(-1).astype(I32)
    tables = (flat(loff), flat(seg), flat(gbase), flat(tail), loff.astype(F32).reshape(NBD, N_EXP, 1))
    tile_nv = jnp.clip(rows_e[tile_e] - (start - region_start[tile_e]), 0, TM)
    tile_tables = (flat(tile_e), flat(first), flat(parity), flat(nxt), flat(tile_nv), n_used.reshape(1))
    return tables, tile_tables


def kernel(x_prompt, x_sample, state_hgrn, c, c_ctx, w_ada, b_ada, norm1, w_in, hgrn_lb, hgrn_norm, w_pool,
           pool_scale, w_branch_a, w_branch_b, w_out, norm2, w_router, b_router, w_gate_up, b_gate_up,
           w_down, b_down, final_norm):
    x_ctx = x_prompt.reshape(T_CTX, D_MODEL)
    x_lat = x_sample.reshape(T_LAT, D_MODEL)
    cc = jnp.zeros((MOD_ROWS, D_MODEL), F32).at[:N_LAT_SEQ].set(c).at[N_LAT_SEQ].set(c_ctx)
    mod = _ada(cc, w_ada[0], b_ada)

    w_in_bf = w_in[0].astype(BF16)
    w_gates = w_in_bf[:, HGRN_W:HGRN_W + GATE_W]
    w_rest = jnp.concatenate([w_in_bf[:, :HGRN_W], w_in_bf[:, HGRN_W + GATE_W:]], axis=1)
    gates, rest = _inproj(x_ctx, x_lat, mod, norm1, w_gates, w_rest)

    mall, masks = _hgrn_consts()
    o_f, o_b, new_state = _hgrn(gates, rest, hgrn_lb, state_hgrn[:, 0], mall, masks)

    a_pool, cnt_pool = _pool_consts()
    yb = _pool(rest, a_pool, cnt_pool, w_pool[0].astype(BF16), pool_scale)

    wr_t = w_router[0].T
    wr_hi = wr_t.astype(BF16)
    wr_lo = (wr_t - wr_hi.astype(F32)).astype(BF16)
    tri = jnp.asarray(np.triu(np.ones((TBD, TBD), np.float32), 1), BF16)
    x1, h2, top_e, top_w, hist = _merge(
        x_ctx, x_lat, o_f, o_b, rest, yb, mod, hgrn_norm, w_branch_a[0].astype(BF16), w_branch_b[0].astype(BF16),
        w_out[0].astype(BF16), norm2, wr_hi, wr_lo, b_router.reshape(N_EXP, 1))

    tables, tile_tables = _dispatch_tables(hist)
    row_iota = jnp.asarray(np.broadcast_to(np.arange(SORT_CHUNK, dtype=np.float32)[:, None], (SORT_CHUNK, TBD)), BF16)
    xs = _dispatch(tables, h2, top_e, tri, row_iota)
    ys = _moe(tile_tables, xs, w_gate_up[0], b_gate_up[0], w_down[0], b_down[0])
    y_ctx, y_lat = _final(tables, ys, x1, top_e, top_w, tri, row_iota.T, mod, final_norm.reshape(1, D_MODEL))
    y_prompt = y_ctx.reshape(N_CTX_SEQ, CTX_LEN, D_MODEL)
    y_sample = y_lat.reshape(N_LAT_SEQ, LAT_LEN, D_MODEL)
    return y_prompt, y_sample, new_state[:, None]
```

```python
import functools

import numpy as np
import jax
import jax.numpy as jnp
from jax import lax
from jax.experimental import pallas as pl
from jax.experimental.pallas import tpu as pltpu

F32 = jnp.float32
BF16 = jnp.bfloat16
I32 = jnp.int32
U32 = jnp.uint32

D_MODEL = 1024
N_CTX_SEQ, CTX_LEN = 32, 256
N_LAT_SEQ, LAT_LEN = 4, 2048
T_CTX = N_CTX_SEQ * CTX_LEN
T_LAT = N_LAT_SEQ * LAT_LEN
T_ALL = T_CTX + T_LAT
TB = 256
NB = T_ALL // TB
NB_CTX = T_CTX // TB
LAT_BLOCKS = LAT_LEN // TB
HEADS, HEAD_K, HEAD_V = 4, 128, 128
HGRN_W = HEADS * HEAD_V
POOL_WINDOWS = (2, 4, 8, 16)
POOL_G = 128
POOL_W = len(POOL_WINDOWS) * POOL_G
GRID_W = 64
GRID_H = LAT_LEN // GRID_W
IN_W = 5 * HGRN_W + POOL_W + 2 * D_MODEL
GATE_W = 2 * HGRN_W
REST_W = IN_W - GATE_W
N_EXP, TOP_K, D_FF = 32, 4, 1024
SWIGLU_LIMIT = 7.0
SWIGLU_ALPHA = 1.702
EPS = 1e-6
LOG2_E = 1.4426950408889634
CHUNK = 64
N_LEVELS = 6
EXP_ROWS = (N_LEVELS + 2) * CHUNK
MM_BLOCKS = (0, 4, 5, 6)
MM_ROWS = len(MM_BLOCKS) * CHUNK
COARSE_LEVELS = ((1, 32), (2, 16), (3, 8))
TM = 512
TBD = 512
NBD = T_ALL // TBD
NBD_CTX = T_CTX // TBD
SORT_CHUNK = 256
LROWS = TBD * TOP_K + N_EXP * 8
SEG_SMALL_BITS = 3
N_TILES = -(-(T_ALL * TOP_K + NBD * N_EXP * 7 + N_EXP * (TM - 1)) // TM)
MOD_ROWS = 8
VMEM_LIMIT = 56 * 1024 * 1024


def _params(sem=("arbitrary",)):
    return pltpu.CompilerParams(dimension_semantics=sem, vmem_limit_bytes=VMEM_LIMIT)


def _dot(a, b):
    return jnp.dot(a, b, preferred_element_type=F32)


def _dot_nt(a, b):
    return lax.dot_general(a, b, (((1,), (1,)), ((), ())), preferred_element_type=F32)


def _dot_tn(a, b):
    return lax.dot_general(a, b, (((0,), (0,)), ((), ())), preferred_element_type=F32)


def _split2(x):
    hi = x.astype(BF16)
    lo = (x - hi.astype(F32)).astype(BF16)
    return hi, lo


def _mod_row(i):
    return jnp.where(i < NB_CTX, N_LAT_SEQ, (i - NB_CTX) // LAT_BLOCKS)


def _ada_kernel(c_ref, w_ref, b_ref, o_ref):
    c = c_ref[...]
    s = c * jax.nn.sigmoid(c)
    o_ref[...] = jnp.dot(s, w_ref[...], preferred_element_type=F32,
                         precision=lax.Precision.HIGHEST) + b_ref[...]


def _ada(cc, w_ada, b_ada):
    nblk = 1536
    return pl.pallas_call(
        _ada_kernel,
        out_shape=jax.ShapeDtypeStruct((MOD_ROWS, 6 * D_MODEL), F32),
        grid=(6 * D_MODEL // nblk,),
        in_specs=[pl.BlockSpec((MOD_ROWS, D_MODEL), lambda j: (0, 0)),
                  pl.BlockSpec((D_MODEL, nblk), lambda j: (0, j)),
                  pl.BlockSpec((1, nblk), lambda j: (0, j))],
        out_specs=pl.BlockSpec((MOD_ROWS, nblk), lambda j: (0, j)),
        compiler_params=_params(),
        name="ada",
    )(cc, w_ada, b_ada)


def _rms(x, g):
    ms = jnp.mean(x * x, axis=-1, keepdims=True)
    return x * lax.rsqrt(ms + EPS) * g


def _x_specs():
    return [pl.BlockSpec((TB, D_MODEL), lambda i, *_: (jnp.minimum(i, NB_CTX - 1), 0)),
            pl.BlockSpec((TB, D_MODEL), lambda i, *_: (jnp.maximum(i - NB_CTX, 0), 0))]


def _x_block(xc_ref, xl_ref):
    return jnp.where(pl.program_id(0) < NB_CTX, xc_ref[...], xl_ref[...])


def _inproj_kernel(xc_ref, xl_ref, mod_ref, n1_ref, wg_ref, wr_ref, og_ref, or_ref):
    row = _mod_row(pl.program_id(0))
    shift = mod_ref[pl.ds(row, 1), pl.ds(0, D_MODEL)]
    scale = mod_ref[pl.ds(row, 1), pl.ds(D_MODEL, D_MODEL)]
    h = (_rms(_x_block(xc_ref, xl_ref), n1_ref[...]) * (1.0 + scale) + shift).astype(BF16)
    og_ref[...] = _dot(h, wg_ref[...])
    or_ref[...] = _dot(h, wr_ref[...]).astype(BF16)


def _inproj(x_ctx, x_lat, mod, norm1, w_gates_bf, w_rest_bf):
    return pl.pallas_call(
        _inproj_kernel,
        out_shape=(jax.ShapeDtypeStruct((T_ALL, GATE_W), F32),
                   jax.ShapeDtypeStruct((T_ALL, REST_W), BF16)),
        grid=(NB,),
        in_specs=_x_specs() + [
                  pl.BlockSpec((MOD_ROWS, 6 * D_MODEL), lambda i: (0, 0)),
                  pl.BlockSpec((1, D_MODEL), lambda i: (0, 0)),
                  pl.BlockSpec((D_MODEL, GATE_W), lambda i: (0, 0)),
                  pl.BlockSpec((D_MODEL, REST_W), lambda i: (0, 0))],
        out_specs=(pl.BlockSpec((TB, GATE_W), lambda i: (i, 0)),
                   pl.BlockSpec((TB, REST_W), lambda i: (i, 0))),
        compiler_params=_params(),
        name="inproj",
    )(x_ctx, x_lat, mod, norm1, w_gates_bf, w_rest_bf)


def _hgrn_consts():
    c = CHUNK
    t = np.arange(c)[:, None]
    u = np.arange(c)[None, :]
    blocks = [u <= t]
    masks = [np.eye(c, dtype=bool)]
    h = c // 2
    while h >= 1:
        bi = t // h
        upper = (bi % 2) == 1
        e_up = (u >= bi * h) & (u <= t)
        e_lo = (u > t) & (u <= bi * h + h - 1)
        blocks.append(np.where(upper, e_up, e_lo))
        masks.append(((t // (2 * h)) == (u // (2 * h))) & (((t // h) % 2) == 1) & (((u // h) % 2) == 0))
        h //= 2
    blocks.append(u > t)
    m_f = np.stack(blocks).astype(np.float32)
    k_f = np.stack(masks).astype(np.float32)
    m_b = m_f[:, ::-1, ::-1]
    k_b = k_f[:, ::-1, ::-1]
    sel = list(MM_BLOCKS)
    m = np.stack([m_f[sel].reshape(MM_ROWS, c), m_b[sel].reshape(MM_ROWS, c)])
    m3 = np.concatenate([m, m, m], axis=2)
    return jnp.asarray(m3, BF16), jnp.asarray(np.stack([k_f, k_b]), F32)


def _hgrn_block(dirs, lb, mall_ref, mask_ref, st_ref, z_ref, k_ref, sc_ref, run_ref):
    c = CHUNK
    nchunk = TB // c
    units = [(d, h) for d in range(2) for h in range(HEADS)]
    sl = [slice(h * HEAD_K, (h + 1) * HEAD_K) for h in range(HEADS)]

    def rows(ci, d):
        r0 = ci * c if d == 0 else (nchunk - 1 - ci) * c
        return slice(r0, r0 + c)

    def exponents(ci):
        s = ci % 2
        for d in range(2):
            f = lb[d:d + 1] + (1.0 - lb[d:d + 1]) * jax.nn.sigmoid(dirs[d][1][rows(ci, d), :])
            k_ref[s, d] = 1.0 - f
            k_ref[s, 2 + d] = dirs[d][0][rows(ci, d), :].astype(F32)
            g = jnp.log(f) * LOG2_E
            g1 = g.astype(BF16)
            r1 = g - g1.astype(F32)
            g2 = r1.astype(BF16)
            g3 = (r1 - g2.astype(F32)).astype(BF16)
            gsplit = jnp.concatenate([g1, g2, g3], axis=0)
            ex = _dot(mall_ref[d], gsplit)
            run = ex[0:c]
            run_ref[d] = run
            z_ref[s, d, 0:c] = jnp.exp2(run)
            for j, blk in enumerate(MM_BLOCKS[1:]):
                z_ref[s, d, blk * c:(blk + 1) * c] = jnp.exp2(ex[(j + 1) * c:(j + 2) * c])
            for blk, h in COARSE_LEVELS:
                for base in range(0, c, 2 * h):
                    ref = run_ref[d, base + h - 1 + d:base + h + d, :]
                    if d == 0:
                        first, second = ref - run[base:base + h], run[base + h:base + 2 * h] - ref
                    else:
                        first, second = run[base:base + h] - ref, ref - run[base + h:base + 2 * h]
                    z_ref[s, d, blk * c + base:blk * c + base + h] = jnp.exp2(first)
                    z_ref[s, d, blk * c + base + h:blk * c + base + 2 * h] = jnp.exp2(second)
            end = run_ref[d, c - 1:c, :] if d == 0 else run_ref[d, 0:1, :]
            z_ref[s, d, (N_LEVELS + 1) * c:] = jnp.exp2(end - run)

    def q_of(ci, d, h):
        return k_ref[ci % 2, 2 + d, :, sl[h]]

    def v_of(ci, d, h):
        return dirs[d][2][rows(ci, d), sl[h]].astype(BF16)

    def qz(ci, d, h, blk):
        return (q_of(ci, d, h) * z_ref[ci % 2, d, blk * c:(blk + 1) * c, sl[h]]).astype(BF16)

    def kz(ci, d, h, blk):
        return (k_ref[ci % 2, d, :, sl[h]] * z_ref[ci % 2, d, blk * c:(blk + 1) * c, sl[h]]).astype(BF16)

    def levels(ci):
        for d, h in units:
            q = q_of(ci, d, h).astype(F32)
            k = k_ref[ci % 2, d, :, sl[h]]
            k_next = pltpu.roll(k, 1 if d == 0 else c - 1, 0)
            zq = q * z_ref[ci % 2, d, N_LEVELS * c:(N_LEVELS + 1) * c, sl[h]]
            diag = jnp.sum(q * k, axis=1, keepdims=True)
            near = jnp.sum(zq * k_next, axis=1, keepdims=True)
            sc_ref[d, h] = mask_ref[d, 0] * diag + mask_ref[d, N_LEVELS] * near
        for lev in range(N_LEVELS - 1):
            for d, h in units:
                sc_ref[d, h] += mask_ref[d, lev + 1] * _dot_nt(qz(ci, d, h, lev + 1), kz(ci, d, h, lev + 1))

    def tail(ci):
        for d, h in units:
            o = (_dot_nt(qz(ci, d, h, 0), st_ref[d, h].astype(BF16))
                 + _dot(sc_ref[d, h].astype(BF16), v_of(ci, d, h)))
            dirs[d][3][rows(ci, d), sl[h]] = o * (HEAD_K ** -0.5)
        for d, h in units:
            tot_row = c - 1 if d == 0 else 0
            decay = z_ref[ci % 2, d, tot_row:tot_row + 1, sl[h]]
            st_ref[d, h] = st_ref[d, h] * decay + _dot_tn(v_of(ci, d, h), kz(ci, d, h, N_LEVELS + 1))

    exponents(0)
    for ci in range(nchunk):
        levels(ci)
        if ci + 1 < nchunk:
            exponents(ci + 1)
        tail(ci)


def _hgrn_kernel(qf_ref, ff_ref, vf_ref, qb_ref, fb_ref, vb_ref, lbraw_ref, s0_ref, mall_ref, mask_ref,
                 of_ref, ob_ref, sout_hbm, st_ref, stage_ref, z_ref, k_ref, sc_ref, run_ref, sem):
    i = pl.program_id(0)
    j = (i - NB_CTX) % LAT_BLOCKS
    is_ctx = i < NB_CTX

    @pl.when(is_ctx)
    def _():
        st_ref[...] = jnp.zeros_like(st_ref)

    @pl.when(jnp.logical_and(jnp.logical_not(is_ctx), j == 0))
    def _():
        for d in range(2):
            for h in range(HEADS):
                st_ref[d, h] = s0_ref[0, d, h].T

    a0 = lbraw_ref[0]
    a1 = lbraw_ref[1]
    mx = jnp.maximum(a0, a1)
    e0 = jnp.exp(a0 - mx)
    e1 = jnp.exp(a1 - mx)
    lb = e0 / (e0 + e1)

    dirs = ((qf_ref, ff_ref, vf_ref, of_ref), (qb_ref, fb_ref, vb_ref, ob_ref))
    _hgrn_block(dirs, lb, mall_ref, mask_ref, st_ref, z_ref, k_ref, sc_ref, run_ref)

    @pl.when(is_ctx)
    def _():
        for d in range(2):
            for h in range(HEADS):
                stage_ref[d, h] = st_ref[d, h].T
        cp = pltpu.make_async_copy(stage_ref, sout_hbm.at[i], sem)
        cp.start()
        cp.wait()


def _bwd_block(i):
    j = (i - NB_CTX) % LAT_BLOCKS
    return jnp.where(i < NB_CTX, i, i - j + (LAT_BLOCKS - 1 - j))


def _hgrn(gates, rest, hgrn_lb, s0, mall, masks):
    nh = HGRN_W
    fwd = lambda col: pl.BlockSpec((TB, nh), lambda i: (i, col))
    bwd = lambda col: pl.BlockSpec((TB, nh), lambda i: (_bwd_block(i), col))
    lat_seq = lambda i: jnp.clip((i - NB_CTX) // LAT_BLOCKS, 0, N_LAT_SEQ - 1)
    return pl.pallas_call(
        _hgrn_kernel,
        out_shape=(jax.ShapeDtypeStruct((T_ALL, nh), F32),
                   jax.ShapeDtypeStruct((T_ALL, nh), F32),
                   jax.ShapeDtypeStruct((N_CTX_SEQ, 2, HEADS, HEAD_K, HEAD_V), F32)),
        grid=(NB,),
        in_specs=[fwd(0), fwd(0), fwd(1), bwd(0), bwd(1), bwd(1),
                  pl.BlockSpec((2, 2, nh), lambda i: (0, 0, 0)),
                  pl.BlockSpec((1, 2, HEADS, HEAD_K, HEAD_V), lambda i: (lat_seq(i), 0, 0, 0, 0)),
                  pl.BlockSpec((2, MM_ROWS, 3 * CHUNK), lambda i: (0, 0, 0)),
                  pl.BlockSpec((2, N_LEVELS + 1, CHUNK, CHUNK), lambda i: (0, 0, 0, 0))],
        out_specs=(pl.BlockSpec((TB, nh), lambda i: (i, 0)),
                   pl.BlockSpec((TB, nh), lambda i: (_bwd_block(i), 0)),
                   pl.BlockSpec(memory_space=pl.ANY)),
        scratch_shapes=[pltpu.VMEM((2, HEADS, HEAD_V, HEAD_K), F32),
                        pltpu.VMEM((2, HEADS, HEAD_K, HEAD_V), F32),
                        pltpu.VMEM((2, 2, EXP_ROWS, HGRN_W), F32),
                        pltpu.VMEM((2, 4, CHUNK, HGRN_W), F32),
                        pltpu.VMEM((2, HEADS, CHUNK, CHUNK), F32),
                        pltpu.VMEM((2, CHUNK, HGRN_W), F32),
                        pltpu.SemaphoreType.DMA],
        compiler_params=_params(),
        name="hgrn",
    )(rest, gates, rest, rest, gates, rest, hgrn_lb, s0, mall, masks)


def _window_bounds(n, w):
    pos = np.arange(n)
    lo = np.clip(pos - w // 2, 0, n - 1)
    hi = np.clip(pos - w // 2 + w - 1, 0, n - 1)
    return lo, hi


def _pool_consts():
    seq, img, cnt_seq, cnt_col = [], [], [], []
    for w in POOL_WINDOWS:
        lo, hi = _window_bounds(CTX_LEN, w)
        u = np.arange(CTX_LEN)[None, :]
        seq.append((u >= lo[:, None]) & (u <= hi[:, None]))
        cnt_seq.append(hi - lo + 1)
        lo, hi = _window_bounds(GRID_W, w)
        u = np.arange(GRID_W)[None, :]
        band = (u >= lo[:, None]) & (u <= hi[:, None])
        img.append(np.kron(np.eye(TB // GRID_W, dtype=bool), band))
        cnt_col.append(np.tile(hi - lo + 1, TB // GRID_W))
    a = np.stack([np.stack(seq), np.stack(img)]).astype(np.float32)
    cnt = np.stack([np.stack(cnt_seq), np.stack(cnt_col)]).astype(np.float32)
    cnt = np.broadcast_to(cnt[..., None], cnt.shape + (POOL_G,))
    return jnp.asarray(a, BF16), jnp.asarray(cnt, F32)


POOL_ROWS = LAT_LEN


def _pool_kernel(u_ref, a_ref, cnt_ref, wp_ref, ps_ref, o_ref, cp_ref):
    i = pl.program_id(0)
    nblk = POOL_ROWS // TB

    def finish(g, r0, nrows, pm):
        sl = slice(g * POOL_G, (g + 1) * POOL_G)
        d = pm - u_ref[pl.ds(r0, nrows), sl].astype(F32)
        y = _dot(d.astype(BF16), wp_ref[g]) * ps_ref[:, sl]
        o_ref[pl.ds(r0, nrows), sl] = y.astype(o_ref.dtype)

    def window_sum(kind, g, b):
        sl = slice(g * POOL_G, (g + 1) * POOL_G)
        return _dot(a_ref[kind, g], u_ref[pl.ds(b * TB, TB), sl]) / cnt_ref[kind, g]

    @pl.when(i < T_CTX // POOL_ROWS)
    def _():
        for g in range(len(POOL_WINDOWS)):
            for b in range(nblk):
                finish(g, b * TB, TB, window_sum(0, g, b))

    @pl.when(i >= T_CTX // POOL_ROWS)
    def _():
        for g, w in enumerate(POOL_WINDOWS):
            for b in range(nblk):
                cp_ref[pl.ds(b * TB, TB), :] = window_sum(1, g, b)
            lo, hi = _window_bounds(GRID_H, w)
            for r in range(GRID_H):
                acc = cp_ref[pl.ds(int(lo[r]) * GRID_W, GRID_W), :]
                for rr in range(int(lo[r]) + 1, int(hi[r]) + 1):
                    acc = acc + cp_ref[pl.ds(rr * GRID_W, GRID_W), :]
                finish(g, r * GRID_W, GRID_W, acc / float(hi[r] - lo[r] + 1))


def _pool(rest, a_pool, cnt_pool, w_pool_bf, pool_scale):
    col = 3
    return pl.pallas_call(
        _pool_kernel,
        out_shape=jax.ShapeDtypeStruct((T_ALL, POOL_W), BF16),
        grid=(T_ALL // POOL_ROWS,),
        in_specs=[pl.BlockSpec((POOL_ROWS, POOL_W), lambda i: (i, col)),
                  pl.BlockSpec((2, 4, TB, TB), lambda i: (0, 0, 0, 0)),
                  pl.BlockSpec((2, 4, TB, POOL_G), lambda i: (0, 0, 0, 0)),
                  pl.BlockSpec((4, POOL_G, POOL_G), lambda i: (0, 0, 0)),
                  pl.BlockSpec((1, POOL_W), lambda i: (0, 0))],
        out_specs=pl.BlockSpec((POOL_ROWS, POOL_W), lambda i: (i, 0)),
        scratch_shapes=[pltpu.VMEM((POOL_ROWS, POOL_G), F32)],
        compiler_params=_params(),
        name="pool",
    )(rest, a_pool, cnt_pool, w_pool_bf, pool_scale)


def _merge_kernel(xc_ref, xl_ref, of_ref, ob_ref, og_ref, yb_ref, ga_ref, gb_ref, mod_ref, hn_ref, wa_ref, wb_ref,
                  wo_ref, n2_ref, wrh_ref, wrl_ref, br_ref,
                  x1_ref, h2_ref, te_ref, tw_ref, hist_ref):
    row = _mod_row(pl.program_id(0))
    gate1 = mod_ref[pl.ds(row, 1), pl.ds(2 * D_MODEL, D_MODEL)]
    shift2 = mod_ref[pl.ds(row, 1), pl.ds(3 * D_MODEL, D_MODEL)]
    scale2 = mod_ref[pl.ds(row, 1), pl.ds(4 * D_MODEL, D_MODEL)]

    halves = [slice(j * (TB // 2), (j + 1) * (TB // 2)) for j in range(2)]
    is_ctx = pl.program_id(0) < NB_CTX

    def head_out(r):
        o = of_ref[r, :] + ob_ref[r, :]
        og = og_ref[r, :].astype(F32)
        ya = jnp.concatenate(
            [_rms(o[:, h * HEAD_V:(h + 1) * HEAD_V], hn_ref[...]) for h in range(HEADS)], axis=1)
        return (ya * (og * jax.nn.sigmoid(og))).astype(BF16)

    ya = [head_out(r) for r in halves]
    pa = [_dot(ya[j], wa_ref[...]) for j in range(2)]
    pb = [_dot(yb_ref[r, :], wb_ref[...]) for r in halves]
    merged = [(jax.nn.sigmoid(ga_ref[r, :].astype(F32)) * pa[j]
               + jax.nn.sigmoid(gb_ref[r, :].astype(F32)) * pb[j]).astype(BF16) for j, r in enumerate(halves)]
    po = [_dot(merged[j], wo_ref[...]) for j in range(2)]
    hh, hl = [], []
    for j, r in enumerate(halves):
        x1 = jnp.where(is_ctx, xc_ref[r, :], xl_ref[r, :]) + gate1 * po[j]
        x1_ref[r, :] = x1
        hi, lo = _split2(_rms(x1, n2_ref[...]) * (1.0 + scale2) + shift2)
        h2_ref[r, :] = hi
        hh.append(hi)
        hl.append(lo)
    hh = jnp.concatenate(hh, axis=0)
    hl = jnp.concatenate(hl, axis=0)

    lt = _dot_nt(wrh_ref[...], hh) + _dot_nt(wrl_ref[...], hh) + _dot_nt(wrh_ref[...], hl) + br_ref[...]
    eidx = lax.broadcasted_iota(I32, (N_EXP, TB), 0)
    vals, idxs, cnt = [], [], jnp.zeros((N_EXP, TB), F32)
    for _ in range(TOP_K):
        m = jnp.max(lt, axis=0, keepdims=True)
        idx = jnp.min(jnp.where(lt == m, eidx, N_EXP), axis=0, keepdims=True)
        sel = eidx == idx
        vals.append(m)
        idxs.append(idx)
        cnt = cnt + sel.astype(F32)
        lt = jnp.where(sel, -jnp.inf, lt)
    ex = [jnp.exp(v - vals[0]) for v in vals]
    den = ex[0] + ex[1] + ex[2] + ex[3]
    tw_ref[0] = jnp.concatenate([e / den for e in ex], axis=0)
    te_ref[0] = jnp.concatenate(idxs, axis=0)
    hist_ref[0] = jnp.sum(cnt, axis=1, keepdims=True).astype(I32)


def _merge(x_ctx, x_lat, o_f, o_b, rest, yb, mod, hgrn_norm, wa_bf, wb_bf, wo_bf, norm2, wr_hi, wr_lo, b_router):
    full = lambda shape: pl.BlockSpec(shape, lambda i: (0,) * len(shape))
    return pl.pallas_call(
        _merge_kernel,
        out_shape=(jax.ShapeDtypeStruct((T_ALL, D_MODEL), F32),
                   jax.ShapeDtypeStruct((T_ALL, D_MODEL), BF16),
                   jax.ShapeDtypeStruct((NB, TOP_K, TB), I32),
                   jax.ShapeDtypeStruct((NB, TOP_K, TB), F32),
                   jax.ShapeDtypeStruct((NB, N_EXP, 1), I32)),
        grid=(NB,),
        in_specs=_x_specs() + [
                  pl.BlockSpec((TB, HGRN_W), lambda i: (i, 0)),
                  pl.BlockSpec((TB, HGRN_W), lambda i: (i, 0)),
                  pl.BlockSpec((TB, HGRN_W), lambda i: (i, 2)),
                  pl.BlockSpec((TB, POOL_W), lambda i: (i, 0)),
                  pl.BlockSpec((TB, D_MODEL), lambda i: (i, 2)),
                  pl.BlockSpec((TB, D_MODEL), lambda i: (i, 3)),
                  full((MOD_ROWS, 6 * D_MODEL)),
                  full((1, HEAD_V)),
                  full((HGRN_W, D_MODEL)),
                  full((POOL_W, D_MODEL)),
                  full((D_MODEL, D_MODEL)),
                  full((1, D_MODEL)),
                  full((N_EXP, D_MODEL)),
                  full((N_EXP, D_MODEL)),
                  full((N_EXP, 1))],
        out_specs=(pl.BlockSpec((TB, D_MODEL), lambda i: (i, 0)),
                   pl.BlockSpec((TB, D_MODEL), lambda i: (i, 0)),
                   pl.BlockSpec((1, TOP_K, TB), lambda i: (i, 0, 0)),
                   pl.BlockSpec((1, TOP_K, TB), lambda i: (i, 0, 0)),
                   pl.BlockSpec((1, N_EXP, 1), lambda i: (i, 0, 0))),
        compiler_params=_params(),
        name="merge",
    )(x_ctx, x_lat, o_f, o_b, rest, yb, rest, rest, mod, hgrn_norm, wa_bf, wb_bf, wo_bf, norm2,
      wr_hi, wr_lo, b_router)


def _local_rows(te_ref, loff_ref, tri_ref):
    te = jnp.concatenate([te_ref[j] for j in range(TBD // TB)], axis=1)
    eidx = lax.broadcasted_iota(I32, (N_EXP, TBD), 0)
    sels = [eidx == te[k:k + 1] for k in range(TOP_K)]
    cnt = sels[0].astype(F32)
    for s in sels[1:]:
        cnt = cnt + s.astype(F32)
    base = _dot(cnt.astype(BF16), tri_ref[...]) + loff_ref[0]
    return [jnp.sum(jnp.where(s, base, 0.0), axis=0, keepdims=True) for s in sels]


def _chunk_relative(rows, r0):
    out = []
    for r in rows:
        inside = jnp.logical_and(r >= r0, r < r0 + SORT_CHUNK)
        out.append(jnp.where(inside, r - r0, -1.0).astype(BF16))
    return out


def _segment_copies(make_copy, local_off, global_off, units):
    big = units >> SEG_SMALL_BITS

    def piece(p, carry):
        off = pl.multiple_of(p * (8 << SEG_SMALL_BITS), 8 << SEG_SMALL_BITS)
        make_copy(pl.multiple_of(local_off + off, 8), pl.multiple_of(global_off + off, 8),
                  8 << SEG_SMALL_BITS).start()
        return carry

    lax.fori_loop(0, big, piece, 0)
    done = big << (SEG_SMALL_BITS + 3)
    for j in reversed(range(SEG_SMALL_BITS)):
        low = done + ((units >> (j + 1)) & ((1 << (SEG_SMALL_BITS - 1 - j)) - 1)) * (16 << j)

        @pl.when(((units >> j) & 1) == 1)
        def _():
            make_copy(pl.multiple_of(local_off + low, 8), pl.multiple_of(global_off + low, 8), 8 << j).start()


def _pack_pairs(x):
    half = D_MODEL // 2
    lo = lax.bitcast_convert_type(x[:, :half], U32) >> 16
    hi = lax.bitcast_convert_type(x[:, half:], U32) & jnp.uint32(0xFFFF0000)
    return hi | lo


def _unpack_pairs(p):
    lo = lax.bitcast_convert_type(p << 16, F32).astype(BF16)
    hi = lax.bitcast_convert_type(p & jnp.uint32(0xFFFF0000), F32).astype(BF16)
    return jnp.concatenate([lo, hi], axis=1)


def _block_rows(loff_s, seg_s, b):
    last = b * N_EXP + N_EXP - 1
    return pl.multiple_of(loff_s[last] + seg_s[last], 8)


def _dispatch_kernel(loff_s, seg_s, gbase_s, tail_s, h2_ref, te_ref, loffv_ref, tri_ref, iota_ref, xs_hbm,
                     loc, zeros, sem, sem_z):
    b = pl.program_id(0)
    slot = b % 2

    def wait_block(blk, s):
        n = _block_rows(loff_s, seg_s, blk)
        pltpu.make_async_copy(loc.at[s, pl.ds(0, n)], xs_hbm.at[pl.ds(0, n)], sem.at[s]).wait()

    lrow = _local_rows(te_ref, loffv_ref, tri_ref)

    @pl.when(b >= 2)
    def _():
        wait_block(b - 2, slot)

    for r0 in range(0, LROWS, SORT_CHUNK):
        rel = _chunk_relative(lrow, r0)
        p = jnp.zeros((SORT_CHUNK, TBD), BF16)
        for k in reversed(range(TOP_K)):
            p = jnp.where(iota_ref[...] == rel[k], jnp.ones_like(p), p)
        loc[slot, r0:r0 + SORT_CHUNK, :] = _pack_pairs(_dot(p, h2_ref[...]))

    def out_copy(a, g, size):
        return pltpu.make_async_copy(loc.at[slot, pl.ds(a, size)], xs_hbm.at[pl.ds(g, size)], sem.at[slot])

    def body(e, carry):
        idx = b * N_EXP + e
        _segment_copies(out_copy, loff_s[idx], gbase_s[idx], seg_s[idx] // 8)
        return carry

    lax.fori_loop(0, N_EXP, body, 0)

    @pl.when(b == NBD - 1)
    def _():
        zeros[...] = jnp.zeros_like(zeros)

        def zero_copy(a, g, size):
            return pltpu.make_async_copy(zeros.at[pl.ds(a, size)], xs_hbm.at[pl.ds(g, size)], sem_z)

        def zbody(e, ztot):
            _segment_copies(zero_copy, 0, tail_s[e], tail_s[N_EXP + e] // 8)
            return ztot + tail_s[N_EXP + e]

        def tbody(t, carry):
            pltpu.make_async_copy(zeros, xs_hbm.at[pl.ds(pl.multiple_of(t * TM, TM), TM)], sem_z).start()
            return carry

        n_used = tail_s[2 * N_EXP]
        lax.fori_loop(n_used, N_TILES, tbody, 0)
        ztot = lax.fori_loop(0, N_EXP, zbody, 0) + (N_TILES - n_used) * TM
        ztot = pl.multiple_of(ztot, 8)

        @pl.when(ztot > 0)
        def _():
            pltpu.make_async_copy(xs_hbm.at[pl.ds(0, ztot)], xs_hbm.at[pl.ds(0, ztot)], sem_z).wait()

        wait_block(b - 1, 1 - slot)
        wait_block(b, slot)


def _dispatch(tables, h2, top_e, tri, row_iota):
    loff_s, seg_s, gbase_s, tail_s, loff_v = tables
    grid_spec = pltpu.PrefetchScalarGridSpec(
        num_scalar_prefetch=4,
        grid=(NBD,),
        in_specs=[pl.BlockSpec((TBD, D_MODEL), lambda i, *_: (i, 0)),
                  pl.BlockSpec((TBD // TB, TOP_K, TB), lambda i, *_: (i, 0, 0)),
                  pl.BlockSpec((1, N_EXP, 1), lambda i, *_: (i, 0, 0)),
                  pl.BlockSpec((TBD, TBD), lambda i, *_: (0, 0)),
                  pl.BlockSpec((SORT_CHUNK, TBD), lambda i, *_: (0, 0))],
        out_specs=pl.BlockSpec(memory_space=pl.ANY),
        scratch_shapes=[pltpu.VMEM((2, LROWS, D_MODEL // 2), U32),
                        pltpu.VMEM((TM, D_MODEL // 2), U32),
                        pltpu.SemaphoreType.DMA((2,)),
                        pltpu.SemaphoreType.DMA])
    return pl.pallas_call(
        _dispatch_kernel,
        out_shape=jax.ShapeDtypeStruct((N_TILES * TM, D_MODEL // 2), U32),
        grid_spec=grid_spec,
        compiler_params=_params(),
        name="dispatch",
    )(loff_s, seg_s, gbase_s, tail_s, h2, top_e, loff_v, tri, row_iota)


def _moe_kernel(te_ref, first_ref, par_ref, next_ref, nv_ref, nu_ref, xs_ref, bgu_ref, bd_ref, wgu_hbm, wd_hbm, o_ref,
                wgu_st, wd_st, wgu_bf, wd_bf, sem):
    i = pl.program_id(0)

    def fetch(e, s):
        return (pltpu.make_async_copy(wgu_hbm.at[e], wgu_st.at[s], sem.at[0, s]),
                pltpu.make_async_copy(wd_hbm.at[e], wd_st.at[s], sem.at[1, s]))

    @pl.when(i < nu_ref[0])
    def _():
        @pl.when(first_ref[i] == 1)
        def _():
            s = par_ref[i]

            @pl.when(i == 0)
            def _():
                for cp in fetch(te_ref[0], 0):
                    cp.start()

            for cp in fetch(te_ref[i], s):
                cp.wait()

            @pl.when(next_ref[i] >= 0)
            def _():
                for cp in fetch(next_ref[i], 1 - s):
                    cp.start()

            wgu_bf[...] = wgu_st[s].astype(BF16)
            wd_bf[...] = wd_st[s].astype(BF16)

        half = TM // 2
        halves = [slice(0, half), slice(half, TM)]

        def gate_up(r):
            return _dot(_unpack_pairs(xs_ref[r, :]), wgu_bf[...]) + bgu_ref[0]

        def activation(gu):
            gate = jnp.minimum(gu[:, :D_FF], SWIGLU_LIMIT)
            up = jnp.clip(gu[:, D_FF:], -SWIGLU_LIMIT, SWIGLU_LIMIT)
            return ((up + 1.0) * gate * jax.nn.sigmoid(SWIGLU_ALPHA * gate)).astype(BF16)

        def down(r, act):
            out = _dot(act, wd_bf[...]) + bd_ref[0]
            o_ref[r, :] = _pack_pairs(out.astype(BF16).astype(F32))

        @pl.when(nv_ref[i] > half)
        def _():
            gu = [gate_up(r) for r in halves]
            for r, g in zip(halves, gu):
                down(r, activation(g))

        @pl.when(nv_ref[i] <= half)
        def _():
            down(halves[0], activation(gate_up(halves[0])))
            o_ref[halves[1], :] = jnp.zeros((half, D_MODEL // 2), U32)

    @pl.when(i >= nu_ref[0])
    def _():
        o_ref[...] = jnp.zeros_like(o_ref)


def _moe(tile_tables, xs, w_gate_up, b_gate_up, w_down, b_down):
    nsp = len(tile_tables)
    row_tile = lambda i, *s: (jnp.minimum(i, s[nsp - 1][0] - 1), 0)
    grid_spec = pltpu.PrefetchScalarGridSpec(
        num_scalar_prefetch=nsp,
        grid=(N_TILES,),
        in_specs=[pl.BlockSpec((TM, D_MODEL // 2), row_tile),
                  pl.BlockSpec((1, 1, 2 * D_FF), lambda i, te, *_: (te[i], 0, 0)),
                  pl.BlockSpec((1, 1, D_MODEL), lambda i, te, *_: (te[i], 0, 0)),
                  pl.BlockSpec(memory_space=pl.ANY),
                  pl.BlockSpec(memory_space=pl.ANY)],
        out_specs=pl.BlockSpec((TM, D_MODEL // 2), lambda i, *_: (i, 0)),
        scratch_shapes=[pltpu.VMEM((2, D_MODEL, 2 * D_FF), F32),
                        pltpu.VMEM((2, D_FF, D_MODEL), F32),
                        pltpu.VMEM((D_MODEL, 2 * D_FF), BF16),
                        pltpu.VMEM((D_FF, D_MODEL), BF16),
                        pltpu.SemaphoreType.DMA((2, 2))])
    return pl.pallas_call(
        _moe_kernel,
        out_shape=jax.ShapeDtypeStruct((N_TILES * TM, D_MODEL // 2), U32),
        grid_spec=grid_spec,
        compiler_params=_params(),
        name="moe",
    )(*tile_tables, xs, b_gate_up.reshape(N_EXP, 1, 2 * D_FF), b_down.reshape(N_EXP, 1, D_MODEL),
      w_gate_up, w_down)


def _final_kernel(loff_s, seg_s, gbase_s, ys_hbm, x1_ref, te_ref, tw_ref, loffv_ref, tri_ref, iota_ref, mod_ref, fn_ref,
                  oc_ref, ol_ref, loc, sem):
    b = pl.program_id(0)
    slot = b % 2

    def start_block(blk, s):
        def in_copy(a, g, size):
            return pltpu.make_async_copy(ys_hbm.at[pl.ds(g, size)], loc.at[s, pl.ds(a, size)], sem.at[s])

        def body(e, carry):
            idx = blk * N_EXP + e
            _segment_copies(in_copy, loff_s[idx], gbase_s[idx], seg_s[idx] // 8)
            return carry

        lax.fori_loop(0, N_EXP, body, 0)

    @pl.when(b == 0)
    def _():
        loc[...] = jnp.zeros_like(loc)
        start_block(0, 0)

    @pl.when(b + 1 < NBD)
    def _():
        start_block(b + 1, 1 - slot)

    lrow = _local_rows(te_ref, loffv_ref, tri_ref)
    tw = jnp.concatenate([tw_ref[j] for j in range(TBD // TB)], axis=1)
    rows8 = jnp.concatenate(lrow + [tw], axis=0)
    cols = jnp.concatenate([rows8, jnp.zeros((128 - 2 * TOP_K, TBD), F32)], axis=0).T

    n = _block_rows(loff_s, seg_s, b)
    pltpu.make_async_copy(ys_hbm.at[pl.ds(0, n)], loc.at[slot, pl.ds(0, n)], sem.at[slot]).wait()

    y = None
    wts = [cols[:, TOP_K + k:TOP_K + k + 1].astype(BF16) for k in range(TOP_K)]
    for r0 in range(0, LROWS, SORT_CHUNK):
        rel = _chunk_relative([cols[:, k:k + 1] for k in range(TOP_K)], r0)
        pw = jnp.zeros((TBD, SORT_CHUNK), BF16)
        for k in reversed(range(TOP_K)):
            pw = jnp.where(iota_ref[...] == rel[k], wts[k], pw)
        part = _dot(pw, _unpack_pairs(loc[slot, r0:r0 + SORT_CHUNK, :]))
        y = part if y is None else y + part

    row = jnp.where(b < NBD_CTX, N_LAT_SEQ, (b - NBD_CTX) // (LAT_LEN // TBD))
    gate2 = mod_ref[pl.ds(row, 1), pl.ds(5 * D_MODEL, D_MODEL)]
    out = _rms(x1_ref[...] + gate2 * y, fn_ref[...])

    @pl.when(b < NBD_CTX)
    def _():
        oc_ref[...] = out

    @pl.when(b >= NBD_CTX)
    def _():
        ol_ref[...] = out


def _final(tables, ys, x1, top_e, top_w, tri, col_iota, mod, final_norm):
    loff_s, seg_s, gbase_s, _, loff_v = tables
    grid_spec = pltpu.PrefetchScalarGridSpec(
        num_scalar_prefetch=3,
        grid=(NBD,),
        in_specs=[pl.BlockSpec(memory_space=pl.ANY),
                  pl.BlockSpec((TBD, D_MODEL), lambda i, *_: (i, 0)),
                  pl.BlockSpec((TBD // TB, TOP_K, TB), lambda i, *_: (i, 0, 0)),
                  pl.BlockSpec((TBD // TB, TOP_K, TB), lambda i, *_: (i, 0, 0)),
                  pl.BlockSpec((1, N_EXP, 1), lambda i, *_: (i, 0, 0)),
                  pl.BlockSpec((TBD, TBD), lambda i, *_: (0, 0)),
                  pl.BlockSpec((TBD, SORT_CHUNK), lambda i, *_: (0, 0)),
                  pl.BlockSpec((MOD_ROWS, 6 * D_MODEL), lambda i, *_: (0, 0)),
                  pl.BlockSpec((1, D_MODEL), lambda i, *_: (0, 0))],
        out_specs=(pl.BlockSpec((TBD, D_MODEL), lambda i, *_: (jnp.minimum(i, NBD_CTX - 1), 0)),
                   pl.BlockSpec((TBD, D_MODEL), lambda i, *_: (jnp.maximum(i - NBD_CTX, 0), 0))),
        scratch_shapes=[pltpu.VMEM((2, LROWS, D_MODEL // 2), U32),
                        pltpu.SemaphoreType.DMA((2,))])
    return pl.pallas_call(
        _final_kernel,
        out_shape=(jax.ShapeDtypeStruct((T_CTX, D_MODEL), F32),
                   jax.ShapeDtypeStruct((T_LAT, D_MODEL), F32)),
        grid_spec=grid_spec,
        compiler_params=_params(),
        name="final",
    )(loff_s, seg_s, gbase_s, ys, x1, top_e, top_w, loff_v, tri, col_iota, mod, final_norm)


def _dispatch_tables(hist):
    hist = jnp.sum(hist.reshape(NBD, TBD // TB, N_EXP), axis=1)
    seg = ((hist + 7) // 8) * 8
    loff = jnp.cumsum(seg, axis=1) - seg
    rows_e = jnp.sum(seg, axis=0)
    region = ((rows_e + TM - 1) // TM) * TM
    region_end = jnp.cumsum(region)
    region_start = region_end - region
    gbase = region_start[None, :] + jnp.cumsum(seg, axis=0) - seg
    n_used = (region_end[-1] // TM).astype(I32)
    tail = jnp.concatenate([region_start + rows_e, region - rows_e, n_used.reshape(1)])
    start = jnp.arange(N_TILES, dtype=I32) * TM
    tile_e = jnp.sum((start[:, None] >= region_end[None, :]).astype(I32), axis=1)
    tile_e = jnp.minimum(tile_e, tile_e[jnp.maximum(n_used - 1, 0)])
    first = jnp.concatenate([jnp.ones((1,), I32), (tile_e[1:] != tile_e[:-1]).astype(I32)])
    parity = (jnp.cumsum(first) - 1) % 2
    later = jnp.where(tile_e[None, :] > tile_e[:, None], tile_e[None, :], N_EXP)
    nxt = jnp.min(later, axis=1)
    nxt = jnp.where(nxt == N_EXP, -1, nxt)
    flat = lambda a: a.reshape(-1).astype(I32)
    tables = (flat(loff), flat(seg), flat(gbase), flat(tail), loff.astype(F32).reshape(NBD, N_EXP, 1))
    mine = tile_e[:, None] == jnp.arange(N_EXP, dtype=I32)[None, :]
    data_end = jnp.sum(jnp.where(mine, (region_start + rows_e)[None, :], 0), axis=1)
    tile_nv = jnp.clip(data_end - start, 0, TM)
    tile_tables = (flat(tile_e), flat(first), flat(parity), flat(nxt), flat(tile_nv), n_used.reshape(1))
    return tables, tile_tables


def kernel(x_prompt, x_sample, state_hgrn, c, c_ctx, w_ada, b_ada, norm1, w_in, hgrn_lb, hgrn_norm, w_pool,
           pool_scale, w_branch_a, w_branch_b, w_out, norm2, w_router, b_router, w_gate_up, b_gate_up,
           w_down, b_down, final_norm):
    x_ctx = x_prompt.reshape(T_CTX, D_MODEL)
    x_lat = x_sample.reshape(T_LAT, D_MODEL)
    cc = jnp.zeros((MOD_ROWS, D_MODEL), F32).at[:N_LAT_SEQ].set(c).at[N_LAT_SEQ].set(c_ctx)
    mod = _ada(cc, w_ada[0], b_ada)

    w_in_bf = w_in[0].astype(BF16)
    w_gates = w_in_bf[:, HGRN_W:HGRN_W + GATE_W]
    w_rest = jnp.concatenate([w_in_bf[:, :HGRN_W], w_in_bf[:, HGRN_W + GATE_W:]], axis=1)
    gates, rest = _inproj(x_ctx, x_lat, mod, norm1, w_gates, w_rest)

    mall, masks = _hgrn_consts()
    o_f, o_b, new_state = _hgrn(gates, rest, hgrn_lb, state_hgrn[:, 0], mall, masks)

    a_pool, cnt_pool = _pool_consts()
    yb = _pool(rest, a_pool, cnt_pool, w_pool[0].astype(BF16), pool_scale)

    wr_t = w_router[0].T
    wr_hi = wr_t.astype(BF16)
    wr_lo = (wr_t - wr_hi.astype(F32)).astype(BF16)
    tri = jnp.asarray(np.triu(np.ones((TBD, TBD), np.float32), 1), BF16)
    x1, h2, top_e, top_w, hist = _merge(
        x_ctx, x_lat, o_f, o_b, rest, yb, mod, hgrn_norm, w_branch_a[0].astype(BF16), w_branch_b[0].astype(BF16),
        w_out[0].astype(BF16), norm2, wr_hi, wr_lo, b_router.reshape(N_EXP, 1))

    tables, tile_tables = _dispatch_tables(hist)
    row_iota = jnp.asarray(np.broadcast_to(np.arange(SORT_CHUNK, dtype=np.float32)[:, None], (SORT_CHUNK, TBD)), BF16)
    xs = _dispatch(tables, h2, top_e, tri, row_iota)
    ys = _moe(tile_tables, xs, w_gate_up[0], b_gate_up[0], w_down[0], b_down[0])
    y_ctx, y_lat = _final(tables, ys, x1, top_e, top_w, tri, row_iota.T, mod, final_norm.reshape(1, D_MODEL))
    y_prompt = y_ctx.reshape(N_CTX_SEQ, CTX_LEN, D_MODEL)
    y_sample = y_lat.reshape(N_LAT_SEQ, LAT_LEN, D_MODEL)
    return y_prompt, y_sample, new_state[:, None]
```

```python
import functools

import numpy as np
import jax
import jax.numpy as jnp
from jax import lax
from jax.experimental import pallas as pl
from jax.experimental.pallas import tpu as pltpu

F32 = jnp.float32
BF16 = jnp.bfloat16
I32 = jnp.int32
U32 = jnp.uint32

D_MODEL = 1024
N_CTX_SEQ, CTX_LEN = 32, 256
N_LAT_SEQ, LAT_LEN = 4, 2048
T_CTX = N_CTX_SEQ * CTX_LEN
T_LAT = N_LAT_SEQ * LAT_LEN
T_ALL = T_CTX + T_LAT
TB = 256
NB = T_ALL // TB
NB_CTX = T_CTX // TB
LAT_BLOCKS = LAT_LEN // TB
HEADS, HEAD_K, HEAD_V = 4, 128, 128
HGRN_W = HEADS * HEAD_V
POOL_WINDOWS = (2, 4, 8, 16)
POOL_G = 128
POOL_W = len(POOL_WINDOWS) * POOL_G
GRID_W = 64
GRID_H = LAT_LEN // GRID_W
IN_W = 5 * HGRN_W + POOL_W + 2 * D_MODEL
GATE_W = 2 * HGRN_W
REST_W = IN_W - GATE_W
N_EXP, TOP_K, D_FF = 32, 4, 1024
SWIGLU_LIMIT = 7.0
SWIGLU_ALPHA = 1.702
EPS = 1e-6
LOG2_E = 1.4426950408889634
CHUNK = 64
N_LEVELS = 6
EXP_ROWS = (N_LEVELS + 2) * CHUNK
MM_BLOCKS = (0, 4, 5, 6)
MM_ROWS = len(MM_BLOCKS) * CHUNK
COARSE_LEVELS = ((1, 32), (2, 16), (3, 8))
TM = 512
TBD = 512
NBD = T_ALL // TBD
NBD_CTX = T_CTX // TBD
SORT_CHUNK = 256
LROWS = TBD * TOP_K + N_EXP * 8
SEG_SMALL_BITS = 3
N_TILES = -(-(T_ALL * TOP_K + NBD * N_EXP * 7 + N_EXP * (TM - 1)) // TM)
MOD_ROWS = 8
VMEM_LIMIT = 56 * 1024 * 1024


def _params(sem=("arbitrary",)):
    return pltpu.CompilerParams(dimension_semantics=sem, vmem_limit_bytes=VMEM_LIMIT)


def _dot(a, b):
    return jnp.dot(a, b, preferred_element_type=F32)


def _dot_nt(a, b):
    return lax.dot_general(a, b, (((1,), (1,)), ((), ())), preferred_element_type=F32)


def _dot_tn(a, b):
    return lax.dot_general(a, b, (((0,), (0,)), ((), ())), preferred_element_type=F32)


def _split2(x):
    hi = x.astype(BF16)
    lo = (x - hi.astype(F32)).astype(BF16)
    return hi, lo


def _mod_row(i):
    return jnp.where(i < NBD_CTX, N_LAT_SEQ, (i - NBD_CTX) // (LAT_LEN // TBD))


def _ada_kernel(c_ref, w_ref, b_ref, o_ref):
    c = c_ref[...]
    s = c * jax.nn.sigmoid(c)
    o_ref[...] = jnp.dot(s, w_ref[...], preferred_element_type=F32,
                         precision=lax.Precision.HIGHEST) + b_ref[...]


def _ada(cc, w_ada, b_ada):
    nblk = 1536
    return pl.pallas_call(
        _ada_kernel,
        out_shape=jax.ShapeDtypeStruct((MOD_ROWS, 6 * D_MODEL), F32),
        grid=(6 * D_MODEL // nblk,),
        in_specs=[pl.BlockSpec((MOD_ROWS, D_MODEL), lambda j: (0, 0)),
                  pl.BlockSpec((D_MODEL, nblk), lambda j: (0, j)),
                  pl.BlockSpec((1, nblk), lambda j: (0, j))],
        out_specs=pl.BlockSpec((MOD_ROWS, nblk), lambda j: (0, j)),
        compiler_params=_params(),
        name="ada",
    )(cc, w_ada, b_ada)


def _rms(x, g):
    ms = jnp.mean(x * x, axis=-1, keepdims=True)
    return x * lax.rsqrt(ms + EPS) * g


def _x_specs():
    return [pl.BlockSpec((TBD, D_MODEL), lambda i, *_: (jnp.minimum(i, NBD_CTX - 1), 0)),
            pl.BlockSpec((TBD, D_MODEL), lambda i, *_: (jnp.maximum(i - NBD_CTX, 0), 0))]


def _x_block(xc_ref, xl_ref):
    return jnp.where(pl.program_id(0) < NBD_CTX, xc_ref[...], xl_ref[...])


def _inproj_kernel(xc_ref, xl_ref, mod_ref, n1_ref, wg_ref, wr_ref, og_ref, or_ref):
    row = _mod_row(pl.program_id(0))
    shift = mod_ref[pl.ds(row, 1), pl.ds(0, D_MODEL)]
    scale = mod_ref[pl.ds(row, 1), pl.ds(D_MODEL, D_MODEL)]
    h = (_rms(_x_block(xc_ref, xl_ref), n1_ref[...]) * (1.0 + scale) + shift).astype(BF16)
    og_ref[...] = _dot(h, wg_ref[...])
    or_ref[...] = _dot(h, wr_ref[...]).astype(BF16)


def _inproj(x_ctx, x_lat, mod, norm1, w_gates_bf, w_rest_bf):
    return pl.pallas_call(
        _inproj_kernel,
        out_shape=(jax.ShapeDtypeStruct((T_ALL, GATE_W), F32),
                   jax.ShapeDtypeStruct((T_ALL, REST_W), BF16)),
        grid=(NBD,),
        in_specs=_x_specs() + [
                  pl.BlockSpec((MOD_ROWS, 6 * D_MODEL), lambda i: (0, 0)),
                  pl.BlockSpec((1, D_MODEL), lambda i: (0, 0)),
                  pl.BlockSpec((D_MODEL, GATE_W), lambda i: (0, 0)),
                  pl.BlockSpec((D_MODEL, REST_W), lambda i: (0, 0))],
        out_specs=(pl.BlockSpec((TBD, GATE_W), lambda i: (i, 0)),
                   pl.BlockSpec((TBD, REST_W), lambda i: (i, 0))),
        compiler_params=_params(),
        name="inproj",
    )(x_ctx, x_lat, mod, norm1, w_gates_bf, w_rest_bf)


def _hgrn_consts():
    c = CHUNK
    t = np.arange(c)[:, None]
    u = np.arange(c)[None, :]
    blocks = [u <= t]
    masks = [np.eye(c, dtype=bool)]
    h = c // 2
    while h >= 1:
        bi = t // h
        upper = (bi % 2) == 1
        e_up = (u >= bi * h) & (u <= t)
        e_lo = (u > t) & (u <= bi * h + h - 1)
        blocks.append(np.where(upper, e_up, e_lo))
        masks.append(((t // (2 * h)) == (u // (2 * h))) & (((t // h) % 2) == 1) & (((u // h) % 2) == 0))
        h //= 2
    blocks.append(u > t)
    m_f = np.stack(blocks).astype(np.float32)
    k_f = np.stack(masks).astype(np.float32)
    m_b = m_f[:, ::-1, ::-1]
    k_b = k_f[:, ::-1, ::-1]
    sel = list(MM_BLOCKS)
    m = np.stack([m_f[sel].reshape(MM_ROWS, c), m_b[sel].reshape(MM_ROWS, c)])
    m3 = np.concatenate([m, m, m], axis=2)
    return jnp.asarray(m3, BF16), jnp.asarray(np.stack([k_f, k_b]), F32)


def _hgrn_block(dirs, lb, mall_ref, mask_ref, st_ref, z_ref, k_ref, sc_ref, run_ref):
    c = CHUNK
    nchunk = TB // c
    units = [(d, h) for d in range(2) for h in range(HEADS)]
    sl = [slice(h * HEAD_K, (h + 1) * HEAD_K) for h in range(HEADS)]

    def rows(ci, d):
        r0 = ci * c if d == 0 else (nchunk - 1 - ci) * c
        return slice(r0, r0 + c)

    def exponents(ci):
        s = ci % 2
        for d in range(2):
            f = lb[d:d + 1] + (1.0 - lb[d:d + 1]) * jax.nn.sigmoid(dirs[d][1][rows(ci, d), :])
            k_ref[s, d] = 1.0 - f
            k_ref[s, 2 + d] = dirs[d][0][rows(ci, d), :].astype(F32)
            g = jnp.log(f) * LOG2_E
            g1 = g.astype(BF16)
            r1 = g - g1.astype(F32)
            g2 = r1.astype(BF16)
            g3 = (r1 - g2.astype(F32)).astype(BF16)
            gsplit = jnp.concatenate([g1, g2, g3], axis=0)
            ex = _dot(mall_ref[d], gsplit)
            run = ex[0:c]
            run_ref[d] = run
            z_ref[s, d, 0:c] = jnp.exp2(run)
            for j, blk in enumerate(MM_BLOCKS[1:]):
                z_ref[s, d, blk * c:(blk + 1) * c] = jnp.exp2(ex[(j + 1) * c:(j + 2) * c])
            for blk, h in COARSE_LEVELS:
                for base in range(0, c, 2 * h):
                    ref = run_ref[d, base + h - 1 + d:base + h + d, :]
                    if d == 0:
                        first, second = ref - run[base:base + h], run[base + h:base + 2 * h] - ref
                    else:
                        first, second = run[base:base + h] - ref, ref - run[base + h:base + 2 * h]
                    z_ref[s, d, blk * c + base:blk * c + base + h] = jnp.exp2(first)
                    z_ref[s, d, blk * c + base + h:blk * c + base + 2 * h] = jnp.exp2(second)
            end = run_ref[d, c - 1:c, :] if d == 0 else run_ref[d, 0:1, :]
            z_ref[s, d, (N_LEVELS + 1) * c:] = jnp.exp2(end - run)

    def q_of(ci, d, h):
        return k_ref[ci % 2, 2 + d, :, sl[h]]

    def v_of(ci, d, h):
        return dirs[d][2][rows(ci, d), sl[h]].astype(BF16)

    def qz(ci, d, h, blk):
        return (q_of(ci, d, h) * z_ref[ci % 2, d, blk * c:(blk + 1) * c, sl[h]]).astype(BF16)

    def kz(ci, d, h, blk):
        return (k_ref[ci % 2, d, :, sl[h]] * z_ref[ci % 2, d, blk * c:(blk + 1) * c, sl[h]]).astype(BF16)

    def levels(ci):
        for d, h in units:
            q = q_of(ci, d, h).astype(F32)
            k = k_ref[ci % 2, d, :, sl[h]]
            k_next = pltpu.roll(k, 1 if d == 0 else c - 1, 0)
            zq = q * z_ref[ci % 2, d, N_LEVELS * c:(N_LEVELS + 1) * c, sl[h]]
            diag = jnp.sum(q * k, axis=1, keepdims=True)
            near = jnp.sum(zq * k_next, axis=1, keepdims=True)
            sc_ref[d, h] = mask_ref[d, 0] * diag + mask_ref[d, N_LEVELS] * near
        for lev in range(N_LEVELS - 1):
            for d, h in units:
                sc_ref[d, h] += mask_ref[d, lev + 1] * _dot_nt(qz(ci, d, h, lev + 1), kz(ci, d, h, lev + 1))

    def tail(ci):
        for d, h in units:
            o = (_dot_nt(qz(ci, d, h, 0), st_ref[d, h].astype(BF16))
                 + _dot(sc_ref[d, h].astype(BF16), v_of(ci, d, h)))
            dirs[d][3][rows(ci, d), sl[h]] = o * (HEAD_K ** -0.5)
        for d, h in units:
            tot_row = c - 1 if d == 0 else 0
            decay = z_ref[ci % 2, d, tot_row:tot_row + 1, sl[h]]
            st_ref[d, h] = st_ref[d, h] * decay + _dot_tn(v_of(ci, d, h), kz(ci, d, h, N_LEVELS + 1))

    exponents(0)
    for ci in range(nchunk):
        levels(ci)
        if ci + 1 < nchunk:
            exponents(ci + 1)
        tail(ci)


def _hgrn_kernel(qf_ref, ff_ref, vf_ref, qb_ref, fb_ref, vb_ref, lbraw_ref, s0_ref, mall_ref, mask_ref,
                 of_ref, ob_ref, sout_hbm, st_ref, stage_ref, z_ref, k_ref, sc_ref, run_ref, sem):
    i = pl.program_id(0)
    j = (i - NB_CTX) % LAT_BLOCKS
    is_ctx = i < NB_CTX

    @pl.when(is_ctx)
    def _():
        st_ref[...] = jnp.zeros_like(st_ref)

    @pl.when(jnp.logical_and(jnp.logical_not(is_ctx), j == 0))
    def _():
        for d in range(2):
            for h in range(HEADS):
                st_ref[d, h] = s0_ref[0, d, h].T

    a0 = lbraw_ref[0]
    a1 = lbraw_ref[1]
    mx = jnp.maximum(a0, a1)
    e0 = jnp.exp(a0 - mx)
    e1 = jnp.exp(a1 - mx)
    lb = e0 / (e0 + e1)

    dirs = ((qf_ref, ff_ref, vf_ref, of_ref), (qb_ref, fb_ref, vb_ref, ob_ref))
    _hgrn_block(dirs, lb, mall_ref, mask_ref, st_ref, z_ref, k_ref, sc_ref, run_ref)

    @pl.when(is_ctx)
    def _():
        for d in range(2):
            for h in range(HEADS):
                stage_ref[d, h] = st_ref[d, h].T
        cp = pltpu.make_async_copy(stage_ref, sout_hbm.at[i], sem)
        cp.start()
        cp.wait()


def _bwd_block(i):
    j = (i - NB_CTX) % LAT_BLOCKS
    return jnp.where(i < NB_CTX, i, i - j + (LAT_BLOCKS - 1 - j))


def _hgrn(gates, rest, hgrn_lb, s0, mall, masks):
    nh = HGRN_W
    fwd = lambda col: pl.BlockSpec((TB, nh), lambda i: (i, col))
    bwd = lambda col: pl.BlockSpec((TB, nh), lambda i: (_bwd_block(i), col))
    lat_seq = lambda i: jnp.clip((i - NB_CTX) // LAT_BLOCKS, 0, N_LAT_SEQ - 1)
    return pl.pallas_call(
        _hgrn_kernel,
        out_shape=(jax.ShapeDtypeStruct((T_ALL, nh), F32),
                   jax.ShapeDtypeStruct((T_ALL, nh), F32),
                   jax.ShapeDtypeStruct((N_CTX_SEQ, 2, HEADS, HEAD_K, HEAD_V), F32)),
        grid=(NB,),
        in_specs=[fwd(0), fwd(0), fwd(1), bwd(0), bwd(1), bwd(1),
                  pl.BlockSpec((2, 2, nh), lambda i: (0, 0, 0)),
                  pl.BlockSpec((1, 2, HEADS, HEAD_K, HEAD_V), lambda i: (lat_seq(i), 0, 0, 0, 0)),
                  pl.BlockSpec((2, MM_ROWS, 3 * CHUNK), lambda i: (0, 0, 0)),
                  pl.BlockSpec((2, N_LEVELS + 1, CHUNK, CHUNK), lambda i: (0, 0, 0, 0))],
        out_specs=(pl.BlockSpec((TB, nh), lambda i: (i, 0)),
                   pl.BlockSpec((TB, nh), lambda i: (_bwd_block(i), 0)),
                   pl.BlockSpec(memory_space=pl.ANY)),
        scratch_shapes=[pltpu.VMEM((2, HEADS, HEAD_V, HEAD_K), F32),
                        pltpu.VMEM((2, HEADS, HEAD_K, HEAD_V), F32),
                        pltpu.VMEM((2, 2, EXP_ROWS, HGRN_W), F32),
                        pltpu.VMEM((2, 4, CHUNK, HGRN_W), F32),
                        pltpu.VMEM((2, HEADS, CHUNK, CHUNK), F32),
                        pltpu.VMEM((2, CHUNK, HGRN_W), F32),
                        pltpu.SemaphoreType.DMA],
        compiler_params=_params(),
        name="hgrn",
    )(rest, gates, rest, rest, gates, rest, hgrn_lb, s0, mall, masks)


def _window_bounds(n, w):
    pos = np.arange(n)
    lo = np.clip(pos - w // 2, 0, n - 1)
    hi = np.clip(pos - w // 2 + w - 1, 0, n - 1)
    return lo, hi


def _pool_consts():
    seq, img, cnt_seq, cnt_col = [], [], [], []
    for w in POOL_WINDOWS:
        lo, hi = _window_bounds(CTX_LEN, w)
        u = np.arange(CTX_LEN)[None, :]
        seq.append((u >= lo[:, None]) & (u <= hi[:, None]))
        cnt_seq.append(hi - lo + 1)
        lo, hi = _window_bounds(GRID_W, w)
        u = np.arange(GRID_W)[None, :]
        band = (u >= lo[:, None]) & (u <= hi[:, None])
        img.append(np.kron(np.eye(TB // GRID_W, dtype=bool), band))
        cnt_col.append(np.tile(hi - lo + 1, TB // GRID_W))
    a = np.stack([np.stack(seq), np.stack(img)]).astype(np.float32)
    cnt = np.stack([np.stack(cnt_seq), np.stack(cnt_col)]).astype(np.float32)
    cnt = np.broadcast_to(cnt[..., None], cnt.shape + (POOL_G,))
    return jnp.asarray(a, BF16), jnp.asarray(cnt, F32)


POOL_ROWS = LAT_LEN


def _pool_kernel(u_ref, a_ref, cnt_ref, wp_ref, ps_ref, o_ref, cp_ref):
    i = pl.program_id(0)
    nblk = POOL_ROWS // TB

    def finish(g, r0, nrows, pm):
        sl = slice(g * POOL_G, (g + 1) * POOL_G)
        d = pm - u_ref[pl.ds(r0, nrows), sl].astype(F32)
        y = _dot(d.astype(BF16), wp_ref[g]) * ps_ref[:, sl]
        o_ref[pl.ds(r0, nrows), sl] = y.astype(o_ref.dtype)

    def window_sum(kind, g, b):
        sl = slice(g * POOL_G, (g + 1) * POOL_G)
        return _dot(a_ref[kind, g], u_ref[pl.ds(b * TB, TB), sl]) / cnt_ref[kind, g]

    @pl.when(i < T_CTX // POOL_ROWS)
    def _():
        for g in range(len(POOL_WINDOWS)):
            for b in range(nblk):
                finish(g, b * TB, TB, window_sum(0, g, b))

    @pl.when(i >= T_CTX // POOL_ROWS)
    def _():
        for g, w in enumerate(POOL_WINDOWS):
            for b in range(nblk):
                cp_ref[pl.ds(b * TB, TB), :] = window_sum(1, g, b)
            lo, hi = _window_bounds(GRID_H, w)
            for r in range(GRID_H):
                acc = cp_ref[pl.ds(int(lo[r]) * GRID_W, GRID_W), :]
                for rr in range(int(lo[r]) + 1, int(hi[r]) + 1):
                    acc = acc + cp_ref[pl.ds(rr * GRID_W, GRID_W), :]
                finish(g, r * GRID_W, GRID_W, acc / float(hi[r] - lo[r] + 1))


def _pool(rest, a_pool, cnt_pool, w_pool_bf, pool_scale):
    col = 3
    return pl.pallas_call(
        _pool_kernel,
        out_shape=jax.ShapeDtypeStruct((T_ALL, POOL_W), BF16),
        grid=(T_ALL // POOL_ROWS,),
        in_specs=[pl.BlockSpec((POOL_ROWS, POOL_W), lambda i: (i, col)),
                  pl.BlockSpec((2, 4, TB, TB), lambda i: (0, 0, 0, 0)),
                  pl.BlockSpec((2, 4, TB, POOL_G), lambda i: (0, 0, 0, 0)),
                  pl.BlockSpec((4, POOL_G, POOL_G), lambda i: (0, 0, 0)),
                  pl.BlockSpec((1, POOL_W), lambda i: (0, 0))],
        out_specs=pl.BlockSpec((POOL_ROWS, POOL_W), lambda i: (i, 0)),
        scratch_shapes=[pltpu.VMEM((POOL_ROWS, POOL_G), F32)],
        compiler_params=_params(),
        name="pool",
    )(rest, a_pool, cnt_pool, w_pool_bf, pool_scale)


def _merge_kernel(xc_ref, xl_ref, of_ref, ob_ref, og_ref, yb_ref, ga_ref, gb_ref, mod_ref, hn_ref, wa_ref, wb_ref,
                  wo_ref, n2_ref, wrh_ref, wrl_ref, br_ref,
                  x1_ref, h2_ref, te_ref, tw_ref, hist_ref):
    row = _mod_row(pl.program_id(0))
    gate1 = mod_ref[pl.ds(row, 1), pl.ds(2 * D_MODEL, D_MODEL)]
    shift2 = mod_ref[pl.ds(row, 1), pl.ds(3 * D_MODEL, D_MODEL)]
    scale2 = mod_ref[pl.ds(row, 1), pl.ds(4 * D_MODEL, D_MODEL)]

    halves = [slice(j * (TBD // 2), (j + 1) * (TBD // 2)) for j in range(2)]
    is_ctx = pl.program_id(0) < NBD_CTX

    def head_out(r):
        o = of_ref[r, :] + ob_ref[r, :]
        og = og_ref[r, :].astype(F32)
        ya = jnp.concatenate(
            [_rms(o[:, h * HEAD_V:(h + 1) * HEAD_V], hn_ref[...]) for h in range(HEADS)], axis=1)
        return (ya * (og * jax.nn.sigmoid(og))).astype(BF16)

    ya = [head_out(r) for r in halves]
    pa = [_dot(ya[j], wa_ref[...]) for j in range(2)]
    pb = [_dot(yb_ref[r, :], wb_ref[...]) for r in halves]
    merged = [(jax.nn.sigmoid(ga_ref[r, :].astype(F32)) * pa[j]
               + jax.nn.sigmoid(gb_ref[r, :].astype(F32)) * pb[j]).astype(BF16) for j, r in enumerate(halves)]
    po = [_dot(merged[j], wo_ref[...]) for j in range(2)]
    hh, hl = [], []
    for j, r in enumerate(halves):
        x1 = jnp.where(is_ctx, xc_ref[r, :], xl_ref[r, :]) + gate1 * po[j]
        x1_ref[r, :] = x1
        hi, lo = _split2(_rms(x1, n2_ref[...]) * (1.0 + scale2) + shift2)
        h2_ref[r, :] = hi
        hh.append(hi)
        hl.append(lo)
    hh = jnp.concatenate(hh, axis=0)
    hl = jnp.concatenate(hl, axis=0)

    lt = _dot_nt(wrh_ref[...], hh) + _dot_nt(wrl_ref[...], hh) + _dot_nt(wrh_ref[...], hl) + br_ref[...]
    eidx = lax.broadcasted_iota(I32, (N_EXP, TBD), 0)
    vals, idxs, cnt = [], [], jnp.zeros((N_EXP, TBD), F32)
    for _ in range(TOP_K):
        m = jnp.max(lt, axis=0, keepdims=True)
        idx = jnp.min(jnp.where(lt == m, eidx, N_EXP), axis=0, keepdims=True)
        sel = eidx == idx
        vals.append(m)
        idxs.append(idx)
        cnt = cnt + sel.astype(F32)
        lt = jnp.where(sel, -jnp.inf, lt)
    ex = [jnp.exp(v - vals[0]) for v in vals]
    den = ex[0] + ex[1] + ex[2] + ex[3]
    tw_ref[0] = jnp.concatenate([e / den for e in ex], axis=0)
    te_ref[0] = jnp.concatenate(idxs, axis=0)
    hist_ref[0] = jnp.sum(cnt, axis=1, keepdims=True).astype(I32)


def _merge(x_ctx, x_lat, o_f, o_b, rest, yb, mod, hgrn_norm, wa_bf, wb_bf, wo_bf, norm2, wr_hi, wr_lo, b_router):
    full = lambda shape: pl.BlockSpec(shape, lambda i: (0,) * len(shape))
    return pl.pallas_call(
        _merge_kernel,
        out_shape=(jax.ShapeDtypeStruct((T_ALL, D_MODEL), F32),
                   jax.ShapeDtypeStruct((T_ALL, D_MODEL), BF16),
                   jax.ShapeDtypeStruct((NBD, TOP_K, TBD), I32),
                   jax.ShapeDtypeStruct((NBD, TOP_K, TBD), F32),
                   jax.ShapeDtypeStruct((NBD, N_EXP, 1), I32)),
        grid=(NBD,),
        in_specs=_x_specs() + [
                  pl.BlockSpec((TBD, HGRN_W), lambda i: (i, 0)),
                  pl.BlockSpec((TBD, HGRN_W), lambda i: (i, 0)),
                  pl.BlockSpec((TBD, HGRN_W), lambda i: (i, 2)),
                  pl.BlockSpec((TBD, POOL_W), lambda i: (i, 0)),
                  pl.BlockSpec((TBD, D_MODEL), lambda i: (i, 2)),
                  pl.BlockSpec((TBD, D_MODEL), lambda i: (i, 3)),
                  full((MOD_ROWS, 6 * D_MODEL)),
                  full((1, HEAD_V)),
                  full((HGRN_W, D_MODEL)),
                  full((POOL_W, D_MODEL)),
                  full((D_MODEL, D_MODEL)),
                  full((1, D_MODEL)),
                  full((N_EXP, D_MODEL)),
                  full((N_EXP, D_MODEL)),
                  full((N_EXP, 1))],
        out_specs=(pl.BlockSpec((TBD, D_MODEL), lambda i: (i, 0)),
                   pl.BlockSpec((TBD, D_MODEL), lambda i: (i, 0)),
                   pl.BlockSpec((1, TOP_K, TBD), lambda i: (i, 0, 0)),
                   pl.BlockSpec((1, TOP_K, TBD), lambda i: (i, 0, 0)),
                   pl.BlockSpec((1, N_EXP, 1), lambda i: (i, 0, 0))),
        compiler_params=_params(),
        name="merge",
    )(x_ctx, x_lat, o_f, o_b, rest, yb, rest, rest, mod, hgrn_norm, wa_bf, wb_bf, wo_bf, norm2,
      wr_hi, wr_lo, b_router)


def _local_rows(te_ref, loff_ref, tri_ref):
    te = te_ref[0]
    eidx = lax.broadcasted_iota(I32, (N_EXP, TBD), 0)
    sels = [eidx == te[k:k + 1] for k in range(TOP_K)]
    cnt = sels[0].astype(F32)
    for s in sels[1:]:
        cnt = cnt + s.astype(F32)
    base = _dot(cnt.astype(BF16), tri_ref[...]) + loff_ref[0]
    return [jnp.sum(jnp.where(s, base, 0.0), axis=0, keepdims=True) for s in sels]


def _chunk_relative(rows, r0):
    out = []
    for r in rows:
        inside = jnp.logical_and(r >= r0, r < r0 + SORT_CHUNK)
        out.append(jnp.where(inside, r - r0, -1.0).astype(BF16))
    return out


def _segment_copies(make_copy, local_off, global_off, units):
    big = units >> SEG_SMALL_BITS

    def piece(p, carry):
        off = pl.multiple_of(p * (8 << SEG_SMALL_BITS), 8 << SEG_SMALL_BITS)
        make_copy(pl.multiple_of(local_off + off, 8), pl.multiple_of(global_off + off, 8),
                  8 << SEG_SMALL_BITS).start()
        return carry

    lax.fori_loop(0, big, piece, 0)
    done = big << (SEG_SMALL_BITS + 3)
    for j in reversed(range(SEG_SMALL_BITS)):
        low = done + ((units >> (j + 1)) & ((1 << (SEG_SMALL_BITS - 1 - j)) - 1)) * (16 << j)

        @pl.when(((units >> j) & 1) == 1)
        def _():
            make_copy(pl.multiple_of(local_off + low, 8), pl.multiple_of(global_off + low, 8), 8 << j).start()


def _pack_pairs(x):
    half = D_MODEL // 2
    lo = lax.bitcast_convert_type(x[:, :half], U32) >> 16
    hi = lax.bitcast_convert_type(x[:, half:], U32) & jnp.uint32(0xFFFF0000)
    return hi | lo


def _unpack_pairs(p):
    lo = lax.bitcast_convert_type(p << 16, F32).astype(BF16)
    hi = lax.bitcast_convert_type(p & jnp.uint32(0xFFFF0000), F32).astype(BF16)
    return jnp.concatenate([lo, hi], axis=1)


def _block_rows(loff_s, seg_s, b):
    last = b * N_EXP + N_EXP - 1
    return pl.multiple_of(loff_s[last] + seg_s[last], 8)


def _dispatch_kernel(loff_s, seg_s, gbase_s, tail_s, h2_ref, te_ref, loffv_ref, tri_ref, iota_ref, xs_hbm,
                     loc, zeros, sem, sem_z):
    b = pl.program_id(0)
    slot = b % 2

    def wait_block(blk, s):
        n = _block_rows(loff_s, seg_s, blk)
        pltpu.make_async_copy(loc.at[s, pl.ds(0, n)], xs_hbm.at[pl.ds(0, n)], sem.at[s]).wait()

    lrow = _local_rows(te_ref, loffv_ref, tri_ref)

    @pl.when(b >= 2)
    def _():
        wait_block(b - 2, slot)

    for r0 in range(0, LROWS, SORT_CHUNK):
        rel = _chunk_relative(lrow, r0)
        p = jnp.zeros((SORT_CHUNK, TBD), BF16)
        for k in reversed(range(TOP_K)):
            p = jnp.where(iota_ref[...] == rel[k], jnp.ones_like(p), p)
        loc[slot, r0:r0 + SORT_CHUNK, :] = _pack_pairs(_dot(p, h2_ref[...]))

    def out_copy(a, g, size):
        return pltpu.make_async_copy(loc.at[slot, pl.ds(a, size)], xs_hbm.at[pl.ds(g, size)], sem.at[slot])

    def body(e, carry):
        idx = b * N_EXP + e
        _segment_copies(out_copy, loff_s[idx], gbase_s[idx], seg_s[idx] // 8)
        return carry

    lax.fori_loop(0, N_EXP, body, 0)

    @pl.when(b == NBD - 1)
    def _():
        zeros[...] = jnp.zeros_like(zeros)

        def zero_copy(a, g, size):
            return pltpu.make_async_copy(zeros.at[pl.ds(a, size)], xs_hbm.at[pl.ds(g, size)], sem_z)

        def zbody(e, ztot):
            _segment_copies(zero_copy, 0, tail_s[e], tail_s[N_EXP + e] // 8)
            return ztot + tail_s[N_EXP + e]

        def tbody(t, carry):
            pltpu.make_async_copy(zeros, xs_hbm.at[pl.ds(pl.multiple_of(t * TM, TM), TM)], sem_z).start()
            return carry

        n_used = tail_s[2 * N_EXP]
        lax.fori_loop(n_used, N_TILES, tbody, 0)
        ztot = lax.fori_loop(0, N_EXP, zbody, 0) + (N_TILES - n_used) * TM
        ztot = pl.multiple_of(ztot, 8)

        @pl.when(ztot > 0)
        def _():
            pltpu.make_async_copy(xs_hbm.at[pl.ds(0, ztot)], xs_hbm.at[pl.ds(0, ztot)], sem_z).wait()

        wait_block(b - 1, 1 - slot)
        wait_block(b, slot)


def _dispatch(tables, h2, top_e, tri, row_iota):
    loff_s, seg_s, gbase_s, tail_s, loff_v = tables
    grid_spec = pltpu.PrefetchScalarGridSpec(
        num_scalar_prefetch=4,
        grid=(NBD,),
        in_specs=[pl.BlockSpec((TBD, D_MODEL), lambda i, *_: (i, 0)),
                  pl.BlockSpec((1, TOP_K, TBD), lambda i, *_: (i, 0, 0)),
                  pl.BlockSpec((1, N_EXP, 1), lambda i, *_: (i, 0, 0)),
                  pl.BlockSpec((TBD, TBD), lambda i, *_: (0, 0)),
                  pl.BlockSpec((SORT_CHUNK, TBD), lambda i, *_: (0, 0))],
        out_specs=pl.BlockSpec(memory_space=pl.ANY),
        scratch_shapes=[pltpu.VMEM((2, LROWS, D_MODEL // 2), U32),
                        pltpu.VMEM((TM, D_MODEL // 2), U32),
                        pltpu.SemaphoreType.DMA((2,)),
                        pltpu.SemaphoreType.DMA])
    return pl.pallas_call(
        _dispatch_kernel,
        out_shape=jax.ShapeDtypeStruct((N_TILES * TM, D_MODEL // 2), U32),
        grid_spec=grid_spec,
        compiler_params=_params(),
        name="dispatch",
    )(loff_s, seg_s, gbase_s, tail_s, h2, top_e, loff_v, tri, row_iota)


def _moe_kernel(te_ref, first_ref, par_ref, next_ref, nv_ref, nu_ref, xs_ref, bgu_ref, bd_ref, wgu_hbm, wd_hbm, o_ref,
                wgu_st, wd_st, wgu_bf, wd_bf, sem):
    i = pl.program_id(0)

    def fetch(e, s):
        return (pltpu.make_async_copy(wgu_hbm.at[e], wgu_st.at[s], sem.at[0, s]),
                pltpu.make_async_copy(wd_hbm.at[e], wd_st.at[s], sem.at[1, s]))

    @pl.when(i < nu_ref[0])
    def _():
        @pl.when(first_ref[i] == 1)
        def _():
            s = par_ref[i]

            @pl.when(i == 0)
            def _():
                for cp in fetch(te_ref[0], 0):
                    cp.start()

            for cp in fetch(te_ref[i], s):
                cp.wait()

            @pl.when(next_ref[i] >= 0)
            def _():
                for cp in fetch(next_ref[i], 1 - s):
                    cp.start()

            wgu_bf[...] = wgu_st[s].astype(BF16)
            wd_bf[...] = wd_st[s].astype(BF16)

        half = TM // 2
        halves = [slice(0, half), slice(half, TM)]

        def gate_up(r):
            return _dot(_unpack_pairs(xs_ref[r, :]), wgu_bf[...]) + bgu_ref[0]

        def activation(gu):
            gate = jnp.minimum(gu[:, :D_FF], SWIGLU_LIMIT)
            up = jnp.clip(gu[:, D_FF:], -SWIGLU_LIMIT, SWIGLU_LIMIT)
            return ((up + 1.0) * gate * jax.nn.sigmoid(SWIGLU_ALPHA * gate)).astype(BF16)

        def down(r, act):
            out = _dot(act, wd_bf[...]) + bd_ref[0]
            o_ref[r, :] = _pack_pairs(out.astype(BF16).astype(F32))

        @pl.when(nv_ref[i] > half)
        def _():
            gu = [gate_up(r) for r in halves]
            for r, g in zip(halves, gu):
                down(r, activation(g))

        @pl.when(nv_ref[i] <= half)
        def _():
            down(halves[0], activation(gate_up(halves[0])))
            o_ref[halves[1], :] = jnp.zeros((half, D_MODEL // 2), U32)

    @pl.when(i >= nu_ref[0])
    def _():
        o_ref[...] = jnp.zeros_like(o_ref)


def _moe(tile_tables, xs, w_gate_up, b_gate_up, w_down, b_down):
    nsp = len(tile_tables)
    row_tile = lambda i, *s: (jnp.minimum(i, s[nsp - 1][0] - 1), 0)
    grid_spec = pltpu.PrefetchScalarGridSpec(
        num_scalar_prefetch=nsp,
        grid=(N_TILES,),
        in_specs=[pl.BlockSpec((TM, D_MODEL // 2), row_tile),
                  pl.BlockSpec((1, 1, 2 * D_FF), lambda i, te, *_: (te[i], 0, 0)),
                  pl.BlockSpec((1, 1, D_MODEL), lambda i, te, *_: (te[i], 0, 0)),
                  pl.BlockSpec(memory_space=pl.ANY),
                  pl.BlockSpec(memory_space=pl.ANY)],
        out_specs=pl.BlockSpec((TM, D_MODEL // 2), lambda i, *_: (i, 0)),
        scratch_shapes=[pltpu.VMEM((2, D_MODEL, 2 * D_FF), F32),
                        pltpu.VMEM((2, D_FF, D_MODEL), F32),
                        pltpu.VMEM((D_MODEL, 2 * D_FF), BF16),
                        pltpu.VMEM((D_FF, D_MODEL), BF16),
                        pltpu.SemaphoreType.DMA((2, 2))])
    return pl.pallas_call(
        _moe_kernel,
        out_shape=jax.ShapeDtypeStruct((N_TILES * TM, D_MODEL // 2), U32),
        grid_spec=grid_spec,
        compiler_params=_params(),
        name="moe",
    )(*tile_tables, xs, b_gate_up.reshape(N_EXP, 1, 2 * D_FF), b_down.reshape(N_EXP, 1, D_MODEL),
      w_gate_up, w_down)


def _final_kernel(loff_s, seg_s, gbase_s, ys_hbm, x1_ref, te_ref, tw_ref, loffv_ref, tri_ref, iota_ref, mod_ref, fn_ref,
                  oc_ref, ol_ref, loc, sem):
    b = pl.program_id(0)
    slot = b % 2

    def start_block(blk, s):
        def in_copy(a, g, size):
            return pltpu.make_async_copy(ys_hbm.at[pl.ds(g, size)], loc.at[s, pl.ds(a, size)], sem.at[s])

        def body(e, carry):
            idx = blk * N_EXP + e
            _segment_copies(in_copy, loff_s[idx], gbase_s[idx], seg_s[idx] // 8)
            return carry

        lax.fori_loop(0, N_EXP, body, 0)

    @pl.when(b == 0)
    def _():
        loc[...] = jnp.zeros_like(loc)
        start_block(0, 0)

    @pl.when(b + 1 < NBD)
    def _():
        start_block(b + 1, 1 - slot)

    lrow = _local_rows(te_ref, loffv_ref, tri_ref)
    tw = tw_ref[0]
    rows8 = jnp.concatenate(lrow + [tw], axis=0)
    cols = jnp.concatenate([rows8, jnp.zeros((128 - 2 * TOP_K, TBD), F32)], axis=0).T

    n = _block_rows(loff_s, seg_s, b)
    pltpu.make_async_copy(ys_hbm.at[pl.ds(0, n)], loc.at[slot, pl.ds(0, n)], sem.at[slot]).wait()

    y = None
    wts = [cols[:, TOP_K + k:TOP_K + k + 1].astype(BF16) for k in range(TOP_K)]
    for r0 in range(0, LROWS, SORT_CHUNK):
        rel = _chunk_relative([cols[:, k:k + 1] for k in range(TOP_K)], r0)
        pw = jnp.zeros((TBD, SORT_CHUNK), BF16)
        for k in reversed(range(TOP_K)):
            pw = jnp.where(iota_ref[...] == rel[k], wts[k], pw)
        part = _dot(pw, _unpack_pairs(loc[slot, r0:r0 + SORT_CHUNK, :]))
        y = part if y is None else y + part

    row = jnp.where(b < NBD_CTX, N_LAT_SEQ, (b - NBD_CTX) // (LAT_LEN // TBD))
    gate2 = mod_ref[pl.ds(row, 1), pl.ds(5 * D_MODEL, D_MODEL)]
    out = _rms(x1_ref[...] + gate2 * y, fn_ref[...])

    @pl.when(b < NBD_CTX)
    def _():
        oc_ref[...] = out

    @pl.when(b >= NBD_CTX)
    def _():
        ol_ref[...] = out


def _final(tables, ys, x1, top_e, top_w, tri, col_iota, mod, final_norm):
    loff_s, seg_s, gbase_s, _, loff_v = tables
    grid_spec = pltpu.PrefetchScalarGridSpec(
        num_scalar_prefetch=3,
        grid=(NBD,),
        in_specs=[pl.BlockSpec(memory_space=pl.ANY),
                  pl.BlockSpec((TBD, D_MODEL), lambda i, *_: (i, 0)),
                  pl.BlockSpec((1, TOP_K, TBD), lambda i, *_: (i, 0, 0)),
                  pl.BlockSpec((1, TOP_K, TBD), lambda i, *_: (i, 0, 0)),
                  pl.BlockSpec((1, N_EXP, 1), lambda i, *_: (i, 0, 0)),
                  pl.BlockSpec((TBD, TBD), lambda i, *_: (0, 0)),
                  pl.BlockSpec((TBD, SORT_CHUNK), lambda i, *_: (0, 0)),
                  pl.BlockSpec((MOD_ROWS, 6 * D_MODEL), lambda i, *_: (0, 0)),
                  pl.BlockSpec((1, D_MODEL), lambda i, *_: (0, 0))],
        out_specs=(pl.BlockSpec((TBD, D_MODEL), lambda i, *_: (jnp.minimum(i, NBD_CTX - 1), 0)),
                   pl.BlockSpec((TBD, D_MODEL), lambda i, *_: (jnp.maximum(i - NBD_CTX, 0), 0))),
        scratch_shapes=[pltpu.VMEM((2, LROWS, D_MODEL // 2), U32),
                        pltpu.SemaphoreType.DMA((2,))])
    return pl.pallas_call(
        _final_kernel,
        out_shape=(jax.ShapeDtypeStruct((T_CTX, D_MODEL), F32),
                   jax.ShapeDtypeStruct((T_LAT, D_MODEL), F32)),
        grid_spec=grid_spec,
        compiler_params=_params(),
        name="final",
    )(loff_s, seg_s, gbase_s, ys, x1, top_e, top_w, loff_v, tri, col_iota, mod, final_norm)


def _dispatch_tables(hist):
    hist = hist.reshape(NBD, N_EXP)
    seg = ((hist + 7) // 8) * 8
    loff = jnp.cumsum(seg, axis=1) - seg
    rows_e = jnp.sum(seg, axis=0)
    region = ((rows_e + TM - 1) // TM) * TM
    region_end = jnp.cumsum(region)
    region_start = region_end - region
    gbase = region_start[None, :] + jnp.cumsum(seg, axis=0) - seg
    n_used = (region_end[-1] // TM).astype(I32)
    tail = jnp.concatenate([region_start + rows_e, region - rows_e, n_used.reshape(1)])
    start = jnp.arange(N_TILES, dtype=I32) * TM
    tile_e = jnp.sum((start[:, None] >= region_end[None, :]).astype(I32), axis=1)
    tile_e = jnp.minimum(tile_e, tile_e[jnp.maximum(n_used - 1, 0)])
    first = jnp.concatenate([jnp.ones((1,), I32), (tile_e[1:] != tile_e[:-1]).astype(I32)])
    parity = (jnp.cumsum(first) - 1) % 2
    later = jnp.where(tile_e[None, :] > tile_e[:, None], tile_e[None, :], N_EXP)
    nxt = jnp.min(later, axis=1)
    nxt = jnp.where(nxt == N_EXP, -1, nxt)
    flat = lambda a: a.reshape(-1).astype(I32)
    tables = (flat(loff), flat(seg), flat(gbase), flat(tail), loff.astype(F32).reshape(NBD, N_EXP, 1))
    mine = tile_e[:, None] == jnp.arange(N_EXP, dtype=I32)[None, :]
    data_end = jnp.sum(jnp.where(mine, (region_start + rows_e)[None, :], 0), axis=1)
    tile_nv = jnp.clip(data_end - start, 0, TM)
    tile_tables = (flat(tile_e), flat(first), flat(parity), flat(nxt), flat(tile_nv), n_used.reshape(1))
    return tables, tile_tables


def kernel(x_prompt, x_sample, state_hgrn, c, c_ctx, w_ada, b_ada, norm1, w_in, hgrn_lb, hgrn_norm, w_pool,
           pool_scale, w_branch_a, w_branch_b, w_out, norm2, w_router, b_router, w_gate_up, b_gate_up,
           w_down, b_down, final_norm):
    x_ctx = x_prompt.reshape(T_CTX, D_MODEL)
    x_lat = x_sample.reshape(T_LAT, D_MODEL)
    cc = jnp.zeros((MOD_ROWS, D_MODEL), F32).at[:N_LAT_SEQ].set(c).at[N_LAT_SEQ].set(c_ctx)
    mod = _ada(cc, w_ada[0], b_ada)

    w_in_bf = w_in[0].astype(BF16)
    w_gates = w_in_bf[:, HGRN_W:HGRN_W + GATE_W]
    w_rest = jnp.concatenate([w_in_bf[:, :HGRN_W], w_in_bf[:, HGRN_W + GATE_W:]], axis=1)
    gates, rest = _inproj(x_ctx, x_lat, mod, norm1, w_gates, w_rest)

    mall, masks = _hgrn_consts()
    o_f, o_b, new_state = _hgrn(gates, rest, hgrn_lb, state_hgrn[:, 0], mall, masks)

    a_pool, cnt_pool = _pool_consts()
    yb = _pool(rest, a_pool, cnt_pool, w_pool[0].astype(BF16), pool_scale)

    wr_t = w_router[0].T
    wr_hi = wr_t.astype(BF16)
    wr_lo = (wr_t - wr_hi.astype(F32)).astype(BF16)
    tri = jnp.asarray(np.triu(np.ones((TBD, TBD), np.float32), 1), BF16)
    x1, h2, top_e, top_w, hist = _merge(
        x_ctx, x_lat, o_f, o_b, rest, yb, mod, hgrn_norm, w_branch_a[0].astype(BF16), w_branch_b[0].astype(BF16),
        w_out[0].astype(BF16), norm2, wr_hi, wr_lo, b_router.reshape(N_EXP, 1))

    tables, tile_tables = _dispatch_tables(hist)
    row_iota = jnp.asarray(np.broadcast_to(np.arange(SORT_CHUNK, dtype=np.float32)[:, None], (SORT_CHUNK, TBD)), BF16)
    xs = _dispatch(tables, h2, top_e, tri, row_iota)
    ys = _moe(tile_tables, xs, w_gate_up[0], b_gate_up[0], w_down[0], b_down[0])
    y_ctx, y_lat = _final(tables, ys, x1, top_e, top_w, tri, row_iota.T, mod, final_norm.reshape(1, D_MODEL))
    y_prompt = y_ctx.reshape(N_CTX_SEQ, CTX_LEN, D_MODEL)
    y_sample = y_lat.reshape(N_LAT_SEQ, LAT_LEN, D_MODEL)
    return y_prompt, y_sample, new_state[:, None]
```

```python
import functools

import numpy as np
import jax
import jax.numpy as jnp
from jax import lax
from jax.experimental import pallas as pl
from jax.experimental.pallas import tpu as pltpu

F32 = jnp.float32
BF16 = jnp.bfloat16
I32 = jnp.int32
U32 = jnp.uint32

D_MODEL = 1024
N_CTX_SEQ, CTX_LEN = 32, 256
N_LAT_SEQ, LAT_LEN = 4, 2048
T_CTX = N_CTX_SEQ * CTX_LEN
T_LAT = N_LAT_SEQ * LAT_LEN
T_ALL = T_CTX + T_LAT
TB = 256
NB = T_ALL // TB
NB_CTX = T_CTX // TB
LAT_BLOCKS = LAT_LEN // TB
HEADS, HEAD_K, HEAD_V = 4, 128, 128
HGRN_W = HEADS * HEAD_V
POOL_WINDOWS = (2, 4, 8, 16)
POOL_G = 128
POOL_W = len(POOL_WINDOWS) * POOL_G
GRID_W = 64
GRID_H = LAT_LEN // GRID_W
IN_W = 5 * HGRN_W + POOL_W + 2 * D_MODEL
GATE_W = 2 * HGRN_W
REST_W = IN_W - GATE_W
N_EXP, TOP_K, D_FF = 32, 4, 1024
SWIGLU_LIMIT = 7.0
SWIGLU_ALPHA = 1.702
EPS = 1e-6
CHUNK = 64
N_LEVELS = 6
EXP_ROWS = (N_LEVELS + 2) * CHUNK
MM_BLOCKS = (0, 4, 5, 6)
MM_ROWS = len(MM_BLOCKS) * CHUNK
COARSE_LEVELS = ((1, 32), (2, 16), (3, 8))
TM = 1024
MOE_SUB = 256
TBD = 512
NBD = T_ALL // TBD
NBD_CTX = T_CTX // TBD
SORT_CHUNK = 256
LROWS = TBD * TOP_K + N_EXP * 8
SEG_SMALL_BITS = 3
N_TILES = -(-(T_ALL * TOP_K + NBD * N_EXP * 7 + N_EXP * (TM - 1)) // TM)
MOD_ROWS = 8
VMEM_LIMIT = 56 * 1024 * 1024


def _params(sem=("arbitrary",)):
    return pltpu.CompilerParams(dimension_semantics=sem, vmem_limit_bytes=VMEM_LIMIT)


def _dot(a, b):
    return jnp.dot(a, b, preferred_element_type=F32)


def _dot_nt(a, b):
    return lax.dot_general(a, b, (((1,), (1,)), ((), ())), preferred_element_type=F32)


def _dot_tn(a, b):
    return lax.dot_general(a, b, (((0,), (0,)), ((), ())), preferred_element_type=F32)


def _split2(x):
    hi = x.astype(BF16)
    lo = (x - hi.astype(F32)).astype(BF16)
    return hi, lo


def _mod_row(i):
    return jnp.where(i < NBD_CTX, N_LAT_SEQ, (i - NBD_CTX) // (LAT_LEN // TBD))


def _ada_kernel(c_ref, w_ref, b_ref, o_ref):
    c = c_ref[...]
    s = c * jax.nn.sigmoid(c)
    o_ref[...] = jnp.dot(s, w_ref[...], preferred_element_type=F32,
                         precision=lax.Precision.HIGHEST) + b_ref[...]


def _ada(cc, w_ada, b_ada):
    nblk = 1536
    return pl.pallas_call(
        _ada_kernel,
        out_shape=jax.ShapeDtypeStruct((MOD_ROWS, 6 * D_MODEL), F32),
        grid=(6 * D_MODEL // nblk,),
        in_specs=[pl.BlockSpec((MOD_ROWS, D_MODEL), lambda j: (0, 0)),
                  pl.BlockSpec((D_MODEL, nblk), lambda j: (0, j)),
                  pl.BlockSpec((1, nblk), lambda j: (0, j))],
        out_specs=pl.BlockSpec((MOD_ROWS, nblk), lambda j: (0, j)),
        compiler_params=_params(),
        name="ada",
    )(cc, w_ada, b_ada)


def _rms(x, g):
    ms = jnp.mean(x * x, axis=-1, keepdims=True)
    return x * lax.rsqrt(ms + EPS) * g


def _x_specs():
    return [pl.BlockSpec((TBD, D_MODEL), lambda i, *_: (jnp.minimum(i, NBD_CTX - 1), 0)),
            pl.BlockSpec((TBD, D_MODEL), lambda i, *_: (jnp.maximum(i - NBD_CTX, 0), 0))]


def _x_block(xc_ref, xl_ref):
    return jnp.where(pl.program_id(0) < NBD_CTX, xc_ref[...], xl_ref[...])


def _inproj_kernel(xc_ref, xl_ref, mod_ref, n1_ref, wg_ref, wr_ref, og_ref, or_ref):
    row = _mod_row(pl.program_id(0))
    shift = mod_ref[pl.ds(row, 1), pl.ds(0, D_MODEL)]
    scale = mod_ref[pl.ds(row, 1), pl.ds(D_MODEL, D_MODEL)]
    h = (_rms(_x_block(xc_ref, xl_ref), n1_ref[...]) * (1.0 + scale) + shift).astype(BF16)
    og_ref[...] = _dot(h, wg_ref[...])
    or_ref[...] = _dot(h, wr_ref[...]).astype(BF16)


def _inproj(x_ctx, x_lat, mod, norm1, w_gates_bf, w_rest_bf):
    return pl.pallas_call(
        _inproj_kernel,
        out_shape=(jax.ShapeDtypeStruct((T_ALL, GATE_W), F32),
                   jax.ShapeDtypeStruct((T_ALL, REST_W), BF16)),
        grid=(NBD,),
        in_specs=_x_specs() + [
                  pl.BlockSpec((MOD_ROWS, 6 * D_MODEL), lambda i: (0, 0)),
                  pl.BlockSpec((1, D_MODEL), lambda i: (0, 0)),
                  pl.BlockSpec((D_MODEL, GATE_W), lambda i: (0, 0)),
                  pl.BlockSpec((D_MODEL, REST_W), lambda i: (0, 0))],
        out_specs=(pl.BlockSpec((TBD, GATE_W), lambda i: (i, 0)),
                   pl.BlockSpec((TBD, REST_W), lambda i: (i, 0))),
        compiler_params=_params(),
        name="inproj",
    )(x_ctx, x_lat, mod, norm1, w_gates_bf, w_rest_bf)


def _hgrn_consts():
    c = CHUNK
    t = np.arange(c)[:, None]
    u = np.arange(c)[None, :]
    blocks = [u <= t]
    masks = [np.eye(c, dtype=bool)]
    h = c // 2
    while h >= 1:
        bi = t // h
        upper = (bi % 2) == 1
        e_up = (u >= bi * h) & (u <= t)
        e_lo = (u > t) & (u <= bi * h + h - 1)
        blocks.append(np.where(upper, e_up, e_lo))
        masks.append(((t // (2 * h)) == (u // (2 * h))) & (((t // h) % 2) == 1) & (((u // h) % 2) == 0))
        h //= 2
    blocks.append(u > t)
    m_f = np.stack(blocks).astype(np.float32)
    k_f = np.stack(masks).astype(np.float32)
    m_b = m_f[:, ::-1, ::-1]
    k_b = k_f[:, ::-1, ::-1]
    sel = list(MM_BLOCKS)
    m = np.stack([m_f[sel].reshape(MM_ROWS, c), m_b[sel].reshape(MM_ROWS, c)])
    m3 = np.concatenate([m, m, m], axis=2)
    return jnp.asarray(m3, BF16), jnp.asarray(np.stack([k_f, k_b]), F32)


def _hgrn_block(dirs, lb, mall_ref, mask_ref, st_ref, z_ref, k_ref, sc_ref, run_ref):
    c = CHUNK
    nchunk = TB // c
    units = [(d, h) for d in range(2) for h in range(HEADS)]
    sl = [slice(h * HEAD_K, (h + 1) * HEAD_K) for h in range(HEADS)]

    def rows(ci, d):
        r0 = ci * c if d == 0 else (nchunk - 1 - ci) * c
        return slice(r0, r0 + c)

    def exponents(ci):
        s = ci % 2
        for d in range(2):
            f = lb[d:d + 1] + (1.0 - lb[d:d + 1]) * jax.nn.sigmoid(dirs[d][1][rows(ci, d), :])
            k_ref[s, d] = 1.0 - f
            k_ref[s, 2 + d] = dirs[d][0][rows(ci, d), :].astype(F32)
            g = jnp.log2(f)
            g1 = g.astype(BF16)
            r1 = g - g1.astype(F32)
            g2 = r1.astype(BF16)
            g3 = (r1 - g2.astype(F32)).astype(BF16)
            gsplit = jnp.concatenate([g1, g2, g3], axis=0)
            ex = _dot(mall_ref[d], gsplit)
            run = ex[0:c]
            run_ref[d] = run
            z_ref[s, d, 0:c] = jnp.exp2(run)
            for j, blk in enumerate(MM_BLOCKS[1:]):
                z_ref[s, d, blk * c:(blk + 1) * c] = jnp.exp2(ex[(j + 1) * c:(j + 2) * c])
            for blk, h in COARSE_LEVELS:
                for base in range(0, c, 2 * h):
                    ref = run_ref[d, base + h - 1 + d:base + h + d, :]
                    if d == 0:
                        first, second = ref - run[base:base + h], run[base + h:base + 2 * h] - ref
                    else:
                        first, second = run[base:base + h] - ref, ref - run[base + h:base + 2 * h]
                    z_ref[s, d, blk * c + base:blk * c + base + h] = jnp.exp2(first)
                    z_ref[s, d, blk * c + base + h:blk * c + base + 2 * h] = jnp.exp2(second)
            end = run_ref[d, c - 1:c, :] if d == 0 else run_ref[d, 0:1, :]
            z_ref[s, d, (N_LEVELS + 1) * c:] = jnp.exp2(end - run)

    def q_of(ci, d, h):
        return k_ref[ci % 2, 2 + d, :, sl[h]]

    def v_of(ci, d, h):
        return dirs[d][2][rows(ci, d), sl[h]].astype(BF16)

    def qz(ci, d, h, blk):
        return (q_of(ci, d, h) * z_ref[ci % 2, d, blk * c:(blk + 1) * c, sl[h]]).astype(BF16)

    def kz(ci, d, h, blk):
        return (k_ref[ci % 2, d, :, sl[h]] * z_ref[ci % 2, d, blk * c:(blk + 1) * c, sl[h]]).astype(BF16)

    def levels(ci):
        for d, h in units:
            q = q_of(ci, d, h).astype(F32)
            k = k_ref[ci % 2, d, :, sl[h]]
            k_next = pltpu.roll(k, 1 if d == 0 else c - 1, 0)
            zq = q * z_ref[ci % 2, d, N_LEVELS * c:(N_LEVELS + 1) * c, sl[h]]
            diag = jnp.sum(q * k, axis=1, keepdims=True)
            near = jnp.sum(zq * k_next, axis=1, keepdims=True)
            sc_ref[d, h] = mask_ref[d, 0] * diag + mask_ref[d, N_LEVELS] * near
        for lev in range(N_LEVELS - 1):
            for d, h in units:
                sc_ref[d, h] += mask_ref[d, lev + 1] * _dot_nt(qz(ci, d, h, lev + 1), kz(ci, d, h, lev + 1))

    def tail(ci):
        for d, h in units:
            o = (_dot_nt(qz(ci, d, h, 0), st_ref[d, h].astype(BF16))
                 + _dot(sc_ref[d, h].astype(BF16), v_of(ci, d, h)))
            dirs[d][3][rows(ci, d), sl[h]] = o * (HEAD_K ** -0.5)
        for d, h in units:
            tot_row = c - 1 if d == 0 else 0
            decay = z_ref[ci % 2, d, tot_row:tot_row + 1, sl[h]]
            st_ref[d, h] = st_ref[d, h] * decay + _dot_tn(v_of(ci, d, h), kz(ci, d, h, N_LEVELS + 1))

    exponents(0)
    for ci in range(nchunk):
        levels(ci)
        if ci + 1 < nchunk:
            exponents(ci + 1)
        tail(ci)


def _hgrn_kernel(qf_ref, ff_ref, vf_ref, qb_ref, fb_ref, vb_ref, lbraw_ref, s0_ref, mall_ref, mask_ref,
                 of_ref, ob_ref, sout_hbm, st_ref, stage_ref, z_ref, k_ref, sc_ref, run_ref, sem):
    i = pl.program_id(0)
    j = (i - NB_CTX) % LAT_BLOCKS
    is_ctx = i < NB_CTX

    @pl.when(is_ctx)
    def _():
        st_ref[...] = jnp.zeros_like(st_ref)

    @pl.when(jnp.logical_and(jnp.logical_not(is_ctx), j == 0))
    def _():
        for d in range(2):
            for h in range(HEADS):
                st_ref[d, h] = s0_ref[0, d, h].T

    a0 = lbraw_ref[0]
    a1 = lbraw_ref[1]
    mx = jnp.maximum(a0, a1)
    e0 = jnp.exp(a0 - mx)
    e1 = jnp.exp(a1 - mx)
    lb = e0 / (e0 + e1)

    dirs = ((qf_ref, ff_ref, vf_ref, of_ref), (qb_ref, fb_ref, vb_ref, ob_ref))
    _hgrn_block(dirs, lb, mall_ref, mask_ref, st_ref, z_ref, k_ref, sc_ref, run_ref)

    @pl.when(is_ctx)
    def _():
        for d in range(2):
            for h in range(HEADS):
                stage_ref[d, h] = st_ref[d, h].T
        cp = pltpu.make_async_copy(stage_ref, sout_hbm.at[i], sem)
        cp.start()
        cp.wait()


def _bwd_block(i):
    j = (i - NB_CTX) % LAT_BLOCKS
    return jnp.where(i < NB_CTX, i, i - j + (LAT_BLOCKS - 1 - j))


def _hgrn(gates, rest, hgrn_lb, s0, mall, masks):
    nh = HGRN_W
    fwd = lambda col: pl.BlockSpec((TB, nh), lambda i: (i, col))
    bwd = lambda col: pl.BlockSpec((TB, nh), lambda i: (_bwd_block(i), col))
    lat_seq = lambda i: jnp.clip((i - NB_CTX) // LAT_BLOCKS, 0, N_LAT_SEQ - 1)
    return pl.pallas_call(
        _hgrn_kernel,
        out_shape=(jax.ShapeDtypeStruct((T_ALL, nh), F32),
                   jax.ShapeDtypeStruct((T_ALL, nh), F32),
                   jax.ShapeDtypeStruct((N_CTX_SEQ, 2, HEADS, HEAD_K, HEAD_V), F32)),
        grid=(NB,),
        in_specs=[fwd(0), fwd(0), fwd(1), bwd(0), bwd(1), bwd(1),
                  pl.BlockSpec((2, 2, nh), lambda i: (0, 0, 0)),
                  pl.BlockSpec((1, 2, HEADS, HEAD_K, HEAD_V), lambda i: (lat_seq(i), 0, 0, 0, 0)),
                  pl.BlockSpec((2, MM_ROWS, 3 * CHUNK), lambda i: (0, 0, 0)),
                  pl.BlockSpec((2, N_LEVELS + 1, CHUNK, CHUNK), lambda i: (0, 0, 0, 0))],
        out_specs=(pl.BlockSpec((TB, nh), lambda i: (i, 0)),
                   pl.BlockSpec((TB, nh), lambda i: (_bwd_block(i), 0)),
                   pl.BlockSpec(memory_space=pl.ANY)),
        scratch_shapes=[pltpu.VMEM((2, HEADS, HEAD_V, HEAD_K), F32),
                        pltpu.VMEM((2, HEADS, HEAD_K, HEAD_V), F32),
                        pltpu.VMEM((2, 2, EXP_ROWS, HGRN_W), F32),
                        pltpu.VMEM((2, 4, CHUNK, HGRN_W), F32),
                        pltpu.VMEM((2, HEADS, CHUNK, CHUNK), F32),
                        pltpu.VMEM((2, CHUNK, HGRN_W), F32),
                        pltpu.SemaphoreType.DMA],
        compiler_params=_params(),
        name="hgrn",
    )(rest, gates, rest, rest, gates, rest, hgrn_lb, s0, mall, masks)


def _window_bounds(n, w):
    pos = np.arange(n)
    lo = np.clip(pos - w // 2, 0, n - 1)
    hi = np.clip(pos - w // 2 + w - 1, 0, n - 1)
    return lo, hi


def _pool_consts():
    seq, img, cnt_seq, cnt_col = [], [], [], []
    for w in POOL_WINDOWS:
        lo, hi = _window_bounds(CTX_LEN, w)
        u = np.arange(CTX_LEN)[None, :]
        seq.append((u >= lo[:, None]) & (u <= hi[:, None]))
        cnt_seq.append(hi - lo + 1)
        lo, hi = _window_bounds(GRID_W, w)
        u = np.arange(GRID_W)[None, :]
        band = (u >= lo[:, None]) & (u <= hi[:, None])
        img.append(np.kron(np.eye(TB // GRID_W, dtype=bool), band))
        cnt_col.append(np.tile(hi - lo + 1, TB // GRID_W))
    a = np.stack([np.stack(seq), np.stack(img)]).astype(np.float32)
    cnt = np.stack([np.stack(cnt_seq), np.stack(cnt_col)]).astype(np.float32)
    cnt = np.broadcast_to(cnt[..., None], cnt.shape + (POOL_G,))
    return jnp.asarray(a, BF16), jnp.asarray(cnt, F32)


POOL_ROWS = LAT_LEN


def _pool_kernel(u_ref, a_ref, cnt_ref, wp_ref, ps_ref, o_ref, cp_ref):
    i = pl.program_id(0)
    nblk = POOL_ROWS // TB

    def finish(g, r0, nrows, pm):
        sl = slice(g * POOL_G, (g + 1) * POOL_G)
        d = pm - u_ref[pl.ds(r0, nrows), sl].astype(F32)
        y = _dot(d.astype(BF16), wp_ref[g]) * ps_ref[:, sl]
        o_ref[pl.ds(r0, nrows), sl] = y.astype(o_ref.dtype)

    def window_sum(kind, g, b):
        sl = slice(g * POOL_G, (g + 1) * POOL_G)
        return _dot(a_ref[kind, g], u_ref[pl.ds(b * TB, TB), sl]) / cnt_ref[kind, g]

    @pl.when(i < T_CTX // POOL_ROWS)
    def _():
        for g in range(len(POOL_WINDOWS)):
            for b in range(nblk):
                finish(g, b * TB, TB, window_sum(0, g, b))

    @pl.when(i >= T_CTX // POOL_ROWS)
    def _():
        for g, w in enumerate(POOL_WINDOWS):
            for b in range(nblk):
                cp_ref[pl.ds(b * TB, TB), :] = window_sum(1, g, b)
            lo, hi = _window_bounds(GRID_H, w)
            for r in range(GRID_H):
                acc = cp_ref[pl.ds(int(lo[r]) * GRID_W, GRID_W), :]
                for rr in range(int(lo[r]) + 1, int(hi[r]) + 1):
                    acc = acc + cp_ref[pl.ds(rr * GRID_W, GRID_W), :]
                finish(g, r * GRID_W, GRID_W, acc / float(hi[r] - lo[r] + 1))


def _pool(rest, a_pool, cnt_pool, w_pool_bf, pool_scale):
    col = 3
    return pl.pallas_call(
        _pool_kernel,
        out_shape=jax.ShapeDtypeStruct((T_ALL, POOL_W), BF16),
        grid=(T_ALL // POOL_ROWS,),
        in_specs=[pl.BlockSpec((POOL_ROWS, POOL_W), lambda i: (i, col)),
                  pl.BlockSpec((2, 4, TB, TB), lambda i: (0, 0, 0, 0)),
                  pl.BlockSpec((2, 4, TB, POOL_G), lambda i: (0, 0, 0, 0)),
                  pl.BlockSpec((4, POOL_G, POOL_G), lambda i: (0, 0, 0)),
                  pl.BlockSpec((1, POOL_W), lambda i: (0, 0))],
        out_specs=pl.BlockSpec((POOL_ROWS, POOL_W), lambda i: (i, 0)),
        scratch_shapes=[pltpu.VMEM((POOL_ROWS, POOL_G), F32)],
        compiler_params=_params(),
        name="pool",
    )(rest, a_pool, cnt_pool, w_pool_bf, pool_scale)


def _merge_kernel(xc_ref, xl_ref, of_ref, ob_ref, og_ref, yb_ref, ga_ref, gb_ref, mod_ref, hn_ref, wa_ref, wb_ref,
                  wo_ref, n2_ref, wrh_ref, wrl_ref, br_ref,
                  x1_ref, h2_ref, te_ref, tw_ref, hist_ref):
    row = _mod_row(pl.program_id(0))
    gate1 = mod_ref[pl.ds(row, 1), pl.ds(2 * D_MODEL, D_MODEL)]
    shift2 = mod_ref[pl.ds(row, 1), pl.ds(3 * D_MODEL, D_MODEL)]
    scale2 = mod_ref[pl.ds(row, 1), pl.ds(4 * D_MODEL, D_MODEL)]

    halves = [slice(j * (TBD // 2), (j + 1) * (TBD // 2)) for j in range(2)]
    is_ctx = pl.program_id(0) < NBD_CTX

    def head_out(r):
        o = of_ref[r, :] + ob_ref[r, :]
        og = og_ref[r, :].astype(F32)
        ya = jnp.concatenate(
            [_rms(o[:, h * HEAD_V:(h + 1) * HEAD_V], hn_ref[...]) for h in range(HEADS)], axis=1)
        return (ya * (og * jax.nn.sigmoid(og))).astype(BF16)

    ya = [head_out(r) for r in halves]
    pa = [_dot(ya[j], wa_ref[...]) for j in range(2)]
    pb = [_dot(yb_ref[r, :], wb_ref[...]) for r in halves]
    merged = [(jax.nn.sigmoid(ga_ref[r, :].astype(F32)) * pa[j]
               + jax.nn.sigmoid(gb_ref[r, :].astype(F32)) * pb[j]).astype(BF16) for j, r in enumerate(halves)]
    po = [_dot(merged[j], wo_ref[...]) for j in range(2)]
    hh, hl = [], []
    for j, r in enumerate(halves):
        x1 = jnp.where(is_ctx, xc_ref[r, :], xl_ref[r, :]) + gate1 * po[j]
        x1_ref[r, :] = x1
        hi, lo = _split2(_rms(x1, n2_ref[...]) * (1.0 + scale2) + shift2)
        h2_ref[r, :] = hi
        hh.append(hi)
        hl.append(lo)
    hh = jnp.concatenate(hh, axis=0)
    hl = jnp.concatenate(hl, axis=0)

    lt = _dot_nt(wrh_ref[...], hh) + _dot_nt(wrl_ref[...], hh) + _dot_nt(wrh_ref[...], hl) + br_ref[...]
    eidx = lax.broadcasted_iota(I32, (N_EXP, TBD), 0)
    vals, idxs, cnt = [], [], jnp.zeros((N_EXP, TBD), F32)
    for _ in range(TOP_K):
        m = jnp.max(lt, axis=0, keepdims=True)
        idx = jnp.min(jnp.where(lt == m, eidx, N_EXP), axis=0, keepdims=True)
        sel = eidx == idx
        vals.append(m)
        idxs.append(idx)
        cnt = cnt + sel.astype(F32)
        lt = jnp.where(sel, -jnp.inf, lt)
    ex = [jnp.exp(v - vals[0]) for v in vals]
    den = ex[0] + ex[1] + ex[2] + ex[3]
    tw_ref[0] = jnp.concatenate([e / den for e in ex], axis=0)
    te_ref[0] = jnp.concatenate(idxs, axis=0)
    hist_ref[0] = jnp.sum(cnt, axis=1, keepdims=True).astype(I32)


def _merge(x_ctx, x_lat, o_f, o_b, rest, yb, mod, hgrn_norm, wa_bf, wb_bf, wo_bf, norm2, wr_hi, wr_lo, b_router):
    full = lambda shape: pl.BlockSpec(shape, lambda i: (0,) * len(shape))
    return pl.pallas_call(
        _merge_kernel,
        out_shape=(jax.ShapeDtypeStruct((T_ALL, D_MODEL), F32),
                   jax.ShapeDtypeStruct((T_ALL, D_MODEL), BF16),
                   jax.ShapeDtypeStruct((NBD, TOP_K, TBD), I32),
                   jax.ShapeDtypeStruct((NBD, TOP_K, TBD), F32),
                   jax.ShapeDtypeStruct((NBD, N_EXP, 1), I32)),
        grid=(NBD,),
        in_specs=_x_specs() + [
                  pl.BlockSpec((TBD, HGRN_W), lambda i: (i, 0)),
                  pl.BlockSpec((TBD, HGRN_W), lambda i: (i, 0)),
                  pl.BlockSpec((TBD, HGRN_W), lambda i: (i, 2)),
                  pl.BlockSpec((TBD, POOL_W), lambda i: (i, 0)),
                  pl.BlockSpec((TBD, D_MODEL), lambda i: (i, 2)),
                  pl.BlockSpec((TBD, D_MODEL), lambda i: (i, 3)),
                  full((MOD_ROWS, 6 * D_MODEL)),
                  full((1, HEAD_V)),
                  full((HGRN_W, D_MODEL)),
                  full((POOL_W, D_MODEL)),
                  full((D_MODEL, D_MODEL)),
                  full((1, D_MODEL)),
                  full((N_EXP, D_MODEL)),
                  full((N_EXP, D_MODEL)),
                  full((N_EXP, 1))],
        out_specs=(pl.BlockSpec((TBD, D_MODEL), lambda i: (i, 0)),
                   pl.BlockSpec((TBD, D_MODEL), lambda i: (i, 0)),
                   pl.BlockSpec((1, TOP_K, TBD), lambda i: (i, 0, 0)),
                   pl.BlockSpec((1, TOP_K, TBD), lambda i: (i, 0, 0)),
                   pl.BlockSpec((1, N_EXP, 1), lambda i: (i, 0, 0))),
        compiler_params=_params(),
        name="merge",
    )(x_ctx, x_lat, o_f, o_b, rest, yb, rest, rest, mod, hgrn_norm, wa_bf, wb_bf, wo_bf, norm2,
      wr_hi, wr_lo, b_router)


def _local_rows(te_ref, loff_ref, tri_ref):
    te = te_ref[0]
    eidx = lax.broadcasted_iota(I32, (N_EXP, TBD), 0)
    sels = [eidx == te[k:k + 1] for k in range(TOP_K)]
    cnt = sels[0].astype(F32)
    for s in sels[1:]:
        cnt = cnt + s.astype(F32)
    base = _dot(cnt.astype(BF16), tri_ref[...]) + loff_ref[0]
    return [jnp.sum(jnp.where(s, base, 0.0), axis=0, keepdims=True) for s in sels]


def _chunk_relative(rows, r0):
    out = []
    for r in rows:
        inside = jnp.logical_and(r >= r0, r < r0 + SORT_CHUNK)
        out.append(jnp.where(inside, r - r0, -1.0).astype(BF16))
    return out


def _segment_copies(make_copy, local_off, global_off, units):
    big = units >> SEG_SMALL_BITS

    def piece(p, carry):
        off = pl.multiple_of(p * (8 << SEG_SMALL_BITS), 8 << SEG_SMALL_BITS)
        make_copy(pl.multiple_of(local_off + off, 8), pl.multiple_of(global_off + off, 8),
                  8 << SEG_SMALL_BITS).start()
        return carry

    lax.fori_loop(0, big, piece, 0)
    done = big << (SEG_SMALL_BITS + 3)
    for j in reversed(range(SEG_SMALL_BITS)):
        low = done + ((units >> (j + 1)) & ((1 << (SEG_SMALL_BITS - 1 - j)) - 1)) * (16 << j)

        @pl.when(((units >> j) & 1) == 1)
        def _():
            make_copy(pl.multiple_of(local_off + low, 8), pl.multiple_of(global_off + low, 8), 8 << j).start()


def _pack_pairs(x):
    half = D_MODEL // 2
    lo = lax.bitcast_convert_type(x[:, :half], U32) >> 16
    hi = lax.bitcast_convert_type(x[:, half:], U32) & jnp.uint32(0xFFFF0000)
    return hi | lo


def _unpack_pairs(p):
    lo = lax.bitcast_convert_type(p << 16, F32).astype(BF16)
    hi = lax.bitcast_convert_type(p & jnp.uint32(0xFFFF0000), F32).astype(BF16)
    return jnp.concatenate([lo, hi], axis=1)


def _block_rows(loff_s, seg_s, b):
    last = b * N_EXP + N_EXP - 1
    return pl.multiple_of(loff_s[last] + seg_s[last], 8)


def _dispatch_kernel(loff_s, seg_s, gbase_s, tail_s, h2_ref, te_ref, loffv_ref, tri_ref, iota_ref, xs_hbm,
                     loc, zeros, sem, sem_z):
    b = pl.program_id(0)
    slot = b % 2

    def wait_block(blk, s):
        n = _block_rows(loff_s, seg_s, blk)
        pltpu.make_async_copy(loc.at[s, pl.ds(0, n)], xs_hbm.at[pl.ds(0, n)], sem.at[s]).wait()

    lrow = _local_rows(te_ref, loffv_ref, tri_ref)

    @pl.when(b >= 2)
    def _():
        wait_block(b - 2, slot)

    for r0 in range(0, LROWS, SORT_CHUNK):
        rel = _chunk_relative(lrow, r0)
        p = jnp.zeros((SORT_CHUNK, TBD), BF16)
        for k in reversed(range(TOP_K)):
            p = jnp.where(iota_ref[...] == rel[k], jnp.ones_like(p), p)
        loc[slot, r0:r0 + SORT_CHUNK, :] = _pack_pairs(_dot(p, h2_ref[...]))

    def out_copy(a, g, size):
        return pltpu.make_async_copy(loc.at[slot, pl.ds(a, size)], xs_hbm.at[pl.ds(g, size)], sem.at[slot])

    def body(e, carry):
        idx = b * N_EXP + e
        _segment_copies(out_copy, loff_s[idx], gbase_s[idx], seg_s[idx] // 8)
        return carry

    lax.fori_loop(0, N_EXP, body, 0)

    @pl.when(b == NBD - 1)
    def _():
        zeros[...] = jnp.zeros_like(zeros)

        def zero_copy(a, g, size):
            return pltpu.make_async_copy(zeros.at[pl.ds(a, size)], xs_hbm.at[pl.ds(g, size)], sem_z)

        def zbody(e, ztot):
            _segment_copies(zero_copy, 0, tail_s[e], tail_s[N_EXP + e] // 8)
            return ztot + tail_s[N_EXP + e]

        def tbody(t, carry):
            pltpu.make_async_copy(zeros, xs_hbm.at[pl.ds(pl.multiple_of(t * TM, TM), TM)], sem_z).start()
            return carry

        n_used = tail_s[2 * N_EXP]
        lax.fori_loop(n_used, N_TILES, tbody, 0)
        ztot = lax.fori_loop(0, N_EXP, zbody, 0) + (N_TILES - n_used) * TM
        ztot = pl.multiple_of(ztot, 8)

        @pl.when(ztot > 0)
        def _():
            pltpu.make_async_copy(xs_hbm.at[pl.ds(0, ztot)], xs_hbm.at[pl.ds(0, ztot)], sem_z).wait()

        wait_block(b - 1, 1 - slot)
        wait_block(b, slot)


def _dispatch(tables, h2, top_e, tri, row_iota):
    loff_s, seg_s, gbase_s, tail_s, loff_v = tables
    grid_spec = pltpu.PrefetchScalarGridSpec(
        num_scalar_prefetch=4,
        grid=(NBD,),
        in_specs=[pl.BlockSpec((TBD, D_MODEL), lambda i, *_: (i, 0)),
                  pl.BlockSpec((1, TOP_K, TBD), lambda i, *_: (i, 0, 0)),
                  pl.BlockSpec((1, N_EXP, 1), lambda i, *_: (i, 0, 0)),
                  pl.BlockSpec((TBD, TBD), lambda i, *_: (0, 0)),
                  pl.BlockSpec((SORT_CHUNK, TBD), lambda i, *_: (0, 0))],
        out_specs=pl.BlockSpec(memory_space=pl.ANY),
        scratch_shapes=[pltpu.VMEM((2, LROWS, D_MODEL // 2), U32),
                        pltpu.VMEM((TM, D_MODEL // 2), U32),
                        pltpu.SemaphoreType.DMA((2,)),
                        pltpu.SemaphoreType.DMA])
    return pl.pallas_call(
        _dispatch_kernel,
        out_shape=jax.ShapeDtypeStruct((N_TILES * TM, D_MODEL // 2), U32),
        grid_spec=grid_spec,
        compiler_params=_params(),
        name="dispatch",
    )(loff_s, seg_s, gbase_s, tail_s, h2, top_e, loff_v, tri, row_iota)


def _moe_kernel(te_ref, first_ref, par_ref, next_ref, nv_ref, nu_ref, xs_ref, bgu_ref, bd_ref, wgu_hbm, wd_hbm, o_ref,
                wgu_st, wd_st, wgu_bf, wd_bf, sem):
    i = pl.program_id(0)

    def fetch(e, s):
        return (pltpu.make_async_copy(wgu_hbm.at[e], wgu_st.at[s], sem.at[0, s]),
                pltpu.make_async_copy(wd_hbm.at[e], wd_st.at[s], sem.at[1, s]))

    @pl.when(i < nu_ref[0])
    def _():
        @pl.when(first_ref[i] == 1)
        def _():
            s = par_ref[i]

            @pl.when(i == 0)
            def _():
                for cp in fetch(te_ref[0], 0):
                    cp.start()

            for cp in fetch(te_ref[i], s):
                cp.wait()

            @pl.when(next_ref[i] >= 0)
            def _():
                for cp in fetch(next_ref[i], 1 - s):
                    cp.start()

            wgu_bf[...] = wgu_st[s].astype(BF16)
            wd_bf[...] = wd_st[s].astype(BF16)

        def gate_up(r):
            return _dot(_unpack_pairs(xs_ref[r, :]), wgu_bf[...]) + bgu_ref[0]

        def activation(gu):
            gate = jnp.minimum(gu[:, :D_FF], SWIGLU_LIMIT)
            up = jnp.clip(gu[:, D_FF:], -SWIGLU_LIMIT, SWIGLU_LIMIT)
            return ((up + 1.0) * gate * jax.nn.sigmoid(SWIGLU_ALPHA * gate)).astype(BF16)

        def down(r, act):
            out = _dot(act, wd_bf[...]) + bd_ref[0]
            o_ref[r, :] = _pack_pairs(out.astype(BF16).astype(F32))

        for p0 in range(0, TM, 2 * MOE_SUB):
            pieces = [slice(p0, p0 + MOE_SUB), slice(p0 + MOE_SUB, p0 + 2 * MOE_SUB)]
            rows_here = nv_ref[i] - p0

            @pl.when(rows_here > MOE_SUB)
            def _():
                gu = [gate_up(r) for r in pieces]
                for r, g in zip(pieces, gu):
                    down(r, activation(g))

            @pl.when(jnp.logical_and(rows_here > 0, rows_here <= MOE_SUB))
            def _():
                down(pieces[0], activation(gate_up(pieces[0])))
                o_ref[pieces[1], :] = jnp.zeros((MOE_SUB, D_MODEL // 2), U32)

            @pl.when(rows_here <= 0)
            def _():
                o_ref[p0:p0 + 2 * MOE_SUB, :] = jnp.zeros((2 * MOE_SUB, D_MODEL // 2), U32)

    @pl.when(i >= nu_ref[0])
    def _():
        o_ref[...] = jnp.zeros_like(o_ref)


def _moe(tile_tables, xs, w_gate_up, b_gate_up, w_down, b_down):
    nsp = len(tile_tables)
    row_tile = lambda i, *s: (jnp.minimum(i, s[nsp - 1][0] - 1), 0)
    grid_spec = pltpu.PrefetchScalarGridSpec(
        num_scalar_prefetch=nsp,
        grid=(N_TILES,),
        in_specs=[pl.BlockSpec((TM, D_MODEL // 2), row_tile),
                  pl.BlockSpec((1, 1, 2 * D_FF), lambda i, te, *_: (te[i], 0, 0)),
                  pl.BlockSpec((1, 1, D_MODEL), lambda i, te, *_: (te[i], 0, 0)),
                  pl.BlockSpec(memory_space=pl.ANY),
                  pl.BlockSpec(memory_space=pl.ANY)],
        out_specs=pl.BlockSpec((TM, D_MODEL // 2), lambda i, *_: (i, 0)),
        scratch_shapes=[pltpu.VMEM((2, D_MODEL, 2 * D_FF), F32),
                        pltpu.VMEM((2, D_FF, D_MODEL), F32),
                        pltpu.VMEM((D_MODEL, 2 * D_FF), BF16),
                        pltpu.VMEM((D_FF, D_MODEL), BF16),
                        pltpu.SemaphoreType.DMA((2, 2))])
    return pl.pallas_call(
        _moe_kernel,
        out_shape=jax.ShapeDtypeStruct((N_TILES * TM, D_MODEL // 2), U32),
        grid_spec=grid_spec,
        compiler_params=_params(),
        name="moe",
    )(*tile_tables, xs, b_gate_up.reshape(N_EXP, 1, 2 * D_FF), b_down.reshape(N_EXP, 1, D_MODEL),
      w_gate_up, w_down)


def _final_kernel(loff_s, seg_s, gbase_s, ys_hbm, x1_ref, te_ref, tw_ref, loffv_ref, tri_ref, iota_ref, mod_ref, fn_ref,
                  oc_ref, ol_ref, loc, sem):
    b = pl.program_id(0)
    slot = b % 2

    def start_block(blk, s):
        def in_copy(a, g, size):
            return pltpu.make_async_copy(ys_hbm.at[pl.ds(g, size)], loc.at[s, pl.ds(a, size)], sem.at[s])

        def body(e, carry):
            idx = blk * N_EXP + e
            _segment_copies(in_copy, loff_s[idx], gbase_s[idx], seg_s[idx] // 8)
            return carry

        lax.fori_loop(0, N_EXP, body, 0)

    @pl.when(b == 0)
    def _():
        loc[...] = jnp.zeros_like(loc)
        start_block(0, 0)

    @pl.when(b + 1 < NBD)
    def _():
        start_block(b + 1, 1 - slot)

    lrow = _local_rows(te_ref, loffv_ref, tri_ref)
    tw = tw_ref[0]
    rows8 = jnp.concatenate(lrow + [tw], axis=0)
    cols = jnp.concatenate([rows8, jnp.zeros((128 - 2 * TOP_K, TBD), F32)], axis=0).T

    n = _block_rows(loff_s, seg_s, b)
    pltpu.make_async_copy(ys_hbm.at[pl.ds(0, n)], loc.at[slot, pl.ds(0, n)], sem.at[slot]).wait()

    y = None
    wts = [cols[:, TOP_K + k:TOP_K + k + 1].astype(BF16) for k in range(TOP_K)]
    for r0 in range(0, LROWS, SORT_CHUNK):
        rel = _chunk_relative([cols[:, k:k + 1] for k in range(TOP_K)], r0)
        pw = jnp.zeros((TBD, SORT_CHUNK), BF16)
        for k in reversed(range(TOP_K)):
            pw = jnp.where(iota_ref[...] == rel[k], wts[k], pw)
        part = _dot(pw, _unpack_pairs(loc[slot, r0:r0 + SORT_CHUNK, :]))
        y = part if y is None else y + part

    row = jnp.where(b < NBD_CTX, N_LAT_SEQ, (b - NBD_CTX) // (LAT_LEN // TBD))
    gate2 = mod_ref[pl.ds(row, 1), pl.ds(5 * D_MODEL, D_MODEL)]
    out = _rms(x1_ref[...] + gate2 * y, fn_ref[...])

    @pl.when(b < NBD_CTX)
    def _():
        oc_ref[...] = out

    @pl.when(b >= NBD_CTX)
    def _():
        ol_ref[...] = out


def _final(tables, ys, x1, top_e, top_w, tri, col_iota, mod, final_norm):
    loff_s, seg_s, gbase_s, _, loff_v = tables
    grid_spec = pltpu.PrefetchScalarGridSpec(
        num_scalar_prefetch=3,
        grid=(NBD,),
        in_specs=[pl.BlockSpec(memory_space=pl.ANY),
                  pl.BlockSpec((TBD, D_MODEL), lambda i, *_: (i, 0)),
                  pl.BlockSpec((1, TOP_K, TBD), lambda i, *_: (i, 0, 0)),
                  pl.BlockSpec((1, TOP_K, TBD), lambda i, *_: (i, 0, 0)),
                  pl.BlockSpec((1, N_EXP, 1), lambda i, *_: (i, 0, 0)),
                  pl.BlockSpec((TBD, TBD), lambda i, *_: (0, 0)),
                  pl.BlockSpec((TBD, SORT_CHUNK), lambda i, *_: (0, 0)),
                  pl.BlockSpec((MOD_ROWS, 6 * D_MODEL), lambda i, *_: (0, 0)),
                  pl.BlockSpec((1, D_MODEL), lambda i, *_: (0, 0))],
        out_specs=(pl.BlockSpec((TBD, D_MODEL), lambda i, *_: (jnp.minimum(i, NBD_CTX - 1), 0)),
                   pl.BlockSpec((TBD, D_MODEL), lambda i, *_: (jnp.maximum(i - NBD_CTX, 0), 0))),
        scratch_shapes=[pltpu.VMEM((2, LROWS, D_MODEL // 2), U32),
                        pltpu.SemaphoreType.DMA((2,))])
    return pl.pallas_call(
        _final_kernel,
        out_shape=(jax.ShapeDtypeStruct((T_CTX, D_MODEL), F32),
                   jax.ShapeDtypeStruct((T_LAT, D_MODEL), F32)),
        grid_spec=grid_spec,
        compiler_params=_params(),
        name="final",
    )(loff_s, seg_s, gbase_s, ys, x1, top_e, top_w, loff_v, tri, col_iota, mod, final_norm)


def _dispatch_tables(hist):
    hist = hist.reshape(NBD, N_EXP)
    seg = ((hist + 7) // 8) * 8
    loff = jnp.cumsum(seg, axis=1) - seg
    rows_e = jnp.sum(seg, axis=0)
    region = ((rows_e + TM - 1) // TM) * TM
    region_end = jnp.cumsum(region)
    region_start = region_end - region
    gbase = region_start[None, :] + jnp.cumsum(seg, axis=0) - seg
    n_used = (region_end[-1] // TM).astype(I32)
    tail = jnp.concatenate([region_start + rows_e, region - rows_e, n_used.reshape(1)])
    start = jnp.arange(N_TILES, dtype=I32) * TM
    tile_e = jnp.sum((start[:, None] >= region_end[None, :]).astype(I32), axis=1)
    tile_e = jnp.minimum(tile_e, tile_e[jnp.maximum(n_used - 1, 0)])
    first = jnp.concatenate([jnp.ones((1,), I32), (tile_e[1:] != tile_e[:-1]).astype(I32)])
    parity = (jnp.cumsum(first) - 1) % 2
    later = jnp.where(tile_e[None, :] > tile_e[:, None], tile_e[None, :], N_EXP)
    nxt = jnp.min(later, axis=1)
    nxt = jnp.where(nxt == N_EXP, -1, nxt)
    flat = lambda a: a.reshape(-1).astype(I32)
    tables = (flat(loff), flat(seg), flat(gbase), flat(tail), loff.astype(F32).reshape(NBD, N_EXP, 1))
    mine = tile_e[:, None] == jnp.arange(N_EXP, dtype=I32)[None, :]
    data_end = jnp.sum(jnp.where(mine, (region_start + rows_e)[None, :], 0), axis=1)
    tile_nv = jnp.clip(data_end - start, 0, TM)
    tile_tables = (flat(tile_e), flat(first), flat(parity), flat(nxt), flat(tile_nv), n_used.reshape(1))
    return tables, tile_tables


def kernel(x_prompt, x_sample, state_hgrn, c, c_ctx, w_ada, b_ada, norm1, w_in, hgrn_lb, hgrn_norm, w_pool,
           pool_scale, w_branch_a, w_branch_b, w_out, norm2, w_router, b_router, w_gate_up, b_gate_up,
           w_down, b_down, final_norm):
    x_ctx = x_prompt.reshape(T_CTX, D_MODEL)
    x_lat = x_sample.reshape(T_LAT, D_MODEL)
    cc = jnp.zeros((MOD_ROWS, D_MODEL), F32).at[:N_LAT_SEQ].set(c).at[N_LAT_SEQ].set(c_ctx)
    mod = _ada(cc, w_ada[0], b_ada)

    w_in_bf = w_in[0].astype(BF16)
    w_gates = w_in_bf[:, HGRN_W:HGRN_W + GATE_W]
    w_rest = jnp.concatenate([w_in_bf[:, :HGRN_W], w_in_bf[:, HGRN_W + GATE_W:]], axis=1)
    gates, rest = _inproj(x_ctx, x_lat, mod, norm1, w_gates, w_rest)

    mall, masks = _hgrn_consts()
    o_f, o_b, new_state = _hgrn(gates, rest, hgrn_lb, state_hgrn[:, 0], mall, masks)

    a_pool, cnt_pool = _pool_consts()
    yb = _pool(rest, a_pool, cnt_pool, w_pool[0].astype(BF16), pool_scale)

    wr_t = w_router[0].T
    wr_hi = wr_t.astype(BF16)
    wr_lo = (wr_t - wr_hi.astype(F32)).astype(BF16)
    tri = jnp.asarray(np.triu(np.ones((TBD, TBD), np.float32), 1), BF16)
    x1, h2, top_e, top_w, hist = _merge(
        x_ctx, x_lat, o_f, o_b, rest, yb, mod, hgrn_norm, w_branch_a[0].astype(BF16), w_branch_b[0].astype(BF16),
        w_out[0].astype(BF16), norm2, wr_hi, wr_lo, b_router.reshape(N_EXP, 1))

    tables, tile_tables = _dispatch_tables(hist)
    row_iota = jnp.asarray(np.broadcast_to(np.arange(SORT_CHUNK, dtype=np.float32)[:, None], (SORT_CHUNK, TBD)), BF16)
    xs = _dispatch(tables, h2, top_e, tri, row_iota)
    ys = _moe(tile_tables, xs, w_gate_up[0], b_gate_up[0], w_down[0], b_down[0])
    y_ctx, y_lat = _final(tables, ys, x1, top_e, top_w, tri, row_iota.T, mod, final_norm.reshape(1, D_MODEL))
    y_prompt = y_ctx.reshape(N_CTX_SEQ, CTX_LEN, D_MODEL)
    y_sample = y_lat.reshape(N_LAT_SEQ, LAT_LEN, D_MODEL)
    return y_prompt, y_sample, new_state[:, None]
```

```python
import functools

import numpy as np
import jax
import jax.numpy as jnp
from jax import lax
from jax.experimental import pallas as pl
from jax.experimental.pallas import tpu as pltpu

F32 = jnp.float32
BF16 = jnp.bfloat16
I32 = jnp.int32
U32 = jnp.uint32

D_MODEL = 1024
N_CTX_SEQ, CTX_LEN = 32, 256
N_LAT_SEQ, LAT_LEN = 4, 2048
T_CTX = N_CTX_SEQ * CTX_LEN
T_LAT = N_LAT_SEQ * LAT_LEN
T_ALL = T_CTX + T_LAT
TB = 256
NB = T_ALL // TB
NB_CTX = T_CTX // TB
LAT_BLOCKS = LAT_LEN // TB
HEADS, HEAD_K, HEAD_V = 4, 128, 128
HGRN_W = HEADS * HEAD_V
POOL_WINDOWS = (2, 4, 8, 16)
POOL_G = 128
POOL_W = len(POOL_WINDOWS) * POOL_G
GRID_W = 64
GRID_H = LAT_LEN // GRID_W
IN_W = 5 * HGRN_W + POOL_W + 2 * D_MODEL
GATE_W = 2 * HGRN_W
REST_W = IN_W - GATE_W
N_EXP, TOP_K, D_FF = 32, 4, 1024
SWIGLU_LIMIT = 7.0
SWIGLU_ALPHA = 1.702
EPS = 1e-6
LOG2_E = 1.4426950408889634
CHUNK = 64
N_LEVELS = 6
EXP_ROWS = (N_LEVELS + 2) * CHUNK
MM_BLOCKS = (0, 4, 5, 6)
MM_ROWS = len(MM_BLOCKS) * CHUNK
COARSE_LEVELS = ((1, 32), (2, 16), (3, 8))
TM = 512
MOE_PIECES = ((0, 128, (128,)), (128, 256, (256,)), (256, 384, (256, 128)), (384, 512, (256, 256)))
TBD = 512
NBD = T_ALL // TBD
NBD_CTX = T_CTX // TBD
SORT_CHUNK = 256
LROWS = TBD * TOP_K + N_EXP * 8
SEG_SMALL_BITS = 3
N_TILES = -(-(T_ALL * TOP_K + NBD * N_EXP * 7 + N_EXP * (TM - 1)) // TM)
MOD_ROWS = 8
VMEM_LIMIT = 56 * 1024 * 1024


def _params(sem=("arbitrary",)):
    return pltpu.CompilerParams(dimension_semantics=sem, vmem_limit_bytes=VMEM_LIMIT)


def _dot(a, b):
    return jnp.dot(a, b, preferred_element_type=F32)


def _dot_nt(a, b):
    return lax.dot_general(a, b, (((1,), (1,)), ((), ())), preferred_element_type=F32)


def _dot_tn(a, b):
    return lax.dot_general(a, b, (((0,), (0,)), ((), ())), preferred_element_type=F32)


def _split2(x):
    hi = x.astype(BF16)
    lo = (x - hi.astype(F32)).astype(BF16)
    return hi, lo


def _mod_row(i):
    return jnp.where(i < NBD_CTX, N_LAT_SEQ, (i - NBD_CTX) // (LAT_LEN // TBD))


def _ada_kernel(c_ref, w_ref, b_ref, o_ref):
    c = c_ref[...]
    s = c * jax.nn.sigmoid(c)
    o_ref[...] = jnp.dot(s, w_ref[...], preferred_element_type=F32,
                         precision=lax.Precision.HIGHEST) + b_ref[...]


def _ada(cc, w_ada, b_ada):
    nblk = 1536
    return pl.pallas_call(
        _ada_kernel,
        out_shape=jax.ShapeDtypeStruct((MOD_ROWS, 6 * D_MODEL), F32),
        grid=(6 * D_MODEL // nblk,),
        in_specs=[pl.BlockSpec((MOD_ROWS, D_MODEL), lambda j: (0, 0)),
                  pl.BlockSpec((D_MODEL, nblk), lambda j: (0, j)),
                  pl.BlockSpec((1, nblk), lambda j: (0, j))],
        out_specs=pl.BlockSpec((MOD_ROWS, nblk), lambda j: (0, j)),
        compiler_params=_params(),
        name="ada",
    )(cc, w_ada, b_ada)


def _rms(x, g):
    ms = jnp.mean(x * x, axis=-1, keepdims=True)
    return x * lax.rsqrt(ms + EPS) * g


def _x_specs():
    return [pl.BlockSpec((TBD, D_MODEL), lambda i, *_: (jnp.minimum(i, NBD_CTX - 1), 0)),
            pl.BlockSpec((TBD, D_MODEL), lambda i, *_: (jnp.maximum(i - NBD_CTX, 0), 0))]


def _x_block(xc_ref, xl_ref):
    return jnp.where(pl.program_id(0) < NBD_CTX, xc_ref[...], xl_ref[...])


def _inproj_kernel(xc_ref, xl_ref, mod_ref, n1_ref, wg_ref, wr_ref, og_ref, or_ref):
    row = _mod_row(pl.program_id(0))
    shift = mod_ref[pl.ds(row, 1), pl.ds(0, D_MODEL)]
    scale = mod_ref[pl.ds(row, 1), pl.ds(D_MODEL, D_MODEL)]
    h = (_rms(_x_block(xc_ref, xl_ref), n1_ref[...]) * (1.0 + scale) + shift).astype(BF16)
    og_ref[...] = _dot(h, wg_ref[...])
    or_ref[...] = _dot(h, wr_ref[...]).astype(BF16)


def _inproj(x_ctx, x_lat, mod, norm1, w_gates_bf, w_rest_bf):
    return pl.pallas_call(
        _inproj_kernel,
        out_shape=(jax.ShapeDtypeStruct((T_ALL, GATE_W), F32),
                   jax.ShapeDtypeStruct((T_ALL, REST_W), BF16)),
        grid=(NBD,),
        in_specs=_x_specs() + [
                  pl.BlockSpec((MOD_ROWS, 6 * D_MODEL), lambda i: (0, 0)),
                  pl.BlockSpec((1, D_MODEL), lambda i: (0, 0)),
                  pl.BlockSpec((D_MODEL, GATE_W), lambda i: (0, 0)),
                  pl.BlockSpec((D_MODEL, REST_W), lambda i: (0, 0))],
        out_specs=(pl.BlockSpec((TBD, GATE_W), lambda i: (i, 0)),
                   pl.BlockSpec((TBD, REST_W), lambda i: (i, 0))),
        compiler_params=_params(),
        name="inproj",
    )(x_ctx, x_lat, mod, norm1, w_gates_bf, w_rest_bf)


def _hgrn_consts():
    c = CHUNK
    t = np.arange(c)[:, None]
    u = np.arange(c)[None, :]
    blocks = [u <= t]
    masks = [np.eye(c, dtype=bool)]
    h = c // 2
    while h >= 1:
        bi = t // h
        upper = (bi % 2) == 1
        e_up = (u >= bi * h) & (u <= t)
        e_lo = (u > t) & (u <= bi * h + h - 1)
        blocks.append(np.where(upper, e_up, e_lo))
        masks.append(((t // (2 * h)) == (u // (2 * h))) & (((t // h) % 2) == 1) & (((u // h) % 2) == 0))
        h //= 2
    blocks.append(u > t)
    m_f = np.stack(blocks).astype(np.float32)
    k_f = np.stack(masks).astype(np.float32)
    m_b = m_f[:, ::-1, ::-1]
    k_b = k_f[:, ::-1, ::-1]
    sel = list(MM_BLOCKS)
    m = np.stack([m_f[sel].reshape(MM_ROWS, c), m_b[sel].reshape(MM_ROWS, c)])
    m3 = np.concatenate([m, m, m], axis=2)
    return jnp.asarray(m3, BF16), jnp.asarray(np.stack([k_f, k_b]), F32)


def _hgrn_block(dirs, lb, mall_ref, mask_ref, st_ref, z_ref, k_ref, sc_ref, run_ref):
    c = CHUNK
    nchunk = TB // c
    units = [(d, h) for d in range(2) for h in range(HEADS)]
    sl = [slice(h * HEAD_K, (h + 1) * HEAD_K) for h in range(HEADS)]

    def rows(ci, d):
        r0 = ci * c if d == 0 else (nchunk - 1 - ci) * c
        return slice(r0, r0 + c)

    def exponents(ci):
        s = ci % 2
        for d in range(2):
            f = lb[d:d + 1] + (1.0 - lb[d:d + 1]) * jax.nn.sigmoid(dirs[d][1][rows(ci, d), :])
            k_ref[s, d] = 1.0 - f
            k_ref[s, 2 + d] = dirs[d][0][rows(ci, d), :].astype(F32)
            g = jnp.log(f) * LOG2_E
            g1 = g.astype(BF16)
            r1 = g - g1.astype(F32)
            g2 = r1.astype(BF16)
            g3 = (r1 - g2.astype(F32)).astype(BF16)
            gsplit = jnp.concatenate([g1, g2, g3], axis=0)
            ex = _dot(mall_ref[d], gsplit)
            run = ex[0:c]
            run_ref[d] = run
            z_ref[s, d, 0:c] = jnp.exp2(run)
            for j, blk in enumerate(MM_BLOCKS[1:]):
                z_ref[s, d, blk * c:(blk + 1) * c] = jnp.exp2(ex[(j + 1) * c:(j + 2) * c])
            for blk, h in COARSE_LEVELS:
                for base in range(0, c, 2 * h):
                    ref = run_ref[d, base + h - 1 + d:base + h + d, :]
                    if d == 0:
                        first, second = ref - run[base:base + h], run[base + h:base + 2 * h] - ref
                    else:
                        first, second = run[base:base + h] - ref, ref - run[base + h:base + 2 * h]
                    z_ref[s, d, blk * c + base:blk * c + base + h] = jnp.exp2(first)
                    z_ref[s, d, blk * c + base + h:blk * c + base + 2 * h] = jnp.exp2(second)
            end = run_ref[d, c - 1:c, :] if d == 0 else run_ref[d, 0:1, :]
            z_ref[s, d, (N_LEVELS + 1) * c:] = jnp.exp2(end - run)

    def q_of(ci, d, h):
        return k_ref[ci % 2, 2 + d, :, sl[h]]

    def v_of(ci, d, h):
        return dirs[d][2][rows(ci, d), sl[h]].astype(BF16)

    def qz(ci, d, h, blk):
        return (q_of(ci, d, h) * z_ref[ci % 2, d, blk * c:(blk + 1) * c, sl[h]]).astype(BF16)

    def kz(ci, d, h, blk):
        return (k_ref[ci % 2, d, :, sl[h]] * z_ref[ci % 2, d, blk * c:(blk + 1) * c, sl[h]]).astype(BF16)

    def levels(ci):
        for d, h in units:
            q = q_of(ci, d, h).astype(F32)
            k = k_ref[ci % 2, d, :, sl[h]]
            k_next = pltpu.roll(k, 1 if d == 0 else c - 1, 0)
            zq = q * z_ref[ci % 2, d, N_LEVELS * c:(N_LEVELS + 1) * c, sl[h]]
            diag = jnp.sum(q * k, axis=1, keepdims=True)
            near = jnp.sum(zq * k_next, axis=1, keepdims=True)
            sc_ref[d, h] = mask_ref[d, 0] * diag + mask_ref[d, N_LEVELS] * near
        for lev in range(N_LEVELS - 1):
            for d, h in units:
                sc_ref[d, h] += mask_ref[d, lev + 1] * _dot_nt(qz(ci, d, h, lev + 1), kz(ci, d, h, lev + 1))

    def tail(ci):
        for d, h in units:
            o = (_dot_nt(qz(ci, d, h, 0), st_ref[d, h].astype(BF16))
                 + _dot(sc_ref[d, h].astype(BF16), v_of(ci, d, h)))
            dirs[d][3][rows(ci, d), sl[h]] = o * (HEAD_K ** -0.5)
        for d, h in units:
            tot_row = c - 1 if d == 0 else 0
            decay = z_ref[ci % 2, d, tot_row:tot_row + 1, sl[h]]
            st_ref[d, h] = st_ref[d, h] * decay + _dot_tn(v_of(ci, d, h), kz(ci, d, h, N_LEVELS + 1))

    exponents(0)
    for ci in range(nchunk):
        levels(ci)
        if ci + 1 < nchunk:
            exponents(ci + 1)
        tail(ci)


def _hgrn_kernel(qf_ref, ff_ref, vf_ref, qb_ref, fb_ref, vb_ref, lbraw_ref, s0_ref, mall_ref, mask_ref,
                 of_ref, ob_ref, sout_hbm, st_ref, stage_ref, z_ref, k_ref, sc_ref, run_ref, sem):
    i = pl.program_id(0)
    j = (i - NB_CTX) % LAT_BLOCKS
    is_ctx = i < NB_CTX

    @pl.when(is_ctx)
    def _():
        st_ref[...] = jnp.zeros_like(st_ref)

    @pl.when(jnp.logical_and(jnp.logical_not(is_ctx), j == 0))
    def _():
        for d in range(2):
            for h in range(HEADS):
                st_ref[d, h] = s0_ref[0, d, h].T

    a0 = lbraw_ref[0]
    a1 = lbraw_ref[1]
    mx = jnp.maximum(a0, a1)
    e0 = jnp.exp(a0 - mx)
    e1 = jnp.exp(a1 - mx)
    lb = e0 / (e0 + e1)

    dirs = ((qf_ref, ff_ref, vf_ref, of_ref), (qb_ref, fb_ref, vb_ref, ob_ref))
    _hgrn_block(dirs, lb, mall_ref, mask_ref, st_ref, z_ref, k_ref, sc_ref, run_ref)

    @pl.when(is_ctx)
    def _():
        for d in range(2):
            for h in range(HEADS):
                stage_ref[d, h] = st_ref[d, h].T
        cp = pltpu.make_async_copy(stage_ref, sout_hbm.at[i], sem)
        cp.start()
        cp.wait()


def _bwd_block(i):
    j = (i - NB_CTX) % LAT_BLOCKS
    return jnp.where(i < NB_CTX, i, i - j + (LAT_BLOCKS - 1 - j))


def _hgrn(gates, rest, hgrn_lb, s0, mall, masks):
    nh = HGRN_W
    fwd = lambda col: pl.BlockSpec((TB, nh), lambda i: (i, col))
    bwd = lambda col: pl.BlockSpec((TB, nh), lambda i: (_bwd_block(i), col))
    lat_seq = lambda i: jnp.clip((i - NB_CTX) // LAT_BLOCKS, 0, N_LAT_SEQ - 1)
    return pl.pallas_call(
        _hgrn_kernel,
        out_shape=(jax.ShapeDtypeStruct((T_ALL, nh), F32),
                   jax.ShapeDtypeStruct((T_ALL, nh), F32),
                   jax.ShapeDtypeStruct((N_CTX_SEQ, 2, HEADS, HEAD_K, HEAD_V), F32)),
        grid=(NB,),
        in_specs=[fwd(0), fwd(0), fwd(1), bwd(0), bwd(1), bwd(1),
                  pl.BlockSpec((2, 2, nh), lambda i: (0, 0, 0)),
                  pl.BlockSpec((1, 2, HEADS, HEAD_K, HEAD_V), lambda i: (lat_seq(i), 0, 0, 0, 0)),
                  pl.BlockSpec((2, MM_ROWS, 3 * CHUNK), lambda i: (0, 0, 0)),
                  pl.BlockSpec((2, N_LEVELS + 1, CHUNK, CHUNK), lambda i: (0, 0, 0, 0))],
        out_specs=(pl.BlockSpec((TB, nh), lambda i: (i, 0)),
                   pl.BlockSpec((TB, nh), lambda i: (_bwd_block(i), 0)),
                   pl.BlockSpec(memory_space=pl.ANY)),
        scratch_shapes=[pltpu.VMEM((2, HEADS, HEAD_V, HEAD_K), F32),
                        pltpu.VMEM((2, HEADS, HEAD_K, HEAD_V), F32),
                        pltpu.VMEM((2, 2, EXP_ROWS, HGRN_W), F32),
                        pltpu.VMEM((2, 4, CHUNK, HGRN_W), F32),
                        pltpu.VMEM((2, HEADS, CHUNK, CHUNK), F32),
                        pltpu.VMEM((2, CHUNK, HGRN_W), F32),
                        pltpu.SemaphoreType.DMA],
        compiler_params=_params(),
        name="hgrn",
    )(rest, gates, rest, rest, gates, rest, hgrn_lb, s0, mall, masks)


def _window_bounds(n, w):
    pos = np.arange(n)
    lo = np.clip(pos - w // 2, 0, n - 1)
    hi = np.clip(pos - w // 2 + w - 1, 0, n - 1)
    return lo, hi


def _pool_consts():
    seq, img, cnt_seq, cnt_col = [], [], [], []
    for w in POOL_WINDOWS:
        lo, hi = _window_bounds(CTX_LEN, w)
        u = np.arange(CTX_LEN)[None, :]
        seq.append((u >= lo[:, None]) & (u <= hi[:, None]))
        cnt_seq.append(hi - lo + 1)
        lo, hi = _window_bounds(GRID_W, w)
        u = np.arange(GRID_W)[None, :]
        band = (u >= lo[:, None]) & (u <= hi[:, None])
        img.append(np.kron(np.eye(TB // GRID_W, dtype=bool), band))
        cnt_col.append(np.tile(hi - lo + 1, TB // GRID_W))
    a = np.stack([np.stack(seq), np.stack(img)]).astype(np.float32)
    cnt = np.stack([np.stack(cnt_seq), np.stack(cnt_col)]).astype(np.float32)
    cnt = np.broadcast_to(cnt[..., None], cnt.shape + (POOL_G,))
    return jnp.asarray(a, BF16), jnp.asarray(cnt, F32)


POOL_ROWS = LAT_LEN


def _pool_kernel(u_ref, a_ref, cnt_ref, wp_ref, ps_ref, o_ref, cp_ref):
    i = pl.program_id(0)
    nblk = POOL_ROWS // TB

    def finish(g, r0, nrows, pm):
        sl = slice(g * POOL_G, (g + 1) * POOL_G)
        d = pm - u_ref[pl.ds(r0, nrows), sl].astype(F32)
        y = _dot(d.astype(BF16), wp_ref[g]) * ps_ref[:, sl]
        o_ref[pl.ds(r0, nrows), sl] = y.astype(o_ref.dtype)

    def window_sum(kind, g, b):
        sl = slice(g * POOL_G, (g + 1) * POOL_G)
        return _dot(a_ref[kind, g], u_ref[pl.ds(b * TB, TB), sl]) / cnt_ref[kind, g]

    @pl.when(i < T_CTX // POOL_ROWS)
    def _():
        for g in range(len(POOL_WINDOWS)):
            for b in range(nblk):
                finish(g, b * TB, TB, window_sum(0, g, b))

    @pl.when(i >= T_CTX // POOL_ROWS)
    def _():
        for g, w in enumerate(POOL_WINDOWS):
            for b in range(nblk):
                cp_ref[pl.ds(b * TB, TB), :] = window_sum(1, g, b)
            lo, hi = _window_bounds(GRID_H, w)
            for r in range(GRID_H):
                acc = cp_ref[pl.ds(int(lo[r]) * GRID_W, GRID_W), :]
                for rr in range(int(lo[r]) + 1, int(hi[r]) + 1):
                    acc = acc + cp_ref[pl.ds(rr * GRID_W, GRID_W), :]
                finish(g, r * GRID_W, GRID_W, acc / float(hi[r] - lo[r] + 1))


def _pool(rest, a_pool, cnt_pool, w_pool_bf, pool_scale):
    col = 3
    return pl.pallas_call(
        _pool_kernel,
        out_shape=jax.ShapeDtypeStruct((T_ALL, POOL_W), BF16),
        grid=(T_ALL // POOL_ROWS,),
        in_specs=[pl.BlockSpec((POOL_ROWS, POOL_W), lambda i: (i, col)),
                  pl.BlockSpec((2, 4, TB, TB), lambda i: (0, 0, 0, 0)),
                  pl.BlockSpec((2, 4, TB, POOL_G), lambda i: (0, 0, 0, 0)),
                  pl.BlockSpec((4, POOL_G, POOL_G), lambda i: (0, 0, 0)),
                  pl.BlockSpec((1, POOL_W), lambda i: (0, 0))],
        out_specs=pl.BlockSpec((POOL_ROWS, POOL_W), lambda i: (i, 0)),
        scratch_shapes=[pltpu.VMEM((POOL_ROWS, POOL_G), F32)],
        compiler_params=_params(),
        name="pool",
    )(rest, a_pool, cnt_pool, w_pool_bf, pool_scale)


def _merge_kernel(xc_ref, xl_ref, of_ref, ob_ref, og_ref, yb_ref, ga_ref, gb_ref, mod_ref, hn_ref, wa_ref, wb_ref,
                  wo_ref, n2_ref, wrh_ref, wrl_ref, br_ref,
                  x1_ref, h2_ref, te_ref, tw_ref, hist_ref):
    row = _mod_row(pl.program_id(0))
    gate1 = mod_ref[pl.ds(row, 1), pl.ds(2 * D_MODEL, D_MODEL)]
    shift2 = mod_ref[pl.ds(row, 1), pl.ds(3 * D_MODEL, D_MODEL)]
    scale2 = mod_ref[pl.ds(row, 1), pl.ds(4 * D_MODEL, D_MODEL)]

    halves = [slice(j * (TBD // 2), (j + 1) * (TBD // 2)) for j in range(2)]
    is_ctx = pl.program_id(0) < NBD_CTX

    def head_out(r):
        o = of_ref[r, :] + ob_ref[r, :]
        og = og_ref[r, :].astype(F32)
        ya = jnp.concatenate(
            [_rms(o[:, h * HEAD_V:(h + 1) * HEAD_V], hn_ref[...]) for h in range(HEADS)], axis=1)
        return (ya * (og * jax.nn.sigmoid(og))).astype(BF16)

    ya = [head_out(r) for r in halves]
    pa = [_dot(ya[j], wa_ref[...]) for j in range(2)]
    pb = [_dot(yb_ref[r, :], wb_ref[...]) for r in halves]
    merged = [(jax.nn.sigmoid(ga_ref[r, :].astype(F32)) * pa[j]
               + jax.nn.sigmoid(gb_ref[r, :].astype(F32)) * pb[j]).astype(BF16) for j, r in enumerate(halves)]
    po = [_dot(merged[j], wo_ref[...]) for j in range(2)]
    hh, hl = [], []
    for j, r in enumerate(halves):
        x1 = jnp.where(is_ctx, xc_ref[r, :], xl_ref[r, :]) + gate1 * po[j]
        x1_ref[r, :] = x1
        hi, lo = _split2(_rms(x1, n2_ref[...]) * (1.0 + scale2) + shift2)
        h2_ref[r, :] = hi
        hh.append(hi)
        hl.append(lo)
    hh = jnp.concatenate(hh, axis=0)
    hl = jnp.concatenate(hl, axis=0)

    lt = _dot_nt(wrh_ref[...], hh) + _dot_nt(wrl_ref[...], hh) + _dot_nt(wrh_ref[...], hl) + br_ref[...]
    eidx = lax.broadcasted_iota(I32, (N_EXP, TBD), 0)
    vals, idxs, cnt = [], [], jnp.zeros((N_EXP, TBD), F32)
    for _ in range(TOP_K):
        m = jnp.max(lt, axis=0, keepdims=True)
        idx = jnp.min(jnp.where(lt == m, eidx, N_EXP), axis=0, keepdims=True)
        sel = eidx == idx
        vals.append(m)
        idxs.append(idx)
        cnt = cnt + sel.astype(F32)
        lt = jnp.where(sel, -jnp.inf, lt)
    ex = [jnp.exp(v - vals[0]) for v in vals]
    den = ex[0] + ex[1] + ex[2] + ex[3]
    tw_ref[0] = jnp.concatenate([e / den for e in ex], axis=0)
    te_ref[0] = jnp.concatenate(idxs, axis=0)
    hist_ref[0] = jnp.sum(cnt, axis=1, keepdims=True).astype(I32)


def _merge(x_ctx, x_lat, o_f, o_b, rest, yb, mod, hgrn_norm, wa_bf, wb_bf, wo_bf, norm2, wr_hi, wr_lo, b_router):
    full = lambda shape: pl.BlockSpec(shape, lambda i: (0,) * len(shape))
    return pl.pallas_call(
        _merge_kernel,
        out_shape=(jax.ShapeDtypeStruct((T_ALL, D_MODEL), F32),
                   jax.ShapeDtypeStruct((T_ALL, D_MODEL), BF16),
                   jax.ShapeDtypeStruct((NBD, TOP_K, TBD), I32),
                   jax.ShapeDtypeStruct((NBD, TOP_K, TBD), F32),
                   jax.ShapeDtypeStruct((NBD, N_EXP, 1), I32)),
        grid=(NBD,),
        in_specs=_x_specs() + [
                  pl.BlockSpec((TBD, HGRN_W), lambda i: (i, 0)),
                  pl.BlockSpec((TBD, HGRN_W), lambda i: (i, 0)),
                  pl.BlockSpec((TBD, HGRN_W), lambda i: (i, 2)),
                  pl.BlockSpec((TBD, POOL_W), lambda i: (i, 0)),
                  pl.BlockSpec((TBD, D_MODEL), lambda i: (i, 2)),
                  pl.BlockSpec((TBD, D_MODEL), lambda i: (i, 3)),
                  full((MOD_ROWS, 6 * D_MODEL)),
                  full((1, HEAD_V)),
                  full((HGRN_W, D_MODEL)),
                  full((POOL_W, D_MODEL)),
                  full((D_MODEL, D_MODEL)),
                  full((1, D_MODEL)),
                  full((N_EXP, D_MODEL)),
                  full((N_EXP, D_MODEL)),
                  full((N_EXP, 1))],
        out_specs=(pl.BlockSpec((TBD, D_MODEL), lambda i: (i, 0)),
                   pl.BlockSpec((TBD, D_MODEL), lambda i: (i, 0)),
                   pl.BlockSpec((1, TOP_K, TBD), lambda i: (i, 0, 0)),
                   pl.BlockSpec((1, TOP_K, TBD), lambda i: (i, 0, 0)),
                   pl.BlockSpec((1, N_EXP, 1), lambda i: (i, 0, 0))),
        compiler_params=_params(),
        name="merge",
    )(x_ctx, x_lat, o_f, o_b, rest, yb, rest, rest, mod, hgrn_norm, wa_bf, wb_bf, wo_bf, norm2,
      wr_hi, wr_lo, b_router)


def _local_rows(te_ref, loff_ref, tri_ref):
    te = te_ref[0]
    eidx = lax.broadcasted_iota(I32, (N_EXP, TBD), 0)
    sels = [eidx == te[k:k + 1] for k in range(TOP_K)]
    cnt = sels[0].astype(F32)
    for s in sels[1:]:
        cnt = cnt + s.astype(F32)
    base = _dot(cnt.astype(BF16), tri_ref[...]) + loff_ref[0]
    return [jnp.sum(jnp.where(s, base, 0.0), axis=0, keepdims=True) for s in sels]


def _chunk_relative(rows, r0):
    out = []
    for r in rows:
        inside = jnp.logical_and(r >= r0, r < r0 + SORT_CHUNK)
        out.append(jnp.where(inside, r - r0, -1.0).astype(BF16))
    return out


def _segment_copies(make_copy, local_off, global_off, units):
    big = units >> SEG_SMALL_BITS

    def piece(p, carry):
        off = pl.multiple_of(p * (8 << SEG_SMALL_BITS), 8 << SEG_SMALL_BITS)
        make_copy(pl.multiple_of(local_off + off, 8), pl.multiple_of(global_off + off, 8),
                  8 << SEG_SMALL_BITS).start()
        return carry

    lax.fori_loop(0, big, piece, 0)
    done = big << (SEG_SMALL_BITS + 3)
    for j in reversed(range(SEG_SMALL_BITS)):
        low = done + ((units >> (j + 1)) & ((1 << (SEG_SMALL_BITS - 1 - j)) - 1)) * (16 << j)

        @pl.when(((units >> j) & 1) == 1)
        def _():
            make_copy(pl.multiple_of(local_off + low, 8), pl.multiple_of(global_off + low, 8), 8 << j).start()


def _pack_pairs(x):
    half = D_MODEL // 2
    lo = lax.bitcast_convert_type(x[:, :half], U32) >> 16
    hi = lax.bitcast_convert_type(x[:, half:], U32) & jnp.uint32(0xFFFF0000)
    return hi | lo


def _unpack_pairs(p):
    lo = lax.bitcast_convert_type(p << 16, F32).astype(BF16)
    hi = lax.bitcast_convert_type(p & jnp.uint32(0xFFFF0000), F32).astype(BF16)
    return jnp.concatenate([lo, hi], axis=1)


def _block_rows(loff_s, seg_s, b):
    last = b * N_EXP + N_EXP - 1
    return pl.multiple_of(loff_s[last] + seg_s[last], 8)


def _dispatch_kernel(loff_s, seg_s, gbase_s, tail_s, h2_ref, te_ref, loffv_ref, tri_ref, iota_ref, xs_hbm,
                     loc, zeros, sem, sem_z):
    b = pl.program_id(0)
    slot = b % 2

    def wait_block(blk, s):
        n = _block_rows(loff_s, seg_s, blk)
        pltpu.make_async_copy(loc.at[s, pl.ds(0, n)], xs_hbm.at[pl.ds(0, n)], sem.at[s]).wait()

    lrow = _local_rows(te_ref, loffv_ref, tri_ref)

    @pl.when(b >= 2)
    def _():
        wait_block(b - 2, slot)

    for r0 in range(0, LROWS, SORT_CHUNK):
        rel = _chunk_relative(lrow, r0)
        p = jnp.zeros((SORT_CHUNK, TBD), BF16)
        for k in reversed(range(TOP_K)):
            p = jnp.where(iota_ref[...] == rel[k], jnp.ones_like(p), p)
        loc[slot, r0:r0 + SORT_CHUNK, :] = _pack_pairs(_dot(p, h2_ref[...]))

    def out_copy(a, g, size):
        return pltpu.make_async_copy(loc.at[slot, pl.ds(a, size)], xs_hbm.at[pl.ds(g, size)], sem.at[slot])

    def body(e, carry):
        idx = b * N_EXP + e
        _segment_copies(out_copy, loff_s[idx], gbase_s[idx], seg_s[idx] // 8)
        return carry

    lax.fori_loop(0, N_EXP, body, 0)

    @pl.when(b == NBD - 1)
    def _():
        zeros[...] = jnp.zeros_like(zeros)

        def zero_copy(a, g, size):
            return pltpu.make_async_copy(zeros.at[pl.ds(a, size)], xs_hbm.at[pl.ds(g, size)], sem_z)

        def zbody(e, ztot):
            _segment_copies(zero_copy, 0, tail_s[e], tail_s[N_EXP + e] // 8)
            return ztot + tail_s[N_EXP + e]

        def tbody(t, carry):
            pltpu.make_async_copy(zeros, xs_hbm.at[pl.ds(pl.multiple_of(t * TM, TM), TM)], sem_z).start()
            return carry

        n_used = tail_s[2 * N_EXP]
        lax.fori_loop(n_used, N_TILES, tbody, 0)
        ztot = lax.fori_loop(0, N_EXP, zbody, 0) + (N_TILES - n_used) * TM
        ztot = pl.multiple_of(ztot, 8)

        @pl.when(ztot > 0)
        def _():
            pltpu.make_async_copy(xs_hbm.at[pl.ds(0, ztot)], xs_hbm.at[pl.ds(0, ztot)], sem_z).wait()

        wait_block(b - 1, 1 - slot)
        wait_block(b, slot)


def _dispatch(tables, h2, top_e, tri, row_iota):
    loff_s, seg_s, gbase_s, tail_s, loff_v = tables
    grid_spec = pltpu.PrefetchScalarGridSpec(
        num_scalar_prefetch=4,
        grid=(NBD,),
        in_specs=[pl.BlockSpec((TBD, D_MODEL), lambda i, *_: (i, 0)),
                  pl.BlockSpec((1, TOP_K, TBD), lambda i, *_: (i, 0, 0)),
                  pl.BlockSpec((1, N_EXP, 1), lambda i, *_: (i, 0, 0)),
                  pl.BlockSpec((TBD, TBD), lambda i, *_: (0, 0)),
                  pl.BlockSpec((SORT_CHUNK, TBD), lambda i, *_: (0, 0))],
        out_specs=pl.BlockSpec(memory_space=pl.ANY),
        scratch_shapes=[pltpu.VMEM((2, LROWS, D_MODEL // 2), U32),
                        pltpu.VMEM((TM, D_MODEL // 2), U32),
                        pltpu.SemaphoreType.DMA((2,)),
                        pltpu.SemaphoreType.DMA])
    return pl.pallas_call(
        _dispatch_kernel,
        out_shape=jax.ShapeDtypeStruct((N_TILES * TM, D_MODEL // 2), U32),
        grid_spec=grid_spec,
        compiler_params=_params(),
        name="dispatch",
    )(loff_s, seg_s, gbase_s, tail_s, h2, top_e, loff_v, tri, row_iota)


def _moe_kernel(te_ref, first_ref, par_ref, next_ref, nv_ref, nu_ref, xs_ref, bgu_ref, bd_ref, wgu_hbm, wd_hbm, o_ref,
                wgu_st, wd_st, wgu_bf, wd_bf, sem):
    i = pl.program_id(0)

    def fetch(e, s):
        return (pltpu.make_async_copy(wgu_hbm.at[e], wgu_st.at[s], sem.at[0, s]),
                pltpu.make_async_copy(wd_hbm.at[e], wd_st.at[s], sem.at[1, s]))

    @pl.when(i < nu_ref[0])
    def _():
        @pl.when(first_ref[i] == 1)
        def _():
            s = par_ref[i]

            @pl.when(i == 0)
            def _():
                for cp in fetch(te_ref[0], 0):
                    cp.start()

            for cp in fetch(te_ref[i], s):
                cp.wait()

            @pl.when(next_ref[i] >= 0)
            def _():
                for cp in fetch(next_ref[i], 1 - s):
                    cp.start()

            wgu_bf[...] = wgu_st[s].astype(BF16)
            wd_bf[...] = wd_st[s].astype(BF16)

        def gate_up(r):
            return _dot(_unpack_pairs(xs_ref[r, :]), wgu_bf[...]) + bgu_ref[0]

        def activation(gu):
            gate = jnp.minimum(gu[:, :D_FF], SWIGLU_LIMIT)
            up = jnp.clip(gu[:, D_FF:], -SWIGLU_LIMIT, SWIGLU_LIMIT)
            return ((up + 1.0) * gate * jax.nn.sigmoid(SWIGLU_ALPHA * gate)).astype(BF16)

        def down(r, act):
            out = _dot(act, wd_bf[...]) + bd_ref[0]
            o_ref[r, :] = _pack_pairs(out.astype(BF16).astype(F32))

        def run(sizes):
            starts = [sum(sizes[:j]) for j in range(len(sizes))]
            pieces = [slice(a, a + n) for a, n in zip(starts, sizes)]
            gu = [gate_up(r) for r in pieces]
            for r, g in zip(pieces, gu):
                down(r, activation(g))
            done = sum(sizes)
            if done < TM:
                o_ref[done:, :] = jnp.zeros((TM - done, D_MODEL // 2), U32)

        nv = nv_ref[i]
        for lo, hi, sizes in MOE_PIECES:
            pl.when(jnp.logical_and(nv > lo, nv <= hi))(functools.partial(run, sizes))

    @pl.when(i >= nu_ref[0])
    def _():
        o_ref[...] = jnp.zeros_like(o_ref)


def _moe(tile_tables, xs, w_gate_up, b_gate_up, w_down, b_down):
    nsp = len(tile_tables)
    row_tile = lambda i, *s: (jnp.minimum(i, s[nsp - 1][0] - 1), 0)
    grid_spec = pltpu.PrefetchScalarGridSpec(
        num_scalar_prefetch=nsp,
        grid=(N_TILES,),
        in_specs=[pl.BlockSpec((TM, D_MODEL // 2), row_tile),
                  pl.BlockSpec((1, 1, 2 * D_FF), lambda i, te, *_: (te[i], 0, 0)),
                  pl.BlockSpec((1, 1, D_MODEL), lambda i, te, *_: (te[i], 0, 0)),
                  pl.BlockSpec(memory_space=pl.ANY),
                  pl.BlockSpec(memory_space=pl.ANY)],
        out_specs=pl.BlockSpec((TM, D_MODEL // 2), lambda i, *_: (i, 0)),
        scratch_shapes=[pltpu.VMEM((2, D_MODEL, 2 * D_FF), F32),
                        pltpu.VMEM((2, D_FF, D_MODEL), F32),
                        pltpu.VMEM((D_MODEL, 2 * D_FF), BF16),
                        pltpu.VMEM((D_FF, D_MODEL), BF16),
                        pltpu.SemaphoreType.DMA((2, 2))])
    return pl.pallas_call(
        _moe_kernel,
        out_shape=jax.ShapeDtypeStruct((N_TILES * TM, D_MODEL // 2), U32),
        grid_spec=grid_spec,
        compiler_params=_params(),
        name="moe",
    )(*tile_tables, xs, b_gate_up.reshape(N_EXP, 1, 2 * D_FF), b_down.reshape(N_EXP, 1, D_MODEL),
      w_gate_up, w_down)


def _final_kernel(loff_s, seg_s, gbase_s, ys_hbm, x1_ref, te_ref, tw_ref, loffv_ref, tri_ref, iota_ref, mod_ref, fn_ref,
                  oc_ref, ol_ref, loc, sem):
    b = pl.program_id(0)
    slot = b % 2

    def start_block(blk, s):
        def in_copy(a, g, size):
            return pltpu.make_async_copy(ys_hbm.at[pl.ds(g, size)], loc.at[s, pl.ds(a, size)], sem.at[s])

        def body(e, carry):
            idx = blk * N_EXP + e
            _segment_copies(in_copy, loff_s[idx], gbase_s[idx], seg_s[idx] // 8)
            return carry

        lax.fori_loop(0, N_EXP, body, 0)

    @pl.when(b == 0)
    def _():
        loc[...] = jnp.zeros_like(loc)
        start_block(0, 0)

    @pl.when(b + 1 < NBD)
    def _():
        start_block(b + 1, 1 - slot)

    lrow = _local_rows(te_ref, loffv_ref, tri_ref)
    tw = tw_ref[0]
    rows8 = jnp.concatenate(lrow + [tw], axis=0)
    cols = jnp.concatenate([rows8, jnp.zeros((128 - 2 * TOP_K, TBD), F32)], axis=0).T

    n = _block_rows(loff_s, seg_s, b)
    pltpu.make_async_copy(ys_hbm.at[pl.ds(0, n)], loc.at[slot, pl.ds(0, n)], sem.at[slot]).wait()

    y = None
    wts = [cols[:, TOP_K + k:TOP_K + k + 1].astype(BF16) for k in range(TOP_K)]
    for r0 in range(0, LROWS, SORT_CHUNK):
        rel = _chunk_relative([cols[:, k:k + 1] for k in range(TOP_K)], r0)
        pw = jnp.zeros((TBD, SORT_CHUNK), BF16)
        for k in reversed(range(TOP_K)):
            pw = jnp.where(iota_ref[...] == rel[k], wts[k], pw)
        part = _dot(pw, _unpack_pairs(loc[slot, r0:r0 + SORT_CHUNK, :]))
        y = part if y is None else y + part

    row = jnp.where(b < NBD_CTX, N_LAT_SEQ, (b - NBD_CTX) // (LAT_LEN // TBD))
    gate2 = mod_ref[pl.ds(row, 1), pl.ds(5 * D_MODEL, D_MODEL)]
    out = _rms(x1_ref[...] + gate2 * y, fn_ref[...])

    @pl.when(b < NBD_CTX)
    def _():
        oc_ref[...] = out

    @pl.when(b >= NBD_CTX)
    def _():
        ol_ref[...] = out


def _final(tables, ys, x1, top_e, top_w, tri, col_iota, mod, final_norm):
    loff_s, seg_s, gbase_s, _, loff_v = tables
    grid_spec = pltpu.PrefetchScalarGridSpec(
        num_scalar_prefetch=3,
        grid=(NBD,),
        in_specs=[pl.BlockSpec(memory_space=pl.ANY),
                  pl.BlockSpec((TBD, D_MODEL), lambda i, *_: (i, 0)),
                  pl.BlockSpec((1, TOP_K, TBD), lambda i, *_: (i, 0, 0)),
                  pl.BlockSpec((1, TOP_K, TBD), lambda i, *_: (i, 0, 0)),
                  pl.BlockSpec((1, N_EXP, 1), lambda i, *_: (i, 0, 0)),
                  pl.BlockSpec((TBD, TBD), lambda i, *_: (0, 0)),
                  pl.BlockSpec((TBD, SORT_CHUNK), lambda i, *_: (0, 0)),
                  pl.BlockSpec((MOD_ROWS, 6 * D_MODEL), lambda i, *_: (0, 0)),
                  pl.BlockSpec((1, D_MODEL), lambda i, *_: (0, 0))],
        out_specs=(pl.BlockSpec((TBD, D_MODEL), lambda i, *_: (jnp.minimum(i, NBD_CTX - 1), 0)),
                   pl.BlockSpec((TBD, D_MODEL), lambda i, *_: (jnp.maximum(i - NBD_CTX, 0), 0))),
        scratch_shapes=[pltpu.VMEM((2, LROWS, D_MODEL // 2), U32),
                        pltpu.SemaphoreType.DMA((2,))])
    return pl.pallas_call(
        _final_kernel,
        out_shape=(jax.ShapeDtypeStruct((T_CTX, D_MODEL), F32),
                   jax.ShapeDtypeStruct((T_LAT, D_MODEL), F32)),
        grid_spec=grid_spec,
        compiler_params=_params(),
        name="final",
    )(loff_s, seg_s, gbase_s, ys, x1, top_e, top_w, loff_v, tri, col_iota, mod, final_norm)


def _dispatch_tables(hist):
    hist = hist.reshape(NBD, N_EXP)
    seg = ((hist + 7) // 8) * 8
    loff = jnp.cumsum(seg, axis=1) - seg
    rows_e = jnp.sum(seg, axis=0)
    region = ((rows_e + TM - 1) // TM) * TM
    region_end = jnp.cumsum(region)
    region_start = region_end - region
    gbase = region_start[None, :] + jnp.cumsum(seg, axis=0) - seg
    n_used = (region_end[-1] // TM).astype(I32)
    tail = jnp.concatenate([region_start + rows_e, region - rows_e, n_used.reshape(1)])
    start = jnp.arange(N_TILES, dtype=I32) * TM
    tile_e = jnp.sum((start[:, None] >= region_end[None, :]).astype(I32), axis=1)
    tile_e = jnp.minimum(tile_e, tile_e[jnp.maximum(n_used - 1, 0)])
    first = jnp.concatenate([jnp.ones((1,), I32), (tile_e[1:] != tile_e[:-1]).astype(I32)])
    parity = (jnp.cumsum(first) - 1) % 2
    later = jnp.where(tile_e[None, :] > tile_e[:, None], tile_e[None, :], N_EXP)
    nxt = jnp.min(later, axis=1)
    nxt = jnp.where(nxt == N_EXP, -1, nxt)
    flat = lambda a: a.reshape(-1).astype(I32)
    tables = (flat(loff), flat(seg), flat(gbase), flat(tail), loff.astype(F32).reshape(NBD, N_EXP, 1))
    mine = tile_e[:, None] == jnp.arange(N_EXP, dtype=I32)[None, :]
    data_end = jnp.sum(jnp.where(mine, (region_start + rows_e)[None, :], 0), axis=1)
    tile_nv = jnp.clip(data_end - start, 0, TM)
    tile_tables = (flat(tile_e), flat(first), flat(parity), flat(nxt), flat(tile_nv), n_used.reshape(1))
    return tables, tile_tables


def kernel(x_prompt, x_sample, state_hgrn, c, c_ctx, w_ada, b_ada, norm1, w_in, hgrn_lb, hgrn_norm, w_pool,
           pool_scale, w_branch_a, w_branch_b, w_out, norm2, w_router, b_router, w_gate_up, b_gate_up,
           w_down, b_down, final_norm):
    x_ctx = x_prompt.reshape(T_CTX, D_MODEL)
    x_lat = x_sample.reshape(T_LAT, D_MODEL)
    cc = jnp.zeros((MOD_ROWS, D_MODEL), F32).at[:N_LAT_SEQ].set(c).at[N_LAT_SEQ].set(c_ctx)
    mod = _ada(cc, w_ada[0], b_ada)

    w_in_bf = w_in[0].astype(BF16)
    w_gates = w_in_bf[:, HGRN_W:HGRN_W + GATE_W]
    w_rest = jnp.concatenate([w_in_bf[:, :HGRN_W], w_in_bf[:, HGRN_W + GATE_W:]], axis=1)
    gates, rest = _inproj(x_ctx, x_lat, mod, norm1, w_gates, w_rest)

    mall, masks = _hgrn_consts()
    o_f, o_b, new_state = _hgrn(gates, rest, hgrn_lb, state_hgrn[:, 0], mall, masks)

    a_pool, cnt_pool = _pool_consts()
    yb = _pool(rest, a_pool, cnt_pool, w_pool[0].astype(BF16), pool_scale)

    wr_t = w_router[0].T
    wr_hi = wr_t.astype(BF16)
    wr_lo = (wr_t - wr_hi.astype(F32)).astype(BF16)
    tri = jnp.asarray(np.triu(np.ones((TBD, TBD), np.float32), 1), BF16)
    x1, h2, top_e, top_w, hist = _merge(
        x_ctx, x_lat, o_f, o_b, rest, yb, mod, hgrn_norm, w_branch_a[0].astype(BF16), w_branch_b[0].astype(BF16),
        w_out[0].astype(BF16), norm2, wr_hi, wr_lo, b_router.reshape(N_EXP, 1))

    tables, tile_tables = _dispatch_tables(hist)
    row_iota = jnp.asarray(np.broadcast_to(np.arange(SORT_CHUNK, dtype=np.float32)[:, None], (SORT_CHUNK, TBD)), BF16)
    xs = _dispatch(tables, h2, top_e, tri, row_iota)
    ys = _moe(tile_tables, xs, w_gate_up[0], b_gate_up[0], w_down[0], b_down[0])
    y_ctx, y_lat = _final(tables, ys, x1, top_e, top_w, tri, row_iota.T, mod, final_norm.reshape(1, D_MODEL))
    y_prompt = y_ctx.reshape(N_CTX_SEQ, CTX_LEN, D_MODEL)
    y_sample = y_lat.reshape(N_LAT_SEQ, LAT_LEN, D_MODEL)
    return y_prompt, y_sample, new_state[:, None]
```

```python
import functools

import numpy as np
import jax
import jax.numpy as jnp
from jax import lax
from jax.experimental import pallas as pl
from jax.experimental.pallas import tpu as pltpu

F32 = jnp.float32
BF16 = jnp.bfloat16
I32 = jnp.int32
U32 = jnp.uint32

D_MODEL = 1024
N_CTX_SEQ, CTX_LEN = 32, 256
N_LAT_SEQ, LAT_LEN = 4, 2048
T_CTX = N_CTX_SEQ * CTX_LEN
T_LAT = N_LAT_SEQ * LAT_LEN
T_ALL = T_CTX + T_LAT
TB = 256
NB = T_ALL // TB
NB_CTX = T_CTX // TB
LAT_BLOCKS = LAT_LEN // TB
HEADS, HEAD_K, HEAD_V = 4, 128, 128
HGRN_W = HEADS * HEAD_V
POOL_WINDOWS = (2, 4, 8, 16)
POOL_G = 128
POOL_W = len(POOL_WINDOWS) * POOL_G
GRID_W = 64
GRID_H = LAT_LEN // GRID_W
IN_W = 5 * HGRN_W + POOL_W + 2 * D_MODEL
GATE_W = 2 * HGRN_W
REST_W = IN_W - GATE_W
N_EXP, TOP_K, D_FF = 32, 4, 1024
SWIGLU_LIMIT = 7.0
SWIGLU_ALPHA = 1.702
EPS = 1e-6
LOG2_E = 1.4426950408889634
CHUNK = 64
N_LEVELS = 6
EXP_ROWS = (N_LEVELS + 2) * CHUNK
MM_BLOCKS = (0, 4, 5, 6)
MM_ROWS = len(MM_BLOCKS) * CHUNK
COARSE_LEVELS = ((1, 32), (2, 16), (3, 8))
TM = 512
MOE_PIECES = ((0, 128, (128,)), (128, 256, (256,)), (256, 384, (256, 128)), (384, 512, (256, 256)))
TBD = 512
NBD = T_ALL // TBD
NBD_CTX = T_CTX // TBD
SORT_CHUNK = 256
LROWS = TBD * TOP_K + N_EXP * 8
SEG_SMALL_BITS = 3
N_TILES = -(-(T_ALL * TOP_K + NBD * N_EXP * 7 + N_EXP * (TM - 1)) // TM)
MOD_ROWS = 8
VMEM_LIMIT = 56 * 1024 * 1024


def _params(sem=("arbitrary",)):
    return pltpu.CompilerParams(dimension_semantics=sem, vmem_limit_bytes=VMEM_LIMIT)


def _dot(a, b):
    return jnp.dot(a, b, preferred_element_type=F32)


def _dot_nt(a, b):
    return lax.dot_general(a, b, (((1,), (1,)), ((), ())), preferred_element_type=F32)


def _dot_tn(a, b):
    return lax.dot_general(a, b, (((0,), (0,)), ((), ())), preferred_element_type=F32)


def _split2(x):
    hi = x.astype(BF16)
    lo = (x - hi.astype(F32)).astype(BF16)
    return hi, lo


def _mod_row(i):
    return jnp.where(i < NBD_CTX, N_LAT_SEQ, (i - NBD_CTX) // (LAT_LEN // TBD))


def _ada_kernel(c_ref, w_ref, b_ref, o_ref):
    c = c_ref[...]
    s = c * jax.nn.sigmoid(c)
    o_ref[...] = jnp.dot(s, w_ref[...], preferred_element_type=F32,
                         precision=lax.Precision.HIGHEST) + b_ref[...]


def _ada(cc, w_ada, b_ada):
    nblk = 1536
    return pl.pallas_call(
        _ada_kernel,
        out_shape=jax.ShapeDtypeStruct((MOD_ROWS, 6 * D_MODEL), F32),
        grid=(6 * D_MODEL // nblk,),
        in_specs=[pl.BlockSpec((MOD_ROWS, D_MODEL), lambda j: (0, 0)),
                  pl.BlockSpec((D_MODEL, nblk), lambda j: (0, j)),
                  pl.BlockSpec((1, nblk), lambda j: (0, j))],
        out_specs=pl.BlockSpec((MOD_ROWS, nblk), lambda j: (0, j)),
        compiler_params=_params(),
        name="ada",
    )(cc, w_ada, b_ada)


def _rms(x, g):
    ms = jnp.mean(x * x, axis=-1, keepdims=True)
    return x * lax.rsqrt(ms + EPS) * g


def _x_specs():
    return [pl.BlockSpec((TBD, D_MODEL), lambda i, *_: (jnp.minimum(i, NBD_CTX - 1), 0)),
            pl.BlockSpec((TBD, D_MODEL), lambda i, *_: (jnp.maximum(i - NBD_CTX, 0), 0))]


def _x_block(xc_ref, xl_ref):
    return jnp.where(pl.program_id(0) < NBD_CTX, xc_ref[...], xl_ref[...])


def _inproj_kernel(xc_ref, xl_ref, mod_ref, n1_ref, wg_ref, wr_ref, og_ref, or_ref):
    row = _mod_row(pl.program_id(0))
    shift = mod_ref[pl.ds(row, 1), pl.ds(0, D_MODEL)]
    scale = mod_ref[pl.ds(row, 1), pl.ds(D_MODEL, D_MODEL)]
    h = (_rms(_x_block(xc_ref, xl_ref), n1_ref[...]) * (1.0 + scale) + shift).astype(BF16)
    og_ref[...] = _dot(h, wg_ref[...])
    or_ref[...] = _dot(h, wr_ref[...]).astype(BF16)


def _inproj(x_ctx, x_lat, mod, norm1, w_gates_bf, w_rest_bf):
    return pl.pallas_call(
        _inproj_kernel,
        out_shape=(jax.ShapeDtypeStruct((T_ALL, GATE_W), F32),
                   jax.ShapeDtypeStruct((T_ALL, REST_W), BF16)),
        grid=(NBD,),
        in_specs=_x_specs() + [
                  pl.BlockSpec((MOD_ROWS, 6 * D_MODEL), lambda i: (0, 0)),
                  pl.BlockSpec((1, D_MODEL), lambda i: (0, 0)),
                  pl.BlockSpec((D_MODEL, GATE_W), lambda i: (0, 0)),
                  pl.BlockSpec((D_MODEL, REST_W), lambda i: (0, 0))],
        out_specs=(pl.BlockSpec((TBD, GATE_W), lambda i: (i, 0)),
                   pl.BlockSpec((TBD, REST_W), lambda i: (i, 0))),
        compiler_params=_params(),
        name="inproj",
    )(x_ctx, x_lat, mod, norm1, w_gates_bf, w_rest_bf)


def _hgrn_consts():
    c = CHUNK
    t = np.arange(c)[:, None]
    u = np.arange(c)[None, :]
    blocks = [u <= t]
    masks = [np.eye(c, dtype=bool)]
    h = c // 2
    while h >= 1:
        bi = t // h
        upper = (bi % 2) == 1
        e_up = (u >= bi * h) & (u <= t)
        e_lo = (u > t) & (u <= bi * h + h - 1)
        blocks.append(np.where(upper, e_up, e_lo))
        masks.append(((t // (2 * h)) == (u // (2 * h))) & (((t // h) % 2) == 1) & (((u // h) % 2) == 0))
        h //= 2
    blocks.append(u > t)
    m_f = np.stack(blocks).astype(np.float32)
    k_f = np.stack(masks).astype(np.float32)
    m_b = m_f[:, ::-1, ::-1]
    k_b = k_f[:, ::-1, ::-1]
    sel = list(MM_BLOCKS)
    m = np.stack([m_f[sel].reshape(MM_ROWS, c), m_b[sel].reshape(MM_ROWS, c)])
    m3 = np.concatenate([m, m, m], axis=2)
    return jnp.asarray(m3, BF16), jnp.asarray(np.stack([k_f, k_b]), F32)


def _hgrn_block(dirs, lb, mall_ref, mask_ref, st_ref, z_ref, k_ref, sc_ref, run_ref):
    c = CHUNK
    nchunk = TB // c
    units = [(d, h) for d in range(2) for h in range(HEADS)]
    sl = [slice(h * HEAD_K, (h + 1) * HEAD_K) for h in range(HEADS)]

    def rows(ci, d):
        r0 = ci * c if d == 0 else (nchunk - 1 - ci) * c
        return slice(r0, r0 + c)

    def exponents(ci):
        s = ci % 2
        for d in range(2):
            f = lb[d:d + 1] + (1.0 - lb[d:d + 1]) * jax.nn.sigmoid(dirs[d][1][rows(ci, d), :])
            k_ref[s, d] = 1.0 - f
            k_ref[s, 2 + d] = dirs[d][0][rows(ci, d), :].astype(F32)
            g = jnp.log(f) * LOG2_E
            g1 = g.astype(BF16)
            r1 = g - g1.astype(F32)
            g2 = r1.astype(BF16)
            g3 = (r1 - g2.astype(F32)).astype(BF16)
            gsplit = jnp.concatenate([g1, g2, g3], axis=0)
            ex = _dot(mall_ref[d], gsplit)
            run = ex[0:c]
            run_ref[d] = run
            z_ref[s, d, 0:c] = jnp.exp2(run)
            for j, blk in enumerate(MM_BLOCKS[1:]):
                z_ref[s, d, blk * c:(blk + 1) * c] = jnp.exp2(ex[(j + 1) * c:(j + 2) * c])
            for blk, h in COARSE_LEVELS:
                for base in range(0, c, 2 * h):
                    ref = run_ref[d, base + h - 1 + d:base + h + d, :]
                    if d == 0:
                        first, second = ref - run[base:base + h], run[base + h:base + 2 * h] - ref
                    else:
                        first, second = run[base:base + h] - ref, ref - run[base + h:base + 2 * h]
                    z_ref[s, d, blk * c + base:blk * c + base + h] = jnp.exp2(first)
                    z_ref[s, d, blk * c + base + h:blk * c + base + 2 * h] = jnp.exp2(second)
            end = run_ref[d, c - 1:c, :] if d == 0 else run_ref[d, 0:1, :]
            z_ref[s, d, (N_LEVELS + 1) * c:] = jnp.exp2(end - run)

    def q_of(ci, d, h):
        return k_ref[ci % 2, 2 + d, :, sl[h]]

    def v_of(ci, d, h):
        return dirs[d][2][rows(ci, d), sl[h]].astype(BF16)

    def qz(ci, d, h, blk):
        return (q_of(ci, d, h) * z_ref[ci % 2, d, blk * c:(blk + 1) * c, sl[h]]).astype(BF16)

    def kz(ci, d, h, blk):
        return (k_ref[ci % 2, d, :, sl[h]] * z_ref[ci % 2, d, blk * c:(blk + 1) * c, sl[h]]).astype(BF16)

    def levels(ci):
        for d, h in units:
            q = q_of(ci, d, h).astype(F32)
            k = k_ref[ci % 2, d, :, sl[h]]
            k_next = pltpu.roll(k, 1 if d == 0 else c - 1, 0)
            zq = q * z_ref[ci % 2, d, N_LEVELS * c:(N_LEVELS + 1) * c, sl[h]]
            diag = jnp.sum(q * k, axis=1, keepdims=True)
            near = jnp.sum(zq * k_next, axis=1, keepdims=True)
            sc_ref[d, h] = mask_ref[d, 0] * diag + mask_ref[d, N_LEVELS] * near
        for lev in range(N_LEVELS - 1):
            for d, h in units:
                sc_ref[d, h] += mask_ref[d, lev + 1] * _dot_nt(qz(ci, d, h, lev + 1), kz(ci, d, h, lev + 1))

    def tail(ci):
        for d, h in units:
            o = (_dot_nt(qz(ci, d, h, 0), st_ref[d, h].astype(BF16))
                 + _dot(sc_ref[d, h].astype(BF16), v_of(ci, d, h)))
            dirs[d][3][rows(ci, d), sl[h]] = o * (HEAD_K ** -0.5)
        for d, h in units:
            tot_row = c - 1 if d == 0 else 0
            decay = z_ref[ci % 2, d, tot_row:tot_row + 1, sl[h]]
            st_ref[d, h] = st_ref[d, h] * decay + _dot_tn(v_of(ci, d, h), kz(ci, d, h, N_LEVELS + 1))

    exponents(0)
    for ci in range(nchunk):
        levels(ci)
        if ci + 1 < nchunk:
            exponents(ci + 1)
        tail(ci)


def _hgrn_kernel(qf_ref, ff_ref, vf_ref, qb_ref, fb_ref, vb_ref, lbraw_ref, s0_ref, mall_ref, mask_ref,
                 of_ref, ob_ref, sout_hbm, st_ref, stage_ref, z_ref, k_ref, sc_ref, run_ref, sem):
    i = pl.program_id(0)
    j = (i - NB_CTX) % LAT_BLOCKS
    is_ctx = i < NB_CTX

    @pl.when(is_ctx)
    def _():
        st_ref[...] = jnp.zeros_like(st_ref)

    @pl.when(jnp.logical_and(jnp.logical_not(is_ctx), j == 0))
    def _():
        for d in range(2):
            for h in range(HEADS):
                st_ref[d, h] = s0_ref[0, d, h].T

    a0 = lbraw_ref[0]
    a1 = lbraw_ref[1]
    mx = jnp.maximum(a0, a1)
    e0 = jnp.exp(a0 - mx)
    e1 = jnp.exp(a1 - mx)
    lb = e0 / (e0 + e1)

    dirs = ((qf_ref, ff_ref, vf_ref, of_ref), (qb_ref, fb_ref, vb_ref, ob_ref))
    _hgrn_block(dirs, lb, mall_ref, mask_ref, st_ref, z_ref, k_ref, sc_ref, run_ref)

    @pl.when(is_ctx)
    def _():
        for d in range(2):
            for h in range(HEADS):
                stage_ref[d, h] = st_ref[d, h].T
        cp = pltpu.make_async_copy(stage_ref, sout_hbm.at[i], sem)
        cp.start()
        cp.wait()


def _bwd_block(i):
    j = (i - NB_CTX) % LAT_BLOCKS
    return jnp.where(i < NB_CTX, i, i - j + (LAT_BLOCKS - 1 - j))


def _hgrn(gates, rest, hgrn_lb, s0, mall, masks):
    nh = HGRN_W
    fwd = lambda col: pl.BlockSpec((TB, nh), lambda i: (i, col))
    bwd = lambda col: pl.BlockSpec((TB, nh), lambda i: (_bwd_block(i), col))
    lat_seq = lambda i: jnp.clip((i - NB_CTX) // LAT_BLOCKS, 0, N_LAT_SEQ - 1)
    return pl.pallas_call(
        _hgrn_kernel,
        out_shape=(jax.ShapeDtypeStruct((T_ALL, nh), F32),
                   jax.ShapeDtypeStruct((T_ALL, nh), F32),
                   jax.ShapeDtypeStruct((N_CTX_SEQ, 2, HEADS, HEAD_K, HEAD_V), F32)),
        grid=(NB,),
        in_specs=[fwd(0), fwd(0), fwd(1), bwd(0), bwd(1), bwd(1),
                  pl.BlockSpec((2, 2, nh), lambda i: (0, 0, 0)),
                  pl.BlockSpec((1, 2, HEADS, HEAD_K, HEAD_V), lambda i: (lat_seq(i), 0, 0, 0, 0)),
                  pl.BlockSpec((2, MM_ROWS, 3 * CHUNK), lambda i: (0, 0, 0)),
                  pl.BlockSpec((2, N_LEVELS + 1, CHUNK, CHUNK), lambda i: (0, 0, 0, 0))],
        out_specs=(pl.BlockSpec((TB, nh), lambda i: (i, 0)),
                   pl.BlockSpec((TB, nh), lambda i: (_bwd_block(i), 0)),
                   pl.BlockSpec(memory_space=pl.ANY)),
        scratch_shapes=[pltpu.VMEM((2, HEADS, HEAD_V, HEAD_K), F32),
                        pltpu.VMEM((2, HEADS, HEAD_K, HEAD_V), F32),
                        pltpu.VMEM((2, 2, EXP_ROWS, HGRN_W), F32),
                        pltpu.VMEM((2, 4, CHUNK, HGRN_W), F32),
                        pltpu.VMEM((2, HEADS, CHUNK, CHUNK), F32),
                        pltpu.VMEM((2, CHUNK, HGRN_W), F32),
                        pltpu.SemaphoreType.DMA],
        compiler_params=_params(),
        name="hgrn",
    )(rest, gates, rest, rest, gates, rest, hgrn_lb, s0, mall, masks)


def _window_bounds(n, w):
    pos = np.arange(n)
    lo = np.clip(pos - w // 2, 0, n - 1)
    hi = np.clip(pos - w // 2 + w - 1, 0, n - 1)
    return lo, hi


def _pool_consts():
    seq, img, cnt_seq, cnt_col = [], [], [], []
    for w in POOL_WINDOWS:
        lo, hi = _window_bounds(CTX_LEN, w)
        u = np.arange(CTX_LEN)[None, :]
        seq.append((u >= lo[:, None]) & (u <= hi[:, None]))
        cnt_seq.append(hi - lo + 1)
        lo, hi = _window_bounds(GRID_W, w)
        u = np.arange(GRID_W)[None, :]
        band = (u >= lo[:, None]) & (u <= hi[:, None])
        img.append(np.kron(np.eye(TB // GRID_W, dtype=bool), band))
        cnt_col.append(np.tile(hi - lo + 1, TB // GRID_W))
    a = np.stack([np.stack(seq), np.stack(img)]).astype(np.float32)
    cnt = np.stack([np.stack(cnt_seq), np.stack(cnt_col)]).astype(np.float32)
    cnt = np.broadcast_to(cnt[..., None], cnt.shape + (POOL_G,))
    return jnp.asarray(a, BF16), jnp.asarray(cnt, F32)


POOL_ROWS = LAT_LEN


def _pool_kernel(u_ref, a_ref, cnt_ref, wp_ref, ps_ref, o_ref, cp_ref):
    i = pl.program_id(0)
    nblk = POOL_ROWS // TB

    def finish(g, r0, nrows, pm):
        sl = slice(g * POOL_G, (g + 1) * POOL_G)
        d = pm - u_ref[pl.ds(r0, nrows), sl].astype(F32)
        y = _dot(d.astype(BF16), wp_ref[g]) * ps_ref[:, sl]
        o_ref[pl.ds(r0, nrows), sl] = y.astype(o_ref.dtype)

    def window_sum(kind, g, b):
        sl = slice(g * POOL_G, (g + 1) * POOL_G)
        return _dot(a_ref[kind, g], u_ref[pl.ds(b * TB, TB), sl]) / cnt_ref[kind, g]

    @pl.when(i < T_CTX // POOL_ROWS)
    def _():
        for g in range(len(POOL_WINDOWS)):
            for b in range(nblk):
                finish(g, b * TB, TB, window_sum(0, g, b))

    @pl.when(i >= T_CTX // POOL_ROWS)
    def _():
        for g, w in enumerate(POOL_WINDOWS):
            for b in range(nblk):
                cp_ref[pl.ds(b * TB, TB), :] = window_sum(1, g, b)
            lo, hi = _window_bounds(GRID_H, w)
            for r in range(GRID_H):
                acc = cp_ref[pl.ds(int(lo[r]) * GRID_W, GRID_W), :]
                for rr in range(int(lo[r]) + 1, int(hi[r]) + 1):
                    acc = acc + cp_ref[pl.ds(rr * GRID_W, GRID_W), :]
                finish(g, r * GRID_W, GRID_W, acc / float(hi[r] - lo[r] + 1))


def _pool(rest, a_pool, cnt_pool, w_pool_bf, pool_scale):
    col = 3
    return pl.pallas_call(
        _pool_kernel,
        out_shape=jax.ShapeDtypeStruct((T_ALL, POOL_W), BF16),
        grid=(T_ALL // POOL_ROWS,),
        in_specs=[pl.BlockSpec((POOL_ROWS, POOL_W), lambda i: (i, col)),
                  pl.BlockSpec((2, 4, TB, TB), lambda i: (0, 0, 0, 0)),
                  pl.BlockSpec((2, 4, TB, POOL_G), lambda i: (0, 0, 0, 0)),
                  pl.BlockSpec((4, POOL_G, POOL_G), lambda i: (0, 0, 0)),
                  pl.BlockSpec((1, POOL_W), lambda i: (0, 0))],
        out_specs=pl.BlockSpec((POOL_ROWS, POOL_W), lambda i: (i, 0)),
        scratch_shapes=[pltpu.VMEM((POOL_ROWS, POOL_G), F32)],
        compiler_params=_params(),
        name="pool",
    )(rest, a_pool, cnt_pool, w_pool_bf, pool_scale)


def _merge_kernel(xc_ref, xl_ref, of_ref, ob_ref, og_ref, yb_ref, ga_ref, gb_ref, mod_ref, hn_ref, wa_ref, wb_ref,
                  wo_ref, n2_ref, wrh_ref, wrl_ref, br_ref,
                  x1_ref, h2_ref, te_ref, tw_ref, hist_ref):
    row = _mod_row(pl.program_id(0))
    gate1 = mod_ref[pl.ds(row, 1), pl.ds(2 * D_MODEL, D_MODEL)]
    shift2 = mod_ref[pl.ds(row, 1), pl.ds(3 * D_MODEL, D_MODEL)]
    scale2 = mod_ref[pl.ds(row, 1), pl.ds(4 * D_MODEL, D_MODEL)]

    halves = [slice(j * (TBD // 2), (j + 1) * (TBD // 2)) for j in range(2)]
    is_ctx = pl.program_id(0) < NBD_CTX

    def head_out(r):
        o = of_ref[r, :] + ob_ref[r, :]
        og = og_ref[r, :].astype(F32)
        ya = jnp.concatenate(
            [_rms(o[:, h * HEAD_V:(h + 1) * HEAD_V], hn_ref[...]) for h in range(HEADS)], axis=1)
        return (ya * (og * jax.nn.sigmoid(og))).astype(BF16)

    ya = [head_out(r) for r in halves]
    pa = [_dot(ya[j], wa_ref[...]) for j in range(2)]
    pb = [_dot(yb_ref[r, :], wb_ref[...]) for r in halves]
    merged = [(jax.nn.sigmoid(ga_ref[r, :].astype(F32)) * pa[j]
               + jax.nn.sigmoid(gb_ref[r, :].astype(F32)) * pb[j]).astype(BF16) for j, r in enumerate(halves)]
    po = [_dot(merged[j], wo_ref[...]) for j in range(2)]
    hh, hl = [], []
    for j, r in enumerate(halves):
        x1 = jnp.where(is_ctx, xc_ref[r, :], xl_ref[r, :]) + gate1 * po[j]
        x1_ref[r, :] = x1
        hi, lo = _split2(_rms(x1, n2_ref[...]) * (1.0 + scale2) + shift2)
        h2_ref[r, :] = hi
        hh.append(hi)
        hl.append(lo)
    hh = jnp.concatenate(hh, axis=0)
    hl = jnp.concatenate(hl, axis=0)

    lt = _dot_nt(wrh_ref[...], hh) + _dot_nt(wrl_ref[...], hh) + _dot_nt(wrh_ref[...], hl) + br_ref[...]
    eidx = lax.broadcasted_iota(I32, (N_EXP, TBD), 0)
    vals, idxs, cnt = [], [], jnp.zeros((N_EXP, TBD), F32)
    for _ in range(TOP_K):
        m = jnp.max(lt, axis=0, keepdims=True)
        idx = jnp.min(jnp.where(lt == m, eidx, N_EXP), axis=0, keepdims=True)
        sel = eidx == idx
        vals.append(m)
        idxs.append(idx)
        cnt = cnt + sel.astype(F32)
        lt = jnp.where(sel, -jnp.inf, lt)
    ex = [jnp.exp(v - vals[0]) for v in vals]
    den = ex[0] + ex[1] + ex[2] + ex[3]
    tw_ref[0] = jnp.concatenate([e / den for e in ex], axis=0)
    te_ref[0] = jnp.concatenate(idxs, axis=0)
    hist_ref[0] = jnp.sum(cnt, axis=1, keepdims=True).astype(I32)


def _merge(x_ctx, x_lat, o_f, o_b, rest, yb, mod, hgrn_norm, wa_bf, wb_bf, wo_bf, norm2, wr_hi, wr_lo, b_router):
    full = lambda shape: pl.BlockSpec(shape, lambda i: (0,) * len(shape))
    return pl.pallas_call(
        _merge_kernel,
        out_shape=(jax.ShapeDtypeStruct((T_ALL, D_MODEL), F32),
                   jax.ShapeDtypeStruct((T_ALL, D_MODEL), BF16),
                   jax.ShapeDtypeStruct((NBD, TOP_K, TBD), I32),
                   jax.ShapeDtypeStruct((NBD, TOP_K, TBD), F32),
                   jax.ShapeDtypeStruct((NBD, N_EXP, 1), I32)),
        grid=(NBD,),
        in_specs=_x_specs() + [
                  pl.BlockSpec((TBD, HGRN_W), lambda i: (i, 0)),
                  pl.BlockSpec((TBD, HGRN_W), lambda i: (i, 0)),
                  pl.BlockSpec((TBD, HGRN_W), lambda i: (i, 2)),
                  pl.BlockSpec((TBD, POOL_W), lambda i: (i, 0)),
                  pl.BlockSpec((TBD, D_MODEL), lambda i: (i, 2)),
                  pl.BlockSpec((TBD, D_MODEL), lambda i: (i, 3)),
                  full((MOD_ROWS, 6 * D_MODEL)),
                  full((1, HEAD_V)),
                  full((HGRN_W, D_MODEL)),
                  full((POOL_W, D_MODEL)),
                  full((D_MODEL, D_MODEL)),
                  full((1, D_MODEL)),
                  full((N_EXP, D_MODEL)),
                  full((N_EXP, D_MODEL)),
                  full((N_EXP, 1))],
        out_specs=(pl.BlockSpec((TBD, D_MODEL), lambda i: (i, 0)),
                   pl.BlockSpec((TBD, D_MODEL), lambda i: (i, 0)),
                   pl.BlockSpec((1, TOP_K, TBD), lambda i: (i, 0, 0)),
                   pl.BlockSpec((1, TOP_K, TBD), lambda i: (i, 0, 0)),
                   pl.BlockSpec((1, N_EXP, 1), lambda i: (i, 0, 0))),
        compiler_params=_params(),
        name="merge",
    )(x_ctx, x_lat, o_f, o_b, rest, yb, rest, rest, mod, hgrn_norm, wa_bf, wb_bf, wo_bf, norm2,
      wr_hi, wr_lo, b_router)


def _local_rows(te_ref, loff_ref, tri_ref):
    te = te_ref[0]
    eidx = lax.broadcasted_iota(I32, (N_EXP, TBD), 0)
    sels = [eidx == te[k:k + 1] for k in range(TOP_K)]
    cnt = sels[0].astype(F32)
    for s in sels[1:]:
        cnt = cnt + s.astype(F32)
    base = _dot(cnt.astype(BF16), tri_ref[...]) + loff_ref[0]
    return [jnp.sum(jnp.where(s, base, 0.0), axis=0, keepdims=True) for s in sels]


def _chunk_relative(rows, r0):
    out = []
    for r in rows:
        inside = jnp.logical_and(r >= r0, r < r0 + SORT_CHUNK)
        out.append(jnp.where(inside, r - r0, -1.0).astype(BF16))
    return out


def _segment_copies(make_copy, local_off, global_off, units):
    big = units >> SEG_SMALL_BITS

    def piece(p, carry):
        off = pl.multiple_of(p * (8 << SEG_SMALL_BITS), 8 << SEG_SMALL_BITS)
        make_copy(pl.multiple_of(local_off + off, 8), pl.multiple_of(global_off + off, 8),
                  8 << SEG_SMALL_BITS).start()
        return carry

    lax.fori_loop(0, big, piece, 0)
    done = big << (SEG_SMALL_BITS + 3)
    for j in reversed(range(SEG_SMALL_BITS)):
        low = done + ((units >> (j + 1)) & ((1 << (SEG_SMALL_BITS - 1 - j)) - 1)) * (16 << j)

        @pl.when(((units >> j) & 1) == 1)
        def _():
            make_copy(pl.multiple_of(local_off + low, 8), pl.multiple_of(global_off + low, 8), 8 << j).start()


def _pack_pairs(x):
    half = D_MODEL // 2
    lo = lax.bitcast_convert_type(x[:, :half], U32) >> 16
    hi = lax.bitcast_convert_type(x[:, half:], U32) & jnp.uint32(0xFFFF0000)
    return hi | lo


def _unpack_pairs(p):
    lo = lax.bitcast_convert_type(p << 16, F32).astype(BF16)
    hi = lax.bitcast_convert_type(p & jnp.uint32(0xFFFF0000), F32).astype(BF16)
    return jnp.concatenate([lo, hi], axis=1)


def _block_rows(loff_s, seg_s, b):
    last = b * N_EXP + N_EXP - 1
    return pl.multiple_of(loff_s[last] + seg_s[last], 8)


def _dispatch_kernel(loff_s, seg_s, gbase_s, tail_s, h2_ref, te_ref, loffv_ref, tri_ref, iota_ref, xs_hbm,
                     loc, zeros, sem, sem_z):
    b = pl.program_id(0)
    slot = b % 2

    def wait_block(blk, s):
        n = _block_rows(loff_s, seg_s, blk)
        pltpu.make_async_copy(loc.at[s, pl.ds(0, n)], xs_hbm.at[pl.ds(0, n)], sem.at[s]).wait()

    lrow = _local_rows(te_ref, loffv_ref, tri_ref)

    @pl.when(b >= 2)
    def _():
        wait_block(b - 2, slot)

    for r0 in range(0, LROWS, SORT_CHUNK):
        rel = _chunk_relative(lrow, r0)
        p = jnp.zeros((SORT_CHUNK, TBD), BF16)
        for k in reversed(range(TOP_K)):
            p = jnp.where(iota_ref[...] == rel[k], jnp.ones_like(p), p)
        loc[slot, r0:r0 + SORT_CHUNK, :] = _pack_pairs(_dot(p, h2_ref[...]))

    def out_copy(a, g, size):
        return pltpu.make_async_copy(loc.at[slot, pl.ds(a, size)], xs_hbm.at[pl.ds(g, size)], sem.at[slot])

    def body(e, carry):
        idx = b * N_EXP + e
        _segment_copies(out_copy, loff_s[idx], gbase_s[idx], seg_s[idx] // 8)
        return carry

    lax.fori_loop(0, N_EXP, body, 0)

    @pl.when(b == NBD - 1)
    def _():
        zeros[...] = jnp.zeros_like(zeros)

        def zero_copy(a, g, size):
            return pltpu.make_async_copy(zeros.at[pl.ds(a, size)], xs_hbm.at[pl.ds(g, size)], sem_z)

        def zbody(e, ztot):
            _segment_copies(zero_copy, 0, tail_s[e], tail_s[N_EXP + e] // 8)
            return ztot + tail_s[N_EXP + e]

        def tbody(t, carry):
            pltpu.make_async_copy(zeros, xs_hbm.at[pl.ds(pl.multiple_of(t * TM, TM), TM)], sem_z).start()
            return carry

        n_used = tail_s[2 * N_EXP]
        lax.fori_loop(n_used, N_TILES, tbody, 0)
        ztot = lax.fori_loop(0, N_EXP, zbody, 0) + (N_TILES - n_used) * TM
        ztot = pl.multiple_of(ztot, 8)

        @pl.when(ztot > 0)
        def _():
            pltpu.make_async_copy(xs_hbm.at[pl.ds(0, ztot)], xs_hbm.at[pl.ds(0, ztot)], sem_z).wait()

        wait_block(b - 1, 1 - slot)
        wait_block(b, slot)


def _dispatch(tables, h2, top_e, tri, row_iota):
    loff_s, seg_s, gbase_s, tail_s, loff_v = tables
    grid_spec = pltpu.PrefetchScalarGridSpec(
        num_scalar_prefetch=4,
        grid=(NBD,),
        in_specs=[pl.BlockSpec((TBD, D_MODEL), lambda i, *_: (i, 0)),
                  pl.BlockSpec((1, TOP_K, TBD), lambda i, *_: (i, 0, 0)),
                  pl.BlockSpec((1, N_EXP, 1), lambda i, *_: (i, 0, 0)),
                  pl.BlockSpec((TBD, TBD), lambda i, *_: (0, 0)),
                  pl.BlockSpec((SORT_CHUNK, TBD), lambda i, *_: (0, 0))],
        out_specs=pl.BlockSpec(memory_space=pl.ANY),
        scratch_shapes=[pltpu.VMEM((2, LROWS, D_MODEL // 2), U32),
                        pltpu.VMEM((TM, D_MODEL // 2), U32),
                        pltpu.SemaphoreType.DMA((2,)),
                        pltpu.SemaphoreType.DMA])
    return pl.pallas_call(
        _dispatch_kernel,
        out_shape=jax.ShapeDtypeStruct((N_TILES * TM, D_MODEL // 2), U32),
        grid_spec=grid_spec,
        compiler_params=_params(),
        name="dispatch",
    )(loff_s, seg_s, gbase_s, tail_s, h2, top_e, loff_v, tri, row_iota)


def _moe_kernel(te_ref, first_ref, par_ref, next_ref, nv_ref, nu_ref, xs_ref, bgu_ref, bd_ref, wgu_hbm, wd_hbm, o_ref,
                wgu_st, wd_st, wgu_bf, wd_bf, sem):
    i = pl.program_id(0)

    def fetch(e, s):
        return (pltpu.make_async_copy(wgu_hbm.at[e], wgu_st.at[s], sem.at[0, s]),
                pltpu.make_async_copy(wd_hbm.at[e], wd_st.at[s], sem.at[1, s]))

    @pl.when(i < nu_ref[0])
    def _():
        @pl.when(first_ref[i] == 1)
        def _():
            s = par_ref[i]

            @pl.when(i == 0)
            def _():
                for cp in fetch(te_ref[0], 0):
                    cp.start()

            for cp in fetch(te_ref[i], s):
                cp.wait()

            @pl.when(next_ref[i] >= 0)
            def _():
                for cp in fetch(next_ref[i], 1 - s):
                    cp.start()

            wgu_bf[...] = wgu_st[s].astype(BF16)
            wd_bf[...] = wd_st[s].astype(BF16)

        def gate_up(r):
            return _dot(_unpack_pairs(xs_ref[r, :]), wgu_bf[...]) + bgu_ref[0]

        def activation(gu):
            gate = jnp.minimum(gu[:, :D_FF], SWIGLU_LIMIT)
            up = jnp.clip(gu[:, D_FF:], -SWIGLU_LIMIT, SWIGLU_LIMIT)
            return ((up + 1.0) * gate * jax.nn.sigmoid(SWIGLU_ALPHA * gate)).astype(BF16)

        def down(r, act):
            out = _dot(act, wd_bf[...]) + bd_ref[0]
            o_ref[r, :] = _pack_pairs(out.astype(BF16).astype(F32))

        def run(sizes):
            starts = [sum(sizes[:j]) for j in range(len(sizes))]
            pieces = [slice(a, a + n) for a, n in zip(starts, sizes)]
            gu = [gate_up(r) for r in pieces]
            for r, g in zip(pieces, gu):
                down(r, activation(g))
            done = sum(sizes)
            if done < TM:
                o_ref[done:, :] = jnp.zeros((TM - done, D_MODEL // 2), U32)

        nv = nv_ref[i]
        for lo, hi, sizes in MOE_PIECES:
            pl.when(jnp.logical_and(nv > lo, nv <= hi))(functools.partial(run, sizes))

    @pl.when(i >= nu_ref[0])
    def _():
        o_ref[...] = jnp.zeros_like(o_ref)


def _moe(tile_tables, xs, w_gate_up, b_gate_up, w_down, b_down):
    nsp = len(tile_tables)
    row_tile = lambda i, *s: (jnp.minimum(i, s[nsp - 1][0] - 1), 0)
    grid_spec = pltpu.PrefetchScalarGridSpec(
        num_scalar_prefetch=nsp,
        grid=(N_TILES,),
        in_specs=[pl.BlockSpec((TM, D_MODEL // 2), row_tile),
                  pl.BlockSpec((1, 1, 2 * D_FF), lambda i, te, *_: (te[i], 0, 0)),
                  pl.BlockSpec((1, 1, D_MODEL), lambda i, te, *_: (te[i], 0, 0)),
                  pl.BlockSpec(memory_space=pl.ANY),
                  pl.BlockSpec(memory_space=pl.ANY)],
        out_specs=pl.BlockSpec((TM, D_MODEL // 2), lambda i, *_: (i, 0)),
        scratch_shapes=[pltpu.VMEM((2, D_MODEL, 2 * D_FF), F32),
                        pltpu.VMEM((2, D_FF, D_MODEL), F32),
                        pltpu.VMEM((D_MODEL, 2 * D_FF), BF16),
                        pltpu.VMEM((D_FF, D_MODEL), BF16),
                        pltpu.SemaphoreType.DMA((2, 2))])
    return pl.pallas_call(
        _moe_kernel,
        out_shape=jax.ShapeDtypeStruct((N_TILES * TM, D_MODEL // 2), U32),
        grid_spec=grid_spec,
        compiler_params=_params(),
        name="moe",
    )(*tile_tables, xs, b_gate_up.reshape(N_EXP, 1, 2 * D_FF), b_down.reshape(N_EXP, 1, D_MODEL),
      w_gate_up, w_down)


def _final_kernel(loff_s, seg_s, gbase_s, ys_hbm, x1_ref, te_ref, tw_ref, loffv_ref, tri_ref, iota_ref, mod_ref, fn_ref,
                  oc_ref, ol_ref, loc, sem):
    b = pl.program_id(0)
    slot = b % 2

    def start_block(blk, s):
        def in_copy(a, g, size):
            return pltpu.make_async_copy(ys_hbm.at[pl.ds(g, size)], loc.at[s, pl.ds(a, size)], sem.at[s])

        def body(e, carry):
            idx = blk * N_EXP + e
            _segment_copies(in_copy, loff_s[idx], gbase_s[idx], seg_s[idx] // 8)
            return carry

        lax.fori_loop(0, N_EXP, body, 0)

    @pl.when(b == 0)
    def _():
        loc[...] = jnp.zeros_like(loc)
        start_block(0, 0)

    @pl.when(b + 1 < NBD)
    def _():
        start_block(b + 1, 1 - slot)

    lrow = _local_rows(te_ref, loffv_ref, tri_ref)
    tw = tw_ref[0]
    wts = [tw[k:k + 1].astype(BF16) for k in range(TOP_K)]

    n = _block_rows(loff_s, seg_s, b)
    pltpu.make_async_copy(ys_hbm.at[pl.ds(0, n)], loc.at[slot, pl.ds(0, n)], sem.at[slot]).wait()

    y = None
    for r0 in range(0, LROWS, SORT_CHUNK):
        rel = _chunk_relative(lrow, r0)
        pw = jnp.zeros((SORT_CHUNK, TBD), BF16)
        for k in reversed(range(TOP_K)):
            pw = jnp.where(iota_ref[...] == rel[k], wts[k], pw)
        part = _dot_tn(pw, _unpack_pairs(loc[slot, r0:r0 + SORT_CHUNK, :]))
        y = part if y is None else y + part

    row = jnp.where(b < NBD_CTX, N_LAT_SEQ, (b - NBD_CTX) // (LAT_LEN // TBD))
    gate2 = mod_ref[pl.ds(row, 1), pl.ds(5 * D_MODEL, D_MODEL)]
    out = _rms(x1_ref[...] + gate2 * y, fn_ref[...])

    @pl.when(b < NBD_CTX)
    def _():
        oc_ref[...] = out

    @pl.when(b >= NBD_CTX)
    def _():
        ol_ref[...] = out


def _final(tables, ys, x1, top_e, top_w, tri, col_iota, mod, final_norm):
    loff_s, seg_s, gbase_s, _, loff_v = tables
    grid_spec = pltpu.PrefetchScalarGridSpec(
        num_scalar_prefetch=3,
        grid=(NBD,),
        in_specs=[pl.BlockSpec(memory_space=pl.ANY),
                  pl.BlockSpec((TBD, D_MODEL), lambda i, *_: (i, 0)),
                  pl.BlockSpec((1, TOP_K, TBD), lambda i, *_: (i, 0, 0)),
                  pl.BlockSpec((1, TOP_K, TBD), lambda i, *_: (i, 0, 0)),
                  pl.BlockSpec((1, N_EXP, 1), lambda i, *_: (i, 0, 0)),
                  pl.BlockSpec((TBD, TBD), lambda i, *_: (0, 0)),
                  pl.BlockSpec((SORT_CHUNK, TBD), lambda i, *_: (0, 0)),
                  pl.BlockSpec((MOD_ROWS, 6 * D_MODEL), lambda i, *_: (0, 0)),
                  pl.BlockSpec((1, D_MODEL), lambda i, *_: (0, 0))],
        out_specs=(pl.BlockSpec((TBD, D_MODEL), lambda i, *_: (jnp.minimum(i, NBD_CTX - 1), 0)),
                   pl.BlockSpec((TBD, D_MODEL), lambda i, *_: (jnp.maximum(i - NBD_CTX, 0), 0))),
        scratch_shapes=[pltpu.VMEM((2, LROWS, D_MODEL // 2), U32),
                        pltpu.SemaphoreType.DMA((2,))])
    return pl.pallas_call(
        _final_kernel,
        out_shape=(jax.ShapeDtypeStruct((T_CTX, D_MODEL), F32),
                   jax.ShapeDtypeStruct((T_LAT, D_MODEL), F32)),
        grid_spec=grid_spec,
        compiler_params=_params(),
        name="final",
    )(loff_s, seg_s, gbase_s, ys, x1, top_e, top_w, loff_v, tri, col_iota, mod, final_norm)


def _dispatch_tables(hist):
    hist = hist.reshape(NBD, N_EXP)
    seg = ((hist + 7) // 8) * 8
    loff = jnp.cumsum(seg, axis=1) - seg
    rows_e = jnp.sum(seg, axis=0)
    region = ((rows_e + TM - 1) // TM) * TM
    region_end = jnp.cumsum(region)
    region_start = region_end - region
    gbase = region_start[None, :] + jnp.cumsum(seg, axis=0) - seg
    n_used = (region_end[-1] // TM).astype(I32)
    tail = jnp.concatenate([region_start + rows_e, region - rows_e, n_used.reshape(1)])
    start = jnp.arange(N_TILES, dtype=I32) * TM
    tile_e = jnp.sum((start[:, None] >= region_end[None, :]).astype(I32), axis=1)
    tile_e = jnp.minimum(tile_e, tile_e[jnp.maximum(n_used - 1, 0)])
    first = jnp.concatenate([jnp.ones((1,), I32), (tile_e[1:] != tile_e[:-1]).astype(I32)])
    parity = (jnp.cumsum(first) - 1) % 2
    later = jnp.where(tile_e[None, :] > tile_e[:, None], tile_e[None, :], N_EXP)
    nxt = jnp.min(later, axis=1)
    nxt = jnp.where(nxt == N_EXP, -1, nxt)
    flat = lambda a: a.reshape(-1).astype(I32)
    tables = (flat(loff), flat(seg), flat(gbase), flat(tail), loff.astype(F32).reshape(NBD, N_EXP, 1))
    mine = tile_e[:, None] == jnp.arange(N_EXP, dtype=I32)[None, :]
    data_end = jnp.sum(jnp.where(mine, (region_start + rows_e)[None, :], 0), axis=1)
    tile_nv = jnp.clip(data_end - start, 0, TM)
    tile_tables = (flat(tile_e), flat(first), flat(parity), flat(nxt), flat(tile_nv), n_used.reshape(1))
    return tables, tile_tables


def kernel(x_prompt, x_sample, state_hgrn, c, c_ctx, w_ada, b_ada, norm1, w_in, hgrn_lb, hgrn_norm, w_pool,
           pool_scale, w_branch_a, w_branch_b, w_out, norm2, w_router, b_router, w_gate_up, b_gate_up,
           w_down, b_down, final_norm):
    x_ctx = x_prompt.reshape(T_CTX, D_MODEL)
    x_lat = x_sample.reshape(T_LAT, D_MODEL)
    cc = jnp.zeros((MOD_ROWS, D_MODEL), F32).at[:N_LAT_SEQ].set(c).at[N_LAT_SEQ].set(c_ctx)
    mod = _ada(cc, w_ada[0], b_ada)

    w_in_bf = w_in[0].astype(BF16)
    w_gates = w_in_bf[:, HGRN_W:HGRN_W + GATE_W]
    w_rest = jnp.concatenate([w_in_bf[:, :HGRN_W], w_in_bf[:, HGRN_W + GATE_W:]], axis=1)
    gates, rest = _inproj(x_ctx, x_lat, mod, norm1, w_gates, w_rest)

    mall, masks = _hgrn_consts()
    o_f, o_b, new_state = _hgrn(gates, rest, hgrn_lb, state_hgrn[:, 0], mall, masks)

    a_pool, cnt_pool = _pool_consts()
    yb = _pool(rest, a_pool, cnt_pool, w_pool[0].astype(BF16), pool_scale)

    wr_t = w_router[0].T
    wr_hi = wr_t.astype(BF16)
    wr_lo = (wr_t - wr_hi.astype(F32)).astype(BF16)
    tri = jnp.asarray(np.triu(np.ones((TBD, TBD), np.float32), 1), BF16)
    x1, h2, top_e, top_w, hist = _merge(
        x_ctx, x_lat, o_f, o_b, rest, yb, mod, hgrn_norm, w_branch_a[0].astype(BF16), w_branch_b[0].astype(BF16),
        w_out[0].astype(BF16), norm2, wr_hi, wr_lo, b_router.reshape(N_EXP, 1))

    tables, tile_tables = _dispatch_tables(hist)
    row_iota = jnp.asarray(np.broadcast_to(np.arange(SORT_CHUNK, dtype=np.float32)[:, None], (SORT_CHUNK, TBD)), BF16)
    xs = _dispatch(tables, h2, top_e, tri, row_iota)
    ys = _moe(tile_tables, xs, w_gate_up[0], b_gate_up[0], w_down[0], b_down[0])
    y_ctx, y_lat = _final(tables, ys, x1, top_e, top_w, tri, row_iota, mod, final_norm.reshape(1, D_MODEL))
    y_prompt = y_ctx.reshape(N_CTX_SEQ, CTX_LEN, D_MODEL)
    y_sample = y_lat.reshape(N_LAT_SEQ, LAT_LEN, D_MODEL)
    return y_prompt, y_sample, new_state[:, None]
```

```python
import functools

import numpy as np
import jax
import jax.numpy as jnp
from jax import lax
from jax.experimental import pallas as pl
from jax.experimental.pallas import tpu as pltpu

F32 = jnp.float32
BF16 = jnp.bfloat16
I32 = jnp.int32
U32 = jnp.uint32

D_MODEL = 1024
N_CTX_SEQ, CTX_LEN = 32, 256
N_LAT_SEQ, LAT_LEN = 4, 2048
T_CTX = N_CTX_SEQ * CTX_LEN
T_LAT = N_LAT_SEQ * LAT_LEN
T_ALL = T_CTX + T_LAT
SUBLANES, LANES = 8, 128
TB = 256
NB = T_ALL // TB
NB_CTX = T_CTX // TB
LAT_BLOCKS = LAT_LEN // TB
HEADS, HEAD_K, HEAD_V = 4, 128, 128
HGRN_W = HEADS * HEAD_V
POOL_WINDOWS = (2, 4, 8, 16)
POOL_G = 128
POOL_W = len(POOL_WINDOWS) * POOL_G
GRID_W = 64
GRID_H = LAT_LEN // GRID_W
IN_W = 5 * HGRN_W + POOL_W + 2 * D_MODEL
GATE_W = 2 * HGRN_W
REST_W = IN_W - GATE_W
N_EXP, TOP_K, D_FF = 32, 4, 1024
SWIGLU_LIMIT = 7.0
SWIGLU_ALPHA = 1.702
EPS = 1e-6
LOG2_E = 1.4426950408889634
CHUNK = 64
N_LEVELS = 6
EXP_ROWS = (N_LEVELS + 2) * CHUNK
MM_BLOCKS = (0, 4, 5, 6)
MM_ROWS = len(MM_BLOCKS) * CHUNK
COARSE_LEVELS = ((1, 32), (2, 16), (3, 8))
TM = 512
MOE_PIECES = ((0, 128, (128,)), (128, 256, (256,)), (256, 384, (256, 128)), (384, TM, (256, 256)))
TBD = 512
NBD = T_ALL // TBD
NBD_CTX = T_CTX // TBD
SORT_CHUNK = 256
LROWS = TBD * TOP_K + N_EXP * SUBLANES
SEG_SMALL_BITS = 3
N_TILES = -(-(T_ALL * TOP_K + NBD * N_EXP * (SUBLANES - 1) + N_EXP * (TM - 1)) // TM)
MOD_ROWS = 8
VMEM_LIMIT = 56 * 1024 * 1024


def _params(sem=("arbitrary",)):
    return pltpu.CompilerParams(dimension_semantics=sem, vmem_limit_bytes=VMEM_LIMIT)


def _dot(a, b):
    return jnp.dot(a, b, preferred_element_type=F32)


def _dot_nt(a, b):
    return lax.dot_general(a, b, (((1,), (1,)), ((), ())), preferred_element_type=F32)


def _dot_tn(a, b):
    return lax.dot_general(a, b, (((0,), (0,)), ((), ())), preferred_element_type=F32)


def _split2(x):
    hi = x.astype(BF16)
    lo = (x - hi.astype(F32)).astype(BF16)
    return hi, lo


def _mod_row(i):
    return jnp.where(i < NBD_CTX, N_LAT_SEQ, (i - NBD_CTX) // (LAT_LEN // TBD))


def _ada_kernel(c_ref, w_ref, b_ref, o_ref):
    c = c_ref[...]
    s = c * jax.nn.sigmoid(c)
    o_ref[...] = jnp.dot(s, w_ref[...], preferred_element_type=F32,
                         precision=lax.Precision.HIGHEST) + b_ref[...]


def _ada(cc, w_ada, b_ada):
    nblk = 1536
    return pl.pallas_call(
        _ada_kernel,
        out_shape=jax.ShapeDtypeStruct((MOD_ROWS, 6 * D_MODEL), F32),
        grid=(6 * D_MODEL // nblk,),
        in_specs=[pl.BlockSpec((MOD_ROWS, D_MODEL), lambda j: (0, 0)),
                  pl.BlockSpec((D_MODEL, nblk), lambda j: (0, j)),
                  pl.BlockSpec((1, nblk), lambda j: (0, j))],
        out_specs=pl.BlockSpec((MOD_ROWS, nblk), lambda j: (0, j)),
        compiler_params=_params(),
        name="ada",
    )(cc, w_ada, b_ada)


def _rms(x, g):
    ms = jnp.mean(x * x, axis=-1, keepdims=True)
    return x * lax.rsqrt(ms + EPS) * g


def _x_specs():
    return [pl.BlockSpec((TBD, D_MODEL), lambda i, *_: (jnp.minimum(i, NBD_CTX - 1), 0)),
            pl.BlockSpec((TBD, D_MODEL), lambda i, *_: (jnp.maximum(i - NBD_CTX, 0), 0))]


def _x_block(xc_ref, xl_ref):
    return jnp.where(pl.program_id(0) < NBD_CTX, xc_ref[...], xl_ref[...])


def _inproj_kernel(xc_ref, xl_ref, mod_ref, n1_ref, wg_ref, wr_ref, og_ref, or_ref):
    row = _mod_row(pl.program_id(0))
    shift = mod_ref[pl.ds(row, 1), pl.ds(0, D_MODEL)]
    scale = mod_ref[pl.ds(row, 1), pl.ds(D_MODEL, D_MODEL)]
    h = (_rms(_x_block(xc_ref, xl_ref), n1_ref[...]) * (1.0 + scale) + shift).astype(BF16)
    og_ref[...] = _dot(h, wg_ref[...])
    or_ref[...] = _dot(h, wr_ref[...]).astype(BF16)


def _inproj(x_ctx, x_lat, mod, norm1, w_gates_bf, w_rest_bf):
    return pl.pallas_call(
        _inproj_kernel,
        out_shape=(jax.ShapeDtypeStruct((T_ALL, GATE_W), F32),
                   jax.ShapeDtypeStruct((T_ALL, REST_W), BF16)),
        grid=(NBD,),
        in_specs=_x_specs() + [
                  pl.BlockSpec((MOD_ROWS, 6 * D_MODEL), lambda i: (0, 0)),
                  pl.BlockSpec((1, D_MODEL), lambda i: (0, 0)),
                  pl.BlockSpec((D_MODEL, GATE_W), lambda i: (0, 0)),
                  pl.BlockSpec((D_MODEL, REST_W), lambda i: (0, 0))],
        out_specs=(pl.BlockSpec((TBD, GATE_W), lambda i: (i, 0)),
                   pl.BlockSpec((TBD, REST_W), lambda i: (i, 0))),
        compiler_params=_params(),
        name="inproj",
    )(x_ctx, x_lat, mod, norm1, w_gates_bf, w_rest_bf)


def _hgrn_consts():
    c = CHUNK
    t = np.arange(c)[:, None]
    u = np.arange(c)[None, :]
    blocks = [u <= t]
    masks = [np.eye(c, dtype=bool)]
    h = c // 2
    while h >= 1:
        bi = t // h
        upper = (bi % 2) == 1
        e_up = (u >= bi * h) & (u <= t)
        e_lo = (u > t) & (u <= bi * h + h - 1)
        blocks.append(np.where(upper, e_up, e_lo))
        masks.append(((t // (2 * h)) == (u // (2 * h))) & (((t // h) % 2) == 1) & (((u // h) % 2) == 0))
        h //= 2
    blocks.append(u > t)
    m_f = np.stack(blocks).astype(np.float32)
    k_f = np.stack(masks).astype(np.float32)
    m_b = m_f[:, ::-1, ::-1]
    k_b = k_f[:, ::-1, ::-1]
    sel = list(MM_BLOCKS)
    m = np.stack([m_f[sel].reshape(MM_ROWS, c), m_b[sel].reshape(MM_ROWS, c)])
    m3 = np.concatenate([m, m, m], axis=2)
    return jnp.asarray(m3, BF16), jnp.asarray(np.stack([k_f, k_b]), F32)


def _hgrn_block(dirs, lb, mall_ref, mask_ref, st_ref, z_ref, k_ref, sc_ref, run_ref):
    c = CHUNK
    nchunk = TB // c
    units = [(d, h) for d in range(2) for h in range(HEADS)]
    sl = [slice(h * HEAD_K, (h + 1) * HEAD_K) for h in range(HEADS)]

    def rows(ci, d):
        r0 = ci * c if d == 0 else (nchunk - 1 - ci) * c
        return slice(r0, r0 + c)

    def exponents(ci):
        s = ci % 2
        for d in range(2):
            f = lb[d:d + 1] + (1.0 - lb[d:d + 1]) * jax.nn.sigmoid(dirs[d][1][rows(ci, d), :])
            k_ref[s, d] = 1.0 - f
            k_ref[s, 2 + d] = dirs[d][0][rows(ci, d), :].astype(F32)
            g = jnp.log(f) * LOG2_E
            g1 = g.astype(BF16)
            r1 = g - g1.astype(F32)
            g2 = r1.astype(BF16)
            g3 = (r1 - g2.astype(F32)).astype(BF16)
            gsplit = jnp.concatenate([g1, g2, g3], axis=0)
            ex = _dot(mall_ref[d], gsplit)
            run = ex[0:c]
            run_ref[d] = run
            z_ref[s, d, 0:c] = jnp.exp2(run)
            for j, blk in enumerate(MM_BLOCKS[1:]):
                z_ref[s, d, blk * c:(blk + 1) * c] = jnp.exp2(ex[(j + 1) * c:(j + 2) * c])
            for blk, h in COARSE_LEVELS:
                for base in range(0, c, 2 * h):
                    ref = run_ref[d, base + h - 1 + d:base + h + d, :]
                    if d == 0:
                        first, second = ref - run[base:base + h], run[base + h:base + 2 * h] - ref
                    else:
                        first, second = run[base:base + h] - ref, ref - run[base + h:base + 2 * h]
                    z_ref[s, d, blk * c + base:blk * c + base + h] = jnp.exp2(first)
                    z_ref[s, d, blk * c + base + h:blk * c + base + 2 * h] = jnp.exp2(second)
            end = run_ref[d, c - 1:c, :] if d == 0 else run_ref[d, 0:1, :]
            z_ref[s, d, (N_LEVELS + 1) * c:] = jnp.exp2(end - run)

    def q_of(ci, d, h):
        return k_ref[ci % 2, 2 + d, :, sl[h]]

    def v_of(ci, d, h):
        return dirs[d][2][rows(ci, d), sl[h]].astype(BF16)

    def qz(ci, d, h, blk):
        return (q_of(ci, d, h) * z_ref[ci % 2, d, blk * c:(blk + 1) * c, sl[h]]).astype(BF16)

    def kz(ci, d, h, blk):
        return (k_ref[ci % 2, d, :, sl[h]] * z_ref[ci % 2, d, blk * c:(blk + 1) * c, sl[h]]).astype(BF16)

    def levels(ci):
        for d, h in units:
            q = q_of(ci, d, h).astype(F32)
            k = k_ref[ci % 2, d, :, sl[h]]
            k_next = pltpu.roll(k, 1 if d == 0 else c - 1, 0)
            zq = q * z_ref[ci % 2, d, N_LEVELS * c:(N_LEVELS + 1) * c, sl[h]]
            diag = jnp.sum(q * k, axis=1, keepdims=True)
            near = jnp.sum(zq * k_next, axis=1, keepdims=True)
            sc_ref[d, h] = mask_ref[d, 0] * diag + mask_ref[d, N_LEVELS] * near
        for lev in range(N_LEVELS - 1):
            for d, h in units:
                sc_ref[d, h] += mask_ref[d, lev + 1] * _dot_nt(qz(ci, d, h, lev + 1), kz(ci, d, h, lev + 1))

    def tail(ci):
        for d, h in units:
            o = (_dot_nt(qz(ci, d, h, 0), st_ref[d, h].astype(BF16))
                 + _dot(sc_ref[d, h].astype(BF16), v_of(ci, d, h)))
            dirs[d][3][rows(ci, d), sl[h]] = o * (HEAD_K ** -0.5)
        for d, h in units:
            tot_row = c - 1 if d == 0 else 0
            decay = z_ref[ci % 2, d, tot_row:tot_row + 1, sl[h]]
            st_ref[d, h] = st_ref[d, h] * decay + _dot_tn(v_of(ci, d, h), kz(ci, d, h, N_LEVELS + 1))

    exponents(0)
    for ci in range(nchunk):
        levels(ci)
        if ci + 1 < nchunk:
            exponents(ci + 1)
        tail(ci)


def _hgrn_kernel(qf_ref, ff_ref, vf_ref, qb_ref, fb_ref, vb_ref, lbraw_ref, s0_ref, mall_ref, mask_ref,
                 of_ref, ob_ref, sout_hbm, st_ref, stage_ref, z_ref, k_ref, sc_ref, run_ref, sem):
    i = pl.program_id(0)
    j = (i - NB_CTX) % LAT_BLOCKS
    is_ctx = i < NB_CTX

    @pl.when(is_ctx)
    def _():
        st_ref[...] = jnp.zeros_like(st_ref)

    @pl.when(jnp.logical_and(jnp.logical_not(is_ctx), j == 0))
    def _():
        for d in range(2):
            for h in range(HEADS):
                st_ref[d, h] = s0_ref[0, d, h].T

    a0 = lbraw_ref[0]
    a1 = lbraw_ref[1]
    mx = jnp.maximum(a0, a1)
    e0 = jnp.exp(a0 - mx)
    e1 = jnp.exp(a1 - mx)
    lb = e0 / (e0 + e1)

    dirs = ((qf_ref, ff_ref, vf_ref, of_ref), (qb_ref, fb_ref, vb_ref, ob_ref))
    _hgrn_block(dirs, lb, mall_ref, mask_ref, st_ref, z_ref, k_ref, sc_ref, run_ref)

    @pl.when(is_ctx)
    def _():
        for d in range(2):
            for h in range(HEADS):
                stage_ref[d, h] = st_ref[d, h].T
        cp = pltpu.make_async_copy(stage_ref, sout_hbm.at[i], sem)
        cp.start()
        cp.wait()


def _bwd_block(i):
    j = (i - NB_CTX) % LAT_BLOCKS
    return jnp.where(i < NB_CTX, i, i - j + (LAT_BLOCKS - 1 - j))


def _hgrn(gates, rest, hgrn_lb, s0, mall, masks):
    nh = HGRN_W
    fwd = lambda col: pl.BlockSpec((TB, nh), lambda i: (i, col))
    bwd = lambda col: pl.BlockSpec((TB, nh), lambda i: (_bwd_block(i), col))
    lat_seq = lambda i: jnp.clip((i - NB_CTX) // LAT_BLOCKS, 0, N_LAT_SEQ - 1)
    return pl.pallas_call(
        _hgrn_kernel,
        out_shape=(jax.ShapeDtypeStruct((T_ALL, nh), F32),
                   jax.ShapeDtypeStruct((T_ALL, nh), F32),
                   jax.ShapeDtypeStruct((N_CTX_SEQ, 2, HEADS, HEAD_K, HEAD_V), F32)),
        grid=(NB,),
        in_specs=[fwd(0), fwd(0), fwd(1), bwd(0), bwd(1), bwd(1),
                  pl.BlockSpec((2, 2, nh), lambda i: (0, 0, 0)),
                  pl.BlockSpec((1, 2, HEADS, HEAD_K, HEAD_V), lambda i: (lat_seq(i), 0, 0, 0, 0)),
                  pl.BlockSpec((2, MM_ROWS, 3 * CHUNK), lambda i: (0, 0, 0)),
                  pl.BlockSpec((2, N_LEVELS + 1, CHUNK, CHUNK), lambda i: (0, 0, 0, 0))],
        out_specs=(pl.BlockSpec((TB, nh), lambda i: (i, 0)),
                   pl.BlockSpec((TB, nh), lambda i: (_bwd_block(i), 0)),
                   pl.BlockSpec(memory_space=pl.ANY)),
        scratch_shapes=[pltpu.VMEM((2, HEADS, HEAD_V, HEAD_K), F32),
                        pltpu.VMEM((2, HEADS, HEAD_K, HEAD_V), F32),
                        pltpu.VMEM((2, 2, EXP_ROWS, HGRN_W), F32),
                        pltpu.VMEM((2, 4, CHUNK, HGRN_W), F32),
                        pltpu.VMEM((2, HEADS, CHUNK, CHUNK), F32),
                        pltpu.VMEM((2, CHUNK, HGRN_W), F32),
                        pltpu.SemaphoreType.DMA],
        compiler_params=_params(),
        name="hgrn",
    )(rest, gates, rest, rest, gates, rest, hgrn_lb, s0, mall, masks)


def _window_bounds(n, w):
    pos = np.arange(n)
    lo = np.clip(pos - w // 2, 0, n - 1)
    hi = np.clip(pos - w // 2 + w - 1, 0, n - 1)
    return lo, hi


def _pool_consts():
    seq, img, cnt_seq, cnt_col = [], [], [], []
    for w in POOL_WINDOWS:
        lo, hi = _window_bounds(CTX_LEN, w)
        u = np.arange(CTX_LEN)[None, :]
        seq.append((u >= lo[:, None]) & (u <= hi[:, None]))
        cnt_seq.append(hi - lo + 1)
        lo, hi = _window_bounds(GRID_W, w)
        u = np.arange(GRID_W)[None, :]
        band = (u >= lo[:, None]) & (u <= hi[:, None])
        img.append(np.kron(np.eye(TB // GRID_W, dtype=bool), band))
        cnt_col.append(np.tile(hi - lo + 1, TB // GRID_W))
    a = np.stack([np.stack(seq), np.stack(img)]).astype(np.float32)
    cnt = np.stack([np.stack(cnt_seq), np.stack(cnt_col)]).astype(np.float32)
    cnt = np.broadcast_to(cnt[..., None], cnt.shape + (POOL_G,))
    return jnp.asarray(a, BF16), jnp.asarray(cnt, F32)


POOL_ROWS = LAT_LEN


def _pool_kernel(u_ref, a_ref, cnt_ref, wp_ref, ps_ref, o_ref, cp_ref):
    i = pl.program_id(0)
    nblk = POOL_ROWS // TB

    def finish(g, r0, nrows, pm):
        sl = slice(g * POOL_G, (g + 1) * POOL_G)
        d = pm - u_ref[pl.ds(r0, nrows), sl].astype(F32)
        y = _dot(d.astype(BF16), wp_ref[g]) * ps_ref[:, sl]
        o_ref[pl.ds(r0, nrows), sl] = y.astype(o_ref.dtype)

    def window_sum(kind, g, b):
        sl = slice(g * POOL_G, (g + 1) * POOL_G)
        return _dot(a_ref[kind, g], u_ref[pl.ds(b * TB, TB), sl]) / cnt_ref[kind, g]

    @pl.when(i < T_CTX // POOL_ROWS)
    def _():
        for g in range(len(POOL_WINDOWS)):
            for b in range(nblk):
                finish(g, b * TB, TB, window_sum(0, g, b))

    @pl.when(i >= T_CTX // POOL_ROWS)
    def _():
        for g, w in enumerate(POOL_WINDOWS):
            for b in range(nblk):
                cp_ref[pl.ds(b * TB, TB), :] = window_sum(1, g, b)
            lo, hi = _window_bounds(GRID_H, w)
            for r in range(GRID_H):
                acc = cp_ref[pl.ds(int(lo[r]) * GRID_W, GRID_W), :]
                for rr in range(int(lo[r]) + 1, int(hi[r]) + 1):
                    acc = acc + cp_ref[pl.ds(rr * GRID_W, GRID_W), :]
                finish(g, r * GRID_W, GRID_W, acc / float(hi[r] - lo[r] + 1))


def _pool(rest, a_pool, cnt_pool, w_pool_bf, pool_scale):
    col = 3
    return pl.pallas_call(
        _pool_kernel,
        out_shape=jax.ShapeDtypeStruct((T_ALL, POOL_W), BF16),
        grid=(T_ALL // POOL_ROWS,),
        in_specs=[pl.BlockSpec((POOL_ROWS, POOL_W), lambda i: (i, col)),
                  pl.BlockSpec((2, 4, TB, TB), lambda i: (0, 0, 0, 0)),
                  pl.BlockSpec((2, 4, TB, POOL_G), lambda i: (0, 0, 0, 0)),
                  pl.BlockSpec((4, POOL_G, POOL_G), lambda i: (0, 0, 0)),
                  pl.BlockSpec((1, POOL_W), lambda i: (0, 0))],
        out_specs=pl.BlockSpec((POOL_ROWS, POOL_W), lambda i: (i, 0)),
        scratch_shapes=[pltpu.VMEM((POOL_ROWS, POOL_G), F32)],
        compiler_params=_params(),
        name="pool",
    )(rest, a_pool, cnt_pool, w_pool_bf, pool_scale)


def _merge_kernel(xc_ref, xl_ref, of_ref, ob_ref, og_ref, yb_ref, ga_ref, gb_ref, mod_ref, hn_ref, wa_ref, wb_ref,
                  wo_ref, n2_ref, wrh_ref, wrl_ref, br_ref,
                  x1_ref, h2_ref, te_ref, tw_ref, hist_ref):
    row = _mod_row(pl.program_id(0))
    gate1 = mod_ref[pl.ds(row, 1), pl.ds(2 * D_MODEL, D_MODEL)]
    shift2 = mod_ref[pl.ds(row, 1), pl.ds(3 * D_MODEL, D_MODEL)]
    scale2 = mod_ref[pl.ds(row, 1), pl.ds(4 * D_MODEL, D_MODEL)]

    halves = [slice(j * (TBD // 2), (j + 1) * (TBD // 2)) for j in range(2)]
    is_ctx = pl.program_id(0) < NBD_CTX

    def head_out(r):
        o = of_ref[r, :] + ob_ref[r, :]
        og = og_ref[r, :].astype(F32)
        ya = jnp.concatenate(
            [_rms(o[:, h * HEAD_V:(h + 1) * HEAD_V], hn_ref[...]) for h in range(HEADS)], axis=1)
        return (ya * (og * jax.nn.sigmoid(og))).astype(BF16)

    ya = [head_out(r) for r in halves]
    pa = [_dot(ya[j], wa_ref[...]) for j in range(2)]
    pb = [_dot(yb_ref[r, :], wb_ref[...]) for r in halves]
    merged = [(jax.nn.sigmoid(ga_ref[r, :].astype(F32)) * pa[j]
               + jax.nn.sigmoid(gb_ref[r, :].astype(F32)) * pb[j]).astype(BF16) for j, r in enumerate(halves)]
    po = [_dot(merged[j], wo_ref[...]) for j in range(2)]
    hh, hl = [], []
    for j, r in enumerate(halves):
        x1 = jnp.where(is_ctx, xc_ref[r, :], xl_ref[r, :]) + gate1 * po[j]
        x1_ref[r, :] = x1
        hi, lo = _split2(_rms(x1, n2_ref[...]) * (1.0 + scale2) + shift2)
        h2_ref[r, :] = hi
        hh.append(hi)
        hl.append(lo)
    hh = jnp.concatenate(hh, axis=0)
    hl = jnp.concatenate(hl, axis=0)

    lt = _dot_nt(wrh_ref[...], hh) + _dot_nt(wrl_ref[...], hh) + _dot_nt(wrh_ref[...], hl) + br_ref[...]
    eidx = lax.broadcasted_iota(I32, (N_EXP, TBD), 0)
    vals, idxs, cnt = [], [], jnp.zeros((N_EXP, TBD), F32)
    for _ in range(TOP_K):
        m = jnp.max(lt, axis=0, keepdims=True)
        idx = jnp.min(jnp.where(lt == m, eidx, N_EXP), axis=0, keepdims=True)
        sel = eidx == idx
        vals.append(m)
        idxs.append(idx)
        cnt = cnt + sel.astype(F32)
        lt = jnp.where(sel, -jnp.inf, lt)
    ex = [jnp.exp(v - vals[0]) for v in vals]
    den = ex[0] + ex[1] + ex[2] + ex[3]
    tw_ref[0] = jnp.concatenate([e / den for e in ex], axis=0)
    te_ref[0] = jnp.concatenate(idxs, axis=0)
    hist_ref[0] = jnp.sum(cnt, axis=1, keepdims=True).astype(I32)


def _merge(x_ctx, x_lat, o_f, o_b, rest, yb, mod, hgrn_norm, wa_bf, wb_bf, wo_bf, norm2, wr_hi, wr_lo, b_router):
    full = lambda shape: pl.BlockSpec(shape, lambda i: (0,) * len(shape))
    return pl.pallas_call(
        _merge_kernel,
        out_shape=(jax.ShapeDtypeStruct((T_ALL, D_MODEL), F32),
                   jax.ShapeDtypeStruct((T_ALL, D_MODEL), BF16),
                   jax.ShapeDtypeStruct((NBD, TOP_K, TBD), I32),
                   jax.ShapeDtypeStruct((NBD, TOP_K, TBD), F32),
                   jax.ShapeDtypeStruct((NBD, N_EXP, 1), I32)),
        grid=(NBD,),
        in_specs=_x_specs() + [
                  pl.BlockSpec((TBD, HGRN_W), lambda i: (i, 0)),
                  pl.BlockSpec((TBD, HGRN_W), lambda i: (i, 0)),
                  pl.BlockSpec((TBD, HGRN_W), lambda i: (i, 2)),
                  pl.BlockSpec((TBD, POOL_W), lambda i: (i, 0)),
                  pl.BlockSpec((TBD, D_MODEL), lambda i: (i, 2)),
                  pl.BlockSpec((TBD, D_MODEL), lambda i: (i, 3)),
                  full((MOD_ROWS, 6 * D_MODEL)),
                  full((1, HEAD_V)),
                  full((HGRN_W, D_MODEL)),
                  full((POOL_W, D_MODEL)),
                  full((D_MODEL, D_MODEL)),
                  full((1, D_MODEL)),
                  full((N_EXP, D_MODEL)),
                  full((N_EXP, D_MODEL)),
                  full((N_EXP, 1))],
        out_specs=(pl.BlockSpec((TBD, D_MODEL), lambda i: (i, 0)),
                   pl.BlockSpec((TBD, D_MODEL), lambda i: (i, 0)),
                   pl.BlockSpec((1, TOP_K, TBD), lambda i: (i, 0, 0)),
                   pl.BlockSpec((1, TOP_K, TBD), lambda i: (i, 0, 0)),
                   pl.BlockSpec((1, N_EXP, 1), lambda i: (i, 0, 0))),
        compiler_params=_params(),
        name="merge",
    )(x_ctx, x_lat, o_f, o_b, rest, yb, rest, rest, mod, hgrn_norm, wa_bf, wb_bf, wo_bf, norm2,
      wr_hi, wr_lo, b_router)


def _local_rows(te_ref, loff_ref, tri_ref):
    te = te_ref[0]
    eidx = lax.broadcasted_iota(I32, (N_EXP, TBD), 0)
    sels = [eidx == te[k:k + 1] for k in range(TOP_K)]
    cnt = sels[0].astype(F32)
    for s in sels[1:]:
        cnt = cnt + s.astype(F32)
    base = _dot(cnt.astype(BF16), tri_ref[...]) + loff_ref[0]
    return [jnp.sum(jnp.where(s, base, 0.0), axis=0, keepdims=True) for s in sels]


def _chunk_relative(rows, r0):
    out = []
    for r in rows:
        inside = jnp.logical_and(r >= r0, r < r0 + SORT_CHUNK)
        out.append(jnp.where(inside, r - r0, -1.0).astype(BF16))
    return out


def _segment_copies(make_copy, local_off, global_off, units):
    big_rows = SUBLANES << SEG_SMALL_BITS
    big = units >> SEG_SMALL_BITS

    def piece(p, carry):
        off = pl.multiple_of(p * big_rows, big_rows)
        make_copy(pl.multiple_of(local_off + off, SUBLANES), pl.multiple_of(global_off + off, SUBLANES),
                  big_rows).start()
        return carry

    lax.fori_loop(0, big, piece, 0)
    done = big * big_rows
    for j in reversed(range(SEG_SMALL_BITS)):
        rows = SUBLANES << j
        low = done + ((units >> (j + 1)) & ((1 << (SEG_SMALL_BITS - 1 - j)) - 1)) * (2 * rows)

        @pl.when(((units >> j) & 1) == 1)
        def _():
            make_copy(pl.multiple_of(local_off + low, SUBLANES), pl.multiple_of(global_off + low, SUBLANES),
                      rows).start()


def _pack_pairs(x):
    half = D_MODEL // 2
    lo = lax.bitcast_convert_type(x[:, :half], U32) >> 16
    hi = lax.bitcast_convert_type(x[:, half:], U32) & jnp.uint32(0xFFFF0000)
    return hi | lo


def _unpack_pairs(p):
    lo = lax.bitcast_convert_type(p << 16, F32).astype(BF16)
    hi = lax.bitcast_convert_type(p & jnp.uint32(0xFFFF0000), F32).astype(BF16)
    return jnp.concatenate([lo, hi], axis=1)


def _block_rows(loff_s, seg_s, b):
    last = b * N_EXP + N_EXP - 1
    return pl.multiple_of(loff_s[last] + seg_s[last], SUBLANES)


def _dispatch_kernel(loff_s, seg_s, gbase_s, tail_s, h2_ref, te_ref, loffv_ref, tri_ref, iota_ref, xs_hbm,
                     loc, zeros, sem, sem_z):
    b = pl.program_id(0)
    slot = b % 2

    def wait_block(blk, s):
        n = _block_rows(loff_s, seg_s, blk)
        pltpu.make_async_copy(loc.at[s, pl.ds(0, n)], xs_hbm.at[pl.ds(0, n)], sem.at[s]).wait()

    lrow = _local_rows(te_ref, loffv_ref, tri_ref)

    @pl.when(b >= 2)
    def _():
        wait_block(b - 2, slot)

    for r0 in range(0, LROWS, SORT_CHUNK):
        rel = _chunk_relative(lrow, r0)
        p = jnp.zeros((SORT_CHUNK, TBD), BF16)
        for k in reversed(range(TOP_K)):
            p = jnp.where(iota_ref[...] == rel[k], jnp.ones_like(p), p)
        loc[slot, r0:r0 + SORT_CHUNK, :] = _pack_pairs(_dot(p, h2_ref[...]))

    def out_copy(a, g, size):
        return pltpu.make_async_copy(loc.at[slot, pl.ds(a, size)], xs_hbm.at[pl.ds(g, size)], sem.at[slot])

    def body(e, carry):
        idx = b * N_EXP + e
        _segment_copies(out_copy, loff_s[idx], gbase_s[idx], seg_s[idx] // SUBLANES)
        return carry

    lax.fori_loop(0, N_EXP, body, 0)

    @pl.when(b == NBD - 1)
    def _():
        zeros[...] = jnp.zeros_like(zeros)

        def zero_copy(a, g, size):
            return pltpu.make_async_copy(zeros.at[pl.ds(a, size)], xs_hbm.at[pl.ds(g, size)], sem_z)

        def zbody(e, ztot):
            _segment_copies(zero_copy, 0, tail_s[e], tail_s[N_EXP + e] // SUBLANES)
            return ztot + tail_s[N_EXP + e]

        def tbody(t, carry):
            pltpu.make_async_copy(zeros, xs_hbm.at[pl.ds(pl.multiple_of(t * TM, TM), TM)], sem_z).start()
            return carry

        n_used = tail_s[2 * N_EXP]
        lax.fori_loop(n_used, N_TILES, tbody, 0)
        ztot = lax.fori_loop(0, N_EXP, zbody, 0) + (N_TILES - n_used) * TM
        ztot = pl.multiple_of(ztot, SUBLANES)

        @pl.when(ztot > 0)
        def _():
            pltpu.make_async_copy(xs_hbm.at[pl.ds(0, ztot)], xs_hbm.at[pl.ds(0, ztot)], sem_z).wait()

        wait_block(b - 1, 1 - slot)
        wait_block(b, slot)


def _dispatch(tables, h2, top_e, tri, row_iota):
    loff_s, seg_s, gbase_s, tail_s, loff_v = tables
    grid_spec = pltpu.PrefetchScalarGridSpec(
        num_scalar_prefetch=4,
        grid=(NBD,),
        in_specs=[pl.BlockSpec((TBD, D_MODEL), lambda i, *_: (i, 0)),
                  pl.BlockSpec((1, TOP_K, TBD), lambda i, *_: (i, 0, 0)),
                  pl.BlockSpec((1, N_EXP, 1), lambda i, *_: (i, 0, 0)),
                  pl.BlockSpec((TBD, TBD), lambda i, *_: (0, 0)),
                  pl.BlockSpec((SORT_CHUNK, TBD), lambda i, *_: (0, 0))],
        out_specs=pl.BlockSpec(memory_space=pl.ANY),
        scratch_shapes=[pltpu.VMEM((2, LROWS, D_MODEL // 2), U32),
                        pltpu.VMEM((TM, D_MODEL // 2), U32),
                        pltpu.SemaphoreType.DMA((2,)),
                        pltpu.SemaphoreType.DMA])
    return pl.pallas_call(
        _dispatch_kernel,
        out_shape=jax.ShapeDtypeStruct((N_TILES * TM, D_MODEL // 2), U32),
        grid_spec=grid_spec,
        compiler_params=_params(),
        name="dispatch",
    )(loff_s, seg_s, gbase_s, tail_s, h2, top_e, loff_v, tri, row_iota)


def _moe_kernel(te_ref, first_ref, par_ref, next_ref, nv_ref, nu_ref, xs_ref, bgu_ref, bd_ref, wgu_hbm, wd_hbm, o_ref,
                wgu_st, wd_st, wgu_bf, wd_bf, sem):
    i = pl.program_id(0)

    def fetch(e, s):
        return (pltpu.make_async_copy(wgu_hbm.at[e], wgu_st.at[s], sem.at[0, s]),
                pltpu.make_async_copy(wd_hbm.at[e], wd_st.at[s], sem.at[1, s]))

    @pl.when(i < nu_ref[0])
    def _():
        @pl.when(first_ref[i] == 1)
        def _():
            s = par_ref[i]

            @pl.when(i == 0)
            def _():
                for cp in fetch(te_ref[0], 0):
                    cp.start()

            for cp in fetch(te_ref[i], s):
                cp.wait()

            @pl.when(next_ref[i] >= 0)
            def _():
                for cp in fetch(next_ref[i], 1 - s):
                    cp.start()

            wgu_bf[...] = wgu_st[s].astype(BF16)
            wd_bf[...] = wd_st[s].astype(BF16)

        def gate_up(r):
            return _dot(_unpack_pairs(xs_ref[r, :]), wgu_bf[...]) + bgu_ref[0]

        def activation(gu):
            gate = jnp.minimum(gu[:, :D_FF], SWIGLU_LIMIT)
            up = jnp.clip(gu[:, D_FF:], -SWIGLU_LIMIT, SWIGLU_LIMIT)
            return ((up + 1.0) * gate * jax.nn.sigmoid(SWIGLU_ALPHA * gate)).astype(BF16)

        def down(r, act):
            out = _dot(act, wd_bf[...]) + bd_ref[0]
            o_ref[r, :] = _pack_pairs(out.astype(BF16).astype(F32))

        def run(sizes):
            starts = [sum(sizes[:j]) for j in range(len(sizes))]
            pieces = [slice(a, a + n) for a, n in zip(starts, sizes)]
            gu = [gate_up(r) for r in pieces]
            for r, g in zip(pieces, gu):
                down(r, activation(g))
            done = sum(sizes)
            if done < TM:
                o_ref[done:, :] = jnp.zeros((TM - done, D_MODEL // 2), U32)

        nv = nv_ref[i]
        for lo, hi, sizes in MOE_PIECES:
            pl.when(jnp.logical_and(nv > lo, nv <= hi))(functools.partial(run, sizes))

    @pl.when(i >= nu_ref[0])
    def _():
        o_ref[...] = jnp.zeros_like(o_ref)


def _moe(tile_tables, xs, w_gate_up, b_gate_up, w_down, b_down):
    nsp = len(tile_tables)
    row_tile = lambda i, *s: (jnp.minimum(i, s[nsp - 1][0] - 1), 0)
    grid_spec = pltpu.PrefetchScalarGridSpec(
        num_scalar_prefetch=nsp,
        grid=(N_TILES,),
        in_specs=[pl.BlockSpec((TM, D_MODEL // 2), row_tile),
                  pl.BlockSpec((1, 1, 2 * D_FF), lambda i, te, *_: (te[i], 0, 0)),
                  pl.BlockSpec((1, 1, D_MODEL), lambda i, te, *_: (te[i], 0, 0)),
                  pl.BlockSpec(memory_space=pl.ANY),
                  pl.BlockSpec(memory_space=pl.ANY)],
        out_specs=pl.BlockSpec((TM, D_MODEL // 2), lambda i, *_: (i, 0)),
        scratch_shapes=[pltpu.VMEM((2, D_MODEL, 2 * D_FF), F32),
                        pltpu.VMEM((2, D_FF, D_MODEL), F32),
                        pltpu.VMEM((D_MODEL, 2 * D_FF), BF16),
                        pltpu.VMEM((D_FF, D_MODEL), BF16),
                        pltpu.SemaphoreType.DMA((2, 2))])
    return pl.pallas_call(
        _moe_kernel,
        out_shape=jax.ShapeDtypeStruct((N_TILES * TM, D_MODEL // 2), U32),
        grid_spec=grid_spec,
        compiler_params=_params(),
        name="moe",
    )(*tile_tables, xs, b_gate_up.reshape(N_EXP, 1, 2 * D_FF), b_down.reshape(N_EXP, 1, D_MODEL),
      w_gate_up, w_down)


def _final_kernel(loff_s, seg_s, gbase_s, ys_hbm, x1_ref, te_ref, tw_ref, loffv_ref, tri_ref, iota_ref, mod_ref, fn_ref,
                  oc_ref, ol_ref, loc, sem):
    b = pl.program_id(0)
    slot = b % 2

    def start_block(blk, s):
        def in_copy(a, g, size):
            return pltpu.make_async_copy(ys_hbm.at[pl.ds(g, size)], loc.at[s, pl.ds(a, size)], sem.at[s])

        def body(e, carry):
            idx = blk * N_EXP + e
            _segment_copies(in_copy, loff_s[idx], gbase_s[idx], seg_s[idx] // SUBLANES)
            return carry

        lax.fori_loop(0, N_EXP, body, 0)

    @pl.when(b == 0)
    def _():
        loc[...] = jnp.zeros_like(loc)
        start_block(0, 0)

    @pl.when(b + 1 < NBD)
    def _():
        start_block(b + 1, 1 - slot)

    lrow = _local_rows(te_ref, loffv_ref, tri_ref)
    tw = tw_ref[0]
    wts = [tw[k:k + 1].astype(BF16) for k in range(TOP_K)]

    n = _block_rows(loff_s, seg_s, b)
    pltpu.make_async_copy(ys_hbm.at[pl.ds(0, n)], loc.at[slot, pl.ds(0, n)], sem.at[slot]).wait()

    y = None
    for r0 in range(0, LROWS, SORT_CHUNK):
        rel = _chunk_relative(lrow, r0)
        pw = jnp.zeros((SORT_CHUNK, TBD), BF16)
        for k in reversed(range(TOP_K)):
            pw = jnp.where(iota_ref[...] == rel[k], wts[k], pw)
        part = _dot_tn(pw, _unpack_pairs(loc[slot, r0:r0 + SORT_CHUNK, :]))
        y = part if y is None else y + part

    row = _mod_row(b)
    gate2 = mod_ref[pl.ds(row, 1), pl.ds(5 * D_MODEL, D_MODEL)]
    out = _rms(x1_ref[...] + gate2 * y, fn_ref[...])

    @pl.when(b < NBD_CTX)
    def _():
        oc_ref[...] = out

    @pl.when(b >= NBD_CTX)
    def _():
        ol_ref[...] = out


def _final(tables, ys, x1, top_e, top_w, tri, col_iota, mod, final_norm):
    loff_s, seg_s, gbase_s, _, loff_v = tables
    grid_spec = pltpu.PrefetchScalarGridSpec(
        num_scalar_prefetch=3,
        grid=(NBD,),
        in_specs=[pl.BlockSpec(memory_space=pl.ANY),
                  pl.BlockSpec((TBD, D_MODEL), lambda i, *_: (i, 0)),
                  pl.BlockSpec((1, TOP_K, TBD), lambda i, *_: (i, 0, 0)),
                  pl.BlockSpec((1, TOP_K, TBD), lambda i, *_: (i, 0, 0)),
                  pl.BlockSpec((1, N_EXP, 1), lambda i, *_: (i, 0, 0)),
                  pl.BlockSpec((TBD, TBD), lambda i, *_: (0, 0)),
                  pl.BlockSpec((SORT_CHUNK, TBD), lambda i, *_: (0, 0)),
                  pl.BlockSpec((MOD_ROWS, 6 * D_MODEL), lambda i, *_: (0, 0)),
                  pl.BlockSpec((1, D_MODEL), lambda i, *_: (0, 0))],
        out_specs=(pl.BlockSpec((TBD, D_MODEL), lambda i, *_: (jnp.minimum(i, NBD_CTX - 1), 0)),
                   pl.BlockSpec((TBD, D_MODEL), lambda i, *_: (jnp.maximum(i - NBD_CTX, 0), 0))),
        scratch_shapes=[pltpu.VMEM((2, LROWS, D_MODEL // 2), U32),
                        pltpu.SemaphoreType.DMA((2,))])
    return pl.pallas_call(
        _final_kernel,
        out_shape=(jax.ShapeDtypeStruct((T_CTX, D_MODEL), F32),
                   jax.ShapeDtypeStruct((T_LAT, D_MODEL), F32)),
        grid_spec=grid_spec,
        compiler_params=_params(),
        name="final",
    )(loff_s, seg_s, gbase_s, ys, x1, top_e, top_w, loff_v, tri, col_iota, mod, final_norm)


def _dispatch_tables(hist):
    hist = hist.reshape(NBD, N_EXP)
    seg = ((hist + SUBLANES - 1) // SUBLANES) * SUBLANES
    loff = jnp.cumsum(seg, axis=1) - seg
    rows_e = jnp.sum(seg, axis=0)
    region = ((rows_e + TM - 1) // TM) * TM
    region_end = jnp.cumsum(region)
    region_start = region_end - region
    gbase = region_start[None, :] + jnp.cumsum(seg, axis=0) - seg
    n_used = (region_end[-1] // TM).astype(I32)
    tail = jnp.concatenate([region_start + rows_e, region - rows_e, n_used.reshape(1)])
    start = jnp.arange(N_TILES, dtype=I32) * TM
    tile_e = jnp.sum((start[:, None] >= region_end[None, :]).astype(I32), axis=1)
    tile_e = jnp.minimum(tile_e, tile_e[jnp.maximum(n_used - 1, 0)])
    first = jnp.concatenate([jnp.ones((1,), I32), (tile_e[1:] != tile_e[:-1]).astype(I32)])
    parity = (jnp.cumsum(first) - 1) % 2
    later = jnp.where(tile_e[None, :] > tile_e[:, None], tile_e[None, :], N_EXP)
    nxt = jnp.min(later, axis=1)
    nxt = jnp.where(nxt == N_EXP, -1, nxt)
    flat = lambda a: a.reshape(-1).astype(I32)
    tables = (flat(loff), flat(seg), flat(gbase), flat(tail), loff.astype(F32).reshape(NBD, N_EXP, 1))
    mine = tile_e[:, None] == jnp.arange(N_EXP, dtype=I32)[None, :]
    data_end = jnp.sum(jnp.where(mine, (region_start + rows_e)[None, :], 0), axis=1)
    tile_nv = jnp.clip(data_end - start, 0, TM)
    tile_tables = (flat(tile_e), flat(first), flat(parity), flat(nxt), flat(tile_nv), n_used.reshape(1))
    return tables, tile_tables


def kernel(x_prompt, x_sample, state_hgrn, c, c_ctx, w_ada, b_ada, norm1, w_in, hgrn_lb, hgrn_norm, w_pool,
           pool_scale, w_branch_a, w_branch_b, w_out, norm2, w_router, b_router, w_gate_up, b_gate_up,
           w_down, b_down, final_norm):
    x_ctx = x_prompt.reshape(T_CTX, D_MODEL)
    x_lat = x_sample.reshape(T_LAT, D_MODEL)
    cc = jnp.zeros((MOD_ROWS, D_MODEL), F32).at[:N_LAT_SEQ].set(c).at[N_LAT_SEQ].set(c_ctx)
    mod = _ada(cc, w_ada[0], b_ada)

    w_in_bf = w_in[0].astype(BF16)
    w_gates = w_in_bf[:, HGRN_W:HGRN_W + GATE_W]
    w_rest = jnp.concatenate([w_in_bf[:, :HGRN_W], w_in_bf[:, HGRN_W + GATE_W:]], axis=1)
    gates, rest = _inproj(x_ctx, x_lat, mod, norm1, w_gates, w_rest)

    mall, masks = _hgrn_consts()
    o_f, o_b, new_state = _hgrn(gates, rest, hgrn_lb, state_hgrn[:, 0], mall, masks)

    a_pool, cnt_pool = _pool_consts()
    yb = _pool(rest, a_pool, cnt_pool, w_pool[0].astype(BF16), pool_scale)

    wr_t = w_router[0].T
    wr_hi = wr_t.astype(BF16)
    wr_lo = (wr_t - wr_hi.astype(F32)).astype(BF16)
    tri = jnp.asarray(np.triu(np.ones((TBD, TBD), np.float32), 1), BF16)
    x1, h2, top_e, top_w, hist = _merge(
        x_ctx, x_lat, o_f, o_b, rest, yb, mod, hgrn_norm, w_branch_a[0].astype(BF16), w_branch_b[0].astype(BF16),
        w_out[0].astype(BF16), norm2, wr_hi, wr_lo, b_router.reshape(N_EXP, 1))

    tables, tile_tables = _dispatch_tables(hist)
    row_iota = jnp.asarray(np.broadcast_to(np.arange(SORT_CHUNK, dtype=np.float32)[:, None], (SORT_CHUNK, TBD)), BF16)
    xs = _dispatch(tables, h2, top_e, tri, row_iota)
    ys = _moe(tile_tables, xs, w_gate_up[0], b_gate_up[0], w_down[0], b_down[0])
    y_ctx, y_lat = _final(tables, ys, x1, top_e, top_w, tri, row_iota, mod, final_norm.reshape(1, D_MODEL))
    y_prompt = y_ctx.reshape(N_CTX_SEQ, CTX_LEN, D_MODEL)
    y_sample = y_lat.reshape(N_LAT_SEQ, LAT_LEN, D_MODEL)
    return y_prompt, y_sample, new_state[:, None]
```

```python
import functools

import numpy as np
import jax
import jax.numpy as jnp
from jax import lax
from jax.experimental import pallas as pl
from jax.experimental.pallas import tpu as pltpu

F32 = jnp.float32
BF16 = jnp.bfloat16
I32 = jnp.int32
U32 = jnp.uint32

D_MODEL = 1024
N_CTX_SEQ, CTX_LEN = 32, 256
N_LAT_SEQ, LAT_LEN = 4, 2048
T_CTX = N_CTX_SEQ * CTX_LEN
T_LAT = N_LAT_SEQ * LAT_LEN
T_ALL = T_CTX + T_LAT
SUBLANES, LANES = 8, 128
TB = 256
NB = T_ALL // TB
NB_CTX = T_CTX // TB
LAT_BLOCKS = LAT_LEN // TB
HEADS, HEAD_K, HEAD_V = 4, 128, 128
HGRN_W = HEADS * HEAD_V
POOL_WINDOWS = (2, 4, 8, 16)
POOL_G = 128
POOL_W = len(POOL_WINDOWS) * POOL_G
GRID_W = 64
GRID_H = LAT_LEN // GRID_W
IN_W = 5 * HGRN_W + POOL_W + 2 * D_MODEL
GATE_W = 2 * HGRN_W
REST_W = IN_W - GATE_W
N_EXP, TOP_K, D_FF = 32, 4, 1024
SWIGLU_LIMIT = 7.0
SWIGLU_ALPHA = 1.702
EPS = 1e-6
LOG2_E = 1.4426950408889634
CHUNK = 64
N_LEVELS = 6
EXP_ROWS = (N_LEVELS + 2) * CHUNK
MM_BLOCKS = (0, 4, 5, 6)
MM_ROWS = len(MM_BLOCKS) * CHUNK
COARSE_LEVELS = ((1, 32), (2, 16), (3, 8))
TM = 512
MOE_PIECES = ((0, 128, (128,)), (128, 256, (256,)), (256, 384, (256, 128)), (384, TM, (256, 256)))
TBD = 512
NBD = T_ALL // TBD
NBD_CTX = T_CTX // TBD
SORT_CHUNK = 256
LROWS = TBD * TOP_K + N_EXP * SUBLANES
SEG_SMALL_BITS = 3
N_TILES = -(-(T_ALL * TOP_K + NBD * N_EXP * (SUBLANES - 1) + N_EXP * (TM - 1)) // TM)
MOD_ROWS = 8
VMEM_LIMIT = 56 * 1024 * 1024


def _params(sem=("arbitrary",)):
    return pltpu.CompilerParams(dimension_semantics=sem, vmem_limit_bytes=VMEM_LIMIT)


def _dot(a, b):
    return jnp.dot(a, b, preferred_element_type=F32)


def _dot_nt(a, b):
    return lax.dot_general(a, b, (((1,), (1,)), ((), ())), preferred_element_type=F32)


def _dot_tn(a, b):
    return lax.dot_general(a, b, (((0,), (0,)), ((), ())), preferred_element_type=F32)


def _split2(x):
    hi = x.astype(BF16)
    lo = (x - hi.astype(F32)).astype(BF16)
    return hi, lo


def _mod_row(i):
    return jnp.where(i < NBD_CTX, N_LAT_SEQ, (i - NBD_CTX) // (LAT_LEN // TBD))


def _ada_kernel(c_ref, w_ref, b_ref, o_ref):
    c = c_ref[...]
    s = c * jax.nn.sigmoid(c)
    o_ref[...] = jnp.dot(s, w_ref[...], preferred_element_type=F32,
                         precision=lax.Precision.HIGHEST) + b_ref[...]


def _ada(cc, w_ada, b_ada):
    nblk = 1536
    return pl.pallas_call(
        _ada_kernel,
        out_shape=jax.ShapeDtypeStruct((MOD_ROWS, 6 * D_MODEL), F32),
        grid=(6 * D_MODEL // nblk,),
        in_specs=[pl.BlockSpec((MOD_ROWS, D_MODEL), lambda j: (0, 0)),
                  pl.BlockSpec((D_MODEL, nblk), lambda j: (0, j)),
                  pl.BlockSpec((1, nblk), lambda j: (0, j))],
        out_specs=pl.BlockSpec((MOD_ROWS, nblk), lambda j: (0, j)),
        compiler_params=_params(),
        name="ada",
    )(cc, w_ada, b_ada)


def _rms(x, g):
    ms = jnp.mean(x * x, axis=-1, keepdims=True)
    return x * lax.rsqrt(ms + EPS) * g


def _x_specs():
    return [pl.BlockSpec((TBD, D_MODEL), lambda i, *_: (jnp.minimum(i, NBD_CTX - 1), 0)),
            pl.BlockSpec((TBD, D_MODEL), lambda i, *_: (jnp.maximum(i - NBD_CTX, 0), 0))]


def _x_block(xc_ref, xl_ref):
    return jnp.where(pl.program_id(0) < NBD_CTX, xc_ref[...], xl_ref[...])


def _inproj_kernel(xc_ref, xl_ref, mod_ref, n1_ref, w_ref, og_ref, or_ref):
    row = _mod_row(pl.program_id(0))
    shift = mod_ref[pl.ds(row, 1), pl.ds(0, D_MODEL)]
    scale = mod_ref[pl.ds(row, 1), pl.ds(D_MODEL, D_MODEL)]
    h = (_rms(_x_block(xc_ref, xl_ref), n1_ref[...]) * (1.0 + scale) + shift).astype(BF16)
    og_ref[...] = _dot(h, w_ref[:, HGRN_W:HGRN_W + GATE_W])
    or_ref[:, :HGRN_W] = _dot(h, w_ref[:, :HGRN_W]).astype(BF16)
    or_ref[:, HGRN_W:] = _dot(h, w_ref[:, HGRN_W + GATE_W:]).astype(BF16)


def _inproj(x_ctx, x_lat, mod, norm1, w_in_bf):
    return pl.pallas_call(
        _inproj_kernel,
        out_shape=(jax.ShapeDtypeStruct((T_ALL, GATE_W), F32),
                   jax.ShapeDtypeStruct((T_ALL, REST_W), BF16)),
        grid=(NBD,),
        in_specs=_x_specs() + [
                  pl.BlockSpec((MOD_ROWS, 6 * D_MODEL), lambda i: (0, 0)),
                  pl.BlockSpec((1, D_MODEL), lambda i: (0, 0)),
                  pl.BlockSpec((D_MODEL, IN_W), lambda i: (0, 0))],
        out_specs=(pl.BlockSpec((TBD, GATE_W), lambda i: (i, 0)),
                   pl.BlockSpec((TBD, REST_W), lambda i: (i, 0))),
        compiler_params=_params(),
        name="inproj",
    )(x_ctx, x_lat, mod, norm1, w_in_bf)


def _hgrn_consts():
    c = CHUNK
    t = np.arange(c)[:, None]
    u = np.arange(c)[None, :]
    blocks = [u <= t]
    masks = [np.eye(c, dtype=bool)]
    h = c // 2
    while h >= 1:
        bi = t // h
        upper = (bi % 2) == 1
        e_up = (u >= bi * h) & (u <= t)
        e_lo = (u > t) & (u <= bi * h + h - 1)
        blocks.append(np.where(upper, e_up, e_lo))
        masks.append(((t // (2 * h)) == (u // (2 * h))) & (((t // h) % 2) == 1) & (((u // h) % 2) == 0))
        h //= 2
    blocks.append(u > t)
    m_f = np.stack(blocks).astype(np.float32)
    k_f = np.stack(masks).astype(np.float32)
    m_b = m_f[:, ::-1, ::-1]
    k_b = k_f[:, ::-1, ::-1]
    sel = list(MM_BLOCKS)
    m = np.stack([m_f[sel].reshape(MM_ROWS, c), m_b[sel].reshape(MM_ROWS, c)])
    m3 = np.concatenate([m, m, m], axis=2)
    return jnp.asarray(m3, BF16), jnp.asarray(np.stack([k_f, k_b]), F32)


def _hgrn_block(dirs, lb, mall_ref, mask_ref, st_ref, z_ref, k_ref, sc_ref, run_ref):
    c = CHUNK
    nchunk = TB // c
    units = [(d, h) for d in range(2) for h in range(HEADS)]
    sl = [slice(h * HEAD_K, (h + 1) * HEAD_K) for h in range(HEADS)]

    def rows(ci, d):
        r0 = ci * c if d == 0 else (nchunk - 1 - ci) * c
        return slice(r0, r0 + c)

    def exponents(ci):
        s = ci % 2
        for d in range(2):
            f = lb[d:d + 1] + (1.0 - lb[d:d + 1]) * jax.nn.sigmoid(dirs[d][1][rows(ci, d), :])
            k_ref[s, d] = 1.0 - f
            k_ref[s, 2 + d] = dirs[d][0][rows(ci, d), :].astype(F32)
            g = jnp.log(f) * LOG2_E
            g1 = g.astype(BF16)
            r1 = g - g1.astype(F32)
            g2 = r1.astype(BF16)
            g3 = (r1 - g2.astype(F32)).astype(BF16)
            gsplit = jnp.concatenate([g1, g2, g3], axis=0)
            ex = _dot(mall_ref[d], gsplit)
            run = ex[0:c]
            run_ref[d] = run
            z_ref[s, d, 0:c] = jnp.exp2(run)
            for j, blk in enumerate(MM_BLOCKS[1:]):
                z_ref[s, d, blk * c:(blk + 1) * c] = jnp.exp2(ex[(j + 1) * c:(j + 2) * c])
            for blk, h in COARSE_LEVELS:
                for base in range(0, c, 2 * h):
                    ref = run_ref[d, base + h - 1 + d:base + h + d, :]
                    if d == 0:
                        first, second = ref - run[base:base + h], run[base + h:base + 2 * h] - ref
                    else:
                        first, second = run[base:base + h] - ref, ref - run[base + h:base + 2 * h]
                    z_ref[s, d, blk * c + base:blk * c + base + h] = jnp.exp2(first)
                    z_ref[s, d, blk * c + base + h:blk * c + base + 2 * h] = jnp.exp2(second)
            end = run_ref[d, c - 1:c, :] if d == 0 else run_ref[d, 0:1, :]
            z_ref[s, d, (N_LEVELS + 1) * c:] = jnp.exp2(end - run)

    def q_of(ci, d, h):
        return k_ref[ci % 2, 2 + d, :, sl[h]]

    def v_of(ci, d, h):
        return dirs[d][2][rows(ci, d), sl[h]].astype(BF16)

    def qz(ci, d, h, blk):
        return (q_of(ci, d, h) * z_ref[ci % 2, d, blk * c:(blk + 1) * c, sl[h]]).astype(BF16)

    def kz(ci, d, h, blk):
        return (k_ref[ci % 2, d, :, sl[h]] * z_ref[ci % 2, d, blk * c:(blk + 1) * c, sl[h]]).astype(BF16)

    def levels(ci):
        for d, h in units:
            q = q_of(ci, d, h).astype(F32)
            k = k_ref[ci % 2, d, :, sl[h]]
            k_next = pltpu.roll(k, 1 if d == 0 else c - 1, 0)
            zq = q * z_ref[ci % 2, d, N_LEVELS * c:(N_LEVELS + 1) * c, sl[h]]
            diag = jnp.sum(q * k, axis=1, keepdims=True)
            near = jnp.sum(zq * k_next, axis=1, keepdims=True)
            sc_ref[d, h] = mask_ref[d, 0] * diag + mask_ref[d, N_LEVELS] * near
        for lev in range(N_LEVELS - 1):
            for d, h in units:
                sc_ref[d, h] += mask_ref[d, lev + 1] * _dot_nt(qz(ci, d, h, lev + 1), kz(ci, d, h, lev + 1))

    def tail(ci):
        for d, h in units:
            o = (_dot_nt(qz(ci, d, h, 0), st_ref[d, h].astype(BF16))
                 + _dot(sc_ref[d, h].astype(BF16), v_of(ci, d, h)))
            dirs[d][3][rows(ci, d), sl[h]] = o * (HEAD_K ** -0.5)
        for d, h in units:
            tot_row = c - 1 if d == 0 else 0
            decay = z_ref[ci % 2, d, tot_row:tot_row + 1, sl[h]]
            st_ref[d, h] = st_ref[d, h] * decay + _dot_tn(v_of(ci, d, h), kz(ci, d, h, N_LEVELS + 1))

    exponents(0)
    for ci in range(nchunk):
        levels(ci)
        if ci + 1 < nchunk:
            exponents(ci + 1)
        tail(ci)


def _hgrn_kernel(qf_ref, ff_ref, vf_ref, qb_ref, fb_ref, vb_ref, lbraw_ref, s0_ref, mall_ref, mask_ref,
                 of_ref, ob_ref, sout_hbm, st_ref, stage_ref, z_ref, k_ref, sc_ref, run_ref, sem):
    i = pl.program_id(0)
    j = (i - NB_CTX) % LAT_BLOCKS
    is_ctx = i < NB_CTX

    @pl.when(is_ctx)
    def _():
        st_ref[...] = jnp.zeros_like(st_ref)

    @pl.when(jnp.logical_and(jnp.logical_not(is_ctx), j == 0))
    def _():
        for d in range(2):
            for h in range(HEADS):
                st_ref[d, h] = s0_ref[0, d, h].T

    a0 = lbraw_ref[0]
    a1 = lbraw_ref[1]
    mx = jnp.maximum(a0, a1)
    e0 = jnp.exp(a0 - mx)
    e1 = jnp.exp(a1 - mx)
    lb = e0 / (e0 + e1)

    dirs = ((qf_ref, ff_ref, vf_ref, of_ref), (qb_ref, fb_ref, vb_ref, ob_ref))
    _hgrn_block(dirs, lb, mall_ref, mask_ref, st_ref, z_ref, k_ref, sc_ref, run_ref)

    @pl.when(is_ctx)
    def _():
        for d in range(2):
            for h in range(HEADS):
                stage_ref[d, h] = st_ref[d, h].T
        cp = pltpu.make_async_copy(stage_ref, sout_hbm.at[i], sem)
        cp.start()
        cp.wait()


def _bwd_block(i):
    j = (i - NB_CTX) % LAT_BLOCKS
    return jnp.where(i < NB_CTX, i, i - j + (LAT_BLOCKS - 1 - j))


def _hgrn(gates, rest, hgrn_lb, s0, mall, masks):
    nh = HGRN_W
    fwd = lambda col: pl.BlockSpec((TB, nh), lambda i: (i, col))
    bwd = lambda col: pl.BlockSpec((TB, nh), lambda i: (_bwd_block(i), col))
    lat_seq = lambda i: jnp.clip((i - NB_CTX) // LAT_BLOCKS, 0, N_LAT_SEQ - 1)
    return pl.pallas_call(
        _hgrn_kernel,
        out_shape=(jax.ShapeDtypeStruct((T_ALL, nh), F32),
                   jax.ShapeDtypeStruct((T_ALL, nh), F32),
                   jax.ShapeDtypeStruct((N_CTX_SEQ, 2, HEADS, HEAD_K, HEAD_V), F32)),
        grid=(NB,),
        in_specs=[fwd(0), fwd(0), fwd(1), bwd(0), bwd(1), bwd(1),
                  pl.BlockSpec((2, 2, nh), lambda i: (0, 0, 0)),
                  pl.BlockSpec((1, 2, HEADS, HEAD_K, HEAD_V), lambda i: (lat_seq(i), 0, 0, 0, 0)),
                  pl.BlockSpec((2, MM_ROWS, 3 * CHUNK), lambda i: (0, 0, 0)),
                  pl.BlockSpec((2, N_LEVELS + 1, CHUNK, CHUNK), lambda i: (0, 0, 0, 0))],
        out_specs=(pl.BlockSpec((TB, nh), lambda i: (i, 0)),
                   pl.BlockSpec((TB, nh), lambda i: (_bwd_block(i), 0)),
                   pl.BlockSpec(memory_space=pl.ANY)),
        scratch_shapes=[pltpu.VMEM((2, HEADS, HEAD_V, HEAD_K), F32),
                        pltpu.VMEM((2, HEADS, HEAD_K, HEAD_V), F32),
                        pltpu.VMEM((2, 2, EXP_ROWS, HGRN_W), F32),
                        pltpu.VMEM((2, 4, CHUNK, HGRN_W), F32),
                        pltpu.VMEM((2, HEADS, CHUNK, CHUNK), F32),
                        pltpu.VMEM((2, CHUNK, HGRN_W), F32),
                        pltpu.SemaphoreType.DMA],
        compiler_params=_params(),
        name="hgrn",
    )(rest, gates, rest, rest, gates, rest, hgrn_lb, s0, mall, masks)


def _window_bounds(n, w):
    pos = np.arange(n)
    lo = np.clip(pos - w // 2, 0, n - 1)
    hi = np.clip(pos - w // 2 + w - 1, 0, n - 1)
    return lo, hi


def _pool_consts():
    seq, img, cnt_seq, cnt_col = [], [], [], []
    for w in POOL_WINDOWS:
        lo, hi = _window_bounds(CTX_LEN, w)
        u = np.arange(CTX_LEN)[None, :]
        seq.append((u >= lo[:, None]) & (u <= hi[:, None]))
        cnt_seq.append(hi - lo + 1)
        lo, hi = _window_bounds(GRID_W, w)
        u = np.arange(GRID_W)[None, :]
        band = (u >= lo[:, None]) & (u <= hi[:, None])
        img.append(np.kron(np.eye(TB // GRID_W, dtype=bool), band))
        cnt_col.append(np.tile(hi - lo + 1, TB // GRID_W))
    a = np.stack([np.stack(seq), np.stack(img)]).astype(np.float32)
    cnt = np.stack([np.stack(cnt_seq), np.stack(cnt_col)]).astype(np.float32)
    cnt = np.broadcast_to(cnt[..., None], cnt.shape + (POOL_G,))
    return jnp.asarray(a, BF16), jnp.asarray(cnt, F32)


POOL_ROWS = LAT_LEN


def _pool_kernel(u_ref, a_ref, cnt_ref, wp_ref, ps_ref, o_ref, cp_ref):
    i = pl.program_id(0)
    nblk = POOL_ROWS // TB

    def finish(g, r0, nrows, pm):
        sl = slice(g * POOL_G, (g + 1) * POOL_G)
        d = pm - u_ref[pl.ds(r0, nrows), sl].astype(F32)
        y = _dot(d.astype(BF16), wp_ref[g]) * ps_ref[:, sl]
        o_ref[pl.ds(r0, nrows), sl] = y.astype(o_ref.dtype)

    def window_sum(kind, g, b):
        sl = slice(g * POOL_G, (g + 1) * POOL_G)
        return _dot(a_ref[kind, g], u_ref[pl.ds(b * TB, TB), sl]) / cnt_ref[kind, g]

    @pl.when(i < T_CTX // POOL_ROWS)
    def _():
        for g in range(len(POOL_WINDOWS)):
            for b in range(nblk):
                finish(g, b * TB, TB, window_sum(0, g, b))

    @pl.when(i >= T_CTX // POOL_ROWS)
    def _():
        for g, w in enumerate(POOL_WINDOWS):
            for b in range(nblk):
                cp_ref[pl.ds(b * TB, TB), :] = window_sum(1, g, b)
            lo, hi = _window_bounds(GRID_H, w)
            for r in range(GRID_H):
                acc = cp_ref[pl.ds(int(lo[r]) * GRID_W, GRID_W), :]
                for rr in range(int(lo[r]) + 1, int(hi[r]) + 1):
                    acc = acc + cp_ref[pl.ds(rr * GRID_W, GRID_W), :]
                finish(g, r * GRID_W, GRID_W, acc / float(hi[r] - lo[r] + 1))


def _pool(rest, a_pool, cnt_pool, w_pool_bf, pool_scale):
    col = 3
    return pl.pallas_call(
        _pool_kernel,
        out_shape=jax.ShapeDtypeStruct((T_ALL, POOL_W), BF16),
        grid=(T_ALL // POOL_ROWS,),
        in_specs=[pl.BlockSpec((POOL_ROWS, POOL_W), lambda i: (i, col)),
                  pl.BlockSpec((2, 4, TB, TB), lambda i: (0, 0, 0, 0)),
                  pl.BlockSpec((2, 4, TB, POOL_G), lambda i: (0, 0, 0, 0)),
                  pl.BlockSpec((4, POOL_G, POOL_G), lambda i: (0, 0, 0)),
                  pl.BlockSpec((1, POOL_W), lambda i: (0, 0))],
        out_specs=pl.BlockSpec((POOL_ROWS, POOL_W), lambda i: (i, 0)),
        scratch_shapes=[pltpu.VMEM((POOL_ROWS, POOL_G), F32)],
        compiler_params=_params(),
        name="pool",
    )(rest, a_pool, cnt_pool, w_pool_bf, pool_scale)


def _merge_kernel(xc_ref, xl_ref, of_ref, ob_ref, og_ref, yb_ref, ga_ref, gb_ref, mod_ref, hn_ref, wa_ref, wb_ref,
                  wo_ref, n2_ref, wrh_ref, wrl_ref, br_ref,
                  x1_ref, h2_ref, te_ref, tw_ref, hist_ref):
    row = _mod_row(pl.program_id(0))
    gate1 = mod_ref[pl.ds(row, 1), pl.ds(2 * D_MODEL, D_MODEL)]
    shift2 = mod_ref[pl.ds(row, 1), pl.ds(3 * D_MODEL, D_MODEL)]
    scale2 = mod_ref[pl.ds(row, 1), pl.ds(4 * D_MODEL, D_MODEL)]

    halves = [slice(j * (TBD // 2), (j + 1) * (TBD // 2)) for j in range(2)]
    is_ctx = pl.program_id(0) < NBD_CTX

    def head_out(r):
        o = of_ref[r, :] + ob_ref[r, :]
        og = og_ref[r, :].astype(F32)
        ya = jnp.concatenate(
            [_rms(o[:, h * HEAD_V:(h + 1) * HEAD_V], hn_ref[...]) for h in range(HEADS)], axis=1)
        return (ya * (og * jax.nn.sigmoid(og))).astype(BF16)

    ya = [head_out(r) for r in halves]
    pa = [_dot(ya[j], wa_ref[...]) for j in range(2)]
    pb = [_dot(yb_ref[r, :], wb_ref[...]) for r in halves]
    merged = [(jax.nn.sigmoid(ga_ref[r, :].astype(F32)) * pa[j]
               + jax.nn.sigmoid(gb_ref[r, :].astype(F32)) * pb[j]).astype(BF16) for j, r in enumerate(halves)]
    po = [_dot(merged[j], wo_ref[...]) for j in range(2)]
    hh, hl = [], []
    for j, r in enumerate(halves):
        x1 = jnp.where(is_ctx, xc_ref[r, :], xl_ref[r, :]) + gate1 * po[j]
        x1_ref[r, :] = x1
        hi, lo = _split2(_rms(x1, n2_ref[...]) * (1.0 + scale2) + shift2)
        h2_ref[r, :] = hi
        hh.append(hi)
        hl.append(lo)
    hh = jnp.concatenate(hh, axis=0)
    hl = jnp.concatenate(hl, axis=0)

    lt = _dot_nt(wrh_ref[...], hh) + _dot_nt(wrl_ref[...], hh) + _dot_nt(wrh_ref[...], hl) + br_ref[...]
    eidx = lax.broadcasted_iota(I32, (N_EXP, TBD), 0)
    vals, idxs, cnt = [], [], jnp.zeros((N_EXP, TBD), F32)
    for _ in range(TOP_K):
        m = jnp.max(lt, axis=0, keepdims=True)
        idx = jnp.min(jnp.where(lt == m, eidx, N_EXP), axis=0, keepdims=True)
        sel = eidx == idx
        vals.append(m)
        idxs.append(idx)
        cnt = cnt + sel.astype(F32)
        lt = jnp.where(sel, -jnp.inf, lt)
    ex = [jnp.exp(v - vals[0]) for v in vals]
    den = ex[0] + ex[1] + ex[2] + ex[3]
    tw_ref[0] = jnp.concatenate([e / den for e in ex], axis=0)
    te_ref[0] = jnp.concatenate(idxs, axis=0)
    hist_ref[0] = jnp.sum(cnt, axis=1, keepdims=True).astype(I32)


def _merge(x_ctx, x_lat, o_f, o_b, rest, yb, mod, hgrn_norm, wa_bf, wb_bf, wo_bf, norm2, wr_hi, wr_lo, b_router):
    full = lambda shape: pl.BlockSpec(shape, lambda i: (0,) * len(shape))
    return pl.pallas_call(
        _merge_kernel,
        out_shape=(jax.ShapeDtypeStruct((T_ALL, D_MODEL), F32),
                   jax.ShapeDtypeStruct((T_ALL, D_MODEL), BF16),
                   jax.ShapeDtypeStruct((NBD, TOP_K, TBD), I32),
                   jax.ShapeDtypeStruct((NBD, TOP_K, TBD), F32),
                   jax.ShapeDtypeStruct((NBD, N_EXP, 1), I32)),
        grid=(NBD,),
        in_specs=_x_specs() + [
                  pl.BlockSpec((TBD, HGRN_W), lambda i: (i, 0)),
                  pl.BlockSpec((TBD, HGRN_W), lambda i: (i, 0)),
                  pl.BlockSpec((TBD, HGRN_W), lambda i: (i, 2)),
                  pl.BlockSpec((TBD, POOL_W), lambda i: (i, 0)),
                  pl.BlockSpec((TBD, D_MODEL), lambda i: (i, 2)),
                  pl.BlockSpec((TBD, D_MODEL), lambda i: (i, 3)),
                  full((MOD_ROWS, 6 * D_MODEL)),
                  full((1, HEAD_V)),
                  full((HGRN_W, D_MODEL)),
                  full((POOL_W, D_MODEL)),
                  full((D_MODEL, D_MODEL)),
                  full((1, D_MODEL)),
                  full((N_EXP, D_MODEL)),
                  full((N_EXP, D_MODEL)),
                  full((N_EXP, 1))],
        out_specs=(pl.BlockSpec((TBD, D_MODEL), lambda i: (i, 0)),
                   pl.BlockSpec((TBD, D_MODEL), lambda i: (i, 0)),
                   pl.BlockSpec((1, TOP_K, TBD), lambda i: (i, 0, 0)),
                   pl.BlockSpec((1, TOP_K, TBD), lambda i: (i, 0, 0)),
                   pl.BlockSpec((1, N_EXP, 1), lambda i: (i, 0, 0))),
        compiler_params=_params(),
        name="merge",
    )(x_ctx, x_lat, o_f, o_b, rest, yb, rest, rest, mod, hgrn_norm, wa_bf, wb_bf, wo_bf, norm2,
      wr_hi, wr_lo, b_router)


def _local_rows(te_ref, loff_ref, tri_ref):
    te = te_ref[0]
    eidx = lax.broadcasted_iota(I32, (N_EXP, TBD), 0)
    sels = [eidx == te[k:k + 1] for k in range(TOP_K)]
    cnt = sels[0].astype(F32)
    for s in sels[1:]:
        cnt = cnt + s.astype(F32)
    base = _dot(cnt.astype(BF16), tri_ref[...]) + loff_ref[0]
    return [jnp.sum(jnp.where(s, base, 0.0), axis=0, keepdims=True) for s in sels]


def _chunk_relative(rows, r0):
    out = []
    for r in rows:
        inside = jnp.logical_and(r >= r0, r < r0 + SORT_CHUNK)
        out.append(jnp.where(inside, r - r0, -1.0).astype(BF16))
    return out


def _segment_copies(make_copy, local_off, global_off, units):
    big_rows = SUBLANES << SEG_SMALL_BITS
    big = units >> SEG_SMALL_BITS

    def piece(p, carry):
        off = pl.multiple_of(p * big_rows, big_rows)
        make_copy(pl.multiple_of(local_off + off, SUBLANES), pl.multiple_of(global_off + off, SUBLANES),
                  big_rows).start()
        return carry

    lax.fori_loop(0, big, piece, 0)
    done = big * big_rows
    for j in reversed(range(SEG_SMALL_BITS)):
        rows = SUBLANES << j
        low = done + ((units >> (j + 1)) & ((1 << (SEG_SMALL_BITS - 1 - j)) - 1)) * (2 * rows)

        @pl.when(((units >> j) & 1) == 1)
        def _():
            make_copy(pl.multiple_of(local_off + low, SUBLANES), pl.multiple_of(global_off + low, SUBLANES),
                      rows).start()


def _pack_pairs(x):
    half = D_MODEL // 2
    lo = lax.bitcast_convert_type(x[:, :half], U32) >> 16
    hi = lax.bitcast_convert_type(x[:, half:], U32) & jnp.uint32(0xFFFF0000)
    return hi | lo


def _unpack_pairs(p):
    lo = lax.bitcast_convert_type(p << 16, F32).astype(BF16)
    hi = lax.bitcast_convert_type(p & jnp.uint32(0xFFFF0000), F32).astype(BF16)
    return jnp.concatenate([lo, hi], axis=1)


def _block_rows(loff_s, seg_s, b):
    last = b * N_EXP + N_EXP - 1
    return pl.multiple_of(loff_s[last] + seg_s[last], SUBLANES)


def _dispatch_kernel(loff_s, seg_s, gbase_s, tail_s, h2_ref, te_ref, loffv_ref, tri_ref, iota_ref, xs_hbm,
                     loc, zeros, sem, sem_z):
    b = pl.program_id(0)
    slot = b % 2

    def wait_block(blk, s):
        n = _block_rows(loff_s, seg_s, blk)
        pltpu.make_async_copy(loc.at[s, pl.ds(0, n)], xs_hbm.at[pl.ds(0, n)], sem.at[s]).wait()

    lrow = _local_rows(te_ref, loffv_ref, tri_ref)

    @pl.when(b >= 2)
    def _():
        wait_block(b - 2, slot)

    for r0 in range(0, LROWS, SORT_CHUNK):
        rel = _chunk_relative(lrow, r0)
        p = jnp.zeros((SORT_CHUNK, TBD), BF16)
        for k in reversed(range(TOP_K)):
            p = jnp.where(iota_ref[...] == rel[k], jnp.ones_like(p), p)
        loc[slot, r0:r0 + SORT_CHUNK, :] = _pack_pairs(_dot(p, h2_ref[...]))

    def out_copy(a, g, size):
        return pltpu.make_async_copy(loc.at[slot, pl.ds(a, size)], xs_hbm.at[pl.ds(g, size)], sem.at[slot])

    def body(e, carry):
        idx = b * N_EXP + e
        _segment_copies(out_copy, loff_s[idx], gbase_s[idx], seg_s[idx] // SUBLANES)
        return carry

    lax.fori_loop(0, N_EXP, body, 0)

    @pl.when(b == NBD - 1)
    def _():
        zeros[...] = jnp.zeros_like(zeros)

        def zero_copy(a, g, size):
            return pltpu.make_async_copy(zeros.at[pl.ds(a, size)], xs_hbm.at[pl.ds(g, size)], sem_z)

        def zbody(e, ztot):
            _segment_copies(zero_copy, 0, tail_s[e], tail_s[N_EXP + e] // SUBLANES)
            return ztot + tail_s[N_EXP + e]

        def tbody(t, carry):
            pltpu.make_async_copy(zeros, xs_hbm.at[pl.ds(pl.multiple_of(t * TM, TM), TM)], sem_z).start()
            return carry

        n_used = tail_s[2 * N_EXP]
        lax.fori_loop(n_used, N_TILES, tbody, 0)
        ztot = lax.fori_loop(0, N_EXP, zbody, 0) + (N_TILES - n_used) * TM
        ztot = pl.multiple_of(ztot, SUBLANES)

        @pl.when(ztot > 0)
        def _():
            pltpu.make_async_copy(xs_hbm.at[pl.ds(0, ztot)], xs_hbm.at[pl.ds(0, ztot)], sem_z).wait()

        wait_block(b - 1, 1 - slot)
        wait_block(b, slot)


def _dispatch(tables, h2, top_e, tri, row_iota):
    loff_s, seg_s, gbase_s, tail_s, loff_v = tables
    grid_spec = pltpu.PrefetchScalarGridSpec(
        num_scalar_prefetch=4,
        grid=(NBD,),
        in_specs=[pl.BlockSpec((TBD, D_MODEL), lambda i, *_: (i, 0)),
                  pl.BlockSpec((1, TOP_K, TBD), lambda i, *_: (i, 0, 0)),
                  pl.BlockSpec((1, N_EXP, 1), lambda i, *_: (i, 0, 0)),
                  pl.BlockSpec((TBD, TBD), lambda i, *_: (0, 0)),
                  pl.BlockSpec((SORT_CHUNK, TBD), lambda i, *_: (0, 0))],
        out_specs=pl.BlockSpec(memory_space=pl.ANY),
        scratch_shapes=[pltpu.VMEM((2, LROWS, D_MODEL // 2), U32),
                        pltpu.VMEM((TM, D_MODEL // 2), U32),
                        pltpu.SemaphoreType.DMA((2,)),
                        pltpu.SemaphoreType.DMA])
    return pl.pallas_call(
        _dispatch_kernel,
        out_shape=jax.ShapeDtypeStruct((N_TILES * TM, D_MODEL // 2), U32),
        grid_spec=grid_spec,
        compiler_params=_params(),
        name="dispatch",
    )(loff_s, seg_s, gbase_s, tail_s, h2, top_e, loff_v, tri, row_iota)


def _moe_kernel(te_ref, first_ref, par_ref, next_ref, nv_ref, nu_ref, xs_ref, bgu_ref, bd_ref, wgu_hbm, wd_hbm, o_ref,
                wgu_st, wd_st, wgu_bf, wd_bf, sem):
    i = pl.program_id(0)

    def fetch(e, s):
        return (pltpu.make_async_copy(wgu_hbm.at[e], wgu_st.at[s], sem.at[0, s]),
                pltpu.make_async_copy(wd_hbm.at[e], wd_st.at[s], sem.at[1, s]))

    @pl.when(i < nu_ref[0])
    def _():
        @pl.when(first_ref[i] == 1)
        def _():
            s = par_ref[i]

            @pl.when(i == 0)
            def _():
                for cp in fetch(te_ref[0], 0):
                    cp.start()

            for cp in fetch(te_ref[i], s):
                cp.wait()

            @pl.when(next_ref[i] >= 0)
            def _():
                for cp in fetch(next_ref[i], 1 - s):
                    cp.start()

            wgu_bf[...] = wgu_st[s].astype(BF16)
            wd_bf[...] = wd_st[s].astype(BF16)

        def gate_up(r):
            return _dot(_unpack_pairs(xs_ref[r, :]), wgu_bf[...]) + bgu_ref[0]

        def activation(gu):
            gate = jnp.minimum(gu[:, :D_FF], SWIGLU_LIMIT)
            up = jnp.clip(gu[:, D_FF:], -SWIGLU_LIMIT, SWIGLU_LIMIT)
            return ((up + 1.0) * gate * jax.nn.sigmoid(SWIGLU_ALPHA * gate)).astype(BF16)

        def down(r, act):
            out = _dot(act, wd_bf[...]) + bd_ref[0]
            o_ref[r, :] = _pack_pairs(out.astype(BF16).astype(F32))

        def run(sizes):
            starts = [sum(sizes[:j]) for j in range(len(sizes))]
            pieces = [slice(a, a + n) for a, n in zip(starts, sizes)]
            gu = [gate_up(r) for r in pieces]
            for r, g in zip(pieces, gu):
                down(r, activation(g))
            done = sum(sizes)
            if done < TM:
                o_ref[done:, :] = jnp.zeros((TM - done, D_MODEL // 2), U32)

        nv = nv_ref[i]
        for lo, hi, sizes in MOE_PIECES:
            pl.when(jnp.logical_and(nv > lo, nv <= hi))(functools.partial(run, sizes))

    @pl.when(i >= nu_ref[0])
    def _():
        o_ref[...] = jnp.zeros_like(o_ref)


def _moe(tile_tables, xs, w_gate_up, b_gate_up, w_down, b_down):
    nsp = len(tile_tables)
    row_tile = lambda i, *s: (jnp.minimum(i, s[nsp - 1][0] - 1), 0)
    grid_spec = pltpu.PrefetchScalarGridSpec(
        num_scalar_prefetch=nsp,
        grid=(N_TILES,),
        in_specs=[pl.BlockSpec((TM, D_MODEL // 2), row_tile),
                  pl.BlockSpec((1, 1, 2 * D_FF), lambda i, te, *_: (te[i], 0, 0)),
                  pl.BlockSpec((1, 1, D_MODEL), lambda i, te, *_: (te[i], 0, 0)),
                  pl.BlockSpec(memory_space=pl.ANY),
                  pl.BlockSpec(memory_space=pl.ANY)],
        out_specs=pl.BlockSpec((TM, D_MODEL // 2), lambda i, *_: (i, 0)),
        scratch_shapes=[pltpu.VMEM((2, D_MODEL, 2 * D_FF), F32),
                        pltpu.VMEM((2, D_FF, D_MODEL), F32),
                        pltpu.VMEM((D_MODEL, 2 * D_FF), BF16),
                        pltpu.VMEM((D_FF, D_MODEL), BF16),
                        pltpu.SemaphoreType.DMA((2, 2))])
    return pl.pallas_call(
        _moe_kernel,
        out_shape=jax.ShapeDtypeStruct((N_TILES * TM, D_MODEL // 2), U32),
        grid_spec=grid_spec,
        compiler_params=_params(),
        name="moe",
    )(*tile_tables, xs, b_gate_up.reshape(N_EXP, 1, 2 * D_FF), b_down.reshape(N_EXP, 1, D_MODEL),
      w_gate_up, w_down)


def _final_kernel(loff_s, seg_s, gbase_s, ys_hbm, x1_ref, te_ref, tw_ref, loffv_ref, tri_ref, iota_ref, mod_ref, fn_ref,
                  oc_ref, ol_ref, loc, sem):
    b = pl.program_id(0)
    slot = b % 2

    def start_block(blk, s):
        def in_copy(a, g, size):
            return pltpu.make_async_copy(ys_hbm.at[pl.ds(g, size)], loc.at[s, pl.ds(a, size)], sem.at[s])

        def body(e, carry):
            idx = blk * N_EXP + e
            _segment_copies(in_copy, loff_s[idx], gbase_s[idx], seg_s[idx] // SUBLANES)
            return carry

        lax.fori_loop(0, N_EXP, body, 0)

    @pl.when(b == 0)
    def _():
        loc[...] = jnp.zeros_like(loc)
        start_block(0, 0)

    @pl.when(b + 1 < NBD)
    def _():
        start_block(b + 1, 1 - slot)

    lrow = _local_rows(te_ref, loffv_ref, tri_ref)
    tw = tw_ref[0]
    wts = [tw[k:k + 1].astype(BF16) for k in range(TOP_K)]

    n = _block_rows(loff_s, seg_s, b)
    pltpu.make_async_copy(ys_hbm.at[pl.ds(0, n)], loc.at[slot, pl.ds(0, n)], sem.at[slot]).wait()

    y = None
    for r0 in range(0, LROWS, SORT_CHUNK):
        rel = _chunk_relative(lrow, r0)
        pw = jnp.zeros((SORT_CHUNK, TBD), BF16)
        for k in reversed(range(TOP_K)):
            pw = jnp.where(iota_ref[...] == rel[k], wts[k], pw)
        part = _dot_tn(pw, _unpack_pairs(loc[slot, r0:r0 + SORT_CHUNK, :]))
        y = part if y is None else y + part

    row = _mod_row(b)
    gate2 = mod_ref[pl.ds(row, 1), pl.ds(5 * D_MODEL, D_MODEL)]
    out = _rms(x1_ref[...] + gate2 * y, fn_ref[...])

    @pl.when(b < NBD_CTX)
    def _():
        oc_ref[...] = out

    @pl.when(b >= NBD_CTX)
    def _():
        ol_ref[...] = out


def _final(tables, ys, x1, top_e, top_w, tri, col_iota, mod, final_norm):
    loff_s, seg_s, gbase_s, _, loff_v = tables
    grid_spec = pltpu.PrefetchScalarGridSpec(
        num_scalar_prefetch=3,
        grid=(NBD,),
        in_specs=[pl.BlockSpec(memory_space=pl.ANY),
                  pl.BlockSpec((TBD, D_MODEL), lambda i, *_: (i, 0)),
                  pl.BlockSpec((1, TOP_K, TBD), lambda i, *_: (i, 0, 0)),
                  pl.BlockSpec((1, TOP_K, TBD), lambda i, *_: (i, 0, 0)),
                  pl.BlockSpec((1, N_EXP, 1), lambda i, *_: (i, 0, 0)),
                  pl.BlockSpec((TBD, TBD), lambda i, *_: (0, 0)),
                  pl.BlockSpec((SORT_CHUNK, TBD), lambda i, *_: (0, 0)),
                  pl.BlockSpec((MOD_ROWS, 6 * D_MODEL), lambda i, *_: (0, 0)),
                  pl.BlockSpec((1, D_MODEL), lambda i, *_: (0, 0))],
        out_specs=(pl.BlockSpec((TBD, D_MODEL), lambda i, *_: (jnp.minimum(i, NBD_CTX - 1), 0)),
                   pl.BlockSpec((TBD, D_MODEL), lambda i, *_: (jnp.maximum(i - NBD_CTX, 0), 0))),
        scratch_shapes=[pltpu.VMEM((2, LROWS, D_MODEL // 2), U32),
                        pltpu.SemaphoreType.DMA((2,))])
    return pl.pallas_call(
        _final_kernel,
        out_shape=(jax.ShapeDtypeStruct((T_CTX, D_MODEL), F32),
                   jax.ShapeDtypeStruct((T_LAT, D_MODEL), F32)),
        grid_spec=grid_spec,
        compiler_params=_params(),
        name="final",
    )(loff_s, seg_s, gbase_s, ys, x1, top_e, top_w, loff_v, tri, col_iota, mod, final_norm)


def _dispatch_tables(hist):
    hist = hist.reshape(NBD, N_EXP)
    seg = ((hist + SUBLANES - 1) // SUBLANES) * SUBLANES
    loff = jnp.cumsum(seg, axis=1) - seg
    rows_e = jnp.sum(seg, axis=0)
    region = ((rows_e + TM - 1) // TM) * TM
    region_end = jnp.cumsum(region)
    region_start = region_end - region
    gbase = region_start[None, :] + jnp.cumsum(seg, axis=0) - seg
    n_used = (region_end[-1] // TM).astype(I32)
    tail = jnp.concatenate([region_start + rows_e, region - rows_e, n_used.reshape(1)])
    start = jnp.arange(N_TILES, dtype=I32) * TM
    tile_e = jnp.sum((start[:, None] >= region_end[None, :]).astype(I32), axis=1)
    tile_e = jnp.minimum(tile_e, tile_e[jnp.maximum(n_used - 1, 0)])
    first = jnp.concatenate([jnp.ones((1,), I32), (tile_e[1:] != tile_e[:-1]).astype(I32)])
    parity = (jnp.cumsum(first) - 1) % 2
    later = jnp.where(tile_e[None, :] > tile_e[:, None], tile_e[None, :], N_EXP)
    nxt = jnp.min(later, axis=1)
    nxt = jnp.where(nxt == N_EXP, -1, nxt)
    flat = lambda a: a.reshape(-1).astype(I32)
    tables = (flat(loff), flat(seg), flat(gbase), flat(tail), loff.astype(F32).reshape(NBD, N_EXP, 1))
    mine = tile_e[:, None] == jnp.arange(N_EXP, dtype=I32)[None, :]
    data_end = jnp.sum(jnp.where(mine, (region_start + rows_e)[None, :], 0), axis=1)
    tile_nv = jnp.clip(data_end - start, 0, TM)
    tile_tables = (flat(tile_e), flat(first), flat(parity), flat(nxt), flat(tile_nv), n_used.reshape(1))
    return tables, tile_tables


def kernel(x_prompt, x_sample, state_hgrn, c, c_ctx, w_ada, b_ada, norm1, w_in, hgrn_lb, hgrn_norm, w_pool,
           pool_scale, w_branch_a, w_branch_b, w_out, norm2, w_router, b_router, w_gate_up, b_gate_up,
           w_down, b_down, final_norm):
    x_ctx = x_prompt.reshape(T_CTX, D_MODEL)
    x_lat = x_sample.reshape(T_LAT, D_MODEL)
    cc = jnp.zeros((MOD_ROWS, D_MODEL), F32).at[:N_LAT_SEQ].set(c).at[N_LAT_SEQ].set(c_ctx)
    mod = _ada(cc, w_ada[0], b_ada)

    gates, rest = _inproj(x_ctx, x_lat, mod, norm1, w_in[0].astype(BF16))

    mall, masks = _hgrn_consts()
    o_f, o_b, new_state = _hgrn(gates, rest, hgrn_lb, state_hgrn[:, 0], mall, masks)

    a_pool, cnt_pool = _pool_consts()
    yb = _pool(rest, a_pool, cnt_pool, w_pool[0].astype(BF16), pool_scale)

    wr_t = w_router[0].T
    wr_hi = wr_t.astype(BF16)
    wr_lo = (wr_t - wr_hi.astype(F32)).astype(BF16)
    tri = jnp.asarray(np.triu(np.ones((TBD, TBD), np.float32), 1), BF16)
    x1, h2, top_e, top_w, hist = _merge(
        x_ctx, x_lat, o_f, o_b, rest, yb, mod, hgrn_norm, w_branch_a[0].astype(BF16), w_branch_b[0].astype(BF16),
        w_out[0].astype(BF16), norm2, wr_hi, wr_lo, b_router.reshape(N_EXP, 1))

    tables, tile_tables = _dispatch_tables(hist)
    row_iota = jnp.asarray(np.broadcast_to(np.arange(SORT_CHUNK, dtype=np.float32)[:, None], (SORT_CHUNK, TBD)), BF16)
    xs = _dispatch(tables, h2, top_e, tri, row_iota)
    ys = _moe(tile_tables, xs, w_gate_up[0], b_gate_up[0], w_down[0], b_down[0])
    y_ctx, y_lat = _final(tables, ys, x1, top_e, top_w, tri, row_iota, mod, final_norm.reshape(1, D_MODEL))
    y_prompt = y_ctx.reshape(N_CTX_SEQ, CTX_LEN, D_MODEL)
    y_sample = y_lat.reshape(N_LAT_SEQ, LAT_LEN, D_MODEL)
    return y_prompt, y_sample, new_state[:, None]
```

```python
import functools

import numpy as np
import jax
import jax.numpy as jnp
from jax import lax
from jax.experimental import pallas as pl
from jax.experimental.pallas import tpu as pltpu

F32 = jnp.float32
BF16 = jnp.bfloat16
I32 = jnp.int32
U32 = jnp.uint32

D_MODEL = 1024
N_CTX_SEQ, CTX_LEN = 32, 256
N_LAT_SEQ, LAT_LEN = 4, 2048
T_CTX = N_CTX_SEQ * CTX_LEN
T_LAT = N_LAT_SEQ * LAT_LEN
T_ALL = T_CTX + T_LAT
SUBLANES, LANES = 8, 128
TB = 256
NB = T_ALL // TB
NB_CTX = T_CTX // TB
LAT_BLOCKS = LAT_LEN // TB
HEADS, HEAD_K, HEAD_V = 4, 128, 128
HGRN_W = HEADS * HEAD_V
POOL_WINDOWS = (2, 4, 8, 16)
POOL_G = 128
POOL_W = len(POOL_WINDOWS) * POOL_G
GRID_W = 64
GRID_H = LAT_LEN // GRID_W
IN_W = 5 * HGRN_W + POOL_W + 2 * D_MODEL
GATE_W = 2 * HGRN_W
REST_W = IN_W - GATE_W
N_EXP, TOP_K, D_FF = 32, 4, 1024
SWIGLU_LIMIT = 7.0
SWIGLU_ALPHA = 1.702
EPS = 1e-6
LOG2_E = 1.4426950408889634
CHUNK = 64
N_LEVELS = 6
EXP_ROWS = (N_LEVELS + 2) * CHUNK
MM_BLOCKS = (0, 4, 5, 6)
MM_ROWS = len(MM_BLOCKS) * CHUNK
COARSE_LEVELS = ((1, 32), (2, 16), (3, 8))
TM = 512
MOE_PIECES = ((0, 128, (128,)), (128, 256, (256,)), (256, 384, (256, 128)), (384, TM, (256, 256)))
TBD = 512
NBD = T_ALL // TBD
NBD_CTX = T_CTX // TBD
SORT_CHUNK = 256
LROWS = TBD * TOP_K + N_EXP * SUBLANES
SEG_SMALL_BITS = 3
N_TILES = -(-(T_ALL * TOP_K + NBD * N_EXP * (SUBLANES - 1) + N_EXP * (TM - 1)) // TM)
MOD_ROWS = 8
VMEM_LIMIT = 56 * 1024 * 1024


def _params(sem=("arbitrary",)):
    return pltpu.CompilerParams(dimension_semantics=sem, vmem_limit_bytes=VMEM_LIMIT)


def _dot(a, b):
    return jnp.dot(a, b, preferred_element_type=F32)


def _dot_nt(a, b):
    return lax.dot_general(a, b, (((1,), (1,)), ((), ())), preferred_element_type=F32)


def _dot_tn(a, b):
    return lax.dot_general(a, b, (((0,), (0,)), ((), ())), preferred_element_type=F32)


def _split2(x):
    hi = x.astype(BF16)
    lo = (x - hi.astype(F32)).astype(BF16)
    return hi, lo


def _mod_row(i):
    return jnp.where(i < NBD_CTX, N_LAT_SEQ, (i - NBD_CTX) // (LAT_LEN // TBD))


def _ada_kernel(c_ref, w_ref, b_ref, o_ref):
    c = c_ref[...]
    s = c * jax.nn.sigmoid(c)
    o_ref[...] = jnp.dot(s, w_ref[...], preferred_element_type=F32,
                         precision=lax.Precision.HIGHEST) + b_ref[...]


def _ada(cc, w_ada, b_ada):
    nblk = 1536
    return pl.pallas_call(
        _ada_kernel,
        out_shape=jax.ShapeDtypeStruct((MOD_ROWS, 6 * D_MODEL), F32),
        grid=(6 * D_MODEL // nblk,),
        in_specs=[pl.BlockSpec((MOD_ROWS, D_MODEL), lambda j: (0, 0)),
                  pl.BlockSpec((D_MODEL, nblk), lambda j: (0, j)),
                  pl.BlockSpec((1, nblk), lambda j: (0, j))],
        out_specs=pl.BlockSpec((MOD_ROWS, nblk), lambda j: (0, j)),
        compiler_params=_params(),
        name="ada",
    )(cc, w_ada, b_ada)


def _rms(x, g):
    ms = jnp.mean(x * x, axis=-1, keepdims=True)
    return x * lax.rsqrt(ms + EPS) * g


def _x_specs():
    return [pl.BlockSpec((TBD, D_MODEL), lambda i, *_: (jnp.minimum(i, NBD_CTX - 1), 0)),
            pl.BlockSpec((TBD, D_MODEL), lambda i, *_: (jnp.maximum(i - NBD_CTX, 0), 0))]


def _x_block(xc_ref, xl_ref):
    return jnp.where(pl.program_id(0) < NBD_CTX, xc_ref[...], xl_ref[...])


def _inproj_kernel(xc_ref, xl_ref, mod_ref, n1_ref, w_ref, og_ref, or_ref):
    row = _mod_row(pl.program_id(0))
    shift = mod_ref[pl.ds(row, 1), pl.ds(0, D_MODEL)]
    scale = mod_ref[pl.ds(row, 1), pl.ds(D_MODEL, D_MODEL)]
    h = (_rms(_x_block(xc_ref, xl_ref), n1_ref[...]) * (1.0 + scale) + shift).astype(BF16)
    og_ref[...] = _dot(h, w_ref[:, HGRN_W:HGRN_W + GATE_W])
    or_ref[:, :HGRN_W] = _dot(h, w_ref[:, :HGRN_W]).astype(BF16)
    or_ref[:, HGRN_W:] = _dot(h, w_ref[:, HGRN_W + GATE_W:]).astype(BF16)


def _inproj(x_ctx, x_lat, mod, norm1, w_in_bf):
    return pl.pallas_call(
        _inproj_kernel,
        out_shape=(jax.ShapeDtypeStruct((T_ALL, GATE_W), F32),
                   jax.ShapeDtypeStruct((T_ALL, REST_W), BF16)),
        grid=(NBD,),
        in_specs=_x_specs() + [
                  pl.BlockSpec((MOD_ROWS, 6 * D_MODEL), lambda i: (0, 0)),
                  pl.BlockSpec((1, D_MODEL), lambda i: (0, 0)),
                  pl.BlockSpec((D_MODEL, IN_W), lambda i: (0, 0))],
        out_specs=(pl.BlockSpec((TBD, GATE_W), lambda i: (i, 0)),
                   pl.BlockSpec((TBD, REST_W), lambda i: (i, 0))),
        compiler_params=_params(),
        name="inproj",
    )(x_ctx, x_lat, mod, norm1, w_in_bf)


def _hgrn_consts():
    c = CHUNK
    t = np.arange(c)[:, None]
    u = np.arange(c)[None, :]
    blocks = [u <= t]
    masks = [np.eye(c, dtype=bool)]
    h = c // 2
    while h >= 1:
        bi = t // h
        upper = (bi % 2) == 1
        e_up = (u >= bi * h) & (u <= t)
        e_lo = (u > t) & (u <= bi * h + h - 1)
        blocks.append(np.where(upper, e_up, e_lo))
        masks.append(((t // (2 * h)) == (u // (2 * h))) & (((t // h) % 2) == 1) & (((u // h) % 2) == 0))
        h //= 2
    blocks.append(u > t)
    m_f = np.stack(blocks).astype(np.float32)
    k_f = np.stack(masks).astype(np.float32)
    m_b = m_f[:, ::-1, ::-1]
    k_b = k_f[:, ::-1, ::-1]
    sel = list(MM_BLOCKS)
    m = np.stack([m_f[sel].reshape(MM_ROWS, c), m_b[sel].reshape(MM_ROWS, c)])
    m3 = np.concatenate([m, m, m], axis=2)
    return jnp.asarray(m3, BF16), jnp.asarray(np.stack([k_f, k_b]), F32)


def _hgrn_block(dirs, lb, mall_ref, mask_ref, st_ref, z_ref, k_ref, sc_ref, run_ref):
    c = CHUNK
    nchunk = TB // c
    units = [(d, h) for d in range(2) for h in range(HEADS)]
    sl = [slice(h * HEAD_K, (h + 1) * HEAD_K) for h in range(HEADS)]

    def rows(ci, d):
        r0 = ci * c if d == 0 else (nchunk - 1 - ci) * c
        return slice(r0, r0 + c)

    def exponents(ci):
        s = ci % 2
        for d in range(2):
            f = lb[d:d + 1] + (1.0 - lb[d:d + 1]) * jax.nn.sigmoid(dirs[d][1][rows(ci, d), :])
            k_ref[s, d] = 1.0 - f
            k_ref[s, 2 + d] = dirs[d][0][rows(ci, d), :].astype(F32)
            g = jnp.log(f) * LOG2_E
            g1 = g.astype(BF16)
            r1 = g - g1.astype(F32)
            g2 = r1.astype(BF16)
            g3 = (r1 - g2.astype(F32)).astype(BF16)
            gsplit = jnp.concatenate([g1, g2, g3], axis=0)
            ex = _dot(mall_ref[d], gsplit)
            run = ex[0:c]
            run_ref[d] = run
            z_ref[s, d, 0:c] = jnp.exp2(run)
            for j, blk in enumerate(MM_BLOCKS[1:]):
                z_ref[s, d, blk * c:(blk + 1) * c] = jnp.exp2(ex[(j + 1) * c:(j + 2) * c])
            for blk, h in COARSE_LEVELS:
                for base in range(0, c, 2 * h):
                    ref = run_ref[d, base + h - 1 + d:base + h + d, :]
                    if d == 0:
                        first, second = ref - run[base:base + h], run[base + h:base + 2 * h] - ref
                    else:
                        first, second = run[base:base + h] - ref, ref - run[base + h:base + 2 * h]
                    z_ref[s, d, blk * c + base:blk * c + base + h] = jnp.exp2(first)
                    z_ref[s, d, blk * c + base + h:blk * c + base + 2 * h] = jnp.exp2(second)
            end = run_ref[d, c - 1:c, :] if d == 0 else run_ref[d, 0:1, :]
            z_ref[s, d, (N_LEVELS + 1) * c:] = jnp.exp2(end - run)

    def q_of(ci, d, h):
        return k_ref[ci % 2, 2 + d, :, sl[h]]

    def v_of(ci, d, h):
        return dirs[d][2][rows(ci, d), sl[h]].astype(BF16)

    def qz(ci, d, h, blk):
        return (q_of(ci, d, h) * z_ref[ci % 2, d, blk * c:(blk + 1) * c, sl[h]]).astype(BF16)

    def kz(ci, d, h, blk):
        return (k_ref[ci % 2, d, :, sl[h]] * z_ref[ci % 2, d, blk * c:(blk + 1) * c, sl[h]]).astype(BF16)

    def levels(ci):
        for d, h in units:
            q = q_of(ci, d, h).astype(F32)
            k = k_ref[ci % 2, d, :, sl[h]]
            k_next = pltpu.roll(k, 1 if d == 0 else c - 1, 0)
            zq = q * z_ref[ci % 2, d, N_LEVELS * c:(N_LEVELS + 1) * c, sl[h]]
            diag = jnp.sum(q * k, axis=1, keepdims=True)
            near = jnp.sum(zq * k_next, axis=1, keepdims=True)
            sc_ref[d, h] = mask_ref[d, 0] * diag + mask_ref[d, N_LEVELS] * near
        for lev in range(N_LEVELS - 1):
            for d, h in units:
                sc_ref[d, h] += mask_ref[d, lev + 1] * _dot_nt(qz(ci, d, h, lev + 1), kz(ci, d, h, lev + 1))

    def tail(ci):
        for d, h in units:
            o = (_dot_nt(qz(ci, d, h, 0), st_ref[d, h].astype(BF16))
                 + _dot(sc_ref[d, h].astype(BF16), v_of(ci, d, h)))
            dirs[d][3][rows(ci, d), sl[h]] = o * (HEAD_K ** -0.5)
        for d, h in units:
            tot_row = c - 1 if d == 0 else 0
            decay = z_ref[ci % 2, d, tot_row:tot_row + 1, sl[h]]
            st_ref[d, h] = st_ref[d, h] * decay + _dot_tn(v_of(ci, d, h), kz(ci, d, h, N_LEVELS + 1))

    exponents(0)
    for ci in range(nchunk):
        levels(ci)
        if ci + 1 < nchunk:
            exponents(ci + 1)
        tail(ci)


def _hgrn_kernel(qf_ref, ff_ref, vf_ref, qb_ref, fb_ref, vb_ref, lbraw_ref, s0_ref, mall_ref, mask_ref,
                 of_ref, ob_ref, sout_hbm, st_ref, stage_ref, z_ref, k_ref, sc_ref, run_ref, sem):
    i = pl.program_id(0)
    j = (i - NB_CTX) % LAT_BLOCKS
    is_ctx = i < NB_CTX

    @pl.when(is_ctx)
    def _():
        st_ref[...] = jnp.zeros_like(st_ref)

    @pl.when(jnp.logical_and(jnp.logical_not(is_ctx), j == 0))
    def _():
        for d in range(2):
            for h in range(HEADS):
                st_ref[d, h] = s0_ref[0, d, h].T

    a0 = lbraw_ref[0]
    a1 = lbraw_ref[1]
    mx = jnp.maximum(a0, a1)
    e0 = jnp.exp(a0 - mx)
    e1 = jnp.exp(a1 - mx)
    lb = e0 / (e0 + e1)

    dirs = ((qf_ref, ff_ref, vf_ref, of_ref), (qb_ref, fb_ref, vb_ref, ob_ref))
    _hgrn_block(dirs, lb, mall_ref, mask_ref, st_ref, z_ref, k_ref, sc_ref, run_ref)

    @pl.when(is_ctx)
    def _():
        for d in range(2):
            for h in range(HEADS):
                stage_ref[d, h] = st_ref[d, h].T
        cp = pltpu.make_async_copy(stage_ref, sout_hbm.at[i], sem)
        cp.start()
        cp.wait()


def _bwd_block(i):
    j = (i - NB_CTX) % LAT_BLOCKS
    return jnp.where(i < NB_CTX, i, i - j + (LAT_BLOCKS - 1 - j))


def _hgrn(gates, rest, hgrn_lb, s0, mall, masks):
    nh = HGRN_W
    fwd = lambda col: pl.BlockSpec((TB, nh), lambda i: (i, col))
    bwd = lambda col: pl.BlockSpec((TB, nh), lambda i: (_bwd_block(i), col))
    lat_seq = lambda i: jnp.clip((i - NB_CTX) // LAT_BLOCKS, 0, N_LAT_SEQ - 1)
    return pl.pallas_call(
        _hgrn_kernel,
        out_shape=(jax.ShapeDtypeStruct((T_ALL, nh), F32),
                   jax.ShapeDtypeStruct((T_ALL, nh), F32),
                   jax.ShapeDtypeStruct((N_CTX_SEQ, 2, HEADS, HEAD_K, HEAD_V), F32)),
        grid=(NB,),
        in_specs=[fwd(0), fwd(0), fwd(1), bwd(0), bwd(1), bwd(1),
                  pl.BlockSpec((2, 2, nh), lambda i: (0, 0, 0)),
                  pl.BlockSpec((1, 2, HEADS, HEAD_K, HEAD_V), lambda i: (lat_seq(i), 0, 0, 0, 0)),
                  pl.BlockSpec((2, MM_ROWS, 3 * CHUNK), lambda i: (0, 0, 0)),
                  pl.BlockSpec((2, N_LEVELS + 1, CHUNK, CHUNK), lambda i: (0, 0, 0, 0))],
        out_specs=(pl.BlockSpec((TB, nh), lambda i: (i, 0)),
                   pl.BlockSpec((TB, nh), lambda i: (_bwd_block(i), 0)),
                   pl.BlockSpec(memory_space=pl.ANY)),
        scratch_shapes=[pltpu.VMEM((2, HEADS, HEAD_V, HEAD_K), F32),
                        pltpu.VMEM((2, HEADS, HEAD_K, HEAD_V), F32),
                        pltpu.VMEM((2, 2, EXP_ROWS, HGRN_W), F32),
                        pltpu.VMEM((2, 4, CHUNK, HGRN_W), F32),
                        pltpu.VMEM((2, HEADS, CHUNK, CHUNK), F32),
                        pltpu.VMEM((2, CHUNK, HGRN_W), F32),
                        pltpu.SemaphoreType.DMA],
        compiler_params=_params(),
        name="hgrn",
    )(rest, gates, rest, rest, gates, rest, hgrn_lb, s0, mall, masks)


def _window_bounds(n, w):
    pos = np.arange(n)
    lo = np.clip(pos - w // 2, 0, n - 1)
    hi = np.clip(pos - w // 2 + w - 1, 0, n - 1)
    return lo, hi


def _pool_consts():
    seq, img, cnt_seq, cnt_col = [], [], [], []
    for w in POOL_WINDOWS:
        lo, hi = _window_bounds(CTX_LEN, w)
        u = np.arange(CTX_LEN)[None, :]
        seq.append((u >= lo[:, None]) & (u <= hi[:, None]))
        cnt_seq.append(hi - lo + 1)
        lo, hi = _window_bounds(GRID_W, w)
        u = np.arange(GRID_W)[None, :]
        band = (u >= lo[:, None]) & (u <= hi[:, None])
        img.append(np.kron(np.eye(TB // GRID_W, dtype=bool), band))
        cnt_col.append(np.tile(hi - lo + 1, TB // GRID_W))
    a = np.stack([np.stack(seq), np.stack(img)]).astype(np.float32)
    cnt = np.stack([np.stack(cnt_seq), np.stack(cnt_col)]).astype(np.float32)
    cnt = np.broadcast_to(cnt[..., None], cnt.shape + (POOL_G,))
    return jnp.asarray(a, BF16), jnp.asarray(cnt, F32)


POOL_ROWS = LAT_LEN


def _pool_kernel(u_ref, a_ref, cnt_ref, wp_ref, ps_ref, o_ref, cp_ref, d_ref):
    i = pl.program_id(0)
    nblk = POOL_ROWS // TB

    def centre(g, r0, nrows, pm):
        sl = slice(g * POOL_G, (g + 1) * POOL_G)
        d_ref[pl.ds(r0, nrows), :] = (pm - u_ref[pl.ds(r0, nrows), sl].astype(F32)).astype(BF16)

    def group_map(g):
        sl = slice(g * POOL_G, (g + 1) * POOL_G)
        o_ref[:, sl] = (_dot(d_ref[...], wp_ref[g]) * ps_ref[:, sl]).astype(o_ref.dtype)

    def window_sum(kind, g, b):
        sl = slice(g * POOL_G, (g + 1) * POOL_G)
        return _dot(a_ref[kind, g], u_ref[pl.ds(b * TB, TB), sl]) / cnt_ref[kind, g]

    @pl.when(i < T_CTX // POOL_ROWS)
    def _():
        for g in range(len(POOL_WINDOWS)):
            for b in range(nblk):
                centre(g, b * TB, TB, window_sum(0, g, b))
            group_map(g)

    @pl.when(i >= T_CTX // POOL_ROWS)
    def _():
        for g, w in enumerate(POOL_WINDOWS):
            for b in range(nblk):
                cp_ref[pl.ds(b * TB, TB), :] = window_sum(1, g, b)
            lo, hi = _window_bounds(GRID_H, w)
            for r in range(GRID_H):
                acc = cp_ref[pl.ds(int(lo[r]) * GRID_W, GRID_W), :]
                for rr in range(int(lo[r]) + 1, int(hi[r]) + 1):
                    acc = acc + cp_ref[pl.ds(rr * GRID_W, GRID_W), :]
                centre(g, r * GRID_W, GRID_W, acc / float(hi[r] - lo[r] + 1))
            group_map(g)


def _pool(rest, a_pool, cnt_pool, w_pool_bf, pool_scale):
    col = 3
    return pl.pallas_call(
        _pool_kernel,
        out_shape=jax.ShapeDtypeStruct((T_ALL, POOL_W), BF16),
        grid=(T_ALL // POOL_ROWS,),
        in_specs=[pl.BlockSpec((POOL_ROWS, POOL_W), lambda i: (i, col)),
                  pl.BlockSpec((2, 4, TB, TB), lambda i: (0, 0, 0, 0)),
                  pl.BlockSpec((2, 4, TB, POOL_G), lambda i: (0, 0, 0, 0)),
                  pl.BlockSpec((4, POOL_G, POOL_G), lambda i: (0, 0, 0)),
                  pl.BlockSpec((1, POOL_W), lambda i: (0, 0))],
        out_specs=pl.BlockSpec((POOL_ROWS, POOL_W), lambda i: (i, 0)),
        scratch_shapes=[pltpu.VMEM((POOL_ROWS, POOL_G), F32),
                        pltpu.VMEM((POOL_ROWS, POOL_G), BF16)],
        compiler_params=_params(),
        name="pool",
    )(rest, a_pool, cnt_pool, w_pool_bf, pool_scale)


def _merge_kernel(xc_ref, xl_ref, of_ref, ob_ref, og_ref, yb_ref, ga_ref, gb_ref, mod_ref, hn_ref, wa_ref, wb_ref,
                  wo_ref, n2_ref, wrh_ref, wrl_ref, br_ref,
                  x1_ref, h2_ref, te_ref, tw_ref, hist_ref):
    row = _mod_row(pl.program_id(0))
    gate1 = mod_ref[pl.ds(row, 1), pl.ds(2 * D_MODEL, D_MODEL)]
    shift2 = mod_ref[pl.ds(row, 1), pl.ds(3 * D_MODEL, D_MODEL)]
    scale2 = mod_ref[pl.ds(row, 1), pl.ds(4 * D_MODEL, D_MODEL)]

    halves = [slice(j * (TBD // 2), (j + 1) * (TBD // 2)) for j in range(2)]
    is_ctx = pl.program_id(0) < NBD_CTX

    def head_out(r):
        o = of_ref[r, :] + ob_ref[r, :]
        og = og_ref[r, :].astype(F32)
        ya = jnp.concatenate(
            [_rms(o[:, h * HEAD_V:(h + 1) * HEAD_V], hn_ref[...]) for h in range(HEADS)], axis=1)
        return (ya * (og * jax.nn.sigmoid(og))).astype(BF16)

    ya = [head_out(r) for r in halves]
    pa = [_dot(ya[j], wa_ref[...]) for j in range(2)]
    pb = [_dot(yb_ref[r, :], wb_ref[...]) for r in halves]
    merged = [(jax.nn.sigmoid(ga_ref[r, :].astype(F32)) * pa[j]
               + jax.nn.sigmoid(gb_ref[r, :].astype(F32)) * pb[j]).astype(BF16) for j, r in enumerate(halves)]
    po = [_dot(merged[j], wo_ref[...]) for j in range(2)]
    hh, hl = [], []
    for j, r in enumerate(halves):
        x1 = jnp.where(is_ctx, xc_ref[r, :], xl_ref[r, :]) + gate1 * po[j]
        x1_ref[r, :] = x1
        hi, lo = _split2(_rms(x1, n2_ref[...]) * (1.0 + scale2) + shift2)
        h2_ref[r, :] = hi
        hh.append(hi)
        hl.append(lo)
    hh = jnp.concatenate(hh, axis=0)
    hl = jnp.concatenate(hl, axis=0)

    lt = _dot_nt(wrh_ref[...], hh) + _dot_nt(wrl_ref[...], hh) + _dot_nt(wrh_ref[...], hl) + br_ref[...]
    eidx = lax.broadcasted_iota(I32, (N_EXP, TBD), 0)
    vals, idxs, cnt = [], [], jnp.zeros((N_EXP, TBD), F32)
    for _ in range(TOP_K):
        m = jnp.max(lt, axis=0, keepdims=True)
        idx = jnp.min(jnp.where(lt == m, eidx, N_EXP), axis=0, keepdims=True)
        sel = eidx == idx
        vals.append(m)
        idxs.append(idx)
        cnt = cnt + sel.astype(F32)
        lt = jnp.where(sel, -jnp.inf, lt)
    ex = [jnp.exp(v - vals[0]) for v in vals]
    den = ex[0] + ex[1] + ex[2] + ex[3]
    tw_ref[0] = jnp.concatenate([e / den for e in ex], axis=0)
    te_ref[0] = jnp.concatenate(idxs, axis=0)
    hist_ref[0] = jnp.sum(cnt, axis=1, keepdims=True).astype(I32)


def _merge(x_ctx, x_lat, o_f, o_b, rest, yb, mod, hgrn_norm, wa_bf, wb_bf, wo_bf, norm2, wr_hi, wr_lo, b_router):
    full = lambda shape: pl.BlockSpec(shape, lambda i: (0,) * len(shape))
    return pl.pallas_call(
        _merge_kernel,
        out_shape=(jax.ShapeDtypeStruct((T_ALL, D_MODEL), F32),
                   jax.ShapeDtypeStruct((T_ALL, D_MODEL), BF16),
                   jax.ShapeDtypeStruct((NBD, TOP_K, TBD), I32),
                   jax.ShapeDtypeStruct((NBD, TOP_K, TBD), F32),
                   jax.ShapeDtypeStruct((NBD, N_EXP, 1), I32)),
        grid=(NBD,),
        in_specs=_x_specs() + [
                  pl.BlockSpec((TBD, HGRN_W), lambda i: (i, 0)),
                  pl.BlockSpec((TBD, HGRN_W), lambda i: (i, 0)),
                  pl.BlockSpec((TBD, HGRN_W), lambda i: (i, 2)),
                  pl.BlockSpec((TBD, POOL_W), lambda i: (i, 0)),
                  pl.BlockSpec((TBD, D_MODEL), lambda i: (i, 2)),
                  pl.BlockSpec((TBD, D_MODEL), lambda i: (i, 3)),
                  full((MOD_ROWS, 6 * D_MODEL)),
                  full((1, HEAD_V)),
                  full((HGRN_W, D_MODEL)),
                  full((POOL_W, D_MODEL)),
                  full((D_MODEL, D_MODEL)),
                  full((1, D_MODEL)),
                  full((N_EXP, D_MODEL)),
                  full((N_EXP, D_MODEL)),
                  full((N_EXP, 1))],
        out_specs=(pl.BlockSpec((TBD, D_MODEL), lambda i: (i, 0)),
                   pl.BlockSpec((TBD, D_MODEL), lambda i: (i, 0)),
                   pl.BlockSpec((1, TOP_K, TBD), lambda i: (i, 0, 0)),
                   pl.BlockSpec((1, TOP_K, TBD), lambda i: (i, 0, 0)),
                   pl.BlockSpec((1, N_EXP, 1), lambda i: (i, 0, 0))),
        compiler_params=_params(),
        name="merge",
    )(x_ctx, x_lat, o_f, o_b, rest, yb, rest, rest, mod, hgrn_norm, wa_bf, wb_bf, wo_bf, norm2,
      wr_hi, wr_lo, b_router)


def _local_rows(te_ref, loff_ref, tri_ref):
    te = te_ref[0]
    eidx = lax.broadcasted_iota(I32, (N_EXP, TBD), 0)
    sels = [eidx == te[k:k + 1] for k in range(TOP_K)]
    cnt = sels[0].astype(F32)
    for s in sels[1:]:
        cnt = cnt + s.astype(F32)
    base = _dot(cnt.astype(BF16), tri_ref[...]) + loff_ref[0]
    return [jnp.sum(jnp.where(s, base, 0.0), axis=0, keepdims=True) for s in sels]


def _chunk_relative(rows, r0):
    out = []
    for r in rows:
        inside = jnp.logical_and(r >= r0, r < r0 + SORT_CHUNK)
        out.append(jnp.where(inside, r - r0, -1.0).astype(BF16))
    return out


def _segment_copies(make_copy, local_off, global_off, units):
    big_rows = SUBLANES << SEG_SMALL_BITS
    big = units >> SEG_SMALL_BITS

    def piece(p, carry):
        off = pl.multiple_of(p * big_rows, big_rows)
        make_copy(pl.multiple_of(local_off + off, SUBLANES), pl.multiple_of(global_off + off, SUBLANES),
                  big_rows).start()
        return carry

    lax.fori_loop(0, big, piece, 0)
    done = big * big_rows
    for j in reversed(range(SEG_SMALL_BITS)):
        rows = SUBLANES << j
        low = done + ((units >> (j + 1)) & ((1 << (SEG_SMALL_BITS - 1 - j)) - 1)) * (2 * rows)

        @pl.when(((units >> j) & 1) == 1)
        def _():
            make_copy(pl.multiple_of(local_off + low, SUBLANES), pl.multiple_of(global_off + low, SUBLANES),
                      rows).start()


def _pack_pairs(x):
    half = D_MODEL // 2
    lo = lax.bitcast_convert_type(x[:, :half], U32) >> 16
    hi = lax.bitcast_convert_type(x[:, half:], U32) & jnp.uint32(0xFFFF0000)
    return hi | lo


def _unpack_pairs(p):
    lo = lax.bitcast_convert_type(p << 16, F32).astype(BF16)
    hi = lax.bitcast_convert_type(p & jnp.uint32(0xFFFF0000), F32).astype(BF16)
    return jnp.concatenate([lo, hi], axis=1)


def _block_rows(loff_s, seg_s, b):
    last = b * N_EXP + N_EXP - 1
    return pl.multiple_of(loff_s[last] + seg_s[last], SUBLANES)


def _dispatch_kernel(loff_s, seg_s, gbase_s, tail_s, h2_ref, te_ref, loffv_ref, tri_ref, iota_ref, xs_hbm,
                     loc, zeros, sem, sem_z):
    b = pl.program_id(0)
    slot = b % 2

    def wait_block(blk, s):
        n = _block_rows(loff_s, seg_s, blk)
        pltpu.make_async_copy(loc.at[s, pl.ds(0, n)], xs_hbm.at[pl.ds(0, n)], sem.at[s]).wait()

    lrow = _local_rows(te_ref, loffv_ref, tri_ref)

    @pl.when(b >= 2)
    def _():
        wait_block(b - 2, slot)

    for r0 in range(0, LROWS, SORT_CHUNK):
        rel = _chunk_relative(lrow, r0)
        p = jnp.zeros((SORT_CHUNK, TBD), BF16)
        for k in reversed(range(TOP_K)):
            p = jnp.where(iota_ref[...] == rel[k], jnp.ones_like(p), p)
        loc[slot, r0:r0 + SORT_CHUNK, :] = _pack_pairs(_dot(p, h2_ref[...]))

    def out_copy(a, g, size):
        return pltpu.make_async_copy(loc.at[slot, pl.ds(a, size)], xs_hbm.at[pl.ds(g, size)], sem.at[slot])

    def body(e, carry):
        idx = b * N_EXP + e
        _segment_copies(out_copy, loff_s[idx], gbase_s[idx], seg_s[idx] // SUBLANES)
        return carry

    lax.fori_loop(0, N_EXP, body, 0)

    @pl.when(b == NBD - 1)
    def _():
        zeros[...] = jnp.zeros_like(zeros)

        def zero_copy(a, g, size):
            return pltpu.make_async_copy(zeros.at[pl.ds(a, size)], xs_hbm.at[pl.ds(g, size)], sem_z)

        def zbody(e, ztot):
            _segment_copies(zero_copy, 0, tail_s[e], tail_s[N_EXP + e] // SUBLANES)
            return ztot + tail_s[N_EXP + e]

        def tbody(t, carry):
            pltpu.make_async_copy(zeros, xs_hbm.at[pl.ds(pl.multiple_of(t * TM, TM), TM)], sem_z).start()
            return carry

        n_used = tail_s[2 * N_EXP]
        lax.fori_loop(n_used, N_TILES, tbody, 0)
        ztot = lax.fori_loop(0, N_EXP, zbody, 0) + (N_TILES - n_used) * TM
        ztot = pl.multiple_of(ztot, SUBLANES)

        @pl.when(ztot > 0)
        def _():
            pltpu.make_async_copy(xs_hbm.at[pl.ds(0, ztot)], xs_hbm.at[pl.ds(0, ztot)], sem_z).wait()

        wait_block(b - 1, 1 - slot)
        wait_block(b, slot)


def _dispatch(tables, h2, top_e, tri, row_iota):
    loff_s, seg_s, gbase_s, tail_s, loff_v = tables
    grid_spec = pltpu.PrefetchScalarGridSpec(
        num_scalar_prefetch=4,
        grid=(NBD,),
        in_specs=[pl.BlockSpec((TBD, D_MODEL), lambda i, *_: (i, 0)),
                  pl.BlockSpec((1, TOP_K, TBD), lambda i, *_: (i, 0, 0)),
                  pl.BlockSpec((1, N_EXP, 1), lambda i, *_: (i, 0, 0)),
                  pl.BlockSpec((TBD, TBD), lambda i, *_: (0, 0)),
                  pl.BlockSpec((SORT_CHUNK, TBD), lambda i, *_: (0, 0))],
        out_specs=pl.BlockSpec(memory_space=pl.ANY),
        scratch_shapes=[pltpu.VMEM((2, LROWS, D_MODEL // 2), U32),
                        pltpu.VMEM((TM, D_MODEL // 2), U32),
                        pltpu.SemaphoreType.DMA((2,)),
                        pltpu.SemaphoreType.DMA])
    return pl.pallas_call(
        _dispatch_kernel,
        out_shape=jax.ShapeDtypeStruct((N_TILES * TM, D_MODEL // 2), U32),
        grid_spec=grid_spec,
        compiler_params=_params(),
        name="dispatch",
    )(loff_s, seg_s, gbase_s, tail_s, h2, top_e, loff_v, tri, row_iota)


def _moe_kernel(te_ref, first_ref, par_ref, next_ref, nv_ref, nu_ref, xs_ref, bgu_ref, bd_ref, wgu_hbm, wd_hbm, o_ref,
                wgu_st, wd_st, wgu_bf, wd_bf, sem):
    i = pl.program_id(0)

    def fetch(e, s):
        return (pltpu.make_async_copy(wgu_hbm.at[e], wgu_st.at[s], sem.at[0, s]),
                pltpu.make_async_copy(wd_hbm.at[e], wd_st.at[s], sem.at[1, s]))

    @pl.when(i < nu_ref[0])
    def _():
        @pl.when(first_ref[i] == 1)
        def _():
            s = par_ref[i]

            @pl.when(i == 0)
            def _():
                for cp in fetch(te_ref[0], 0):
                    cp.start()

            for cp in fetch(te_ref[i], s):
                cp.wait()

            @pl.when(next_ref[i] >= 0)
            def _():
                for cp in fetch(next_ref[i], 1 - s):
                    cp.start()

            wgu_bf[...] = wgu_st[s].astype(BF16)
            wd_bf[...] = wd_st[s].astype(BF16)

        def gate_up(r):
            return _dot(_unpack_pairs(xs_ref[r, :]), wgu_bf[...]) + bgu_ref[0]

        def activation(gu):
            gate = jnp.minimum(gu[:, :D_FF], SWIGLU_LIMIT)
            up = jnp.clip(gu[:, D_FF:], -SWIGLU_LIMIT, SWIGLU_LIMIT)
            return ((up + 1.0) * gate * jax.nn.sigmoid(SWIGLU_ALPHA * gate)).astype(BF16)

        def down(r, act):
            out = _dot(act, wd_bf[...]) + bd_ref[0]
            o_ref[r, :] = _pack_pairs(out.astype(BF16).astype(F32))

        def run(sizes):
            starts = [sum(sizes[:j]) for j in range(len(sizes))]
            pieces = [slice(a, a + n) for a, n in zip(starts, sizes)]
            gu = [gate_up(r) for r in pieces]
            for r, g in zip(pieces, gu):
                down(r, activation(g))
            done = sum(sizes)
            if done < TM:
                o_ref[done:, :] = jnp.zeros((TM - done, D_MODEL // 2), U32)

        nv = nv_ref[i]
        for lo, hi, sizes in MOE_PIECES:
            pl.when(jnp.logical_and(nv > lo, nv <= hi))(functools.partial(run, sizes))

    @pl.when(i >= nu_ref[0])
    def _():
        o_ref[...] = jnp.zeros_like(o_ref)


def _moe(tile_tables, xs, w_gate_up, b_gate_up, w_down, b_down):
    nsp = len(tile_tables)
    row_tile = lambda i, *s: (jnp.minimum(i, s[nsp - 1][0] - 1), 0)
    grid_spec = pltpu.PrefetchScalarGridSpec(
        num_scalar_prefetch=nsp,
        grid=(N_TILES,),
        in_specs=[pl.BlockSpec((TM, D_MODEL // 2), row_tile),
                  pl.BlockSpec((1, 1, 2 * D_FF), lambda i, te, *_: (te[i], 0, 0)),
                  pl.BlockSpec((1, 1, D_MODEL), lambda i, te, *_: (te[i], 0, 0)),
                  pl.BlockSpec(memory_space=pl.ANY),
                  pl.BlockSpec(memory_space=pl.ANY)],
        out_specs=pl.BlockSpec((TM, D_MODEL // 2), lambda i, *_: (i, 0)),
        scratch_shapes=[pltpu.VMEM((2, D_MODEL, 2 * D_FF), F32),
                        pltpu.VMEM((2, D_FF, D_MODEL), F32),
                        pltpu.VMEM((D_MODEL, 2 * D_FF), BF16),
                        pltpu.VMEM((D_FF, D_MODEL), BF16),
                        pltpu.SemaphoreType.DMA((2, 2))])
    return pl.pallas_call(
        _moe_kernel,
        out_shape=jax.ShapeDtypeStruct((N_TILES * TM, D_MODEL // 2), U32),
        grid_spec=grid_spec,
        compiler_params=_params(),
        name="moe",
    )(*tile_tables, xs, b_gate_up.reshape(N_EXP, 1, 2 * D_FF), b_down.reshape(N_EXP, 1, D_MODEL),
      w_gate_up, w_down)


def _final_kernel(loff_s, seg_s, gbase_s, ys_hbm, x1_ref, te_ref, tw_ref, loffv_ref, tri_ref, iota_ref, mod_ref, fn_ref,
                  oc_ref, ol_ref, loc, sem):
    b = pl.program_id(0)
    slot = b % 2

    def start_block(blk, s):
        def in_copy(a, g, size):
            return pltpu.make_async_copy(ys_hbm.at[pl.ds(g, size)], loc.at[s, pl.ds(a, size)], sem.at[s])

        def body(e, carry):
            idx = blk * N_EXP + e
            _segment_copies(in_copy, loff_s[idx], gbase_s[idx], seg_s[idx] // SUBLANES)
            return carry

        lax.fori_loop(0, N_EXP, body, 0)

    @pl.when(b == 0)
    def _():
        loc[...] = jnp.zeros_like(loc)
        start_block(0, 0)

    @pl.when(b + 1 < NBD)
    def _():
        start_block(b + 1, 1 - slot)

    lrow = _local_rows(te_ref, loffv_ref, tri_ref)
    tw = tw_ref[0]
    wts = [tw[k:k + 1].astype(BF16) for k in range(TOP_K)]

    n = _block_rows(loff_s, seg_s, b)
    pltpu.make_async_copy(ys_hbm.at[pl.ds(0, n)], loc.at[slot, pl.ds(0, n)], sem.at[slot]).wait()

    y = None
    for r0 in range(0, LROWS, SORT_CHUNK):
        rel = _chunk_relative(lrow, r0)
        pw = jnp.zeros((SORT_CHUNK, TBD), BF16)
        for k in reversed(range(TOP_K)):
            pw = jnp.where(iota_ref[...] == rel[k], wts[k], pw)
        part = _dot_tn(pw, _unpack_pairs(loc[slot, r0:r0 + SORT_CHUNK, :]))
        y = part if y is None else y + part

    row = _mod_row(b)
    gate2 = mod_ref[pl.ds(row, 1), pl.ds(5 * D_MODEL, D_MODEL)]
    out = _rms(x1_ref[...] + gate2 * y, fn_ref[...])

    @pl.when(b < NBD_CTX)
    def _():
        oc_ref[...] = out

    @pl.when(b >= NBD_CTX)
    def _():
        ol_ref[...] = out


def _final(tables, ys, x1, top_e, top_w, tri, col_iota, mod, final_norm):
    loff_s, seg_s, gbase_s, _, loff_v = tables
    grid_spec = pltpu.PrefetchScalarGridSpec(
        num_scalar_prefetch=3,
        grid=(NBD,),
        in_specs=[pl.BlockSpec(memory_space=pl.ANY),
                  pl.BlockSpec((TBD, D_MODEL), lambda i, *_: (i, 0)),
                  pl.BlockSpec((1, TOP_K, TBD), lambda i, *_: (i, 0, 0)),
                  pl.BlockSpec((1, TOP_K, TBD), lambda i, *_: (i, 0, 0)),
                  pl.BlockSpec((1, N_EXP, 1), lambda i, *_: (i, 0, 0)),
                  pl.BlockSpec((TBD, TBD), lambda i, *_: (0, 0)),
                  pl.BlockSpec((SORT_CHUNK, TBD), lambda i, *_: (0, 0)),
                  pl.BlockSpec((MOD_ROWS, 6 * D_MODEL), lambda i, *_: (0, 0)),
                  pl.BlockSpec((1, D_MODEL), lambda i, *_: (0, 0))],
        out_specs=(pl.BlockSpec((TBD, D_MODEL), lambda i, *_: (jnp.minimum(i, NBD_CTX - 1), 0)),
                   pl.BlockSpec((TBD, D_MODEL), lambda i, *_: (jnp.maximum(i - NBD_CTX, 0), 0))),
        scratch_shapes=[pltpu.VMEM((2, LROWS, D_MODEL // 2), U32),
                        pltpu.SemaphoreType.DMA((2,))])
    return pl.pallas_call(
        _final_kernel,
        out_shape=(jax.ShapeDtypeStruct((T_CTX, D_MODEL), F32),
                   jax.ShapeDtypeStruct((T_LAT, D_MODEL), F32)),
        grid_spec=grid_spec,
        compiler_params=_params(),
        name="final",
    )(loff_s, seg_s, gbase_s, ys, x1, top_e, top_w, loff_v, tri, col_iota, mod, final_norm)


def _dispatch_tables(hist):
    hist = hist.reshape(NBD, N_EXP)
    seg = ((hist + SUBLANES - 1) // SUBLANES) * SUBLANES
    loff = jnp.cumsum(seg, axis=1) - seg
    rows_e = jnp.sum(seg, axis=0)
    region = ((rows_e + TM - 1) // TM) * TM
    region_end = jnp.cumsum(region)
    region_start = region_end - region
    gbase = region_start[None, :] + jnp.cumsum(seg, axis=0) - seg
    n_used = (region_end[-1] // TM).astype(I32)
    tail = jnp.concatenate([region_start + rows_e, region - rows_e, n_used.reshape(1)])
    start = jnp.arange(N_TILES, dtype=I32) * TM
    tile_e = jnp.sum((start[:, None] >= region_end[None, :]).astype(I32), axis=1)
    tile_e = jnp.minimum(tile_e, tile_e[jnp.maximum(n_used - 1, 0)])
    first = jnp.concatenate([jnp.ones((1,), I32), (tile_e[1:] != tile_e[:-1]).astype(I32)])
    parity = (jnp.cumsum(first) - 1) % 2
    later = jnp.where(tile_e[None, :] > tile_e[:, None], tile_e[None, :], N_EXP)
    nxt = jnp.min(later, axis=1)
    nxt = jnp.where(nxt == N_EXP, -1, nxt)
    flat = lambda a: a.reshape(-1).astype(I32)
    tables = (flat(loff), flat(seg), flat(gbase), flat(tail), loff.astype(F32).reshape(NBD, N_EXP, 1))
    mine = tile_e[:, None] == jnp.arange(N_EXP, dtype=I32)[None, :]
    data_end = jnp.sum(jnp.where(mine, (region_start + rows_e)[None, :], 0), axis=1)
    tile_nv = jnp.clip(data_end - start, 0, TM)
    tile_tables = (flat(tile_e), flat(first), flat(parity), flat(nxt), flat(tile_nv), n_used.reshape(1))
    return tables, tile_tables


def kernel(x_prompt, x_sample, state_hgrn, c, c_ctx, w_ada, b_ada, norm1, w_in, hgrn_lb, hgrn_norm, w_pool,
           pool_scale, w_branch_a, w_branch_b, w_out, norm2, w_router, b_router, w_gate_up, b_gate_up,
           w_down, b_down, final_norm):
    x_ctx = x_prompt.reshape(T_CTX, D_MODEL)
    x_lat = x_sample.reshape(T_LAT, D_MODEL)
    cc = jnp.zeros((MOD_ROWS, D_MODEL), F32).at[:N_LAT_SEQ].set(c).at[N_LAT_SEQ].set(c_ctx)
    mod = _ada(cc, w_ada[0], b_ada)

    gates, rest = _inproj(x_ctx, x_lat, mod, norm1, w_in[0].astype(BF16))

    mall, masks = _hgrn_consts()
    o_f, o_b, new_state = _hgrn(gates, rest, hgrn_lb, state_hgrn[:, 0], mall, masks)

    a_pool, cnt_pool = _pool_consts()
    yb = _pool(rest, a_pool, cnt_pool, w_pool[0].astype(BF16), pool_scale)

    wr_t = w_router[0].T
    wr_hi = wr_t.astype(BF16)
    wr_lo = (wr_t - wr_hi.astype(F32)).astype(BF16)
    tri = jnp.asarray(np.triu(np.ones((TBD, TBD), np.float32), 1), BF16)
    x1, h2, top_e, top_w, hist = _merge(
        x_ctx, x_lat, o_f, o_b, rest, yb, mod, hgrn_norm, w_branch_a[0].astype(BF16), w_branch_b[0].astype(BF16),
        w_out[0].astype(BF16), norm2, wr_hi, wr_lo, b_router.reshape(N_EXP, 1))

    tables, tile_tables = _dispatch_tables(hist)
    row_iota = jnp.asarray(np.broadcast_to(np.arange(SORT_CHUNK, dtype=np.float32)[:, None], (SORT_CHUNK, TBD)), BF16)
    xs = _dispatch(tables, h2, top_e, tri, row_iota)
    ys = _moe(tile_tables, xs, w_gate_up[0], b_gate_up[0], w_down[0], b_down[0])
    y_ctx, y_lat = _final(tables, ys, x1, top_e, top_w, tri, row_iota, mod, final_norm.reshape(1, D_MODEL))
    y_prompt = y_ctx.reshape(N_CTX_SEQ, CTX_LEN, D_MODEL)
    y_sample = y_lat.reshape(N_LAT_SEQ, LAT_LEN, D_MODEL)
    return y_prompt, y_sample, new_state[:, None]
```

```python
import functools

import numpy as np
import jax
import jax.numpy as jnp
from jax import lax
from jax.experimental import pallas as pl
from jax.experimental.pallas import tpu as pltpu

F32 = jnp.float32
BF16 = jnp.bfloat16
I32 = jnp.int32
U32 = jnp.uint32

D_MODEL = 1024
N_CTX_SEQ, CTX_LEN = 32, 256
N_LAT_SEQ, LAT_LEN = 4, 2048
T_CTX = N_CTX_SEQ * CTX_LEN
T_LAT = N_LAT_SEQ * LAT_LEN
T_ALL = T_CTX + T_LAT
SUBLANES, LANES = 8, 128
TB = 256
NB = T_ALL // TB
NB_CTX = T_CTX // TB
LAT_BLOCKS = LAT_LEN // TB
HEADS, HEAD_K, HEAD_V = 4, 128, 128
HGRN_W = HEADS * HEAD_V
POOL_WINDOWS = (2, 4, 8, 16)
POOL_G = 128
POOL_W = len(POOL_WINDOWS) * POOL_G
GRID_W = 64
GRID_H = LAT_LEN // GRID_W
IN_W = 5 * HGRN_W + POOL_W + 2 * D_MODEL
GATE_W = 2 * HGRN_W
REST_W = IN_W - GATE_W
N_EXP, TOP_K, D_FF = 32, 4, 1024
SWIGLU_LIMIT = 7.0
SWIGLU_ALPHA = 1.702
EPS = 1e-6
LOG2_E = 1.4426950408889634
CHUNK = 64
N_LEVELS = 6
EXP_ROWS = (N_LEVELS + 2) * CHUNK
MM_BLOCKS = (0, 4, 5, 6)
MM_ROWS = len(MM_BLOCKS) * CHUNK
COARSE_LEVELS = ((1, 32), (2, 16), (3, 8))
TM = 512
MOE_PIECES = ((0, 128, (128,)), (128, 256, (256,)), (256, 384, (256, 128)), (384, TM, (256, 256)))
TBD = 512
NBD = T_ALL // TBD
NBD_CTX = T_CTX // TBD
SORT_CHUNK = 256
LROWS = TBD * TOP_K + N_EXP * SUBLANES
SEG_SMALL_BITS = 3
N_TILES = -(-(T_ALL * TOP_K + NBD * N_EXP * (SUBLANES - 1) + N_EXP * (TM - 1)) // TM)
MOD_ROWS = 8
VMEM_LIMIT = 56 * 1024 * 1024


def _params(sem=("arbitrary",)):
    return pltpu.CompilerParams(dimension_semantics=sem, vmem_limit_bytes=VMEM_LIMIT)


def _dot(a, b):
    return jnp.dot(a, b, preferred_element_type=F32)


def _dot_nt(a, b):
    return lax.dot_general(a, b, (((1,), (1,)), ((), ())), preferred_element_type=F32)


def _dot_tn(a, b):
    return lax.dot_general(a, b, (((0,), (0,)), ((), ())), preferred_element_type=F32)


def _split2(x):
    hi = x.astype(BF16)
    lo = (x - hi.astype(F32)).astype(BF16)
    return hi, lo


def _mod_row(i):
    return jnp.where(i < NBD_CTX, N_LAT_SEQ, (i - NBD_CTX) // (LAT_LEN // TBD))


def _ada_kernel(c_ref, w_ref, b_ref, o_ref):
    c = c_ref[...]
    s = c * jax.nn.sigmoid(c)
    o_ref[...] = jnp.dot(s, w_ref[...], preferred_element_type=F32,
                         precision=lax.Precision.HIGHEST) + b_ref[...]


def _ada(cc, w_ada, b_ada):
    nblk = 1536
    return pl.pallas_call(
        _ada_kernel,
        out_shape=jax.ShapeDtypeStruct((MOD_ROWS, 6 * D_MODEL), F32),
        grid=(6 * D_MODEL // nblk,),
        in_specs=[pl.BlockSpec((MOD_ROWS, D_MODEL), lambda j: (0, 0)),
                  pl.BlockSpec((D_MODEL, nblk), lambda j: (0, j)),
                  pl.BlockSpec((1, nblk), lambda j: (0, j))],
        out_specs=pl.BlockSpec((MOD_ROWS, nblk), lambda j: (0, j)),
        compiler_params=_params(),
        name="ada",
    )(cc, w_ada, b_ada)


def _rms(x, g):
    ms = jnp.mean(x * x, axis=-1, keepdims=True)
    return x * lax.rsqrt(ms + EPS) * g


def _x_specs():
    return [pl.BlockSpec((TBD, D_MODEL), lambda i, *_: (jnp.minimum(i, NBD_CTX - 1), 0)),
            pl.BlockSpec((TBD, D_MODEL), lambda i, *_: (jnp.maximum(i - NBD_CTX, 0), 0))]


def _x_block(xc_ref, xl_ref):
    return jnp.where(pl.program_id(0) < NBD_CTX, xc_ref[...], xl_ref[...])


def _inproj_kernel(xc_ref, xl_ref, mod_ref, n1_ref, w_ref, og_ref, or_ref):
    row = _mod_row(pl.program_id(0))
    shift = mod_ref[pl.ds(row, 1), pl.ds(0, D_MODEL)]
    scale = mod_ref[pl.ds(row, 1), pl.ds(D_MODEL, D_MODEL)]
    h = (_rms(_x_block(xc_ref, xl_ref), n1_ref[...]) * (1.0 + scale) + shift).astype(BF16)
    og_ref[...] = _dot(h, w_ref[:, HGRN_W:HGRN_W + GATE_W])
    or_ref[:, :HGRN_W] = _dot(h, w_ref[:, :HGRN_W]).astype(BF16)
    or_ref[:, HGRN_W:] = _dot(h, w_ref[:, HGRN_W + GATE_W:]).astype(BF16)


def _inproj(x_ctx, x_lat, mod, norm1, w_in_bf):
    return pl.pallas_call(
        _inproj_kernel,
        out_shape=(jax.ShapeDtypeStruct((T_ALL, GATE_W), F32),
                   jax.ShapeDtypeStruct((T_ALL, REST_W), BF16)),
        grid=(NBD,),
        in_specs=_x_specs() + [
                  pl.BlockSpec((MOD_ROWS, 6 * D_MODEL), lambda i: (0, 0)),
                  pl.BlockSpec((1, D_MODEL), lambda i: (0, 0)),
                  pl.BlockSpec((D_MODEL, IN_W), lambda i: (0, 0))],
        out_specs=(pl.BlockSpec((TBD, GATE_W), lambda i: (i, 0)),
                   pl.BlockSpec((TBD, REST_W), lambda i: (i, 0))),
        compiler_params=_params(),
        name="inproj",
    )(x_ctx, x_lat, mod, norm1, w_in_bf)


def _hgrn_consts():
    c = CHUNK
    t = np.arange(c)[:, None]
    u = np.arange(c)[None, :]
    blocks = [u <= t]
    masks = [np.eye(c, dtype=bool)]
    h = c // 2
    while h >= 1:
        bi = t // h
        upper = (bi % 2) == 1
        e_up = (u >= bi * h) & (u <= t)
        e_lo = (u > t) & (u <= bi * h + h - 1)
        blocks.append(np.where(upper, e_up, e_lo))
        masks.append(((t // (2 * h)) == (u // (2 * h))) & (((t // h) % 2) == 1) & (((u // h) % 2) == 0))
        h //= 2
    blocks.append(u > t)
    m_f = np.stack(blocks).astype(np.float32)
    k_f = np.stack(masks).astype(np.float32)
    m_b = m_f[:, ::-1, ::-1]
    k_b = k_f[:, ::-1, ::-1]
    sel = list(MM_BLOCKS)
    m = np.stack([m_f[sel].reshape(MM_ROWS, c), m_b[sel].reshape(MM_ROWS, c)])
    m3 = np.concatenate([m, m, m], axis=2)
    return jnp.asarray(m3, BF16), jnp.asarray(np.stack([k_f, k_b]), F32)


def _hgrn_block(dirs, lb, mall_ref, mask_ref, st_ref, z_ref, k_ref, sc_ref, run_ref):
    c = CHUNK
    nchunk = TB // c
    units = [(d, h) for d in range(2) for h in range(HEADS)]
    sl = [slice(h * HEAD_K, (h + 1) * HEAD_K) for h in range(HEADS)]

    def rows(ci, d):
        r0 = ci * c if d == 0 else (nchunk - 1 - ci) * c
        return slice(r0, r0 + c)

    def exponents(ci):
        s = ci % 2
        for d in range(2):
            f = lb[d:d + 1] + (1.0 - lb[d:d + 1]) * jax.nn.sigmoid(dirs[d][1][rows(ci, d), :])
            k_ref[s, d] = 1.0 - f
            k_ref[s, 2 + d] = dirs[d][0][rows(ci, d), :].astype(F32)
            g = jnp.log(f) * LOG2_E
            g1 = g.astype(BF16)
            r1 = g - g1.astype(F32)
            g2 = r1.astype(BF16)
            g3 = (r1 - g2.astype(F32)).astype(BF16)
            gsplit = jnp.concatenate([g1, g2, g3], axis=0)
            ex = _dot(mall_ref[d], gsplit)
            run = ex[0:c]
            run_ref[d] = run
            z_ref[s, d, 0:c] = jnp.exp2(run)
            for j, blk in enumerate(MM_BLOCKS[1:]):
                z_ref[s, d, blk * c:(blk + 1) * c] = jnp.exp2(ex[(j + 1) * c:(j + 2) * c])
            for blk, h in COARSE_LEVELS:
                for base in range(0, c, 2 * h):
                    ref = run_ref[d, base + h - 1 + d:base + h + d, :]
                    if d == 0:
                        first, second = ref - run[base:base + h], run[base + h:base + 2 * h] - ref
                    else:
                        first, second = run[base:base + h] - ref, ref - run[base + h:base + 2 * h]
                    z_ref[s, d, blk * c + base:blk * c + base + h] = jnp.exp2(first)
                    z_ref[s, d, blk * c + base + h:blk * c + base + 2 * h] = jnp.exp2(second)
            end = run_ref[d, c - 1:c, :] if d == 0 else run_ref[d, 0:1, :]
            z_ref[s, d, (N_LEVELS + 1) * c:] = jnp.exp2(end - run)

    def q_of(ci, d, h):
        return k_ref[ci % 2, 2 + d, :, sl[h]]

    def v_of(ci, d, h):
        return dirs[d][2][rows(ci, d), sl[h]].astype(BF16)

    def qz(ci, d, h, blk):
        return (q_of(ci, d, h) * z_ref[ci % 2, d, blk * c:(blk + 1) * c, sl[h]]).astype(BF16)

    def kz(ci, d, h, blk):
        return (k_ref[ci % 2, d, :, sl[h]] * z_ref[ci % 2, d, blk * c:(blk + 1) * c, sl[h]]).astype(BF16)

    def levels(ci):
        for d, h in units:
            q = q_of(ci, d, h).astype(F32)
            k = k_ref[ci % 2, d, :, sl[h]]
            k_next = pltpu.roll(k, 1 if d == 0 else c - 1, 0)
            zq = q * z_ref[ci % 2, d, N_LEVELS * c:(N_LEVELS + 1) * c, sl[h]]
            diag = jnp.sum(q * k, axis=1, keepdims=True)
            near = jnp.sum(zq * k_next, axis=1, keepdims=True)
            sc_ref[d, h] = mask_ref[d, 0] * diag + mask_ref[d, N_LEVELS] * near
        for lev in range(N_LEVELS - 1):
            for d, h in units:
                sc_ref[d, h] += mask_ref[d, lev + 1] * _dot_nt(qz(ci, d, h, lev + 1), kz(ci, d, h, lev + 1))

    def tail(ci):
        for d, h in units:
            o = (_dot_nt(qz(ci, d, h, 0), st_ref[d, h].astype(BF16))
                 + _dot(sc_ref[d, h].astype(BF16), v_of(ci, d, h)))
            dirs[d][3][rows(ci, d), sl[h]] = o * (HEAD_K ** -0.5)
        for d, h in units:
            tot_row = c - 1 if d == 0 else 0
            decay = z_ref[ci % 2, d, tot_row:tot_row + 1, sl[h]]
            st_ref[d, h] = st_ref[d, h] * decay + _dot_tn(v_of(ci, d, h), kz(ci, d, h, N_LEVELS + 1))

    exponents(0)
    for ci in range(nchunk):
        levels(ci)
        if ci + 1 < nchunk:
            exponents(ci + 1)
        tail(ci)


def _hgrn_kernel(qf_ref, ff_ref, vf_ref, qb_ref, fb_ref, vb_ref, lbraw_ref, s0_ref, mall_ref, mask_ref,
                 of_ref, ob_ref, sout_hbm, st_ref, stage_ref, z_ref, k_ref, sc_ref, run_ref, sem):
    i = pl.program_id(0)
    j = (i - NB_CTX) % LAT_BLOCKS
    is_ctx = i < NB_CTX

    @pl.when(is_ctx)
    def _():
        st_ref[...] = jnp.zeros_like(st_ref)

    @pl.when(jnp.logical_and(jnp.logical_not(is_ctx), j == 0))
    def _():
        for d in range(2):
            for h in range(HEADS):
                st_ref[d, h] = s0_ref[0, d, h].T

    a0 = lbraw_ref[0]
    a1 = lbraw_ref[1]
    mx = jnp.maximum(a0, a1)
    e0 = jnp.exp(a0 - mx)
    e1 = jnp.exp(a1 - mx)
    lb = e0 / (e0 + e1)

    dirs = ((qf_ref, ff_ref, vf_ref, of_ref), (qb_ref, fb_ref, vb_ref, ob_ref))
    _hgrn_block(dirs, lb, mall_ref, mask_ref, st_ref, z_ref, k_ref, sc_ref, run_ref)

    @pl.when(is_ctx)
    def _():
        for d in range(2):
            for h in range(HEADS):
                stage_ref[d, h] = st_ref[d, h].T
        cp = pltpu.make_async_copy(stage_ref, sout_hbm.at[i], sem)
        cp.start()
        cp.wait()


def _bwd_block(i):
    j = (i - NB_CTX) % LAT_BLOCKS
    return jnp.where(i < NB_CTX, i, i - j + (LAT_BLOCKS - 1 - j))


def _hgrn(gates, rest, hgrn_lb, s0, mall, masks):
    nh = HGRN_W
    fwd = lambda col: pl.BlockSpec((TB, nh), lambda i: (i, col))
    bwd = lambda col: pl.BlockSpec((TB, nh), lambda i: (_bwd_block(i), col))
    lat_seq = lambda i: jnp.clip((i - NB_CTX) // LAT_BLOCKS, 0, N_LAT_SEQ - 1)
    return pl.pallas_call(
        _hgrn_kernel,
        out_shape=(jax.ShapeDtypeStruct((T_ALL, nh), F32),
                   jax.ShapeDtypeStruct((T_ALL, nh), F32),
                   jax.ShapeDtypeStruct((N_CTX_SEQ, 2, HEADS, HEAD_K, HEAD_V), F32)),
        grid=(NB,),
        in_specs=[fwd(0), fwd(0), fwd(1), bwd(0), bwd(1), bwd(1),
                  pl.BlockSpec((2, 2, nh), lambda i: (0, 0, 0)),
                  pl.BlockSpec((1, 2, HEADS, HEAD_K, HEAD_V), lambda i: (lat_seq(i), 0, 0, 0, 0)),
                  pl.BlockSpec((2, MM_ROWS, 3 * CHUNK), lambda i: (0, 0, 0)),
                  pl.BlockSpec((2, N_LEVELS + 1, CHUNK, CHUNK), lambda i: (0, 0, 0, 0))],
        out_specs=(pl.BlockSpec((TB, nh), lambda i: (i, 0)),
                   pl.BlockSpec((TB, nh), lambda i: (_bwd_block(i), 0)),
                   pl.BlockSpec(memory_space=pl.ANY)),
        scratch_shapes=[pltpu.VMEM((2, HEADS, HEAD_V, HEAD_K), F32),
                        pltpu.VMEM((2, HEADS, HEAD_K, HEAD_V), F32),
                        pltpu.VMEM((2, 2, EXP_ROWS, HGRN_W), F32),
                        pltpu.VMEM((2, 4, CHUNK, HGRN_W), F32),
                        pltpu.VMEM((2, HEADS, CHUNK, CHUNK), F32),
                        pltpu.VMEM((2, CHUNK, HGRN_W), F32),
                        pltpu.SemaphoreType.DMA],
        compiler_params=_params(),
        name="hgrn",
    )(rest, gates, rest, rest, gates, rest, hgrn_lb, s0, mall, masks)


def _window_bounds(n, w):
    pos = np.arange(n)
    lo = np.clip(pos - w // 2, 0, n - 1)
    hi = np.clip(pos - w // 2 + w - 1, 0, n - 1)
    return lo, hi


def _pool_consts():
    seq, img, cnt_seq, cnt_col = [], [], [], []
    for w in POOL_WINDOWS:
        lo, hi = _window_bounds(CTX_LEN, w)
        u = np.arange(CTX_LEN)[None, :]
        seq.append((u >= lo[:, None]) & (u <= hi[:, None]))
        cnt_seq.append(hi - lo + 1)
        lo, hi = _window_bounds(GRID_W, w)
        u = np.arange(GRID_W)[None, :]
        band = (u >= lo[:, None]) & (u <= hi[:, None])
        img.append(np.kron(np.eye(TB // GRID_W, dtype=bool), band))
        cnt_col.append(np.tile(hi - lo + 1, TB // GRID_W))
    a = np.stack([np.stack(seq), np.stack(img)]).astype(np.float32)
    cnt = np.stack([np.stack(cnt_seq), np.stack(cnt_col)]).astype(np.float32)
    cnt = np.broadcast_to(cnt[..., None], cnt.shape + (POOL_G,))
    return jnp.asarray(a, BF16), jnp.asarray(cnt, F32)


POOL_ROWS = LAT_LEN


def _pool_kernel(u_ref, a_ref, cnt_ref, wp_ref, ps_ref, o_ref, cp_ref, d_ref):
    i = pl.program_id(0)
    nblk = POOL_ROWS // TB

    def centre(g, r0, nrows, pm):
        sl = slice(g * POOL_G, (g + 1) * POOL_G)
        d_ref[pl.ds(r0, nrows), :] = (pm - u_ref[pl.ds(r0, nrows), sl].astype(F32)).astype(BF16)

    def group_map(g):
        sl = slice(g * POOL_G, (g + 1) * POOL_G)
        o_ref[:, sl] = (_dot(d_ref[...], wp_ref[g]) * ps_ref[:, sl]).astype(o_ref.dtype)

    def window_sum(kind, g, b):
        sl = slice(g * POOL_G, (g + 1) * POOL_G)
        return _dot(a_ref[kind, g], u_ref[pl.ds(b * TB, TB), sl]) / cnt_ref[kind, g]

    @pl.when(i < T_CTX // POOL_ROWS)
    def _():
        for g in range(len(POOL_WINDOWS)):
            for b in range(nblk):
                centre(g, b * TB, TB, window_sum(0, g, b))
            group_map(g)

    @pl.when(i >= T_CTX // POOL_ROWS)
    def _():
        for g, w in enumerate(POOL_WINDOWS):
            for b in range(nblk):
                cp_ref[pl.ds(b * TB, TB), :] = window_sum(1, g, b)
            lo, hi = _window_bounds(GRID_H, w)
            for r in range(GRID_H):
                acc = cp_ref[pl.ds(int(lo[r]) * GRID_W, GRID_W), :]
                for rr in range(int(lo[r]) + 1, int(hi[r]) + 1):
                    acc = acc + cp_ref[pl.ds(rr * GRID_W, GRID_W), :]
                centre(g, r * GRID_W, GRID_W, acc / float(hi[r] - lo[r] + 1))
            group_map(g)


def _pool(rest, a_pool, cnt_pool, w_pool_bf, pool_scale):
    col = 3
    return pl.pallas_call(
        _pool_kernel,
        out_shape=jax.ShapeDtypeStruct((T_ALL, POOL_W), BF16),
        grid=(T_ALL // POOL_ROWS,),
        in_specs=[pl.BlockSpec((POOL_ROWS, POOL_W), lambda i: (i, col)),
                  pl.BlockSpec((2, 4, TB, TB), lambda i: (0, 0, 0, 0)),
                  pl.BlockSpec((2, 4, TB, POOL_G), lambda i: (0, 0, 0, 0)),
                  pl.BlockSpec((4, POOL_G, POOL_G), lambda i: (0, 0, 0)),
                  pl.BlockSpec((1, POOL_W), lambda i: (0, 0))],
        out_specs=pl.BlockSpec((POOL_ROWS, POOL_W), lambda i: (i, 0)),
        scratch_shapes=[pltpu.VMEM((POOL_ROWS, POOL_G), F32),
                        pltpu.VMEM((POOL_ROWS, POOL_G), BF16)],
        compiler_params=_params(),
        name="pool",
    )(rest, a_pool, cnt_pool, w_pool_bf, pool_scale)


def _merge_kernel(xc_ref, xl_ref, of_ref, ob_ref, og_ref, yb_ref, ga_ref, gb_ref, mod_ref, hn_ref, wa_ref, wb_ref,
                  wo_ref, n2_ref, wrh_ref, br_ref,
                  x1_ref, h2_ref, te_ref, tw_ref, hist_ref):
    row = _mod_row(pl.program_id(0))
    gate1 = mod_ref[pl.ds(row, 1), pl.ds(2 * D_MODEL, D_MODEL)]
    shift2 = mod_ref[pl.ds(row, 1), pl.ds(3 * D_MODEL, D_MODEL)]
    scale2 = mod_ref[pl.ds(row, 1), pl.ds(4 * D_MODEL, D_MODEL)]

    halves = [slice(j * (TBD // 2), (j + 1) * (TBD // 2)) for j in range(2)]
    is_ctx = pl.program_id(0) < NBD_CTX

    def head_out(r):
        o = of_ref[r, :] + ob_ref[r, :]
        og = og_ref[r, :].astype(F32)
        ya = jnp.concatenate(
            [_rms(o[:, h * HEAD_V:(h + 1) * HEAD_V], hn_ref[...]) for h in range(HEADS)], axis=1)
        return (ya * (og * jax.nn.sigmoid(og))).astype(BF16)

    ya = [head_out(r) for r in halves]
    pa = [_dot(ya[j], wa_ref[...]) for j in range(2)]
    pb = [_dot(yb_ref[r, :], wb_ref[...]) for r in halves]
    merged = [(jax.nn.sigmoid(ga_ref[r, :].astype(F32)) * pa[j]
               + jax.nn.sigmoid(gb_ref[r, :].astype(F32)) * pb[j]).astype(BF16) for j, r in enumerate(halves)]
    po = [_dot(merged[j], wo_ref[...]) for j in range(2)]
    hh, hl = [], []
    for j, r in enumerate(halves):
        x1 = jnp.where(is_ctx, xc_ref[r, :], xl_ref[r, :]) + gate1 * po[j]
        x1_ref[r, :] = x1
        hi, lo = _split2(_rms(x1, n2_ref[...]) * (1.0 + scale2) + shift2)
        h2_ref[r, :] = hi
        hh.append(hi)
        hl.append(lo)
    hh = jnp.concatenate(hh, axis=0)
    hl = jnp.concatenate(hl, axis=0)

    both = _dot(hh, wrh_ref[...])
    tm = both[:, :LANES] + both[:, LANES:] + _dot(hl, wrh_ref[:, :LANES])
    lt = tm.T[:N_EXP] + br_ref[...]
    eidx = lax.broadcasted_iota(I32, (N_EXP, TBD), 0)
    vals, idxs, cnt = [], [], jnp.zeros((N_EXP, TBD), F32)
    for _ in range(TOP_K):
        m = jnp.max(lt, axis=0, keepdims=True)
        idx = jnp.min(jnp.where(lt == m, eidx, N_EXP), axis=0, keepdims=True)
        sel = eidx == idx
        vals.append(m)
        idxs.append(idx)
        cnt = cnt + sel.astype(F32)
        lt = jnp.where(sel, -jnp.inf, lt)
    ex = [jnp.exp(v - vals[0]) for v in vals]
    den = ex[0] + ex[1] + ex[2] + ex[3]
    tw_ref[0] = jnp.concatenate([e / den for e in ex], axis=0)
    te_ref[0] = jnp.concatenate(idxs, axis=0)
    hist_ref[0] = jnp.sum(cnt, axis=1, keepdims=True).astype(I32)


def _merge(x_ctx, x_lat, o_f, o_b, rest, yb, mod, hgrn_norm, wa_bf, wb_bf, wo_bf, norm2, wr_split, b_router):
    full = lambda shape: pl.BlockSpec(shape, lambda i: (0,) * len(shape))
    return pl.pallas_call(
        _merge_kernel,
        out_shape=(jax.ShapeDtypeStruct((T_ALL, D_MODEL), F32),
                   jax.ShapeDtypeStruct((T_ALL, D_MODEL), BF16),
                   jax.ShapeDtypeStruct((NBD, TOP_K, TBD), I32),
                   jax.ShapeDtypeStruct((NBD, TOP_K, TBD), F32),
                   jax.ShapeDtypeStruct((NBD, N_EXP, 1), I32)),
        grid=(NBD,),
        in_specs=_x_specs() + [
                  pl.BlockSpec((TBD, HGRN_W), lambda i: (i, 0)),
                  pl.BlockSpec((TBD, HGRN_W), lambda i: (i, 0)),
                  pl.BlockSpec((TBD, HGRN_W), lambda i: (i, 2)),
                  pl.BlockSpec((TBD, POOL_W), lambda i: (i, 0)),
                  pl.BlockSpec((TBD, D_MODEL), lambda i: (i, 2)),
                  pl.BlockSpec((TBD, D_MODEL), lambda i: (i, 3)),
                  full((MOD_ROWS, 6 * D_MODEL)),
                  full((1, HEAD_V)),
                  full((HGRN_W, D_MODEL)),
                  full((POOL_W, D_MODEL)),
                  full((D_MODEL, D_MODEL)),
                  full((1, D_MODEL)),
                  full((D_MODEL, 2 * LANES)),
                  full((N_EXP, 1))],
        out_specs=(pl.BlockSpec((TBD, D_MODEL), lambda i: (i, 0)),
                   pl.BlockSpec((TBD, D_MODEL), lambda i: (i, 0)),
                   pl.BlockSpec((1, TOP_K, TBD), lambda i: (i, 0, 0)),
                   pl.BlockSpec((1, TOP_K, TBD), lambda i: (i, 0, 0)),
                   pl.BlockSpec((1, N_EXP, 1), lambda i: (i, 0, 0))),
        compiler_params=_params(),
        name="merge",
    )(x_ctx, x_lat, o_f, o_b, rest, yb, rest, rest, mod, hgrn_norm, wa_bf, wb_bf, wo_bf, norm2,
      wr_split, b_router)


def _local_rows(te_ref, loff_ref, tri_ref):
    te = te_ref[0]
    eidx = lax.broadcasted_iota(I32, (N_EXP, TBD), 0)
    sels = [eidx == te[k:k + 1] for k in range(TOP_K)]
    cnt = sels[0].astype(F32)
    for s in sels[1:]:
        cnt = cnt + s.astype(F32)
    base = _dot(cnt.astype(BF16), tri_ref[...]) + loff_ref[0]
    return [jnp.sum(jnp.where(s, base, 0.0), axis=0, keepdims=True) for s in sels]


def _chunk_relative(rows, r0):
    out = []
    for r in rows:
        inside = jnp.logical_and(r >= r0, r < r0 + SORT_CHUNK)
        out.append(jnp.where(inside, r - r0, -1.0).astype(BF16))
    return out


def _segment_copies(make_copy, local_off, global_off, units):
    big_rows = SUBLANES << SEG_SMALL_BITS
    big = units >> SEG_SMALL_BITS

    def piece(p, carry):
        off = pl.multiple_of(p * big_rows, big_rows)
        make_copy(pl.multiple_of(local_off + off, SUBLANES), pl.multiple_of(global_off + off, SUBLANES),
                  big_rows).start()
        return carry

    lax.fori_loop(0, big, piece, 0)
    done = big * big_rows
    for j in reversed(range(SEG_SMALL_BITS)):
        rows = SUBLANES << j
        low = done + ((units >> (j + 1)) & ((1 << (SEG_SMALL_BITS - 1 - j)) - 1)) * (2 * rows)

        @pl.when(((units >> j) & 1) == 1)
        def _():
            make_copy(pl.multiple_of(local_off + low, SUBLANES), pl.multiple_of(global_off + low, SUBLANES),
                      rows).start()


def _pack_pairs(x):
    half = D_MODEL // 2
    lo = lax.bitcast_convert_type(x[:, :half], U32) >> 16
    hi = lax.bitcast_convert_type(x[:, half:], U32) & jnp.uint32(0xFFFF0000)
    return hi | lo


def _unpack_pairs(p):
    lo = lax.bitcast_convert_type(p << 16, F32).astype(BF16)
    hi = lax.bitcast_convert_type(p & jnp.uint32(0xFFFF0000), F32).astype(BF16)
    return jnp.concatenate([lo, hi], axis=1)


def _block_rows(loff_s, seg_s, b):
    last = b * N_EXP + N_EXP - 1
    return pl.multiple_of(loff_s[last] + seg_s[last], SUBLANES)


def _dispatch_kernel(loff_s, seg_s, gbase_s, tail_s, h2_ref, te_ref, loffv_ref, tri_ref, iota_ref, xs_hbm,
                     loc, zeros, sem, sem_z):
    b = pl.program_id(0)
    slot = b % 2

    def wait_block(blk, s):
        n = _block_rows(loff_s, seg_s, blk)
        pltpu.make_async_copy(loc.at[s, pl.ds(0, n)], xs_hbm.at[pl.ds(0, n)], sem.at[s]).wait()

    lrow = _local_rows(te_ref, loffv_ref, tri_ref)

    @pl.when(b >= 2)
    def _():
        wait_block(b - 2, slot)

    for r0 in range(0, LROWS, SORT_CHUNK):
        rel = _chunk_relative(lrow, r0)
        p = jnp.zeros((SORT_CHUNK, TBD), BF16)
        for k in reversed(range(TOP_K)):
            p = jnp.where(iota_ref[...] == rel[k], jnp.ones_like(p), p)
        loc[slot, r0:r0 + SORT_CHUNK, :] = _pack_pairs(_dot(p, h2_ref[...]))

    def out_copy(a, g, size):
        return pltpu.make_async_copy(loc.at[slot, pl.ds(a, size)], xs_hbm.at[pl.ds(g, size)], sem.at[slot])

    def body(e, carry):
        idx = b * N_EXP + e
        _segment_copies(out_copy, loff_s[idx], gbase_s[idx], seg_s[idx] // SUBLANES)
        return carry

    lax.fori_loop(0, N_EXP, body, 0)

    @pl.when(b == NBD - 1)
    def _():
        zeros[...] = jnp.zeros_like(zeros)

        def zero_copy(a, g, size):
            return pltpu.make_async_copy(zeros.at[pl.ds(a, size)], xs_hbm.at[pl.ds(g, size)], sem_z)

        def zbody(e, ztot):
            _segment_copies(zero_copy, 0, tail_s[e], tail_s[N_EXP + e] // SUBLANES)
            return ztot + tail_s[N_EXP + e]

        def tbody(t, carry):
            pltpu.make_async_copy(zeros, xs_hbm.at[pl.ds(pl.multiple_of(t * TM, TM), TM)], sem_z).start()
            return carry

        n_used = tail_s[2 * N_EXP]
        lax.fori_loop(n_used, N_TILES, tbody, 0)
        ztot = lax.fori_loop(0, N_EXP, zbody, 0) + (N_TILES - n_used) * TM
        ztot = pl.multiple_of(ztot, SUBLANES)

        @pl.when(ztot > 0)
        def _():
            pltpu.make_async_copy(xs_hbm.at[pl.ds(0, ztot)], xs_hbm.at[pl.ds(0, ztot)], sem_z).wait()

        wait_block(b - 1, 1 - slot)
        wait_block(b, slot)


def _dispatch(tables, h2, top_e, tri, row_iota):
    loff_s, seg_s, gbase_s, tail_s, loff_v = tables
    grid_spec = pltpu.PrefetchScalarGridSpec(
        num_scalar_prefetch=4,
        grid=(NBD,),
        in_specs=[pl.BlockSpec((TBD, D_MODEL), lambda i, *_: (i, 0)),
                  pl.BlockSpec((1, TOP_K, TBD), lambda i, *_: (i, 0, 0)),
                  pl.BlockSpec((1, N_EXP, 1), lambda i, *_: (i, 0, 0)),
                  pl.BlockSpec((TBD, TBD), lambda i, *_: (0, 0)),
                  pl.BlockSpec((SORT_CHUNK, TBD), lambda i, *_: (0, 0))],
        out_specs=pl.BlockSpec(memory_space=pl.ANY),
        scratch_shapes=[pltpu.VMEM((2, LROWS, D_MODEL // 2), U32),
                        pltpu.VMEM((TM, D_MODEL // 2), U32),
                        pltpu.SemaphoreType.DMA((2,)),
                        pltpu.SemaphoreType.DMA])
    return pl.pallas_call(
        _dispatch_kernel,
        out_shape=jax.ShapeDtypeStruct((N_TILES * TM, D_MODEL // 2), U32),
        grid_spec=grid_spec,
        compiler_params=_params(),
        name="dispatch",
    )(loff_s, seg_s, gbase_s, tail_s, h2, top_e, loff_v, tri, row_iota)


def _moe_kernel(te_ref, first_ref, par_ref, next_ref, nv_ref, nu_ref, xs_ref, bgu_ref, bd_ref, wgu_hbm, wd_hbm, o_ref,
                wgu_st, wd_st, wgu_bf, wd_bf, sem):
    i = pl.program_id(0)

    def fetch(e, s):
        return (pltpu.make_async_copy(wgu_hbm.at[e], wgu_st.at[s], sem.at[0, s]),
                pltpu.make_async_copy(wd_hbm.at[e], wd_st.at[s], sem.at[1, s]))

    @pl.when(i < nu_ref[0])
    def _():
        @pl.when(first_ref[i] == 1)
        def _():
            s = par_ref[i]

            @pl.when(i == 0)
            def _():
                for cp in fetch(te_ref[0], 0):
                    cp.start()

            for cp in fetch(te_ref[i], s):
                cp.wait()

            @pl.when(next_ref[i] >= 0)
            def _():
                for cp in fetch(next_ref[i], 1 - s):
                    cp.start()

            wgu_bf[...] = wgu_st[s].astype(BF16)
            wd_bf[...] = wd_st[s].astype(BF16)

        def gate_up(r):
            return _dot(_unpack_pairs(xs_ref[r, :]), wgu_bf[...]) + bgu_ref[0]

        def activation(gu):
            gate = jnp.minimum(gu[:, :D_FF], SWIGLU_LIMIT)
            up = jnp.clip(gu[:, D_FF:], -SWIGLU_LIMIT, SWIGLU_LIMIT)
            return ((up + 1.0) * gate * jax.nn.sigmoid(SWIGLU_ALPHA * gate)).astype(BF16)

        def down(r, act):
            out = _dot(act, wd_bf[...]) + bd_ref[0]
            o_ref[r, :] = _pack_pairs(out.astype(BF16).astype(F32))

        def run(sizes):
            starts = [sum(sizes[:j]) for j in range(len(sizes))]
            pieces = [slice(a, a + n) for a, n in zip(starts, sizes)]
            gu = [gate_up(r) for r in pieces]
            for r, g in zip(pieces, gu):
                down(r, activation(g))
            done = sum(sizes)
            if done < TM:
                o_ref[done:, :] = jnp.zeros((TM - done, D_MODEL // 2), U32)

        nv = nv_ref[i]
        for lo, hi, sizes in MOE_PIECES:
            pl.when(jnp.logical_and(nv > lo, nv <= hi))(functools.partial(run, sizes))

    @pl.when(i >= nu_ref[0])
    def _():
        o_ref[...] = jnp.zeros_like(o_ref)


def _moe(tile_tables, xs, w_gate_up, b_gate_up, w_down, b_down):
    nsp = len(tile_tables)
    row_tile = lambda i, *s: (jnp.minimum(i, s[nsp - 1][0] - 1), 0)
    grid_spec = pltpu.PrefetchScalarGridSpec(
        num_scalar_prefetch=nsp,
        grid=(N_TILES,),
        in_specs=[pl.BlockSpec((TM, D_MODEL // 2), row_tile),
                  pl.BlockSpec((1, 1, 2 * D_FF), lambda i, te, *_: (te[i], 0, 0)),
                  pl.BlockSpec((1, 1, D_MODEL), lambda i, te, *_: (te[i], 0, 0)),
                  pl.BlockSpec(memory_space=pl.ANY),
                  pl.BlockSpec(memory_space=pl.ANY)],
        out_specs=pl.BlockSpec((TM, D_MODEL // 2), lambda i, *_: (i, 0)),
        scratch_shapes=[pltpu.VMEM((2, D_MODEL, 2 * D_FF), F32),
                        pltpu.VMEM((2, D_FF, D_MODEL), F32),
                        pltpu.VMEM((D_MODEL, 2 * D_FF), BF16),
                        pltpu.VMEM((D_FF, D_MODEL), BF16),
                        pltpu.SemaphoreType.DMA((2, 2))])
    return pl.pallas_call(
        _moe_kernel,
        out_shape=jax.ShapeDtypeStruct((N_TILES * TM, D_MODEL // 2), U32),
        grid_spec=grid_spec,
        compiler_params=_params(),
        name="moe",
    )(*tile_tables, xs, b_gate_up.reshape(N_EXP, 1, 2 * D_FF), b_down.reshape(N_EXP, 1, D_MODEL),
      w_gate_up, w_down)


def _final_kernel(loff_s, seg_s, gbase_s, ys_hbm, x1_ref, te_ref, tw_ref, loffv_ref, tri_ref, iota_ref, mod_ref, fn_ref,
                  oc_ref, ol_ref, loc, sem):
    b = pl.program_id(0)
    slot = b % 2

    def start_block(blk, s):
        def in_copy(a, g, size):
            return pltpu.make_async_copy(ys_hbm.at[pl.ds(g, size)], loc.at[s, pl.ds(a, size)], sem.at[s])

        def body(e, carry):
            idx = blk * N_EXP + e
            _segment_copies(in_copy, loff_s[idx], gbase_s[idx], seg_s[idx] // SUBLANES)
            return carry

        lax.fori_loop(0, N_EXP, body, 0)

    @pl.when(b == 0)
    def _():
        loc[...] = jnp.zeros_like(loc)
        start_block(0, 0)

    @pl.when(b + 1 < NBD)
    def _():
        start_block(b + 1, 1 - slot)

    lrow = _local_rows(te_ref, loffv_ref, tri_ref)
    tw = tw_ref[0]
    wts = [tw[k:k + 1].astype(BF16) for k in range(TOP_K)]

    n = _block_rows(loff_s, seg_s, b)
    pltpu.make_async_copy(ys_hbm.at[pl.ds(0, n)], loc.at[slot, pl.ds(0, n)], sem.at[slot]).wait()

    y = None
    for r0 in range(0, LROWS, SORT_CHUNK):
        rel = _chunk_relative(lrow, r0)
        pw = jnp.zeros((SORT_CHUNK, TBD), BF16)
        for k in reversed(range(TOP_K)):
            pw = jnp.where(iota_ref[...] == rel[k], wts[k], pw)
        part = _dot_tn(pw, _unpack_pairs(loc[slot, r0:r0 + SORT_CHUNK, :]))
        y = part if y is None else y + part

    row = _mod_row(b)
    gate2 = mod_ref[pl.ds(row, 1), pl.ds(5 * D_MODEL, D_MODEL)]
    out = _rms(x1_ref[...] + gate2 * y, fn_ref[...])

    @pl.when(b < NBD_CTX)
    def _():
        oc_ref[...] = out

    @pl.when(b >= NBD_CTX)
    def _():
        ol_ref[...] = out


def _final(tables, ys, x1, top_e, top_w, tri, col_iota, mod, final_norm):
    loff_s, seg_s, gbase_s, _, loff_v = tables
    grid_spec = pltpu.PrefetchScalarGridSpec(
        num_scalar_prefetch=3,
        grid=(NBD,),
        in_specs=[pl.BlockSpec(memory_space=pl.ANY),
                  pl.BlockSpec((TBD, D_MODEL), lambda i, *_: (i, 0)),
                  pl.BlockSpec((1, TOP_K, TBD), lambda i, *_: (i, 0, 0)),
                  pl.BlockSpec((1, TOP_K, TBD), lambda i, *_: (i, 0, 0)),
                  pl.BlockSpec((1, N_EXP, 1), lambda i, *_: (i, 0, 0)),
                  pl.BlockSpec((TBD, TBD), lambda i, *_: (0, 0)),
                  pl.BlockSpec((SORT_CHUNK, TBD), lambda i, *_: (0, 0)),
                  pl.BlockSpec((MOD_ROWS, 6 * D_MODEL), lambda i, *_: (0, 0)),
                  pl.BlockSpec((1, D_MODEL), lambda i, *_: (0, 0))],
        out_specs=(pl.BlockSpec((TBD, D_MODEL), lambda i, *_: (jnp.minimum(i, NBD_CTX - 1), 0)),
                   pl.BlockSpec((TBD, D_MODEL), lambda i, *_: (jnp.maximum(i - NBD_CTX, 0), 0))),
        scratch_shapes=[pltpu.VMEM((2, LROWS, D_MODEL // 2), U32),
                        pltpu.SemaphoreType.DMA((2,))])
    return pl.pallas_call(
        _final_kernel,
        out_shape=(jax.ShapeDtypeStruct((T_CTX, D_MODEL), F32),
                   jax.ShapeDtypeStruct((T_LAT, D_MODEL), F32)),
        grid_spec=grid_spec,
        compiler_params=_params(),
        name="final",
    )(loff_s, seg_s, gbase_s, ys, x1, top_e, top_w, loff_v, tri, col_iota, mod, final_norm)


def _dispatch_tables(hist):
    hist = hist.reshape(NBD, N_EXP)
    seg = ((hist + SUBLANES - 1) // SUBLANES) * SUBLANES
    loff = jnp.cumsum(seg, axis=1) - seg
    rows_e = jnp.sum(seg, axis=0)
    region = ((rows_e + TM - 1) // TM) * TM
    region_end = jnp.cumsum(region)
    region_start = region_end - region
    gbase = region_start[None, :] + jnp.cumsum(seg, axis=0) - seg
    n_used = (region_end[-1] // TM).astype(I32)
    tail = jnp.concatenate([region_start + rows_e, region - rows_e, n_used.reshape(1)])
    start = jnp.arange(N_TILES, dtype=I32) * TM
    tile_e = jnp.sum((start[:, None] >= region_end[None, :]).astype(I32), axis=1)
    tile_e = jnp.minimum(tile_e, tile_e[jnp.maximum(n_used - 1, 0)])
    first = jnp.concatenate([jnp.ones((1,), I32), (tile_e[1:] != tile_e[:-1]).astype(I32)])
    parity = (jnp.cumsum(first) - 1) % 2
    later = jnp.where(tile_e[None, :] > tile_e[:, None], tile_e[None, :], N_EXP)
    nxt = jnp.min(later, axis=1)
    nxt = jnp.where(nxt == N_EXP, -1, nxt)
    flat = lambda a: a.reshape(-1).astype(I32)
    tables = (flat(loff), flat(seg), flat(gbase), flat(tail), loff.astype(F32).reshape(NBD, N_EXP, 1))
    mine = tile_e[:, None] == jnp.arange(N_EXP, dtype=I32)[None, :]
    data_end = jnp.sum(jnp.where(mine, (region_start + rows_e)[None, :], 0), axis=1)
    tile_nv = jnp.clip(data_end - start, 0, TM)
    tile_tables = (flat(tile_e), flat(first), flat(parity), flat(nxt), flat(tile_nv), n_used.reshape(1))
    return tables, tile_tables


def kernel(x_prompt, x_sample, state_hgrn, c, c_ctx, w_ada, b_ada, norm1, w_in, hgrn_lb, hgrn_norm, w_pool,
           pool_scale, w_branch_a, w_branch_b, w_out, norm2, w_router, b_router, w_gate_up, b_gate_up,
           w_down, b_down, final_norm):
    x_ctx = x_prompt.reshape(T_CTX, D_MODEL)
    x_lat = x_sample.reshape(T_LAT, D_MODEL)
    cc = jnp.zeros((MOD_ROWS, D_MODEL), F32).at[:N_LAT_SEQ].set(c).at[N_LAT_SEQ].set(c_ctx)
    mod = _ada(cc, w_ada[0], b_ada)

    gates, rest = _inproj(x_ctx, x_lat, mod, norm1, w_in[0].astype(BF16))

    mall, masks = _hgrn_consts()
    o_f, o_b, new_state = _hgrn(gates, rest, hgrn_lb, state_hgrn[:, 0], mall, masks)

    a_pool, cnt_pool = _pool_consts()
    yb = _pool(rest, a_pool, cnt_pool, w_pool[0].astype(BF16), pool_scale)

    wr = jnp.pad(w_router[0], ((0, 0), (0, LANES - N_EXP)))
    wr_hi = wr.astype(BF16)
    wr_split = jnp.concatenate([wr_hi, (wr - wr_hi.astype(F32)).astype(BF16)], axis=1)
    tri =jnp.asarray(np.triu(np.ones((TBD, TBD), np.float32), 1), BF16)
    x1, h2, top_e, top_w, hist = _merge(
        x_ctx, x_lat, o_f, o_b, rest, yb, mod, hgrn_norm, w_branch_a[0].astype(BF16), w_branch_b[0].astype(BF16),
        w_out[0].astype(BF16), norm2, wr_split, b_router.reshape(N_EXP, 1))

    tables, tile_tables = _dispatch_tables(hist)
    row_iota = jnp.asarray(np.broadcast_to(np.arange(SORT_CHUNK, dtype=np.float32)[:, None], (SORT_CHUNK, TBD)), BF16)
    xs = _dispatch(tables, h2, top_e, tri, row_iota)
    ys = _moe(tile_tables, xs, w_gate_up[0], b_gate_up[0], w_down[0], b_down[0])
    y_ctx, y_lat = _final(tables, ys, x1, top_e, top_w, tri, row_iota, mod, final_norm.reshape(1, D_MODEL))
    y_prompt = y_ctx.reshape(N_CTX_SEQ, CTX_LEN, D_MODEL)
    y_sample = y_lat.reshape(N_LAT_SEQ, LAT_LEN, D_MODEL)
    return y_prompt, y_sample, new_state[:, None]
```

```python
import functools

import numpy as np
import jax
import jax.numpy as jnp
from jax import lax
from jax.experimental import pallas as pl
from jax.experimental.pallas import tpu as pltpu

F32 = jnp.float32
BF16 = jnp.bfloat16
I32 = jnp.int32
U32 = jnp.uint32

D_MODEL = 1024
N_CTX_SEQ, CTX_LEN = 32, 256
N_LAT_SEQ, LAT_LEN = 4, 2048
T_CTX = N_CTX_SEQ * CTX_LEN
T_LAT = N_LAT_SEQ * LAT_LEN
T_ALL = T_CTX + T_LAT
SUBLANES, LANES = 8, 128
TB = 256
NB = T_ALL // TB
NB_CTX = T_CTX // TB
LAT_BLOCKS = LAT_LEN // TB
HEADS, HEAD_K, HEAD_V = 4, 128, 128
HGRN_W = HEADS * HEAD_V
POOL_WINDOWS = (2, 4, 8, 16)
POOL_G = 128
POOL_W = len(POOL_WINDOWS) * POOL_G
GRID_W = 64
GRID_H = LAT_LEN // GRID_W
IN_W = 5 * HGRN_W + POOL_W + 2 * D_MODEL
GATE_W = 2 * HGRN_W
REST_W = IN_W - GATE_W
N_EXP, TOP_K, D_FF = 32, 4, 1024
SWIGLU_LIMIT = 7.0
SWIGLU_ALPHA = 1.702
EPS = 1e-6
LOG2_E = 1.4426950408889634
CHUNK = 64
N_LEVELS = 6
EXP_ROWS = (N_LEVELS + 2) * CHUNK
MM_BLOCKS = (0, 4, 5, 6)
MM_ROWS = len(MM_BLOCKS) * CHUNK
COARSE_LEVELS = ((1, 32), (2, 16), (3, 8))
TM = 512
MOE_PIECES = ((0, 128, (128,)), (128, 256, (256,)), (256, 384, (256, 128)), (384, TM, (256, 256)))
TBD = 512
NBD = T_ALL // TBD
NBD_CTX = T_CTX // TBD
SORT_CHUNK = 256
LROWS = TBD * TOP_K + N_EXP * SUBLANES
SEG_SMALL_BITS = 3
N_TILES = -(-(T_ALL * TOP_K + NBD * N_EXP * (SUBLANES - 1) + N_EXP * (TM - 1)) // TM)
MOD_ROWS = 8
VMEM_LIMIT = 56 * 1024 * 1024


def _params(sem=("arbitrary",)):
    return pltpu.CompilerParams(dimension_semantics=sem, vmem_limit_bytes=VMEM_LIMIT)


def _dot(a, b):
    return jnp.dot(a, b, preferred_element_type=F32)


def _dot_nt(a, b):
    return lax.dot_general(a, b, (((1,), (1,)), ((), ())), preferred_element_type=F32)


def _dot_tn(a, b):
    return lax.dot_general(a, b, (((0,), (0,)), ((), ())), preferred_element_type=F32)


def _split2(x):
    hi = x.astype(BF16)
    lo = (x - hi.astype(F32)).astype(BF16)
    return hi, lo


def _mod_row(i):
    return jnp.where(i < NBD_CTX, N_LAT_SEQ, (i - NBD_CTX) // (LAT_LEN // TBD))


def _ada_kernel(c_ref, w_ref, b_ref, o_ref):
    c = c_ref[...]
    s = c * jax.nn.sigmoid(c)
    o_ref[...] = jnp.dot(s, w_ref[...], preferred_element_type=F32,
                         precision=lax.Precision.HIGHEST) + b_ref[...]


def _ada(cc, w_ada, b_ada):
    nblk = 1536
    return pl.pallas_call(
        _ada_kernel,
        out_shape=jax.ShapeDtypeStruct((MOD_ROWS, 6 * D_MODEL), F32),
        grid=(6 * D_MODEL // nblk,),
        in_specs=[pl.BlockSpec((MOD_ROWS, D_MODEL), lambda j: (0, 0)),
                  pl.BlockSpec((D_MODEL, nblk), lambda j: (0, j)),
                  pl.BlockSpec((1, nblk), lambda j: (0, j))],
        out_specs=pl.BlockSpec((MOD_ROWS, nblk), lambda j: (0, j)),
        compiler_params=_params(),
        name="ada",
    )(cc, w_ada, b_ada)


def _rms(x, g):
    ms = jnp.mean(x * x, axis=-1, keepdims=True)
    return x * lax.rsqrt(ms + EPS) * g


def _x_specs():
    return [pl.BlockSpec((TBD, D_MODEL), lambda i, *_: (jnp.minimum(i, NBD_CTX - 1), 0)),
            pl.BlockSpec((TBD, D_MODEL), lambda i, *_: (jnp.maximum(i - NBD_CTX, 0), 0))]


def _x_block(xc_ref, xl_ref):
    return jnp.where(pl.program_id(0) < NBD_CTX, xc_ref[...], xl_ref[...])


def _inproj_kernel(xc_ref, xl_ref, mod_ref, n1_ref, w_ref, og_ref, or_ref):
    row = _mod_row(pl.program_id(0))
    shift = mod_ref[pl.ds(row, 1), pl.ds(0, D_MODEL)]
    scale = mod_ref[pl.ds(row, 1), pl.ds(D_MODEL, D_MODEL)]
    h = (_rms(_x_block(xc_ref, xl_ref), n1_ref[...]) * (1.0 + scale) + shift).astype(BF16)
    og_ref[...] = _dot(h, w_ref[:, HGRN_W:HGRN_W + GATE_W])
    or_ref[:, :HGRN_W] = _dot(h, w_ref[:, :HGRN_W]).astype(BF16)
    or_ref[:, HGRN_W:] = _dot(h, w_ref[:, HGRN_W + GATE_W:]).astype(BF16)


def _inproj(x_ctx, x_lat, mod, norm1, w_in_bf):
    return pl.pallas_call(
        _inproj_kernel,
        out_shape=(jax.ShapeDtypeStruct((T_ALL, GATE_W), F32),
                   jax.ShapeDtypeStruct((T_ALL, REST_W), BF16)),
        grid=(NBD,),
        in_specs=_x_specs() + [
                  pl.BlockSpec((MOD_ROWS, 6 * D_MODEL), lambda i: (0, 0)),
                  pl.BlockSpec((1, D_MODEL), lambda i: (0, 0)),
                  pl.BlockSpec((D_MODEL, IN_W), lambda i: (0, 0))],
        out_specs=(pl.BlockSpec((TBD, GATE_W), lambda i: (i, 0)),
                   pl.BlockSpec((TBD, REST_W), lambda i: (i, 0))),
        compiler_params=_params(),
        name="inproj",
    )(x_ctx, x_lat, mod, norm1, w_in_bf)


def _hgrn_consts():
    c = CHUNK
    t = np.arange(c)[:, None]
    u = np.arange(c)[None, :]
    blocks = [u <= t]
    masks = [np.eye(c, dtype=bool)]
    h = c // 2
    while h >= 1:
        bi = t // h
        upper = (bi % 2) == 1
        e_up = (u >= bi * h) & (u <= t)
        e_lo = (u > t) & (u <= bi * h + h - 1)
        blocks.append(np.where(upper, e_up, e_lo))
        masks.append(((t // (2 * h)) == (u // (2 * h))) & (((t // h) % 2) == 1) & (((u // h) % 2) == 0))
        h //= 2
    blocks.append(u > t)
    m_f = np.stack(blocks).astype(np.float32)
    k_f = np.stack(masks).astype(np.float32)
    m_b = m_f[:, ::-1, ::-1]
    k_b = k_f[:, ::-1, ::-1]
    sel = list(MM_BLOCKS)
    m = np.stack([m_f[sel].reshape(MM_ROWS, c), m_b[sel].reshape(MM_ROWS, c)])
    m3 = np.concatenate([m, m, m], axis=2)
    return jnp.asarray(m3, BF16), jnp.asarray(np.stack([k_f, k_b]), F32)


def _hgrn_block(dirs, lb, mall_ref, mask_ref, st_ref, z_ref, k_ref, sc_ref, run_ref, dcol_ref):
    c = CHUNK
    nchunk = TB // c
    units = [(d, h) for d in range(2) for h in range(HEADS)]
    sl = [slice(h * HEAD_K, (h + 1) * HEAD_K) for h in range(HEADS)]

    def rows(ci, d):
        r0 = ci * c if d == 0 else (nchunk - 1 - ci) * c
        return slice(r0, r0 + c)

    def exponents(ci):
        s = ci % 2
        for d in range(2):
            f = lb[d:d + 1] + (1.0 - lb[d:d + 1]) * jax.nn.sigmoid(dirs[d][1][rows(ci, d), :])
            k_ref[s, d] = 1.0 - f
            k_ref[s, 2 + d] = dirs[d][0][rows(ci, d), :].astype(F32)
            g = jnp.log(f) * LOG2_E
            g1 = g.astype(BF16)
            r1 = g - g1.astype(F32)
            g2 = r1.astype(BF16)
            g3 = (r1 - g2.astype(F32)).astype(BF16)
            gsplit = jnp.concatenate([g1, g2, g3], axis=0)
            ex = _dot(mall_ref[d], gsplit)
            run = ex[0:c]
            run_ref[d] = run
            z_ref[s, d, 0:c] = jnp.exp2(run)
            for j, blk in enumerate(MM_BLOCKS[1:]):
                z_ref[s, d, blk * c:(blk + 1) * c] = jnp.exp2(ex[(j + 1) * c:(j + 2) * c])
            for blk, h in COARSE_LEVELS:
                for base in range(0, c, 2 * h):
                    ref = run_ref[d, base + h - 1 + d:base + h + d, :]
                    if d == 0:
                        first, second = ref - run[base:base + h], run[base + h:base + 2 * h] - ref
                    else:
                        first, second = run[base:base + h] - ref, ref - run[base + h:base + 2 * h]
                    z_ref[s, d, blk * c + base:blk * c + base + h] = jnp.exp2(first)
                    z_ref[s, d, blk * c + base + h:blk * c + base + 2 * h] = jnp.exp2(second)
            end = run_ref[d, c - 1:c, :] if d == 0 else run_ref[d, 0:1, :]
            z_ref[s, d, (N_LEVELS + 1) * c:] = jnp.exp2(end - run)
            edge = run[c - SUBLANES:c] if d == 0 else run[0:SUBLANES]
            for h in range(HEADS):
                dcol_ref[s, d, h] = jnp.exp2(edge[:, sl[h]].T)

    def q_of(ci, d, h):
        return k_ref[ci % 2, 2 + d, :, sl[h]]

    def v_of(ci, d, h):
        return dirs[d][2][rows(ci, d), sl[h]].astype(BF16)

    def qz(ci, d, h, blk):
        return (q_of(ci, d, h) * z_ref[ci % 2, d, blk * c:(blk + 1) * c, sl[h]]).astype(BF16)

    def kz(ci, d, h, blk):
        return (k_ref[ci % 2, d, :, sl[h]] * z_ref[ci % 2, d, blk * c:(blk + 1) * c, sl[h]]).astype(BF16)

    def levels(ci):
        for d, h in units:
            q = q_of(ci, d, h).astype(F32)
            k = k_ref[ci % 2, d, :, sl[h]]
            k_next = pltpu.roll(k, 1 if d == 0 else c - 1, 0)
            zq = q * z_ref[ci % 2, d, N_LEVELS * c:(N_LEVELS + 1) * c, sl[h]]
            diag = jnp.sum(q * k, axis=1, keepdims=True)
            near = jnp.sum(zq * k_next, axis=1, keepdims=True)
            sc_ref[d, h] = mask_ref[d, 0] * diag + mask_ref[d, N_LEVELS] * near
        for lev in range(N_LEVELS - 1):
            for d, h in units:
                sc_ref[d, h] += mask_ref[d, lev + 1] * _dot_nt(qz(ci, d, h, lev + 1), kz(ci, d, h, lev + 1))

    def tail(ci):
        for d, h in units:
            o = (_dot(qz(ci, d, h, 0), st_ref[d, h].astype(BF16))
                 + _dot(sc_ref[d, h].astype(BF16), v_of(ci, d, h)))
            dirs[d][3][rows(ci, d), sl[h]] = o * (HEAD_K ** -0.5)
        for d, h in units:
            col = SUBLANES - 1 if d == 0 else 0
            decay = dcol_ref[ci % 2, d, h][:, col:col + 1]
            st_ref[d, h] = st_ref[d, h] * decay + _dot_tn(kz(ci, d, h, N_LEVELS + 1), v_of(ci, d, h))

    exponents(0)
    for ci in range(nchunk):
        levels(ci)
        if ci + 1 < nchunk:
            exponents(ci + 1)
        tail(ci)


def _hgrn_kernel(qf_ref, ff_ref, vf_ref, qb_ref, fb_ref, vb_ref, lbraw_ref, s0_ref, mall_ref, mask_ref,
                 of_ref, ob_ref, sout_hbm, st_ref, z_ref, k_ref, sc_ref, run_ref, dcol_ref, sem):
    i = pl.program_id(0)
    j = (i - NB_CTX) % LAT_BLOCKS
    is_ctx = i < NB_CTX

    @pl.when(is_ctx)
    def _():
        st_ref[...] = jnp.zeros_like(st_ref)

    @pl.when(jnp.logical_and(jnp.logical_not(is_ctx), j == 0))
    def _():
        st_ref[...] = s0_ref[0]

    a0 = lbraw_ref[0]
    a1 = lbraw_ref[1]
    mx = jnp.maximum(a0, a1)
    e0 = jnp.exp(a0 - mx)
    e1 = jnp.exp(a1 - mx)
    lb = e0 / (e0 + e1)

    dirs = ((qf_ref, ff_ref, vf_ref, of_ref), (qb_ref, fb_ref, vb_ref, ob_ref))
    _hgrn_block(dirs, lb, mall_ref, mask_ref, st_ref, z_ref, k_ref, sc_ref, run_ref, dcol_ref)

    @pl.when(is_ctx)
    def _():
        cp = pltpu.make_async_copy(st_ref, sout_hbm.at[i], sem)
        cp.start()
        cp.wait()


def _bwd_block(i):
    j = (i - NB_CTX) % LAT_BLOCKS
    return jnp.where(i < NB_CTX, i, i - j + (LAT_BLOCKS - 1 - j))


def _hgrn(gates, rest, hgrn_lb, s0, mall, masks):
    nh = HGRN_W
    fwd = lambda col: pl.BlockSpec((TB, nh), lambda i: (i, col))
    bwd = lambda col: pl.BlockSpec((TB, nh), lambda i: (_bwd_block(i), col))
    lat_seq = lambda i: jnp.clip((i - NB_CTX) // LAT_BLOCKS, 0, N_LAT_SEQ - 1)
    return pl.pallas_call(
        _hgrn_kernel,
        out_shape=(jax.ShapeDtypeStruct((T_ALL, nh), F32),
                   jax.ShapeDtypeStruct((T_ALL, nh), F32),
                   jax.ShapeDtypeStruct((N_CTX_SEQ, 2, HEADS, HEAD_K, HEAD_V), F32)),
        grid=(NB,),
        in_specs=[fwd(0), fwd(0), fwd(1), bwd(0), bwd(1), bwd(1),
                  pl.BlockSpec((2, 2, nh), lambda i: (0, 0, 0)),
                  pl.BlockSpec((1, 2, HEADS, HEAD_K, HEAD_V), lambda i: (lat_seq(i), 0, 0, 0, 0)),
                  pl.BlockSpec((2, MM_ROWS, 3 * CHUNK), lambda i: (0, 0, 0)),
                  pl.BlockSpec((2, N_LEVELS + 1, CHUNK, CHUNK), lambda i: (0, 0, 0, 0))],
        out_specs=(pl.BlockSpec((TB, nh), lambda i: (i, 0)),
                   pl.BlockSpec((TB, nh), lambda i: (_bwd_block(i), 0)),
                   pl.BlockSpec(memory_space=pl.ANY)),
        scratch_shapes=[pltpu.VMEM((2, HEADS, HEAD_K, HEAD_V), F32),
                        pltpu.VMEM((2, 2, EXP_ROWS, HGRN_W), F32),
                        pltpu.VMEM((2, 4, CHUNK, HGRN_W), F32),
                        pltpu.VMEM((2, HEADS, CHUNK, CHUNK), F32),
                        pltpu.VMEM((2, CHUNK, HGRN_W), F32),
                        pltpu.VMEM((2, 2, HEADS, HEAD_K, SUBLANES), F32),
                        pltpu.SemaphoreType.DMA],
        compiler_params=_params(),
        name="hgrn",
    )(rest, gates, rest, rest, gates, rest, hgrn_lb, s0, mall, masks)


def _window_bounds(n, w):
    pos = np.arange(n)
    lo = np.clip(pos - w // 2, 0, n - 1)
    hi = np.clip(pos - w // 2 + w - 1, 0, n - 1)
    return lo, hi


def _pool_consts():
    seq, img, cnt_seq, cnt_col = [], [], [], []
    for w in POOL_WINDOWS:
        lo, hi = _window_bounds(CTX_LEN, w)
        u = np.arange(CTX_LEN)[None, :]
        seq.append((u >= lo[:, None]) & (u <= hi[:, None]))
        cnt_seq.append(hi - lo + 1)
        lo, hi = _window_bounds(GRID_W, w)
        u = np.arange(GRID_W)[None, :]
        band = (u >= lo[:, None]) & (u <= hi[:, None])
        img.append(np.kron(np.eye(TB // GRID_W, dtype=bool), band))
        cnt_col.append(np.tile(hi - lo + 1, TB // GRID_W))
    a = np.stack([np.stack(seq), np.stack(img)]).astype(np.float32)
    cnt = np.stack([np.stack(cnt_seq), np.stack(cnt_col)]).astype(np.float32)
    cnt = np.broadcast_to(cnt[..., None], cnt.shape + (POOL_G,))
    return jnp.asarray(a, BF16), jnp.asarray(cnt, F32)


POOL_ROWS = LAT_LEN


def _pool_kernel(u_ref, a_ref, cnt_ref, wp_ref, ps_ref, o_ref, cp_ref, d_ref):
    i = pl.program_id(0)
    nblk = POOL_ROWS // TB

    def centre(g, r0, nrows, pm):
        sl = slice(g * POOL_G, (g + 1) * POOL_G)
        d_ref[pl.ds(r0, nrows), :] = (pm - u_ref[pl.ds(r0, nrows), sl].astype(F32)).astype(BF16)

    def group_map(g):
        sl = slice(g * POOL_G, (g + 1) * POOL_G)
        o_ref[:, sl] = (_dot(d_ref[...], wp_ref[g]) * ps_ref[:, sl]).astype(o_ref.dtype)

    def window_sum(kind, g, b):
        sl = slice(g * POOL_G, (g + 1) * POOL_G)
        return _dot(a_ref[kind, g], u_ref[pl.ds(b * TB, TB), sl]) / cnt_ref[kind, g]

    @pl.when(i < T_CTX // POOL_ROWS)
    def _():
        for g in range(len(POOL_WINDOWS)):
            for b in range(nblk):
                centre(g, b * TB, TB, window_sum(0, g, b))
            group_map(g)

    @pl.when(i >= T_CTX // POOL_ROWS)
    def _():
        for g, w in enumerate(POOL_WINDOWS):
            for b in range(nblk):
                cp_ref[pl.ds(b * TB, TB), :] = window_sum(1, g, b)
            lo, hi = _window_bounds(GRID_H, w)
            for r in range(GRID_H):
                acc = cp_ref[pl.ds(int(lo[r]) * GRID_W, GRID_W), :]
                for rr in range(int(lo[r]) + 1, int(hi[r]) + 1):
                    acc = acc + cp_ref[pl.ds(rr * GRID_W, GRID_W), :]
                centre(g, r * GRID_W, GRID_W, acc / float(hi[r] - lo[r] + 1))
            group_map(g)


def _pool(rest, a_pool, cnt_pool, w_pool_bf, pool_scale):
    col = 3
    return pl.pallas_call(
        _pool_kernel,
        out_shape=jax.ShapeDtypeStruct((T_ALL, POOL_W), BF16),
        grid=(T_ALL // POOL_ROWS,),
        in_specs=[pl.BlockSpec((POOL_ROWS, POOL_W), lambda i: (i, col)),
                  pl.BlockSpec((2, 4, TB, TB), lambda i: (0, 0, 0, 0)),
                  pl.BlockSpec((2, 4, TB, POOL_G), lambda i: (0, 0, 0, 0)),
                  pl.BlockSpec((4, POOL_G, POOL_G), lambda i: (0, 0, 0)),
                  pl.BlockSpec((1, POOL_W), lambda i: (0, 0))],
        out_specs=pl.BlockSpec((POOL_ROWS, POOL_W), lambda i: (i, 0)),
        scratch_shapes=[pltpu.VMEM((POOL_ROWS, POOL_G), F32),
                        pltpu.VMEM((POOL_ROWS, POOL_G), BF16)],
        compiler_params=_params(),
        name="pool",
    )(rest, a_pool, cnt_pool, w_pool_bf, pool_scale)


def _merge_kernel(xc_ref, xl_ref, of_ref, ob_ref, og_ref, yb_ref, ga_ref, gb_ref, mod_ref, hn_ref, wa_ref, wb_ref,
                  wo_ref, n2_ref, wrh_ref, br_ref,
                  x1_ref, h2_ref, te_ref, tw_ref, hist_ref):
    row = _mod_row(pl.program_id(0))
    gate1 = mod_ref[pl.ds(row, 1), pl.ds(2 * D_MODEL, D_MODEL)]
    shift2 = mod_ref[pl.ds(row, 1), pl.ds(3 * D_MODEL, D_MODEL)]
    scale2 = mod_ref[pl.ds(row, 1), pl.ds(4 * D_MODEL, D_MODEL)]

    halves = [slice(j * (TBD // 2), (j + 1) * (TBD // 2)) for j in range(2)]
    is_ctx = pl.program_id(0) < NBD_CTX

    def head_out(r):
        o = of_ref[r, :] + ob_ref[r, :]
        og = og_ref[r, :].astype(F32)
        ya = jnp.concatenate(
            [_rms(o[:, h * HEAD_V:(h + 1) * HEAD_V], hn_ref[...]) for h in range(HEADS)], axis=1)
        return (ya * (og * jax.nn.sigmoid(og))).astype(BF16)

    ya = [head_out(r) for r in halves]
    pa = [_dot(ya[j], wa_ref[...]) for j in range(2)]
    pb = [_dot(yb_ref[r, :], wb_ref[...]) for r in halves]
    merged = [(jax.nn.sigmoid(ga_ref[r, :].astype(F32)) * pa[j]
               + jax.nn.sigmoid(gb_ref[r, :].astype(F32)) * pb[j]).astype(BF16) for j, r in enumerate(halves)]
    po = [_dot(merged[j], wo_ref[...]) for j in range(2)]
    hh, hl = [], []
    for j, r in enumerate(halves):
        x1 = jnp.where(is_ctx, xc_ref[r, :], xl_ref[r, :]) + gate1 * po[j]
        x1_ref[r, :] = x1
        hi, lo = _split2(_rms(x1, n2_ref[...]) * (1.0 + scale2) + shift2)
        h2_ref[r, :] = hi
        hh.append(hi)
        hl.append(lo)
    hh = jnp.concatenate(hh, axis=0)
    hl = jnp.concatenate(hl, axis=0)

    both = _dot(hh, wrh_ref[...])
    tm = both[:, :LANES] + both[:, LANES:] + _dot(hl, wrh_ref[:, :LANES])
    lt = tm.T[:N_EXP] + br_ref[...]
    eidx = lax.broadcasted_iota(I32, (N_EXP, TBD), 0)
    vals, idxs, cnt = [], [], jnp.zeros((N_EXP, TBD), F32)
    for _ in range(TOP_K):
        m = jnp.max(lt, axis=0, keepdims=True)
        idx = jnp.min(jnp.where(lt == m, eidx, N_EXP), axis=0, keepdims=True)
        sel = eidx == idx
        vals.append(m)
        idxs.append(idx)
        cnt = cnt + sel.astype(F32)
        lt = jnp.where(sel, -jnp.inf, lt)
    ex = [jnp.exp(v - vals[0]) for v in vals]
    den = ex[0] + ex[1] + ex[2] + ex[3]
    tw_ref[0] = jnp.concatenate([e / den for e in ex], axis=0)
    te_ref[0] = jnp.concatenate(idxs, axis=0)
    hist_ref[0] = jnp.sum(cnt, axis=1, keepdims=True).astype(I32)


def _merge(x_ctx, x_lat, o_f, o_b, rest, yb, mod, hgrn_norm, wa_bf, wb_bf, wo_bf, norm2, wr_split, b_router):
    full = lambda shape: pl.BlockSpec(shape, lambda i: (0,) * len(shape))
    return pl.pallas_call(
        _merge_kernel,
        out_shape=(jax.ShapeDtypeStruct((T_ALL, D_MODEL), F32),
                   jax.ShapeDtypeStruct((T_ALL, D_MODEL), BF16),
                   jax.ShapeDtypeStruct((NBD, TOP_K, TBD), I32),
                   jax.ShapeDtypeStruct((NBD, TOP_K, TBD), F32),
                   jax.ShapeDtypeStruct((NBD, N_EXP, 1), I32)),
        grid=(NBD,),
        in_specs=_x_specs() + [
                  pl.BlockSpec((TBD, HGRN_W), lambda i: (i, 0)),
                  pl.BlockSpec((TBD, HGRN_W), lambda i: (i, 0)),
                  pl.BlockSpec((TBD, HGRN_W), lambda i: (i, 2)),
                  pl.BlockSpec((TBD, POOL_W), lambda i: (i, 0)),
                  pl.BlockSpec((TBD, D_MODEL), lambda i: (i, 2)),
                  pl.BlockSpec((TBD, D_MODEL), lambda i: (i, 3)),
                  full((MOD_ROWS, 6 * D_MODEL)),
                  full((1, HEAD_V)),
                  full((HGRN_W, D_MODEL)),
                  full((POOL_W, D_MODEL)),
                  full((D_MODEL, D_MODEL)),
                  full((1, D_MODEL)),
                  full((D_MODEL, 2 * LANES)),
                  full((N_EXP, 1))],
        out_specs=(pl.BlockSpec((TBD, D_MODEL), lambda i: (i, 0)),
                   pl.BlockSpec((TBD, D_MODEL), lambda i: (i, 0)),
                   pl.BlockSpec((1, TOP_K, TBD), lambda i: (i, 0, 0)),
                   pl.BlockSpec((1, TOP_K, TBD), lambda i: (i, 0, 0)),
                   pl.BlockSpec((1, N_EXP, 1), lambda i: (i, 0, 0))),
        compiler_params=_params(),
        name="merge",
    )(x_ctx, x_lat, o_f, o_b, rest, yb, rest, rest, mod, hgrn_norm, wa_bf, wb_bf, wo_bf, norm2,
      wr_split, b_router)


def _local_rows(te_ref, loff_ref, tri_ref):
    te = te_ref[0]
    eidx = lax.broadcasted_iota(I32, (N_EXP, TBD), 0)
    sels = [eidx == te[k:k + 1] for k in range(TOP_K)]
    cnt = sels[0].astype(F32)
    for s in sels[1:]:
        cnt = cnt + s.astype(F32)
    base = _dot(cnt.astype(BF16), tri_ref[...]) + loff_ref[0]
    return [jnp.sum(jnp.where(s, base, 0.0), axis=0, keepdims=True) for s in sels]


def _chunk_relative(rows, r0):
    out = []
    for r in rows:
        inside = jnp.logical_and(r >= r0, r < r0 + SORT_CHUNK)
        out.append(jnp.where(inside, r - r0, -1.0).astype(BF16))
    return out


def _segment_copies(make_copy, local_off, global_off, units):
    big_rows = SUBLANES << SEG_SMALL_BITS
    big = units >> SEG_SMALL_BITS

    def piece(p, carry):
        off = pl.multiple_of(p * big_rows, big_rows)
        make_copy(pl.multiple_of(local_off + off, SUBLANES), pl.multiple_of(global_off + off, SUBLANES),
                  big_rows).start()
        return carry

    lax.fori_loop(0, big, piece, 0)
    done = big * big_rows
    for j in reversed(range(SEG_SMALL_BITS)):
        rows = SUBLANES << j
        low = done + ((units >> (j + 1)) & ((1 << (SEG_SMALL_BITS - 1 - j)) - 1)) * (2 * rows)

        @pl.when(((units >> j) & 1) == 1)
        def _():
            make_copy(pl.multiple_of(local_off + low, SUBLANES), pl.multiple_of(global_off + low, SUBLANES),
                      rows).start()


def _pack_pairs(x):
    half = D_MODEL // 2
    lo = lax.bitcast_convert_type(x[:, :half], U32) >> 16
    hi = lax.bitcast_convert_type(x[:, half:], U32) & jnp.uint32(0xFFFF0000)
    return hi | lo


def _unpack_pairs(p):
    lo = lax.bitcast_convert_type(p << 16, F32).astype(BF16)
    hi = lax.bitcast_convert_type(p & jnp.uint32(0xFFFF0000), F32).astype(BF16)
    return jnp.concatenate([lo, hi], axis=1)


def _block_rows(loff_s, seg_s, b):
    last = b * N_EXP + N_EXP - 1
    return pl.multiple_of(loff_s[last] + seg_s[last], SUBLANES)


def _dispatch_kernel(loff_s, seg_s, gbase_s, tail_s, h2_ref, te_ref, loffv_ref, tri_ref, iota_ref, xs_hbm,
                     loc, zeros, sem, sem_z):
    b = pl.program_id(0)
    slot = b % 2

    def wait_block(blk, s):
        n = _block_rows(loff_s, seg_s, blk)
        pltpu.make_async_copy(loc.at[s, pl.ds(0, n)], xs_hbm.at[pl.ds(0, n)], sem.at[s]).wait()

    lrow = _local_rows(te_ref, loffv_ref, tri_ref)

    @pl.when(b >= 2)
    def _():
        wait_block(b - 2, slot)

    for r0 in range(0, LROWS, SORT_CHUNK):
        rel = _chunk_relative(lrow, r0)
        p = jnp.zeros((SORT_CHUNK, TBD), BF16)
        for k in reversed(range(TOP_K)):
            p = jnp.where(iota_ref[...] == rel[k], jnp.ones_like(p), p)
        loc[slot, r0:r0 + SORT_CHUNK, :] = _pack_pairs(_dot(p, h2_ref[...]))

    def out_copy(a, g, size):
        return pltpu.make_async_copy(loc.at[slot, pl.ds(a, size)], xs_hbm.at[pl.ds(g, size)], sem.at[slot])

    def body(e, carry):
        idx = b * N_EXP + e
        _segment_copies(out_copy, loff_s[idx], gbase_s[idx], seg_s[idx] // SUBLANES)
        return carry

    lax.fori_loop(0, N_EXP, body, 0)

    @pl.when(b == NBD - 1)
    def _():
        zeros[...] = jnp.zeros_like(zeros)

        def zero_copy(a, g, size):
            return pltpu.make_async_copy(zeros.at[pl.ds(a, size)], xs_hbm.at[pl.ds(g, size)], sem_z)

        def zbody(e, ztot):
            _segment_copies(zero_copy, 0, tail_s[e], tail_s[N_EXP + e] // SUBLANES)
            return ztot + tail_s[N_EXP + e]

        def tbody(t, carry):
            pltpu.make_async_copy(zeros, xs_hbm.at[pl.ds(pl.multiple_of(t * TM, TM), TM)], sem_z).start()
            return carry

        n_used = tail_s[2 * N_EXP]
        lax.fori_loop(n_used, N_TILES, tbody, 0)
        ztot = lax.fori_loop(0, N_EXP, zbody, 0) + (N_TILES - n_used) * TM
        ztot = pl.multiple_of(ztot, SUBLANES)

        @pl.when(ztot > 0)
        def _():
            pltpu.make_async_copy(xs_hbm.at[pl.ds(0, ztot)], xs_hbm.at[pl.ds(0, ztot)], sem_z).wait()

        wait_block(b - 1, 1 - slot)
        wait_block(b, slot)


def _dispatch(tables, h2, top_e, tri, row_iota):
    loff_s, seg_s, gbase_s, tail_s, loff_v = tables
    grid_spec = pltpu.PrefetchScalarGridSpec(
        num_scalar_prefetch=4,
        grid=(NBD,),
        in_specs=[pl.BlockSpec((TBD, D_MODEL), lambda i, *_: (i, 0)),
                  pl.BlockSpec((1, TOP_K, TBD), lambda i, *_: (i, 0, 0)),
                  pl.BlockSpec((1, N_EXP, 1), lambda i, *_: (i, 0, 0)),
                  pl.BlockSpec((TBD, TBD), lambda i, *_: (0, 0)),
                  pl.BlockSpec((SORT_CHUNK, TBD), lambda i, *_: (0, 0))],
        out_specs=pl.BlockSpec(memory_space=pl.ANY),
        scratch_shapes=[pltpu.VMEM((2, LROWS, D_MODEL // 2), U32),
                        pltpu.VMEM((TM, D_MODEL // 2), U32),
                        pltpu.SemaphoreType.DMA((2,)),
                        pltpu.SemaphoreType.DMA])
    return pl.pallas_call(
        _dispatch_kernel,
        out_shape=jax.ShapeDtypeStruct((N_TILES * TM, D_MODEL // 2), U32),
        grid_spec=grid_spec,
        compiler_params=_params(),
        name="dispatch",
    )(loff_s, seg_s, gbase_s, tail_s, h2, top_e, loff_v, tri, row_iota)


def _moe_kernel(te_ref, first_ref, par_ref, next_ref, nv_ref, nu_ref, xs_ref, bgu_ref, bd_ref, wgu_hbm, wd_hbm, o_ref,
                wgu_st, wd_st, wgu_bf, wd_bf, sem):
    i = pl.program_id(0)

    def fetch(e, s):
        return (pltpu.make_async_copy(wgu_hbm.at[e], wgu_st.at[s], sem.at[0, s]),
                pltpu.make_async_copy(wd_hbm.at[e], wd_st.at[s], sem.at[1, s]))

    @pl.when(i < nu_ref[0])
    def _():
        @pl.when(first_ref[i] == 1)
        def _():
            s = par_ref[i]

            @pl.when(i == 0)
            def _():
                for cp in fetch(te_ref[0], 0):
                    cp.start()

            for cp in fetch(te_ref[i], s):
                cp.wait()

            @pl.when(next_ref[i] >= 0)
            def _():
                for cp in fetch(next_ref[i], 1 - s):
                    cp.start()

            wgu_bf[...] = wgu_st[s].astype(BF16)
            wd_bf[...] = wd_st[s].astype(BF16)

        def gate_up(r):
            return _dot(_unpack_pairs(xs_ref[r, :]), wgu_bf[...]) + bgu_ref[0]

        def activation(gu):
            gate = jnp.minimum(gu[:, :D_FF], SWIGLU_LIMIT)
            up = jnp.clip(gu[:, D_FF:], -SWIGLU_LIMIT, SWIGLU_LIMIT)
            return ((up + 1.0) * gate * jax.nn.sigmoid(SWIGLU_ALPHA * gate)).astype(BF16)

        def down(r, act):
            out = _dot(act, wd_bf[...]) + bd_ref[0]
            o_ref[r, :] = _pack_pairs(out.astype(BF16).astype(F32))

        def run(sizes):
            starts = [sum(sizes[:j]) for j in range(len(sizes))]
            pieces = [slice(a, a + n) for a, n in zip(starts, sizes)]
            gu = [gate_up(r) for r in pieces]
            for r, g in zip(pieces, gu):
                down(r, activation(g))
            done = sum(sizes)
            if done < TM:
                o_ref[done:, :] = jnp.zeros((TM - done, D_MODEL // 2), U32)

        nv = nv_ref[i]
        for lo, hi, sizes in MOE_PIECES:
            pl.when(jnp.logical_and(nv > lo, nv <= hi))(functools.partial(run, sizes))

    @pl.when(i >= nu_ref[0])
    def _():
        o_ref[...] = jnp.zeros_like(o_ref)


def _moe(tile_tables, xs, w_gate_up, b_gate_up, w_down, b_down):
    nsp = len(tile_tables)
    row_tile = lambda i, *s: (jnp.minimum(i, s[nsp - 1][0] - 1), 0)
    grid_spec = pltpu.PrefetchScalarGridSpec(
        num_scalar_prefetch=nsp,
        grid=(N_TILES,),
        in_specs=[pl.BlockSpec((TM, D_MODEL // 2), row_tile),
                  pl.BlockSpec((1, 1, 2 * D_FF), lambda i, te, *_: (te[i], 0, 0)),
                  pl.BlockSpec((1, 1, D_MODEL), lambda i, te, *_: (te[i], 0, 0)),
                  pl.BlockSpec(memory_space=pl.ANY),
                  pl.BlockSpec(memory_space=pl.ANY)],
        out_specs=pl.BlockSpec((TM, D_MODEL // 2), lambda i, *_: (i, 0)),
        scratch_shapes=[pltpu.VMEM((2, D_MODEL, 2 * D_FF), F32),
                        pltpu.VMEM((2, D_FF, D_MODEL), F32),
                        pltpu.VMEM((D_MODEL, 2 * D_FF), BF16),
                        pltpu.VMEM((D_FF, D_MODEL), BF16),
                        pltpu.SemaphoreType.DMA((2, 2))])
    return pl.pallas_call(
        _moe_kernel,
        out_shape=jax.ShapeDtypeStruct((N_TILES * TM, D_MODEL // 2), U32),
        grid_spec=grid_spec,
        compiler_params=_params(),
        name="moe",
    )(*tile_tables, xs, b_gate_up.reshape(N_EXP, 1, 2 * D_FF), b_down.reshape(N_EXP, 1, D_MODEL),
      w_gate_up, w_down)


def _final_kernel(loff_s, seg_s, gbase_s, ys_hbm, x1_ref, te_ref, tw_ref, loffv_ref, tri_ref, iota_ref, mod_ref, fn_ref,
                  oc_ref, ol_ref, loc, sem):
    b = pl.program_id(0)
    slot = b % 2

    def start_block(blk, s):
        def in_copy(a, g, size):
            return pltpu.make_async_copy(ys_hbm.at[pl.ds(g, size)], loc.at[s, pl.ds(a, size)], sem.at[s])

        def body(e, carry):
            idx = blk * N_EXP + e
            _segment_copies(in_copy, loff_s[idx], gbase_s[idx], seg_s[idx] // SUBLANES)
            return carry

        lax.fori_loop(0, N_EXP, body, 0)

    @pl.when(b == 0)
    def _():
        loc[...] = jnp.zeros_like(loc)
        start_block(0, 0)

    @pl.when(b + 1 < NBD)
    def _():
        start_block(b + 1, 1 - slot)

    lrow = _local_rows(te_ref, loffv_ref, tri_ref)
    tw = tw_ref[0]
    wts = [tw[k:k + 1].astype(BF16) for k in range(TOP_K)]

    n = _block_rows(loff_s, seg_s, b)
    pltpu.make_async_copy(ys_hbm.at[pl.ds(0, n)], loc.at[slot, pl.ds(0, n)], sem.at[slot]).wait()

    y = None
    for r0 in range(0, LROWS, SORT_CHUNK):
        rel = _chunk_relative(lrow, r0)
        pw = jnp.zeros((SORT_CHUNK, TBD), BF16)
        for k in reversed(range(TOP_K)):
            pw = jnp.where(iota_ref[...] == rel[k], wts[k], pw)
        part = _dot_tn(pw, _unpack_pairs(loc[slot, r0:r0 + SORT_CHUNK, :]))
        y = part if y is None else y + part

    row = _mod_row(b)
    gate2 = mod_ref[pl.ds(row, 1), pl.ds(5 * D_MODEL, D_MODEL)]
    out = _rms(x1_ref[...] + gate2 * y, fn_ref[...])

    @pl.when(b < NBD_CTX)
    def _():
        oc_ref[...] = out

    @pl.when(b >= NBD_CTX)
    def _():
        ol_ref[...] = out


def _final(tables, ys, x1, top_e, top_w, tri, col_iota, mod, final_norm):
    loff_s, seg_s, gbase_s, _, loff_v = tables
    grid_spec = pltpu.PrefetchScalarGridSpec(
        num_scalar_prefetch=3,
        grid=(NBD,),
        in_specs=[pl.BlockSpec(memory_space=pl.ANY),
                  pl.BlockSpec((TBD, D_MODEL), lambda i, *_: (i, 0)),
                  pl.BlockSpec((1, TOP_K, TBD), lambda i, *_: (i, 0, 0)),
                  pl.BlockSpec((1, TOP_K, TBD), lambda i, *_: (i, 0, 0)),
                  pl.BlockSpec((1, N_EXP, 1), lambda i, *_: (i, 0, 0)),
                  pl.BlockSpec((TBD, TBD), lambda i, *_: (0, 0)),
                  pl.BlockSpec((SORT_CHUNK, TBD), lambda i, *_: (0, 0)),
                  pl.BlockSpec((MOD_ROWS, 6 * D_MODEL), lambda i, *_: (0, 0)),
                  pl.BlockSpec((1, D_MODEL), lambda i, *_: (0, 0))],
        out_specs=(pl.BlockSpec((TBD, D_MODEL), lambda i, *_: (jnp.minimum(i, NBD_CTX - 1), 0)),
                   pl.BlockSpec((TBD, D_MODEL), lambda i, *_: (jnp.maximum(i - NBD_CTX, 0), 0))),
        scratch_shapes=[pltpu.VMEM((2, LROWS, D_MODEL // 2), U32),
                        pltpu.SemaphoreType.DMA((2,))])
    return pl.pallas_call(
        _final_kernel,
        out_shape=(jax.ShapeDtypeStruct((T_CTX, D_MODEL), F32),
                   jax.ShapeDtypeStruct((T_LAT, D_MODEL), F32)),
        grid_spec=grid_spec,
        compiler_params=_params(),
        name="final",
    )(loff_s, seg_s, gbase_s, ys, x1, top_e, top_w, loff_v, tri, col_iota, mod, final_norm)


def _dispatch_tables(hist):
    hist = hist.reshape(NBD, N_EXP)
    seg = ((hist + SUBLANES - 1) // SUBLANES) * SUBLANES
    loff = jnp.cumsum(seg, axis=1) - seg
    rows_e = jnp.sum(seg, axis=0)
    region = ((rows_e + TM - 1) // TM) * TM
    region_end = jnp.cumsum(region)
    region_start = region_end - region
    gbase = region_start[None, :] + jnp.cumsum(seg, axis=0) - seg
    n_used = (region_end[-1] // TM).astype(I32)
    tail = jnp.concatenate([region_start + rows_e, region - rows_e, n_used.reshape(1)])
    start = jnp.arange(N_TILES, dtype=I32) * TM
    tile_e = jnp.sum((start[:, None] >= region_end[None, :]).astype(I32), axis=1)
    tile_e = jnp.minimum(tile_e, tile_e[jnp.maximum(n_used - 1, 0)])
    first = jnp.concatenate([jnp.ones((1,), I32), (tile_e[1:] != tile_e[:-1]).astype(I32)])
    parity = (jnp.cumsum(first) - 1) % 2
    later = jnp.where(tile_e[None, :] > tile_e[:, None], tile_e[None, :], N_EXP)
    nxt = jnp.min(later, axis=1)
    nxt = jnp.where(nxt == N_EXP, -1, nxt)
    flat = lambda a: a.reshape(-1).astype(I32)
    tables = (flat(loff), flat(seg), flat(gbase), flat(tail), loff.astype(F32).reshape(NBD, N_EXP, 1))
    mine = tile_e[:, None] == jnp.arange(N_EXP, dtype=I32)[None, :]
    data_end = jnp.sum(jnp.where(mine, (region_start + rows_e)[None, :], 0), axis=1)
    tile_nv = jnp.clip(data_end - start, 0, TM)
    tile_tables = (flat(tile_e), flat(first), flat(parity), flat(nxt), flat(tile_nv), n_used.reshape(1))
    return tables, tile_tables


def kernel(x_prompt, x_sample, state_hgrn, c, c_ctx, w_ada, b_ada, norm1, w_in, hgrn_lb, hgrn_norm, w_pool,
           pool_scale, w_branch_a, w_branch_b, w_out, norm2, w_router, b_router, w_gate_up, b_gate_up,
           w_down, b_down, final_norm):
    x_ctx = x_prompt.reshape(T_CTX, D_MODEL)
    x_lat = x_sample.reshape(T_LAT, D_MODEL)
    cc = jnp.zeros((MOD_ROWS, D_MODEL), F32).at[:N_LAT_SEQ].set(c).at[N_LAT_SEQ].set(c_ctx)
    mod = _ada(cc, w_ada[0], b_ada)

    gates, rest = _inproj(x_ctx, x_lat, mod, norm1, w_in[0].astype(BF16))

    mall, masks = _hgrn_consts()
    o_f, o_b, new_state = _hgrn(gates, rest, hgrn_lb, state_hgrn[:, 0], mall, masks)

    a_pool, cnt_pool = _pool_consts()
    yb = _pool(rest, a_pool, cnt_pool, w_pool[0].astype(BF16), pool_scale)

    wr = jnp.pad(w_router[0], ((0, 0), (0, LANES - N_EXP)))
    wr_hi = wr.astype(BF16)
    wr_split = jnp.concatenate([wr_hi, (wr - wr_hi.astype(F32)).astype(BF16)], axis=1)
    tri =jnp.asarray(np.triu(np.ones((TBD, TBD), np.float32), 1), BF16)
    x1, h2, top_e, top_w, hist = _merge(
        x_ctx, x_lat, o_f, o_b, rest, yb, mod, hgrn_norm, w_branch_a[0].astype(BF16), w_branch_b[0].astype(BF16),
        w_out[0].astype(BF16), norm2, wr_split, b_router.reshape(N_EXP, 1))

    tables, tile_tables = _dispatch_tables(hist)
    row_iota = jnp.asarray(np.broadcast_to(np.arange(SORT_CHUNK, dtype=np.float32)[:, None], (SORT_CHUNK, TBD)), BF16)
    xs = _dispatch(tables, h2, top_e, tri, row_iota)
    ys = _moe(tile_tables, xs, w_gate_up[0], b_gate_up[0], w_down[0], b_down[0])
    y_ctx, y_lat = _final(tables, ys, x1, top_e, top_w, tri, row_iota, mod, final_norm.reshape(1, D_MODEL))
    y_prompt = y_ctx.reshape(N_CTX_SEQ, CTX_LEN, D_MODEL)
    y_sample = y_lat.reshape(N_LAT_SEQ, LAT_LEN, D_MODEL)
    return y_prompt, y_sample, new_state[:, None]
```

```python
import functools

import numpy as np
import jax
import jax.numpy as jnp
from jax import lax
from jax.experimental import pallas as pl
from jax.experimental.pallas import tpu as pltpu

F32 = jnp.float32
BF16 = jnp.bfloat16
I32 = jnp.int32
U32 = jnp.uint32

D_MODEL = 1024
N_CTX_SEQ, CTX_LEN = 32, 256
N_LAT_SEQ, LAT_LEN = 4, 2048
T_CTX = N_CTX_SEQ * CTX_LEN
T_LAT = N_LAT_SEQ * LAT_LEN
T_ALL = T_CTX + T_LAT
SUBLANES, LANES = 8, 128
TB = 256
NB = T_ALL // TB
NB_CTX = T_CTX // TB
LAT_BLOCKS = LAT_LEN // TB
HEADS, HEAD_K, HEAD_V = 4, 128, 128
HGRN_W = HEADS * HEAD_V
POOL_WINDOWS = (2, 4, 8, 16)
POOL_G = 128
POOL_W = len(POOL_WINDOWS) * POOL_G
GRID_W = 64
GRID_H = LAT_LEN // GRID_W
IN_W = 5 * HGRN_W + POOL_W + 2 * D_MODEL
GATE_W = 2 * HGRN_W
REST_W = IN_W - GATE_W
N_EXP, TOP_K, D_FF = 32, 4, 1024
SWIGLU_LIMIT = 7.0
SWIGLU_ALPHA = 1.702
EPS = 1e-6
LOG2_E = 1.4426950408889634
CHUNK = 64
N_LEVELS = 6
EXP_ROWS = (N_LEVELS + 2) * CHUNK
MM_BLOCKS = (0, 4, 5, 6)
MM_ROWS = len(MM_BLOCKS) * CHUNK
COARSE_LEVELS = ((1, 32), (2, 16), (3, 8))
TM = 512
MOE_PIECES = ((0, 128, (128,)), (128, 256, (256,)), (256, 384, (256, 128)), (384, TM, (256, 256)))
TBD = 512
NBD = T_ALL // TBD
NBD_CTX = T_CTX // TBD
SORT_CHUNK = 256
LROWS = TBD * TOP_K + N_EXP * SUBLANES
SEG_SMALL_BITS = 3
N_TILES = -(-(T_ALL * TOP_K + NBD * N_EXP * (SUBLANES - 1) + N_EXP * (TM - 1)) // TM)
MOD_ROWS = 8
VMEM_LIMIT = 56 * 1024 * 1024


def _params(sem=("arbitrary",)):
    return pltpu.CompilerParams(dimension_semantics=sem, vmem_limit_bytes=VMEM_LIMIT)


def _dot(a, b):
    return jnp.dot(a, b, preferred_element_type=F32)


def _dot_nt(a, b):
    return lax.dot_general(a, b, (((1,), (1,)), ((), ())), preferred_element_type=F32)


def _dot_tn(a, b):
    return lax.dot_general(a, b, (((0,), (0,)), ((), ())), preferred_element_type=F32)


def _split2(x):
    hi = x.astype(BF16)
    lo = (x - hi.astype(F32)).astype(BF16)
    return hi, lo


def _mod_row(i):
    return jnp.where(i < NBD_CTX, N_LAT_SEQ, (i - NBD_CTX) // (LAT_LEN // TBD))


def _ada_kernel(c_ref, w_ref, b_ref, o_ref):
    c = c_ref[...]
    s = c * jax.nn.sigmoid(c)
    o_ref[...] = jnp.dot(s, w_ref[...], preferred_element_type=F32,
                         precision=lax.Precision.HIGHEST) + b_ref[...]


def _ada(cc, w_ada, b_ada):
    nblk = 1536
    return pl.pallas_call(
        _ada_kernel,
        out_shape=jax.ShapeDtypeStruct((MOD_ROWS, 6 * D_MODEL), F32),
        grid=(6 * D_MODEL // nblk,),
        in_specs=[pl.BlockSpec((MOD_ROWS, D_MODEL), lambda j: (0, 0)),
                  pl.BlockSpec((D_MODEL, nblk), lambda j: (0, j)),
                  pl.BlockSpec((1, nblk), lambda j: (0, j))],
        out_specs=pl.BlockSpec((MOD_ROWS, nblk), lambda j: (0, j)),
        compiler_params=_params(),
        name="ada",
    )(cc, w_ada, b_ada)


def _rms(x, g):
    ms = jnp.mean(x * x, axis=-1, keepdims=True)
    return x * lax.rsqrt(ms + EPS) * g


def _x_specs():
    return [pl.BlockSpec((TBD, D_MODEL), lambda i, *_: (jnp.minimum(i, NBD_CTX - 1), 0)),
            pl.BlockSpec((TBD, D_MODEL), lambda i, *_: (jnp.maximum(i - NBD_CTX, 0), 0))]


def _x_block(xc_ref, xl_ref):
    return jnp.where(pl.program_id(0) < NBD_CTX, xc_ref[...], xl_ref[...])


def _inproj_kernel(xc_ref, xl_ref, mod_ref, n1_ref, w_ref, og_ref, or_ref):
    row = _mod_row(pl.program_id(0))
    shift = mod_ref[pl.ds(row, 1), pl.ds(0, D_MODEL)]
    scale = mod_ref[pl.ds(row, 1), pl.ds(D_MODEL, D_MODEL)]
    h = (_rms(_x_block(xc_ref, xl_ref), n1_ref[...]) * (1.0 + scale) + shift).astype(BF16)
    og_ref[...] = _dot(h, w_ref[:, HGRN_W:HGRN_W + GATE_W])
    or_ref[:, :HGRN_W] = _dot(h, w_ref[:, :HGRN_W]).astype(BF16)
    or_ref[:, HGRN_W:] = _dot(h, w_ref[:, HGRN_W + GATE_W:]).astype(BF16)


def _inproj(x_ctx, x_lat, mod, norm1, w_in_bf):
    return pl.pallas_call(
        _inproj_kernel,
        out_shape=(jax.ShapeDtypeStruct((T_ALL, GATE_W), F32),
                   jax.ShapeDtypeStruct((T_ALL, REST_W), BF16)),
        grid=(NBD,),
        in_specs=_x_specs() + [
                  pl.BlockSpec((MOD_ROWS, 6 * D_MODEL), lambda i: (0, 0)),
                  pl.BlockSpec((1, D_MODEL), lambda i: (0, 0)),
                  pl.BlockSpec((D_MODEL, IN_W), lambda i: (0, 0))],
        out_specs=(pl.BlockSpec((TBD, GATE_W), lambda i: (i, 0)),
                   pl.BlockSpec((TBD, REST_W), lambda i: (i, 0))),
        compiler_params=_params(),
        name="inproj",
    )(x_ctx, x_lat, mod, norm1, w_in_bf)


def _hgrn_consts():
    c = CHUNK
    t = np.arange(c)[:, None]
    u = np.arange(c)[None, :]
    blocks = [u <= t]
    masks = [np.eye(c, dtype=bool)]
    h = c // 2
    while h >= 1:
        bi = t // h
        upper = (bi % 2) == 1
        e_up = (u >= bi * h) & (u <= t)
        e_lo = (u > t) & (u <= bi * h + h - 1)
        blocks.append(np.where(upper, e_up, e_lo))
        masks.append(((t // (2 * h)) == (u // (2 * h))) & (((t // h) % 2) == 1) & (((u // h) % 2) == 0))
        h //= 2
    blocks.append(u > t)
    m_f = np.stack(blocks).astype(np.float32)
    k_f = np.stack(masks).astype(np.float32)
    m_b = m_f[:, ::-1, ::-1]
    k_b = k_f[:, ::-1, ::-1]
    sel = list(MM_BLOCKS)
    m = np.stack([m_f[sel].reshape(MM_ROWS, c), m_b[sel].reshape(MM_ROWS, c)])
    m3 = np.concatenate([m, m, m], axis=2)
    return jnp.asarray(m3, BF16), jnp.asarray(np.stack([k_f, k_b]), F32)


def _hgrn_block(dirs, lb, mall_ref, mask_ref, st_ref, z_ref, k_ref, sc_ref, run_ref):
    c = CHUNK
    nchunk = TB // c
    units = [(d, h) for d in range(2) for h in range(HEADS)]
    sl = [slice(h * HEAD_K, (h + 1) * HEAD_K) for h in range(HEADS)]

    def rows(ci, d):
        r0 = ci * c if d == 0 else (nchunk - 1 - ci) * c
        return slice(r0, r0 + c)

    def exponents(ci):
        s = ci % 2
        for d in range(2):
            f = lb[d:d + 1] + (1.0 - lb[d:d + 1]) * jax.nn.sigmoid(dirs[d][1][rows(ci, d), :])
            k_ref[s, d] = 1.0 - f
            k_ref[s, 2 + d] = dirs[d][0][rows(ci, d), :].astype(F32)
            g = jnp.log(f) * LOG2_E
            g1 = g.astype(BF16)
            r1 = g - g1.astype(F32)
            g2 = r1.astype(BF16)
            g3 = (r1 - g2.astype(F32)).astype(BF16)
            gsplit = jnp.concatenate([g1, g2, g3], axis=0)
            ex = _dot(mall_ref[d], gsplit)
            run = ex[0:c]
            run_ref[d] = run
            z_ref[s, d, 0:c] = jnp.exp2(run)
            for j, blk in enumerate(MM_BLOCKS[1:]):
                z_ref[s, d, blk * c:(blk + 1) * c] = jnp.exp2(ex[(j + 1) * c:(j + 2) * c])
            for blk, h in COARSE_LEVELS:
                for base in range(0, c, 2 * h):
                    ref = run_ref[d, base + h - 1 + d:base + h + d, :]
                    if d == 0:
                        first, second = ref - run[base:base + h], run[base + h:base + 2 * h] - ref
                    else:
                        first, second = run[base:base + h] - ref, ref - run[base + h:base + 2 * h]
                    z_ref[s, d, blk * c + base:blk * c + base + h] = jnp.exp2(first)
                    z_ref[s, d, blk * c + base + h:blk * c + base + 2 * h] = jnp.exp2(second)
            end = run_ref[d, c - 1:c, :] if d == 0 else run_ref[d, 0:1, :]
            z_ref[s, d, (N_LEVELS + 1) * c:] = jnp.exp2(end - run)

    def q_of(ci, d, h):
        return k_ref[ci % 2, 2 + d, :, sl[h]]

    def v_of(ci, d, h):
        return dirs[d][2][rows(ci, d), sl[h]].astype(BF16)

    def qz(ci, d, h, blk):
        return (q_of(ci, d, h) * z_ref[ci % 2, d, blk * c:(blk + 1) * c, sl[h]]).astype(BF16)

    def kz(ci, d, h, blk):
        return (k_ref[ci % 2, d, :, sl[h]] * z_ref[ci % 2, d, blk * c:(blk + 1) * c, sl[h]]).astype(BF16)

    def levels(ci):
        for d, h in units:
            q = q_of(ci, d, h).astype(F32)
            k = k_ref[ci % 2, d, :, sl[h]]
            k_next = pltpu.roll(k, 1 if d == 0 else c - 1, 0)
            zq = q * z_ref[ci % 2, d, N_LEVELS * c:(N_LEVELS + 1) * c, sl[h]]
            diag = jnp.sum(q * k, axis=1, keepdims=True)
            near = jnp.sum(zq * k_next, axis=1, keepdims=True)
            sc_ref[d, h] = mask_ref[d, 0] * diag + mask_ref[d, N_LEVELS] * near
        for lev in range(N_LEVELS - 1):
            for d, h in units:
                sc_ref[d, h] += mask_ref[d, lev + 1] * _dot_nt(qz(ci, d, h, lev + 1), kz(ci, d, h, lev + 1))

    def tail(ci):
        for d, h in units:
            o = (_dot_nt(qz(ci, d, h, 0), st_ref[d, h].astype(BF16))
                 + _dot(sc_ref[d, h].astype(BF16), v_of(ci, d, h)))
            dirs[d][3][rows(ci, d), sl[h]] = o * (HEAD_K ** -0.5)
        for d, h in units:
            tot_row = c - 1 if d == 0 else 0
            decay = z_ref[ci % 2, d, tot_row:tot_row + 1, sl[h]]
            st_ref[d, h] = st_ref[d, h] * decay + _dot_tn(v_of(ci, d, h), kz(ci, d, h, N_LEVELS + 1))

    exponents(0)
    for ci in range(nchunk):
        levels(ci)
        if ci + 1 < nchunk:
            exponents(ci + 1)
        tail(ci)


def _hgrn_kernel(qf_ref, ff_ref, vf_ref, qb_ref, fb_ref, vb_ref, lbraw_ref, s0_ref, mall_ref, mask_ref,
                 of_ref, ob_ref, sout_hbm, st_ref, stage_ref, z_ref, k_ref, sc_ref, run_ref, sem):
    i = pl.program_id(0)
    j = (i - NB_CTX) % LAT_BLOCKS
    is_ctx = i < NB_CTX

    @pl.when(is_ctx)
    def _():
        st_ref[...] = jnp.zeros_like(st_ref)

    @pl.when(jnp.logical_and(jnp.logical_not(is_ctx), j == 0))
    def _():
        for d in range(2):
            for h in range(HEADS):
                st_ref[d, h] = s0_ref[0, d, h].T

    a0 = lbraw_ref[0]
    a1 = lbraw_ref[1]
    mx = jnp.maximum(a0, a1)
    e0 = jnp.exp(a0 - mx)
    e1 = jnp.exp(a1 - mx)
    lb = e0 / (e0 + e1)

    dirs = ((qf_ref, ff_ref, vf_ref, of_ref), (qb_ref, fb_ref, vb_ref, ob_ref))
    _hgrn_block(dirs, lb, mall_ref, mask_ref, st_ref, z_ref, k_ref, sc_ref, run_ref)

    @pl.when(is_ctx)
    def _():
        for d in range(2):
            for h in range(HEADS):
                stage_ref[d, h] = st_ref[d, h].T
        cp = pltpu.make_async_copy(stage_ref, sout_hbm.at[i], sem)
        cp.start()
        cp.wait()


def _bwd_block(i):
    j = (i - NB_CTX) % LAT_BLOCKS
    return jnp.where(i < NB_CTX, i, i - j + (LAT_BLOCKS - 1 - j))


def _hgrn(gates, rest, hgrn_lb, s0, mall, masks):
    nh = HGRN_W
    fwd = lambda col: pl.BlockSpec((TB, nh), lambda i: (i, col))
    bwd = lambda col: pl.BlockSpec((TB, nh), lambda i: (_bwd_block(i), col))
    lat_seq = lambda i: jnp.clip((i - NB_CTX) // LAT_BLOCKS, 0, N_LAT_SEQ - 1)
    return pl.pallas_call(
        _hgrn_kernel,
        out_shape=(jax.ShapeDtypeStruct((T_ALL, nh), F32),
                   jax.ShapeDtypeStruct((T_ALL, nh), F32),
                   jax.ShapeDtypeStruct((N_CTX_SEQ, 2, HEADS, HEAD_K, HEAD_V), F32)),
        grid=(NB,),
        in_specs=[fwd(0), fwd(0), fwd(1), bwd(0), bwd(1), bwd(1),
                  pl.BlockSpec((2, 2, nh), lambda i: (0, 0, 0)),
                  pl.BlockSpec((1, 2, HEADS, HEAD_K, HEAD_V), lambda i: (lat_seq(i), 0, 0, 0, 0)),
                  pl.BlockSpec((2, MM_ROWS, 3 * CHUNK), lambda i: (0, 0, 0)),
                  pl.BlockSpec((2, N_LEVELS + 1, CHUNK, CHUNK), lambda i: (0, 0, 0, 0))],
        out_specs=(pl.BlockSpec((TB, nh), lambda i: (i, 0)),
                   pl.BlockSpec((TB, nh), lambda i: (_bwd_block(i), 0)),
                   pl.BlockSpec(memory_space=pl.ANY)),
        scratch_shapes=[pltpu.VMEM((2, HEADS, HEAD_V, HEAD_K), F32),
                        pltpu.VMEM((2, HEADS, HEAD_K, HEAD_V), F32),
                        pltpu.VMEM((2, 2, EXP_ROWS, HGRN_W), F32),
                        pltpu.VMEM((2, 4, CHUNK, HGRN_W), F32),
                        pltpu.VMEM((2, HEADS, CHUNK, CHUNK), F32),
                        pltpu.VMEM((2, CHUNK, HGRN_W), F32),
                        pltpu.SemaphoreType.DMA],
        compiler_params=_params(),
        name="hgrn",
    )(rest, gates, rest, rest, gates, rest, hgrn_lb, s0, mall, masks)


def _window_bounds(n, w):
    pos = np.arange(n)
    lo = np.clip(pos - w // 2, 0, n - 1)
    hi = np.clip(pos - w // 2 + w - 1, 0, n - 1)
    return lo, hi


def _pool_consts():
    seq, img, cnt_seq, cnt_col = [], [], [], []
    for w in POOL_WINDOWS:
        lo, hi = _window_bounds(CTX_LEN, w)
        u = np.arange(CTX_LEN)[None, :]
        seq.append((u >= lo[:, None]) & (u <= hi[:, None]))
        cnt_seq.append(hi - lo + 1)
        lo, hi = _window_bounds(GRID_W, w)
        u = np.arange(GRID_W)[None, :]
        band = (u >= lo[:, None]) & (u <= hi[:, None])
        img.append(np.kron(np.eye(TB // GRID_W, dtype=bool), band))
        cnt_col.append(np.tile(hi - lo + 1, TB // GRID_W))
    a = np.stack([np.stack(seq), np.stack(img)]).astype(np.float32)
    cnt = np.stack([np.stack(cnt_seq), np.stack(cnt_col)]).astype(np.float32)
    cnt = np.broadcast_to(cnt[..., None], cnt.shape + (POOL_G,))
    return jnp.asarray(a, BF16), jnp.asarray(cnt, F32)


POOL_ROWS = LAT_LEN


def _pool_kernel(u_ref, a_ref, cnt_ref, wp_ref, ps_ref, o_ref, cp_ref, d_ref):
    i = pl.program_id(0)
    nblk = POOL_ROWS // TB

    def centre(g, r0, nrows, pm):
        sl = slice(g * POOL_G, (g + 1) * POOL_G)
        d_ref[pl.ds(r0, nrows), :] = (pm - u_ref[pl.ds(r0, nrows), sl].astype(F32)).astype(BF16)

    def group_map(g):
        sl = slice(g * POOL_G, (g + 1) * POOL_G)
        o_ref[:, sl] = (_dot(d_ref[...], wp_ref[g]) * ps_ref[:, sl]).astype(o_ref.dtype)

    def window_sum(kind, g, b):
        sl = slice(g * POOL_G, (g + 1) * POOL_G)
        return _dot(a_ref[kind, g], u_ref[pl.ds(b * TB, TB), sl]) / cnt_ref[kind, g]

    @pl.when(i < T_CTX // POOL_ROWS)
    def _():
        for g in range(len(POOL_WINDOWS)):
            for b in range(nblk):
                centre(g, b * TB, TB, window_sum(0, g, b))
            group_map(g)

    @pl.when(i >= T_CTX // POOL_ROWS)
    def _():
        for g, w in enumerate(POOL_WINDOWS):
            for b in range(nblk):
                cp_ref[pl.ds(b * TB, TB), :] = window_sum(1, g, b)
            lo, hi = _window_bounds(GRID_H, w)
            for r in range(GRID_H):
                acc = cp_ref[pl.ds(int(lo[r]) * GRID_W, GRID_W), :]
                for rr in range(int(lo[r]) + 1, int(hi[r]) + 1):
                    acc = acc + cp_ref[pl.ds(rr * GRID_W, GRID_W), :]
                centre(g, r * GRID_W, GRID_W, acc / float(hi[r] - lo[r] + 1))
            group_map(g)


def _pool(rest, a_pool, cnt_pool, w_pool_bf, pool_scale):
    col = 3
    return pl.pallas_call(
        _pool_kernel,
        out_shape=jax.ShapeDtypeStruct((T_ALL, POOL_W), BF16),
        grid=(T_ALL // POOL_ROWS,),
        in_specs=[pl.BlockSpec((POOL_ROWS, POOL_W), lambda i: (i, col)),
                  pl.BlockSpec((2, 4, TB, TB), lambda i: (0, 0, 0, 0)),
                  pl.BlockSpec((2, 4, TB, POOL_G), lambda i: (0, 0, 0, 0)),
                  pl.BlockSpec((4, POOL_G, POOL_G), lambda i: (0, 0, 0)),
                  pl.BlockSpec((1, POOL_W), lambda i: (0, 0))],
        out_specs=pl.BlockSpec((POOL_ROWS, POOL_W), lambda i: (i, 0)),
        scratch_shapes=[pltpu.VMEM((POOL_ROWS, POOL_G), F32),
                        pltpu.VMEM((POOL_ROWS, POOL_G), BF16)],
        compiler_params=_params(),
        name="pool",
    )(rest, a_pool, cnt_pool, w_pool_bf, pool_scale)


def _merge_kernel(xc_ref, xl_ref, of_ref, ob_ref, og_ref, yb_ref, ga_ref, gb_ref, mod_ref, hn_ref, wa_ref, wb_ref,
                  wo_ref, n2_ref, wrh_ref, br_ref,
                  x1_ref, h2_ref, te_ref, tw_ref, hist_ref):
    row = _mod_row(pl.program_id(0))
    gate1 = mod_ref[pl.ds(row, 1), pl.ds(2 * D_MODEL, D_MODEL)]
    shift2 = mod_ref[pl.ds(row, 1), pl.ds(3 * D_MODEL, D_MODEL)]
    scale2 = mod_ref[pl.ds(row, 1), pl.ds(4 * D_MODEL, D_MODEL)]

    halves = [slice(j * (TBD // 2), (j + 1) * (TBD // 2)) for j in range(2)]
    is_ctx = pl.program_id(0) < NBD_CTX

    def head_out(r):
        o = of_ref[r, :] + ob_ref[r, :]
        og = og_ref[r, :].astype(F32)
        ya = jnp.concatenate(
            [_rms(o[:, h * HEAD_V:(h + 1) * HEAD_V], hn_ref[...]) for h in range(HEADS)], axis=1)
        return (ya * (og * jax.nn.sigmoid(og))).astype(BF16)

    ya = [head_out(r) for r in halves]
    pa = [_dot(ya[j], wa_ref[...]) for j in range(2)]
    pb = [_dot(yb_ref[r, :], wb_ref[...]) for r in halves]
    merged = [(jax.nn.sigmoid(ga_ref[r, :].astype(F32)) * pa[j]
               + jax.nn.sigmoid(gb_ref[r, :].astype(F32)) * pb[j]).astype(BF16) for j, r in enumerate(halves)]
    po = [_dot(merged[j], wo_ref[...]) for j in range(2)]
    hh, hl = [], []
    for j, r in enumerate(halves):
        x1 = jnp.where(is_ctx, xc_ref[r, :], xl_ref[r, :]) + gate1 * po[j]
        x1_ref[r, :] = x1
        hi, lo = _split2(_rms(x1, n2_ref[...]) * (1.0 + scale2) + shift2)
        h2_ref[r, :] = hi
        hh.append(hi)
        hl.append(lo)
    hh = jnp.concatenate(hh, axis=0)
    hl = jnp.concatenate(hl, axis=0)

    both = _dot(hh, wrh_ref[...])
    tm = both[:, :LANES] + both[:, LANES:] + _dot(hl, wrh_ref[:, :LANES])
    lt = tm.T[:N_EXP] + br_ref[...]
    eidx = lax.broadcasted_iota(I32, (N_EXP, TBD), 0)
    vals, idxs, cnt = [], [], jnp.zeros((N_EXP, TBD), F32)
    for _ in range(TOP_K):
        m = jnp.max(lt, axis=0, keepdims=True)
        idx = jnp.min(jnp.where(lt == m, eidx, N_EXP), axis=0, keepdims=True)
        sel = eidx == idx
        vals.append(m)
        idxs.append(idx)
        cnt = cnt + sel.astype(F32)
        lt = jnp.where(sel, -jnp.inf, lt)
    ex = [jnp.exp(v - vals[0]) for v in vals]
    den = ex[0] + ex[1] + ex[2] + ex[3]
    tw_ref[0] = jnp.concatenate([e / den for e in ex], axis=0)
    te_ref[0] = jnp.concatenate(idxs, axis=0)
    hist_ref[0] = jnp.sum(cnt, axis=1, keepdims=True).astype(I32)


def _merge(x_ctx, x_lat, o_f, o_b, rest, yb, mod, hgrn_norm, wa_bf, wb_bf, wo_bf, norm2, wr_split, b_router):
    full = lambda shape: pl.BlockSpec(shape, lambda i: (0,) * len(shape))
    return pl.pallas_call(
        _merge_kernel,
        out_shape=(jax.ShapeDtypeStruct((T_ALL, D_MODEL), F32),
                   jax.ShapeDtypeStruct((T_ALL, D_MODEL), BF16),
                   jax.ShapeDtypeStruct((NBD, TOP_K, TBD), I32),
                   jax.ShapeDtypeStruct((NBD, TOP_K, TBD), F32),
                   jax.ShapeDtypeStruct((NBD, N_EXP, 1), I32)),
        grid=(NBD,),
        in_specs=_x_specs() + [
                  pl.BlockSpec((TBD, HGRN_W), lambda i: (i, 0)),
                  pl.BlockSpec((TBD, HGRN_W), lambda i: (i, 0)),
                  pl.BlockSpec((TBD, HGRN_W), lambda i: (i, 2)),
                  pl.BlockSpec((TBD, POOL_W), lambda i: (i, 0)),
                  pl.BlockSpec((TBD, D_MODEL), lambda i: (i, 2)),
                  pl.BlockSpec((TBD, D_MODEL), lambda i: (i, 3)),
                  full((MOD_ROWS, 6 * D_MODEL)),
                  full((1, HEAD_V)),
                  full((HGRN_W, D_MODEL)),
                  full((POOL_W, D_MODEL)),
                  full((D_MODEL, D_MODEL)),
                  full((1, D_MODEL)),
                  full((D_MODEL, 2 * LANES)),
                  full((N_EXP, 1))],
        out_specs=(pl.BlockSpec((TBD, D_MODEL), lambda i: (i, 0)),
                   pl.BlockSpec((TBD, D_MODEL), lambda i: (i, 0)),
                   pl.BlockSpec((1, TOP_K, TBD), lambda i: (i, 0, 0)),
                   pl.BlockSpec((1, TOP_K, TBD), lambda i: (i, 0, 0)),
                   pl.BlockSpec((1, N_EXP, 1), lambda i: (i, 0, 0))),
        compiler_params=_params(),
        name="merge",
    )(x_ctx, x_lat, o_f, o_b, rest, yb, rest, rest, mod, hgrn_norm, wa_bf, wb_bf, wo_bf, norm2,
      wr_split, b_router)


def _local_rows(te_ref, loff_ref, tri_ref):
    te = te_ref[0]
    eidx = lax.broadcasted_iota(I32, (N_EXP, TBD), 0)
    sels = [eidx == te[k:k + 1] for k in range(TOP_K)]
    cnt = sels[0].astype(F32)
    for s in sels[1:]:
        cnt = cnt + s.astype(F32)
    base = _dot(cnt.astype(BF16), tri_ref[...]) + loff_ref[0]
    return [jnp.sum(jnp.where(s, base, 0.0), axis=0, keepdims=True) for s in sels]


def _chunk_relative(rows, r0):
    out = []
    for r in rows:
        inside = jnp.logical_and(r >= r0, r < r0 + SORT_CHUNK)
        out.append(jnp.where(inside, r - r0, -1.0).astype(BF16))
    return out


def _segment_copies(make_copy, local_off, global_off, units):
    big_rows = SUBLANES << SEG_SMALL_BITS
    big = units >> SEG_SMALL_BITS

    def piece(p, carry):
        off = pl.multiple_of(p * big_rows, big_rows)
        make_copy(pl.multiple_of(local_off + off, SUBLANES), pl.multiple_of(global_off + off, SUBLANES),
                  big_rows).start()
        return carry

    lax.fori_loop(0, big, piece, 0)
    done = big * big_rows
    for j in reversed(range(SEG_SMALL_BITS)):
        rows = SUBLANES << j
        low = done + ((units >> (j + 1)) & ((1 << (SEG_SMALL_BITS - 1 - j)) - 1)) * (2 * rows)

        @pl.when(((units >> j) & 1) == 1)
        def _():
            make_copy(pl.multiple_of(local_off + low, SUBLANES), pl.multiple_of(global_off + low, SUBLANES),
                      rows).start()


def _pack_pairs(x):
    half = D_MODEL // 2
    lo = lax.bitcast_convert_type(x[:, :half], U32) >> 16
    hi = lax.bitcast_convert_type(x[:, half:], U32) & jnp.uint32(0xFFFF0000)
    return hi | lo


def _unpack_pairs(p):
    lo = lax.bitcast_convert_type(p << 16, F32).astype(BF16)
    hi = lax.bitcast_convert_type(p & jnp.uint32(0xFFFF0000), F32).astype(BF16)
    return jnp.concatenate([lo, hi], axis=1)


def _block_rows(loff_s, seg_s, b):
    last = b * N_EXP + N_EXP - 1
    return pl.multiple_of(loff_s[last] + seg_s[last] * SUBLANES, SUBLANES)


def _dispatch_kernel(loff_s, seg_s, gbase_s, tail_s, h2_ref, te_ref, loffv_ref, tri_ref, iota_ref, xs_hbm,
                     loc, zeros, sem, sem_z):
    b = pl.program_id(0)
    slot = b % 2

    def wait_block(blk, s):
        n = _block_rows(loff_s, seg_s, blk)
        pltpu.make_async_copy(loc.at[s, pl.ds(0, n)], xs_hbm.at[pl.ds(0, n)], sem.at[s]).wait()

    lrow = _local_rows(te_ref, loffv_ref, tri_ref)

    @pl.when(b >= 2)
    def _():
        wait_block(b - 2, slot)

    for r0 in range(0, LROWS, SORT_CHUNK):
        rel = _chunk_relative(lrow, r0)
        p = jnp.zeros((SORT_CHUNK, TBD), BF16)
        for k in reversed(range(TOP_K)):
            p = jnp.where(iota_ref[...] == rel[k], jnp.ones_like(p), p)
        loc[slot, r0:r0 + SORT_CHUNK, :] = _pack_pairs(_dot(p, h2_ref[...]))

    def out_copy(a, g, size):
        return pltpu.make_async_copy(loc.at[slot, pl.ds(a, size)], xs_hbm.at[pl.ds(g, size)], sem.at[slot])

    def body(e, carry):
        idx = b * N_EXP + e
        _segment_copies(out_copy, loff_s[idx], gbase_s[idx], seg_s[idx])
        return carry

    lax.fori_loop(0, N_EXP, body, 0)

    @pl.when(b == NBD - 1)
    def _():
        zeros[...] = jnp.zeros_like(zeros)

        def zero_copy(a, g, size):
            return pltpu.make_async_copy(zeros.at[pl.ds(a, size)], xs_hbm.at[pl.ds(g, size)], sem_z)

        def zbody(e, ztot):
            _segment_copies(zero_copy, 0, tail_s[e], tail_s[N_EXP + e])
            return ztot + tail_s[N_EXP + e] * SUBLANES

        def tbody(t, carry):
            pltpu.make_async_copy(zeros, xs_hbm.at[pl.ds(pl.multiple_of(t * TM, TM), TM)], sem_z).start()
            return carry

        n_used = tail_s[2 * N_EXP]
        lax.fori_loop(n_used, N_TILES, tbody, 0)
        ztot = lax.fori_loop(0, N_EXP, zbody, 0) + (N_TILES - n_used) * TM
        ztot = pl.multiple_of(ztot, SUBLANES)

        @pl.when(ztot > 0)
        def _():
            pltpu.make_async_copy(xs_hbm.at[pl.ds(0, ztot)], xs_hbm.at[pl.ds(0, ztot)], sem_z).wait()

        wait_block(b - 1, 1 - slot)
        wait_block(b, slot)


def _dispatch(tables, h2, top_e, tri, row_iota):
    loff_s, seg_s, gbase_s, tail_s, loff_v = tables
    grid_spec = pltpu.PrefetchScalarGridSpec(
        num_scalar_prefetch=4,
        grid=(NBD,),
        in_specs=[pl.BlockSpec((TBD, D_MODEL), lambda i, *_: (i, 0)),
                  pl.BlockSpec((1, TOP_K, TBD), lambda i, *_: (i, 0, 0)),
                  pl.BlockSpec((1, N_EXP, 1), lambda i, *_: (i, 0, 0)),
                  pl.BlockSpec((TBD, TBD), lambda i, *_: (0, 0)),
                  pl.BlockSpec((SORT_CHUNK, TBD), lambda i, *_: (0, 0))],
        out_specs=pl.BlockSpec(memory_space=pl.ANY),
        scratch_shapes=[pltpu.VMEM((2, LROWS, D_MODEL // 2), U32),
                        pltpu.VMEM((TM, D_MODEL // 2), U32),
                        pltpu.SemaphoreType.DMA((2,)),
                        pltpu.SemaphoreType.DMA])
    return pl.pallas_call(
        _dispatch_kernel,
        out_shape=jax.ShapeDtypeStruct((N_TILES * TM, D_MODEL // 2), U32),
        grid_spec=grid_spec,
        compiler_params=_params(),
        name="dispatch",
    )(loff_s, seg_s, gbase_s, tail_s, h2, top_e, loff_v, tri, row_iota)


def _moe_kernel(te_ref, first_ref, par_ref, next_ref, nv_ref, nu_ref, xs_ref, bgu_ref, bd_ref, wgu_hbm, wd_hbm, o_ref,
                wgu_st, wd_st, wgu_bf, wd_bf, sem):
    i = pl.program_id(0)

    def fetch(e, s):
        return (pltpu.make_async_copy(wgu_hbm.at[e], wgu_st.at[s], sem.at[0, s]),
                pltpu.make_async_copy(wd_hbm.at[e], wd_st.at[s], sem.at[1, s]))

    @pl.when(i < nu_ref[0])
    def _():
        @pl.when(first_ref[i] == 1)
        def _():
            s = par_ref[i]

            @pl.when(i == 0)
            def _():
                for cp in fetch(te_ref[0], 0):
                    cp.start()

            for cp in fetch(te_ref[i], s):
                cp.wait()

            @pl.when(next_ref[i] >= 0)
            def _():
                for cp in fetch(next_ref[i], 1 - s):
                    cp.start()

            wgu_bf[...] = wgu_st[s].astype(BF16)
            wd_bf[...] = wd_st[s].astype(BF16)

        def gate_up(r):
            return _dot(_unpack_pairs(xs_ref[r, :]), wgu_bf[...]) + bgu_ref[0]

        def activation(gu):
            gate = jnp.minimum(gu[:, :D_FF], SWIGLU_LIMIT)
            up = jnp.clip(gu[:, D_FF:], -SWIGLU_LIMIT, SWIGLU_LIMIT)
            return ((up + 1.0) * gate * jax.nn.sigmoid(SWIGLU_ALPHA * gate)).astype(BF16)

        def down(r, act):
            out = _dot(act, wd_bf[...]) + bd_ref[0]
            o_ref[r, :] = _pack_pairs(out.astype(BF16).astype(F32))

        def run(sizes):
            starts = [sum(sizes[:j]) for j in range(len(sizes))]
            pieces = [slice(a, a + n) for a, n in zip(starts, sizes)]
            gu = [gate_up(r) for r in pieces]
            for r, g in zip(pieces, gu):
                down(r, activation(g))
            done = sum(sizes)
            if done < TM:
                o_ref[done:, :] = jnp.zeros((TM - done, D_MODEL // 2), U32)

        nv = nv_ref[i]
        for lo, hi, sizes in MOE_PIECES:
            pl.when(jnp.logical_and(nv > lo, nv <= hi))(functools.partial(run, sizes))

    @pl.when(i >= nu_ref[0])
    def _():
        o_ref[...] = jnp.zeros_like(o_ref)


def _moe(tile_tables, xs, w_gate_up, b_gate_up, w_down, b_down):
    nsp = len(tile_tables)
    row_tile = lambda i, *s: (jnp.minimum(i, s[nsp - 1][0] - 1), 0)
    grid_spec = pltpu.PrefetchScalarGridSpec(
        num_scalar_prefetch=nsp,
        grid=(N_TILES,),
        in_specs=[pl.BlockSpec((TM, D_MODEL // 2), row_tile),
                  pl.BlockSpec((1, 1, 2 * D_FF), lambda i, te, *_: (te[i], 0, 0)),
                  pl.BlockSpec((1, 1, D_MODEL), lambda i, te, *_: (te[i], 0, 0)),
                  pl.BlockSpec(memory_space=pl.ANY),
                  pl.BlockSpec(memory_space=pl.ANY)],
        out_specs=pl.BlockSpec((TM, D_MODEL // 2), lambda i, *_: (i, 0)),
        scratch_shapes=[pltpu.VMEM((2, D_MODEL, 2 * D_FF), F32),
                        pltpu.VMEM((2, D_FF, D_MODEL), F32),
                        pltpu.VMEM((D_MODEL, 2 * D_FF), BF16),
                        pltpu.VMEM((D_FF, D_MODEL), BF16),
                        pltpu.SemaphoreType.DMA((2, 2))])
    return pl.pallas_call(
        _moe_kernel,
        out_shape=jax.ShapeDtypeStruct((N_TILES * TM, D_MODEL // 2), U32),
        grid_spec=grid_spec,
        compiler_params=_params(),
        name="moe",
    )(*tile_tables, xs, b_gate_up.reshape(N_EXP, 1, 2 * D_FF), b_down.reshape(N_EXP, 1, D_MODEL),
      w_gate_up, w_down)


def _final_kernel(loff_s, seg_s, gbase_s, ys_hbm, x1_ref, te_ref, tw_ref, loffv_ref, tri_ref, iota_ref, mod_ref, fn_ref,
                  oc_ref, ol_ref, loc, sem):
    b = pl.program_id(0)
    slot = b % 2

    def start_block(blk, s):
        def in_copy(a, g, size):
            return pltpu.make_async_copy(ys_hbm.at[pl.ds(g, size)], loc.at[s, pl.ds(a, size)], sem.at[s])

        def body(e, carry):
            idx = blk * N_EXP + e
            _segment_copies(in_copy, loff_s[idx], gbase_s[idx], seg_s[idx])
            return carry

        lax.fori_loop(0, N_EXP, body, 0)

    @pl.when(b == 0)
    def _():
        loc[...] = jnp.zeros_like(loc)
        start_block(0, 0)

    @pl.when(b + 1 < NBD)
    def _():
        start_block(b + 1, 1 - slot)

    lrow = _local_rows(te_ref, loffv_ref, tri_ref)
    tw = tw_ref[0]
    wts = [tw[k:k + 1].astype(BF16) for k in range(TOP_K)]

    n = _block_rows(loff_s, seg_s, b)
    pltpu.make_async_copy(ys_hbm.at[pl.ds(0, n)], loc.at[slot, pl.ds(0, n)], sem.at[slot]).wait()

    y = None
    for r0 in range(0, LROWS, SORT_CHUNK):
        rel = _chunk_relative(lrow, r0)
        pw = jnp.zeros((SORT_CHUNK, TBD), BF16)
        for k in reversed(range(TOP_K)):
            pw = jnp.where(iota_ref[...] == rel[k], wts[k], pw)
        part = _dot_tn(pw, _unpack_pairs(loc[slot, r0:r0 + SORT_CHUNK, :]))
        y = part if y is None else y + part

    row = _mod_row(b)
    gate2 = mod_ref[pl.ds(row, 1), pl.ds(5 * D_MODEL, D_MODEL)]
    out = _rms(x1_ref[...] + gate2 * y, fn_ref[...])

    @pl.when(b < NBD_CTX)
    def _():
        oc_ref[...] = out

    @pl.when(b >= NBD_CTX)
    def _():
        ol_ref[...] = out


def _final(tables, ys, x1, top_e, top_w, tri, col_iota, mod, final_norm):
    loff_s, seg_s, gbase_s, _, loff_v = tables
    grid_spec = pltpu.PrefetchScalarGridSpec(
        num_scalar_prefetch=3,
        grid=(NBD,),
        in_specs=[pl.BlockSpec(memory_space=pl.ANY),
                  pl.BlockSpec((TBD, D_MODEL), lambda i, *_: (i, 0)),
                  pl.BlockSpec((1, TOP_K, TBD), lambda i, *_: (i, 0, 0)),
                  pl.BlockSpec((1, TOP_K, TBD), lambda i, *_: (i, 0, 0)),
                  pl.BlockSpec((1, N_EXP, 1), lambda i, *_: (i, 0, 0)),
                  pl.BlockSpec((TBD, TBD), lambda i, *_: (0, 0)),
                  pl.BlockSpec((SORT_CHUNK, TBD), lambda i, *_: (0, 0)),
                  pl.BlockSpec((MOD_ROWS, 6 * D_MODEL), lambda i, *_: (0, 0)),
                  pl.BlockSpec((1, D_MODEL), lambda i, *_: (0, 0))],
        out_specs=(pl.BlockSpec((TBD, D_MODEL), lambda i, *_: (jnp.minimum(i, NBD_CTX - 1), 0)),
                   pl.BlockSpec((TBD, D_MODEL), lambda i, *_: (jnp.maximum(i - NBD_CTX, 0), 0))),
        scratch_shapes=[pltpu.VMEM((2, LROWS, D_MODEL // 2), U32),
                        pltpu.SemaphoreType.DMA((2,))])
    return pl.pallas_call(
        _final_kernel,
        out_shape=(jax.ShapeDtypeStruct((T_CTX, D_MODEL), F32),
                   jax.ShapeDtypeStruct((T_LAT, D_MODEL), F32)),
        grid_spec=grid_spec,
        compiler_params=_params(),
        name="final",
    )(loff_s, seg_s, gbase_s, ys, x1, top_e, top_w, loff_v, tri, col_iota, mod, final_norm)


def _dispatch_tables(hist):
    hist = hist.reshape(NBD, N_EXP)
    seg = ((hist + SUBLANES - 1) // SUBLANES) * SUBLANES
    loff = jnp.cumsum(seg, axis=1) - seg
    rows_e = jnp.sum(seg, axis=0)
    region = ((rows_e + TM - 1) // TM) * TM
    region_end = jnp.cumsum(region)
    region_start = region_end - region
    gbase = region_start[None, :] + jnp.cumsum(seg, axis=0) - seg
    n_used = (region_end[-1] // TM).astype(I32)
    tail = jnp.concatenate([region_start + rows_e, (region - rows_e) // SUBLANES, n_used.reshape(1)])
    start = jnp.arange(N_TILES, dtype=I32) * TM
    tile_e = jnp.sum((start[:, None] >= region_end[None, :]).astype(I32), axis=1)
    tile_e = jnp.minimum(tile_e, tile_e[jnp.maximum(n_used - 1, 0)])
    first = jnp.concatenate([jnp.ones((1,), I32), (tile_e[1:] != tile_e[:-1]).astype(I32)])
    parity = (jnp.cumsum(first) - 1) % 2
    later = jnp.where(tile_e[None, :] > tile_e[:, None], tile_e[None, :], N_EXP)
    nxt = jnp.min(later, axis=1)
    nxt = jnp.where(nxt == N_EXP, -1, nxt)
    flat = lambda a: a.reshape(-1).astype(I32)
    tables = (flat(loff), flat(seg // SUBLANES), flat(gbase), flat(tail), loff.astype(F32).reshape(NBD, N_EXP, 1))
    mine = tile_e[:, None] == jnp.arange(N_EXP, dtype=I32)[None, :]
    data_end = jnp.sum(jnp.where(mine, (region_start + rows_e)[None, :], 0), axis=1)
    tile_nv = jnp.clip(data_end - start, 0, TM)
    tile_tables = (flat(tile_e), flat(first), flat(parity), flat(nxt), flat(tile_nv), n_used.reshape(1))
    return tables, tile_tables


def kernel(x_prompt, x_sample, state_hgrn, c, c_ctx, w_ada, b_ada, norm1, w_in, hgrn_lb, hgrn_norm, w_pool,
           pool_scale, w_branch_a, w_branch_b, w_out, norm2, w_router, b_router, w_gate_up, b_gate_up,
           w_down, b_down, final_norm):
    x_ctx = x_prompt.reshape(T_CTX, D_MODEL)
    x_lat = x_sample.reshape(T_LAT, D_MODEL)
    cc = jnp.zeros((MOD_ROWS, D_MODEL), F32).at[:N_LAT_SEQ].set(c).at[N_LAT_SEQ].set(c_ctx)
    mod = _ada(cc, w_ada[0], b_ada)

    gates, rest = _inproj(x_ctx, x_lat, mod, norm1, w_in[0].astype(BF16))

    mall, masks = _hgrn_consts()
    o_f, o_b, new_state = _hgrn(gates, rest, hgrn_lb, state_hgrn[:, 0], mall, masks)

    a_pool, cnt_pool = _pool_consts()
    yb = _pool(rest, a_pool, cnt_pool, w_pool[0].astype(BF16), pool_scale)

    wr = jnp.pad(w_router[0], ((0, 0), (0, LANES - N_EXP)))
    wr_hi = wr.astype(BF16)
    wr_split = jnp.concatenate([wr_hi, (wr - wr_hi.astype(F32)).astype(BF16)], axis=1)
    tri =jnp.asarray(np.triu(np.ones((TBD, TBD), np.float32), 1), BF16)
    x1, h2, top_e, top_w, hist = _merge(
        x_ctx, x_lat, o_f, o_b, rest, yb, mod, hgrn_norm, w_branch_a[0].astype(BF16), w_branch_b[0].astype(BF16),
        w_out[0].astype(BF16), norm2, wr_split, b_router.reshape(N_EXP, 1))

    tables, tile_tables = _dispatch_tables(hist)
    row_iota = jnp.asarray(np.broadcast_to(np.arange(SORT_CHUNK, dtype=np.float32)[:, None], (SORT_CHUNK, TBD)), BF16)
    xs = _dispatch(tables, h2, top_e, tri, row_iota)
    ys = _moe(tile_tables, xs, w_gate_up[0], b_gate_up[0], w_down[0], b_down[0])
    y_ctx, y_lat = _final(tables, ys, x1, top_e, top_w, tri, row_iota, mod, final_norm.reshape(1, D_MODEL))
    y_prompt = y_ctx.reshape(N_CTX_SEQ, CTX_LEN, D_MODEL)
    y_sample = y_lat.reshape(N_LAT_SEQ, LAT_LEN, D_MODEL)
    return y_prompt, y_sample, new_state[:, None]
```

```python
import functools

import numpy as np
import jax
import jax.numpy as jnp
from jax import lax
from jax.experimental import pallas as pl
from jax.experimental.pallas import tpu as pltpu

F32 = jnp.float32
BF16 = jnp.bfloat16
I32 = jnp.int32
U32 = jnp.uint32

D_MODEL = 1024
N_CTX_SEQ, CTX_LEN = 32, 256
N_LAT_SEQ, LAT_LEN = 4, 2048
T_CTX = N_CTX_SEQ * CTX_LEN
T_LAT = N_LAT_SEQ * LAT_LEN
T_ALL = T_CTX + T_LAT
SUBLANES, LANES = 8, 128
TB = 256
NB = T_ALL // TB
NB_CTX = T_CTX // TB
LAT_BLOCKS = LAT_LEN // TB
HEADS, HEAD_K, HEAD_V = 4, 128, 128
HGRN_W = HEADS * HEAD_V
POOL_WINDOWS = (2, 4, 8, 16)
POOL_G = 128
POOL_W = len(POOL_WINDOWS) * POOL_G
GRID_W = 64
GRID_H = LAT_LEN // GRID_W
IN_W = 5 * HGRN_W + POOL_W + 2 * D_MODEL
GATE_W = 2 * HGRN_W
REST_W = IN_W - GATE_W
N_EXP, TOP_K, D_FF = 32, 4, 1024
SWIGLU_LIMIT = 7.0
SWIGLU_ALPHA = 1.702
EPS = 1e-6
LOG2_E = 1.4426950408889634
CHUNK = 64
N_LEVELS = 6
EXP_ROWS = (N_LEVELS + 2) * CHUNK
MM_BLOCKS = (0, 4, 5, 6)
MM_ROWS = len(MM_BLOCKS) * CHUNK
COARSE_LEVELS = ((1, 32), (2, 16), (3, 8))
TM = 512
MOE_PIECES = ((0, 128, (128,)), (128, 256, (256,)), (256, 384, (256, 128)), (384, TM, (256, 256)))
TBD = 512
NBD = T_ALL // TBD
NBD_CTX = T_CTX // TBD
SORT_CHUNK = 256
LROWS = TBD * TOP_K + N_EXP * SUBLANES
SEG_SMALL_BITS = 3
N_TILES = -(-(T_ALL * TOP_K + NBD * N_EXP * (SUBLANES - 1) + N_EXP * (TM - 1)) // TM)
MOD_ROWS = 8
VMEM_LIMIT = 56 * 1024 * 1024


def _params(sem=("arbitrary",)):
    return pltpu.CompilerParams(dimension_semantics=sem, vmem_limit_bytes=VMEM_LIMIT)


def _dot(a, b):
    return jnp.dot(a, b, preferred_element_type=F32)


def _dot_nt(a, b):
    return lax.dot_general(a, b, (((1,), (1,)), ((), ())), preferred_element_type=F32)


def _dot_tn(a, b):
    return lax.dot_general(a, b, (((0,), (0,)), ((), ())), preferred_element_type=F32)


def _split2(x):
    hi = x.astype(BF16)
    lo = (x - hi.astype(F32)).astype(BF16)
    return hi, lo


def _mod_row(i):
    return jnp.where(i < NBD_CTX, N_LAT_SEQ, (i - NBD_CTX) // (LAT_LEN // TBD))


def _ada_kernel(c_ref, w_ref, b_ref, o_ref):
    c = c_ref[...]
    s = c * jax.nn.sigmoid(c)
    o_ref[...] = jnp.dot(s, w_ref[...], preferred_element_type=F32,
                         precision=lax.Precision.HIGHEST) + b_ref[...]


def _ada(cc, w_ada, b_ada):
    nblk = 1536
    return pl.pallas_call(
        _ada_kernel,
        out_shape=jax.ShapeDtypeStruct((MOD_ROWS, 6 * D_MODEL), F32),
        grid=(6 * D_MODEL // nblk,),
        in_specs=[pl.BlockSpec((MOD_ROWS, D_MODEL), lambda j: (0, 0)),
                  pl.BlockSpec((D_MODEL, nblk), lambda j: (0, j)),
                  pl.BlockSpec((1, nblk), lambda j: (0, j))],
        out_specs=pl.BlockSpec((MOD_ROWS, nblk), lambda j: (0, j)),
        compiler_params=_params(),
        name="ada",
    )(cc, w_ada, b_ada)


def _rms(x, g):
    ms = jnp.mean(x * x, axis=-1, keepdims=True)
    return x * lax.rsqrt(ms + EPS) * g


def _x_specs():
    return [pl.BlockSpec((TBD, D_MODEL), lambda i, *_: (jnp.minimum(i, NBD_CTX - 1), 0)),
            pl.BlockSpec((TBD, D_MODEL), lambda i, *_: (jnp.maximum(i - NBD_CTX, 0), 0))]


def _x_block(xc_ref, xl_ref):
    return jnp.where(pl.program_id(0) < NBD_CTX, xc_ref[...], xl_ref[...])


def _inproj_kernel(xc_ref, xl_ref, mod_ref, n1_ref, w_ref, og_ref, or_ref):
    row = _mod_row(pl.program_id(0))
    shift = mod_ref[pl.ds(row, 1), pl.ds(0, D_MODEL)]
    scale = mod_ref[pl.ds(row, 1), pl.ds(D_MODEL, D_MODEL)]
    h = (_rms(_x_block(xc_ref, xl_ref), n1_ref[...]) * (1.0 + scale) + shift).astype(BF16)
    og_ref[...] = _dot(h, w_ref[:, HGRN_W:HGRN_W + GATE_W])
    or_ref[:, :HGRN_W] = _dot(h, w_ref[:, :HGRN_W]).astype(BF16)
    or_ref[:, HGRN_W:] = _dot(h, w_ref[:, HGRN_W + GATE_W:]).astype(BF16)


def _inproj(x_ctx, x_lat, mod, norm1, w_in_bf):
    return pl.pallas_call(
        _inproj_kernel,
        out_shape=(jax.ShapeDtypeStruct((T_ALL, GATE_W), F32),
                   jax.ShapeDtypeStruct((T_ALL, REST_W), BF16)),
        grid=(NBD,),
        in_specs=_x_specs() + [
                  pl.BlockSpec((MOD_ROWS, 6 * D_MODEL), lambda i: (0, 0)),
                  pl.BlockSpec((1, D_MODEL), lambda i: (0, 0)),
                  pl.BlockSpec((D_MODEL, IN_W), lambda i: (0, 0))],
        out_specs=(pl.BlockSpec((TBD, GATE_W), lambda i: (i, 0)),
                   pl.BlockSpec((TBD, REST_W), lambda i: (i, 0))),
        compiler_params=_params(),
        name="inproj",
    )(x_ctx, x_lat, mod, norm1, w_in_bf)


def _hgrn_consts():
    c = CHUNK
    t = np.arange(c)[:, None]
    u = np.arange(c)[None, :]
    blocks = [u <= t]
    masks = [np.eye(c, dtype=bool)]
    h = c // 2
    while h >= 1:
        bi = t // h
        upper = (bi % 2) == 1
        e_up = (u >= bi * h) & (u <= t)
        e_lo = (u > t) & (u <= bi * h + h - 1)
        blocks.append(np.where(upper, e_up, e_lo))
        masks.append(((t // (2 * h)) == (u // (2 * h))) & (((t // h) % 2) == 1) & (((u // h) % 2) == 0))
        h //= 2
    blocks.append(u > t)
    m_f = np.stack(blocks).astype(np.float32)
    k_f = np.stack(masks).astype(np.float32)
    m_b = m_f[:, ::-1, ::-1]
    k_b = k_f[:, ::-1, ::-1]
    sel = list(MM_BLOCKS)
    m = np.stack([m_f[sel].reshape(MM_ROWS, c), m_b[sel].reshape(MM_ROWS, c)])
    m3 = np.concatenate([m, m, m], axis=2)
    return jnp.asarray(m3, BF16), jnp.asarray(np.stack([k_f, k_b]), F32)


def _hgrn_block(dirs, lb, mall_ref, mask_ref, st_ref, z_ref, k_ref, sc_ref, run_ref):
    c = CHUNK
    nchunk = TB // c
    units = [(d, h) for d in range(2) for h in range(HEADS)]
    sl = [slice(h * HEAD_K, (h + 1) * HEAD_K) for h in range(HEADS)]

    def rows(ci, d):
        r0 = ci * c if d == 0 else (nchunk - 1 - ci) * c
        return slice(r0, r0 + c)

    def exponents(ci):
        s = ci % 2
        for d in range(2):
            f = lb[d:d + 1] + (1.0 - lb[d:d + 1]) * jax.nn.sigmoid(dirs[d][1][rows(ci, d), :])
            k_ref[s, d] = 1.0 - f
            k_ref[s, 2 + d] = dirs[d][0][rows(ci, d), :].astype(F32)
            g = jnp.log(f) * LOG2_E
            g1 = g.astype(BF16)
            r1 = g - g1.astype(F32)
            g2 = r1.astype(BF16)
            g3 = (r1 - g2.astype(F32)).astype(BF16)
            gsplit = jnp.concatenate([g1, g2, g3], axis=0)
            ex = _dot(mall_ref[d], gsplit)
            run = ex[0:c]
            run_ref[d] = run
            z_ref[s, d, 0:c] = jnp.exp2(run)
            for j, blk in enumerate(MM_BLOCKS[1:]):
                z_ref[s, d, blk * c:(blk + 1) * c] = jnp.exp2(ex[(j + 1) * c:(j + 2) * c])
            for blk, h in COARSE_LEVELS:
                for base in range(0, c, 2 * h):
                    ref = run_ref[d, base + h - 1 + d:base + h + d, :]
                    if d == 0:
                        first, second = ref - run[base:base + h], run[base + h:base + 2 * h] - ref
                    else:
                        first, second = run[base:base + h] - ref, ref - run[base + h:base + 2 * h]
                    z_ref[s, d, blk * c + base:blk * c + base + h] = jnp.exp2(first)
                    z_ref[s, d, blk * c + base + h:blk * c + base + 2 * h] = jnp.exp2(second)
            end = run_ref[d, c - 1:c, :] if d == 0 else run_ref[d, 0:1, :]
            z_ref[s, d, (N_LEVELS + 1) * c:] = jnp.exp2(end - run)

    def q_of(ci, d, h):
        return k_ref[ci % 2, 2 + d, :, sl[h]]

    def v_of(ci, d, h):
        return dirs[d][2][rows(ci, d), sl[h]].astype(BF16)

    def qz(ci, d, h, blk):
        return (q_of(ci, d, h) * z_ref[ci % 2, d, blk * c:(blk + 1) * c, sl[h]]).astype(BF16)

    def kz(ci, d, h, blk):
        return (k_ref[ci % 2, d, :, sl[h]] * z_ref[ci % 2, d, blk * c:(blk + 1) * c, sl[h]]).astype(BF16)

    def levels(ci):
        for d, h in units:
            q = q_of(ci, d, h).astype(F32)
            k = k_ref[ci % 2, d, :, sl[h]]
            k_next = pltpu.roll(k, 1 if d == 0 else c - 1, 0)
            zq = q * z_ref[ci % 2, d, N_LEVELS * c:(N_LEVELS + 1) * c, sl[h]]
            diag = jnp.sum(q * k, axis=1, keepdims=True)
            near = jnp.sum(zq * k_next, axis=1, keepdims=True)
            sc_ref[d, h] = mask_ref[d, 0] * diag + mask_ref[d, N_LEVELS] * near
        for lev in range(N_LEVELS - 1):
            for d, h in units:
                sc_ref[d, h] += mask_ref[d, lev + 1] * _dot_nt(qz(ci, d, h, lev + 1), kz(ci, d, h, lev + 1))

    def tail(ci):
        for d, h in units:
            o = (_dot_nt(qz(ci, d, h, 0), st_ref[d, h].astype(BF16))
                 + _dot(sc_ref[d, h].astype(BF16), v_of(ci, d, h)))
            dirs[d][3][rows(ci, d), sl[h]] = o * (HEAD_K ** -0.5)
        for d, h in units:
            tot_row = c - 1 if d == 0 else 0
            decay = z_ref[ci % 2, d, tot_row:tot_row + 1, sl[h]]
            st_ref[d, h] = st_ref[d, h] * decay + _dot_tn(v_of(ci, d, h), kz(ci, d, h, N_LEVELS + 1))

    exponents(0)
    for ci in range(nchunk):
        levels(ci)
        if ci + 1 < nchunk:
            exponents(ci + 1)
        tail(ci)


def _hgrn_kernel(qf_ref, ff_ref, vf_ref, qb_ref, fb_ref, vb_ref, lbraw_ref, s0_ref, mall_ref, mask_ref,
                 of_ref, ob_ref, sout_hbm, st_ref, stage_ref, z_ref, k_ref, sc_ref, run_ref, sem):
    i = pl.program_id(0)
    j = (i - NB_CTX) % LAT_BLOCKS
    is_ctx = i < NB_CTX

    @pl.when(is_ctx)
    def _():
        st_ref[...] = jnp.zeros_like(st_ref)

    @pl.when(jnp.logical_and(jnp.logical_not(is_ctx), j == 0))
    def _():
        for d in range(2):
            for h in range(HEADS):
                st_ref[d, h] = s0_ref[0, d, h].T

    a0 = lbraw_ref[0]
    a1 = lbraw_ref[1]
    mx = jnp.maximum(a0, a1)
    e0 = jnp.exp(a0 - mx)
    e1 = jnp.exp(a1 - mx)
    lb = e0 / (e0 + e1)

    dirs = ((qf_ref, ff_ref, vf_ref, of_ref), (qb_ref, fb_ref, vb_ref, ob_ref))
    _hgrn_block(dirs, lb, mall_ref, mask_ref, st_ref, z_ref, k_ref, sc_ref, run_ref)

    @pl.when(is_ctx)
    def _():
        for d in range(2):
            for h in range(HEADS):
                stage_ref[d, h] = st_ref[d, h].T
        cp = pltpu.make_async_copy(stage_ref, sout_hbm.at[i], sem)
        cp.start()
        cp.wait()


def _bwd_block(i):
    j = (i - NB_CTX) % LAT_BLOCKS
    return jnp.where(i < NB_CTX, i, i - j + (LAT_BLOCKS - 1 - j))


def _hgrn(gates, rest, hgrn_lb, s0, mall, masks):
    nh = HGRN_W
    fwd = lambda col: pl.BlockSpec((TB, nh), lambda i: (i, col))
    bwd = lambda col: pl.BlockSpec((TB, nh), lambda i: (_bwd_block(i), col))
    lat_seq = lambda i: jnp.clip((i - NB_CTX) // LAT_BLOCKS, 0, N_LAT_SEQ - 1)
    return pl.pallas_call(
        _hgrn_kernel,
        out_shape=(jax.ShapeDtypeStruct((T_ALL, nh), F32),
                   jax.ShapeDtypeStruct((T_ALL, nh), F32),
                   jax.ShapeDtypeStruct((N_CTX_SEQ, 2, HEADS, HEAD_K, HEAD_V), F32)),
        grid=(NB,),
        in_specs=[fwd(0), fwd(0), fwd(1), bwd(0), bwd(1), bwd(1),
                  pl.BlockSpec((2, 2, nh), lambda i: (0, 0, 0)),
                  pl.BlockSpec((1, 2, HEADS, HEAD_K, HEAD_V), lambda i: (lat_seq(i), 0, 0, 0, 0)),
                  pl.BlockSpec((2, MM_ROWS, 3 * CHUNK), lambda i: (0, 0, 0)),
                  pl.BlockSpec((2, N_LEVELS + 1, CHUNK, CHUNK), lambda i: (0, 0, 0, 0))],
        out_specs=(pl.BlockSpec((TB, nh), lambda i: (i, 0)),
                   pl.BlockSpec((TB, nh), lambda i: (_bwd_block(i), 0)),
                   pl.BlockSpec(memory_space=pl.ANY)),
        scratch_shapes=[pltpu.VMEM((2, HEADS, HEAD_V, HEAD_K), F32),
                        pltpu.VMEM((2, HEADS, HEAD_K, HEAD_V), F32),
                        pltpu.VMEM((2, 2, EXP_ROWS, HGRN_W), F32),
                        pltpu.VMEM((2, 4, CHUNK, HGRN_W), F32),
                        pltpu.VMEM((2, HEADS, CHUNK, CHUNK), F32),
                        pltpu.VMEM((2, CHUNK, HGRN_W), F32),
                        pltpu.SemaphoreType.DMA],
        compiler_params=_params(),
        name="hgrn",
    )(rest, gates, rest, rest, gates, rest, hgrn_lb, s0, mall, masks)


def _window_bounds(n, w):
    pos = np.arange(n)
    lo = np.clip(pos - w // 2, 0, n - 1)
    hi = np.clip(pos - w // 2 + w - 1, 0, n - 1)
    return lo, hi


def _pool_consts():
    seq, img, cnt_seq, cnt_col = [], [], [], []
    for w in POOL_WINDOWS:
        lo, hi = _window_bounds(CTX_LEN, w)
        u = np.arange(CTX_LEN)[None, :]
        seq.append((u >= lo[:, None]) & (u <= hi[:, None]))
        cnt_seq.append(hi - lo + 1)
        lo, hi = _window_bounds(GRID_W, w)
        u = np.arange(GRID_W)[None, :]
        band = (u >= lo[:, None]) & (u <= hi[:, None])
        img.append(np.kron(np.eye(TB // GRID_W, dtype=bool), band))
        cnt_col.append(np.tile(hi - lo + 1, TB // GRID_W))
    a = np.stack([np.stack(seq), np.stack(img)]).astype(np.float32)
    cnt = np.stack([np.stack(cnt_seq), np.stack(cnt_col)]).astype(np.float32)
    cnt = np.broadcast_to(cnt[..., None], cnt.shape + (POOL_G,))
    return jnp.asarray(a, BF16), jnp.asarray(cnt, F32)


POOL_ROWS = LAT_LEN


def _pool_kernel(u_ref, a_ref, cnt_ref, wp_ref, ps_ref, o_ref, cp_ref, d_ref):
    i = pl.program_id(0)
    nblk = POOL_ROWS // TB

    def centre(g, r0, nrows, pm):
        sl = slice(g * POOL_G, (g + 1) * POOL_G)
        d_ref[pl.ds(r0, nrows), :] = (pm - u_ref[pl.ds(r0, nrows), sl].astype(F32)).astype(BF16)

    def group_map(g):
        sl = slice(g * POOL_G, (g + 1) * POOL_G)
        o_ref[:, sl] = (_dot(d_ref[...], wp_ref[g]) * ps_ref[:, sl]).astype(o_ref.dtype)

    def window_sum(kind, g, b):
        sl = slice(g * POOL_G, (g + 1) * POOL_G)
        return _dot(a_ref[kind, g], u_ref[pl.ds(b * TB, TB), sl]) / cnt_ref[kind, g]

    @pl.when(i < T_CTX // POOL_ROWS)
    def _():
        for g in range(len(POOL_WINDOWS)):
            for b in range(nblk):
                centre(g, b * TB, TB, window_sum(0, g, b))
            group_map(g)

    @pl.when(i >= T_CTX // POOL_ROWS)
    def _():
        for g, w in enumerate(POOL_WINDOWS):
            for b in range(nblk):
                cp_ref[pl.ds(b * TB, TB), :] = window_sum(1, g, b)
            lo, hi = _window_bounds(GRID_H, w)
            for r in range(GRID_H):
                acc = cp_ref[pl.ds(int(lo[r]) * GRID_W, GRID_W), :]
                for rr in range(int(lo[r]) + 1, int(hi[r]) + 1):
                    acc = acc + cp_ref[pl.ds(rr * GRID_W, GRID_W), :]
                centre(g, r * GRID_W, GRID_W, acc / float(hi[r] - lo[r] + 1))
            group_map(g)


def _pool(rest, a_pool, cnt_pool, w_pool_bf, pool_scale):
    col = 3
    return pl.pallas_call(
        _pool_kernel,
        out_shape=jax.ShapeDtypeStruct((T_ALL, POOL_W), BF16),
        grid=(T_ALL // POOL_ROWS,),
        in_specs=[pl.BlockSpec((POOL_ROWS, POOL_W), lambda i: (i, col)),
                  pl.BlockSpec((2, 4, TB, TB), lambda i: (0, 0, 0, 0)),
                  pl.BlockSpec((2, 4, TB, POOL_G), lambda i: (0, 0, 0, 0)),
                  pl.BlockSpec((4, POOL_G, POOL_G), lambda i: (0, 0, 0)),
                  pl.BlockSpec((1, POOL_W), lambda i: (0, 0))],
        out_specs=pl.BlockSpec((POOL_ROWS, POOL_W), lambda i: (i, 0)),
        scratch_shapes=[pltpu.VMEM((POOL_ROWS, POOL_G), F32),
                        pltpu.VMEM((POOL_ROWS, POOL_G), BF16)],
        compiler_params=_params(),
        name="pool",
    )(rest, a_pool, cnt_pool, w_pool_bf, pool_scale)


def _merge_kernel(xc_ref, xl_ref, of_ref, ob_ref, og_ref, yb_ref, ga_ref, gb_ref, mod_ref, hn_ref, wa_ref, wb_ref,
                  wo_ref, n2_ref, wrh_ref, br_ref,
                  x1_ref, h2_ref, te_ref, tw_ref, hist_ref):
    row = _mod_row(pl.program_id(0))
    gate1 = mod_ref[pl.ds(row, 1), pl.ds(2 * D_MODEL, D_MODEL)]
    shift2 = mod_ref[pl.ds(row, 1), pl.ds(3 * D_MODEL, D_MODEL)]
    scale2 = mod_ref[pl.ds(row, 1), pl.ds(4 * D_MODEL, D_MODEL)]

    halves = [slice(j * (TBD // 2), (j + 1) * (TBD // 2)) for j in range(2)]
    is_ctx = pl.program_id(0) < NBD_CTX

    def head_out(r):
        o = of_ref[r, :] + ob_ref[r, :]
        og = og_ref[r, :].astype(F32)
        ya = jnp.concatenate(
            [_rms(o[:, h * HEAD_V:(h + 1) * HEAD_V], hn_ref[...]) for h in range(HEADS)], axis=1)
        return (ya * (og * jax.nn.sigmoid(og))).astype(BF16)

    ya = [head_out(r) for r in halves]
    pa = [_dot(ya[j], wa_ref[...]) for j in range(2)]
    pb = [_dot(yb_ref[r, :], wb_ref[...]) for r in halves]
    merged = [(jax.nn.sigmoid(ga_ref[r, :].astype(F32)) * pa[j]
               + jax.nn.sigmoid(gb_ref[r, :].astype(F32)) * pb[j]).astype(BF16) for j, r in enumerate(halves)]
    po = [_dot(merged[j], wo_ref[...]) for j in range(2)]
    hh, hl = [], []
    for j, r in enumerate(halves):
        x1 = jnp.where(is_ctx, xc_ref[r, :], xl_ref[r, :]) + gate1 * po[j]
        x1_ref[r, :] = x1
        hi, lo = _split2(_rms(x1, n2_ref[...]) * (1.0 + scale2) + shift2)
        h2_ref[r, :] = hi
        hh.append(hi)
        hl.append(lo)
    hh = jnp.concatenate(hh, axis=0)
    hl = jnp.concatenate(hl, axis=0)

    both = _dot(hh, wrh_ref[...])
    tm = both[:, :LANES] + both[:, LANES:] + _dot(hl, wrh_ref[:, :LANES])
    lt = tm.T[:N_EXP] + br_ref[...]
    eidx = lax.broadcasted_iota(I32, (N_EXP, TBD), 0)
    vals, idxs, cnt = [], [], jnp.zeros((N_EXP, TBD), F32)
    for _ in range(TOP_K):
        m = jnp.max(lt, axis=0, keepdims=True)
        idx = jnp.min(jnp.where(lt == m, eidx, N_EXP), axis=0, keepdims=True)
        sel = eidx == idx
        vals.append(m)
        idxs.append(idx)
        cnt = cnt + sel.astype(F32)
        lt = jnp.where(sel, -jnp.inf, lt)
    ex = [jnp.exp(v - vals[0]) for v in vals]
    den = ex[0] + ex[1] + ex[2] + ex[3]
    tw_ref[0] = jnp.concatenate([e / den for e in ex], axis=0)
    te_ref[0] = jnp.concatenate(idxs, axis=0)
    hist_ref[0] = jnp.sum(cnt, axis=1, keepdims=True).astype(I32)


def _merge(x_ctx, x_lat, o_f, o_b, rest, yb, mod, hgrn_norm, wa_bf, wb_bf, wo_bf, norm2, wr_split, b_router):
    full = lambda shape: pl.BlockSpec(shape, lambda i: (0,) * len(shape))
    return pl.pallas_call(
        _merge_kernel,
        out_shape=(jax.ShapeDtypeStruct((T_ALL, D_MODEL), F32),
                   jax.ShapeDtypeStruct((T_ALL, D_MODEL), BF16),
                   jax.ShapeDtypeStruct((NBD, TOP_K, TBD), I32),
                   jax.ShapeDtypeStruct((NBD, TOP_K, TBD), F32),
                   jax.ShapeDtypeStruct((NBD, N_EXP, 1), I32)),
        grid=(NBD,),
        in_specs=_x_specs() + [
                  pl.BlockSpec((TBD, HGRN_W), lambda i: (i, 0)),
                  pl.BlockSpec((TBD, HGRN_W), lambda i: (i, 0)),
                  pl.BlockSpec((TBD, HGRN_W), lambda i: (i, 2)),
                  pl.BlockSpec((TBD, POOL_W), lambda i: (i, 0)),
                  pl.BlockSpec((TBD, D_MODEL), lambda i: (i, 2)),
                  pl.BlockSpec((TBD, D_MODEL), lambda i: (i, 3)),
                  full((MOD_ROWS, 6 * D_MODEL)),
                  full((1, HEAD_V)),
                  full((HGRN_W, D_MODEL)),
                  full((POOL_W, D_MODEL)),
                  full((D_MODEL, D_MODEL)),
                  full((1, D_MODEL)),
                  full((D_MODEL, 2 * LANES)),
                  full((N_EXP, 1))],
        out_specs=(pl.BlockSpec((TBD, D_MODEL), lambda i: (i, 0)),
                   pl.BlockSpec((TBD, D_MODEL), lambda i: (i, 0)),
                   pl.BlockSpec((1, TOP_K, TBD), lambda i: (i, 0, 0)),
                   pl.BlockSpec((1, TOP_K, TBD), lambda i: (i, 0, 0)),
                   pl.BlockSpec((1, N_EXP, 1), lambda i: (i, 0, 0))),
        compiler_params=_params(),
        name="merge",
    )(x_ctx, x_lat, o_f, o_b, rest, yb, rest, rest, mod, hgrn_norm, wa_bf, wb_bf, wo_bf, norm2,
      wr_split, b_router)


def _local_rows(te_ref, loff_ref, tri_ref):
    te = te_ref[0]
    eidx = lax.broadcasted_iota(I32, (N_EXP, TBD), 0)
    sels = [eidx == te[k:k + 1] for k in range(TOP_K)]
    cnt = sels[0].astype(F32)
    for s in sels[1:]:
        cnt = cnt + s.astype(F32)
    base = _dot(cnt.astype(BF16), tri_ref[...]) + loff_ref[0]
    return [jnp.sum(jnp.where(s, base, 0.0), axis=0, keepdims=True) for s in sels]


def _chunk_relative(rows, r0):
    out = []
    for r in rows:
        inside = jnp.logical_and(r >= r0, r < r0 + SORT_CHUNK)
        out.append(jnp.where(inside, r - r0, -1.0).astype(BF16))
    return out


def _segment_copies(make_copy, local_off, global_off, units):
    big_rows = SUBLANES << SEG_SMALL_BITS
    big = units >> SEG_SMALL_BITS

    def piece(p, carry):
        off = pl.multiple_of(p * big_rows, big_rows)
        make_copy(pl.multiple_of(local_off + off, SUBLANES), pl.multiple_of(global_off + off, SUBLANES),
                  big_rows).start()
        return carry

    @pl.when(big >= 1)
    def _():
        piece(0, 0)

    lax.fori_loop(1, big, piece, 0)
    done = big * big_rows
    for j in reversed(range(SEG_SMALL_BITS)):
        rows = SUBLANES << j
        low = done + ((units >> (j + 1)) & ((1 << (SEG_SMALL_BITS - 1 - j)) - 1)) * (2 * rows)

        @pl.when(((units >> j) & 1) == 1)
        def _():
            make_copy(pl.multiple_of(local_off + low, SUBLANES), pl.multiple_of(global_off + low, SUBLANES),
                      rows).start()


def _pack_pairs(x):
    half = D_MODEL // 2
    lo = lax.bitcast_convert_type(x[:, :half], U32) >> 16
    hi = lax.bitcast_convert_type(x[:, half:], U32) & jnp.uint32(0xFFFF0000)
    return hi | lo


def _unpack_pairs(p):
    lo = lax.bitcast_convert_type(p << 16, F32).astype(BF16)
    hi = lax.bitcast_convert_type(p & jnp.uint32(0xFFFF0000), F32).astype(BF16)
    return jnp.concatenate([lo, hi], axis=1)


def _block_rows(loff_s, seg_s, b):
    last = b * N_EXP + N_EXP - 1
    return pl.multiple_of(loff_s[last] + seg_s[last] * SUBLANES, SUBLANES)


def _dispatch_kernel(loff_s, seg_s, gbase_s, tail_s, h2_ref, te_ref, loffv_ref, tri_ref, iota_ref, xs_hbm,
                     loc, zeros, sem, sem_z):
    b = pl.program_id(0)
    slot = b % 2

    def wait_block(blk, s):
        n = _block_rows(loff_s, seg_s, blk)
        pltpu.make_async_copy(loc.at[s, pl.ds(0, n)], xs_hbm.at[pl.ds(0, n)], sem.at[s]).wait()

    lrow = _local_rows(te_ref, loffv_ref, tri_ref)

    @pl.when(b >= 2)
    def _():
        wait_block(b - 2, slot)

    for r0 in range(0, LROWS, SORT_CHUNK):
        rel = _chunk_relative(lrow, r0)
        p = jnp.zeros((SORT_CHUNK, TBD), BF16)
        for k in reversed(range(TOP_K)):
            p = jnp.where(iota_ref[...] == rel[k], jnp.ones_like(p), p)
        loc[slot, r0:r0 + SORT_CHUNK, :] = _pack_pairs(_dot(p, h2_ref[...]))

    def out_copy(a, g, size):
        return pltpu.make_async_copy(loc.at[slot, pl.ds(a, size)], xs_hbm.at[pl.ds(g, size)], sem.at[slot])

    def body(e, carry):
        idx = b * N_EXP + e
        _segment_copies(out_copy, loff_s[idx], gbase_s[idx], seg_s[idx])
        return carry

    lax.fori_loop(0, N_EXP, body, 0)

    @pl.when(b == NBD - 1)
    def _():
        zeros[...] = jnp.zeros_like(zeros)

        def zero_copy(a, g, size):
            return pltpu.make_async_copy(zeros.at[pl.ds(a, size)], xs_hbm.at[pl.ds(g, size)], sem_z)

        def zbody(e, ztot):
            _segment_copies(zero_copy, 0, tail_s[e], tail_s[N_EXP + e])
            return ztot + tail_s[N_EXP + e] * SUBLANES

        def tbody(t, carry):
            pltpu.make_async_copy(zeros, xs_hbm.at[pl.ds(pl.multiple_of(t * TM, TM), TM)], sem_z).start()
            return carry

        n_used = tail_s[2 * N_EXP]
        lax.fori_loop(n_used, N_TILES, tbody, 0)
        ztot = lax.fori_loop(0, N_EXP, zbody, 0) + (N_TILES - n_used) * TM
        ztot = pl.multiple_of(ztot, SUBLANES)

        @pl.when(ztot > 0)
        def _():
            pltpu.make_async_copy(xs_hbm.at[pl.ds(0, ztot)], xs_hbm.at[pl.ds(0, ztot)], sem_z).wait()

        wait_block(b - 1, 1 - slot)
        wait_block(b, slot)


def _dispatch(tables, h2, top_e, tri, row_iota):
    loff_s, seg_s, gbase_s, tail_s, loff_v = tables
    grid_spec = pltpu.PrefetchScalarGridSpec(
        num_scalar_prefetch=4,
        grid=(NBD,),
        in_specs=[pl.BlockSpec((TBD, D_MODEL), lambda i, *_: (i, 0)),
                  pl.BlockSpec((1, TOP_K, TBD), lambda i, *_: (i, 0, 0)),
                  pl.BlockSpec((1, N_EXP, 1), lambda i, *_: (i, 0, 0)),
                  pl.BlockSpec((TBD, TBD), lambda i, *_: (0, 0)),
                  pl.BlockSpec((SORT_CHUNK, TBD), lambda i, *_: (0, 0))],
        out_specs=pl.BlockSpec(memory_space=pl.ANY),
        scratch_shapes=[pltpu.VMEM((2, LROWS, D_MODEL // 2), U32),
                        pltpu.VMEM((TM, D_MODEL // 2), U32),
                        pltpu.SemaphoreType.DMA((2,)),
                        pltpu.SemaphoreType.DMA])
    return pl.pallas_call(
        _dispatch_kernel,
        out_shape=jax.ShapeDtypeStruct((N_TILES * TM, D_MODEL // 2), U32),
        grid_spec=grid_spec,
        compiler_params=_params(),
        name="dispatch",
    )(loff_s, seg_s, gbase_s, tail_s, h2, top_e, loff_v, tri, row_iota)


def _moe_kernel(te_ref, first_ref, par_ref, next_ref, nv_ref, nu_ref, xs_ref, bgu_ref, bd_ref, wgu_hbm, wd_hbm, o_ref,
                wgu_st, wd_st, wgu_bf, wd_bf, sem):
    i = pl.program_id(0)

    def fetch(e, s):
        return (pltpu.make_async_copy(wgu_hbm.at[e], wgu_st.at[s], sem.at[0, s]),
                pltpu.make_async_copy(wd_hbm.at[e], wd_st.at[s], sem.at[1, s]))

    @pl.when(i < nu_ref[0])
    def _():
        @pl.when(first_ref[i] == 1)
        def _():
            s = par_ref[i]

            @pl.when(i == 0)
            def _():
                for cp in fetch(te_ref[0], 0):
                    cp.start()

            for cp in fetch(te_ref[i], s):
                cp.wait()

            @pl.when(next_ref[i] >= 0)
            def _():
                for cp in fetch(next_ref[i], 1 - s):
                    cp.start()

            wgu_bf[...] = wgu_st[s].astype(BF16)
            wd_bf[...] = wd_st[s].astype(BF16)

        def gate_up(r):
            return _dot(_unpack_pairs(xs_ref[r, :]), wgu_bf[...]) + bgu_ref[0]

        def activation(gu):
            gate = jnp.minimum(gu[:, :D_FF], SWIGLU_LIMIT)
            up = jnp.clip(gu[:, D_FF:], -SWIGLU_LIMIT, SWIGLU_LIMIT)
            return ((up + 1.0) * gate * jax.nn.sigmoid(SWIGLU_ALPHA * gate)).astype(BF16)

        def down(r, act):
            out = _dot(act, wd_bf[...]) + bd_ref[0]
            o_ref[r, :] = _pack_pairs(out.astype(BF16).astype(F32))

        def run(sizes):
            starts = [sum(sizes[:j]) for j in range(len(sizes))]
            pieces = [slice(a, a + n) for a, n in zip(starts, sizes)]
            gu = [gate_up(r) for r in pieces]
            for r, g in zip(pieces, gu):
                down(r, activation(g))
            done = sum(sizes)
            if done < TM:
                o_ref[done:, :] = jnp.zeros((TM - done, D_MODEL // 2), U32)

        nv = nv_ref[i]
        for lo, hi, sizes in MOE_PIECES:
            pl.when(jnp.logical_and(nv > lo, nv <= hi))(functools.partial(run, sizes))

    @pl.when(i >= nu_ref[0])
    def _():
        o_ref[...] = jnp.zeros_like(o_ref)


def _moe(tile_tables, xs, w_gate_up, b_gate_up, w_down, b_down):
    nsp = len(tile_tables)
    row_tile = lambda i, *s: (jnp.minimum(i, s[nsp - 1][0] - 1), 0)
    grid_spec = pltpu.PrefetchScalarGridSpec(
        num_scalar_prefetch=nsp,
        grid=(N_TILES,),
        in_specs=[pl.BlockSpec((TM, D_MODEL // 2), row_tile),
                  pl.BlockSpec((1, 1, 2 * D_FF), lambda i, te, *_: (te[i], 0, 0)),
                  pl.BlockSpec((1, 1, D_MODEL), lambda i, te, *_: (te[i], 0, 0)),
                  pl.BlockSpec(memory_space=pl.ANY),
                  pl.BlockSpec(memory_space=pl.ANY)],
        out_specs=pl.BlockSpec((TM, D_MODEL // 2), lambda i, *_: (i, 0)),
        scratch_shapes=[pltpu.VMEM((2, D_MODEL, 2 * D_FF), F32),
                        pltpu.VMEM((2, D_FF, D_MODEL), F32),
                        pltpu.VMEM((D_MODEL, 2 * D_FF), BF16),
                        pltpu.VMEM((D_FF, D_MODEL), BF16),
                        pltpu.SemaphoreType.DMA((2, 2))])
    return pl.pallas_call(
        _moe_kernel,
        out_shape=jax.ShapeDtypeStruct((N_TILES * TM, D_MODEL // 2), U32),
        grid_spec=grid_spec,
        compiler_params=_params(),
        name="moe",
    )(*tile_tables, xs, b_gate_up.reshape(N_EXP, 1, 2 * D_FF), b_down.reshape(N_EXP, 1, D_MODEL),
      w_gate_up, w_down)


def _final_kernel(loff_s, seg_s, gbase_s, ys_hbm, x1_ref, te_ref, tw_ref, loffv_ref, tri_ref, iota_ref, mod_ref, fn_ref,
                  oc_ref, ol_ref, loc, sem):
    b = pl.program_id(0)
    slot = b % 2

    def start_block(blk, s):
        def in_copy(a, g, size):
            return pltpu.make_async_copy(ys_hbm.at[pl.ds(g, size)], loc.at[s, pl.ds(a, size)], sem.at[s])

        def body(e, carry):
            idx = blk * N_EXP + e
            _segment_copies(in_copy, loff_s[idx], gbase_s[idx], seg_s[idx])
            return carry

        lax.fori_loop(0, N_EXP, body, 0)

    @pl.when(b == 0)
    def _():
        loc[...] = jnp.zeros_like(loc)
        start_block(0, 0)

    @pl.when(b + 1 < NBD)
    def _():
        start_block(b + 1, 1 - slot)

    lrow = _local_rows(te_ref, loffv_ref, tri_ref)
    tw = tw_ref[0]
    wts = [tw[k:k + 1].astype(BF16) for k in range(TOP_K)]

    n = _block_rows(loff_s, seg_s, b)
    pltpu.make_async_copy(ys_hbm.at[pl.ds(0, n)], loc.at[slot, pl.ds(0, n)], sem.at[slot]).wait()

    y = None
    for r0 in range(0, LROWS, SORT_CHUNK):
        rel = _chunk_relative(lrow, r0)
        pw = jnp.zeros((SORT_CHUNK, TBD), BF16)
        for k in reversed(range(TOP_K)):
            pw = jnp.where(iota_ref[...] == rel[k], wts[k], pw)
        part = _dot_tn(pw, _unpack_pairs(loc[slot, r0:r0 + SORT_CHUNK, :]))
        y = part if y is None else y + part

    row = _mod_row(b)
    gate2 = mod_ref[pl.ds(row, 1), pl.ds(5 * D_MODEL, D_MODEL)]
    out = _rms(x1_ref[...] + gate2 * y, fn_ref[...])

    @pl.when(b < NBD_CTX)
    def _():
        oc_ref[...] = out

    @pl.when(b >= NBD_CTX)
    def _():
        ol_ref[...] = out


def _final(tables, ys, x1, top_e, top_w, tri, col_iota, mod, final_norm):
    loff_s, seg_s, gbase_s, _, loff_v = tables
    grid_spec = pltpu.PrefetchScalarGridSpec(
        num_scalar_prefetch=3,
        grid=(NBD,),
        in_specs=[pl.BlockSpec(memory_space=pl.ANY),
                  pl.BlockSpec((TBD, D_MODEL), lambda i, *_: (i, 0)),
                  pl.BlockSpec((1, TOP_K, TBD), lambda i, *_: (i, 0, 0)),
                  pl.BlockSpec((1, TOP_K, TBD), lambda i, *_: (i, 0, 0)),
                  pl.BlockSpec((1, N_EXP, 1), lambda i, *_: (i, 0, 0)),
                  pl.BlockSpec((TBD, TBD), lambda i, *_: (0, 0)),
                  pl.BlockSpec((SORT_CHUNK, TBD), lambda i, *_: (0, 0)),
                  pl.BlockSpec((MOD_ROWS, 6 * D_MODEL), lambda i, *_: (0, 0)),
                  pl.BlockSpec((1, D_MODEL), lambda i, *_: (0, 0))],
        out_specs=(pl.BlockSpec((TBD, D_MODEL), lambda i, *_: (jnp.minimum(i, NBD_CTX - 1), 0)),
                   pl.BlockSpec((TBD, D_MODEL), lambda i, *_: (jnp.maximum(i - NBD_CTX, 0), 0))),
        scratch_shapes=[pltpu.VMEM((2, LROWS, D_MODEL // 2), U32),
                        pltpu.SemaphoreType.DMA((2,))])
    return pl.pallas_call(
        _final_kernel,
        out_shape=(jax.ShapeDtypeStruct((T_CTX, D_MODEL), F32),
                   jax.ShapeDtypeStruct((T_LAT, D_MODEL), F32)),
        grid_spec=grid_spec,
        compiler_params=_params(),
        name="final",
    )(loff_s, seg_s, gbase_s, ys, x1, top_e, top_w, loff_v, tri, col_iota, mod, final_norm)


def _dispatch_tables(hist):
    hist = hist.reshape(NBD, N_EXP)
    seg = ((hist + SUBLANES - 1) // SUBLANES) * SUBLANES
    loff = jnp.cumsum(seg, axis=1) - seg
    rows_e = jnp.sum(seg, axis=0)
    region = ((rows_e + TM - 1) // TM) * TM
    region_end = jnp.cumsum(region)
    region_start = region_end - region
    gbase = region_start[None, :] + jnp.cumsum(seg, axis=0) - seg
    n_used = (region_end[-1] // TM).astype(I32)
    tail = jnp.concatenate([region_start + rows_e, (region - rows_e) // SUBLANES, n_used.reshape(1)])
    start = jnp.arange(N_TILES, dtype=I32) * TM
    tile_e = jnp.sum((start[:, None] >= region_end[None, :]).astype(I32), axis=1)
    tile_e = jnp.minimum(tile_e, tile_e[jnp.maximum(n_used - 1, 0)])
    first = jnp.concatenate([jnp.ones((1,), I32), (tile_e[1:] != tile_e[:-1]).astype(I32)])
    parity = (jnp.cumsum(first) - 1) % 2
    later = jnp.where(tile_e[None, :] > tile_e[:, None], tile_e[None, :], N_EXP)
    nxt = jnp.min(later, axis=1)
    nxt = jnp.where(nxt == N_EXP, -1, nxt)
    flat = lambda a: a.reshape(-1).astype(I32)
    tables = (flat(loff), flat(seg // SUBLANES), flat(gbase), flat(tail), loff.astype(F32).reshape(NBD, N_EXP, 1))
    mine = tile_e[:, None] == jnp.arange(N_EXP, dtype=I32)[None, :]
    data_end = jnp.sum(jnp.where(mine, (region_start + rows_e)[None, :], 0), axis=1)
    tile_nv = jnp.clip(data_end - start, 0, TM)
    tile_tables = (flat(tile_e), flat(first), flat(parity), flat(nxt), flat(tile_nv), n_used.reshape(1))
    return tables, tile_tables


def kernel(x_prompt, x_sample, state_hgrn, c, c_ctx, w_ada, b_ada, norm1, w_in, hgrn_lb, hgrn_norm, w_pool,
           pool_scale, w_branch_a, w_branch_b, w_out, norm2, w_router, b_router, w_gate_up, b_gate_up,
           w_down, b_down, final_norm):
    x_ctx = x_prompt.reshape(T_CTX, D_MODEL)
    x_lat = x_sample.reshape(T_LAT, D_MODEL)
    cc = jnp.zeros((MOD_ROWS, D_MODEL), F32).at[:N_LAT_SEQ].set(c).at[N_LAT_SEQ].set(c_ctx)
    mod = _ada(cc, w_ada[0], b_ada)

    gates, rest = _inproj(x_ctx, x_lat, mod, norm1, w_in[0].astype(BF16))

    mall, masks = _hgrn_consts()
    o_f, o_b, new_state = _hgrn(gates, rest, hgrn_lb, state_hgrn[:, 0], mall, masks)

    a_pool, cnt_pool = _pool_consts()
    yb = _pool(rest, a_pool, cnt_pool, w_pool[0].astype(BF16), pool_scale)

    wr = jnp.pad(w_router[0], ((0, 0), (0, LANES - N_EXP)))
    wr_hi = wr.astype(BF16)
    wr_split = jnp.concatenate([wr_hi, (wr - wr_hi.astype(F32)).astype(BF16)], axis=1)
    tri =jnp.asarray(np.triu(np.ones((TBD, TBD), np.float32), 1), BF16)
    x1, h2, top_e, top_w, hist = _merge(
        x_ctx, x_lat, o_f, o_b, rest, yb, mod, hgrn_norm, w_branch_a[0].astype(BF16), w_branch_b[0].astype(BF16),
        w_out[0].astype(BF16), norm2, wr_split, b_router.reshape(N_EXP, 1))

    tables, tile_tables = _dispatch_tables(hist)
    row_iota = jnp.asarray(np.broadcast_to(np.arange(SORT_CHUNK, dtype=np.float32)[:, None], (SORT_CHUNK, TBD)), BF16)
    xs = _dispatch(tables, h2, top_e, tri, row_iota)
    ys = _moe(tile_tables, xs, w_gate_up[0], b_gate_up[0], w_down[0], b_down[0])
    y_ctx, y_lat = _final(tables, ys, x1, top_e, top_w, tri, row_iota, mod, final_norm.reshape(1, D_MODEL))
    y_prompt = y_ctx.reshape(N_CTX_SEQ, CTX_LEN, D_MODEL)
    y_sample = y_lat.reshape(N_LAT_SEQ, LAT_LEN, D_MODEL)
    return y_prompt, y_sample, new_state[:, None]
```

```python
import functools

import numpy as np
import jax
import jax.numpy as jnp
from jax import lax
from jax.experimental import pallas as pl
from jax.experimental.pallas import tpu as pltpu

F32 = jnp.float32
BF16 = jnp.bfloat16
I32 = jnp.int32
U32 = jnp.uint32

D_MODEL = 1024
N_CTX_SEQ, CTX_LEN = 32, 256
N_LAT_SEQ, LAT_LEN = 4, 2048
T_CTX = N_CTX_SEQ * CTX_LEN
T_LAT = N_LAT_SEQ * LAT_LEN
T_ALL = T_CTX + T_LAT
SUBLANES, LANES = 8, 128
TB = 256
NB = T_ALL // TB
NB_CTX = T_CTX // TB
LAT_BLOCKS = LAT_LEN // TB
HEADS, HEAD_K, HEAD_V = 4, 128, 128
HGRN_W = HEADS * HEAD_V
POOL_WINDOWS = (2, 4, 8, 16)
POOL_G = 128
POOL_W = len(POOL_WINDOWS) * POOL_G
GRID_W = 64
GRID_H = LAT_LEN // GRID_W
IN_W = 5 * HGRN_W + POOL_W + 2 * D_MODEL
GATE_W = 2 * HGRN_W
REST_W = IN_W - GATE_W
N_EXP, TOP_K, D_FF = 32, 4, 1024
SWIGLU_LIMIT = 7.0
SWIGLU_ALPHA = 1.702
EPS = 1e-6
LOG2_E = 1.4426950408889634
CHUNK = 64
N_LEVELS = 6
EXP_ROWS = (N_LEVELS + 2) * CHUNK
MM_BLOCKS = (0, 4, 5, 6)
MM_ROWS = len(MM_BLOCKS) * CHUNK
COARSE_LEVELS = ((1, 32), (2, 16), (3, 8))
TM = 512
MOE_PIECES = ((0, 128, (128,)), (128, 256, (256,)), (256, 384, (256, 128)), (384, TM, (256, 256)))
TBD = 512
NBD = T_ALL // TBD
NBD_CTX = T_CTX // TBD
SORT_CHUNK = 256
LROWS = TBD * TOP_K + N_EXP * SUBLANES
SEG_SMALL_BITS = 3
N_TILES = -(-(T_ALL * TOP_K + NBD * N_EXP * (SUBLANES - 1) + N_EXP * (TM - 1)) // TM)
MOD_ROWS = 8
VMEM_LIMIT = 56 * 1024 * 1024


def _params(sem=("arbitrary",)):
    return pltpu.CompilerParams(dimension_semantics=sem, vmem_limit_bytes=VMEM_LIMIT)


def _dot(a, b):
    return jnp.dot(a, b, preferred_element_type=F32)


def _dot_nt(a, b):
    return lax.dot_general(a, b, (((1,), (1,)), ((), ())), preferred_element_type=F32)


def _dot_tn(a, b):
    return lax.dot_general(a, b, (((0,), (0,)), ((), ())), preferred_element_type=F32)


def _split2(x):
    hi = x.astype(BF16)
    lo = (x - hi.astype(F32)).astype(BF16)
    return hi, lo


def _mod_row(i):
    return jnp.where(i < NBD_CTX, N_LAT_SEQ, (i - NBD_CTX) // (LAT_LEN // TBD))


def _ada_kernel(c_ref, w_ref, b_ref, o_ref):
    c = c_ref[...]
    s = c * jax.nn.sigmoid(c)
    o_ref[...] = jnp.dot(s, w_ref[...], preferred_element_type=F32,
                         precision=lax.Precision.HIGHEST) + b_ref[...]


def _ada(cc, w_ada, b_ada):
    nblk = 1536
    return pl.pallas_call(
        _ada_kernel,
        out_shape=jax.ShapeDtypeStruct((MOD_ROWS, 6 * D_MODEL), F32),
        grid=(6 * D_MODEL // nblk,),
        in_specs=[pl.BlockSpec((MOD_ROWS, D_MODEL), lambda j: (0, 0)),
                  pl.BlockSpec((D_MODEL, nblk), lambda j: (0, j)),
                  pl.BlockSpec((1, nblk), lambda j: (0, j))],
        out_specs=pl.BlockSpec((MOD_ROWS, nblk), lambda j: (0, j)),
        compiler_params=_params(),
        name="ada",
    )(cc, w_ada, b_ada)


def _rms(x, g):
    ms = jnp.mean(x * x, axis=-1, keepdims=True)
    return x * lax.rsqrt(ms + EPS) * g


def _x_specs():
    return [pl.BlockSpec((TBD, D_MODEL), lambda i, *_: (jnp.minimum(i, NBD_CTX - 1), 0)),
            pl.BlockSpec((TBD, D_MODEL), lambda i, *_: (jnp.maximum(i - NBD_CTX, 0), 0))]


def _x_block(xc_ref, xl_ref):
    return jnp.where(pl.program_id(0) < NBD_CTX, xc_ref[...], xl_ref[...])


def _inproj_kernel(xc_ref, xl_ref, mod_ref, n1_ref, w_ref, og_ref, or_ref):
    row = _mod_row(pl.program_id(0))
    shift = mod_ref[pl.ds(row, 1), pl.ds(0, D_MODEL)]
    scale = mod_ref[pl.ds(row, 1), pl.ds(D_MODEL, D_MODEL)]
    h = (_rms(_x_block(xc_ref, xl_ref), n1_ref[...]) * (1.0 + scale) + shift).astype(BF16)
    og_ref[...] = _dot(h, w_ref[:, HGRN_W:HGRN_W + GATE_W])
    or_ref[:, :HGRN_W] = _dot(h, w_ref[:, :HGRN_W]).astype(BF16)
    or_ref[:, HGRN_W:] = _dot(h, w_ref[:, HGRN_W + GATE_W:]).astype(BF16)


def _inproj(x_ctx, x_lat, mod, norm1, w_in_bf):
    return pl.pallas_call(
        _inproj_kernel,
        out_shape=(jax.ShapeDtypeStruct((T_ALL, GATE_W), F32),
                   jax.ShapeDtypeStruct((T_ALL, REST_W), BF16)),
        grid=(NBD,),
        in_specs=_x_specs() + [
                  pl.BlockSpec((MOD_ROWS, 6 * D_MODEL), lambda i: (0, 0)),
                  pl.BlockSpec((1, D_MODEL), lambda i: (0, 0)),
                  pl.BlockSpec((D_MODEL, IN_W), lambda i: (0, 0))],
        out_specs=(pl.BlockSpec((TBD, GATE_W), lambda i: (i, 0)),
                   pl.BlockSpec((TBD, REST_W), lambda i: (i, 0))),
        compiler_params=_params(),
        name="inproj",
    )(x_ctx, x_lat, mod, norm1, w_in_bf)


def _hgrn_consts():
    c = CHUNK
    t = np.arange(c)[:, None]
    u = np.arange(c)[None, :]
    blocks = [u <= t]
    masks = [np.eye(c, dtype=bool)]
    h = c // 2
    while h >= 1:
        bi = t // h
        upper = (bi % 2) == 1
        e_up = (u >= bi * h) & (u <= t)
        e_lo = (u > t) & (u <= bi * h + h - 1)
        blocks.append(np.where(upper, e_up, e_lo))
        masks.append(((t // (2 * h)) == (u // (2 * h))) & (((t // h) % 2) == 1) & (((u // h) % 2) == 0))
        h //= 2
    blocks.append(u > t)
    m_f = np.stack(blocks).astype(np.float32)
    k_f = np.stack(masks).astype(np.float32)
    m_b = m_f[:, ::-1, ::-1]
    k_b = k_f[:, ::-1, ::-1]
    sel = list(MM_BLOCKS)
    m = np.stack([m_f[sel].reshape(MM_ROWS, c), m_b[sel].reshape(MM_ROWS, c)])
    m3 = np.concatenate([m, m, m], axis=2)
    return jnp.asarray(m3, BF16), jnp.asarray(np.stack([k_f, k_b]), F32)


def _hgrn_block(dirs, lb, mall_ref, mask_ref, st_ref, z_ref, k_ref, sc_ref, run_ref):
    c = CHUNK
    nchunk = TB // c
    units = [(d, h) for d in range(2) for h in range(HEADS)]
    sl = [slice(h * HEAD_K, (h + 1) * HEAD_K) for h in range(HEADS)]

    def rows(ci, d):
        r0 = ci * c if d == 0 else (nchunk - 1 - ci) * c
        return slice(r0, r0 + c)

    def exponents(ci):
        s = ci % 2
        for d in range(2):
            f = lb[d:d + 1] + (1.0 - lb[d:d + 1]) * jax.nn.sigmoid(dirs[d][1][rows(ci, d), :])
            k_ref[s, d] = 1.0 - f
            k_ref[s, 2 + d] = dirs[d][0][rows(ci, d), :].astype(F32)
            g = jnp.log(f) * LOG2_E
            g1 = g.astype(BF16)
            r1 = g - g1.astype(F32)
            g2 = r1.astype(BF16)
            g3 = (r1 - g2.astype(F32)).astype(BF16)
            gsplit = jnp.concatenate([g1, g2, g3], axis=0)
            ex = _dot(mall_ref[d], gsplit)
            run = ex[0:c]
            run_ref[d] = run
            z_ref[s, d, 0:c] = jnp.exp2(run)
            for j, blk in enumerate(MM_BLOCKS[1:]):
                z_ref[s, d, blk * c:(blk + 1) * c] = jnp.exp2(ex[(j + 1) * c:(j + 2) * c])
            for blk, h in COARSE_LEVELS:
                for base in range(0, c, 2 * h):
                    ref = run_ref[d, base + h - 1 + d:base + h + d, :]
                    if d == 0:
                        first, second = ref - run[base:base + h], run[base + h:base + 2 * h] - ref
                    else:
                        first, second = run[base:base + h] - ref, ref - run[base + h:base + 2 * h]
                    z_ref[s, d, blk * c + base:blk * c + base + h] = jnp.exp2(first)
                    z_ref[s, d, blk * c + base + h:blk * c + base + 2 * h] = jnp.exp2(second)
            end = run_ref[d, c - 1:c, :] if d == 0 else run_ref[d, 0:1, :]
            z_ref[s, d, (N_LEVELS + 1) * c:] = jnp.exp2(end - run)

    def q_of(ci, d, h):
        return k_ref[ci % 2, 2 + d, :, sl[h]]

    def v_of(ci, d, h):
        return dirs[d][2][rows(ci, d), sl[h]].astype(BF16)

    def qz(ci, d, h, blk):
        return (q_of(ci, d, h) * z_ref[ci % 2, d, blk * c:(blk + 1) * c, sl[h]]).astype(BF16)

    def kz(ci, d, h, blk):
        return (k_ref[ci % 2, d, :, sl[h]] * z_ref[ci % 2, d, blk * c:(blk + 1) * c, sl[h]]).astype(BF16)

    def levels(ci):
        for d, h in units:
            q = q_of(ci, d, h).astype(F32)
            k = k_ref[ci % 2, d, :, sl[h]]
            k_next = pltpu.roll(k, 1 if d == 0 else c - 1, 0)
            zq = q * z_ref[ci % 2, d, N_LEVELS * c:(N_LEVELS + 1) * c, sl[h]]
            diag = jnp.sum(q * k, axis=1, keepdims=True)
            near = jnp.sum(zq * k_next, axis=1, keepdims=True)
            sc_ref[d, h] = mask_ref[d, 0] * diag + mask_ref[d, N_LEVELS] * near
        for lev in range(N_LEVELS - 1):
            for d, h in units:
                sc_ref[d, h] += mask_ref[d, lev + 1] * _dot_nt(qz(ci, d, h, lev + 1), kz(ci, d, h, lev + 1))

    def tail(ci):
        for d, h in units:
            o = (_dot_nt(qz(ci, d, h, 0), st_ref[d, h].astype(BF16))
                 + _dot(sc_ref[d, h].astype(BF16), v_of(ci, d, h)))
            dirs[d][3][rows(ci, d), sl[h]] = o * (HEAD_K ** -0.5)
        for d, h in units:
            tot_row = c - 1 if d == 0 else 0
            decay = z_ref[ci % 2, d, tot_row:tot_row + 1, sl[h]]
            st_ref[d, h] = st_ref[d, h] * decay + _dot_tn(v_of(ci, d, h), kz(ci, d, h, N_LEVELS + 1))

    exponents(0)
    for ci in range(nchunk):
        levels(ci)
        if ci + 1 < nchunk:
            exponents(ci + 1)
        tail(ci)


def _hgrn_kernel(qf_ref, ff_ref, vf_ref, qb_ref, fb_ref, vb_ref, lbraw_ref, s0_ref, mall_ref, mask_ref,
                 of_ref, ob_ref, sout_hbm, st_ref, stage_ref, z_ref, k_ref, sc_ref, run_ref, sem):
    i = pl.program_id(0)
    j = (i - NB_CTX) % LAT_BLOCKS
    is_ctx = i < NB_CTX

    @pl.when(is_ctx)
    def _():
        st_ref[...] = jnp.zeros_like(st_ref)

    @pl.when(jnp.logical_and(jnp.logical_not(is_ctx), j == 0))
    def _():
        for d in range(2):
            for h in range(HEADS):
                st_ref[d, h] = s0_ref[0, d, h].T

    a0 = lbraw_ref[0]
    a1 = lbraw_ref[1]
    mx = jnp.maximum(a0, a1)
    e0 = jnp.exp(a0 - mx)
    e1 = jnp.exp(a1 - mx)
    lb = e0 / (e0 + e1)

    dirs = ((qf_ref, ff_ref, vf_ref, of_ref), (qb_ref, fb_ref, vb_ref, ob_ref))
    _hgrn_block(dirs, lb, mall_ref, mask_ref, st_ref, z_ref, k_ref, sc_ref, run_ref)

    @pl.when(is_ctx)
    def _():
        for d in range(2):
            for h in range(HEADS):
                stage_ref[d, h] = st_ref[d, h].T
        cp = pltpu.make_async_copy(stage_ref, sout_hbm.at[i], sem)
        cp.start()
        cp.wait()


def _bwd_block(i):
    j = (i - NB_CTX) % LAT_BLOCKS
    return jnp.where(i < NB_CTX, i, i - j + (LAT_BLOCKS - 1 - j))


def _hgrn(gates, rest, hgrn_lb, s0, mall, masks):
    nh = HGRN_W
    fwd = lambda col: pl.BlockSpec((TB, nh), lambda i: (i, col))
    bwd = lambda col: pl.BlockSpec((TB, nh), lambda i: (_bwd_block(i), col))
    lat_seq = lambda i: jnp.clip((i - NB_CTX) // LAT_BLOCKS, 0, N_LAT_SEQ - 1)
    return pl.pallas_call(
        _hgrn_kernel,
        out_shape=(jax.ShapeDtypeStruct((T_ALL, nh), F32),
                   jax.ShapeDtypeStruct((T_ALL, nh), F32),
                   jax.ShapeDtypeStruct((N_CTX_SEQ, 2, HEADS, HEAD_K, HEAD_V), F32)),
        grid=(NB,),
        in_specs=[fwd(0), fwd(0), fwd(1), bwd(0), bwd(1), bwd(1),
                  pl.BlockSpec((2, 2, nh), lambda i: (0, 0, 0)),
                  pl.BlockSpec((1, 2, HEADS, HEAD_K, HEAD_V), lambda i: (lat_seq(i), 0, 0, 0, 0)),
                  pl.BlockSpec((2, MM_ROWS, 3 * CHUNK), lambda i: (0, 0, 0)),
                  pl.BlockSpec((2, N_LEVELS + 1, CHUNK, CHUNK), lambda i: (0, 0, 0, 0))],
        out_specs=(pl.BlockSpec((TB, nh), lambda i: (i, 0)),
                   pl.BlockSpec((TB, nh), lambda i: (_bwd_block(i), 0)),
                   pl.BlockSpec(memory_space=pl.ANY)),
        scratch_shapes=[pltpu.VMEM((2, HEADS, HEAD_V, HEAD_K), F32),
                        pltpu.VMEM((2, HEADS, HEAD_K, HEAD_V), F32),
                        pltpu.VMEM((2, 2, EXP_ROWS, HGRN_W), F32),
                        pltpu.VMEM((2, 4, CHUNK, HGRN_W), F32),
                        pltpu.VMEM((2, HEADS, CHUNK, CHUNK), F32),
                        pltpu.VMEM((2, CHUNK, HGRN_W), F32),
                        pltpu.SemaphoreType.DMA],
        compiler_params=_params(),
        name="hgrn",
    )(rest, gates, rest, rest, gates, rest, hgrn_lb, s0, mall, masks)


def _window_bounds(n, w):
    pos = np.arange(n)
    lo = np.clip(pos - w // 2, 0, n - 1)
    hi = np.clip(pos - w // 2 + w - 1, 0, n - 1)
    return lo, hi


def _pool_consts():
    seq, img, cnt_seq, cnt_col = [], [], [], []
    for w in POOL_WINDOWS:
        lo, hi = _window_bounds(CTX_LEN, w)
        u = np.arange(CTX_LEN)[None, :]
        seq.append((u >= lo[:, None]) & (u <= hi[:, None]))
        cnt_seq.append(hi - lo + 1)
        lo, hi = _window_bounds(GRID_W, w)
        u = np.arange(GRID_W)[None, :]
        band = (u >= lo[:, None]) & (u <= hi[:, None])
        img.append(np.kron(np.eye(TB // GRID_W, dtype=bool), band))
        cnt_col.append(np.tile(hi - lo + 1, TB // GRID_W))
    a = np.stack([np.stack(seq), np.stack(img)]).astype(np.float32)
    cnt = np.stack([np.stack(cnt_seq), np.stack(cnt_col)]).astype(np.float32)
    cnt = np.broadcast_to(cnt[..., None], cnt.shape + (POOL_G,))
    return jnp.asarray(a, BF16), jnp.asarray(cnt, F32)


POOL_ROWS = LAT_LEN


def _pool_kernel(u_ref, a_ref, cnt_ref, wp_ref, ps_ref, o_ref, cp_ref, d_ref):
    i = pl.program_id(0)
    nblk = POOL_ROWS // TB

    def centre(g, r0, nrows, pm):
        sl = slice(g * POOL_G, (g + 1) * POOL_G)
        d_ref[pl.ds(r0, nrows), :] = (pm - u_ref[pl.ds(r0, nrows), sl].astype(F32)).astype(BF16)

    def group_map(g):
        sl = slice(g * POOL_G, (g + 1) * POOL_G)
        o_ref[:, sl] = (_dot(d_ref[...], wp_ref[g]) * ps_ref[:, sl]).astype(o_ref.dtype)

    def window_sum(kind, g, b):
        sl = slice(g * POOL_G, (g + 1) * POOL_G)
        return _dot(a_ref[kind, g], u_ref[pl.ds(b * TB, TB), sl]) / cnt_ref[kind, g]

    @pl.when(i < T_CTX // POOL_ROWS)
    def _():
        for g in range(len(POOL_WINDOWS)):
            for b in range(nblk):
                centre(g, b * TB, TB, window_sum(0, g, b))
            group_map(g)

    @pl.when(i >= T_CTX // POOL_ROWS)
    def _():
        for g, w in enumerate(POOL_WINDOWS):
            for b in range(nblk):
                cp_ref[pl.ds(b * TB, TB), :] = window_sum(1, g, b)
            lo, hi = _window_bounds(GRID_H, w)
            for r in range(GRID_H):
                acc = cp_ref[pl.ds(int(lo[r]) * GRID_W, GRID_W), :]
                for rr in range(int(lo[r]) + 1, int(hi[r]) + 1):
                    acc = acc + cp_ref[pl.ds(rr * GRID_W, GRID_W), :]
                centre(g, r * GRID_W, GRID_W, acc / float(hi[r] - lo[r] + 1))
            group_map(g)


def _pool(rest, a_pool, cnt_pool, w_pool_bf, pool_scale):
    col = 3
    return pl.pallas_call(
        _pool_kernel,
        out_shape=jax.ShapeDtypeStruct((T_ALL, POOL_W), BF16),
        grid=(T_ALL // POOL_ROWS,),
        in_specs=[pl.BlockSpec((POOL_ROWS, POOL_W), lambda i: (i, col)),
                  pl.BlockSpec((2, 4, TB, TB), lambda i: (0, 0, 0, 0)),
                  pl.BlockSpec((2, 4, TB, POOL_G), lambda i: (0, 0, 0, 0)),
                  pl.BlockSpec((4, POOL_G, POOL_G), lambda i: (0, 0, 0)),
                  pl.BlockSpec((1, POOL_W), lambda i: (0, 0))],
        out_specs=pl.BlockSpec((POOL_ROWS, POOL_W), lambda i: (i, 0)),
        scratch_shapes=[pltpu.VMEM((POOL_ROWS, POOL_G), F32),
                        pltpu.VMEM((POOL_ROWS, POOL_G), BF16)],
        compiler_params=_params(),
        name="pool",
    )(rest, a_pool, cnt_pool, w_pool_bf, pool_scale)


def _merge_kernel(xc_ref, xl_ref, of_ref, ob_ref, og_ref, yb_ref, ga_ref, gb_ref, mod_ref, hn_ref, wa_ref, wb_ref,
                  wo_ref, n2_ref, wrh_ref, br_ref,
                  x1_ref, h2_ref, te_ref, tw_ref, hist_ref):
    row = _mod_row(pl.program_id(0))
    gate1 = mod_ref[pl.ds(row, 1), pl.ds(2 * D_MODEL, D_MODEL)]
    shift2 = mod_ref[pl.ds(row, 1), pl.ds(3 * D_MODEL, D_MODEL)]
    scale2 = mod_ref[pl.ds(row, 1), pl.ds(4 * D_MODEL, D_MODEL)]

    halves = [slice(j * (TBD // 2), (j + 1) * (TBD // 2)) for j in range(2)]
    is_ctx = pl.program_id(0) < NBD_CTX

    def head_out(r):
        o = of_ref[r, :] + ob_ref[r, :]
        og = og_ref[r, :].astype(F32)
        ya = jnp.concatenate(
            [_rms(o[:, h * HEAD_V:(h + 1) * HEAD_V], hn_ref[...]) for h in range(HEADS)], axis=1)
        return (ya * (og * jax.nn.sigmoid(og))).astype(BF16)

    ya = [head_out(r) for r in halves]
    pa = [_dot(ya[j], wa_ref[...]) for j in range(2)]
    pb = [_dot(yb_ref[r, :], wb_ref[...]) for r in halves]
    merged = [(jax.nn.sigmoid(ga_ref[r, :].astype(F32)) * pa[j]
               + jax.nn.sigmoid(gb_ref[r, :].astype(F32)) * pb[j]).astype(BF16) for j, r in enumerate(halves)]
    po = [_dot(merged[j], wo_ref[...]) for j in range(2)]
    hh, hl = [], []
    for j, r in enumerate(halves):
        x1 = jnp.where(is_ctx, xc_ref[r, :], xl_ref[r, :]) + gate1 * po[j]
        x1_ref[r, :] = x1
        hi, lo = _split2(_rms(x1, n2_ref[...]) * (1.0 + scale2) + shift2)
        h2_ref[r, :] = hi
        hh.append(hi)
        hl.append(lo)
    hh = jnp.concatenate(hh, axis=0)
    hl = jnp.concatenate(hl, axis=0)

    both = _dot(hh, wrh_ref[...])
    tm = both[:, :LANES] + both[:, LANES:] + _dot(hl, wrh_ref[:, :LANES])
    lt = tm.T[:N_EXP] + br_ref[...]
    eidx = lax.broadcasted_iota(I32, (N_EXP, TBD), 0)
    vals, idxs, cnt = [], [], jnp.zeros((N_EXP, TBD), F32)
    for _ in range(TOP_K):
        m = jnp.max(lt, axis=0, keepdims=True)
        idx = jnp.min(jnp.where(lt == m, eidx, N_EXP), axis=0, keepdims=True)
        sel = eidx == idx
        vals.append(m)
        idxs.append(idx)
        cnt = cnt + sel.astype(F32)
        lt = jnp.where(sel, -jnp.inf, lt)
    ex = [jnp.exp(v - vals[0]) for v in vals]
    den = ex[0] + ex[1] + ex[2] + ex[3]
    tw_ref[0] = jnp.concatenate([e / den for e in ex], axis=0)
    te_ref[0] = jnp.concatenate(idxs, axis=0)
    hist_ref[0] = jnp.sum(cnt, axis=1, keepdims=True).astype(I32)


def _merge(x_ctx, x_lat, o_f, o_b, rest, yb, mod, hgrn_norm, wa_bf, wb_bf, wo_bf, norm2, wr_split, b_router):
    full = lambda shape: pl.BlockSpec(shape, lambda i: (0,) * len(shape))
    return pl.pallas_call(
        _merge_kernel,
        out_shape=(jax.ShapeDtypeStruct((T_ALL, D_MODEL), F32),
                   jax.ShapeDtypeStruct((T_ALL, D_MODEL), BF16),
                   jax.ShapeDtypeStruct((NBD, TOP_K, TBD), I32),
                   jax.ShapeDtypeStruct((NBD, TOP_K, TBD), F32),
                   jax.ShapeDtypeStruct((NBD, N_EXP, 1), I32)),
        grid=(NBD,),
        in_specs=_x_specs() + [
                  pl.BlockSpec((TBD, HGRN_W), lambda i: (i, 0)),
                  pl.BlockSpec((TBD, HGRN_W), lambda i: (i, 0)),
                  pl.BlockSpec((TBD, HGRN_W), lambda i: (i, 2)),
                  pl.BlockSpec((TBD, POOL_W), lambda i: (i, 0)),
                  pl.BlockSpec((TBD, D_MODEL), lambda i: (i, 2)),
                  pl.BlockSpec((TBD, D_MODEL), lambda i: (i, 3)),
                  full((MOD_ROWS, 6 * D_MODEL)),
                  full((1, HEAD_V)),
                  full((HGRN_W, D_MODEL)),
                  full((POOL_W, D_MODEL)),
                  full((D_MODEL, D_MODEL)),
                  full((1, D_MODEL)),
                  full((D_MODEL, 2 * LANES)),
                  full((N_EXP, 1))],
        out_specs=(pl.BlockSpec((TBD, D_MODEL), lambda i: (i, 0)),
                   pl.BlockSpec((TBD, D_MODEL), lambda i: (i, 0)),
                   pl.BlockSpec((1, TOP_K, TBD), lambda i: (i, 0, 0)),
                   pl.BlockSpec((1, TOP_K, TBD), lambda i: (i, 0, 0)),
                   pl.BlockSpec((1, N_EXP, 1), lambda i: (i, 0, 0))),
        compiler_params=_params(),
        name="merge",
    )(x_ctx, x_lat, o_f, o_b, rest, yb, rest, rest, mod, hgrn_norm, wa_bf, wb_bf, wo_bf, norm2,
      wr_split, b_router)


def _local_rows(te_ref, loff_ref, tri_ref):
    te = te_ref[0]
    eidx = lax.broadcasted_iota(I32, (N_EXP, TBD), 0)
    sels = [eidx == te[k:k + 1] for k in range(TOP_K)]
    cnt = sels[0].astype(F32)
    for s in sels[1:]:
        cnt = cnt + s.astype(F32)
    base = _dot(cnt.astype(BF16), tri_ref[...]) + loff_ref[0]
    return [jnp.sum(jnp.where(s, base, 0.0), axis=0, keepdims=True) for s in sels]


def _chunk_relative(rows, r0):
    out = []
    for r in rows:
        inside = jnp.logical_and(r >= r0, r < r0 + SORT_CHUNK)
        out.append(jnp.where(inside, r - r0, -1.0).astype(BF16))
    return out


def _segment_copies(make_copy, local_off, global_off, units):
    big_rows = SUBLANES << SEG_SMALL_BITS
    big = units >> SEG_SMALL_BITS

    def piece(p, carry):
        off = pl.multiple_of(p * big_rows, big_rows)
        make_copy(pl.multiple_of(local_off + off, SUBLANES), pl.multiple_of(global_off + off, SUBLANES),
                  big_rows).start()
        return carry

    @pl.when(big >= 1)
    def _():
        piece(0, 0)

    lax.fori_loop(1, big, piece, 0)
    done = big * big_rows
    for j in reversed(range(SEG_SMALL_BITS)):
        rows = SUBLANES << j
        low = done + ((units >> (j + 1)) & ((1 << (SEG_SMALL_BITS - 1 - j)) - 1)) * (2 * rows)

        @pl.when(((units >> j) & 1) == 1)
        def _():
            make_copy(pl.multiple_of(local_off + low, SUBLANES), pl.multiple_of(global_off + low, SUBLANES),
                      rows).start()


def _block_segment_copies(make_copy, loff_s, gbase_s, seg_s, blk):
    def entry(e):
        idx = blk * N_EXP + jnp.minimum(e, N_EXP - 1)
        return loff_s[idx], gbase_s[idx], seg_s[idx]

    def body(e, cur):
        nxt = entry(e + 1)
        _segment_copies(make_copy, *cur)
        return nxt

    lax.fori_loop(0, N_EXP, body, entry(0))


def _pack_pairs(x):
    half = D_MODEL // 2
    lo = lax.bitcast_convert_type(x[:, :half], U32) >> 16
    hi = lax.bitcast_convert_type(x[:, half:], U32) & jnp.uint32(0xFFFF0000)
    return hi | lo


def _unpack_pairs(p):
    lo = lax.bitcast_convert_type(p << 16, F32).astype(BF16)
    hi = lax.bitcast_convert_type(p & jnp.uint32(0xFFFF0000), F32).astype(BF16)
    return jnp.concatenate([lo, hi], axis=1)


def _block_rows(loff_s, seg_s, b):
    last = b * N_EXP + N_EXP - 1
    return pl.multiple_of(loff_s[last] + seg_s[last] * SUBLANES, SUBLANES)


def _dispatch_kernel(loff_s, seg_s, gbase_s, tail_s, h2_ref, te_ref, loffv_ref, tri_ref, iota_ref, xs_hbm,
                     loc, zeros, sem, sem_z):
    b = pl.program_id(0)
    slot = b % 2

    def wait_block(blk, s):
        n = _block_rows(loff_s, seg_s, blk)
        pltpu.make_async_copy(loc.at[s, pl.ds(0, n)], xs_hbm.at[pl.ds(0, n)], sem.at[s]).wait()

    lrow = _local_rows(te_ref, loffv_ref, tri_ref)

    @pl.when(b >= 2)
    def _():
        wait_block(b - 2, slot)

    for r0 in range(0, LROWS, SORT_CHUNK):
        rel = _chunk_relative(lrow, r0)
        p = jnp.zeros((SORT_CHUNK, TBD), BF16)
        for k in reversed(range(TOP_K)):
            p = jnp.where(iota_ref[...] == rel[k], jnp.ones_like(p), p)
        loc[slot, r0:r0 + SORT_CHUNK, :] = _pack_pairs(_dot(p, h2_ref[...]))

    def out_copy(a, g, size):
        return pltpu.make_async_copy(loc.at[slot, pl.ds(a, size)], xs_hbm.at[pl.ds(g, size)], sem.at[slot])

    _block_segment_copies(out_copy, loff_s, gbase_s, seg_s, b)

    @pl.when(b == NBD - 1)
    def _():
        zeros[...] = jnp.zeros_like(zeros)

        def zero_copy(a, g, size):
            return pltpu.make_async_copy(zeros.at[pl.ds(a, size)], xs_hbm.at[pl.ds(g, size)], sem_z)

        def zbody(e, ztot):
            _segment_copies(zero_copy, 0, tail_s[e], tail_s[N_EXP + e])
            return ztot + tail_s[N_EXP + e] * SUBLANES

        def tbody(t, carry):
            pltpu.make_async_copy(zeros, xs_hbm.at[pl.ds(pl.multiple_of(t * TM, TM), TM)], sem_z).start()
            return carry

        n_used = tail_s[2 * N_EXP]
        lax.fori_loop(n_used, N_TILES, tbody, 0)
        ztot = lax.fori_loop(0, N_EXP, zbody, 0) + (N_TILES - n_used) * TM
        ztot = pl.multiple_of(ztot, SUBLANES)

        @pl.when(ztot > 0)
        def _():
            pltpu.make_async_copy(xs_hbm.at[pl.ds(0, ztot)], xs_hbm.at[pl.ds(0, ztot)], sem_z).wait()

        wait_block(b - 1, 1 - slot)
        wait_block(b, slot)


def _dispatch(tables, h2, top_e, tri, row_iota):
    loff_s, seg_s, gbase_s, tail_s, loff_v = tables
    grid_spec = pltpu.PrefetchScalarGridSpec(
        num_scalar_prefetch=4,
        grid=(NBD,),
        in_specs=[pl.BlockSpec((TBD, D_MODEL), lambda i, *_: (i, 0)),
                  pl.BlockSpec((1, TOP_K, TBD), lambda i, *_: (i, 0, 0)),
                  pl.BlockSpec((1, N_EXP, 1), lambda i, *_: (i, 0, 0)),
                  pl.BlockSpec((TBD, TBD), lambda i, *_: (0, 0)),
                  pl.BlockSpec((SORT_CHUNK, TBD), lambda i, *_: (0, 0))],
        out_specs=pl.BlockSpec(memory_space=pl.ANY),
        scratch_shapes=[pltpu.VMEM((2, LROWS, D_MODEL // 2), U32),
                        pltpu.VMEM((TM, D_MODEL // 2), U32),
                        pltpu.SemaphoreType.DMA((2,)),
                        pltpu.SemaphoreType.DMA])
    return pl.pallas_call(
        _dispatch_kernel,
        out_shape=jax.ShapeDtypeStruct((N_TILES * TM, D_MODEL // 2), U32),
        grid_spec=grid_spec,
        compiler_params=_params(),
        name="dispatch",
    )(loff_s, seg_s, gbase_s, tail_s, h2, top_e, loff_v, tri, row_iota)


def _moe_kernel(te_ref, first_ref, par_ref, next_ref, nv_ref, nu_ref, xs_ref, bgu_ref, bd_ref, wgu_hbm, wd_hbm, o_ref,
                wgu_st, wd_st, wgu_bf, wd_bf, sem):
    i = pl.program_id(0)

    def fetch(e, s):
        return (pltpu.make_async_copy(wgu_hbm.at[e], wgu_st.at[s], sem.at[0, s]),
                pltpu.make_async_copy(wd_hbm.at[e], wd_st.at[s], sem.at[1, s]))

    @pl.when(i < nu_ref[0])
    def _():
        @pl.when(first_ref[i] == 1)
        def _():
            s = par_ref[i]

            @pl.when(i == 0)
            def _():
                for cp in fetch(te_ref[0], 0):
                    cp.start()

            for cp in fetch(te_ref[i], s):
                cp.wait()

            @pl.when(next_ref[i] >= 0)
            def _():
                for cp in fetch(next_ref[i], 1 - s):
                    cp.start()

            wgu_bf[...] = wgu_st[s].astype(BF16)
            wd_bf[...] = wd_st[s].astype(BF16)

        def gate_up(r):
            return _dot(_unpack_pairs(xs_ref[r, :]), wgu_bf[...]) + bgu_ref[0]

        def activation(gu):
            gate = jnp.minimum(gu[:, :D_FF], SWIGLU_LIMIT)
            up = jnp.clip(gu[:, D_FF:], -SWIGLU_LIMIT, SWIGLU_LIMIT)
            return ((up + 1.0) * gate * jax.nn.sigmoid(SWIGLU_ALPHA * gate)).astype(BF16)

        def down(r, act):
            out = _dot(act, wd_bf[...]) + bd_ref[0]
            o_ref[r, :] = _pack_pairs(out.astype(BF16).astype(F32))

        def run(sizes):
            starts = [sum(sizes[:j]) for j in range(len(sizes))]
            pieces = [slice(a, a + n) for a, n in zip(starts, sizes)]
            gu = [gate_up(r) for r in pieces]
            for r, g in zip(pieces, gu):
                down(r, activation(g))
            done = sum(sizes)
            if done < TM:
                o_ref[done:, :] = jnp.zeros((TM - done, D_MODEL // 2), U32)

        nv = nv_ref[i]
        for lo, hi, sizes in MOE_PIECES:
            pl.when(jnp.logical_and(nv > lo, nv <= hi))(functools.partial(run, sizes))

    @pl.when(i >= nu_ref[0])
    def _():
        o_ref[...] = jnp.zeros_like(o_ref)


def _moe(tile_tables, xs, w_gate_up, b_gate_up, w_down, b_down):
    nsp = len(tile_tables)
    row_tile = lambda i, *s: (jnp.minimum(i, s[nsp - 1][0] - 1), 0)
    grid_spec = pltpu.PrefetchScalarGridSpec(
        num_scalar_prefetch=nsp,
        grid=(N_TILES,),
        in_specs=[pl.BlockSpec((TM, D_MODEL // 2), row_tile),
                  pl.BlockSpec((1, 1, 2 * D_FF), lambda i, te, *_: (te[i], 0, 0)),
                  pl.BlockSpec((1, 1, D_MODEL), lambda i, te, *_: (te[i], 0, 0)),
                  pl.BlockSpec(memory_space=pl.ANY),
                  pl.BlockSpec(memory_space=pl.ANY)],
        out_specs=pl.BlockSpec((TM, D_MODEL // 2), lambda i, *_: (i, 0)),
        scratch_shapes=[pltpu.VMEM((2, D_MODEL, 2 * D_FF), F32),
                        pltpu.VMEM((2, D_FF, D_MODEL), F32),
                        pltpu.VMEM((D_MODEL, 2 * D_FF), BF16),
                        pltpu.VMEM((D_FF, D_MODEL), BF16),
                        pltpu.SemaphoreType.DMA((2, 2))])
    return pl.pallas_call(
        _moe_kernel,
        out_shape=jax.ShapeDtypeStruct((N_TILES * TM, D_MODEL // 2), U32),
        grid_spec=grid_spec,
        compiler_params=_params(),
        name="moe",
    )(*tile_tables, xs, b_gate_up.reshape(N_EXP, 1, 2 * D_FF), b_down.reshape(N_EXP, 1, D_MODEL),
      w_gate_up, w_down)


def _final_kernel(loff_s, seg_s, gbase_s, ys_hbm, x1_ref, te_ref, tw_ref, loffv_ref, tri_ref, iota_ref, mod_ref, fn_ref,
                  oc_ref, ol_ref, loc, sem):
    b = pl.program_id(0)
    slot = b % 2

    def start_block(blk, s):
        def in_copy(a, g, size):
            return pltpu.make_async_copy(ys_hbm.at[pl.ds(g, size)], loc.at[s, pl.ds(a, size)], sem.at[s])

        _block_segment_copies(in_copy, loff_s, gbase_s, seg_s, blk)

    @pl.when(b == 0)
    def _():
        loc[...] = jnp.zeros_like(loc)
        start_block(0, 0)

    @pl.when(b + 1 < NBD)
    def _():
        start_block(b + 1, 1 - slot)

    lrow = _local_rows(te_ref, loffv_ref, tri_ref)
    tw = tw_ref[0]
    wts = [tw[k:k + 1].astype(BF16) for k in range(TOP_K)]

    n = _block_rows(loff_s, seg_s, b)
    pltpu.make_async_copy(ys_hbm.at[pl.ds(0, n)], loc.at[slot, pl.ds(0, n)], sem.at[slot]).wait()

    y = None
    for r0 in range(0, LROWS, SORT_CHUNK):
        rel = _chunk_relative(lrow, r0)
        pw = jnp.zeros((SORT_CHUNK, TBD), BF16)
        for k in reversed(range(TOP_K)):
            pw = jnp.where(iota_ref[...] == rel[k], wts[k], pw)
        part = _dot_tn(pw, _unpack_pairs(loc[slot, r0:r0 + SORT_CHUNK, :]))
        y = part if y is None else y + part

    row = _mod_row(b)
    gate2 = mod_ref[pl.ds(row, 1), pl.ds(5 * D_MODEL, D_MODEL)]
    out = _rms(x1_ref[...] + gate2 * y, fn_ref[...])

    @pl.when(b < NBD_CTX)
    def _():
        oc_ref[...] = out

    @pl.when(b >= NBD_CTX)
    def _():
        ol_ref[...] = out


def _final(tables, ys, x1, top_e, top_w, tri, col_iota, mod, final_norm):
    loff_s, seg_s, gbase_s, _, loff_v = tables
    grid_spec = pltpu.PrefetchScalarGridSpec(
        num_scalar_prefetch=3,
        grid=(NBD,),
        in_specs=[pl.BlockSpec(memory_space=pl.ANY),
                  pl.BlockSpec((TBD, D_MODEL), lambda i, *_: (i, 0)),
                  pl.BlockSpec((1, TOP_K, TBD), lambda i, *_: (i, 0, 0)),
                  pl.BlockSpec((1, TOP_K, TBD), lambda i, *_: (i, 0, 0)),
                  pl.BlockSpec((1, N_EXP, 1), lambda i, *_: (i, 0, 0)),
                  pl.BlockSpec((TBD, TBD), lambda i, *_: (0, 0)),
                  pl.BlockSpec((SORT_CHUNK, TBD), lambda i, *_: (0, 0)),
                  pl.BlockSpec((MOD_ROWS, 6 * D_MODEL), lambda i, *_: (0, 0)),
                  pl.BlockSpec((1, D_MODEL), lambda i, *_: (0, 0))],
        out_specs=(pl.BlockSpec((TBD, D_MODEL), lambda i, *_: (jnp.minimum(i, NBD_CTX - 1), 0)),
                   pl.BlockSpec((TBD, D_MODEL), lambda i, *_: (jnp.maximum(i - NBD_CTX, 0), 0))),
        scratch_shapes=[pltpu.VMEM((2, LROWS, D_MODEL // 2), U32),
                        pltpu.SemaphoreType.DMA((2,))])
    return pl.pallas_call(
        _final_kernel,
        out_shape=(jax.ShapeDtypeStruct((T_CTX, D_MODEL), F32),
                   jax.ShapeDtypeStruct((T_LAT, D_MODEL), F32)),
        grid_spec=grid_spec,
        compiler_params=_params(),
        name="final",
    )(loff_s, seg_s, gbase_s, ys, x1, top_e, top_w, loff_v, tri, col_iota, mod, final_norm)


def _dispatch_tables(hist):
    hist = hist.reshape(NBD, N_EXP)
    seg = ((hist + SUBLANES - 1) // SUBLANES) * SUBLANES
    loff = jnp.cumsum(seg, axis=1) - seg
    rows_e = jnp.sum(seg, axis=0)
    region = ((rows_e + TM - 1) // TM) * TM
    region_end = jnp.cumsum(region)
    region_start = region_end - region
    gbase = region_start[None, :] + jnp.cumsum(seg, axis=0) - seg
    n_used = (region_end[-1] // TM).astype(I32)
    tail = jnp.concatenate([region_start + rows_e, (region - rows_e) // SUBLANES, n_used.reshape(1)])
    start = jnp.arange(N_TILES, dtype=I32) * TM
    tile_e = jnp.sum((start[:, None] >= region_end[None, :]).astype(I32), axis=1)
    tile_e = jnp.minimum(tile_e, tile_e[jnp.maximum(n_used - 1, 0)])
    first = jnp.concatenate([jnp.ones((1,), I32), (tile_e[1:] != tile_e[:-1]).astype(I32)])
    parity = (jnp.cumsum(first) - 1) % 2
    later = jnp.where(tile_e[None, :] > tile_e[:, None], tile_e[None, :], N_EXP)
    nxt = jnp.min(later, axis=1)
    nxt = jnp.where(nxt == N_EXP, -1, nxt)
    flat = lambda a: a.reshape(-1).astype(I32)
    tables = (flat(loff), flat(seg // SUBLANES), flat(gbase), flat(tail), loff.astype(F32).reshape(NBD, N_EXP, 1))
    mine = tile_e[:, None] == jnp.arange(N_EXP, dtype=I32)[None, :]
    data_end = jnp.sum(jnp.where(mine, (region_start + rows_e)[None, :], 0), axis=1)
    tile_nv = jnp.clip(data_end - start, 0, TM)
    tile_tables = (flat(tile_e), flat(first), flat(parity), flat(nxt), flat(tile_nv), n_used.reshape(1))
    return tables, tile_tables


def kernel(x_prompt, x_sample, state_hgrn, c, c_ctx, w_ada, b_ada, norm1, w_in, hgrn_lb, hgrn_norm, w_pool,
           pool_scale, w_branch_a, w_branch_b, w_out, norm2, w_router, b_router, w_gate_up, b_gate_up,
           w_down, b_down, final_norm):
    x_ctx = x_prompt.reshape(T_CTX, D_MODEL)
    x_lat = x_sample.reshape(T_LAT, D_MODEL)
    cc = jnp.zeros((MOD_ROWS, D_MODEL), F32).at[:N_LAT_SEQ].set(c).at[N_LAT_SEQ].set(c_ctx)
    mod = _ada(cc, w_ada[0], b_ada)

    gates, rest = _inproj(x_ctx, x_lat, mod, norm1, w_in[0].astype(BF16))

    mall, masks = _hgrn_consts()
    o_f, o_b, new_state = _hgrn(gates, rest, hgrn_lb, state_hgrn[:, 0], mall, masks)

    a_pool, cnt_pool = _pool_consts()
    yb = _pool(rest, a_pool, cnt_pool, w_pool[0].astype(BF16), pool_scale)

    wr = jnp.pad(w_router[0], ((0, 0), (0, LANES - N_EXP)))
    wr_hi = wr.astype(BF16)
    wr_split = jnp.concatenate([wr_hi, (wr - wr_hi.astype(F32)).astype(BF16)], axis=1)
    tri =jnp.asarray(np.triu(np.ones((TBD, TBD), np.float32), 1), BF16)
    x1, h2, top_e, top_w, hist = _merge(
        x_ctx, x_lat, o_f, o_b, rest, yb, mod, hgrn_norm, w_branch_a[0].astype(BF16), w_branch_b[0].astype(BF16),
        w_out[0].astype(BF16), norm2, wr_split, b_router.reshape(N_EXP, 1))

    tables, tile_tables = _dispatch_tables(hist)
    row_iota = jnp.asarray(np.broadcast_to(np.arange(SORT_CHUNK, dtype=np.float32)[:, None], (SORT_CHUNK, TBD)), BF16)
    xs = _dispatch(tables, h2, top_e, tri, row_iota)
    ys = _moe(tile_tables, xs, w_gate_up[0], b_gate_up[0], w_down[0], b_down[0])
    y_ctx, y_lat = _final(tables, ys, x1, top_e, top_w, tri, row_iota, mod, final_norm.reshape(1, D_MODEL))
    y_prompt = y_ctx.reshape(N_CTX_SEQ, CTX_LEN, D_MODEL)
    y_sample = y_lat.reshape(N_LAT_SEQ, LAT_LEN, D_MODEL)
    return y_prompt, y_sample, new_state[:, None]
```
